```python
import math, functools
import jax, jax.numpy as jnp
from jax import lax
import numpy as np

D_MODEL = 1024
BATCH = 16
SEQ = 2048
DEPTH = 4

GRID_W = 64
CTX_LEN = 256
N_EVEN = (DEPTH + 1) // 2
N_ODD = DEPTH // 2
EPS = 1e-6

MLSTM_HEADS = 4
MLSTM_HEAD_DIM = 128
MLSTM_W = MLSTM_HEADS * MLSTM_HEAD_DIM
MLSTM_CHUNK = 64
CONV_W = 3
S5_W = D_MODEL - MLSTM_W
S5_GROUP = 16
S5_GROUPS = S5_W // S5_GROUP
S5_STATE = 64
EV_GATES = 4 * MLSTM_HEADS
P_EVEN = 4 * MLSTM_W + EV_GATES + S5_W
GLA_HEADS = 4
GLA_DK = D_MODEL // 2
GLA_DV = D_MODEL
GLA_RANK = 16
GLA_TAU = 16.0
GLA_CHUNK = 64
P_ODD = 2 * GLA_DK + 2 * GLA_DV + 2 * GLA_RANK
N_EXPERTS = 32
TOP_K = 4
D_EXPERT = D_MODEL
SWIGLU_LIMIT = 7.0
SWIGLU_ALPHA = 1.702
MOE_BLOCK = 128

kernel_name = 'hybrid_mlstm_s5_gla_moe_prefix_dit'


def rmsnorm(x, g):
    xf = x.astype(jnp.float32)
    y = xf * lax.rsqrt(jnp.mean(xf * xf, axis=-1, keepdims=True) + EPS)
    return (y * g.astype(jnp.float32)).astype(x.dtype)


def head_rmsnorm(h, g):
    Bsz, L, H, d = h.shape
    y = h * lax.rsqrt(jnp.mean(h * h, axis=-1, keepdims=True) + EPS)
    return y.reshape(Bsz, L, H * d) * g.astype(jnp.float32)


def modulate(h, shift, scale):
    return h * (1 + scale) + shift


def centred_dwconv(x, w, b):
    L = x.shape[1]
    half = CONV_W // 2
    xp = jnp.pad(x, ((0, 0), (half, half), (0, 0)))
    y = b
    for tap in range(CONV_W):
        y = y + w[tap] * xp[:, tap:tap + L]
    return y


def bidirectional_prefix_scan(run_f, run_b, init, ctx_f, ctx_b, lat_f, lat_b):
    rev = lambda xs: tuple(jnp.flip(a, axis=1) for a in xs)
    yc_f, sc_f = run_f(ctx_f, init)
    yl_f, _ = run_f(lat_f, sc_f)
    yc_b, sc_b = run_b(rev(ctx_b), init)
    yl_b, _ = run_b(rev(lat_b), sc_b)
    return yc_f + jnp.flip(yc_b, axis=1), yl_f + jnp.flip(yl_b, axis=1)


def mlstm_run(inputs, state):
    q, k, v, log_i, log_f = inputs
    Bsz, L, H, dh = q.shape
    T = MLSTM_CHUNK
    nc = L // T
    chunk = lambda a: jnp.moveaxis(a.reshape((Bsz, nc, T) + a.shape[2:]), 1, 0)
    causal = jnp.tril(jnp.ones((T, T), dtype=bool))

    def step(carry, xs):
        C, n, m = carry
        qc, kc, vc, ic, fc = xs
        ic = jnp.swapaxes(ic, 1, 2)
        b = jnp.cumsum(jnp.swapaxes(fc, 1, 2), axis=-1)
        logw = jnp.where(causal, b[..., :, None] - b[..., None, :] + ic[..., None, :], -jnp.inf)
        log_inter = b + m[..., None]
        m_t = jnp.maximum(log_inter, jnp.max(logw, axis=-1))
        w_intra = jnp.exp(logw - m_t[..., None])
        w_inter = jnp.exp(log_inter - m_t)
        scores = jnp.einsum('bthd,bshd->bhts', qc, kc) * w_intra
        num = (jnp.einsum('bhts,bshe->bhte', scores, vc)
               + w_inter[..., None] * jnp.einsum('bthd,bhde->bhte', qc, C))
        den = jnp.sum(scores, axis=-1) + w_inter * jnp.einsum('bthd,bhd->bht', qc, n)
        h = num / jnp.maximum(jnp.abs(den), jnp.exp(-m_t))[..., None]
        b_last = b[..., -1]
        log_g = b_last[..., None] - b + ic
        m_new = jnp.maximum(b_last + m, jnp.max(log_g, axis=-1))
        keep = jnp.exp(b_last + m - m_new)
        w_g = jnp.exp(log_g - m_new[..., None])
        C_new = keep[..., None, None] * C + jnp.einsum('bhs,bshd,bshe->bhde', w_g, kc, vc)
        n_new = keep[..., None] * n + jnp.einsum('bhs,bshd->bhd', w_g, kc)
        return (C_new, n_new, m_new), jnp.swapaxes(h, 1, 2)

    state, hs = lax.scan(step, state, tuple(chunk(a) for a in (q, k, v, log_i, log_f)))
    return jnp.moveaxis(hs, 0, 1).reshape(Bsz, L, H, dh), state


def _complex_affine_combine(e1, e2):
    a1r, a1i, b1r, b1i = e1
    a2r, a2i, b2r, b2i = e2
    return (a2r * a1r - a2i * a1i, a2r * a1i + a2i * a1r,
            a2r * b1r - a2i * b1i + b2r, a2r * b1i + a2i * b1r + b2i)


def s5_run(inputs, state, params):
    (u,) = inputs
    a_re, a_im, log_dt, b_re, b_im, c_re, c_im = (p.astype(jnp.float32) for p in params)
    s0_re, s0_im = state
    L = u.shape[1]
    dt = jnp.exp(log_dt)[:, None]
    lam_re = jnp.minimum(a_re, -1e-4)
    lam_im = a_im
    decay = jnp.exp(lam_re * dt)
    ab_re = decay * jnp.cos(lam_im * dt)
    ab_im = decay * jnp.sin(lam_im * dt)
    den = lam_re * lam_re + lam_im * lam_im
    zr = ((ab_re - 1) * lam_re + ab_im * lam_im) / den
    zi = (ab_im * lam_re - (ab_re - 1) * lam_im) / den
    bb_re = zr[..., None] * b_re - zi[..., None] * b_im
    bb_im = zr[..., None] * b_im + zi[..., None] * b_re
    bu_re = jnp.einsum('blgc,gpc->lbgp', u, bb_re)
    bu_im = jnp.einsum('blgc,gpc->lbgp', u, bb_im)
    a_seq_re = jnp.broadcast_to(ab_re, (L, 1) + ab_re.shape)
    a_seq_im = jnp.broadcast_to(ab_im, (L, 1) + ab_im.shape)
    pw_re, pw_im, s_re, s_im = lax.associative_scan(_complex_affine_combine, (a_seq_re, a_seq_im, bu_re, bu_im))
    s_re, s_im = (s_re + pw_re * s0_re - pw_im * s0_im, s_im + pw_re * s0_im + pw_im * s0_re)
    y = jnp.einsum('lbgp,gcp->blgc', s_re, c_re) - jnp.einsum('lbgp,gcp->blgc', s_im, c_im)
    return y, (s_re[-1], s_im[-1])


def gla_run(inputs, state):
    q, k, v, log_a = inputs
    Bsz, L, H, dk = q.shape
    dv = v.shape[-1]
    T = GLA_CHUNK
    nc = L // T
    chunk = lambda a: jnp.moveaxis(a.reshape((Bsz, nc, T) + a.shape[2:]), 1, 0)
    causal = jnp.tril(jnp.ones((T, T), dtype=bool))[None, :, :, None, None]

    def step(S, xs):
        qc, kc, vc, ac = xs
        b = jnp.cumsum(ac, axis=1)
        rel = jnp.where(causal, b[:, :, None] - b[:, None, :], -jnp.inf)
        att = jnp.einsum('bthd,bshd,btshd->bhts', qc, kc, jnp.exp(rel))
        o = (jnp.einsum('bhts,bshe->bthe', att, vc)
             + jnp.einsum('bthd,bhde->bthe', qc * jnp.exp(b), S))
        b_last = b[:, -1]
        S_new = (jnp.exp(b_last)[..., None] * S
                 + jnp.einsum('bshd,bshe->bhde', kc * jnp.exp(b_last[:, None] - b), vc))
        return S_new, o

    state, os_ = lax.scan(step, state, tuple(chunk(a) for a in (q, k, v, log_a)))
    return jnp.moveaxis(os_, 0, 1).reshape(Bsz, L, H, dv), state


def even_mixer(h_ctx, h_lat, w_in, b_in, conv_w, conv_b, mnorm_g,
               a_re_f, a_im_f, log_dt_f, a_re_b, a_im_b, log_dt_b,
               b_re, b_im, c_re, c_im, d_skip, glu_w, glu_b, w_out):
    H, dh = MLSTM_HEADS, MLSTM_HEAD_DIM
    f32 = jnp.float32

    def prep(h):
        Bsz, L, _ = h.shape
        z = h @ w_in + b_in
        qk, v, o, gates, u = jnp.split(z, [2 * MLSTM_W, 3 * MLSTM_W, 4 * MLSTM_W, 4 * MLSTM_W + EV_GATES], axis=-1)
        qk = jax.nn.silu(centred_dwconv(qk, conv_w, conv_b)).astype(f32)
        q = qk[..., :MLSTM_W].reshape(Bsz, L, H, dh) * dh ** -0.5
        k = qk[..., MLSTM_W:].reshape(Bsz, L, H, dh)
        v = v.astype(f32).reshape(Bsz, L, H, dh)
        gates = gates.astype(f32).reshape(Bsz, L, 4, H)
        fwd = (q, k, v, gates[:, :, 0], jax.nn.log_sigmoid(gates[:, :, 1]))
        bwd = (q, k, v, gates[:, :, 2], jax.nn.log_sigmoid(gates[:, :, 3]))
        return fwd, bwd, o, u.astype(f32).reshape(Bsz, L, S5_GROUPS, S5_GROUP)

    cf, cb, co, cu = prep(h_ctx)
    lf, lb, lo, lu = prep(h_lat)
    Bsz = h_lat.shape[0]
    m_init = (jnp.zeros((Bsz, H, dh, dh), f32), jnp.zeros((Bsz, H, dh), f32), jnp.zeros((Bsz, H), f32))
    mc, ml = bidirectional_prefix_scan(mlstm_run, mlstm_run, m_init, cf, cb, lf, lb)
    s5_f = functools.partial(s5_run, params=(a_re_f, a_im_f, log_dt_f, b_re, b_im, c_re, c_im))
    s5_b = functools.partial(s5_run, params=(a_re_b, a_im_b, log_dt_b, b_re, b_im, c_re, c_im))
    s_init = (jnp.zeros((Bsz, S5_GROUPS, S5_STATE), f32), jnp.zeros((Bsz, S5_GROUPS, S5_STATE), f32))
    sc, sl = bidirectional_prefix_scan(s5_f, s5_b, s_init, (cu,), (cu,), (lu,), (lu,))
    d_grp = d_skip.astype(f32).reshape(S5_GROUPS, S5_GROUP)

    def post(m, o, s, u, dtype):
        Bsz, L = m.shape[:2]
        m_out = head_rmsnorm(m, mnorm_g) * jax.nn.sigmoid(o.astype(f32))
        y = jax.nn.gelu((s + d_grp * u).reshape(Bsz, L, S5_W))
        s_out = y * jax.nn.sigmoid(y @ glu_w.astype(f32) + glu_b.astype(f32))
        return jnp.concatenate([m_out, s_out], axis=-1).astype(dtype) @ w_out

    return post(mc, co, sc, cu, h_ctx.dtype), post(ml, lo, sl, lu, h_lat.dtype)


def odd_mixer(h_ctx, h_lat, w_in, w2_f, b2_f, w2_b, b2_b, norm_g, w_out):
    H = GLA_HEADS
    dk = GLA_DK // H
    dv = GLA_DV // H
    f32 = jnp.float32

    def prep(h):
        Bsz, L, _ = h.shape
        z = h @ w_in
        q, k, v, g, r_f, r_b = jnp.split(
            z, [GLA_DK, 2 * GLA_DK, 2 * GLA_DK + GLA_DV, 2 * GLA_DK + 2 * GLA_DV,
                2 * GLA_DK + 2 * GLA_DV + GLA_RANK], axis=-1)
        heads = lambda a, d: a.astype(f32).reshape(Bsz, L, H, d)
        gate = lambda r, w, b: jax.nn.log_sigmoid((r @ w + b).astype(f32)).reshape(Bsz, L, H, dk) / GLA_TAU
        q = heads(q, dk) * dk ** -0.5
        k = heads(k, dk)
        v = heads(v, dv)
        return (q, k, v, gate(r_f, w2_f, b2_f)), (q, k, v, gate(r_b, w2_b, b2_b)), g

    cf, cb, cg = prep(h_ctx)
    lf, lb, lg = prep(h_lat)
    Bsz = h_lat.shape[0]
    init = jnp.zeros((Bsz, H, dk, dv), f32)
    oc, ol = bidirectional_prefix_scan(gla_run, gla_run, init, cf, cb, lf, lb)

    def post(o, g, dtype):
        y = head_rmsnorm(o, norm_g) * jax.nn.silu(g.astype(f32))
        return y.astype(dtype) @ w_out

    return post(oc, cg, h_ctx.dtype), post(ol, lg, h_lat.dtype)


def moe_ffn(t, router_w, router_b, w_gu, b_gu, w_down, b_down):
    N, D = t.shape
    logits = (t @ router_w + router_b).astype(jnp.float32)
    top_logit, top_e = lax.top_k(logits, TOP_K)
    gate = jax.nn.softmax(top_logit, axis=-1)
    n_assign = N * TOP_K
    flat_e = top_e.reshape(-1)
    order = jnp.argsort(flat_e)
    e_sorted = flat_e[order]
    counts = jnp.bincount(flat_e, length=N_EXPERTS)
    padded = (counts + MOE_BLOCK - 1) // MOE_BLOCK * MOE_BLOCK
    pad_end = jnp.cumsum(padded)
    pad_start = pad_end - padded
    start = jnp.cumsum(counts) - counts
    dest = pad_start[e_sorted] + jnp.arange(n_assign) - start[e_sorted]
    n_blocks = -(-(n_assign + N_EXPERTS * (MOE_BLOCK - 1)) // MOE_BLOCK)
    n_rows = n_blocks * MOE_BLOCK
    row_token = jnp.full((n_rows,), N, jnp.int32).at[dest].set((order // TOP_K).astype(jnp.int32))
    row_gate = jnp.zeros((n_rows,), jnp.float32).at[dest].set(gate.reshape(-1)[order])
    block_expert = jnp.minimum(
        jnp.searchsorted(pad_end, jnp.arange(n_blocks) * MOE_BLOCK, side='right'), N_EXPERTS - 1)
    t_pad = jnp.concatenate([t, jnp.zeros((1, D), t.dtype)], axis=0)
    xb = t_pad[row_token].reshape(n_blocks, MOE_BLOCK, D)

    def expert_block(args):
        xblk, e = args
        gu = xblk @ w_gu[e] + b_gu[e]
        g = jnp.minimum(gu[..., ::2], SWIGLU_LIMIT)
        u = jnp.clip(gu[..., 1::2], -SWIGLU_LIMIT, SWIGLU_LIMIT)
        hdn = (u + 1) * (g * jax.nn.sigmoid(SWIGLU_ALPHA * g))
        return hdn @ w_down[e] + b_down[e]

    yb = lax.map(expert_block, (xb, block_expert)).reshape(n_rows, D)
    out = jnp.zeros((N + 1, D), jnp.float32).at[row_token].add(yb.astype(jnp.float32) * row_gate[:, None])
    return out[:N].astype(t.dtype)


def setup_inputs(seed: int = 0) -> dict:
    key = jax.random.key(seed)
    ks = iter(jax.random.split(key, 64))

    def nrm(shape, scale):
        return jax.random.normal(next(ks), shape, jnp.float32) * scale

    D = D_MODEL
    H = MLSTM_HEADS
    G, P, CG = S5_GROUPS, S5_STATE, S5_GROUP
    forget_offset = jnp.linspace(3.0, 6.0, H)
    gate_offset = jnp.zeros((4, H), jnp.float32).at[1].set(forget_offset).at[3].set(forget_offset).reshape(-1)
    ev_bias_offset = jnp.concatenate([jnp.zeros((4 * MLSTM_W,), jnp.float32), gate_offset,
                                      jnp.zeros((S5_W,), jnp.float32)])
    n_idx = jnp.arange(S5_STATE, dtype=jnp.float32)
    log_dt = lambda: jnp.log(jax.random.uniform(next(ks), (N_EVEN, G), jnp.float32, 1e-3, 1e-1))
    inputs = {
        'x': nrm((BATCH, SEQ, D), 1.0),
        'c': nrm((BATCH, D), 1.0),
        'ctx': nrm((BATCH, CTX_LEN, D), 1.0),
        'c_ctx': nrm((D,), 1.0),
        'mod_w': nrm((DEPTH, D, 6 * D), 0.5 * D ** -0.5),
        'mod_b': nrm((DEPTH, 6 * D), 0.01),
        'norm_mix_g': 1.0 + nrm((DEPTH, D), 0.02),
        'norm_ffn_g': 1.0 + nrm((DEPTH, D), 0.02),
        'ev_w_in': nrm((N_EVEN, D, P_EVEN), D ** -0.5),
        'ev_b_in': nrm((N_EVEN, P_EVEN), 0.01) + ev_bias_offset,
        'ev_conv_w': nrm((N_EVEN, CONV_W, 2 * MLSTM_W), 0.5),
        'ev_conv_b': nrm((N_EVEN, 2 * MLSTM_W), 0.01),
        'ev_mlstm_norm_g': 1.0 + nrm((N_EVEN, MLSTM_W), 0.02),
        'ev_s5_a_re_f': -0.5 + nrm((N_EVEN, G, P), 0.01),
        'ev_s5_a_im_f': math.pi * n_idx + nrm((N_EVEN, G, P), 0.01),
        'ev_s5_log_dt_f': log_dt(),
        'ev_s5_a_re_b': -0.5 + nrm((N_EVEN, G, P), 0.01),
        'ev_s5_a_im_b': math.pi * n_idx + nrm((N_EVEN, G, P), 0.01),
        'ev_s5_log_dt_b': log_dt(),
        'ev_s5_b_re': nrm((N_EVEN, G, P, CG), (2 * CG) ** -0.5),
        'ev_s5_b_im': nrm((N_EVEN, G, P, CG), (2 * CG) ** -0.5),
        'ev_s5_c_re': nrm((N_EVEN, G, CG, P), (2 * P) ** -0.5),
        'ev_s5_c_im': nrm((N_EVEN, G, CG, P), (2 * P) ** -0.5),
        'ev_s5_d': nrm((N_EVEN, S5_W), 1.0),
        'ev_s5_glu_w': nrm((N_EVEN, S5_W, S5_W), S5_W ** -0.5),
        'ev_s5_glu_b': nrm((N_EVEN, S5_W), 0.01),
        'ev_w_out': nrm((N_EVEN, D, D), D ** -0.5),
        'od_w_in': nrm((N_ODD, D, P_ODD), D ** -0.5),
        'od_gate_w2_f': nrm((N_ODD, GLA_RANK, GLA_DK), GLA_RANK ** -0.5),
        'od_gate_b2_f': nrm((N_ODD, GLA_DK), 0.01),
        'od_gate_w2_b': nrm((N_ODD, GLA_RANK, GLA_DK), GLA_RANK ** -0.5),
        'od_gate_b2_b': nrm((N_ODD, GLA_DK), 0.01),
        'od_norm_g': 1.0 + nrm((N_ODD, GLA_DV), 0.02),
        'od_w_out': nrm((N_ODD, GLA_DV, D), GLA_DV ** -0.5),
        'router_w': nrm((DEPTH, D, N_EXPERTS), D ** -0.5),
        'router_b': nrm((DEPTH, N_EXPERTS), 0.01),
        'moe_w_gu': nrm((DEPTH, N_EXPERTS, D, 2 * D_EXPERT), D ** -0.5),
        'moe_b_gu': nrm((DEPTH, N_EXPERTS, 2 * D_EXPERT), 0.01),
        'moe_w_down': nrm((DEPTH, N_EXPERTS, D_EXPERT, D), D_EXPERT ** -0.5),
        'moe_b_down': nrm((DEPTH, N_EXPERTS, D), 0.01),
        'final_norm_g': 1.0 + nrm((D,), 0.02),
    }
    return inputs


def reference(x, c, ctx, c_ctx, mod_w, mod_b, norm_mix_g, norm_ffn_g,
              ev_w_in, ev_b_in, ev_conv_w, ev_conv_b, ev_mlstm_norm_g,
              ev_s5_a_re_f, ev_s5_a_im_f, ev_s5_log_dt_f,
              ev_s5_a_re_b, ev_s5_a_im_b, ev_s5_log_dt_b,
              ev_s5_b_re, ev_s5_b_im, ev_s5_c_re, ev_s5_c_im, ev_s5_d,
              ev_s5_glu_w, ev_s5_glu_b, ev_w_out,
              od_w_in, od_gate_w2_f, od_gate_b2_f, od_gate_w2_b, od_gate_b2_b,
              od_norm_g, od_w_out,
              router_w, router_b, moe_w_gu, moe_b_gu, moe_w_down, moe_b_down,
              final_norm_g):
    Bsz, L, D = x.shape
    rows = L // GRID_W

    def to_cols(a):
        return a.reshape(Bsz, rows, GRID_W, -1).transpose(0, 2, 1, 3).reshape(Bsz, L, -1)

    def to_rows(a):
        return a.reshape(Bsz, GRID_W, rows, -1).transpose(0, 2, 1, 3).reshape(Bsz, L, -1)

    h_lat, h_ctx = x, ctx
    for layer in range(DEPTH):
        last = layer == DEPTH - 1
        j = layer // 2
        mod_lat = (jax.nn.silu(c) @ mod_w[layer] + mod_b[layer])[:, None, :]
        mod_ctx = jax.nn.silu(c_ctx) @ mod_w[layer] + mod_b[layer]
        sh1_l, sc1_l, g1_l, sh2_l, sc2_l, g2_l = jnp.split(mod_lat, 6, axis=-1)
        sh1_c, sc1_c, g1_c, sh2_c, sc2_c, g2_c = jnp.split(mod_ctx, 6, axis=-1)
        a_lat = modulate(rmsnorm(h_lat, norm_mix_g[layer]), sh1_l, sc1_l)
        a_ctx = modulate(rmsnorm(h_ctx, norm_mix_g[layer]), sh1_c, sc1_c)
        if layer % 2 == 0:
            y_ctx, y_lat = even_mixer(
                a_ctx, a_lat, ev_w_in[j], ev_b_in[j], ev_conv_w[j], ev_conv_b[j], ev_mlstm_norm_g[j],
                ev_s5_a_re_f[j], ev_s5_a_im_f[j], ev_s5_log_dt_f[j],
                ev_s5_a_re_b[j], ev_s5_a_im_b[j], ev_s5_log_dt_b[j],
                ev_s5_b_re[j], ev_s5_b_im[j], ev_s5_c_re[j], ev_s5_c_im[j], ev_s5_d[j],
                ev_s5_glu_w[j], ev_s5_glu_b[j], ev_w_out[j])
        else:
            y_ctx, y_lat = odd_mixer(
                a_ctx, to_cols(a_lat), od_w_in[j], od_gate_w2_f[j], od_gate_b2_f[j],
                od_gate_w2_b[j], od_gate_b2_b[j], od_norm_g[j], od_w_out[j])
            y_lat = to_rows(y_lat)
        h_lat = h_lat + g1_l * y_lat
        f_lat = modulate(rmsnorm(h_lat, norm_ffn_g[layer]), sh2_l, sc2_l).reshape(Bsz * L, D)
        moe_params = (router_w[layer], router_b[layer], moe_w_gu[layer], moe_b_gu[layer],
                      moe_w_down[layer], moe_b_down[layer])
        if last:
            h_lat = h_lat + g2_l * moe_ffn(f_lat, *moe_params).reshape(Bsz, L, D)
        else:
            h_ctx = h_ctx + g1_c * y_ctx
            f_ctx = modulate(rmsnorm(h_ctx, norm_ffn_g[layer]), sh2_c, sc2_c).reshape(-1, D)
            n_ctx = f_ctx.shape[0]
            f_all = moe_ffn(jnp.concatenate([f_ctx, f_lat], axis=0), *moe_params)
            h_ctx = h_ctx + g2_c * f_all[:n_ctx].reshape(h_ctx.shape)
            h_lat = h_lat + g2_l * f_all[n_ctx:].reshape(Bsz, L, D)
    return rmsnorm(h_lat, final_norm_g)
```

```python
import functools
import math

import jax
import jax.numpy as jnp
from jax import lax
from jax.experimental import pallas as pl
from jax.experimental.pallas import tpu as pltpu

F32 = jnp.float32
BF16 = jnp.bfloat16

EPS = 1e-6
GRID_W = 64
MLSTM_HEADS = 4
MLSTM_CHUNK = 64
CONV_W = 3
S5_GROUP = 16
S5_STATE = 64
GLA_HEADS = 4
GLA_RANK = 16
GLA_TAU = 16.0
GLA_CHUNK = 64
N_EXPERTS = 32
TOP_K = 4
SWIGLU_LIMIT = 7.0
SWIGLU_ALPHA = 1.702

ROW_TILE = 256
MOE_TILE = 512
VMEM_LIMIT = 56 * 1024 * 1024


def _cparams(sem):
    return pltpu.CompilerParams(dimension_semantics=sem, vmem_limit_bytes=VMEM_LIMIT)


def _mod_kernel(c_ref, w_ref, b_ref, o_ref):
    c = c_ref[...]
    a = c * jax.nn.sigmoid(c)
    o_ref[0] = jnp.dot(a.astype(BF16), w_ref[0].astype(BF16), preferred_element_type=F32) + b_ref[0]


def _modulation(c_all, mod_w, mod_b):
    depth, d, n6 = mod_w.shape
    rows = c_all.shape[0]
    tn = d
    return pl.pallas_call(
        _mod_kernel,
        grid=(depth, n6 // tn),
        in_specs=[
            pl.BlockSpec((rows, d), lambda l, j: (0, 0)),
            pl.BlockSpec((1, d, tn), lambda l, j: (l, 0, j)),
            pl.BlockSpec((1, 1, tn), lambda l, j: (l, 0, j)),
        ],
        out_specs=pl.BlockSpec((1, rows, tn), lambda l, j: (l, 0, j)),
        out_shape=jax.ShapeDtypeStruct((depth, rows, n6), F32),
        compiler_params=_cparams(("arbitrary", "arbitrary")),
        name="modulation",
    )(c_all, mod_w, mod_b.reshape(depth, 1, n6))


def _norm_mod(x, g, sh, sc):
    ms = jnp.mean(x * x, axis=-1, keepdims=True)
    return (x * lax.rsqrt(ms + EPS) * g) * (1.0 + sc) + sh


def _nm_matmul_kernel(x_ref, g_ref, sh_ref, sc_ref, w_ref, b_ref, *out_refs, splits):
    a = _norm_mod(x_ref[0], g_ref[...], sh_ref[0, 0], sc_ref[0, 0])
    z = jnp.dot(a.astype(BF16), w_ref[...], preferred_element_type=F32) + b_ref[...]
    for (lo, hi), o_ref in zip(splits, out_refs):
        o_ref[0] = z[:, lo:hi].astype(o_ref.dtype)


def _seg_map(b, i):
    return (b, jnp.minimum(i, 1), 0, 0)


def _nm_matmul(h, g, shift, scale, w, bias, widths):
    bsz, lt, d = h.shape
    p = w.shape[1]
    splits, lo = [], 0
    for wd in widths:
        splits.append((lo, lo + wd))
        lo += wd
    assert lo == p
    tm = ROW_TILE
    return pl.pallas_call(
        functools.partial(_nm_matmul_kernel, splits=tuple(splits)),
        grid=(bsz, lt // tm),
        in_specs=[
            pl.BlockSpec((1, tm, d), lambda b, i: (b, i, 0)),
            pl.BlockSpec((1, d), lambda b, i: (0, 0)),
            pl.BlockSpec((1, 1, 1, d), _seg_map),
            pl.BlockSpec((1, 1, 1, d), _seg_map),
            pl.BlockSpec((d, p), lambda b, i: (0, 0)),
            pl.BlockSpec((1, p), lambda b, i: (0, 0)),
        ],
        out_specs=[pl.BlockSpec((1, tm, wd), lambda b, i: (b, i, 0)) for wd in widths],
        out_shape=[jax.ShapeDtypeStruct((bsz, lt, wd), F32) for wd in widths],
        compiler_params=_cparams(("parallel", "parallel")),
        name="norm_mod_matmul",
    )(h, g.reshape(1, d), shift, scale, w.astype(BF16), bias.reshape(1, p))


def _proj_residual_kernel(y_ref, w_ref, h_ref, gate_ref, o_ref):
    z = jnp.dot(y_ref[0].astype(BF16), w_ref[...], preferred_element_type=F32)
    o_ref[0] = h_ref[0] + gate_ref[0, 0] * z


def _proj_residual(y, w, h, gate):
    bsz, lt, k = y.shape
    d = w.shape[1]
    tm = ROW_TILE
    return pl.pallas_call(
        _proj_residual_kernel,
        grid=(bsz, lt // tm),
        in_specs=[
            pl.BlockSpec((1, tm, k), lambda b, i: (b, i, 0)),
            pl.BlockSpec((k, d), lambda b, i: (0, 0)),
            pl.BlockSpec((1, tm, d), lambda b, i: (b, i, 0)),
            pl.BlockSpec((1, 1, 1, d), _seg_map),
        ],
        out_specs=pl.BlockSpec((1, tm, d), lambda b, i: (b, i, 0)),
        out_shape=jax.ShapeDtypeStruct((bsz, lt, d), F32),
        compiler_params=_cparams(("parallel", "parallel")),
        name="proj_residual",
    )(y, w.astype(BF16), h, gate)


def _ffn_prep_kernel(x_ref, g_ref, sh_ref, sc_ref, rw_ref, rb_ref, f_ref, logit_ref):
    a = _norm_mod(x_ref[0], g_ref[...], sh_ref[0, 0], sc_ref[0, 0])
    f_ref[0] = a.astype(f_ref.dtype)
    logit_ref[0] = jnp.dot(a, rw_ref[...], preferred_element_type=F32,
                           precision=lax.Precision.HIGHEST) + rb_ref[...]


def _ffn_prep(h, g, shift, scale, router_w, router_b):
    bsz, lt, d = h.shape
    ne = router_w.shape[1]
    tm = ROW_TILE
    return pl.pallas_call(
        _ffn_prep_kernel,
        grid=(bsz, lt // tm),
        in_specs=[
            pl.BlockSpec((1, tm, d), lambda b, i: (b, i, 0)),
            pl.BlockSpec((1, d), lambda b, i: (0, 0)),
            pl.BlockSpec((1, 1, 1, d), _seg_map),
            pl.BlockSpec((1, 1, 1, d), _seg_map),
            pl.BlockSpec((d, ne), lambda b, i: (0, 0)),
            pl.BlockSpec((1, ne), lambda b, i: (0, 0)),
        ],
        out_specs=[
            pl.BlockSpec((1, tm, d), lambda b, i: (b, i, 0)),
            pl.BlockSpec((1, tm, ne), lambda b, i: (b, i, 0)),
        ],
        out_shape=[
            jax.ShapeDtypeStruct((bsz, lt, d), BF16),
            jax.ShapeDtypeStruct((bsz, lt, ne), F32),
        ],
        compiler_params=_cparams(("parallel", "parallel")),
        name="ffn_prep",
    )(h, g.reshape(1, d), shift, scale, router_w, router_b.reshape(1, ne))


def _moe_kernel(be_ref, nb_ref, x_ref, wg_ref, wu_ref, wd_ref, bg_ref, bu_ref, bd_ref, o_ref):
    i = pl.program_id(0)

    @pl.when(i < nb_ref[0])
    def _():
        x = x_ref[...]
        g = jnp.dot(x, wg_ref[0], preferred_element_type=F32) + bg_ref[0]
        u = jnp.dot(x, wu_ref[0], preferred_element_type=F32) + bu_ref[0]
        g = jnp.minimum(g, SWIGLU_LIMIT)
        u = jnp.clip(u, -SWIGLU_LIMIT, SWIGLU_LIMIT)
        hdn = (u + 1.0) * (g * jax.nn.sigmoid(SWIGLU_ALPHA * g))
        o_ref[...] = jnp.dot(hdn.astype(BF16), wd_ref[0], preferred_element_type=F32) + bd_ref[0]

    @pl.when(i >= nb_ref[0])
    def _():
        o_ref[...] = jnp.zeros_like(o_ref)


def _moe_experts(x_sorted, block_expert, n_used, wg, wu, wd, bg, bu, bd):
    n_rows, d = x_sorted.shape
    f = wg.shape[2]
    tm = MOE_TILE
    n_blocks = n_rows // tm
    ne = wg.shape[0]
    grid_spec = pltpu.PrefetchScalarGridSpec(
        num_scalar_prefetch=2,
        grid=(n_blocks,),
        in_specs=[
            pl.BlockSpec((tm, d), lambda i, be, nb: (i, 0)),
            pl.BlockSpec((1, d, f), lambda i, be, nb: (be[i], 0, 0)),
            pl.BlockSpec((1, d, f), lambda i, be, nb: (be[i], 0, 0)),
            pl.BlockSpec((1, f, d), lambda i, be, nb: (be[i], 0, 0)),
            pl.BlockSpec((1, 1, f), lambda i, be, nb: (be[i], 0, 0)),
            pl.BlockSpec((1, 1, f), lambda i, be, nb: (be[i], 0, 0)),
            pl.BlockSpec((1, 1, d), lambda i, be, nb: (be[i], 0, 0)),
        ],
        out_specs=pl.BlockSpec((tm, d), lambda i, be, nb: (i, 0)),
    )
    return pl.pallas_call(
        _moe_kernel,
        grid_spec=grid_spec,
        out_shape=jax.ShapeDtypeStruct((n_rows, d), F32),
        compiler_params=_cparams(("arbitrary",)),
        name="moe_experts",
    )(block_expert, n_used, x_sorted, wg, wu, wd,
      bg.reshape(ne, 1, f), bu.reshape(ne, 1, f), bd.reshape(ne, 1, d))


def _combine_kernel(y_ref, gt_ref, h_ref, g2_ref, o_ref):
    gt = gt_ref[0]
    acc = y_ref[0, 0] * gt[:, 0:1]
    for k in range(1, TOP_K):
        acc = acc + y_ref[k, 0] * gt[:, k:k + 1]
    o_ref[0] = h_ref[0] + g2_ref[0, 0] * acc


def _moe_combine(yg, gate, h, g2, seg_map):
    k, bsz, lt, d = yg.shape
    tm = ROW_TILE
    return pl.pallas_call(
        _combine_kernel,
        grid=(bsz, lt // tm),
        in_specs=[
            pl.BlockSpec((k, 1, tm, d), lambda b, i: (0, b, i, 0)),
            pl.BlockSpec((1, tm, k), lambda b, i: (b, i, 0)),
            pl.BlockSpec((1, tm, d), lambda b, i: (b, i, 0)),
            pl.BlockSpec((1, 1, 1, d), seg_map),
        ],
        out_specs=pl.BlockSpec((1, tm, d), lambda b, i: (b, i, 0)),
        out_shape=jax.ShapeDtypeStruct((bsz, lt, d), F32),
        compiler_params=_cparams(("parallel", "parallel")),
        name="moe_combine",
    )(yg, gate, h, g2)


def _lat_seg_map(b, i):
    return (b, 1, 0, 0)


def _moe_layer(h, f, logits, g2, weights, lat_only, n_ctx):
    wg, wu, wd, bg, bu, bd = weights
    bsz, lt, d = h.shape
    if lat_only:
        h_in, f, logits = h[:, n_ctx:], f[:, n_ctx:], logits[:, n_ctx:]
    else:
        h_in = h
    ltok = h_in.shape[1]
    n = bsz * ltok
    top_logit, top_e = lax.top_k(logits.reshape(n, -1), TOP_K)
    gate = jax.nn.softmax(top_logit, axis=-1)
    n_assign = n * TOP_K
    tm = MOE_TILE
    flat_e = top_e.reshape(-1).astype(jnp.int32)
    order = jnp.argsort(flat_e, stable=True).astype(jnp.int32)
    e_sorted = flat_e[order]
    counts = jnp.sum((flat_e[:, None] == jnp.arange(N_EXPERTS, dtype=jnp.int32)[None, :]).astype(jnp.int32), axis=0)
    padded = (counts + tm - 1) // tm * tm
    pad_end = jnp.cumsum(padded)
    pad_start = pad_end - padded
    start = jnp.cumsum(counts) - counts
    dest = pad_start[e_sorted] + jnp.arange(n_assign, dtype=jnp.int32) - start[e_sorted]
    n_blocks = -(-(n_assign + N_EXPERTS * (tm - 1)) // tm)
    n_rows = n_blocks * tm
    row_token = jnp.zeros((n_rows,), jnp.int32).at[dest].set(order // TOP_K)
    pos = jnp.zeros((n_assign,), jnp.int32).at[order].set(dest)
    block_expert = jnp.minimum(
        jnp.searchsorted(pad_end, jnp.arange(n_blocks, dtype=jnp.int32) * tm, side='right'),
        N_EXPERTS - 1).astype(jnp.int32)
    n_used = (pad_end[-1] // tm).astype(jnp.int32).reshape(1)
    x_sorted = jnp.take(f.reshape(n, d), row_token, axis=0)
    y = _moe_experts(x_sorted, block_expert, n_used, wg, wu, wd, bg, bu, bd)
    pos_k = pos.reshape(n, TOP_K).T
    yg = jnp.take(y, pos_k.reshape(-1), axis=0).reshape(TOP_K, bsz, ltok, d)
    out = _moe_combine(yg, gate.reshape(bsz, ltok, TOP_K), h_in, g2,
                       _lat_seg_map if lat_only else _seg_map)
    return out


def _rmsnorm_kernel(x_ref, g_ref, o_ref):
    x = x_ref[0]
    ms = jnp.mean(x * x, axis=-1, keepdims=True)
    o_ref[0] = x * lax.rsqrt(ms + EPS) * g_ref[...]


def _final_norm(h, g):
    bsz, lt, d = h.shape
    tm = ROW_TILE
    return pl.pallas_call(
        _rmsnorm_kernel,
        grid=(bsz, lt // tm),
        in_specs=[pl.BlockSpec((1, tm, d), lambda b, i: (b, i, 0)),
                  pl.BlockSpec((1, d), lambda b, i: (0, 0))],
        out_specs=pl.BlockSpec((1, tm, d), lambda b, i: (b, i, 0)),
        out_shape=jax.ShapeDtypeStruct((bsz, lt, d), F32),
        compiler_params=_cparams(("parallel", "parallel")),
        name="final_norm",
    )(h, g.reshape(1, d))


def _head_rmsnorm(h, g):
    bsz, l, hh, d = h.shape
    y = h * lax.rsqrt(jnp.mean(h * h, axis=-1, keepdims=True) + EPS)
    return y.reshape(bsz, l, hh * d) * g


def _centred_dwconv(x, w, b):
    l = x.shape[1]
    half = CONV_W // 2
    xp = jnp.pad(x, ((0, 0), (half, half), (0, 0)))
    y = b
    for tap in range(CONV_W):
        y = y + w[tap] * xp[:, tap:tap + l]
    return y


def _bidir(run_f, run_b, init, ctx_f, ctx_b, lat_f, lat_b):
    rev = lambda xs: tuple(jnp.flip(a, axis=1) for a in xs)
    yc_f, sc_f = run_f(ctx_f, init)
    yl_f, _ = run_f(lat_f, sc_f)
    yc_b, sc_b = run_b(rev(ctx_b), init)
    yl_b, _ = run_b(rev(lat_b), sc_b)
    return yc_f + jnp.flip(yc_b, axis=1), yl_f + jnp.flip(yl_b, axis=1)


def _mlstm_run(inputs, state):
    q, k, v, log_i, log_f = inputs
    bsz, l, hh, dh = q.shape
    t = MLSTM_CHUNK
    nc = l // t
    chunk = lambda a: jnp.moveaxis(a.reshape((bsz, nc, t) + a.shape[2:]), 1, 0)
    causal = jnp.tril(jnp.ones((t, t), dtype=bool))

    def step(carry, xs):
        c, n, m = carry
        qc, kc, vc, ic, fc = xs
        ic = jnp.swapaxes(ic, 1, 2)
        b = jnp.cumsum(jnp.swapaxes(fc, 1, 2), axis=-1)
        logw = jnp.where(causal, b[..., :, None] - b[..., None, :] + ic[..., None, :], -jnp.inf)
        log_inter = b + m[..., None]
        m_t = jnp.maximum(log_inter, jnp.max(logw, axis=-1))
        w_intra = jnp.exp(logw - m_t[..., None])
        w_inter = jnp.exp(log_inter - m_t)
        scores = jnp.einsum('bthd,bshd->bhts', qc, kc) * w_intra
        num = (jnp.einsum('bhts,bshe->bhte', scores, vc)
               + w_inter[..., None] * jnp.einsum('bthd,bhde->bhte', qc, c))
        den = jnp.sum(scores, axis=-1) + w_inter * jnp.einsum('bthd,bhd->bht', qc, n)
        h = num / jnp.maximum(jnp.abs(den), jnp.exp(-m_t))[..., None]
        b_last = b[..., -1]
        log_g = b_last[..., None] - b + ic
        m_new = jnp.maximum(b_last + m, jnp.max(log_g, axis=-1))
        keep = jnp.exp(b_last + m - m_new)
        w_g = jnp.exp(log_g - m_new[..., None])
        c_new = keep[..., None, None] * c + jnp.einsum('bhs,bshd,bshe->bhde', w_g, kc, vc)
        n_new = keep[..., None] * n + jnp.einsum('bhs,bshd->bhd', w_g, kc)
        return (c_new, n_new, m_new), jnp.swapaxes(h, 1, 2)

    state, hs = lax.scan(step, state, tuple(chunk(a) for a in (q, k, v, log_i, log_f)))
    return jnp.moveaxis(hs, 0, 1).reshape(bsz, l, hh, dh), state


def _complex_affine_combine(e1, e2):
    a1r, a1i, b1r, b1i = e1
    a2r, a2i, b2r, b2i = e2
    return (a2r * a1r - a2i * a1i, a2r * a1i + a2i * a1r,
            a2r * b1r - a2i * b1i + b2r, a2r * b1i + a2i * b1r + b2i)


def _s5_run(inputs, state, params):
    (u,) = inputs
    a_re, a_im, log_dt, b_re, b_im, c_re, c_im = params
    s0_re, s0_im = state
    l = u.shape[1]
    dt = jnp.exp(log_dt)[:, None]
    lam_re = jnp.minimum(a_re, -1e-4)
    lam_im = a_im
    decay = jnp.exp(lam_re * dt)
    ab_re = decay * jnp.cos(lam_im * dt)
    ab_im = decay * jnp.sin(lam_im * dt)
    den = lam_re * lam_re + lam_im * lam_im
    zr = ((ab_re - 1) * lam_re + ab_im * lam_im) / den
    zi = (ab_im * lam_re - (ab_re - 1) * lam_im) / den
    bb_re = zr[..., None] * b_re - zi[..., None] * b_im
    bb_im = zr[..., None] * b_im + zi[..., None] * b_re
    bu_re = jnp.einsum('blgc,gpc->lbgp', u, bb_re)
    bu_im = jnp.einsum('blgc,gpc->lbgp', u, bb_im)
    a_seq_re = jnp.broadcast_to(ab_re, (l, 1) + ab_re.shape)
    a_seq_im = jnp.broadcast_to(ab_im, (l, 1) + ab_im.shape)
    pw_re, pw_im, s_re, s_im = lax.associative_scan(_complex_affine_combine, (a_seq_re, a_seq_im, bu_re, bu_im))
    s_re, s_im = (s_re + pw_re * s0_re - pw_im * s0_im, s_im + pw_re * s0_im + pw_im * s0_re)
    y = jnp.einsum('lbgp,gcp->blgc', s_re, c_re) - jnp.einsum('lbgp,gcp->blgc', s_im, c_im)
    return y, (s_re[-1], s_im[-1])


def _gla_run(inputs, state):
    q, k, v, log_a = inputs
    bsz, l, hh, dk = q.shape
    dv = v.shape[-1]
    t = GLA_CHUNK
    nc = l // t
    chunk = lambda a: jnp.moveaxis(a.reshape((bsz, nc, t) + a.shape[2:]), 1, 0)
    causal = jnp.tril(jnp.ones((t, t), dtype=bool))[None, :, :, None, None]

    def step(s, xs):
        qc, kc, vc, ac = xs
        b = jnp.cumsum(ac, axis=1)
        rel = jnp.where(causal, b[:, :, None] - b[:, None, :], -jnp.inf)
        att = jnp.einsum('bthd,bshd,btshd->bhts', qc, kc, jnp.exp(rel))
        o = (jnp.einsum('bhts,bshe->bthe', att, vc)
             + jnp.einsum('bthd,bhde->bthe', qc * jnp.exp(b), s))
        b_last = b[:, -1]
        s_new = (jnp.exp(b_last)[..., None] * s
                 + jnp.einsum('bshd,bshe->bhde', kc * jnp.exp(b_last[:, None] - b), vc))
        return s_new, o

    state, os_ = lax.scan(step, state, tuple(chunk(a) for a in (q, k, v, log_a)))
    return jnp.moveaxis(os_, 0, 1).reshape(bsz, l, hh, dv), state


def _even_mixer_core(qk, v, o, gates, u, n_ctx, p):
    hh = MLSTM_HEADS
    mw = v.shape[-1]
    dh = mw // hh
    bsz = v.shape[0]
    sg = u.shape[-1] // S5_GROUP

    def prep(sl):
        qk_s, v_s, gates_s, u_s = qk[:, sl], v[:, sl], gates[:, sl], u[:, sl]
        l = v_s.shape[1]
        qkc = jax.nn.silu(_centred_dwconv(qk_s, p['conv_w'], p['conv_b']))
        q = qkc[..., :mw].reshape(bsz, l, hh, dh) * dh ** -0.5
        k = qkc[..., mw:].reshape(bsz, l, hh, dh)
        vv = v_s.reshape(bsz, l, hh, dh)
        g4 = gates_s.reshape(bsz, l, 4, hh)
        fwd = (q, k, vv, g4[:, :, 0], jax.nn.log_sigmoid(g4[:, :, 1]))
        bwd = (q, k, vv, g4[:, :, 2], jax.nn.log_sigmoid(g4[:, :, 3]))
        return fwd, bwd, u_s.reshape(bsz, l, sg, S5_GROUP)

    cf, cb, cu = prep(slice(0, n_ctx))
    lf, lb, lu = prep(slice(n_ctx, None))
    m_init = (jnp.zeros((bsz, hh, dh, dh), F32), jnp.zeros((bsz, hh, dh), F32), jnp.zeros((bsz, hh), F32))
    mc, ml = _bidir(_mlstm_run, _mlstm_run, m_init, cf, cb, lf, lb)
    s5_f = functools.partial(_s5_run, params=(p['a_re_f'], p['a_im_f'], p['log_dt_f'], p['b_re'], p['b_im'], p['c_re'], p['c_im']))
    s5_b = functools.partial(_s5_run, params=(p['a_re_b'], p['a_im_b'], p['log_dt_b'], p['b_re'], p['b_im'], p['c_re'], p['c_im']))
    s_init = (jnp.zeros((bsz, sg, S5_STATE), F32), jnp.zeros((bsz, sg, S5_STATE), F32))
    sc, sl_ = _bidir(s5_f, s5_b, s_init, (cu,), (cu,), (lu,), (lu,))
    m_all = jnp.concatenate([mc, ml], axis=1)
    s_all = jnp.concatenate([sc, sl_], axis=1)
    d_grp = p['d_skip'].reshape(sg, S5_GROUP)
    lt = m_all.shape[1]
    m_out = _head_rmsnorm(m_all, p['mnorm_g']) * jax.nn.sigmoid(o)
    y = jax.nn.gelu((s_all + d_grp * u.reshape(bsz, lt, sg, S5_GROUP)).reshape(bsz, lt, -1))
    s_out = y * jax.nn.sigmoid(y @ p['glu_w'] + p['glu_b'])
    return jnp.concatenate([m_out, s_out], axis=-1)


def _odd_mixer_core(q, k, v, g, r, n_ctx, p):
    hh = GLA_HEADS
    bsz, lt, dkt = q.shape
    dk = dkt // hh
    dv = v.shape[-1] // hh
    rows = (lt - n_ctx) // GRID_W

    def to_cols(a):
        c, l = a[:, :n_ctx], a[:, n_ctx:]
        l = l.reshape(bsz, rows, GRID_W, -1).transpose(0, 2, 1, 3).reshape(bsz, lt - n_ctx, -1)
        return jnp.concatenate([c, l], axis=1)

    def to_rows(a):
        c, l = a[:, :n_ctx], a[:, n_ctx:]
        l = l.reshape(bsz, GRID_W, rows, -1).transpose(0, 2, 1, 3).reshape(bsz, lt - n_ctx, -1)
        return jnp.concatenate([c, l], axis=1)

    q, k, v, r = to_cols(q), to_cols(k), to_cols(v), to_cols(r)
    heads = lambda a, d: a.reshape(bsz, lt, hh, d)
    gate = lambda rr, w, b: jax.nn.log_sigmoid(rr @ w + b).reshape(bsz, lt, hh, dk) / GLA_TAU
    qh = heads(q, dk) * dk ** -0.5
    kh = heads(k, dk)
    vh = heads(v, dv)
    la_f = gate(r[..., :GLA_RANK], p['w2_f'], p['b2_f'])
    la_b = gate(r[..., GLA_RANK:], p['w2_b'], p['b2_b'])
    sl_c, sl_l = slice(0, n_ctx), slice(n_ctx, None)
    init = jnp.zeros((bsz, hh, dk, dv), F32)
    cut = lambda xs, sl: tuple(a[:, sl] for a in xs)
    f_in, b_in = (qh, kh, vh, la_f), (qh, kh, vh, la_b)
    oc, ol = _bidir(_gla_run, _gla_run, init, cut(f_in, sl_c), cut(b_in, sl_c), cut(f_in, sl_l), cut(b_in, sl_l))
    o_all = to_rows(jnp.concatenate([oc, ol], axis=1).reshape(bsz, lt, -1)).reshape(bsz, lt, hh, dv)
    return _head_rmsnorm(o_all, p['norm_g']) * jax.nn.silu(g)


def kernel(x, c, ctx, c_ctx, mod_w, mod_b, norm_mix_g, norm_ffn_g, ev_w_in, ev_b_in, ev_conv_w, ev_conv_b, ev_mlstm_norm_g, ev_s5_a_re_f, ev_s5_a_im_f, ev_s5_log_dt_f, ev_s5_a_re_b, ev_s5_a_im_b, ev_s5_log_dt_b, ev_s5_b_re, ev_s5_b_im, ev_s5_c_re, ev_s5_c_im, ev_s5_d, ev_s5_glu_w, ev_s5_glu_b, ev_w_out, od_w_in, od_gate_w2_f, od_gate_b2_f, od_gate_w2_b, od_gate_b2_b, od_norm_g, od_w_out, router_w, router_b, moe_w_gu, moe_b_gu, moe_w_down, moe_b_down, final_norm_g):
    bsz, seq, d = x.shape
    n_ctx = ctx.shape[1]
    depth = mod_w.shape[0]
    assert n_ctx == ROW_TILE and seq % ROW_TILE == 0

    h = jnp.concatenate([ctx, x], axis=1)
    c_all = jnp.concatenate([c, c_ctx[None, :]], axis=0)
    pad = (-c_all.shape[0]) % 8
    c_all = jnp.pad(c_all, ((0, pad), (0, 0)))
    mods = _modulation(c_all, mod_w, mod_b)
    mod_lat = mods[:, :bsz]
    mod_ctx = jnp.broadcast_to(mods[:, bsz:bsz + 1], mod_lat.shape)
    mod6 = jnp.stack([mod_ctx, mod_lat], axis=2).reshape(depth, bsz, 2, 6, 1, d)

    wg_all = moe_w_gu[..., 0::2].astype(BF16)
    wu_all = moe_w_gu[..., 1::2].astype(BF16)
    wd_all = moe_w_down.astype(BF16)
    bg_all = moe_b_gu[..., 0::2]
    bu_all = moe_b_gu[..., 1::2]

    mw = ev_conv_w.shape[-1] // 2
    n_gates = 4 * MLSTM_HEADS
    s5w = ev_s5_d.shape[-1]
    dk_t = od_gate_w2_f.shape[-1]
    dv_t = od_norm_g.shape[-1]

    for layer in range(depth):
        last = layer == depth - 1
        j = layer // 2
        m6 = mod6[layer]
        sh1, sc1, g1, sh2, sc2, g2 = (m6[:, :, i] for i in range(6))
        if layer % 2 == 0:
            w_in, b_in = ev_w_in[j], ev_b_in[j]
            cols = jnp.concatenate([jnp.arange(0, 4 * mw), jnp.arange(4 * mw + n_gates, 4 * mw + n_gates + s5w),
                                    jnp.arange(4 * mw, 4 * mw + n_gates)])
            qk, v, o, u, gates = _nm_matmul(h, norm_mix_g[layer], sh1, sc1, w_in[:, cols], b_in[cols],
                                            (2 * mw, mw, mw, s5w, n_gates))
            p = dict(conv_w=ev_conv_w[j], conv_b=ev_conv_b[j], mnorm_g=ev_mlstm_norm_g[j],
                     a_re_f=ev_s5_a_re_f[j], a_im_f=ev_s5_a_im_f[j], log_dt_f=ev_s5_log_dt_f[j],
                     a_re_b=ev_s5_a_re_b[j], a_im_b=ev_s5_a_im_b[j], log_dt_b=ev_s5_log_dt_b[j],
                     b_re=ev_s5_b_re[j], b_im=ev_s5_b_im[j], c_re=ev_s5_c_re[j], c_im=ev_s5_c_im[j],
                     d_skip=ev_s5_d[j], glu_w=ev_s5_glu_w[j], glu_b=ev_s5_glu_b[j])
            y = _even_mixer_core(qk, v, o, gates, u, n_ctx, p)
            w_out = ev_w_out[j]
        else:
            qq, kk, vv, gg, rr = _nm_matmul(h, norm_mix_g[layer], sh1, sc1, od_w_in[j],
                                            jnp.zeros((od_w_in.shape[-1],), F32),
                                            (dk_t, dk_t, dv_t, dv_t, 2 * GLA_RANK))
            p = dict(w2_f=od_gate_w2_f[j], b2_f=od_gate_b2_f[j], w2_b=od_gate_w2_b[j], b2_b=od_gate_b2_b[j],
                     norm_g=od_norm_g[j])
            y = _odd_mixer_core(qq, kk, vv, gg, rr, n_ctx, p)
            w_out = od_w_out[j]
        h = _proj_residual(y, w_out, h, g1)
        f, logits = _ffn_prep(h, norm_ffn_g[layer], sh2, sc2, router_w[layer], router_b[layer])
        weights = (wg_all[layer], wu_all[layer], wd_all[layer], bg_all[layer], bu_all[layer], moe_b_down[layer])
        if last:
            h = _moe_layer(h, f, logits, g2, weights, True, n_ctx)
        else:
            h = _moe_layer(h, f, logits, g2, weights, False, n_ctx)
    return _final_norm(h, final_norm_g)
```

```python
import functools

import jax
import jax.numpy as jnp
from jax import lax
from jax.experimental import pallas as pl
from jax.experimental.pallas import tpu as pltpu

F32 = jnp.float32
BF16 = jnp.bfloat16
HI = lax.Precision.HIGHEST

EPS = 1e-6
GRID_W = 64
MLSTM_HEADS = 4
S5_GROUP = 16
GLA_HEADS = 4
GLA_RANK = 16
GLA_TAU = 16.0
N_EXPERTS = 32
TOP_K = 4
SWIGLU_LIMIT = 7.0
SWIGLU_ALPHA = 1.702

ROW_TILE = 256
MOE_TILE = 512
MIX_CHUNK = 64
S5_J = 8
VMEM_LIMIT = 56 * 1024 * 1024

NT = (((1,), (1,)), ((), ()))
TN = (((0,), (0,)), ((), ()))


def _cparams(sem):
    return pltpu.CompilerParams(dimension_semantics=sem, vmem_limit_bytes=VMEM_LIMIT)


def _mod_kernel(c_ref, w_ref, b_ref, o_ref):
    c = c_ref[...]
    a = c * jax.nn.sigmoid(c)
    o_ref[0] = jnp.dot(a.astype(BF16), w_ref[0].astype(BF16), preferred_element_type=F32) + b_ref[0]


def _modulation(c_all, mod_w, mod_b):
    depth, d, n6 = mod_w.shape
    rows = c_all.shape[0]
    tn = d
    return pl.pallas_call(
        _mod_kernel,
        grid=(depth, n6 // tn),
        in_specs=[
            pl.BlockSpec((rows, d), lambda l, j: (0, 0)),
            pl.BlockSpec((1, d, tn), lambda l, j: (l, 0, j)),
            pl.BlockSpec((1, 1, tn), lambda l, j: (l, 0, j)),
        ],
        out_specs=pl.BlockSpec((1, rows, tn), lambda l, j: (l, 0, j)),
        out_shape=jax.ShapeDtypeStruct((depth, rows, n6), F32),
        compiler_params=_cparams(("arbitrary", "arbitrary")),
        name="modulation",
    )(c_all, mod_w, mod_b.reshape(depth, 1, n6))


def _norm_mod(x, g, sh, sc):
    ms = jnp.mean(x * x, axis=-1, keepdims=True)
    return (x * lax.rsqrt(ms + EPS) * g) * (1.0 + sc) + sh


def _nm_matmul_kernel(x_ref, g_ref, sh_ref, sc_ref, w_ref, b_ref, *out_refs, splits):
    a = _norm_mod(x_ref[0], g_ref[...], sh_ref[0, 0], sc_ref[0, 0])
    z = jnp.dot(a.astype(BF16), w_ref[...], preferred_element_type=F32) + b_ref[...]
    for (lo, hi), o_ref in zip(splits, out_refs):
        o_ref[0] = z[:, lo:hi].astype(o_ref.dtype)


def _seg_map(b, i):
    return (b, jnp.minimum(i, 1), 0, 0)


def _nm_matmul(h, g, shift, scale, w, bias, widths):
    bsz, lt, d = h.shape
    p = w.shape[1]
    splits, lo = [], 0
    for wd in widths:
        splits.append((lo, lo + wd))
        lo += wd
    assert lo == p
    tm = ROW_TILE
    return pl.pallas_call(
        functools.partial(_nm_matmul_kernel, splits=tuple(splits)),
        grid=(bsz, lt // tm),
        in_specs=[
            pl.BlockSpec((1, tm, d), lambda b, i: (b, i, 0)),
            pl.BlockSpec((1, d), lambda b, i: (0, 0)),
            pl.BlockSpec((1, 1, 1, d), _seg_map),
            pl.BlockSpec((1, 1, 1, d), _seg_map),
            pl.BlockSpec((d, p), lambda b, i: (0, 0)),
            pl.BlockSpec((1, p), lambda b, i: (0, 0)),
        ],
        out_specs=[pl.BlockSpec((1, tm, wd), lambda b, i: (b, i, 0)) for wd in widths],
        out_shape=[jax.ShapeDtypeStruct((bsz, lt, wd), F32) for wd in widths],
        compiler_params=_cparams(("parallel", "parallel")),
        name="norm_mod_matmul",
    )(h, g.reshape(1, d), shift, scale, w.astype(BF16), bias.reshape(1, p))


def _ffn_prep_kernel(x_ref, g_ref, sh_ref, sc_ref, rw_ref, rb_ref, f_ref, logit_ref):
    a = _norm_mod(x_ref[0], g_ref[...], sh_ref[0, 0], sc_ref[0, 0])
    f_ref[0] = a.astype(f_ref.dtype)
    logit_ref[0] = jnp.dot(a, rw_ref[...], preferred_element_type=F32, precision=HI) + rb_ref[...]


def _ffn_prep(h, g, shift, scale, router_w, router_b):
    bsz, lt, d = h.shape
    ne = router_w.shape[1]
    tm = ROW_TILE
    return pl.pallas_call(
        _ffn_prep_kernel,
        grid=(bsz, lt // tm),
        in_specs=[
            pl.BlockSpec((1, tm, d), lambda b, i: (b, i, 0)),
            pl.BlockSpec((1, d), lambda b, i: (0, 0)),
            pl.BlockSpec((1, 1, 1, d), _seg_map),
            pl.BlockSpec((1, 1, 1, d), _seg_map),
            pl.BlockSpec((d, ne), lambda b, i: (0, 0)),
            pl.BlockSpec((1, ne), lambda b, i: (0, 0)),
        ],
        out_specs=[
            pl.BlockSpec((1, tm, d), lambda b, i: (b, i, 0)),
            pl.BlockSpec((1, tm, ne), lambda b, i: (b, i, 0)),
        ],
        out_shape=[
            jax.ShapeDtypeStruct((bsz, lt, d), BF16),
            jax.ShapeDtypeStruct((bsz, lt, ne), F32),
        ],
        compiler_params=_cparams(("parallel", "parallel")),
        name="ffn_prep",
    )(h, g.reshape(1, d), shift, scale, router_w, router_b.reshape(1, ne))


def _moe_kernel(be_ref, nb_ref, x_ref, wg_ref, wu_ref, wd_ref, bg_ref, bu_ref, bd_ref, o_ref):
    i = pl.program_id(0)

    @pl.when(i < nb_ref[0])
    def _():
        x = x_ref[...]
        g = jnp.dot(x, wg_ref[0], preferred_element_type=F32) + bg_ref[0]
        u = jnp.dot(x, wu_ref[0], preferred_element_type=F32) + bu_ref[0]
        g = jnp.minimum(g, SWIGLU_LIMIT)
        u = jnp.clip(u, -SWIGLU_LIMIT, SWIGLU_LIMIT)
        hdn = (u + 1.0) * (g * jax.nn.sigmoid(SWIGLU_ALPHA * g))
        o_ref[...] = jnp.dot(hdn.astype(BF16), wd_ref[0], preferred_element_type=F32) + bd_ref[0]

    @pl.when(i >= nb_ref[0])
    def _():
        o_ref[...] = jnp.zeros_like(o_ref)


def _moe_experts(x_sorted, block_expert, n_used, wg, wu, wd, bg, bu, bd):
    n_rows, d = x_sorted.shape
    f = wg.shape[2]
    tm = MOE_TILE
    n_blocks = n_rows // tm
    ne = wg.shape[0]
    grid_spec = pltpu.PrefetchScalarGridSpec(
        num_scalar_prefetch=2,
        grid=(n_blocks,),
        in_specs=[
            pl.BlockSpec((tm, d), lambda i, be, nb: (i, 0)),
            pl.BlockSpec((1, d, f), lambda i, be, nb: (be[i], 0, 0)),
            pl.BlockSpec((1, d, f), lambda i, be, nb: (be[i], 0, 0)),
            pl.BlockSpec((1, f, d), lambda i, be, nb: (be[i], 0, 0)),
            pl.BlockSpec((1, 1, f), lambda i, be, nb: (be[i], 0, 0)),
            pl.BlockSpec((1, 1, f), lambda i, be, nb: (be[i], 0, 0)),
            pl.BlockSpec((1, 1, d), lambda i, be, nb: (be[i], 0, 0)),
        ],
        out_specs=pl.BlockSpec((tm, d), lambda i, be, nb: (i, 0)),
    )
    return pl.pallas_call(
        _moe_kernel,
        grid_spec=grid_spec,
        out_shape=jax.ShapeDtypeStruct((n_rows, d), F32),
        compiler_params=_cparams(("arbitrary",)),
        name="moe_experts",
    )(block_expert, n_used, x_sorted, wg, wu, wd,
      bg.reshape(ne, 1, f), bu.reshape(ne, 1, f), bd.reshape(ne, 1, d))


def _combine_kernel(y_ref, gt_ref, h_ref, g2_ref, o_ref):
    gt = gt_ref[0]
    acc = y_ref[0, 0] * gt[:, 0:1]
    for k in range(1, TOP_K):
        acc = acc + y_ref[k, 0] * gt[:, k:k + 1]
    o_ref[0] = h_ref[0] + g2_ref[0, 0] * acc


def _moe_combine(yg, gate, h, g2, seg_map):
    k, bsz, lt, d = yg.shape
    tm = ROW_TILE
    return pl.pallas_call(
        _combine_kernel,
        grid=(bsz, lt // tm),
        in_specs=[
            pl.BlockSpec((k, 1, tm, d), lambda b, i: (0, b, i, 0)),
            pl.BlockSpec((1, tm, k), lambda b, i: (b, i, 0)),
            pl.BlockSpec((1, tm, d), lambda b, i: (b, i, 0)),
            pl.BlockSpec((1, 1, 1, d), seg_map),
        ],
        out_specs=pl.BlockSpec((1, tm, d), lambda b, i: (b, i, 0)),
        out_shape=jax.ShapeDtypeStruct((bsz, lt, d), F32),
        compiler_params=_cparams(("parallel", "parallel")),
        name="moe_combine",
    )(yg, gate, h, g2)


def _lat_seg_map(b, i):
    return (b, 1, 0, 0)


def _moe_layer(h, f, logits, g2, weights, lat_only, n_ctx):
    wg, wu, wd, bg, bu, bd = weights
    bsz, lt, d = h.shape
    if lat_only:
        h_in, f, logits = h[:, n_ctx:], f[:, n_ctx:], logits[:, n_ctx:]
    else:
        h_in = h
    ltok = h_in.shape[1]
    n = bsz * ltok
    top_logit, top_e = lax.top_k(logits.reshape(n, -1), TOP_K)
    gate = jax.nn.softmax(top_logit, axis=-1)
    n_assign = n * TOP_K
    tm = MOE_TILE
    flat_e = top_e.reshape(-1).astype(jnp.int32)
    order = jnp.argsort(flat_e, stable=True).astype(jnp.int32)
    e_sorted = flat_e[order]
    counts = jnp.sum((flat_e[:, None] == jnp.arange(N_EXPERTS, dtype=jnp.int32)[None, :]).astype(jnp.int32), axis=0)
    padded = (counts + tm - 1) // tm * tm
    pad_end = jnp.cumsum(padded)
    pad_start = pad_end - padded
    start = jnp.cumsum(counts) - counts
    dest = pad_start[e_sorted] + jnp.arange(n_assign, dtype=jnp.int32) - start[e_sorted]
    n_blocks = -(-(n_assign + N_EXPERTS * (tm - 1)) // tm)
    n_rows = n_blocks * tm
    row_token = jnp.zeros((n_rows,), jnp.int32).at[dest].set(order // TOP_K)
    pos = jnp.zeros((n_assign,), jnp.int32).at[order].set(dest)
    block_expert = jnp.minimum(
        jnp.searchsorted(pad_end, jnp.arange(n_blocks, dtype=jnp.int32) * tm, side='right'),
        N_EXPERTS - 1).astype(jnp.int32)
    n_used = (pad_end[-1] // tm).astype(jnp.int32).reshape(1)
    x_sorted = jnp.take(f.reshape(n, d), row_token, axis=0)
    y = _moe_experts(x_sorted, block_expert, n_used, wg, wu, wd, bg, bu, bd)
    pos_k = pos.reshape(n, TOP_K).T
    yg = jnp.take(y, pos_k.reshape(-1), axis=0).reshape(TOP_K, bsz, ltok, d)
    out = _moe_combine(yg, gate.reshape(bsz, ltok, TOP_K), h_in, g2,
                       _lat_seg_map if lat_only else _seg_map)
    return out


def _rmsnorm_kernel(x_ref, g_ref, o_ref):
    x = x_ref[0]
    ms = jnp.mean(x * x, axis=-1, keepdims=True)
    o_ref[0] = x * lax.rsqrt(ms + EPS) * g_ref[...]


def _final_norm(h, g):
    bsz, lt, d = h.shape
    tm = ROW_TILE
    return pl.pallas_call(
        _rmsnorm_kernel,
        grid=(bsz, lt // tm),
        in_specs=[pl.BlockSpec((1, tm, d), lambda b, i: (b, i, 0)),
                  pl.BlockSpec((1, d), lambda b, i: (0, 0))],
        out_specs=pl.BlockSpec((1, tm, d), lambda b, i: (b, i, 0)),
        out_shape=jax.ShapeDtypeStruct((bsz, lt, d), F32),
        compiler_params=_cparams(("parallel", "parallel")),
        name="final_norm",
    )(h, g.reshape(1, d))


def _chunk_order(d, c, n_ctx_chunks, n_chunks):
    bwd = jnp.where(c < n_ctx_chunks, n_ctx_chunks - 1 - c, n_chunks + n_ctx_chunks - 1 - c)
    return jnp.where(d == 0, c, bwd)


def _dir_tri(d, t):
    row = lax.broadcasted_iota(jnp.int32, (t, t), 0)
    col = lax.broadcasted_iota(jnp.int32, (t, t), 1)
    return jnp.where(d == 0, col - row, row - col) <= 0


def _gla_kernel(q_ref, k_ref, v_ref, r_ref, w2_ref, b2_ref, o_ref, st_ref, *, t, heads, scale):
    d = pl.program_id(0)
    c = pl.program_id(2)

    @pl.when(c == 0)
    def _():
        st_ref[...] = jnp.zeros_like(st_ref)

    dk = q_ref.shape[-1] // heads
    dv = v_ref.shape[-1] // heads
    x = jnp.dot(r_ref[0], w2_ref[0], preferred_element_type=F32, precision=HI) + b2_ref[0]
    la = jax.nn.log_sigmoid(x) * (1.0 / GLA_TAU)
    mask = _dir_tri(d, t)
    b = jnp.dot(mask.astype(F32), la, preferred_element_type=F32, precision=HI)
    mid = t // 2
    b_m = b[mid:mid + 1, :]
    b_end = jnp.where(d == 0, b[t - 1:t, :], b[0:1, :])
    qt = q_ref[0] * (jnp.exp(b - b_m) * scale)
    kt = k_ref[0] * jnp.exp(b_m - b)
    qe = (qt * jnp.exp(b_m)).astype(BF16)
    kh_end = (kt * jnp.exp(b_end - b_m)).astype(BF16)
    e_end = jnp.exp(b_end)
    qt = qt.astype(BF16)
    kt = kt.astype(BF16)
    for h in range(heads):
        ks = slice(h * dk, (h + 1) * dk)
        vs = slice(h * dv, (h + 1) * dv)
        att = lax.dot_general(qt[:, ks], kt[:, ks], NT, preferred_element_type=F32)
        att = jnp.where(mask, att, 0.0).astype(BF16)
        vh = v_ref[0, :, vs].astype(BF16)
        st = st_ref[h]
        o = (jnp.dot(att, vh, preferred_element_type=F32)
             + lax.dot_general(qe[:, ks], st.astype(BF16), NT, preferred_element_type=F32))
        o_ref[0, 0, :, vs] = o
        upd = lax.dot_general(vh, kh_end[:, ks], TN, preferred_element_type=F32)
        st_ref[h] = st * e_end[:, ks] + upd


def _gla_mixer(q, k, v, r, w2, b2, n_ctx):
    bsz, lt, dkt = q.shape
    dvt = v.shape[-1]
    nr = r.shape[-1]
    t, heads = MIX_CHUNK, GLA_HEADS
    nch = lt // t
    ncc = n_ctx // t
    dk = dkt // heads
    dv = dvt // heads
    imap = lambda d, b, c: (b, _chunk_order(d, c, ncc, nch), 0)
    return pl.pallas_call(
        functools.partial(_gla_kernel, t=t, heads=heads, scale=dk ** -0.5),
        grid=(2, bsz, nch),
        in_specs=[
            pl.BlockSpec((1, t, dkt), imap),
            pl.BlockSpec((1, t, dkt), imap),
            pl.BlockSpec((1, t, dvt), imap),
            pl.BlockSpec((1, t, nr), imap),
            pl.BlockSpec((1, nr, dkt), lambda d, b, c: (d, 0, 0)),
            pl.BlockSpec((1, 1, dkt), lambda d, b, c: (d, 0, 0)),
        ],
        out_specs=pl.BlockSpec((1, 1, t, dvt), lambda d, b, c: (d, b, _chunk_order(d, c, ncc, nch), 0)),
        out_shape=jax.ShapeDtypeStruct((2, bsz, lt, dvt), F32),
        scratch_shapes=[pltpu.VMEM((heads, dv, dk), F32)],
        compiler_params=_cparams(("parallel", "parallel", "arbitrary")),
        name="gla_mixer",
    )(q, k, v, r, w2, b2)


def _mlstm_kernel(q_ref, k_ref, v_ref, gc_ref, gr_ref, o_ref, c_ref, n_ref, m_ref, *, t, heads):
    d = pl.program_id(0)
    c = pl.program_id(2)

    @pl.when(c == 0)
    def _():
        c_ref[...] = jnp.zeros_like(c_ref)
        n_ref[...] = jnp.zeros_like(n_ref)
        m_ref[...] = jnp.zeros_like(m_ref)

    dh = q_ref.shape[-1] // heads
    gc = gc_ref[0, 0]
    gr = gr_ref[0, 0, 0]
    ic = gc[:, :heads]
    fc = jax.nn.log_sigmoid(gc[:, heads:])
    ir = gr[:heads, :]
    fr = jax.nn.log_sigmoid(gr[heads:, :])
    mask = _dir_tri(d, t)
    tri = mask.astype(F32)
    b_col = jnp.dot(tri, fc, preferred_element_type=F32, precision=HI)
    b_row = lax.dot_general(fr, tri, NT, preferred_element_type=F32, precision=HI)
    b_last = jnp.where(d == 0, b_col[t - 1:t, :], b_col[0:1, :])
    q = q_ref[0]
    k = k_ref[0]
    for h in range(heads):
        hs = slice(h * dh, (h + 1) * dh)
        bc = b_col[:, h:h + 1]
        br = b_row[h:h + 1, :]
        m = m_ref[h]
        logw = jnp.where(mask, bc - br + ir[h:h + 1, :], -jnp.inf)
        log_inter = bc + m
        m_t = jnp.maximum(log_inter, jnp.max(logw, axis=-1, keepdims=True))
        w_intra = jnp.exp(logw - m_t)
        w_inter = jnp.exp(log_inter - m_t)
        qh = q[:, hs]
        kh = k[:, hs]
        vh = v_ref[0, :, hs].astype(BF16)
        cm = c_ref[h]
        nv = n_ref[h]
        scores = lax.dot_general(qh, kh, NT, preferred_element_type=F32) * w_intra
        num = (jnp.dot(scores.astype(BF16), vh, preferred_element_type=F32)
               + w_inter * jnp.dot(qh, cm.astype(BF16), preferred_element_type=F32))
        den = (jnp.sum(scores, axis=-1, keepdims=True)
               + w_inter * jnp.sum(qh.astype(F32) * nv, axis=-1, keepdims=True))
        o_ref[0, 0, :, hs] = num / jnp.maximum(jnp.abs(den), jnp.exp(-m_t))
        bl = b_last[:, h:h + 1]
        log_g = bl - bc + ic[:, h:h + 1]
        m_new = jnp.maximum(bl + m, jnp.max(log_g, axis=0, keepdims=True))
        keep = jnp.exp(bl + m - m_new)
        kw = kh.astype(F32) * jnp.exp(log_g - m_new)
        c_ref[h] = keep * cm + lax.dot_general(kw.astype(BF16), vh, TN, preferred_element_type=F32)
        n_ref[h] = keep * nv + jnp.sum(kw, axis=0, keepdims=True)
        m_ref[h] = m_new


def _mlstm_mixer(qk, v, gates, n_ctx):
    bsz, lt, w2 = qk.shape
    w = w2 // 2
    t, heads = MIX_CHUNK, MLSTM_HEADS
    dh = w // heads
    nch = lt // t
    ncc = n_ctx // t
    gc = gates.reshape(bsz, lt, 2, 2 * heads).transpose(2, 0, 1, 3)
    gr = gc.reshape(2, bsz, nch, t, 2 * heads).transpose(0, 1, 2, 4, 3)
    cmap = lambda d, b, c: _chunk_order(d, c, ncc, nch)
    return pl.pallas_call(
        functools.partial(_mlstm_kernel, t=t, heads=heads),
        grid=(2, bsz, nch),
        in_specs=[
            pl.BlockSpec((1, t, w), lambda d, b, c: (b, cmap(d, b, c), 0)),
            pl.BlockSpec((1, t, w), lambda d, b, c: (b, cmap(d, b, c), 1)),
            pl.BlockSpec((1, t, w), lambda d, b, c: (b, cmap(d, b, c), 0)),
            pl.BlockSpec((1, 1, t, 2 * heads), lambda d, b, c: (d, b, cmap(d, b, c), 0)),
            pl.BlockSpec((1, 1, 1, 2 * heads, t), lambda d, b, c: (d, b, cmap(d, b, c), 0, 0)),
        ],
        out_specs=pl.BlockSpec((1, 1, t, w), lambda d, b, c: (d, b, cmap(d, b, c), 0)),
        out_shape=jax.ShapeDtypeStruct((2, bsz, lt, w), F32),
        scratch_shapes=[pltpu.VMEM((heads, dh, dh), F32), pltpu.VMEM((heads, 1, dh), F32),
                        pltpu.VMEM((heads, 1, 1), F32)],
        compiler_params=_cparams(("parallel", "parallel", "arbitrary")),
        name="mlstm_mixer",
    )(qk, qk, v, gc, gr)


def _conv_kernel(x_ref, w_ref, b_ref, s_ref, o_ref, *, n_ctx):
    x = x_ref[0]
    lt = x.shape[0]
    row = lax.broadcasted_iota(jnp.int32, x.shape, 0)
    prev = jnp.where((row == 0) | (row == n_ctx), 0.0, pltpu.roll(x, 1, 0))
    nxt = jnp.where((row == n_ctx - 1) | (row == lt - 1), 0.0, pltpu.roll(x, lt - 1, 0))
    y = b_ref[...] + w_ref[0:1, :] * prev + w_ref[1:2, :] * x + w_ref[2:3, :] * nxt
    o_ref[0] = (y * jax.nn.sigmoid(y) * s_ref[...]).astype(o_ref.dtype)


def _conv_silu(x, w, b, colscale, n_ctx):
    bsz, lt, ch = x.shape
    tc = 256
    return pl.pallas_call(
        functools.partial(_conv_kernel, n_ctx=n_ctx),
        grid=(bsz, ch // tc),
        in_specs=[
            pl.BlockSpec((1, lt, tc), lambda b, j: (b, 0, j)),
            pl.BlockSpec((3, tc), lambda b, j: (0, j)),
            pl.BlockSpec((1, tc), lambda b, j: (0, j)),
            pl.BlockSpec((1, tc), lambda b, j: (0, j)),
        ],
        out_specs=pl.BlockSpec((1, lt, tc), lambda b, j: (b, 0, j)),
        out_shape=jax.ShapeDtypeStruct((bsz, lt, ch), BF16),
        compiler_params=_cparams(("parallel", "parallel")),
        name="conv_silu",
    )(x, w, b.reshape(1, ch), colscale.reshape(1, ch))


def _s5_matrices(a_re, a_im, log_dt, b_re, b_im, c_re, c_im, backward, lane_groups=8):
    g, p = a_re.shape
    cg = b_re.shape[-1]
    j = S5_J
    lg = lane_groups
    nq = g // lg
    dt = jnp.exp(log_dt)[:, None]
    lam_re = jnp.minimum(a_re, -1e-4)
    lam_im = a_im
    decay = jnp.exp(lam_re * dt)
    ab_re = decay * jnp.cos(lam_im * dt)
    ab_im = decay * jnp.sin(lam_im * dt)
    den = lam_re * lam_re + lam_im * lam_im
    zr = ((ab_re - 1) * lam_re + ab_im * lam_im) / den
    zi = (ab_im * lam_re - (ab_re - 1) * lam_im) / den
    bb_re = zr[..., None] * b_re - zi[..., None] * b_im
    bb_im = zr[..., None] * b_im + zi[..., None] * b_re
    pw_re, pw_im = [jnp.ones_like(ab_re)], [jnp.zeros_like(ab_im)]
    for _ in range(j):
        r0, i0 = pw_re[-1], pw_im[-1]
        pw_re.append(ab_re * r0 - ab_im * i0)
        pw_im.append(ab_re * i0 + ab_im * r0)
    pw_re, pw_im = jnp.stack(pw_re), jnp.stack(pw_im)
    ca_re = c_re[None] * pw_re[:, :, None, :] - c_im[None] * pw_im[:, :, None, :]
    ca_im = c_re[None] * pw_im[:, :, None, :] + c_im[None] * pw_re[:, :, None, :]
    kk = (jnp.einsum('tgcp,gpd->tgcd', ca_re[:j], bb_re, precision=HI)
          - jnp.einsum('tgcp,gpd->tgcd', ca_im[:j], bb_im, precision=HI))
    ab_pw_re = pw_re[:j, :, :, None] * bb_re[None] - pw_im[:j, :, :, None] * bb_im[None]
    ab_pw_im = pw_re[:j, :, :, None] * bb_im[None] + pw_im[:j, :, :, None] * bb_re[None]
    jj = jnp.arange(j)
    lag = (jj[:, None] - jj[None, :]) if backward else (jj[None, :] - jj[:, None])
    kt = jnp.where((lag >= 0)[:, :, None, None, None], kk[jnp.clip(lag, 0, j - 1)], 0.0)
    eye = jnp.eye(lg, dtype=F32)
    kt = kt.reshape(j, j, nq, lg, cg, cg)
    ktoep = jnp.einsum('ioqgcd,gh->qigdohc', kt, eye).reshape(nq, j * lg * cg, j * lg * cg)
    tau_in = jj if backward else (j - 1 - jj)
    wi_re = ab_pw_re[tau_in].reshape(j, nq, lg, p, cg)
    wi_im = ab_pw_im[tau_in].reshape(j, nq, lg, p, cg)
    win_re = jnp.einsum('jqgpc,gh->qjgchp', wi_re, eye).reshape(nq, j * lg * cg, lg * p)
    win_im = jnp.einsum('jqgpc,gh->qjgchp', wi_im, eye).reshape(nq, j * lg * cg, lg * p)
    tau_out = (j - jj) if backward else (jj + 1)
    wo_re = ca_re[tau_out].reshape(j, nq, lg, cg, p)
    wo_im = ca_im[tau_out].reshape(j, nq, lg, cg, p)
    wout_re = jnp.einsum('jqgcp,gh->qgpjhc', wo_re, eye).reshape(nq, lg * p, j * lg * cg)
    wout_im = -jnp.einsum('jqgcp,gh->qgpjhc', wo_im, eye).reshape(nq, lg * p, j * lg * cg)
    dec_re = pw_re[j].reshape(nq, 1, lg * p)
    dec_im = pw_im[j].reshape(nq, 1, lg * p)
    return (ktoep.astype(BF16), win_re.astype(BF16), win_im.astype(BF16),
            wout_re.astype(BF16), wout_im.astype(BF16), dec_re, dec_im)


def _s5_kernel(u_ref, kt_ref, wir_ref, wii_ref, wor_ref, woi_ref, dr_ref, di_ref, y_ref,
               xf_ref, yf_ref, sre_ref, sim_ref, *, bt, nk, nk_ctx, rs):
    d = pl.program_id(0)
    j = S5_J
    lanes = u_ref.shape[-1]
    for b in range(bt):
        for jj in range(j):
            xf_ref[b * nk:(b + 1) * nk, jj * lanes:(jj + 1) * lanes] = (
                u_ref.at[b][pl.ds(jj, nk, stride=j), :].astype(BF16))
    xf = xf_ref[...]
    yf_ref[...] = jnp.dot(xf, kt_ref[0, 0], preferred_element_type=F32)
    inc_re = jnp.dot(xf, wir_ref[0, 0], preferred_element_type=F32)
    inc_im = jnp.dot(xf, wii_ref[0, 0], preferred_element_type=F32)
    nl = sre_ref.shape[0]
    for b in range(bt):
        for l in range(nl):
            sre_ref[l, b * rs:b * rs + nk, :] = inc_re[b * nk:(b + 1) * nk, l * lanes:(l + 1) * lanes]
            sim_ref[l, b * rs:b * rs + nk, :] = inc_im[b * nk:(b + 1) * nk, l * lanes:(l + 1) * lanes]
    a_re = [dr_ref[0, 0, :, l * lanes:(l + 1) * lanes] for l in range(nl)]
    a_im = [di_ref[0, 0, :, l * lanes:(l + 1) * lanes] for l in range(nl)]

    def step(kidx, carry):
        rows = pl.ds(kidx, bt, stride=rs)
        out = []
        for l in range(nl):
            s_re, s_im = carry[2 * l], carry[2 * l + 1]
            i_re = sre_ref.at[l][rows, :]
            i_im = sim_ref.at[l][rows, :]
            sre_ref.at[l][rows, :] = s_re
            sim_ref.at[l][rows, :] = s_im
            out.append(a_re[l] * s_re - a_im[l] * s_im + i_re)
            out.append(a_re[l] * s_im + a_im[l] * s_re + i_im)
        return tuple(out)

    zero = tuple(jnp.zeros((bt, lanes), F32) for _ in range(2 * nl))

    @pl.when(d == 0)
    def _():
        lax.fori_loop(0, nk, step, zero)

    @pl.when(d == 1)
    def _():
        carry = lax.fori_loop(0, nk_ctx, lambda i, cr: step(nk_ctx - 1 - i, cr), zero)
        lax.fori_loop(0, nk - nk_ctx, lambda i, cr: step(nk - 1 - i, cr), carry)

    for b in range(bt):
        sp_re = jnp.concatenate([sre_ref[l, b * rs:b * rs + nk, :] for l in range(nl)], axis=-1).astype(BF16)
        sp_im = jnp.concatenate([sim_ref[l, b * rs:b * rs + nk, :] for l in range(nl)], axis=-1).astype(BF16)
        yb = (yf_ref[b * nk:(b + 1) * nk, :]
              + jnp.dot(sp_re, wor_ref[0, 0], preferred_element_type=F32)
              + jnp.dot(sp_im, woi_ref[0, 0], preferred_element_type=F32))
        for jj in range(j):
            y_ref.at[0, b][pl.ds(jj, nk, stride=j), :] = yb[:, jj * lanes:(jj + 1) * lanes]


def _s5_mixer(u, mats, n_ctx):
    bsz, lt, w = u.shape
    ktoep, win_re, win_im, wout_re, wout_im, dec_re, dec_im = mats
    lanes = 128
    bt = 4 if bsz % 4 == 0 else 2
    nq = w // lanes
    j = S5_J
    nk = lt // j
    nk_ctx = n_ctx // j
    rs = nk + 8
    fl = j * lanes
    sw = win_re.shape[-1]
    wmap = lambda d, q, b: (d, q, 0, 0)
    return pl.pallas_call(
        functools.partial(_s5_kernel, bt=bt, nk=nk, nk_ctx=nk_ctx, rs=rs),
        grid=(2, nq, bsz // bt),
        in_specs=[
            pl.BlockSpec((bt, lt, lanes), lambda d, q, b: (b, 0, q)),
            pl.BlockSpec((1, 1, fl, fl), wmap),
            pl.BlockSpec((1, 1, fl, sw), wmap),
            pl.BlockSpec((1, 1, fl, sw), wmap),
            pl.BlockSpec((1, 1, sw, fl), wmap),
            pl.BlockSpec((1, 1, sw, fl), wmap),
            pl.BlockSpec((1, 1, 1, sw), wmap),
            pl.BlockSpec((1, 1, 1, sw), wmap),
        ],
        out_specs=pl.BlockSpec((1, bt, lt, lanes), lambda d, q, b: (d, b, 0, q)),
        out_shape=jax.ShapeDtypeStruct((2, bsz, lt, w), F32),
        scratch_shapes=[pltpu.VMEM((bt * nk, fl), BF16), pltpu.VMEM((bt * nk, fl), F32),
                        pltpu.VMEM((sw // lanes, bt * rs, lanes), F32),
                        pltpu.VMEM((sw // lanes, bt * rs, lanes), F32)],
        compiler_params=_cparams(("parallel", "parallel", "arbitrary")),
        name="s5_mixer",
    )(u, ktoep, win_re, win_im, wout_re, wout_im, dec_re, dec_im)


def _head_norm(x, heads):
    dh = x.shape[-1] // heads
    outs = []
    for h in range(heads):
        xh = x[:, h * dh:(h + 1) * dh]
        outs.append(xh * lax.rsqrt(jnp.mean(xh * xh, axis=-1, keepdims=True) + EPS))
    return jnp.concatenate(outs, axis=-1)


def _even_post_kernel(m_ref, o_ref, s_ref, u_ref, mg_ref, dsk_ref, gw_ref, gb_ref, w_ref, h_ref, gate_ref,
                      out_ref, *, heads):
    m = m_ref[0, 0] + m_ref[1, 0]
    m_out = _head_norm(m, heads) * mg_ref[...] * jax.nn.sigmoid(o_ref[0])
    y = jax.nn.gelu(s_ref[0, 0] + s_ref[1, 0] + dsk_ref[...] * u_ref[0])
    glu = jnp.dot(y.astype(BF16), gw_ref[...], preferred_element_type=F32) + gb_ref[...]
    s_out = y * jax.nn.sigmoid(glu)
    cat = jnp.concatenate([m_out, s_out], axis=-1).astype(BF16)
    z = jnp.dot(cat, w_ref[...], preferred_element_type=F32)
    out_ref[0] = h_ref[0] + gate_ref[0, 0] * z


def _even_post(m2, o, s2, u, mnorm_g, d_skip, glu_w, glu_b, w_out, h, gate):
    bsz, lt, d = h.shape
    mw = o.shape[-1]
    sw = u.shape[-1]
    tm = ROW_TILE
    row = lambda b, i: (b, i, 0)
    row2 = lambda b, i: (0, b, i, 0)
    const = lambda b, i: (0, 0)
    return pl.pallas_call(
        functools.partial(_even_post_kernel, heads=MLSTM_HEADS),
        grid=(bsz, lt // tm),
        in_specs=[
            pl.BlockSpec((2, 1, tm, mw), row2),
            pl.BlockSpec((1, tm, mw), row),
            pl.BlockSpec((2, 1, tm, sw), row2),
            pl.BlockSpec((1, tm, sw), row),
            pl.BlockSpec((1, mw), const),
            pl.BlockSpec((1, sw), const),
            pl.BlockSpec((sw, sw), const),
            pl.BlockSpec((1, sw), const),
            pl.BlockSpec((mw + sw, d), const),
            pl.BlockSpec((1, tm, d), row),
            pl.BlockSpec((1, 1, 1, d), _seg_map),
        ],
        out_specs=pl.BlockSpec((1, tm, d), row),
        out_shape=jax.ShapeDtypeStruct((bsz, lt, d), F32),
        compiler_params=_cparams(("parallel", "parallel")),
        name="even_post",
    )(m2, o, s2, u, mnorm_g.reshape(1, mw), d_skip.reshape(1, sw), glu_w.astype(BF16), glu_b.reshape(1, sw),
      w_out.astype(BF16), h, gate)


def _odd_post_kernel(o_ref, g_ref, ng_ref, w_ref, h_ref, gate_ref, out_ref, *, heads):
    g = g_ref[0]
    y = _head_norm(o_ref[0], heads) * ng_ref[...] * (g * jax.nn.sigmoid(g))
    z = jnp.dot(y.astype(BF16), w_ref[...], preferred_element_type=F32)
    out_ref[0] = h_ref[0] + gate_ref[0, 0] * z


def _odd_post(o, g, norm_g, w_out, h, gate):
    bsz, lt, d = h.shape
    dv = o.shape[-1]
    tm = ROW_TILE
    row = lambda b, i: (b, i, 0)
    const = lambda b, i: (0, 0)
    return pl.pallas_call(
        functools.partial(_odd_post_kernel, heads=GLA_HEADS),
        grid=(bsz, lt // tm),
        in_specs=[
            pl.BlockSpec((1, tm, dv), row),
            pl.BlockSpec((1, tm, dv), row),
            pl.BlockSpec((1, dv), const),
            pl.BlockSpec((dv, d), const),
            pl.BlockSpec((1, tm, d), row),
            pl.BlockSpec((1, 1, 1, d), _seg_map),
        ],
        out_specs=pl.BlockSpec((1, tm, d), row),
        out_shape=jax.ShapeDtypeStruct((bsz, lt, d), F32),
        compiler_params=_cparams(("parallel", "parallel")),
        name="odd_post",
    )(o, g, norm_g.reshape(1, dv), w_out.astype(BF16), h, gate)


def kernel(x, c, ctx, c_ctx, mod_w, mod_b, norm_mix_g, norm_ffn_g, ev_w_in, ev_b_in, ev_conv_w, ev_conv_b, ev_mlstm_norm_g, ev_s5_a_re_f, ev_s5_a_im_f, ev_s5_log_dt_f, ev_s5_a_re_b, ev_s5_a_im_b, ev_s5_log_dt_b, ev_s5_b_re, ev_s5_b_im, ev_s5_c_re, ev_s5_c_im, ev_s5_d, ev_s5_glu_w, ev_s5_glu_b, ev_w_out, od_w_in, od_gate_w2_f, od_gate_b2_f, od_gate_w2_b, od_gate_b2_b, od_norm_g, od_w_out, router_w, router_b, moe_w_gu, moe_b_gu, moe_w_down, moe_b_down, final_norm_g):
    bsz, seq, d = x.shape
    n_ctx = ctx.shape[1]
    depth = mod_w.shape[0]
    lt = n_ctx + seq
    assert n_ctx == ROW_TILE and seq % ROW_TILE == 0 and seq % GRID_W == 0

    h = jnp.concatenate([ctx, x], axis=1)
    c_all = jnp.concatenate([c, c_ctx[None, :]], axis=0)
    c_all = jnp.pad(c_all, ((0, (-c_all.shape[0]) % 8), (0, 0)))
    mods = _modulation(c_all, mod_w, mod_b)
    mod_lat = mods[:, :bsz]
    mod_ctx = jnp.broadcast_to(mods[:, bsz:bsz + 1], mod_lat.shape)
    mod6 = jnp.stack([mod_ctx, mod_lat], axis=2).reshape(depth, bsz, 2, 6, 1, d)

    wg_all = moe_w_gu[..., 0::2].astype(BF16)
    wu_all = moe_w_gu[..., 1::2].astype(BF16)
    wd_all = moe_w_down.astype(BF16)
    bg_all = moe_b_gu[..., 0::2]
    bu_all = moe_b_gu[..., 1::2]

    mw = ev_conv_w.shape[-1] // 2
    n_gates = 4 * MLSTM_HEADS
    s5w = ev_s5_d.shape[-1]
    dk_t = od_gate_w2_f.shape[-1]
    dv_t = od_norm_g.shape[-1]
    rows = seq // GRID_W

    def to_cols(a):
        lat = a[:, n_ctx:].reshape(bsz, rows, GRID_W, -1).transpose(0, 2, 1, 3).reshape(bsz, seq, -1)
        return jnp.concatenate([a[:, :n_ctx], lat], axis=1)

    def to_rows(a):
        lat = a[:, n_ctx:].reshape(bsz, GRID_W, rows, -1).transpose(0, 2, 1, 3).reshape(bsz, seq, -1)
        return jnp.concatenate([a[:, :n_ctx], lat], axis=1)

    for layer in range(depth):
        last = layer == depth - 1
        j = layer // 2
        m6 = mod6[layer]
        sh1, sc1, g1, sh2, sc2, g2 = (m6[:, :, i] for i in range(6))
        if layer % 2 == 0:
            w_in, b_in = ev_w_in[j], ev_b_in[j]
            cols = jnp.concatenate([jnp.arange(0, 4 * mw), jnp.arange(4 * mw + n_gates, 4 * mw + n_gates + s5w),
                                    jnp.arange(4 * mw, 4 * mw + n_gates)])
            qk_pre, v, o, u, gates = _nm_matmul(h, norm_mix_g[layer], sh1, sc1, w_in[:, cols], b_in[cols],
                                                (2 * mw, mw, mw, s5w, n_gates))
            dh = mw // MLSTM_HEADS
            colscale = jnp.concatenate([jnp.full((mw,), dh ** -0.5, F32), jnp.ones((mw,), F32)])
            qk = _conv_silu(qk_pre, ev_conv_w[j], ev_conv_b[j], colscale, n_ctx)
            m2 = _mlstm_mixer(qk, v, gates, n_ctx)
            shared = (ev_s5_b_re[j], ev_s5_b_im[j], ev_s5_c_re[j], ev_s5_c_im[j])
            mats_f = _s5_matrices(ev_s5_a_re_f[j], ev_s5_a_im_f[j], ev_s5_log_dt_f[j], *shared, backward=False)
            mats_b = _s5_matrices(ev_s5_a_re_b[j], ev_s5_a_im_b[j], ev_s5_log_dt_b[j], *shared, backward=True)
            s2 = _s5_mixer(u, tuple(jnp.stack([a, b]) for a, b in zip(mats_f, mats_b)), n_ctx)
            h = _even_post(m2, o, s2, u, ev_mlstm_norm_g[j], ev_s5_d[j], ev_s5_glu_w[j], ev_s5_glu_b[j],
                           ev_w_out[j], h, g1)
        else:
            qq, kk, vv, gg, rr = _nm_matmul(h, norm_mix_g[layer], sh1, sc1, od_w_in[j],
                                            jnp.zeros((od_w_in.shape[-1],), F32),
                                            (dk_t, dk_t, dv_t, dv_t, 2 * GLA_RANK))
            zero = jnp.zeros_like(od_gate_w2_f[j])
            w2 = jnp.stack([jnp.concatenate([od_gate_w2_f[j], zero], axis=0),
                            jnp.concatenate([zero, od_gate_w2_b[j]], axis=0)])
            b2 = jnp.stack([od_gate_b2_f[j], od_gate_b2_b[j]])[:, None, :]
            o2 = _gla_mixer(to_cols(qq), to_cols(kk), to_cols(vv), to_cols(rr), w2, b2, n_ctx)
            h = _odd_post(to_rows(o2[0] + o2[1]), gg, od_norm_g[j], od_w_out[j], h, g1)
        f, logits = _ffn_prep(h, norm_ffn_g[layer], sh2, sc2, router_w[layer], router_b[layer])
        weights = (wg_all[layer], wu_all[layer], wd_all[layer], bg_all[layer], bu_all[layer], moe_b_down[layer])
        h = _moe_layer(h, f, logits, g2, weights, last, n_ctx)
    return _final_norm(h, final_norm_g)
```

```python
import functools

import jax
import jax.numpy as jnp
from jax import lax
from jax.experimental import pallas as pl
from jax.experimental.pallas import tpu as pltpu

F32 = jnp.float32
BF16 = jnp.bfloat16
HI = lax.Precision.HIGHEST

EPS = 1e-6
GRID_W = 64
MLSTM_HEADS = 4
S5_GROUP = 16
GLA_HEADS = 4
GLA_RANK = 16
GLA_TAU = 16.0
N_EXPERTS = 32
TOP_K = 4
SWIGLU_LIMIT = 7.0
SWIGLU_ALPHA = 1.702

ROW_TILE = 256
MOE_TILE = 512
MIX_CHUNK = 64
MIX_BATCH = 4
S5_J = 8
VMEM_LIMIT = 56 * 1024 * 1024

NT = (((1,), (1,)), ((), ()))
TN = (((0,), (0,)), ((), ()))


def _cparams(sem):
    return pltpu.CompilerParams(dimension_semantics=sem, vmem_limit_bytes=VMEM_LIMIT)


def _mod_kernel(c_ref, w_ref, b_ref, o_ref):
    c = c_ref[...]
    a = c * jax.nn.sigmoid(c)
    o_ref[0] = jnp.dot(a.astype(BF16), w_ref[0].astype(BF16), preferred_element_type=F32) + b_ref[0]


def _modulation(c_all, mod_w, mod_b):
    depth, d, n6 = mod_w.shape
    rows = c_all.shape[0]
    tn = d
    return pl.pallas_call(
        _mod_kernel,
        grid=(depth, n6 // tn),
        in_specs=[
            pl.BlockSpec((rows, d), lambda l, j: (0, 0)),
            pl.BlockSpec((1, d, tn), lambda l, j: (l, 0, j)),
            pl.BlockSpec((1, 1, tn), lambda l, j: (l, 0, j)),
        ],
        out_specs=pl.BlockSpec((1, rows, tn), lambda l, j: (l, 0, j)),
        out_shape=jax.ShapeDtypeStruct((depth, rows, n6), F32),
        compiler_params=_cparams(("arbitrary", "arbitrary")),
        name="modulation",
    )(c_all, mod_w, mod_b.reshape(depth, 1, n6))


def _norm_mod(x, g, sh, sc):
    ms = jnp.mean(x * x, axis=-1, keepdims=True)
    return (x * lax.rsqrt(ms + EPS) * g) * (1.0 + sc) + sh


def _nm_matmul_kernel(x_ref, g_ref, sh_ref, sc_ref, w_ref, b_ref, *out_refs, splits):
    a = _norm_mod(x_ref[0], g_ref[...], sh_ref[0, 0], sc_ref[0, 0])
    z = jnp.dot(a.astype(BF16), w_ref[...], preferred_element_type=F32) + b_ref[...]
    for (lo, hi), o_ref in zip(splits, out_refs):
        o_ref[0] = z[:, lo:hi].astype(o_ref.dtype)


def _seg_map(b, i):
    return (b, jnp.minimum(i, 1), 0, 0)


def _nm_matmul(h, g, shift, scale, w, bias, widths):
    bsz, lt, d = h.shape
    p = w.shape[1]
    splits, lo = [], 0
    for wd in widths:
        splits.append((lo, lo + wd))
        lo += wd
    assert lo == p
    tm = ROW_TILE
    return pl.pallas_call(
        functools.partial(_nm_matmul_kernel, splits=tuple(splits)),
        grid=(bsz, lt // tm),
        in_specs=[
            pl.BlockSpec((1, tm, d), lambda b, i: (b, i, 0)),
            pl.BlockSpec((1, d), lambda b, i: (0, 0)),
            pl.BlockSpec((1, 1, 1, d), _seg_map),
            pl.BlockSpec((1, 1, 1, d), _seg_map),
            pl.BlockSpec((d, p), lambda b, i: (0, 0)),
            pl.BlockSpec((1, p), lambda b, i: (0, 0)),
        ],
        out_specs=[pl.BlockSpec((1, tm, wd), lambda b, i: (b, i, 0)) for wd in widths],
        out_shape=[jax.ShapeDtypeStruct((bsz, lt, wd), F32) for wd in widths],
        compiler_params=_cparams(("parallel", "parallel")),
        name="norm_mod_matmul",
    )(h, g.reshape(1, d), shift, scale, w.astype(BF16), bias.reshape(1, p))


def _ffn_prep_kernel(x_ref, g_ref, sh_ref, sc_ref, rw_ref, rb_ref, f_ref, logit_ref):
    a = _norm_mod(x_ref[0], g_ref[...], sh_ref[0, 0], sc_ref[0, 0])
    f_ref[0] = a.astype(f_ref.dtype)
    logit_ref[0] = jnp.dot(a, rw_ref[...], preferred_element_type=F32, precision=HI) + rb_ref[...]


def _ffn_prep(h, g, shift, scale, router_w, router_b):
    bsz, lt, d = h.shape
    ne = router_w.shape[1]
    tm = ROW_TILE
    return pl.pallas_call(
        _ffn_prep_kernel,
        grid=(bsz, lt // tm),
        in_specs=[
            pl.BlockSpec((1, tm, d), lambda b, i: (b, i, 0)),
            pl.BlockSpec((1, d), lambda b, i: (0, 0)),
            pl.BlockSpec((1, 1, 1, d), _seg_map),
            pl.BlockSpec((1, 1, 1, d), _seg_map),
            pl.BlockSpec((d, ne), lambda b, i: (0, 0)),
            pl.BlockSpec((1, ne), lambda b, i: (0, 0)),
        ],
        out_specs=[
            pl.BlockSpec((1, tm, d), lambda b, i: (b, i, 0)),
            pl.BlockSpec((1, tm, ne), lambda b, i: (b, i, 0)),
        ],
        out_shape=[
            jax.ShapeDtypeStruct((bsz, lt, d), BF16),
            jax.ShapeDtypeStruct((bsz, lt, ne), F32),
        ],
        compiler_params=_cparams(("parallel", "parallel")),
        name="ffn_prep",
    )(h, g.reshape(1, d), shift, scale, router_w, router_b.reshape(1, ne))


GU_BLOCK = 256


def _moe_kernel(be_ref, nb_ref, x_ref, wgu_ref, wd_ref, bg_ref, bu_ref, bd_ref, o_ref, wgu_s, wd_s):
    i = pl.program_id(0)
    active = i < nb_ref[0]
    half = GU_BLOCK // 2
    nblk = wgu_s.shape[1] // GU_BLOCK

    @pl.when(active & ((i == 0) | (be_ref[i] != be_ref[jnp.maximum(i - 1, 0)])))
    def _():
        r = lax.broadcasted_iota(jnp.int32, (GU_BLOCK, GU_BLOCK), 0)
        c = lax.broadcasted_iota(jnp.int32, (GU_BLOCK, GU_BLOCK), 1)
        perm = (r == jnp.where(c < half, 2 * c, 2 * (c - half) + 1)).astype(BF16)
        for k in range(nblk):
            cs = slice(k * GU_BLOCK, (k + 1) * GU_BLOCK)
            wgu_s[:, cs] = jnp.dot(wgu_ref[0, 0, :, cs].astype(BF16), perm,
                                   preferred_element_type=F32).astype(BF16)
        wd_s[...] = wd_ref[0, 0].astype(BF16)

    @pl.when(active)
    def _():
        gu = jnp.dot(x_ref[...], wgu_s[...], preferred_element_type=F32)
        hdn = []
        for k in range(nblk):
            hs = slice(k * half, (k + 1) * half)
            g = gu[:, k * GU_BLOCK:k * GU_BLOCK + half] + bg_ref[0, :, hs]
            u = gu[:, k * GU_BLOCK + half:(k + 1) * GU_BLOCK] + bu_ref[0, :, hs]
            g = jnp.minimum(g, SWIGLU_LIMIT)
            u = jnp.clip(u, -SWIGLU_LIMIT, SWIGLU_LIMIT)
            hdn.append(((u + 1.0) * (g * jax.nn.sigmoid(SWIGLU_ALPHA * g))).astype(BF16))
        hdn = jnp.concatenate(hdn, axis=-1)
        o_ref[...] = jnp.dot(hdn, wd_s[...], preferred_element_type=F32) + bd_ref[0]

    @pl.when(jnp.logical_not(active))
    def _():
        o_ref[...] = jnp.zeros_like(o_ref)


def _moe_experts(x_sorted, block_expert, n_used, layer, w_gu, w_down, bg, bu, bd):
    n_rows, d = x_sorted.shape
    _, ne, _, f2 = w_gu.shape
    f = f2 // 2
    tm = MOE_TILE
    n_blocks = n_rows // tm
    assert f2 % GU_BLOCK == 0
    grid_spec = pltpu.PrefetchScalarGridSpec(
        num_scalar_prefetch=2,
        grid=(n_blocks,),
        in_specs=[
            pl.BlockSpec((tm, d), lambda i, be, nb: (i, 0)),
            pl.BlockSpec((1, 1, d, f2), lambda i, be, nb: (layer, be[i], 0, 0)),
            pl.BlockSpec((1, 1, f, d), lambda i, be, nb: (layer, be[i], 0, 0)),
            pl.BlockSpec((1, 1, f), lambda i, be, nb: (be[i], 0, 0)),
            pl.BlockSpec((1, 1, f), lambda i, be, nb: (be[i], 0, 0)),
            pl.BlockSpec((1, 1, d), lambda i, be, nb: (be[i], 0, 0)),
        ],
        out_specs=pl.BlockSpec((tm, d), lambda i, be, nb: (i, 0)),
        scratch_shapes=[pltpu.VMEM((d, f2), BF16), pltpu.VMEM((f, d), BF16)],
    )
    return pl.pallas_call(
        _moe_kernel,
        grid_spec=grid_spec,
        out_shape=jax.ShapeDtypeStruct((n_rows, d), F32),
        compiler_params=_cparams(("arbitrary",)),
        name="moe_experts",
    )(block_expert, n_used, x_sorted, w_gu, w_down,
      bg.reshape(ne, 1, f), bu.reshape(ne, 1, f), bd.reshape(ne, 1, d))


def _combine_kernel(y_ref, gt_ref, h_ref, g2_ref, o_ref):
    gt = gt_ref[0]
    acc = y_ref[0, 0] * gt[:, 0:1]
    for k in range(1, TOP_K):
        acc = acc + y_ref[k, 0] * gt[:, k:k + 1]
    o_ref[0] = h_ref[0] + g2_ref[0, 0] * acc


def _moe_combine(yg, gate, h, g2, seg_map):
    k, bsz, lt, d = yg.shape
    tm = ROW_TILE
    return pl.pallas_call(
        _combine_kernel,
        grid=(bsz, lt // tm),
        in_specs=[
            pl.BlockSpec((k, 1, tm, d), lambda b, i: (0, b, i, 0)),
            pl.BlockSpec((1, tm, k), lambda b, i: (b, i, 0)),
            pl.BlockSpec((1, tm, d), lambda b, i: (b, i, 0)),
            pl.BlockSpec((1, 1, 1, d), seg_map),
        ],
        out_specs=pl.BlockSpec((1, tm, d), lambda b, i: (b, i, 0)),
        out_shape=jax.ShapeDtypeStruct((bsz, lt, d), F32),
        compiler_params=_cparams(("parallel", "parallel")),
        name="moe_combine",
    )(yg, gate, h, g2)


def _lat_seg_map(b, i):
    return (b, 1, 0, 0)


def _moe_layer(h, f, logits, g2, weights, lat_only, n_ctx):
    layer, w_gu, w_down, bg, bu, bd = weights
    bsz, lt, d = h.shape
    if lat_only:
        h_in, f, logits = h[:, n_ctx:], f[:, n_ctx:], logits[:, n_ctx:]
    else:
        h_in = h
    ltok = h_in.shape[1]
    n = bsz * ltok
    top_logit, top_e = lax.top_k(logits.reshape(n, -1), TOP_K)
    gate = jax.nn.softmax(top_logit, axis=-1)
    n_assign = n * TOP_K
    tm = MOE_TILE
    flat_e = top_e.reshape(-1).astype(jnp.int32)
    iota = jnp.arange(n_assign, dtype=jnp.int32)
    e_sorted, order = lax.sort_key_val(flat_e, iota, is_stable=True)
    experts = jnp.arange(N_EXPERTS, dtype=jnp.int32)
    start = jnp.searchsorted(e_sorted, experts, side='left').astype(jnp.int32)
    counts = jnp.searchsorted(e_sorted, experts, side='right').astype(jnp.int32) - start
    padded = (counts + tm - 1) // tm * tm
    pad_end = jnp.cumsum(padded)
    pad_start = pad_end - padded
    dest = pad_start[e_sorted] + iota - start[e_sorted]
    n_blocks = -(-(n_assign + N_EXPERTS * (tm - 1)) // tm)
    n_rows = n_blocks * tm
    block_expert = jnp.minimum(
        jnp.searchsorted(pad_end, jnp.arange(n_blocks, dtype=jnp.int32) * tm, side='right'),
        N_EXPERTS - 1).astype(jnp.int32)
    n_used = (pad_end[-1] // tm).astype(jnp.int32).reshape(1)
    row = jnp.arange(n_rows, dtype=jnp.int32)
    row_e = jnp.repeat(block_expert, tm)
    within = row - pad_start[row_e]
    slot = jnp.clip(start[row_e] + within, 0, n_assign - 1)
    row_token = jnp.where(within < counts[row_e], order[slot] // TOP_K, 0)
    _, pos = lax.sort_key_val(order, dest, is_stable=True)
    x_sorted = jnp.take(f.reshape(n, d), row_token, axis=0)
    y = _moe_experts(x_sorted, block_expert, n_used, layer, w_gu, w_down, bg, bu, bd)
    pos_k = pos.reshape(n, TOP_K).T
    yg = jnp.take(y, pos_k.reshape(-1), axis=0).reshape(TOP_K, bsz, ltok, d)
    out = _moe_combine(yg, gate.reshape(bsz, ltok, TOP_K), h_in, g2,
                       _lat_seg_map if lat_only else _seg_map)
    return out


def _rmsnorm_kernel(x_ref, g_ref, o_ref):
    x = x_ref[0]
    ms = jnp.mean(x * x, axis=-1, keepdims=True)
    o_ref[0] = x * lax.rsqrt(ms + EPS) * g_ref[...]


def _final_norm(h, g):
    bsz, lt, d = h.shape
    tm = ROW_TILE
    return pl.pallas_call(
        _rmsnorm_kernel,
        grid=(bsz, lt // tm),
        in_specs=[pl.BlockSpec((1, tm, d), lambda b, i: (b, i, 0)),
                  pl.BlockSpec((1, d), lambda b, i: (0, 0))],
        out_specs=pl.BlockSpec((1, tm, d), lambda b, i: (b, i, 0)),
        out_shape=jax.ShapeDtypeStruct((bsz, lt, d), F32),
        compiler_params=_cparams(("parallel", "parallel")),
        name="final_norm",
    )(h, g.reshape(1, d))


def _chunk_order(d, c, n_ctx_chunks, n_chunks):
    bwd = jnp.where(c < n_ctx_chunks, n_ctx_chunks - 1 - c, n_chunks + n_ctx_chunks - 1 - c)
    return jnp.where(d == 0, c, bwd)


def _dir_tri(d, t):
    row = lax.broadcasted_iota(jnp.int32, (t, t), 0)
    col = lax.broadcasted_iota(jnp.int32, (t, t), 1)
    return jnp.where(d == 0, col - row, row - col) <= 0


def _gla_kernel(q_ref, k_ref, v_ref, r_ref, w2_ref, b2_ref, o_ref, st_ref, *, t, heads, scale):
    d = pl.program_id(0)
    c = pl.program_id(2)

    @pl.when(c == 0)
    def _():
        st_ref[...] = jnp.zeros_like(st_ref)

    dk = q_ref.shape[-1] // heads
    dv = v_ref.shape[-1] // heads
    nb = q_ref.shape[0]
    mask = _dir_tri(d, t)
    tri = mask.astype(F32)
    mid = t // 2
    items = [(bb, h) for bb in range(nb) for h in range(heads)]
    xs = [jnp.dot(r_ref[bb], w2_ref[0], preferred_element_type=F32, precision=HI) + b2_ref[0] for bb in range(nb)]
    las = [jax.nn.log_sigmoid(x) * (1.0 / GLA_TAU) for x in xs]
    bs = [jnp.dot(tri, la, preferred_element_type=F32, precision=HI) for la in las]
    qt, kt, qe, kh_end, e_end = [], [], [], [], []
    for bb in range(nb):
        b = bs[bb]
        b_m = b[mid:mid + 1, :]
        b_end = jnp.where(d == 0, b[t - 1:t, :], b[0:1, :])
        q_s = q_ref[bb] * (jnp.exp(b - b_m) * scale)
        k_s = k_ref[bb] * jnp.exp(b_m - b)
        qe.append((q_s * jnp.exp(b_m)).astype(BF16))
        kh_end.append((k_s * jnp.exp(b_end - b_m)).astype(BF16))
        e_end.append(jnp.exp(b_end))
        qt.append(q_s.astype(BF16))
        kt.append(k_s.astype(BF16))
    att, q_st, vs = {}, {}, {}
    for bb, h in items:
        ks = slice(h * dk, (h + 1) * dk)
        vs[bb, h] = v_ref[bb, :, h * dv:(h + 1) * dv].astype(BF16)
        att[bb, h] = lax.dot_general(qt[bb][:, ks], kt[bb][:, ks], NT, preferred_element_type=F32)
        q_st[bb, h] = lax.dot_general(qe[bb][:, ks], st_ref[bb * heads + h].astype(BF16), NT,
                                      preferred_element_type=F32)
    for bb, h in items:
        a = jnp.where(mask, att[bb, h], 0.0).astype(BF16)
        o_ref[0, bb, :, h * dv:(h + 1) * dv] = jnp.dot(a, vs[bb, h], preferred_element_type=F32) + q_st[bb, h]
    for bb, h in items:
        ks = slice(h * dk, (h + 1) * dk)
        upd = lax.dot_general(vs[bb, h], kh_end[bb][:, ks], TN, preferred_element_type=F32)
        st_ref[bb * heads + h] = st_ref[bb * heads + h] * e_end[bb][:, ks] + upd


def _gla_mixer(q, k, v, r, w2, b2, n_ctx):
    bsz, lt, dkt = q.shape
    dvt = v.shape[-1]
    nr = r.shape[-1]
    t, heads = MIX_CHUNK, GLA_HEADS
    bt = MIX_BATCH
    nch = lt // t
    ncc = n_ctx // t
    dk = dkt // heads
    dv = dvt // heads
    imap = lambda d, b, c: (b, _chunk_order(d, c, ncc, nch), 0)
    return pl.pallas_call(
        functools.partial(_gla_kernel, t=t, heads=heads, scale=dk ** -0.5),
        grid=(2, bsz // bt, nch),
        in_specs=[
            pl.BlockSpec((bt, t, dkt), imap),
            pl.BlockSpec((bt, t, dkt), imap),
            pl.BlockSpec((bt, t, dvt), imap),
            pl.BlockSpec((bt, t, nr), imap),
            pl.BlockSpec((1, nr, dkt), lambda d, b, c: (d, 0, 0)),
            pl.BlockSpec((1, 1, dkt), lambda d, b, c: (d, 0, 0)),
        ],
        out_specs=pl.BlockSpec((1, bt, t, dvt), lambda d, b, c: (d, b, _chunk_order(d, c, ncc, nch), 0)),
        out_shape=jax.ShapeDtypeStruct((2, bsz, lt, dvt), F32),
        scratch_shapes=[pltpu.VMEM((bt * heads, dv, dk), F32)],
        compiler_params=_cparams(("parallel", "parallel", "arbitrary")),
        name="gla_mixer",
    )(q, k, v, r, w2, b2)


def _mlstm_kernel(q_ref, k_ref, v_ref, gc_ref, gr_ref, o_ref, c_ref, n_ref, m_ref, *, t, heads):
    d = pl.program_id(0)
    c = pl.program_id(2)

    @pl.when(c == 0)
    def _():
        c_ref[...] = jnp.zeros_like(c_ref)
        n_ref[...] = jnp.zeros_like(n_ref)
        m_ref[...] = jnp.zeros_like(m_ref)

    dh = q_ref.shape[-1] // heads
    nb = q_ref.shape[0]
    mask = _dir_tri(d, t)
    tri = mask.astype(F32)
    items = [(bb, h) for bb in range(nb) for h in range(heads)]
    gate = []
    for bb in range(nb):
        gc = gc_ref[0, bb]
        gr = gr_ref[0, bb, 0]
        fc = jax.nn.log_sigmoid(gc[:, heads:])
        fr = jax.nn.log_sigmoid(gr[heads:, :])
        b_col = jnp.dot(tri, fc, preferred_element_type=F32, precision=HI)
        b_row = lax.dot_general(fr, tri, NT, preferred_element_type=F32, precision=HI)
        b_last = jnp.where(d == 0, b_col[t - 1:t, :], b_col[0:1, :])
        gate.append((gc[:, :heads], gr[:heads, :], b_col, b_row, b_last))
    qs, ks, vs, s_raw, q_c = {}, {}, {}, {}, {}
    for bb, h in items:
        hs = slice(h * dh, (h + 1) * dh)
        qs[bb, h] = q_ref[bb, :, hs]
        ks[bb, h] = k_ref[bb, :, hs]
        vs[bb, h] = v_ref[bb, :, hs].astype(BF16)
        s_raw[bb, h] = lax.dot_general(qs[bb, h], ks[bb, h], NT, preferred_element_type=F32)
        q_c[bb, h] = jnp.dot(qs[bb, h], c_ref[bb * heads + h].astype(BF16), preferred_element_type=F32)
    logw, log_inter, m_t, w_inter, scores, den, qn = {}, {}, {}, {}, {}, {}, {}
    for bb, h in items:
        _, ir, b_col, b_row, _ = gate[bb]
        bc = b_col[:, h:h + 1]
        logw[bb, h] = jnp.where(mask, bc - b_row[h:h + 1, :] + ir[h:h + 1, :], -jnp.inf)
        log_inter[bb, h] = bc + m_ref[bb * heads + h]
    for bb, h in items:
        m_t[bb, h] = jnp.maximum(log_inter[bb, h], jnp.max(logw[bb, h], axis=-1, keepdims=True))
        qn[bb, h] = jnp.sum(qs[bb, h].astype(F32) * n_ref[bb * heads + h], axis=-1, keepdims=True)
    for bb, h in items:
        w_inter[bb, h] = jnp.exp(log_inter[bb, h] - m_t[bb, h])
        scores[bb, h] = s_raw[bb, h] * jnp.exp(logw[bb, h] - m_t[bb, h])
    for bb, h in items:
        den[bb, h] = jnp.sum(scores[bb, h], axis=-1, keepdims=True) + w_inter[bb, h] * qn[bb, h]
    num = {}
    for bb, h in items:
        num[bb, h] = (jnp.dot(scores[bb, h].astype(BF16), vs[bb, h], preferred_element_type=F32)
                      + w_inter[bb, h] * q_c[bb, h])
    for bb, h in items:
        hs = slice(h * dh, (h + 1) * dh)
        o_ref[0, bb, :, hs] = num[bb, h] / jnp.maximum(jnp.abs(den[bb, h]), jnp.exp(-m_t[bb, h]))
    log_g, m_new, kw, upd, ksum = {}, {}, {}, {}, {}
    for bb, h in items:
        ic, _, b_col, _, b_last = gate[bb]
        log_g[bb, h] = b_last[:, h:h + 1] - b_col[:, h:h + 1] + ic[:, h:h + 1]
    for bb, h in items:
        b_last = gate[bb][4]
        m_new[bb, h] = jnp.maximum(b_last[:, h:h + 1] + m_ref[bb * heads + h],
                                   jnp.max(log_g[bb, h], axis=0, keepdims=True))
    for bb, h in items:
        kw[bb, h] = ks[bb, h].astype(F32) * jnp.exp(log_g[bb, h] - m_new[bb, h])
    for bb, h in items:
        upd[bb, h] = lax.dot_general(kw[bb, h].astype(BF16), vs[bb, h], TN, preferred_element_type=F32)
        ksum[bb, h] = jnp.sum(kw[bb, h], axis=0, keepdims=True)
    for bb, h in items:
        si = bb * heads + h
        b_last = gate[bb][4]
        keep = jnp.exp(b_last[:, h:h + 1] + m_ref[si] - m_new[bb, h])
        c_ref[si] = keep * c_ref[si] + upd[bb, h]
        n_ref[si] = keep * n_ref[si] + ksum[bb, h]
        m_ref[si] = m_new[bb, h]


def _mlstm_mixer(qk, v, gates, n_ctx):
    bsz, lt, w2 = qk.shape
    w = w2 // 2
    t, heads = MIX_CHUNK, MLSTM_HEADS
    dh = w // heads
    nch = lt // t
    ncc = n_ctx // t
    gc = gates.reshape(bsz, lt, 2, 2 * heads).transpose(2, 0, 1, 3)
    gr = gc.reshape(2, bsz, nch, t, 2 * heads).transpose(0, 1, 2, 4, 3)
    cmap = lambda d, b, c: _chunk_order(d, c, ncc, nch)
    bt = MIX_BATCH
    return pl.pallas_call(
        functools.partial(_mlstm_kernel, t=t, heads=heads),
        grid=(2, bsz // bt, nch),
        in_specs=[
            pl.BlockSpec((bt, t, w), lambda d, b, c: (b, cmap(d, b, c), 0)),
            pl.BlockSpec((bt, t, w), lambda d, b, c: (b, cmap(d, b, c), 1)),
            pl.BlockSpec((bt, t, w), lambda d, b, c: (b, cmap(d, b, c), 0)),
            pl.BlockSpec((1, bt, t, 2 * heads), lambda d, b, c: (d, b, cmap(d, b, c), 0)),
            pl.BlockSpec((1, bt, 1, 2 * heads, t), lambda d, b, c: (d, b, cmap(d, b, c), 0, 0)),
        ],
        out_specs=pl.BlockSpec((1, bt, t, w), lambda d, b, c: (d, b, cmap(d, b, c), 0)),
        out_shape=jax.ShapeDtypeStruct((2, bsz, lt, w), F32),
        scratch_shapes=[pltpu.VMEM((bt * heads, dh, dh), F32), pltpu.VMEM((bt * heads, 1, dh), F32),
                        pltpu.VMEM((bt * heads, 1, 1), F32)],
        compiler_params=_cparams(("parallel", "parallel", "arbitrary")),
        name="mlstm_mixer",
    )(qk, qk, v, gc, gr)


def _conv_kernel(x_ref, w_ref, b_ref, s_ref, o_ref, *, n_ctx):
    x = x_ref[0]
    lt = x.shape[0]
    row = lax.broadcasted_iota(jnp.int32, x.shape, 0)
    prev = jnp.where((row == 0) | (row == n_ctx), 0.0, pltpu.roll(x, 1, 0))
    nxt = jnp.where((row == n_ctx - 1) | (row == lt - 1), 0.0, pltpu.roll(x, lt - 1, 0))
    y = b_ref[...] + w_ref[0:1, :] * prev + w_ref[1:2, :] * x + w_ref[2:3, :] * nxt
    o_ref[0] = (y * jax.nn.sigmoid(y) * s_ref[...]).astype(o_ref.dtype)


def _conv_silu(x, w, b, colscale, n_ctx):
    bsz, lt, ch = x.shape
    tc = 256
    return pl.pallas_call(
        functools.partial(_conv_kernel, n_ctx=n_ctx),
        grid=(bsz, ch // tc),
        in_specs=[
            pl.BlockSpec((1, lt, tc), lambda b, j: (b, 0, j)),
            pl.BlockSpec((3, tc), lambda b, j: (0, j)),
            pl.BlockSpec((1, tc), lambda b, j: (0, j)),
            pl.BlockSpec((1, tc), lambda b, j: (0, j)),
        ],
        out_specs=pl.BlockSpec((1, lt, tc), lambda b, j: (b, 0, j)),
        out_shape=jax.ShapeDtypeStruct((bsz, lt, ch), BF16),
        compiler_params=_cparams(("parallel", "parallel")),
        name="conv_silu",
    )(x, w, b.reshape(1, ch), colscale.reshape(1, ch))


def _s5_matrices(a_re, a_im, log_dt, b_re, b_im, c_re, c_im, backward, lane_groups=8):
    g, p = a_re.shape
    cg = b_re.shape[-1]
    j = S5_J
    lg = lane_groups
    nq = g // lg
    dt = jnp.exp(log_dt)[:, None]
    lam_re = jnp.minimum(a_re, -1e-4)
    lam_im = a_im
    decay = jnp.exp(lam_re * dt)
    ab_re = decay * jnp.cos(lam_im * dt)
    ab_im = decay * jnp.sin(lam_im * dt)
    den = lam_re * lam_re + lam_im * lam_im
    zr = ((ab_re - 1) * lam_re + ab_im * lam_im) / den
    zi = (ab_im * lam_re - (ab_re - 1) * lam_im) / den
    bb_re = zr[..., None] * b_re - zi[..., None] * b_im
    bb_im = zr[..., None] * b_im + zi[..., None] * b_re
    pw_re, pw_im = [jnp.ones_like(ab_re)], [jnp.zeros_like(ab_im)]
    for _ in range(j):
        r0, i0 = pw_re[-1], pw_im[-1]
        pw_re.append(ab_re * r0 - ab_im * i0)
        pw_im.append(ab_re * i0 + ab_im * r0)
    pw_re, pw_im = jnp.stack(pw_re), jnp.stack(pw_im)
    ca_re = c_re[None] * pw_re[:, :, None, :] - c_im[None] * pw_im[:, :, None, :]
    ca_im = c_re[None] * pw_im[:, :, None, :] + c_im[None] * pw_re[:, :, None, :]
    kk = (jnp.einsum('tgcp,gpd->tgcd', ca_re[:j], bb_re, precision=HI)
          - jnp.einsum('tgcp,gpd->tgcd', ca_im[:j], bb_im, precision=HI))
    ab_pw_re = pw_re[:j, :, :, None] * bb_re[None] - pw_im[:j, :, :, None] * bb_im[None]
    ab_pw_im = pw_re[:j, :, :, None] * bb_im[None] + pw_im[:j, :, :, None] * bb_re[None]
    jj = jnp.arange(j)
    lag = (jj[:, None] - jj[None, :]) if backward else (jj[None, :] - jj[:, None])
    kt = jnp.where((lag >= 0)[:, :, None, None, None], kk[jnp.clip(lag, 0, j - 1)], 0.0)
    eye = jnp.eye(lg, dtype=F32)
    kt = kt.reshape(j, j, nq, lg, cg, cg)
    ktoep = jnp.einsum('ioqgcd,gh->qigdohc', kt, eye).reshape(nq, j * lg * cg, j * lg * cg)
    tau_in = jj if backward else (j - 1 - jj)
    wi_re = ab_pw_re[tau_in].reshape(j, nq, lg, p, cg)
    wi_im = ab_pw_im[tau_in].reshape(j, nq, lg, p, cg)
    win_re = jnp.einsum('jqgpc,gh->qjgchp', wi_re, eye).reshape(nq, j * lg * cg, lg * p)
    win_im = jnp.einsum('jqgpc,gh->qjgchp', wi_im, eye).reshape(nq, j * lg * cg, lg * p)
    tau_out = (j - jj) if backward else (jj + 1)
    wo_re = ca_re[tau_out].reshape(j, nq, lg, cg, p)
    wo_im = ca_im[tau_out].reshape(j, nq, lg, cg, p)
    wout_re = jnp.einsum('jqgcp,gh->qgpjhc', wo_re, eye).reshape(nq, lg * p, j * lg * cg)
    wout_im = -jnp.einsum('jqgcp,gh->qgpjhc', wo_im, eye).reshape(nq, lg * p, j * lg * cg)
    dec_re = pw_re[j].reshape(nq, 1, lg * p)
    dec_im = pw_im[j].reshape(nq, 1, lg * p)
    return (ktoep.astype(BF16), win_re.astype(BF16), win_im.astype(BF16),
            wout_re.astype(BF16), wout_im.astype(BF16), dec_re, dec_im)


def _s5_kernel(u_ref, kt_ref, wir_ref, wii_ref, wor_ref, woi_ref, dr_ref, di_ref, y_ref,
               xf_ref, yf_ref, sre_ref, sim_ref, *, bt, nk, nk_ctx, rs):
    d = pl.program_id(0)
    j = S5_J
    lanes = u_ref.shape[-1]
    for b in range(bt):
        for jj in range(j):
            xf_ref[b * nk:(b + 1) * nk, jj * lanes:(jj + 1) * lanes] = (
                u_ref.at[b][pl.ds(jj, nk, stride=j), :].astype(BF16))
    xf = xf_ref[...]
    yf_ref[...] = jnp.dot(xf, kt_ref[0, 0], preferred_element_type=F32)
    inc_re = jnp.dot(xf, wir_ref[0, 0], preferred_element_type=F32)
    inc_im = jnp.dot(xf, wii_ref[0, 0], preferred_element_type=F32)
    nl = sre_ref.shape[0]
    for b in range(bt):
        for l in range(nl):
            sre_ref[l, b * rs:b * rs + nk, :] = inc_re[b * nk:(b + 1) * nk, l * lanes:(l + 1) * lanes]
            sim_ref[l, b * rs:b * rs + nk, :] = inc_im[b * nk:(b + 1) * nk, l * lanes:(l + 1) * lanes]
    a_re = [dr_ref[0, 0, :, l * lanes:(l + 1) * lanes] for l in range(nl)]
    a_im = [di_ref[0, 0, :, l * lanes:(l + 1) * lanes] for l in range(nl)]

    def step(kidx, carry):
        rows = pl.ds(kidx, bt, stride=rs)
        out = []
        for l in range(nl):
            s_re, s_im = carry[2 * l], carry[2 * l + 1]
            i_re = sre_ref.at[l][rows, :]
            i_im = sim_ref.at[l][rows, :]
            sre_ref.at[l][rows, :] = s_re
            sim_ref.at[l][rows, :] = s_im
            out.append(a_re[l] * s_re - a_im[l] * s_im + i_re)
            out.append(a_re[l] * s_im + a_im[l] * s_re + i_im)
        return tuple(out)

    zero = tuple(jnp.zeros((bt, lanes), F32) for _ in range(2 * nl))

    @pl.when(d == 0)
    def _():
        lax.fori_loop(0, nk, step, zero)

    @pl.when(d == 1)
    def _():
        carry = lax.fori_loop(0, nk_ctx, lambda i, cr: step(nk_ctx - 1 - i, cr), zero)
        lax.fori_loop(0, nk - nk_ctx, lambda i, cr: step(nk - 1 - i, cr), carry)

    for b in range(bt):
        sp_re = jnp.concatenate([sre_ref[l, b * rs:b * rs + nk, :] for l in range(nl)], axis=-1).astype(BF16)
        sp_im = jnp.concatenate([sim_ref[l, b * rs:b * rs + nk, :] for l in range(nl)], axis=-1).astype(BF16)
        yb = (yf_ref[b * nk:(b + 1) * nk, :]
              + jnp.dot(sp_re, wor_ref[0, 0], preferred_element_type=F32)
              + jnp.dot(sp_im, woi_ref[0, 0], preferred_element_type=F32))
        for jj in range(j):
            y_ref.at[0, b][pl.ds(jj, nk, stride=j), :] = yb[:, jj * lanes:(jj + 1) * lanes]


def _s5_mixer(u, mats, n_ctx):
    bsz, lt, w = u.shape
    ktoep, win_re, win_im, wout_re, wout_im, dec_re, dec_im = mats
    lanes = 128
    bt = 4 if bsz % 4 == 0 else 2
    nq = w // lanes
    j = S5_J
    nk = lt // j
    nk_ctx = n_ctx // j
    rs = nk + 8
    fl = j * lanes
    sw = win_re.shape[-1]
    wmap = lambda d, q, b: (d, q, 0, 0)
    return pl.pallas_call(
        functools.partial(_s5_kernel, bt=bt, nk=nk, nk_ctx=nk_ctx, rs=rs),
        grid=(2, nq, bsz // bt),
        in_specs=[
            pl.BlockSpec((bt, lt, lanes), lambda d, q, b: (b, 0, q)),
            pl.BlockSpec((1, 1, fl, fl), wmap),
            pl.BlockSpec((1, 1, fl, sw), wmap),
            pl.BlockSpec((1, 1, fl, sw), wmap),
            pl.BlockSpec((1, 1, sw, fl), wmap),
            pl.BlockSpec((1, 1, sw, fl), wmap),
            pl.BlockSpec((1, 1, 1, sw), wmap),
            pl.BlockSpec((1, 1, 1, sw), wmap),
        ],
        out_specs=pl.BlockSpec((1, bt, lt, lanes), lambda d, q, b: (d, b, 0, q)),
        out_shape=jax.ShapeDtypeStruct((2, bsz, lt, w), F32),
        scratch_shapes=[pltpu.VMEM((bt * nk, fl), BF16), pltpu.VMEM((bt * nk, fl), F32),
                        pltpu.VMEM((sw // lanes, bt * rs, lanes), F32),
                        pltpu.VMEM((sw // lanes, bt * rs, lanes), F32)],
        compiler_params=_cparams(("parallel", "parallel", "arbitrary")),
        name="s5_mixer",
    )(u, ktoep, win_re, win_im, wout_re, wout_im, dec_re, dec_im)


def _head_norm(x, heads):
    dh = x.shape[-1] // heads
    outs = []
    for h in range(heads):
        xh = x[:, h * dh:(h + 1) * dh]
        outs.append(xh * lax.rsqrt(jnp.mean(xh * xh, axis=-1, keepdims=True) + EPS))
    return jnp.concatenate(outs, axis=-1)


def _even_post_kernel(m_ref, o_ref, s_ref, u_ref, mg_ref, dsk_ref, gw_ref, gb_ref, w_ref, h_ref, gate_ref,
                      out_ref, *, heads):
    m = m_ref[0, 0] + m_ref[1, 0]
    m_out = _head_norm(m, heads) * mg_ref[...] * jax.nn.sigmoid(o_ref[0])
    y = jax.nn.gelu(s_ref[0, 0] + s_ref[1, 0] + dsk_ref[...] * u_ref[0])
    glu = jnp.dot(y.astype(BF16), gw_ref[...], preferred_element_type=F32) + gb_ref[...]
    s_out = y * jax.nn.sigmoid(glu)
    cat = jnp.concatenate([m_out, s_out], axis=-1).astype(BF16)
    z = jnp.dot(cat, w_ref[...], preferred_element_type=F32)
    out_ref[0] = h_ref[0] + gate_ref[0, 0] * z


def _even_post(m2, o, s2, u, mnorm_g, d_skip, glu_w, glu_b, w_out, h, gate):
    bsz, lt, d = h.shape
    mw = o.shape[-1]
    sw = u.shape[-1]
    tm = ROW_TILE
    row = lambda b, i: (b, i, 0)
    row2 = lambda b, i: (0, b, i, 0)
    const = lambda b, i: (0, 0)
    return pl.pallas_call(
        functools.partial(_even_post_kernel, heads=MLSTM_HEADS),
        grid=(bsz, lt // tm),
        in_specs=[
            pl.BlockSpec((2, 1, tm, mw), row2),
            pl.BlockSpec((1, tm, mw), row),
            pl.BlockSpec((2, 1, tm, sw), row2),
            pl.BlockSpec((1, tm, sw), row),
            pl.BlockSpec((1, mw), const),
            pl.BlockSpec((1, sw), const),
            pl.BlockSpec((sw, sw), const),
            pl.BlockSpec((1, sw), const),
            pl.BlockSpec((mw + sw, d), const),
            pl.BlockSpec((1, tm, d), row),
            pl.BlockSpec((1, 1, 1, d), _seg_map),
        ],
        out_specs=pl.BlockSpec((1, tm, d), row),
        out_shape=jax.ShapeDtypeStruct((bsz, lt, d), F32),
        compiler_params=_cparams(("parallel", "parallel")),
        name="even_post",
    )(m2, o, s2, u, mnorm_g.reshape(1, mw), d_skip.reshape(1, sw), glu_w.astype(BF16), glu_b.reshape(1, sw),
      w_out.astype(BF16), h, gate)


def _odd_post_kernel(o_ref, g_ref, ng_ref, w_ref, h_ref, gate_ref, out_ref, *, heads):
    g = g_ref[0]
    y = _head_norm(o_ref[0], heads) * ng_ref[...] * (g * jax.nn.sigmoid(g))
    z = jnp.dot(y.astype(BF16), w_ref[...], preferred_element_type=F32)
    out_ref[0] = h_ref[0] + gate_ref[0, 0] * z


def _odd_post(o, g, norm_g, w_out, h, gate):
    bsz, lt, d = h.shape
    dv = o.shape[-1]
    tm = ROW_TILE
    row = lambda b, i: (b, i, 0)
    const = lambda b, i: (0, 0)
    return pl.pallas_call(
        functools.partial(_odd_post_kernel, heads=GLA_HEADS),
        grid=(bsz, lt // tm),
        in_specs=[
            pl.BlockSpec((1, tm, dv), row),
            pl.BlockSpec((1, tm, dv), row),
            pl.BlockSpec((1, dv), const),
            pl.BlockSpec((dv, d), const),
            pl.BlockSpec((1, tm, d), row),
            pl.BlockSpec((1, 1, 1, d), _seg_map),
        ],
        out_specs=pl.BlockSpec((1, tm, d), row),
        out_shape=jax.ShapeDtypeStruct((bsz, lt, d), F32),
        compiler_params=_cparams(("parallel", "parallel")),
        name="odd_post",
    )(o, g, norm_g.reshape(1, dv), w_out.astype(BF16), h, gate)


def kernel(x, c, ctx, c_ctx, mod_w, mod_b, norm_mix_g, norm_ffn_g, ev_w_in, ev_b_in, ev_conv_w, ev_conv_b, ev_mlstm_norm_g, ev_s5_a_re_f, ev_s5_a_im_f, ev_s5_log_dt_f, ev_s5_a_re_b, ev_s5_a_im_b, ev_s5_log_dt_b, ev_s5_b_re, ev_s5_b_im, ev_s5_c_re, ev_s5_c_im, ev_s5_d, ev_s5_glu_w, ev_s5_glu_b, ev_w_out, od_w_in, od_gate_w2_f, od_gate_b2_f, od_gate_w2_b, od_gate_b2_b, od_norm_g, od_w_out, router_w, router_b, moe_w_gu, moe_b_gu, moe_w_down, moe_b_down, final_norm_g):
    bsz, seq, d = x.shape
    n_ctx = ctx.shape[1]
    depth = mod_w.shape[0]
    lt = n_ctx + seq
    assert n_ctx == ROW_TILE and seq % ROW_TILE == 0 and seq % GRID_W == 0

    h = jnp.concatenate([ctx, x], axis=1)
    c_all = jnp.concatenate([c, c_ctx[None, :]], axis=0)
    c_all = jnp.pad(c_all, ((0, (-c_all.shape[0]) % 8), (0, 0)))
    mods = _modulation(c_all, mod_w, mod_b)
    mod_lat = mods[:, :bsz]
    mod_ctx = jnp.broadcast_to(mods[:, bsz:bsz + 1], mod_lat.shape)
    mod6 = jnp.stack([mod_ctx, mod_lat], axis=2).reshape(depth, bsz, 2, 6, 1, d)

    bg_all = moe_b_gu[..., 0::2]
    bu_all = moe_b_gu[..., 1::2]

    mw = ev_conv_w.shape[-1] // 2
    n_gates = 4 * MLSTM_HEADS
    s5w = ev_s5_d.shape[-1]
    dk_t = od_gate_w2_f.shape[-1]
    dv_t = od_norm_g.shape[-1]
    rows = seq // GRID_W

    def to_cols(a):
        lat = a[:, n_ctx:].reshape(bsz, rows, GRID_W, -1).transpose(0, 2, 1, 3).reshape(bsz, seq, -1)
        return jnp.concatenate([a[:, :n_ctx], lat], axis=1)

    def to_rows(a):
        lat = a[:, n_ctx:].reshape(bsz, GRID_W, rows, -1).transpose(0, 2, 1, 3).reshape(bsz, seq, -1)
        return jnp.concatenate([a[:, :n_ctx], lat], axis=1)

    for layer in range(depth):
        last = layer == depth - 1
        j = layer // 2
        m6 = mod6[layer]
        sh1, sc1, g1, sh2, sc2, g2 = (m6[:, :, i] for i in range(6))
        if layer % 2 == 0:
            w_in, b_in = ev_w_in[j], ev_b_in[j]
            cols = jnp.concatenate([jnp.arange(0, 4 * mw), jnp.arange(4 * mw + n_gates, 4 * mw + n_gates + s5w),
                                    jnp.arange(4 * mw, 4 * mw + n_gates)])
            qk_pre, v, o, u, gates = _nm_matmul(h, norm_mix_g[layer], sh1, sc1, w_in[:, cols], b_in[cols],
                                                (2 * mw, mw, mw, s5w, n_gates))
            dh = mw // MLSTM_HEADS
            colscale = jnp.concatenate([jnp.full((mw,), dh ** -0.5, F32), jnp.ones((mw,), F32)])
            qk = _conv_silu(qk_pre, ev_conv_w[j], ev_conv_b[j], colscale, n_ctx)
            m2 = _mlstm_mixer(qk, v, gates, n_ctx)
            shared = (ev_s5_b_re[j], ev_s5_b_im[j], ev_s5_c_re[j], ev_s5_c_im[j])
            mats_f = _s5_matrices(ev_s5_a_re_f[j], ev_s5_a_im_f[j], ev_s5_log_dt_f[j], *shared, backward=False)
            mats_b = _s5_matrices(ev_s5_a_re_b[j], ev_s5_a_im_b[j], ev_s5_log_dt_b[j], *shared, backward=True)
            s2 = _s5_mixer(u, tuple(jnp.stack([a, b]) for a, b in zip(mats_f, mats_b)), n_ctx)
            h = _even_post(m2, o, s2, u, ev_mlstm_norm_g[j], ev_s5_d[j], ev_s5_glu_w[j], ev_s5_glu_b[j],
                           ev_w_out[j], h, g1)
        else:
            qq, kk, vv, gg, rr = _nm_matmul(h, norm_mix_g[layer], sh1, sc1, od_w_in[j],
                                            jnp.zeros((od_w_in.shape[-1],), F32),
                                            (dk_t, dk_t, dv_t, dv_t, 2 * GLA_RANK))
            zero = jnp.zeros_like(od_gate_w2_f[j])
            w2 = jnp.stack([jnp.concatenate([od_gate_w2_f[j], zero], axis=0),
                            jnp.concatenate([zero, od_gate_w2_b[j]], axis=0)])
            b2 = jnp.stack([od_gate_b2_f[j], od_gate_b2_b[j]])[:, None, :]
            o2 = _gla_mixer(to_cols(qq), to_cols(kk), to_cols(vv), to_cols(rr), w2, b2, n_ctx)
            h = _odd_post(to_rows(o2[0] + o2[1]), gg, od_norm_g[j], od_w_out[j], h, g1)
        f, logits = _ffn_prep(h, norm_ffn_g[layer], sh2, sc2, router_w[layer], router_b[layer])
        weights = (layer, moe_w_gu, moe_w_down, bg_all[layer], bu_all[layer], moe_b_down[layer])
        h = _moe_layer(h, f, logits, g2, weights, last, n_ctx)
    return _final_norm(h, final_norm_g)
```

```python
import functools

import jax
import jax.numpy as jnp
from jax import lax
from jax.experimental import pallas as pl
from jax.experimental.pallas import tpu as pltpu

F32 = jnp.float32
BF16 = jnp.bfloat16
HI = lax.Precision.HIGHEST

EPS = 1e-6
GRID_W = 64
MLSTM_HEADS = 4
S5_GROUP = 16
GLA_HEADS = 4
GLA_RANK = 16
GLA_TAU = 16.0
N_EXPERTS = 32
TOP_K = 4
SWIGLU_LIMIT = 7.0
SWIGLU_ALPHA = 1.702

ROW_TILE = 256
MOE_TILE = 512
MIX_CHUNK = 64
MIX_BATCH = 4
S5_J = 8
VMEM_LIMIT = 56 * 1024 * 1024

NT = (((1,), (1,)), ((), ()))
TN = (((0,), (0,)), ((), ()))


def _cparams(sem):
    return pltpu.CompilerParams(dimension_semantics=sem, vmem_limit_bytes=VMEM_LIMIT)


def _mod_kernel(c_ref, w_ref, b_ref, o_ref):
    c = c_ref[...]
    a = c * jax.nn.sigmoid(c)
    o_ref[0] = jnp.dot(a.astype(BF16), w_ref[0].astype(BF16), preferred_element_type=F32) + b_ref[0]


def _modulation(c_all, mod_w, mod_b):
    depth, d, n6 = mod_w.shape
    rows = c_all.shape[0]
    tn = d
    return pl.pallas_call(
        _mod_kernel,
        grid=(depth, n6 // tn),
        in_specs=[
            pl.BlockSpec((rows, d), lambda l, j: (0, 0)),
            pl.BlockSpec((1, d, tn), lambda l, j: (l, 0, j)),
            pl.BlockSpec((1, 1, tn), lambda l, j: (l, 0, j)),
        ],
        out_specs=pl.BlockSpec((1, rows, tn), lambda l, j: (l, 0, j)),
        out_shape=jax.ShapeDtypeStruct((depth, rows, n6), F32),
        compiler_params=_cparams(("arbitrary", "arbitrary")),
        name="modulation",
    )(c_all, mod_w, mod_b.reshape(depth, 1, n6))


def _norm_mod(x, g, sh, sc):
    ms = jnp.mean(x * x, axis=-1, keepdims=True)
    return (x * lax.rsqrt(ms + EPS) * g) * (1.0 + sc) + sh


def _nm_matmul_kernel(x_ref, g_ref, sh_ref, sc_ref, w_ref, b_ref, *out_refs, splits):
    a = _norm_mod(x_ref[0], g_ref[...], sh_ref[0, 0], sc_ref[0, 0])
    z = jnp.dot(a.astype(BF16), w_ref[...], preferred_element_type=F32) + b_ref[...]
    for (lo, hi), o_ref in zip(splits, out_refs):
        o_ref[0] = z[:, lo:hi].astype(o_ref.dtype)


def _seg_map(b, i):
    return (b, jnp.minimum(i, 1), 0, 0)


def _nm_matmul(h, g, shift, scale, w, bias, widths):
    bsz, lt, d = h.shape
    p = w.shape[1]
    splits, lo = [], 0
    for wd in widths:
        splits.append((lo, lo + wd))
        lo += wd
    assert lo == p
    tm = ROW_TILE
    return pl.pallas_call(
        functools.partial(_nm_matmul_kernel, splits=tuple(splits)),
        grid=(bsz, lt // tm),
        in_specs=[
            pl.BlockSpec((1, tm, d), lambda b, i: (b, i, 0)),
            pl.BlockSpec((1, d), lambda b, i: (0, 0)),
            pl.BlockSpec((1, 1, 1, d), _seg_map),
            pl.BlockSpec((1, 1, 1, d), _seg_map),
            pl.BlockSpec((d, p), lambda b, i: (0, 0)),
            pl.BlockSpec((1, p), lambda b, i: (0, 0)),
        ],
        out_specs=[pl.BlockSpec((1, tm, wd), lambda b, i: (b, i, 0)) for wd in widths],
        out_shape=[jax.ShapeDtypeStruct((bsz, lt, wd), F32) for wd in widths],
        compiler_params=_cparams(("parallel", "parallel")),
        name="norm_mod_matmul",
    )(h, g.reshape(1, d), shift, scale, w.astype(BF16), bias.reshape(1, p))


def _ffn_prep_kernel(x_ref, g_ref, sh_ref, sc_ref, rw_ref, rb_ref, f_ref, logit_ref):
    a = _norm_mod(x_ref[0], g_ref[...], sh_ref[0, 0], sc_ref[0, 0])
    f_ref[0] = a.astype(f_ref.dtype)
    logit_ref[0] = jnp.dot(a, rw_ref[...], preferred_element_type=F32, precision=HI) + rb_ref[...]


def _ffn_prep(h, g, shift, scale, router_w, router_b):
    bsz, lt, d = h.shape
    ne = router_w.shape[1]
    tm = ROW_TILE
    return pl.pallas_call(
        _ffn_prep_kernel,
        grid=(bsz, lt // tm),
        in_specs=[
            pl.BlockSpec((1, tm, d), lambda b, i: (b, i, 0)),
            pl.BlockSpec((1, d), lambda b, i: (0, 0)),
            pl.BlockSpec((1, 1, 1, d), _seg_map),
            pl.BlockSpec((1, 1, 1, d), _seg_map),
            pl.BlockSpec((d, ne), lambda b, i: (0, 0)),
            pl.BlockSpec((1, ne), lambda b, i: (0, 0)),
        ],
        out_specs=[
            pl.BlockSpec((1, tm, d), lambda b, i: (b, i, 0)),
            pl.BlockSpec((1, tm, ne), lambda b, i: (b, i, 0)),
        ],
        out_shape=[
            jax.ShapeDtypeStruct((bsz, lt, d), BF16),
            jax.ShapeDtypeStruct((bsz, lt, ne), F32),
        ],
        compiler_params=_cparams(("parallel", "parallel")),
        name="ffn_prep",
    )(h, g.reshape(1, d), shift, scale, router_w, router_b.reshape(1, ne))


GU_BLOCK = 256


def _moe_kernel(be_ref, nb_ref, x_ref, wgu_ref, wd_ref, bg_ref, bu_ref, bd_ref, o_ref, wgu_s, wd_s):
    i = pl.program_id(0)
    active = i < nb_ref[0]
    half = GU_BLOCK // 2
    nblk = wgu_s.shape[1] // GU_BLOCK

    @pl.when(active & ((i == 0) | (be_ref[i] != be_ref[jnp.maximum(i - 1, 0)])))
    def _():
        r = lax.broadcasted_iota(jnp.int32, (GU_BLOCK, GU_BLOCK), 0)
        c = lax.broadcasted_iota(jnp.int32, (GU_BLOCK, GU_BLOCK), 1)
        perm = (r == jnp.where(c < half, 2 * c, 2 * (c - half) + 1)).astype(BF16)
        for k in range(nblk):
            cs = slice(k * GU_BLOCK, (k + 1) * GU_BLOCK)
            wgu_s[:, cs] = jnp.dot(wgu_ref[0, 0, :, cs].astype(BF16), perm,
                                   preferred_element_type=F32).astype(BF16)
        wd_s[...] = wd_ref[0, 0].astype(BF16)

    @pl.when(active)
    def _():
        gu = jnp.dot(x_ref[...], wgu_s[...], preferred_element_type=F32)
        hdn = []
        for k in range(nblk):
            hs = slice(k * half, (k + 1) * half)
            g = gu[:, k * GU_BLOCK:k * GU_BLOCK + half] + bg_ref[0, :, hs]
            u = gu[:, k * GU_BLOCK + half:(k + 1) * GU_BLOCK] + bu_ref[0, :, hs]
            g = jnp.minimum(g, SWIGLU_LIMIT)
            u = jnp.clip(u, -SWIGLU_LIMIT, SWIGLU_LIMIT)
            hdn.append(((u + 1.0) * (g * jax.nn.sigmoid(SWIGLU_ALPHA * g))).astype(BF16))
        hdn = jnp.concatenate(hdn, axis=-1)
        o_ref[...] = (jnp.dot(hdn, wd_s[...], preferred_element_type=F32) + bd_ref[0]).astype(o_ref.dtype)

    @pl.when(jnp.logical_not(active))
    def _():
        o_ref[...] = jnp.zeros_like(o_ref)


def _moe_experts(x_sorted, block_expert, n_used, layer, w_gu, w_down, bg, bu, bd):
    n_rows, d = x_sorted.shape
    _, ne, _, f2 = w_gu.shape
    f = f2 // 2
    tm = MOE_TILE
    n_blocks = n_rows // tm
    assert f2 % GU_BLOCK == 0
    grid_spec = pltpu.PrefetchScalarGridSpec(
        num_scalar_prefetch=2,
        grid=(n_blocks,),
        in_specs=[
            pl.BlockSpec((tm, d), lambda i, be, nb: (i, 0)),
            pl.BlockSpec((1, 1, d, f2), lambda i, be, nb: (layer, be[i], 0, 0)),
            pl.BlockSpec((1, 1, f, d), lambda i, be, nb: (layer, be[i], 0, 0)),
            pl.BlockSpec((1, 1, f), lambda i, be, nb: (be[i], 0, 0)),
            pl.BlockSpec((1, 1, f), lambda i, be, nb: (be[i], 0, 0)),
            pl.BlockSpec((1, 1, d), lambda i, be, nb: (be[i], 0, 0)),
        ],
        out_specs=pl.BlockSpec((tm, d), lambda i, be, nb: (i, 0)),
        scratch_shapes=[pltpu.VMEM((d, f2), BF16), pltpu.VMEM((f, d), BF16)],
    )
    return pl.pallas_call(
        _moe_kernel,
        grid_spec=grid_spec,
        out_shape=jax.ShapeDtypeStruct((n_rows, d), BF16),
        compiler_params=_cparams(("arbitrary",)),
        name="moe_experts",
    )(block_expert, n_used, x_sorted, w_gu, w_down,
      bg.reshape(ne, 1, f), bu.reshape(ne, 1, f), bd.reshape(ne, 1, d))


def _combine_kernel(y_ref, gt_ref, h_ref, g2_ref, o_ref):
    gt = gt_ref[0]
    acc = y_ref[0, 0].astype(F32) * gt[:, 0:1]
    for k in range(1, TOP_K):
        acc = acc + y_ref[k, 0].astype(F32) * gt[:, k:k + 1]
    o_ref[0] = h_ref[0] + g2_ref[0, 0] * acc


def _moe_combine(yg, gate, h, g2, seg_map):
    k, bsz, lt, d = yg.shape
    tm = ROW_TILE
    return pl.pallas_call(
        _combine_kernel,
        grid=(bsz, lt // tm),
        in_specs=[
            pl.BlockSpec((k, 1, tm, d), lambda b, i: (0, b, i, 0)),
            pl.BlockSpec((1, tm, k), lambda b, i: (b, i, 0)),
            pl.BlockSpec((1, tm, d), lambda b, i: (b, i, 0)),
            pl.BlockSpec((1, 1, 1, d), seg_map),
        ],
        out_specs=pl.BlockSpec((1, tm, d), lambda b, i: (b, i, 0)),
        out_shape=jax.ShapeDtypeStruct((bsz, lt, d), F32),
        compiler_params=_cparams(("parallel", "parallel")),
        name="moe_combine",
    )(yg, gate, h, g2)


def _lat_seg_map(b, i):
    return (b, 1, 0, 0)


def _moe_layer(h, f, logits, g2, weights, lat_only, n_ctx):
    layer, w_gu, w_down, bg, bu, bd = weights
    bsz, lt, d = h.shape
    if lat_only:
        h_in, f, logits = h[:, n_ctx:], f[:, n_ctx:], logits[:, n_ctx:]
    else:
        h_in = h
    ltok = h_in.shape[1]
    n = bsz * ltok
    top_logit, top_e = lax.top_k(logits.reshape(n, -1), TOP_K)
    gate = jax.nn.softmax(top_logit, axis=-1)
    n_assign = n * TOP_K
    tm = MOE_TILE
    flat_e = top_e.reshape(-1).astype(jnp.int32)
    iota = jnp.arange(n_assign, dtype=jnp.int32)
    e_sorted, order = lax.sort_key_val(flat_e, iota, is_stable=True)
    experts = jnp.arange(N_EXPERTS, dtype=jnp.int32)
    start = jnp.searchsorted(e_sorted, experts, side='left', method='compare_all').astype(jnp.int32)
    counts = jnp.searchsorted(e_sorted, experts, side='right', method='compare_all').astype(jnp.int32) - start
    padded = (counts + tm - 1) // tm * tm
    pad_end = jnp.cumsum(padded)
    pad_start = pad_end - padded
    dest = pad_start[e_sorted] + iota - start[e_sorted]
    n_blocks = -(-(n_assign + N_EXPERTS * (tm - 1)) // tm)
    n_rows = n_blocks * tm
    block_expert = jnp.minimum(
        jnp.searchsorted(pad_end, jnp.arange(n_blocks, dtype=jnp.int32) * tm, side='right', method='compare_all'),
        N_EXPERTS - 1).astype(jnp.int32)
    n_used = (pad_end[-1] // tm).astype(jnp.int32).reshape(1)
    row = jnp.arange(n_rows, dtype=jnp.int32).reshape(n_blocks, tm)
    blk_shift = (start - pad_start)[block_expert][:, None]
    blk_end = (pad_start + counts)[block_expert][:, None]
    slot = jnp.clip(row + blk_shift, 0, n_assign - 1).reshape(-1)
    row_token = jnp.where((row < blk_end).reshape(-1),
                          order.at[slot].get(mode='promise_in_bounds') // TOP_K, 0)
    _, pos = lax.sort_key_val(order, dest, is_stable=True)
    x_sorted = f.reshape(n, d).at[row_token].get(mode='promise_in_bounds')
    y = _moe_experts(x_sorted, block_expert, n_used, layer, w_gu, w_down, bg, bu, bd)
    pos_k = pos.reshape(n, TOP_K).T
    yg = y.at[pos_k.reshape(-1)].get(mode='promise_in_bounds').reshape(TOP_K, bsz, ltok, d)
    out = _moe_combine(yg, gate.reshape(bsz, ltok, TOP_K), h_in, g2,
                       _lat_seg_map if lat_only else _seg_map)
    return out


def _rmsnorm_kernel(x_ref, g_ref, o_ref):
    x = x_ref[0]
    ms = jnp.mean(x * x, axis=-1, keepdims=True)
    o_ref[0] = x * lax.rsqrt(ms + EPS) * g_ref[...]


def _final_norm(h, g):
    bsz, lt, d = h.shape
    tm = ROW_TILE
    return pl.pallas_call(
        _rmsnorm_kernel,
        grid=(bsz, lt // tm),
        in_specs=[pl.BlockSpec((1, tm, d), lambda b, i: (b, i, 0)),
                  pl.BlockSpec((1, d), lambda b, i: (0, 0))],
        out_specs=pl.BlockSpec((1, tm, d), lambda b, i: (b, i, 0)),
        out_shape=jax.ShapeDtypeStruct((bsz, lt, d), F32),
        compiler_params=_cparams(("parallel", "parallel")),
        name="final_norm",
    )(h, g.reshape(1, d))


def _chunk_order(d, c, n_ctx_chunks, n_chunks):
    bwd = jnp.where(c < n_ctx_chunks, n_ctx_chunks - 1 - c, n_chunks + n_ctx_chunks - 1 - c)
    return jnp.where(d == 0, c, bwd)


def _dir_tri(d, t):
    row = lax.broadcasted_iota(jnp.int32, (t, t), 0)
    col = lax.broadcasted_iota(jnp.int32, (t, t), 1)
    return jnp.where(d == 0, col - row, row - col) <= 0


def _gla_kernel(q_ref, k_ref, v_ref, r_ref, w2_ref, b2_ref, o_ref, st_ref, *, t, heads, scale):
    d = pl.program_id(0)
    c = pl.program_id(2)

    @pl.when(c == 0)
    def _():
        st_ref[...] = jnp.zeros_like(st_ref)

    dk = q_ref.shape[-1] // heads
    dv = v_ref.shape[-1] // heads
    nb = q_ref.shape[0]
    mask = _dir_tri(d, t)
    tri = mask.astype(F32)
    mid = t // 2
    items = [(bb, h) for bb in range(nb) for h in range(heads)]
    xs = [jnp.dot(r_ref[bb], w2_ref[0], preferred_element_type=F32, precision=HI) + b2_ref[0] for bb in range(nb)]
    las = [jax.nn.log_sigmoid(x) * (1.0 / GLA_TAU) for x in xs]
    bs = [jnp.dot(tri, la, preferred_element_type=F32, precision=HI) for la in las]
    qt, kt, qe, kh_end, e_end = [], [], [], [], []
    for bb in range(nb):
        b = bs[bb]
        b_m = b[mid:mid + 1, :]
        b_end = jnp.where(d == 0, b[t - 1:t, :], b[0:1, :])
        q_s = q_ref[bb] * (jnp.exp(b - b_m) * scale)
        k_s = k_ref[bb] * jnp.exp(b_m - b)
        qe.append((q_s * jnp.exp(b_m)).astype(BF16))
        kh_end.append((k_s * jnp.exp(b_end - b_m)).astype(BF16))
        e_end.append(jnp.exp(b_end))
        qt.append(q_s.astype(BF16))
        kt.append(k_s.astype(BF16))
    att, q_st, vs = {}, {}, {}
    for bb, h in items:
        ks = slice(h * dk, (h + 1) * dk)
        vs[bb, h] = v_ref[bb, :, h * dv:(h + 1) * dv].astype(BF16)
        att[bb, h] = lax.dot_general(qt[bb][:, ks], kt[bb][:, ks], NT, preferred_element_type=F32)
        q_st[bb, h] = lax.dot_general(qe[bb][:, ks], st_ref[bb * heads + h].astype(BF16), NT,
                                      preferred_element_type=F32)
    for bb, h in items:
        a = jnp.where(mask, att[bb, h], 0.0).astype(BF16)
        o_ref[0, bb, :, h * dv:(h + 1) * dv] = jnp.dot(a, vs[bb, h], preferred_element_type=F32) + q_st[bb, h]
    for bb, h in items:
        ks = slice(h * dk, (h + 1) * dk)
        upd = lax.dot_general(vs[bb, h], kh_end[bb][:, ks], TN, preferred_element_type=F32)
        st_ref[bb * heads + h] = st_ref[bb * heads + h] * e_end[bb][:, ks] + upd


def _gla_mixer(q, k, v, r, w2, b2, n_ctx):
    bsz, lt, dkt = q.shape
    dvt = v.shape[-1]
    nr = r.shape[-1]
    t, heads = MIX_CHUNK, GLA_HEADS
    bt = MIX_BATCH
    nch = lt // t
    ncc = n_ctx // t
    dk = dkt // heads
    dv = dvt // heads
    imap = lambda d, b, c: (b, _chunk_order(d, c, ncc, nch), 0)
    return pl.pallas_call(
        functools.partial(_gla_kernel, t=t, heads=heads, scale=dk ** -0.5),
        grid=(2, bsz // bt, nch),
        in_specs=[
            pl.BlockSpec((bt, t, dkt), imap),
            pl.BlockSpec((bt, t, dkt), imap),
            pl.BlockSpec((bt, t, dvt), imap),
            pl.BlockSpec((bt, t, nr), imap),
            pl.BlockSpec((1, nr, dkt), lambda d, b, c: (d, 0, 0)),
            pl.BlockSpec((1, 1, dkt), lambda d, b, c: (d, 0, 0)),
        ],
        out_specs=pl.BlockSpec((1, bt, t, dvt), lambda d, b, c: (d, b, _chunk_order(d, c, ncc, nch), 0)),
        out_shape=jax.ShapeDtypeStruct((2, bsz, lt, dvt), F32),
        scratch_shapes=[pltpu.VMEM((bt * heads, dv, dk), F32)],
        compiler_params=_cparams(("parallel", "parallel", "arbitrary")),
        name="gla_mixer",
    )(q, k, v, r, w2, b2)


def _mlstm_kernel(q_ref, k_ref, v_ref, gc_ref, gr_ref, o_ref, c_ref, n_ref, m_ref, *, t, heads):
    d = pl.program_id(0)
    c = pl.program_id(2)

    @pl.when(c == 0)
    def _():
        c_ref[...] = jnp.zeros_like(c_ref)
        n_ref[...] = jnp.zeros_like(n_ref)
        m_ref[...] = jnp.zeros_like(m_ref)

    dh = q_ref.shape[-1] // heads
    nb = q_ref.shape[0]
    mask = _dir_tri(d, t)
    tri = mask.astype(F32)
    items = [(bb, h) for bb in range(nb) for h in range(heads)]
    gate = []
    for bb in range(nb):
        gc = gc_ref[0, bb]
        gr = gr_ref[0, bb, 0]
        fc = jax.nn.log_sigmoid(gc[:, heads:])
        fr = jax.nn.log_sigmoid(gr[heads:, :])
        b_col = jnp.dot(tri, fc, preferred_element_type=F32, precision=HI)
        b_row = lax.dot_general(fr, tri, NT, preferred_element_type=F32, precision=HI)
        b_last = jnp.where(d == 0, b_col[t - 1:t, :], b_col[0:1, :])
        gate.append((gc[:, :heads], gr[:heads, :], b_col, b_row, b_last))
    qs, ks, vs, s_raw, q_c = {}, {}, {}, {}, {}
    for bb, h in items:
        hs = slice(h * dh, (h + 1) * dh)
        qs[bb, h] = q_ref[bb, :, hs]
        ks[bb, h] = k_ref[bb, :, hs]
        vs[bb, h] = v_ref[bb, :, hs].astype(BF16)
        s_raw[bb, h] = lax.dot_general(qs[bb, h], ks[bb, h], NT, preferred_element_type=F32)
        q_c[bb, h] = jnp.dot(qs[bb, h], c_ref[bb * heads + h].astype(BF16), preferred_element_type=F32)
    logw, log_inter, m_t, w_inter, scores, den, qn = {}, {}, {}, {}, {}, {}, {}
    for bb, h in items:
        _, ir, b_col, b_row, _ = gate[bb]
        bc = b_col[:, h:h + 1]
        logw[bb, h] = jnp.where(mask, bc - b_row[h:h + 1, :] + ir[h:h + 1, :], -jnp.inf)
        log_inter[bb, h] = bc + m_ref[bb * heads + h]
    for bb, h in items:
        m_t[bb, h] = jnp.maximum(log_inter[bb, h], jnp.max(logw[bb, h], axis=-1, keepdims=True))
        qn[bb, h] = jnp.sum(qs[bb, h].astype(F32) * n_ref[bb * heads + h], axis=-1, keepdims=True)
    for bb, h in items:
        w_inter[bb, h] = jnp.exp(log_inter[bb, h] - m_t[bb, h])
        scores[bb, h] = s_raw[bb, h] * jnp.exp(logw[bb, h] - m_t[bb, h])
    for bb, h in items:
        den[bb, h] = jnp.sum(scores[bb, h], axis=-1, keepdims=True) + w_inter[bb, h] * qn[bb, h]
    num = {}
    for bb, h in items:
        num[bb, h] = (jnp.dot(scores[bb, h].astype(BF16), vs[bb, h], preferred_element_type=F32)
                      + w_inter[bb, h] * q_c[bb, h])
    for bb, h in items:
        hs = slice(h * dh, (h + 1) * dh)
        o_ref[0, bb, :, hs] = num[bb, h] / jnp.maximum(jnp.abs(den[bb, h]), jnp.exp(-m_t[bb, h]))
    log_g, m_new, kw, upd, ksum = {}, {}, {}, {}, {}
    for bb, h in items:
        ic, _, b_col, _, b_last = gate[bb]
        log_g[bb, h] = b_last[:, h:h + 1] - b_col[:, h:h + 1] + ic[:, h:h + 1]
    for bb, h in items:
        b_last = gate[bb][4]
        m_new[bb, h] = jnp.maximum(b_last[:, h:h + 1] + m_ref[bb * heads + h],
                                   jnp.max(log_g[bb, h], axis=0, keepdims=True))
    for bb, h in items:
        kw[bb, h] = ks[bb, h].astype(F32) * jnp.exp(log_g[bb, h] - m_new[bb, h])
    for bb, h in items:
        upd[bb, h] = lax.dot_general(kw[bb, h].astype(BF16), vs[bb, h], TN, preferred_element_type=F32)
        ksum[bb, h] = jnp.sum(kw[bb, h], axis=0, keepdims=True)
    for bb, h in items:
        si = bb * heads + h
        b_last = gate[bb][4]
        keep = jnp.exp(b_last[:, h:h + 1] + m_ref[si] - m_new[bb, h])
        c_ref[si] = keep * c_ref[si] + upd[bb, h]
        n_ref[si] = keep * n_ref[si] + ksum[bb, h]
        m_ref[si] = m_new[bb, h]


def _mlstm_mixer(qk, v, gates, n_ctx):
    bsz, lt, w2 = qk.shape
    w = w2 // 2
    t, heads = MIX_CHUNK, MLSTM_HEADS
    dh = w // heads
    nch = lt // t
    ncc = n_ctx // t
    gc = gates.reshape(bsz, lt, 2, 2 * heads).transpose(2, 0, 1, 3)
    gr = gc.reshape(2, bsz, nch, t, 2 * heads).transpose(0, 1, 2, 4, 3)
    cmap = lambda d, b, c: _chunk_order(d, c, ncc, nch)
    bt = MIX_BATCH
    return pl.pallas_call(
        functools.partial(_mlstm_kernel, t=t, heads=heads),
        grid=(2, bsz // bt, nch),
        in_specs=[
            pl.BlockSpec((bt, t, w), lambda d, b, c: (b, cmap(d, b, c), 0)),
            pl.BlockSpec((bt, t, w), lambda d, b, c: (b, cmap(d, b, c), 1)),
            pl.BlockSpec((bt, t, w), lambda d, b, c: (b, cmap(d, b, c), 0)),
            pl.BlockSpec((1, bt, t, 2 * heads), lambda d, b, c: (d, b, cmap(d, b, c), 0)),
            pl.BlockSpec((1, bt, 1, 2 * heads, t), lambda d, b, c: (d, b, cmap(d, b, c), 0, 0)),
        ],
        out_specs=pl.BlockSpec((1, bt, t, w), lambda d, b, c: (d, b, cmap(d, b, c), 0)),
        out_shape=jax.ShapeDtypeStruct((2, bsz, lt, w), F32),
        scratch_shapes=[pltpu.VMEM((bt * heads, dh, dh), F32), pltpu.VMEM((bt * heads, 1, dh), F32),
                        pltpu.VMEM((bt * heads, 1, 1), F32)],
        compiler_params=_cparams(("parallel", "parallel", "arbitrary")),
        name="mlstm_mixer",
    )(qk, qk, v, gc, gr)


def _conv_kernel(x_ref, w_ref, b_ref, s_ref, o_ref, *, n_ctx):
    x = x_ref[0]
    lt = x.shape[0]
    row = lax.broadcasted_iota(jnp.int32, x.shape, 0)
    prev = jnp.where((row == 0) | (row == n_ctx), 0.0, pltpu.roll(x, 1, 0))
    nxt = jnp.where((row == n_ctx - 1) | (row == lt - 1), 0.0, pltpu.roll(x, lt - 1, 0))
    y = b_ref[...] + w_ref[0:1, :] * prev + w_ref[1:2, :] * x + w_ref[2:3, :] * nxt
    o_ref[0] = (y * jax.nn.sigmoid(y) * s_ref[...]).astype(o_ref.dtype)


def _conv_silu(x, w, b, colscale, n_ctx):
    bsz, lt, ch = x.shape
    tc = 256
    return pl.pallas_call(
        functools.partial(_conv_kernel, n_ctx=n_ctx),
        grid=(bsz, ch // tc),
        in_specs=[
            pl.BlockSpec((1, lt, tc), lambda b, j: (b, 0, j)),
            pl.BlockSpec((3, tc), lambda b, j: (0, j)),
            pl.BlockSpec((1, tc), lambda b, j: (0, j)),
            pl.BlockSpec((1, tc), lambda b, j: (0, j)),
        ],
        out_specs=pl.BlockSpec((1, lt, tc), lambda b, j: (b, 0, j)),
        out_shape=jax.ShapeDtypeStruct((bsz, lt, ch), BF16),
        compiler_params=_cparams(("parallel", "parallel")),
        name="conv_silu",
    )(x, w, b.reshape(1, ch), colscale.reshape(1, ch))


def _s5_matrices(a_re, a_im, log_dt, b_re, b_im, c_re, c_im, backward, lane_groups=8):
    g, p = a_re.shape
    cg = b_re.shape[-1]
    j = S5_J
    lg = lane_groups
    nq = g // lg
    dt = jnp.exp(log_dt)[:, None]
    lam_re = jnp.minimum(a_re, -1e-4)
    lam_im = a_im
    decay = jnp.exp(lam_re * dt)
    ab_re = decay * jnp.cos(lam_im * dt)
    ab_im = decay * jnp.sin(lam_im * dt)
    den = lam_re * lam_re + lam_im * lam_im
    zr = ((ab_re - 1) * lam_re + ab_im * lam_im) / den
    zi = (ab_im * lam_re - (ab_re - 1) * lam_im) / den
    bb_re = zr[..., None] * b_re - zi[..., None] * b_im
    bb_im = zr[..., None] * b_im + zi[..., None] * b_re
    pw_re, pw_im = [jnp.ones_like(ab_re)], [jnp.zeros_like(ab_im)]
    for _ in range(j):
        r0, i0 = pw_re[-1], pw_im[-1]
        pw_re.append(ab_re * r0 - ab_im * i0)
        pw_im.append(ab_re * i0 + ab_im * r0)
    pw_re, pw_im = jnp.stack(pw_re), jnp.stack(pw_im)
    ca_re = c_re[None] * pw_re[:, :, None, :] - c_im[None] * pw_im[:, :, None, :]
    ca_im = c_re[None] * pw_im[:, :, None, :] + c_im[None] * pw_re[:, :, None, :]
    kk = (jnp.einsum('tgcp,gpd->tgcd', ca_re[:j], bb_re, precision=HI)
          - jnp.einsum('tgcp,gpd->tgcd', ca_im[:j], bb_im, precision=HI))
    ab_pw_re = pw_re[:j, :, :, None] * bb_re[None] - pw_im[:j, :, :, None] * bb_im[None]
    ab_pw_im = pw_re[:j, :, :, None] * bb_im[None] + pw_im[:j, :, :, None] * bb_re[None]
    jj = jnp.arange(j)
    lag = (jj[:, None] - jj[None, :]) if backward else (jj[None, :] - jj[:, None])
    kt = jnp.where((lag >= 0)[:, :, None, None, None], kk[jnp.clip(lag, 0, j - 1)], 0.0)
    eye = jnp.eye(lg, dtype=F32)
    kt = kt.reshape(j, j, nq, lg, cg, cg)
    ktoep = jnp.einsum('ioqgcd,gh->qigdohc', kt, eye).reshape(nq, j * lg * cg, j * lg * cg)
    tau_in = jj if backward else (j - 1 - jj)
    wi_re = ab_pw_re[tau_in].reshape(j, nq, lg, p, cg)
    wi_im = ab_pw_im[tau_in].reshape(j, nq, lg, p, cg)
    win_re = jnp.einsum('jqgpc,gh->qjgchp', wi_re, eye).reshape(nq, j * lg * cg, lg * p)
    win_im = jnp.einsum('jqgpc,gh->qjgchp', wi_im, eye).reshape(nq, j * lg * cg, lg * p)
    tau_out = (j - jj) if backward else (jj + 1)
    wo_re = ca_re[tau_out].reshape(j, nq, lg, cg, p)
    wo_im = ca_im[tau_out].reshape(j, nq, lg, cg, p)
    wout_re = jnp.einsum('jqgcp,gh->qgpjhc', wo_re, eye).reshape(nq, lg * p, j * lg * cg)
    wout_im = -jnp.einsum('jqgcp,gh->qgpjhc', wo_im, eye).reshape(nq, lg * p, j * lg * cg)
    dec_re = pw_re[j].reshape(nq, 1, lg * p)
    dec_im = pw_im[j].reshape(nq, 1, lg * p)
    return (ktoep.astype(BF16), win_re.astype(BF16), win_im.astype(BF16),
            wout_re.astype(BF16), wout_im.astype(BF16), dec_re, dec_im)


def _s5_kernel(u_ref, kt_ref, wir_ref, wii_ref, wor_ref, woi_ref, dr_ref, di_ref, y_ref,
               xf_ref, yf_ref, sre_ref, sim_ref, *, bt, nk, nk_ctx, rs):
    d = pl.program_id(0)
    j = S5_J
    lanes = u_ref.shape[-1]
    for b in range(bt):
        for jj in range(j):
            xf_ref[b * nk:(b + 1) * nk, jj * lanes:(jj + 1) * lanes] = (
                u_ref.at[b][pl.ds(jj, nk, stride=j), :].astype(BF16))
    xf = xf_ref[...]
    yf_ref[...] = jnp.dot(xf, kt_ref[0, 0], preferred_element_type=F32)
    inc_re = jnp.dot(xf, wir_ref[0, 0], preferred_element_type=F32)
    inc_im = jnp.dot(xf, wii_ref[0, 0], preferred_element_type=F32)
    nl = sre_ref.shape[0]
    for b in range(bt):
        for l in range(nl):
            sre_ref[l, b * rs:b * rs + nk, :] = inc_re[b * nk:(b + 1) * nk, l * lanes:(l + 1) * lanes]
            sim_ref[l, b * rs:b * rs + nk, :] = inc_im[b * nk:(b + 1) * nk, l * lanes:(l + 1) * lanes]
    a_re = [dr_ref[0, 0, :, l * lanes:(l + 1) * lanes] for l in range(nl)]
    a_im = [di_ref[0, 0, :, l * lanes:(l + 1) * lanes] for l in range(nl)]

    def step(kidx, carry):
        rows = pl.ds(kidx, bt, stride=rs)
        out = []
        for l in range(nl):
            s_re, s_im = carry[2 * l], carry[2 * l + 1]
            i_re = sre_ref.at[l][rows, :]
            i_im = sim_ref.at[l][rows, :]
            sre_ref.at[l][rows, :] = s_re
            sim_ref.at[l][rows, :] = s_im
            out.append(a_re[l] * s_re - a_im[l] * s_im + i_re)
            out.append(a_re[l] * s_im + a_im[l] * s_re + i_im)
        return tuple(out)

    zero = tuple(jnp.zeros((bt, lanes), F32) for _ in range(2 * nl))

    @pl.when(d == 0)
    def _():
        lax.fori_loop(0, nk, step, zero)

    @pl.when(d == 1)
    def _():
        carry = lax.fori_loop(0, nk_ctx, lambda i, cr: step(nk_ctx - 1 - i, cr), zero)
        lax.fori_loop(0, nk - nk_ctx, lambda i, cr: step(nk - 1 - i, cr), carry)

    for b in range(bt):
        sp_re = jnp.concatenate([sre_ref[l, b * rs:b * rs + nk, :] for l in range(nl)], axis=-1).astype(BF16)
        sp_im = jnp.concatenate([sim_ref[l, b * rs:b * rs + nk, :] for l in range(nl)], axis=-1).astype(BF16)
        yb = (yf_ref[b * nk:(b + 1) * nk, :]
              + jnp.dot(sp_re, wor_ref[0, 0], preferred_element_type=F32)
              + jnp.dot(sp_im, woi_ref[0, 0], preferred_element_type=F32))
        for jj in range(j):
            y_ref.at[0, b][pl.ds(jj, nk, stride=j), :] = yb[:, jj * lanes:(jj + 1) * lanes]


def _s5_mixer(u, mats, n_ctx):
    bsz, lt, w = u.shape
    ktoep, win_re, win_im, wout_re, wout_im, dec_re, dec_im = mats
    lanes = 128
    bt = 4 if bsz % 4 == 0 else 2
    nq = w // lanes
    j = S5_J
    nk = lt // j
    nk_ctx = n_ctx // j
    rs = nk + 8
    fl = j * lanes
    sw = win_re.shape[-1]
    wmap = lambda d, q, b: (d, q, 0, 0)
    return pl.pallas_call(
        functools.partial(_s5_kernel, bt=bt, nk=nk, nk_ctx=nk_ctx, rs=rs),
        grid=(2, nq, bsz // bt),
        in_specs=[
            pl.BlockSpec((bt, lt, lanes), lambda d, q, b: (b, 0, q)),
            pl.BlockSpec((1, 1, fl, fl), wmap),
            pl.BlockSpec((1, 1, fl, sw), wmap),
            pl.BlockSpec((1, 1, fl, sw), wmap),
            pl.BlockSpec((1, 1, sw, fl), wmap),
            pl.BlockSpec((1, 1, sw, fl), wmap),
            pl.BlockSpec((1, 1, 1, sw), wmap),
            pl.BlockSpec((1, 1, 1, sw), wmap),
        ],
        out_specs=pl.BlockSpec((1, bt, lt, lanes), lambda d, q, b: (d, b, 0, q)),
        out_shape=jax.ShapeDtypeStruct((2, bsz, lt, w), F32),
        scratch_shapes=[pltpu.VMEM((bt * nk, fl), BF16), pltpu.VMEM((bt * nk, fl), F32),
                        pltpu.VMEM((sw // lanes, bt * rs, lanes), F32),
                        pltpu.VMEM((sw // lanes, bt * rs, lanes), F32)],
        compiler_params=_cparams(("parallel", "parallel", "arbitrary")),
        name="s5_mixer",
    )(u, ktoep, win_re, win_im, wout_re, wout_im, dec_re, dec_im)


def _head_norm(x, heads):
    dh = x.shape[-1] // heads
    outs = []
    for h in range(heads):
        xh = x[:, h * dh:(h + 1) * dh]
        outs.append(xh * lax.rsqrt(jnp.mean(xh * xh, axis=-1, keepdims=True) + EPS))
    return jnp.concatenate(outs, axis=-1)


def _even_post_kernel(m_ref, o_ref, s_ref, u_ref, mg_ref, dsk_ref, gw_ref, gb_ref, w_ref, h_ref, gate_ref,
                      out_ref, *, heads):
    m = m_ref[0, 0] + m_ref[1, 0]
    m_out = _head_norm(m, heads) * mg_ref[...] * jax.nn.sigmoid(o_ref[0])
    y = jax.nn.gelu(s_ref[0, 0] + s_ref[1, 0] + dsk_ref[...] * u_ref[0])
    glu = jnp.dot(y.astype(BF16), gw_ref[...], preferred_element_type=F32) + gb_ref[...]
    s_out = y * jax.nn.sigmoid(glu)
    cat = jnp.concatenate([m_out, s_out], axis=-1).astype(BF16)
    z = jnp.dot(cat, w_ref[...], preferred_element_type=F32)
    out_ref[0] = h_ref[0] + gate_ref[0, 0] * z


def _even_post(m2, o, s2, u, mnorm_g, d_skip, glu_w, glu_b, w_out, h, gate):
    bsz, lt, d = h.shape
    mw = o.shape[-1]
    sw = u.shape[-1]
    tm = ROW_TILE
    row = lambda b, i: (b, i, 0)
    row2 = lambda b, i: (0, b, i, 0)
    const = lambda b, i: (0, 0)
    return pl.pallas_call(
        functools.partial(_even_post_kernel, heads=MLSTM_HEADS),
        grid=(bsz, lt // tm),
        in_specs=[
            pl.BlockSpec((2, 1, tm, mw), row2),
            pl.BlockSpec((1, tm, mw), row),
            pl.BlockSpec((2, 1, tm, sw), row2),
            pl.BlockSpec((1, tm, sw), row),
            pl.BlockSpec((1, mw), const),
            pl.BlockSpec((1, sw), const),
            pl.BlockSpec((sw, sw), const),
            pl.BlockSpec((1, sw), const),
            pl.BlockSpec((mw + sw, d), const),
            pl.BlockSpec((1, tm, d), row),
            pl.BlockSpec((1, 1, 1, d), _seg_map),
        ],
        out_specs=pl.BlockSpec((1, tm, d), row),
        out_shape=jax.ShapeDtypeStruct((bsz, lt, d), F32),
        compiler_params=_cparams(("parallel", "parallel")),
        name="even_post",
    )(m2, o, s2, u, mnorm_g.reshape(1, mw), d_skip.reshape(1, sw), glu_w.astype(BF16), glu_b.reshape(1, sw),
      w_out.astype(BF16), h, gate)


def _odd_post_kernel(o_ref, g_ref, ng_ref, w_ref, h_ref, gate_ref, out_ref, *, heads):
    g = g_ref[0]
    y = _head_norm(o_ref[0, 0] + o_ref[1, 0], heads) * ng_ref[...] * (g * jax.nn.sigmoid(g))
    z = jnp.dot(y.astype(BF16), w_ref[...], preferred_element_type=F32)
    out_ref[0] = h_ref[0] + gate_ref[0, 0] * z


def _odd_post(o, g, norm_g, w_out, h, gate):
    bsz, lt, d = h.shape
    dv = o.shape[-1]
    tm = ROW_TILE
    row = lambda b, i: (b, i, 0)
    const = lambda b, i: (0, 0)
    return pl.pallas_call(
        functools.partial(_odd_post_kernel, heads=GLA_HEADS),
        grid=(bsz, lt // tm),
        in_specs=[
            pl.BlockSpec((2, 1, tm, dv), lambda b, i: (0, b, i, 0)),
            pl.BlockSpec((1, tm, dv), row),
            pl.BlockSpec((1, dv), const),
            pl.BlockSpec((dv, d), const),
            pl.BlockSpec((1, tm, d), row),
            pl.BlockSpec((1, 1, 1, d), _seg_map),
        ],
        out_specs=pl.BlockSpec((1, tm, d), row),
        out_shape=jax.ShapeDtypeStruct((bsz, lt, d), F32),
        compiler_params=_cparams(("parallel", "parallel")),
        name="odd_post",
    )(o, g, norm_g.reshape(1, dv), w_out.astype(BF16), h, gate)


def kernel(x, c, ctx, c_ctx, mod_w, mod_b, norm_mix_g, norm_ffn_g, ev_w_in, ev_b_in, ev_conv_w, ev_conv_b, ev_mlstm_norm_g, ev_s5_a_re_f, ev_s5_a_im_f, ev_s5_log_dt_f, ev_s5_a_re_b, ev_s5_a_im_b, ev_s5_log_dt_b, ev_s5_b_re, ev_s5_b_im, ev_s5_c_re, ev_s5_c_im, ev_s5_d, ev_s5_glu_w, ev_s5_glu_b, ev_w_out, od_w_in, od_gate_w2_f, od_gate_b2_f, od_gate_w2_b, od_gate_b2_b, od_norm_g, od_w_out, router_w, router_b, moe_w_gu, moe_b_gu, moe_w_down, moe_b_down, final_norm_g):
    bsz, seq, d = x.shape
    n_ctx = ctx.shape[1]
    depth = mod_w.shape[0]
    lt = n_ctx + seq
    assert n_ctx == ROW_TILE and seq % ROW_TILE == 0 and seq % GRID_W == 0

    h = jnp.concatenate([ctx, x], axis=1)
    c_all = jnp.concatenate([c, c_ctx[None, :]], axis=0)
    c_all = jnp.pad(c_all, ((0, (-c_all.shape[0]) % 8), (0, 0)))
    mods = _modulation(c_all, mod_w, mod_b)
    mod_lat = mods[:, :bsz]
    mod_ctx = jnp.broadcast_to(mods[:, bsz:bsz + 1], mod_lat.shape)
    mod6 = jnp.stack([mod_ctx, mod_lat], axis=2).reshape(depth, bsz, 2, 6, 1, d)

    bg_all = moe_b_gu[..., 0::2]
    bu_all = moe_b_gu[..., 1::2]

    mw = ev_conv_w.shape[-1] // 2
    n_gates = 4 * MLSTM_HEADS
    s5w = ev_s5_d.shape[-1]
    dk_t = od_gate_w2_f.shape[-1]
    dv_t = od_norm_g.shape[-1]
    rows = seq // GRID_W

    def to_cols(a):
        lat = a[:, n_ctx:].reshape(bsz, rows, GRID_W, -1).transpose(0, 2, 1, 3).reshape(bsz, seq, -1)
        return jnp.concatenate([a[:, :n_ctx], lat], axis=1)

    def to_rows(a):
        lat = a[:, n_ctx:].reshape(bsz, GRID_W, rows, -1).transpose(0, 2, 1, 3).reshape(bsz, seq, -1)
        return jnp.concatenate([a[:, :n_ctx], lat], axis=1)

    for layer in range(depth):
        last = layer == depth - 1
        j = layer // 2
        m6 = mod6[layer]
        sh1, sc1, g1, sh2, sc2, g2 = (m6[:, :, i] for i in range(6))
        if layer % 2 == 0:
            w_in, b_in = ev_w_in[j], ev_b_in[j]
            cols = jnp.concatenate([jnp.arange(0, 4 * mw), jnp.arange(4 * mw + n_gates, 4 * mw + n_gates + s5w),
                                    jnp.arange(4 * mw, 4 * mw + n_gates)])
            qk_pre, v, o, u, gates = _nm_matmul(h, norm_mix_g[layer], sh1, sc1, w_in[:, cols], b_in[cols],
                                                (2 * mw, mw, mw, s5w, n_gates))
            dh = mw // MLSTM_HEADS
            colscale = jnp.concatenate([jnp.full((mw,), dh ** -0.5, F32), jnp.ones((mw,), F32)])
            qk = _conv_silu(qk_pre, ev_conv_w[j], ev_conv_b[j], colscale, n_ctx)
            m2 = _mlstm_mixer(qk, v, gates, n_ctx)
            shared = (ev_s5_b_re[j], ev_s5_b_im[j], ev_s5_c_re[j], ev_s5_c_im[j])
            mats_f = _s5_matrices(ev_s5_a_re_f[j], ev_s5_a_im_f[j], ev_s5_log_dt_f[j], *shared, backward=False)
            mats_b = _s5_matrices(ev_s5_a_re_b[j], ev_s5_a_im_b[j], ev_s5_log_dt_b[j], *shared, backward=True)
            s2 = _s5_mixer(u, tuple(jnp.stack([a, b]) for a, b in zip(mats_f, mats_b)), n_ctx)
            h = _even_post(m2, o, s2, u, ev_mlstm_norm_g[j], ev_s5_d[j], ev_s5_glu_w[j], ev_s5_glu_b[j],
                           ev_w_out[j], h, g1)
        else:
            hc = to_cols(h)
            qq, kk, vv, gg, rr = _nm_matmul(hc, norm_mix_g[layer], sh1, sc1, od_w_in[j],
                                            jnp.zeros((od_w_in.shape[-1],), F32),
                                            (dk_t, dk_t, dv_t, dv_t, 2 * GLA_RANK))
            zero = jnp.zeros_like(od_gate_w2_f[j])
            w2 = jnp.stack([jnp.concatenate([od_gate_w2_f[j], zero], axis=0),
                            jnp.concatenate([zero, od_gate_w2_b[j]], axis=0)])
            b2 = jnp.stack([od_gate_b2_f[j], od_gate_b2_b[j]])[:, None, :]
            o2 = _gla_mixer(qq, kk, vv, rr, w2, b2, n_ctx)
            h = to_rows(_odd_post(o2, gg, od_norm_g[j], od_w_out[j], hc, g1))
        f, logits = _ffn_prep(h, norm_ffn_g[layer], sh2, sc2, router_w[layer], router_b[layer])
        weights = (layer, moe_w_gu, moe_w_down, bg_all[layer], bu_all[layer], moe_b_down[layer])
        h = _moe_layer(h, f, logits, g2, weights, last, n_ctx)
    return _final_norm(h, final_norm_g)
```

```python
import functools

import jax
import jax.numpy as jnp
from jax import lax
from jax.experimental import pallas as pl
from jax.experimental.pallas import tpu as pltpu

F32 = jnp.float32
BF16 = jnp.bfloat16
HI = lax.Precision.HIGHEST

EPS = 1e-6
GRID_W = 64
MLSTM_HEADS = 4
S5_GROUP = 16
GLA_HEADS = 4
GLA_RANK = 16
GLA_TAU = 16.0
N_EXPERTS = 32
TOP_K = 4
SWIGLU_LIMIT = 7.0
SWIGLU_ALPHA = 1.702

ROW_TILE = 256
MOE_TILE = 512
MIX_CHUNK = 64
MIX_BATCH = 4
S5_J = 8
VMEM_LIMIT = 56 * 1024 * 1024

NT = (((1,), (1,)), ((), ()))
TN = (((0,), (0,)), ((), ()))


def _cparams(sem):
    return pltpu.CompilerParams(dimension_semantics=sem, vmem_limit_bytes=VMEM_LIMIT)


def _mod_kernel(c_ref, w_ref, b_ref, o_ref):
    c = c_ref[...]
    a = c * jax.nn.sigmoid(c)
    o_ref[0] = jnp.dot(a.astype(BF16), w_ref[0].astype(BF16), preferred_element_type=F32) + b_ref[0]


def _modulation(c_all, mod_w, mod_b):
    depth, d, n6 = mod_w.shape
    rows = c_all.shape[0]
    tn = d
    return pl.pallas_call(
        _mod_kernel,
        grid=(depth, n6 // tn),
        in_specs=[
            pl.BlockSpec((rows, d), lambda l, j: (0, 0)),
            pl.BlockSpec((1, d, tn), lambda l, j: (l, 0, j)),
            pl.BlockSpec((1, 1, tn), lambda l, j: (l, 0, j)),
        ],
        out_specs=pl.BlockSpec((1, rows, tn), lambda l, j: (l, 0, j)),
        out_shape=jax.ShapeDtypeStruct((depth, rows, n6), F32),
        compiler_params=_cparams(("arbitrary", "arbitrary")),
        name="modulation",
    )(c_all, mod_w, mod_b.reshape(depth, 1, n6))


def _norm_mod(x, g, sh, sc):
    ms = jnp.mean(x * x, axis=-1, keepdims=True)
    return (x * lax.rsqrt(ms + EPS) * g) * (1.0 + sc) + sh


def _nm_matmul_kernel(x_ref, g_ref, sh_ref, sc_ref, w_ref, b_ref, *out_refs, splits):
    a = _norm_mod(x_ref[0], g_ref[...], sh_ref[0, 0], sc_ref[0, 0])
    z = jnp.dot(a.astype(BF16), w_ref[...], preferred_element_type=F32) + b_ref[...]
    for (lo, hi), o_ref in zip(splits, out_refs):
        o_ref[0] = z[:, lo:hi].astype(o_ref.dtype)


def _seg_map(b, i):
    return (b, jnp.minimum(i, 1), 0, 0)


def _nm_matmul(h, g, shift, scale, w, bias, widths):
    bsz, lt, d = h.shape
    p = w.shape[1]
    splits, lo = [], 0
    for wd in widths:
        splits.append((lo, lo + wd))
        lo += wd
    assert lo == p
    tm = ROW_TILE
    return pl.pallas_call(
        functools.partial(_nm_matmul_kernel, splits=tuple(splits)),
        grid=(bsz, lt // tm),
        in_specs=[
            pl.BlockSpec((1, tm, d), lambda b, i: (b, i, 0)),
            pl.BlockSpec((1, d), lambda b, i: (0, 0)),
            pl.BlockSpec((1, 1, 1, d), _seg_map),
            pl.BlockSpec((1, 1, 1, d), _seg_map),
            pl.BlockSpec((d, p), lambda b, i: (0, 0)),
            pl.BlockSpec((1, p), lambda b, i: (0, 0)),
        ],
        out_specs=[pl.BlockSpec((1, tm, wd), lambda b, i: (b, i, 0)) for wd in widths],
        out_shape=[jax.ShapeDtypeStruct((bsz, lt, wd), F32) for wd in widths],
        compiler_params=_cparams(("parallel", "parallel")),
        name="norm_mod_matmul",
    )(h, g.reshape(1, d), shift, scale, w.astype(BF16), bias.reshape(1, p))


def _ffn_prep_kernel(x_ref, g_ref, sh_ref, sc_ref, rw_ref, rb_ref, f_ref, logit_ref):
    a = _norm_mod(x_ref[0], g_ref[...], sh_ref[0, 0], sc_ref[0, 0])
    f_ref[0] = a.astype(f_ref.dtype)
    logit_ref[0] = jnp.dot(a.astype(BF16), rw_ref[...].astype(BF16), preferred_element_type=F32) + rb_ref[...]


def _ffn_prep(h, g, shift, scale, router_w, router_b):
    bsz, lt, d = h.shape
    ne = router_w.shape[1]
    tm = ROW_TILE
    return pl.pallas_call(
        _ffn_prep_kernel,
        grid=(bsz, lt // tm),
        in_specs=[
            pl.BlockSpec((1, tm, d), lambda b, i: (b, i, 0)),
            pl.BlockSpec((1, d), lambda b, i: (0, 0)),
            pl.BlockSpec((1, 1, 1, d), _seg_map),
            pl.BlockSpec((1, 1, 1, d), _seg_map),
            pl.BlockSpec((d, ne), lambda b, i: (0, 0)),
            pl.BlockSpec((1, ne), lambda b, i: (0, 0)),
        ],
        out_specs=[
            pl.BlockSpec((1, tm, d), lambda b, i: (b, i, 0)),
            pl.BlockSpec((1, tm, ne), lambda b, i: (b, i, 0)),
        ],
        out_shape=[
            jax.ShapeDtypeStruct((bsz, lt, d), BF16),
            jax.ShapeDtypeStruct((bsz, lt, ne), F32),
        ],
        compiler_params=_cparams(("parallel", "parallel")),
        name="ffn_prep",
    )(h, g.reshape(1, d), shift, scale, router_w, router_b.reshape(1, ne))


GU_BLOCK = 256


def _moe_kernel(be_ref, nb_ref, x_ref, wgu_ref, wd_ref, bg_ref, bu_ref, bd_ref, o_ref, wgu_s, wd_s):
    i = pl.program_id(0)
    active = i < nb_ref[0]
    half = GU_BLOCK // 2
    nblk = wgu_s.shape[1] // GU_BLOCK

    @pl.when(active & ((i == 0) | (be_ref[i] != be_ref[jnp.maximum(i - 1, 0)])))
    def _():
        r = lax.broadcasted_iota(jnp.int32, (GU_BLOCK, GU_BLOCK), 0)
        c = lax.broadcasted_iota(jnp.int32, (GU_BLOCK, GU_BLOCK), 1)
        perm = (r == jnp.where(c < half, 2 * c, 2 * (c - half) + 1)).astype(BF16)
        for k in range(nblk):
            cs = slice(k * GU_BLOCK, (k + 1) * GU_BLOCK)
            wgu_s[:, cs] = jnp.dot(wgu_ref[0, 0, :, cs].astype(BF16), perm,
                                   preferred_element_type=F32).astype(BF16)
        wd_s[...] = wd_ref[0, 0].astype(BF16)

    @pl.when(active)
    def _():
        gu = jnp.dot(x_ref[...], wgu_s[...], preferred_element_type=F32)
        hdn = []
        for k in range(nblk):
            hs = slice(k * half, (k + 1) * half)
            g = gu[:, k * GU_BLOCK:k * GU_BLOCK + half] + bg_ref[0, :, hs]
            u = gu[:, k * GU_BLOCK + half:(k + 1) * GU_BLOCK] + bu_ref[0, :, hs]
            g = jnp.minimum(g, SWIGLU_LIMIT)
            u = jnp.clip(u, -SWIGLU_LIMIT, SWIGLU_LIMIT)
            hdn.append(((u + 1.0) * (g * jax.nn.sigmoid(SWIGLU_ALPHA * g))).astype(BF16))
        hdn = jnp.concatenate(hdn, axis=-1)
        o_ref[...] = (jnp.dot(hdn, wd_s[...], preferred_element_type=F32) + bd_ref[0]).astype(o_ref.dtype)

    @pl.when(jnp.logical_not(active))
    def _():
        o_ref[...] = jnp.zeros_like(o_ref)


def _moe_experts(x_sorted, block_expert, n_used, layer, w_gu, w_down, bg, bu, bd):
    n_rows, d = x_sorted.shape
    _, ne, _, f2 = w_gu.shape
    f = f2 // 2
    tm = MOE_TILE
    n_blocks = n_rows // tm
    assert f2 % GU_BLOCK == 0
    grid_spec = pltpu.PrefetchScalarGridSpec(
        num_scalar_prefetch=2,
        grid=(n_blocks,),
        in_specs=[
            pl.BlockSpec((tm, d), lambda i, be, nb: (i, 0)),
            pl.BlockSpec((1, 1, d, f2), lambda i, be, nb: (layer, be[i], 0, 0)),
            pl.BlockSpec((1, 1, f, d), lambda i, be, nb: (layer, be[i], 0, 0)),
            pl.BlockSpec((1, 1, f), lambda i, be, nb: (be[i], 0, 0)),
            pl.BlockSpec((1, 1, f), lambda i, be, nb: (be[i], 0, 0)),
            pl.BlockSpec((1, 1, d), lambda i, be, nb: (be[i], 0, 0)),
        ],
        out_specs=pl.BlockSpec((tm, d), lambda i, be, nb: (i, 0)),
        scratch_shapes=[pltpu.VMEM((d, f2), BF16), pltpu.VMEM((f, d), BF16)],
    )
    return pl.pallas_call(
        _moe_kernel,
        grid_spec=grid_spec,
        out_shape=jax.ShapeDtypeStruct((n_rows, d), BF16),
        compiler_params=_cparams(("arbitrary",)),
        name="moe_experts",
    )(block_expert, n_used, x_sorted, w_gu, w_down,
      bg.reshape(ne, 1, f), bu.reshape(ne, 1, f), bd.reshape(ne, 1, d))


def _combine_kernel(y_ref, gt_ref, h_ref, g2_ref, o_ref):
    gt = gt_ref[0]
    acc = y_ref[0, 0].astype(F32) * gt[:, 0:1]
    for k in range(1, TOP_K):
        acc = acc + y_ref[k, 0].astype(F32) * gt[:, k:k + 1]
    o_ref[0] = h_ref[0] + g2_ref[0, 0] * acc


def _moe_combine(yg, gate, h, g2, seg_map):
    k, bsz, lt, d = yg.shape
    tm = ROW_TILE
    return pl.pallas_call(
        _combine_kernel,
        grid=(bsz, lt // tm),
        in_specs=[
            pl.BlockSpec((k, 1, tm, d), lambda b, i: (0, b, i, 0)),
            pl.BlockSpec((1, tm, k), lambda b, i: (b, i, 0)),
            pl.BlockSpec((1, tm, d), lambda b, i: (b, i, 0)),
            pl.BlockSpec((1, 1, 1, d), seg_map),
        ],
        out_specs=pl.BlockSpec((1, tm, d), lambda b, i: (b, i, 0)),
        out_shape=jax.ShapeDtypeStruct((bsz, lt, d), F32),
        compiler_params=_cparams(("parallel", "parallel")),
        name="moe_combine",
    )(yg, gate, h, g2)


def _lat_seg_map(b, i):
    return (b, 1, 0, 0)


def _moe_layer(h, f, logits, g2, weights, lat_only, n_ctx):
    layer, w_gu, w_down, bg, bu, bd = weights
    bsz, lt, d = h.shape
    if lat_only:
        h_in, f, logits = h[:, n_ctx:], f[:, n_ctx:], logits[:, n_ctx:]
    else:
        h_in = h
    ltok = h_in.shape[1]
    n = bsz * ltok
    top_logit, top_e = lax.top_k(logits.reshape(n, -1), TOP_K)
    gate = jax.nn.softmax(top_logit, axis=-1)
    n_assign = n * TOP_K
    tm = MOE_TILE
    flat_e = top_e.reshape(-1).astype(jnp.int32)
    iota = jnp.arange(n_assign, dtype=jnp.int32)
    e_sorted, order = lax.sort_key_val(flat_e, iota, is_stable=True)
    experts = jnp.arange(N_EXPERTS, dtype=jnp.int32)
    start = jnp.searchsorted(e_sorted, experts, side='left', method='compare_all').astype(jnp.int32)
    counts = jnp.searchsorted(e_sorted, experts, side='right', method='compare_all').astype(jnp.int32) - start
    padded = (counts + tm - 1) // tm * tm
    pad_end = jnp.cumsum(padded)
    pad_start = pad_end - padded
    dest = pad_start[e_sorted] + iota - start[e_sorted]
    n_blocks = -(-(n_assign + N_EXPERTS * (tm - 1)) // tm)
    n_rows = n_blocks * tm
    block_expert = jnp.minimum(
        jnp.searchsorted(pad_end, jnp.arange(n_blocks, dtype=jnp.int32) * tm, side='right', method='compare_all'),
        N_EXPERTS - 1).astype(jnp.int32)
    n_used = (pad_end[-1] // tm).astype(jnp.int32).reshape(1)
    row = jnp.arange(n_rows, dtype=jnp.int32).reshape(n_blocks, tm)
    blk_shift = (start - pad_start)[block_expert][:, None]
    blk_end = (pad_start + counts)[block_expert][:, None]
    slot = jnp.clip(row + blk_shift, 0, n_assign - 1).reshape(-1)
    row_token = jnp.where((row < blk_end).reshape(-1),
                          order.at[slot].get(mode='promise_in_bounds') // TOP_K, 0)
    _, pos = lax.sort_key_val(order, dest, is_stable=True)
    x_sorted = f.reshape(n, d).at[row_token].get(mode='promise_in_bounds')
    y = _moe_experts(x_sorted, block_expert, n_used, layer, w_gu, w_down, bg, bu, bd)
    pos_k = pos.reshape(n, TOP_K).T
    yg = y.at[pos_k.reshape(-1)].get(mode='promise_in_bounds').reshape(TOP_K, bsz, ltok, d)
    out = _moe_combine(yg, gate.reshape(bsz, ltok, TOP_K), h_in, g2,
                       _lat_seg_map if lat_only else _seg_map)
    return out


def _rmsnorm_kernel(x_ref, g_ref, o_ref):
    x = x_ref[0]
    ms = jnp.mean(x * x, axis=-1, keepdims=True)
    o_ref[0] = x * lax.rsqrt(ms + EPS) * g_ref[...]


def _final_norm(h, g):
    bsz, lt, d = h.shape
    tm = ROW_TILE
    return pl.pallas_call(
        _rmsnorm_kernel,
        grid=(bsz, lt // tm),
        in_specs=[pl.BlockSpec((1, tm, d), lambda b, i: (b, i, 0)),
                  pl.BlockSpec((1, d), lambda b, i: (0, 0))],
        out_specs=pl.BlockSpec((1, tm, d), lambda b, i: (b, i, 0)),
        out_shape=jax.ShapeDtypeStruct((bsz, lt, d), F32),
        compiler_params=_cparams(("parallel", "parallel")),
        name="final_norm",
    )(h, g.reshape(1, d))


def _grid_reorder_kernel(x_ref, o_ref, *, n_ctx, rows, to_cols):
    o_ref[0, :n_ctx, :] = x_ref[0, :n_ctx, :]
    for c in range(GRID_W):
        raster = pl.ds(n_ctx + c, rows, stride=GRID_W)
        dense = pl.ds(n_ctx + c * rows, rows)
        if to_cols:
            o_ref.at[0][dense, :] = x_ref.at[0][raster, :]
        else:
            o_ref.at[0][raster, :] = x_ref.at[0][dense, :]


def _grid_reorder(h, n_ctx, to_cols):
    bsz, lt, d = h.shape
    lanes = 128
    spec = pl.BlockSpec((1, lt, lanes), lambda b, j: (b, 0, j))
    return pl.pallas_call(
        functools.partial(_grid_reorder_kernel, n_ctx=n_ctx, rows=(lt - n_ctx) // GRID_W, to_cols=to_cols),
        grid=(bsz, d // lanes),
        in_specs=[spec],
        out_specs=spec,
        out_shape=jax.ShapeDtypeStruct(h.shape, h.dtype),
        compiler_params=_cparams(("parallel", "parallel")),
        name="grid_reorder",
    )(h)


def _chunk_order(d, c, n_ctx_chunks, n_chunks):
    bwd = jnp.where(c < n_ctx_chunks, n_ctx_chunks - 1 - c, n_chunks + n_ctx_chunks - 1 - c)
    return jnp.where(d == 0, c, bwd)


def _dir_tri(d, t):
    row = lax.broadcasted_iota(jnp.int32, (t, t), 0)
    col = lax.broadcasted_iota(jnp.int32, (t, t), 1)
    return jnp.where(d == 0, col - row, row - col) <= 0


def _gla_kernel(q_ref, k_ref, v_ref, r_ref, w2_ref, b2_ref, o_ref, st_ref, *, t, heads, scale):
    d = pl.program_id(0)
    c = pl.program_id(2)

    @pl.when(c == 0)
    def _():
        st_ref[...] = jnp.zeros_like(st_ref)

    dk = q_ref.shape[-1] // heads
    dv = v_ref.shape[-1] // heads
    nb = q_ref.shape[0]
    mask = _dir_tri(d, t)
    tri = mask.astype(F32)
    mid = t // 2
    items = [(bb, h) for bb in range(nb) for h in range(heads)]
    xs = [jnp.dot(r_ref[bb], w2_ref[0], preferred_element_type=F32, precision=HI) + b2_ref[0] for bb in range(nb)]
    las = [jax.nn.log_sigmoid(x) * (1.0 / GLA_TAU) for x in xs]
    bs = [jnp.dot(tri, la, preferred_element_type=F32, precision=HI) for la in las]
    qt, kt, qe, kh_end, e_end = [], [], [], [], []
    for bb in range(nb):
        b = bs[bb]
        b_m = b[mid:mid + 1, :]
        b_end = jnp.where(d == 0, b[t - 1:t, :], b[0:1, :])
        q_s = q_ref[bb] * (jnp.exp(b - b_m) * scale)
        k_s = k_ref[bb] * jnp.exp(b_m - b)
        qe.append((q_s * jnp.exp(b_m)).astype(BF16))
        kh_end.append((k_s * jnp.exp(b_end - b_m)).astype(BF16))
        e_end.append(jnp.exp(b_end))
        qt.append(q_s.astype(BF16))
        kt.append(k_s.astype(BF16))
    att, q_st, vs = {}, {}, {}
    for bb, h in items:
        ks = slice(h * dk, (h + 1) * dk)
        vs[bb, h] = v_ref[bb, :, h * dv:(h + 1) * dv].astype(BF16)
        att[bb, h] = lax.dot_general(qt[bb][:, ks], kt[bb][:, ks], NT, preferred_element_type=F32)
        q_st[bb, h] = lax.dot_general(qe[bb][:, ks], st_ref[bb * heads + h].astype(BF16), NT,
                                      preferred_element_type=F32)
    for bb, h in items:
        a = jnp.where(mask, att[bb, h], 0.0).astype(BF16)
        o_ref[0, bb, :, h * dv:(h + 1) * dv] = jnp.dot(a, vs[bb, h], preferred_element_type=F32) + q_st[bb, h]
    for bb, h in items:
        ks = slice(h * dk, (h + 1) * dk)
        upd = lax.dot_general(vs[bb, h], kh_end[bb][:, ks], TN, preferred_element_type=F32)
        st_ref[bb * heads + h] = st_ref[bb * heads + h] * e_end[bb][:, ks] + upd


def _gla_mixer(q, k, v, r, w2, b2, n_ctx):
    bsz, lt, dkt = q.shape
    dvt = v.shape[-1]
    nr = r.shape[-1]
    t, heads = MIX_CHUNK, GLA_HEADS
    bt = MIX_BATCH
    nch = lt // t
    ncc = n_ctx // t
    dk = dkt // heads
    dv = dvt // heads
    imap = lambda d, b, c: (b, _chunk_order(d, c, ncc, nch), 0)
    return pl.pallas_call(
        functools.partial(_gla_kernel, t=t, heads=heads, scale=dk ** -0.5),
        grid=(2, bsz // bt, nch),
        in_specs=[
            pl.BlockSpec((bt, t, dkt), imap),
            pl.BlockSpec((bt, t, dkt), imap),
            pl.BlockSpec((bt, t, dvt), imap),
            pl.BlockSpec((bt, t, nr), imap),
            pl.BlockSpec((1, nr, dkt), lambda d, b, c: (d, 0, 0)),
            pl.BlockSpec((1, 1, dkt), lambda d, b, c: (d, 0, 0)),
        ],
        out_specs=pl.BlockSpec((1, bt, t, dvt), lambda d, b, c: (d, b, _chunk_order(d, c, ncc, nch), 0)),
        out_shape=jax.ShapeDtypeStruct((2, bsz, lt, dvt), F32),
        scratch_shapes=[pltpu.VMEM((bt * heads, dv, dk), F32)],
        compiler_params=_cparams(("parallel", "parallel", "arbitrary")),
        name="gla_mixer",
    )(q, k, v, r, w2, b2)


def _mlstm_kernel(q_ref, k_ref, v_ref, gc_ref, gr_ref, o_ref, c_ref, n_ref, m_ref, *, t, heads):
    d = pl.program_id(0)
    c = pl.program_id(2)

    @pl.when(c == 0)
    def _():
        c_ref[...] = jnp.zeros_like(c_ref)
        n_ref[...] = jnp.zeros_like(n_ref)
        m_ref[...] = jnp.zeros_like(m_ref)

    dh = q_ref.shape[-1] // heads
    nb = q_ref.shape[0]
    mask = _dir_tri(d, t)
    tri = mask.astype(F32)
    items = [(bb, h) for bb in range(nb) for h in range(heads)]
    gate = []
    for bb in range(nb):
        gc = gc_ref[0, bb]
        gr = gr_ref[0, bb, 0]
        fc = jax.nn.log_sigmoid(gc[:, heads:])
        fr = jax.nn.log_sigmoid(gr[heads:, :])
        b_col = jnp.dot(tri, fc, preferred_element_type=F32, precision=HI)
        b_row = lax.dot_general(fr, tri, NT, preferred_element_type=F32, precision=HI)
        b_last = jnp.where(d == 0, b_col[t - 1:t, :], b_col[0:1, :])
        gate.append((gc[:, :heads], gr[:heads, :], b_col, b_row, b_last))
    qs, ks, vs, s_raw, q_c = {}, {}, {}, {}, {}
    for bb, h in items:
        hs = slice(h * dh, (h + 1) * dh)
        qs[bb, h] = q_ref[bb, :, hs]
        ks[bb, h] = k_ref[bb, :, hs]
        vs[bb, h] = v_ref[bb, :, hs].astype(BF16)
        s_raw[bb, h] = lax.dot_general(qs[bb, h], ks[bb, h], NT, preferred_element_type=F32)
        q_c[bb, h] = jnp.dot(qs[bb, h], c_ref[bb * heads + h].astype(BF16), preferred_element_type=F32)
    logw, log_inter, m_t, w_inter, scores, den, qn = {}, {}, {}, {}, {}, {}, {}
    for bb, h in items:
        _, ir, b_col, b_row, _ = gate[bb]
        bc = b_col[:, h:h + 1]
        logw[bb, h] = jnp.where(mask, bc - b_row[h:h + 1, :] + ir[h:h + 1, :], -jnp.inf)
        log_inter[bb, h] = bc + m_ref[bb * heads + h]
    for bb, h in items:
        m_t[bb, h] = jnp.maximum(log_inter[bb, h], jnp.max(logw[bb, h], axis=-1, keepdims=True))
        qn[bb, h] = jnp.sum(qs[bb, h].astype(F32) * n_ref[bb * heads + h], axis=-1, keepdims=True)
    for bb, h in items:
        w_inter[bb, h] = jnp.exp(log_inter[bb, h] - m_t[bb, h])
        scores[bb, h] = s_raw[bb, h] * jnp.exp(logw[bb, h] - m_t[bb, h])
    for bb, h in items:
        den[bb, h] = jnp.sum(scores[bb, h], axis=-1, keepdims=True) + w_inter[bb, h] * qn[bb, h]
    num = {}
    for bb, h in items:
        num[bb, h] = (jnp.dot(scores[bb, h].astype(BF16), vs[bb, h], preferred_element_type=F32)
                      + w_inter[bb, h] * q_c[bb, h])
    for bb, h in items:
        hs = slice(h * dh, (h + 1) * dh)
        o_ref[0, bb, :, hs] = num[bb, h] / jnp.maximum(jnp.abs(den[bb, h]), jnp.exp(-m_t[bb, h]))
    log_g, m_new, kw, upd, ksum = {}, {}, {}, {}, {}
    for bb, h in items:
        ic, _, b_col, _, b_last = gate[bb]
        log_g[bb, h] = b_last[:, h:h + 1] - b_col[:, h:h + 1] + ic[:, h:h + 1]
    for bb, h in items:
        b_last = gate[bb][4]
        m_new[bb, h] = jnp.maximum(b_last[:, h:h + 1] + m_ref[bb * heads + h],
                                   jnp.max(log_g[bb, h], axis=0, keepdims=True))
    for bb, h in items:
        kw[bb, h] = ks[bb, h].astype(F32) * jnp.exp(log_g[bb, h] - m_new[bb, h])
    for bb, h in items:
        upd[bb, h] = lax.dot_general(kw[bb, h].astype(BF16), vs[bb, h], TN, preferred_element_type=F32)
        ksum[bb, h] = jnp.sum(kw[bb, h], axis=0, keepdims=True)
    for bb, h in items:
        si = bb * heads + h
        b_last = gate[bb][4]
        keep = jnp.exp(b_last[:, h:h + 1] + m_ref[si] - m_new[bb, h])
        c_ref[si] = keep * c_ref[si] + upd[bb, h]
        n_ref[si] = keep * n_ref[si] + ksum[bb, h]
        m_ref[si] = m_new[bb, h]


def _mlstm_mixer(qk, v, gates, n_ctx):
    bsz, lt, w2 = qk.shape
    w = w2 // 2
    t, heads = MIX_CHUNK, MLSTM_HEADS
    dh = w // heads
    nch = lt // t
    ncc = n_ctx // t
    gc = gates.reshape(bsz, lt, 2, 2 * heads).transpose(2, 0, 1, 3)
    gr = gc.reshape(2, bsz, nch, t, 2 * heads).transpose(0, 1, 2, 4, 3)
    cmap = lambda d, b, c: _chunk_order(d, c, ncc, nch)
    bt = MIX_BATCH
    return pl.pallas_call(
        functools.partial(_mlstm_kernel, t=t, heads=heads),
        grid=(2, bsz // bt, nch),
        in_specs=[
            pl.BlockSpec((bt, t, w), lambda d, b, c: (b, cmap(d, b, c), 0)),
            pl.BlockSpec((bt, t, w), lambda d, b, c: (b, cmap(d, b, c), 1)),
            pl.BlockSpec((bt, t, w), lambda d, b, c: (b, cmap(d, b, c), 0)),
            pl.BlockSpec((1, bt, t, 2 * heads), lambda d, b, c: (d, b, cmap(d, b, c), 0)),
            pl.BlockSpec((1, bt, 1, 2 * heads, t), lambda d, b, c: (d, b, cmap(d, b, c), 0, 0)),
        ],
        out_specs=pl.BlockSpec((1, bt, t, w), lambda d, b, c: (d, b, cmap(d, b, c), 0)),
        out_shape=jax.ShapeDtypeStruct((2, bsz, lt, w), F32),
        scratch_shapes=[pltpu.VMEM((bt * heads, dh, dh), F32), pltpu.VMEM((bt * heads, 1, dh), F32),
                        pltpu.VMEM((bt * heads, 1, 1), F32)],
        compiler_params=_cparams(("parallel", "parallel", "arbitrary")),
        name="mlstm_mixer",
    )(qk, qk, v, gc, gr)


def _conv_kernel(x_ref, w_ref, b_ref, s_ref, o_ref, *, n_ctx):
    x = x_ref[0]
    lt = x.shape[0]
    row = lax.broadcasted_iota(jnp.int32, x.shape, 0)
    prev = jnp.where((row == 0) | (row == n_ctx), 0.0, pltpu.roll(x, 1, 0))
    nxt = jnp.where((row == n_ctx - 1) | (row == lt - 1), 0.0, pltpu.roll(x, lt - 1, 0))
    y = b_ref[...] + w_ref[0:1, :] * prev + w_ref[1:2, :] * x + w_ref[2:3, :] * nxt
    o_ref[0] = (y * jax.nn.sigmoid(y) * s_ref[...]).astype(o_ref.dtype)


def _conv_silu(x, w, b, colscale, n_ctx):
    bsz, lt, ch = x.shape
    tc = 256
    return pl.pallas_call(
        functools.partial(_conv_kernel, n_ctx=n_ctx),
        grid=(bsz, ch // tc),
        in_specs=[
            pl.BlockSpec((1, lt, tc), lambda b, j: (b, 0, j)),
            pl.BlockSpec((3, tc), lambda b, j: (0, j)),
            pl.BlockSpec((1, tc), lambda b, j: (0, j)),
            pl.BlockSpec((1, tc), lambda b, j: (0, j)),
        ],
        out_specs=pl.BlockSpec((1, lt, tc), lambda b, j: (b, 0, j)),
        out_shape=jax.ShapeDtypeStruct((bsz, lt, ch), BF16),
        compiler_params=_cparams(("parallel", "parallel")),
        name="conv_silu",
    )(x, w, b.reshape(1, ch), colscale.reshape(1, ch))


def _s5_matrices(a_re, a_im, log_dt, b_re, b_im, c_re, c_im, backward, lane_groups=8):
    g, p = a_re.shape
    cg = b_re.shape[-1]
    j = S5_J
    lg = lane_groups
    nq = g // lg
    dt = jnp.exp(log_dt)[:, None]
    lam_re = jnp.minimum(a_re, -1e-4)
    lam_im = a_im
    decay = jnp.exp(lam_re * dt)
    ab_re = decay * jnp.cos(lam_im * dt)
    ab_im = decay * jnp.sin(lam_im * dt)
    den = lam_re * lam_re + lam_im * lam_im
    zr = ((ab_re - 1) * lam_re + ab_im * lam_im) / den
    zi = (ab_im * lam_re - (ab_re - 1) * lam_im) / den
    bb_re = zr[..., None] * b_re - zi[..., None] * b_im
    bb_im = zr[..., None] * b_im + zi[..., None] * b_re
    pw_re, pw_im = [jnp.ones_like(ab_re)], [jnp.zeros_like(ab_im)]
    for _ in range(j):
        r0, i0 = pw_re[-1], pw_im[-1]
        pw_re.append(ab_re * r0 - ab_im * i0)
        pw_im.append(ab_re * i0 + ab_im * r0)
    pw_re, pw_im = jnp.stack(pw_re), jnp.stack(pw_im)
    ca_re = c_re[None] * pw_re[:, :, None, :] - c_im[None] * pw_im[:, :, None, :]
    ca_im = c_re[None] * pw_im[:, :, None, :] + c_im[None] * pw_re[:, :, None, :]
    kk = (jnp.einsum('tgcp,gpd->tgcd', ca_re[:j], bb_re, precision=HI)
          - jnp.einsum('tgcp,gpd->tgcd', ca_im[:j], bb_im, precision=HI))
    ab_pw_re = pw_re[:j, :, :, None] * bb_re[None] - pw_im[:j, :, :, None] * bb_im[None]
    ab_pw_im = pw_re[:j, :, :, None] * bb_im[None] + pw_im[:j, :, :, None] * bb_re[None]
    eye = jnp.eye(lg, dtype=BF16)
    lb = lg * cg
    sw = lg * p
    bd_k = jnp.einsum('tqgcd,gh->tqgdhc', kk.astype(BF16).reshape(j, nq, lg, cg, cg), eye
                      ).reshape(j, nq, lb, lb)
    bd_in = [jnp.einsum('tqgpc,gh->tqgchp', a.astype(BF16).reshape(j, nq, lg, p, cg), eye
                        ).reshape(j, nq, lb, sw) for a in (ab_pw_re, ab_pw_im)]
    bd_out = [jnp.einsum('tqgcp,gh->tqgphc', a.astype(BF16).reshape(j + 1, nq, lg, cg, p), eye
                         ).reshape(j + 1, nq, sw, lb) for a in (ca_re, -ca_im)]
    zero = jnp.zeros((nq, lb, lb), BF16)
    lag = (lambda ji, jo: ji - jo) if backward else (lambda ji, jo: jo - ji)
    ktoep = jnp.concatenate(
        [jnp.concatenate([bd_k[lag(ji, jo)] if lag(ji, jo) >= 0 else zero for jo in range(j)], axis=-1)
         for ji in range(j)], axis=-2)
    tau_in = [ji if backward else j - 1 - ji for ji in range(j)]
    win_re, win_im = (jnp.concatenate([a[t] for t in tau_in], axis=-2) for a in bd_in)
    tau_out = [j - jo if backward else jo + 1 for jo in range(j)]
    wout_re, wout_im = (jnp.concatenate([a[t] for t in tau_out], axis=-1) for a in bd_out)
    dec_re = pw_re[j].reshape(nq, 1, sw)
    dec_im = pw_im[j].reshape(nq, 1, sw)
    return ktoep, win_re, win_im, wout_re, wout_im, dec_re, dec_im


def _s5_kernel(u_ref, kt_ref, wir_ref, wii_ref, wor_ref, woi_ref, dr_ref, di_ref, y_ref,
               xf_ref, yf_ref, sre_ref, sim_ref, *, bt, nk, nk_ctx, rs):
    d = pl.program_id(0)
    j = S5_J
    lanes = u_ref.shape[-1]
    for b in range(bt):
        for jj in range(j):
            xf_ref[b * nk:(b + 1) * nk, jj * lanes:(jj + 1) * lanes] = (
                u_ref.at[b][pl.ds(jj, nk, stride=j), :].astype(BF16))
    xf = xf_ref[...]
    yf_ref[...] = jnp.dot(xf, kt_ref[0, 0], preferred_element_type=F32)
    inc_re = jnp.dot(xf, wir_ref[0, 0], preferred_element_type=F32)
    inc_im = jnp.dot(xf, wii_ref[0, 0], preferred_element_type=F32)
    nl = sre_ref.shape[0]
    for b in range(bt):
        for l in range(nl):
            sre_ref[l, b * rs:b * rs + nk, :] = inc_re[b * nk:(b + 1) * nk, l * lanes:(l + 1) * lanes]
            sim_ref[l, b * rs:b * rs + nk, :] = inc_im[b * nk:(b + 1) * nk, l * lanes:(l + 1) * lanes]
    a_re = [dr_ref[0, 0, :, l * lanes:(l + 1) * lanes] for l in range(nl)]
    a_im = [di_ref[0, 0, :, l * lanes:(l + 1) * lanes] for l in range(nl)]

    def step(kidx, carry):
        rows = pl.ds(kidx, bt, stride=rs)
        out = []
        for l in range(nl):
            s_re, s_im = carry[2 * l], carry[2 * l + 1]
            i_re = sre_ref.at[l][rows, :]
            i_im = sim_ref.at[l][rows, :]
            sre_ref.at[l][rows, :] = s_re
            sim_ref.at[l][rows, :] = s_im
            out.append(a_re[l] * s_re - a_im[l] * s_im + i_re)
            out.append(a_re[l] * s_im + a_im[l] * s_re + i_im)
        return tuple(out)

    zero = tuple(jnp.zeros((bt, lanes), F32) for _ in range(2 * nl))

    @pl.when(d == 0)
    def _():
        lax.fori_loop(0, nk, step, zero)

    @pl.when(d == 1)
    def _():
        carry = lax.fori_loop(0, nk_ctx, lambda i, cr: step(nk_ctx - 1 - i, cr), zero)
        lax.fori_loop(0, nk - nk_ctx, lambda i, cr: step(nk - 1 - i, cr), carry)

    for b in range(bt):
        sp_re = jnp.concatenate([sre_ref[l, b * rs:b * rs + nk, :] for l in range(nl)], axis=-1).astype(BF16)
        sp_im = jnp.concatenate([sim_ref[l, b * rs:b * rs + nk, :] for l in range(nl)], axis=-1).astype(BF16)
        yb = (yf_ref[b * nk:(b + 1) * nk, :]
              + jnp.dot(sp_re, wor_ref[0, 0], preferred_element_type=F32)
              + jnp.dot(sp_im, woi_ref[0, 0], preferred_element_type=F32))
        for jj in range(j):
            y_ref.at[0, b][pl.ds(jj, nk, stride=j), :] = yb[:, jj * lanes:(jj + 1) * lanes]


def _s5_mixer(u, mats, n_ctx):
    bsz, lt, w = u.shape
    ktoep, win_re, win_im, wout_re, wout_im, dec_re, dec_im = mats
    lanes = 128
    bt = 4 if bsz % 4 == 0 else 2
    nq = w // lanes
    j = S5_J
    nk = lt // j
    nk_ctx = n_ctx // j
    rs = nk + 8
    fl = j * lanes
    sw = win_re.shape[-1]
    wmap = lambda d, q, b: (d, q, 0, 0)
    return pl.pallas_call(
        functools.partial(_s5_kernel, bt=bt, nk=nk, nk_ctx=nk_ctx, rs=rs),
        grid=(2, nq, bsz // bt),
        in_specs=[
            pl.BlockSpec((bt, lt, lanes), lambda d, q, b: (b, 0, q)),
            pl.BlockSpec((1, 1, fl, fl), wmap),
            pl.BlockSpec((1, 1, fl, sw), wmap),
            pl.BlockSpec((1, 1, fl, sw), wmap),
            pl.BlockSpec((1, 1, sw, fl), wmap),
            pl.BlockSpec((1, 1, sw, fl), wmap),
            pl.BlockSpec((1, 1, 1, sw), wmap),
            pl.BlockSpec((1, 1, 1, sw), wmap),
        ],
        out_specs=pl.BlockSpec((1, bt, lt, lanes), lambda d, q, b: (d, b, 0, q)),
        out_shape=jax.ShapeDtypeStruct((2, bsz, lt, w), F32),
        scratch_shapes=[pltpu.VMEM((bt * nk, fl), BF16), pltpu.VMEM((bt * nk, fl), F32),
                        pltpu.VMEM((sw // lanes, bt * rs, lanes), F32),
                        pltpu.VMEM((sw // lanes, bt * rs, lanes), F32)],
        compiler_params=_cparams(("parallel", "parallel", "arbitrary")),
        name="s5_mixer",
    )(u, ktoep, win_re, win_im, wout_re, wout_im, dec_re, dec_im)


def _head_norm(x, heads):
    dh = x.shape[-1] // heads
    outs = []
    for h in range(heads):
        xh = x[:, h * dh:(h + 1) * dh]
        outs.append(xh * lax.rsqrt(jnp.mean(xh * xh, axis=-1, keepdims=True) + EPS))
    return jnp.concatenate(outs, axis=-1)


def _even_post_kernel(m_ref, o_ref, s_ref, u_ref, mg_ref, dsk_ref, gw_ref, gb_ref, w_ref, h_ref, gate_ref,
                      out_ref, *, heads):
    m = m_ref[0, 0] + m_ref[1, 0]
    m_out = _head_norm(m, heads) * mg_ref[...] * jax.nn.sigmoid(o_ref[0])
    y = jax.nn.gelu(s_ref[0, 0] + s_ref[1, 0] + dsk_ref[...] * u_ref[0])
    glu = jnp.dot(y.astype(BF16), gw_ref[...], preferred_element_type=F32) + gb_ref[...]
    s_out = y * jax.nn.sigmoid(glu)
    cat = jnp.concatenate([m_out, s_out], axis=-1).astype(BF16)
    z = jnp.dot(cat, w_ref[...], preferred_element_type=F32)
    out_ref[0] = h_ref[0] + gate_ref[0, 0] * z


def _even_post(m2, o, s2, u, mnorm_g, d_skip, glu_w, glu_b, w_out, h, gate):
    bsz, lt, d = h.shape
    mw = o.shape[-1]
    sw = u.shape[-1]
    tm = ROW_TILE
    row = lambda b, i: (b, i, 0)
    row2 = lambda b, i: (0, b, i, 0)
    const = lambda b, i: (0, 0)
    return pl.pallas_call(
        functools.partial(_even_post_kernel, heads=MLSTM_HEADS),
        grid=(bsz, lt // tm),
        in_specs=[
            pl.BlockSpec((2, 1, tm, mw), row2),
            pl.BlockSpec((1, tm, mw), row),
            pl.BlockSpec((2, 1, tm, sw), row2),
            pl.BlockSpec((1, tm, sw), row),
            pl.BlockSpec((1, mw), const),
            pl.BlockSpec((1, sw), const),
            pl.BlockSpec((sw, sw), const),
            pl.BlockSpec((1, sw), const),
            pl.BlockSpec((mw + sw, d), const),
            pl.BlockSpec((1, tm, d), row),
            pl.BlockSpec((1, 1, 1, d), _seg_map),
        ],
        out_specs=pl.BlockSpec((1, tm, d), row),
        out_shape=jax.ShapeDtypeStruct((bsz, lt, d), F32),
        compiler_params=_cparams(("parallel", "parallel")),
        name="even_post",
    )(m2, o, s2, u, mnorm_g.reshape(1, mw), d_skip.reshape(1, sw), glu_w.astype(BF16), glu_b.reshape(1, sw),
      w_out.astype(BF16), h, gate)


def _odd_post_kernel(o_ref, g_ref, ng_ref, w_ref, h_ref, gate_ref, out_ref, *, heads):
    g = g_ref[0]
    y = _head_norm(o_ref[0, 0] + o_ref[1, 0], heads) * ng_ref[...] * (g * jax.nn.sigmoid(g))
    z = jnp.dot(y.astype(BF16), w_ref[...], preferred_element_type=F32)
    out_ref[0] = h_ref[0] + gate_ref[0, 0] * z


def _odd_post(o, g, norm_g, w_out, h, gate):
    bsz, lt, d = h.shape
    dv = o.shape[-1]
    tm = ROW_TILE
    row = lambda b, i: (b, i, 0)
    const = lambda b, i: (0, 0)
    return pl.pallas_call(
        functools.partial(_odd_post_kernel, heads=GLA_HEADS),
        grid=(bsz, lt // tm),
        in_specs=[
            pl.BlockSpec((2, 1, tm, dv), lambda b, i: (0, b, i, 0)),
            pl.BlockSpec((1, tm, dv), row),
            pl.BlockSpec((1, dv), const),
            pl.BlockSpec((dv, d), const),
            pl.BlockSpec((1, tm, d), row),
            pl.BlockSpec((1, 1, 1, d), _seg_map),
        ],
        out_specs=pl.BlockSpec((1, tm, d), row),
        out_shape=jax.ShapeDtypeStruct((bsz, lt, d), F32),
        compiler_params=_cparams(("parallel", "parallel")),
        name="odd_post",
    )(o, g, norm_g.reshape(1, dv), w_out.astype(BF16), h, gate)


def kernel(x, c, ctx, c_ctx, mod_w, mod_b, norm_mix_g, norm_ffn_g, ev_w_in, ev_b_in, ev_conv_w, ev_conv_b, ev_mlstm_norm_g, ev_s5_a_re_f, ev_s5_a_im_f, ev_s5_log_dt_f, ev_s5_a_re_b, ev_s5_a_im_b, ev_s5_log_dt_b, ev_s5_b_re, ev_s5_b_im, ev_s5_c_re, ev_s5_c_im, ev_s5_d, ev_s5_glu_w, ev_s5_glu_b, ev_w_out, od_w_in, od_gate_w2_f, od_gate_b2_f, od_gate_w2_b, od_gate_b2_b, od_norm_g, od_w_out, router_w, router_b, moe_w_gu, moe_b_gu, moe_w_down, moe_b_down, final_norm_g):
    bsz, seq, d = x.shape
    n_ctx = ctx.shape[1]
    depth = mod_w.shape[0]
    lt = n_ctx + seq
    assert n_ctx == ROW_TILE and seq % ROW_TILE == 0 and seq % GRID_W == 0

    h = jnp.concatenate([ctx, x], axis=1)
    c_all = jnp.concatenate([c, c_ctx[None, :]], axis=0)
    c_all = jnp.pad(c_all, ((0, (-c_all.shape[0]) % 8), (0, 0)))
    mods = _modulation(c_all, mod_w, mod_b)
    mod_lat = mods[:, :bsz]
    mod_ctx = jnp.broadcast_to(mods[:, bsz:bsz + 1], mod_lat.shape)
    mod6 = jnp.stack([mod_ctx, mod_lat], axis=2).reshape(depth, bsz, 2, 6, 1, d)

    bg_all = moe_b_gu[..., 0::2]
    bu_all = moe_b_gu[..., 1::2]

    mw = ev_conv_w.shape[-1] // 2
    n_gates = 4 * MLSTM_HEADS
    s5w = ev_s5_d.shape[-1]
    dk_t = od_gate_w2_f.shape[-1]
    dv_t = od_norm_g.shape[-1]
    for layer in range(depth):
        last = layer == depth - 1
        j = layer // 2
        m6 = mod6[layer]
        sh1, sc1, g1, sh2, sc2, g2 = (m6[:, :, i] for i in range(6))
        if layer % 2 == 0:
            w_in, b_in = ev_w_in[j], ev_b_in[j]
            cols = jnp.concatenate([jnp.arange(0, 4 * mw), jnp.arange(4 * mw + n_gates, 4 * mw + n_gates + s5w),
                                    jnp.arange(4 * mw, 4 * mw + n_gates)])
            qk_pre, v, o, u, gates = _nm_matmul(h, norm_mix_g[layer], sh1, sc1, w_in[:, cols], b_in[cols],
                                                (2 * mw, mw, mw, s5w, n_gates))
            dh = mw // MLSTM_HEADS
            colscale = jnp.concatenate([jnp.full((mw,), dh ** -0.5, F32), jnp.ones((mw,), F32)])
            qk = _conv_silu(qk_pre, ev_conv_w[j], ev_conv_b[j], colscale, n_ctx)
            m2 = _mlstm_mixer(qk, v, gates, n_ctx)
            shared = (ev_s5_b_re[j], ev_s5_b_im[j], ev_s5_c_re[j], ev_s5_c_im[j])
            mats_f = _s5_matrices(ev_s5_a_re_f[j], ev_s5_a_im_f[j], ev_s5_log_dt_f[j], *shared, backward=False)
            mats_b = _s5_matrices(ev_s5_a_re_b[j], ev_s5_a_im_b[j], ev_s5_log_dt_b[j], *shared, backward=True)
            s2 = _s5_mixer(u, tuple(jnp.stack([a, b]) for a, b in zip(mats_f, mats_b)), n_ctx)
            h = _even_post(m2, o, s2, u, ev_mlstm_norm_g[j], ev_s5_d[j], ev_s5_glu_w[j], ev_s5_glu_b[j],
                           ev_w_out[j], h, g1)
        else:
            hc = _grid_reorder(h, n_ctx, True)
            qq, kk, vv, gg, rr = _nm_matmul(hc, norm_mix_g[layer], sh1, sc1, od_w_in[j],
                                            jnp.zeros((od_w_in.shape[-1],), F32),
                                            (dk_t, dk_t, dv_t, dv_t, 2 * GLA_RANK))
            zero = jnp.zeros_like(od_gate_w2_f[j])
            w2 = jnp.stack([jnp.concatenate([od_gate_w2_f[j], zero], axis=0),
                            jnp.concatenate([zero, od_gate_w2_b[j]], axis=0)])
            b2 = jnp.stack([od_gate_b2_f[j], od_gate_b2_b[j]])[:, None, :]
            o2 = _gla_mixer(qq, kk, vv, rr, w2, b2, n_ctx)
            h = _grid_reorder(_odd_post(o2, gg, od_norm_g[j], od_w_out[j], hc, g1), n_ctx, False)
        f, logits = _ffn_prep(h, norm_ffn_g[layer], sh2, sc2, router_w[layer], router_b[layer])
        weights = (layer, moe_w_gu, moe_w_down, bg_all[layer], bu_all[layer], moe_b_down[layer])
        h = _moe_layer(h, f, logits, g2, weights, last, n_ctx)
    return _final_norm(h, final_norm_g)
```

```python
import functools

import jax
import jax.numpy as jnp
from jax import lax
from jax.experimental import pallas as pl
from jax.experimental.pallas import tpu as pltpu

F32 = jnp.float32
BF16 = jnp.bfloat16
HI = lax.Precision.HIGHEST

EPS = 1e-6
GRID_W = 64
MLSTM_HEADS = 4
S5_GROUP = 16
GLA_HEADS = 4
GLA_RANK = 16
GLA_TAU = 16.0
N_EXPERTS = 32
TOP_K = 4
SWIGLU_LIMIT = 7.0
SWIGLU_ALPHA = 1.702

ROW_TILE = 256
MOE_TILE = 512
MIX_CHUNK = 64
MIX_BATCH = 4
S5_J = 8
VMEM_LIMIT = 56 * 1024 * 1024

NT = (((1,), (1,)), ((), ()))
TN = (((0,), (0,)), ((), ()))


def _cparams(sem):
    return pltpu.CompilerParams(dimension_semantics=sem, vmem_limit_bytes=VMEM_LIMIT)


def _mod_kernel(c_ref, w_ref, b_ref, o_ref):
    c = c_ref[...]
    a = c * jax.nn.sigmoid(c)
    o_ref[0] = jnp.dot(a.astype(BF16), w_ref[0].astype(BF16), preferred_element_type=F32) + b_ref[0]


def _modulation(c_all, mod_w, mod_b):
    depth, d, n6 = mod_w.shape
    rows = c_all.shape[0]
    tn = d
    return pl.pallas_call(
        _mod_kernel,
        grid=(depth, n6 // tn),
        in_specs=[
            pl.BlockSpec((rows, d), lambda l, j: (0, 0)),
            pl.BlockSpec((1, d, tn), lambda l, j: (l, 0, j)),
            pl.BlockSpec((1, 1, tn), lambda l, j: (l, 0, j)),
        ],
        out_specs=pl.BlockSpec((1, rows, tn), lambda l, j: (l, 0, j)),
        out_shape=jax.ShapeDtypeStruct((depth, rows, n6), F32),
        compiler_params=_cparams(("arbitrary", "arbitrary")),
        name="modulation",
    )(c_all, mod_w, mod_b.reshape(depth, 1, n6))


def _norm_mod(x, g, sh, sc):
    ms = jnp.mean(x * x, axis=-1, keepdims=True)
    return (x * lax.rsqrt(ms + EPS) * g) * (1.0 + sc) + sh


def _nm_matmul_kernel(x_ref, g_ref, sh_ref, sc_ref, w_ref, b_ref, *out_refs, splits):
    a = _norm_mod(x_ref[0], g_ref[...], sh_ref[0, 0], sc_ref[0, 0])
    z = jnp.dot(a.astype(BF16), w_ref[...], preferred_element_type=F32) + b_ref[...]
    for (lo, hi), o_ref in zip(splits, out_refs):
        o_ref[0] = z[:, lo:hi].astype(o_ref.dtype)


def _seg_map(b, i):
    return (b, jnp.minimum(i, 1), 0, 0)


def _nm_matmul(h, g, shift, scale, w, bias, widths):
    bsz, lt, d = h.shape
    p = w.shape[1]
    splits, lo = [], 0
    for wd in widths:
        splits.append((lo, lo + wd))
        lo += wd
    assert lo == p
    tm = ROW_TILE
    return pl.pallas_call(
        functools.partial(_nm_matmul_kernel, splits=tuple(splits)),
        grid=(bsz, lt // tm),
        in_specs=[
            pl.BlockSpec((1, tm, d), lambda b, i: (b, i, 0)),
            pl.BlockSpec((1, d), lambda b, i: (0, 0)),
            pl.BlockSpec((1, 1, 1, d), _seg_map),
            pl.BlockSpec((1, 1, 1, d), _seg_map),
            pl.BlockSpec((d, p), lambda b, i: (0, 0)),
            pl.BlockSpec((1, p), lambda b, i: (0, 0)),
        ],
        out_specs=[pl.BlockSpec((1, tm, wd), lambda b, i: (b, i, 0)) for wd in widths],
        out_shape=[jax.ShapeDtypeStruct((bsz, lt, wd), F32) for wd in widths],
        compiler_params=_cparams(("parallel", "parallel")),
        name="norm_mod_matmul",
    )(h, g.reshape(1, d), shift, scale, w.astype(BF16), bias.reshape(1, p))


def _ffn_prep_kernel(x_ref, g_ref, sh_ref, sc_ref, rw_ref, rb_ref, f_ref, logit_ref):
    a = _norm_mod(x_ref[0], g_ref[...], sh_ref[0, 0], sc_ref[0, 0])
    f_ref[0] = a.astype(f_ref.dtype)
    logit_ref[0] = jnp.dot(a.astype(BF16), rw_ref[...].astype(BF16), preferred_element_type=F32) + rb_ref[...]


def _ffn_prep(h, g, shift, scale, router_w, router_b):
    bsz, lt, d = h.shape
    ne = router_w.shape[1]
    tm = ROW_TILE
    return pl.pallas_call(
        _ffn_prep_kernel,
        grid=(bsz, lt // tm),
        in_specs=[
            pl.BlockSpec((1, tm, d), lambda b, i: (b, i, 0)),
            pl.BlockSpec((1, d), lambda b, i: (0, 0)),
            pl.BlockSpec((1, 1, 1, d), _seg_map),
            pl.BlockSpec((1, 1, 1, d), _seg_map),
            pl.BlockSpec((d, ne), lambda b, i: (0, 0)),
            pl.BlockSpec((1, ne), lambda b, i: (0, 0)),
        ],
        out_specs=[
            pl.BlockSpec((1, tm, d), lambda b, i: (b, i, 0)),
            pl.BlockSpec((1, tm, ne), lambda b, i: (b, i, 0)),
        ],
        out_shape=[
            jax.ShapeDtypeStruct((bsz, lt, d), BF16),
            jax.ShapeDtypeStruct((bsz, lt, ne), F32),
        ],
        compiler_params=_cparams(("parallel", "parallel")),
        name="ffn_prep",
    )(h, g.reshape(1, d), shift, scale, router_w, router_b.reshape(1, ne))


GU_BLOCK = 256


def _moe_kernel(be_ref, nb_ref, first_ref, slot_ref, nxt_ref, x_ref, wgu_hbm, wd_hbm, bg_ref, bu_ref, bd_ref,
                o_ref, wgu_buf, wd_buf, wgu_s, wd_s, sem, *, layer):
    i = pl.program_id(0)
    active = i < nb_ref[0]
    half = GU_BLOCK // 2
    nblk = wgu_s.shape[1] // GU_BLOCK

    def weight_copies(e, slot):
        return (pltpu.make_async_copy(wgu_hbm.at[layer, e], wgu_buf.at[slot], sem.at[0, slot]),
                pltpu.make_async_copy(wd_hbm.at[layer, e], wd_buf.at[slot], sem.at[1, slot]))

    @pl.when(active & (i == 0))
    def _():
        for cp in weight_copies(be_ref[0], 0):
            cp.start()

    @pl.when(active & (first_ref[i] == 1))
    def _():
        slot = slot_ref[i]
        for cp in weight_copies(be_ref[i], slot):
            cp.wait()

        @pl.when(nxt_ref[i] >= 0)
        def _():
            for cp in weight_copies(nxt_ref[i], 1 - slot):
                cp.start()

        r = lax.broadcasted_iota(jnp.int32, (GU_BLOCK, GU_BLOCK), 0)
        c = lax.broadcasted_iota(jnp.int32, (GU_BLOCK, GU_BLOCK), 1)
        perm = (r == jnp.where(c < half, 2 * c, 2 * (c - half) + 1)).astype(BF16)
        for k in range(nblk):
            cs = slice(k * GU_BLOCK, (k + 1) * GU_BLOCK)
            wgu_s[:, cs] = jnp.dot(wgu_buf[slot, :, cs].astype(BF16), perm,
                                   preferred_element_type=F32).astype(BF16)
        wd_s[...] = wd_buf[slot].astype(BF16)

    @pl.when(active)
    def _():
        gu = jnp.dot(x_ref[...], wgu_s[...], preferred_element_type=F32)
        hdn = []
        for k in range(nblk):
            hs = slice(k * half, (k + 1) * half)
            g = gu[:, k * GU_BLOCK:k * GU_BLOCK + half] + bg_ref[0, :, hs]
            u = gu[:, k * GU_BLOCK + half:(k + 1) * GU_BLOCK] + bu_ref[0, :, hs]
            g = jnp.minimum(g, SWIGLU_LIMIT)
            u = jnp.clip(u, -SWIGLU_LIMIT, SWIGLU_LIMIT)
            hdn.append(((u + 1.0) * (g * jax.nn.sigmoid(SWIGLU_ALPHA * g))).astype(BF16))
        hdn = jnp.concatenate(hdn, axis=-1)
        o_ref[...] = (jnp.dot(hdn, wd_s[...], preferred_element_type=F32) + bd_ref[0]).astype(o_ref.dtype)

    @pl.when(jnp.logical_not(active))
    def _():
        o_ref[...] = jnp.zeros_like(o_ref)


def _moe_experts(x_sorted, block_expert, n_used, layer, w_gu, w_down, bg, bu, bd):
    n_rows, d = x_sorted.shape
    _, ne, _, f2 = w_gu.shape
    f = f2 // 2
    tm = MOE_TILE
    n_blocks = n_rows // tm
    assert f2 % GU_BLOCK == 0
    blk = jnp.arange(n_blocks, dtype=jnp.int32)
    prev = jnp.concatenate([block_expert[:1], block_expert[:-1]])
    first = (blk < n_used[0]) & ((blk == 0) | (block_expert != prev))
    slot = (jnp.cumsum(first.astype(jnp.int32)) - 1) & 1
    first_idx = jnp.where(first, blk, n_blocks)
    next_first = lax.cummin(first_idx, axis=0, reverse=True)
    next_first = jnp.concatenate([next_first[1:], jnp.full((1,), n_blocks, jnp.int32)])
    nxt = jnp.where(next_first < n_blocks, block_expert[jnp.minimum(next_first, n_blocks - 1)], -1)
    bmap = lambda i, be, nb, fi, sl, nx: (be[i], 0, 0)
    rmap = lambda i, be, nb, fi, sl, nx: (i, 0)
    grid_spec = pltpu.PrefetchScalarGridSpec(
        num_scalar_prefetch=5,
        grid=(n_blocks,),
        in_specs=[
            pl.BlockSpec((tm, d), rmap),
            pl.BlockSpec(memory_space=pl.ANY),
            pl.BlockSpec(memory_space=pl.ANY),
            pl.BlockSpec((1, 1, f), bmap),
            pl.BlockSpec((1, 1, f), bmap),
            pl.BlockSpec((1, 1, d), bmap),
        ],
        out_specs=pl.BlockSpec((tm, d), rmap),
        scratch_shapes=[pltpu.VMEM((2, d, f2), F32), pltpu.VMEM((2, f, d), F32),
                        pltpu.VMEM((d, f2), BF16), pltpu.VMEM((f, d), BF16),
                        pltpu.SemaphoreType.DMA((2, 2))],
    )
    return pl.pallas_call(
        functools.partial(_moe_kernel, layer=layer),
        grid_spec=grid_spec,
        out_shape=jax.ShapeDtypeStruct((n_rows, d), BF16),
        compiler_params=_cparams(("arbitrary",)),
        name="moe_experts",
    )(block_expert, n_used, first.astype(jnp.int32), slot.astype(jnp.int32), nxt.astype(jnp.int32),
      x_sorted, w_gu, w_down, bg.reshape(ne, 1, f), bu.reshape(ne, 1, f), bd.reshape(ne, 1, d))


def _combine_kernel(y_ref, gt_ref, h_ref, g2_ref, o_ref):
    gt = gt_ref[0]
    acc = y_ref[0, 0].astype(F32) * gt[:, 0:1]
    for k in range(1, TOP_K):
        acc = acc + y_ref[k, 0].astype(F32) * gt[:, k:k + 1]
    o_ref[0] = h_ref[0] + g2_ref[0, 0] * acc


def _moe_combine(yg, gate, h, g2, seg_map):
    k, bsz, lt, d = yg.shape
    tm = ROW_TILE
    return pl.pallas_call(
        _combine_kernel,
        grid=(bsz, lt // tm),
        in_specs=[
            pl.BlockSpec((k, 1, tm, d), lambda b, i: (0, b, i, 0)),
            pl.BlockSpec((1, tm, k), lambda b, i: (b, i, 0)),
            pl.BlockSpec((1, tm, d), lambda b, i: (b, i, 0)),
            pl.BlockSpec((1, 1, 1, d), seg_map),
        ],
        out_specs=pl.BlockSpec((1, tm, d), lambda b, i: (b, i, 0)),
        out_shape=jax.ShapeDtypeStruct((bsz, lt, d), F32),
        compiler_params=_cparams(("parallel", "parallel")),
        name="moe_combine",
    )(yg, gate, h, g2)


def _lat_seg_map(b, i):
    return (b, 1, 0, 0)


def _moe_layer(h, f, logits, g2, weights, lat_only, n_ctx):
    layer, w_gu, w_down, bg, bu, bd = weights
    bsz, lt, d = h.shape
    if lat_only:
        h_in, f, logits = h[:, n_ctx:], f[:, n_ctx:], logits[:, n_ctx:]
    else:
        h_in = h
    ltok = h_in.shape[1]
    n = bsz * ltok
    top_logit, top_e = lax.top_k(logits.reshape(n, -1), TOP_K)
    gate = jax.nn.softmax(top_logit, axis=-1)
    n_assign = n * TOP_K
    tm = MOE_TILE
    flat_e = top_e.reshape(-1).astype(jnp.int32)
    iota = jnp.arange(n_assign, dtype=jnp.int32)
    e_sorted, order = lax.sort_key_val(flat_e, iota, is_stable=True)
    experts = jnp.arange(N_EXPERTS, dtype=jnp.int32)
    start = jnp.searchsorted(e_sorted, experts, side='left', method='compare_all').astype(jnp.int32)
    counts = jnp.searchsorted(e_sorted, experts, side='right', method='compare_all').astype(jnp.int32) - start
    padded = (counts + tm - 1) // tm * tm
    pad_end = jnp.cumsum(padded)
    pad_start = pad_end - padded
    dest = pad_start[e_sorted] + iota - start[e_sorted]
    n_blocks = -(-(n_assign + N_EXPERTS * (tm - 1)) // tm)
    n_rows = n_blocks * tm
    block_expert = jnp.minimum(
        jnp.searchsorted(pad_end, jnp.arange(n_blocks, dtype=jnp.int32) * tm, side='right', method='compare_all'),
        N_EXPERTS - 1).astype(jnp.int32)
    n_used = (pad_end[-1] // tm).astype(jnp.int32).reshape(1)
    row = jnp.arange(n_rows, dtype=jnp.int32).reshape(n_blocks, tm)
    blk_shift = (start - pad_start)[block_expert][:, None]
    blk_end = (pad_start + counts)[block_expert][:, None]
    slot = jnp.clip(row + blk_shift, 0, n_assign - 1).reshape(-1)
    row_token = jnp.where((row < blk_end).reshape(-1),
                          order.at[slot].get(mode='promise_in_bounds') // TOP_K, 0)
    _, pos = lax.sort_key_val(order, dest, is_stable=True)
    x_sorted = f.reshape(n, d).at[row_token].get(mode='promise_in_bounds')
    y = _moe_experts(x_sorted, block_expert, n_used, layer, w_gu, w_down, bg, bu, bd)
    pos_k = pos.reshape(n, TOP_K).T
    yg = y.at[pos_k.reshape(-1)].get(mode='promise_in_bounds').reshape(TOP_K, bsz, ltok, d)
    out = _moe_combine(yg, gate.reshape(bsz, ltok, TOP_K), h_in, g2,
                       _lat_seg_map if lat_only else _seg_map)
    return out


def _rmsnorm_kernel(x_ref, g_ref, o_ref):
    x = x_ref[0]
    ms = jnp.mean(x * x, axis=-1, keepdims=True)
    o_ref[0] = x * lax.rsqrt(ms + EPS) * g_ref[...]


def _final_norm(h, g):
    bsz, lt, d = h.shape
    tm = ROW_TILE
    return pl.pallas_call(
        _rmsnorm_kernel,
        grid=(bsz, lt // tm),
        in_specs=[pl.BlockSpec((1, tm, d), lambda b, i: (b, i, 0)),
                  pl.BlockSpec((1, d), lambda b, i: (0, 0))],
        out_specs=pl.BlockSpec((1, tm, d), lambda b, i: (b, i, 0)),
        out_shape=jax.ShapeDtypeStruct((bsz, lt, d), F32),
        compiler_params=_cparams(("parallel", "parallel")),
        name="final_norm",
    )(h, g.reshape(1, d))


def _grid_reorder_kernel(x_ref, o_ref, *, n_ctx, rows, to_cols):
    o_ref[0, :n_ctx, :] = x_ref[0, :n_ctx, :]
    for c in range(GRID_W):
        raster = pl.ds(n_ctx + c, rows, stride=GRID_W)
        dense = pl.ds(n_ctx + c * rows, rows)
        if to_cols:
            o_ref.at[0][dense, :] = x_ref.at[0][raster, :]
        else:
            o_ref.at[0][raster, :] = x_ref.at[0][dense, :]


def _grid_reorder(h, n_ctx, to_cols):
    bsz, lt, d = h.shape
    lanes = 128
    spec = pl.BlockSpec((1, lt, lanes), lambda b, j: (b, 0, j))
    return pl.pallas_call(
        functools.partial(_grid_reorder_kernel, n_ctx=n_ctx, rows=(lt - n_ctx) // GRID_W, to_cols=to_cols),
        grid=(bsz, d // lanes),
        in_specs=[spec],
        out_specs=spec,
        out_shape=jax.ShapeDtypeStruct(h.shape, h.dtype),
        compiler_params=_cparams(("parallel", "parallel")),
        name="grid_reorder",
    )(h)


def _chunk_order(d, c, n_ctx_chunks, n_chunks):
    bwd = jnp.where(c < n_ctx_chunks, n_ctx_chunks - 1 - c, n_chunks + n_ctx_chunks - 1 - c)
    return jnp.where(d == 0, c, bwd)


def _dir_tri(d, t):
    row = lax.broadcasted_iota(jnp.int32, (t, t), 0)
    col = lax.broadcasted_iota(jnp.int32, (t, t), 1)
    return jnp.where(d == 0, col - row, row - col) <= 0


def _gla_kernel(q_ref, k_ref, v_ref, r_ref, w2_ref, b2_ref, o_ref, st_ref, *, t, heads, scale):
    d = pl.program_id(0)
    c = pl.program_id(2)

    @pl.when(c == 0)
    def _():
        st_ref[...] = jnp.zeros_like(st_ref)

    dk = q_ref.shape[-1] // heads
    dv = v_ref.shape[-1] // heads
    nb = q_ref.shape[0]
    mask = _dir_tri(d, t)
    tri = mask.astype(F32)
    mid = t // 2
    items = [(bb, h) for bb in range(nb) for h in range(heads)]
    xs = [jnp.dot(r_ref[bb], w2_ref[0], preferred_element_type=F32, precision=HI) + b2_ref[0] for bb in range(nb)]
    las = [jax.nn.log_sigmoid(x) * (1.0 / GLA_TAU) for x in xs]
    bs = [jnp.dot(tri, la, preferred_element_type=F32, precision=HI) for la in las]
    qt, kt, qe, kh_end, e_end = [], [], [], [], []
    for bb in range(nb):
        b = bs[bb]
        b_m = b[mid:mid + 1, :]
        b_end = jnp.where(d == 0, b[t - 1:t, :], b[0:1, :])
        q_s = q_ref[bb] * (jnp.exp(b - b_m) * scale)
        k_s = k_ref[bb] * jnp.exp(b_m - b)
        qe.append((q_s * jnp.exp(b_m)).astype(BF16))
        kh_end.append((k_s * jnp.exp(b_end - b_m)).astype(BF16))
        e_end.append(jnp.exp(b_end))
        qt.append(q_s.astype(BF16))
        kt.append(k_s.astype(BF16))
    att, q_st, vs = {}, {}, {}
    for bb, h in items:
        ks = slice(h * dk, (h + 1) * dk)
        vs[bb, h] = v_ref[bb, :, h * dv:(h + 1) * dv].astype(BF16)
        att[bb, h] = lax.dot_general(qt[bb][:, ks], kt[bb][:, ks], NT, preferred_element_type=F32)
        q_st[bb, h] = lax.dot_general(qe[bb][:, ks], st_ref[bb * heads + h].astype(BF16), NT,
                                      preferred_element_type=F32)
    for bb, h in items:
        a = jnp.where(mask, att[bb, h], 0.0).astype(BF16)
        o_ref[0, bb, :, h * dv:(h + 1) * dv] = jnp.dot(a, vs[bb, h], preferred_element_type=F32) + q_st[bb, h]
    for bb, h in items:
        ks = slice(h * dk, (h + 1) * dk)
        upd = lax.dot_general(vs[bb, h], kh_end[bb][:, ks], TN, preferred_element_type=F32)
        st_ref[bb * heads + h] = st_ref[bb * heads + h] * e_end[bb][:, ks] + upd


def _gla_mixer(q, k, v, r, w2, b2, n_ctx):
    bsz, lt, dkt = q.shape
    dvt = v.shape[-1]
    nr = r.shape[-1]
    t, heads = MIX_CHUNK, GLA_HEADS
    bt = MIX_BATCH
    nch = lt // t
    ncc = n_ctx // t
    dk = dkt // heads
    dv = dvt // heads
    imap = lambda d, b, c: (b, _chunk_order(d, c, ncc, nch), 0)
    return pl.pallas_call(
        functools.partial(_gla_kernel, t=t, heads=heads, scale=dk ** -0.5),
        grid=(2, bsz // bt, nch),
        in_specs=[
            pl.BlockSpec((bt, t, dkt), imap),
            pl.BlockSpec((bt, t, dkt), imap),
            pl.BlockSpec((bt, t, dvt), imap),
            pl.BlockSpec((bt, t, nr), imap),
            pl.BlockSpec((1, nr, dkt), lambda d, b, c: (d, 0, 0)),
            pl.BlockSpec((1, 1, dkt), lambda d, b, c: (d, 0, 0)),
        ],
        out_specs=pl.BlockSpec((1, bt, t, dvt), lambda d, b, c: (d, b, _chunk_order(d, c, ncc, nch), 0)),
        out_shape=jax.ShapeDtypeStruct((2, bsz, lt, dvt), F32),
        scratch_shapes=[pltpu.VMEM((bt * heads, dv, dk), F32)],
        compiler_params=_cparams(("parallel", "parallel", "arbitrary")),
        name="gla_mixer",
    )(q, k, v, r, w2, b2)


def _mlstm_kernel(q_ref, k_ref, v_ref, gc_ref, gr_ref, o_ref, c_ref, n_ref, m_ref, *, t, heads):
    d = pl.program_id(0)
    c = pl.program_id(2)

    @pl.when(c == 0)
    def _():
        c_ref[...] = jnp.zeros_like(c_ref)
        n_ref[...] = jnp.zeros_like(n_ref)
        m_ref[...] = jnp.zeros_like(m_ref)

    dh = q_ref.shape[-1] // heads
    nb = q_ref.shape[0]
    mask = _dir_tri(d, t)
    tri = mask.astype(F32)
    items = [(bb, h) for bb in range(nb) for h in range(heads)]
    gate = []
    for bb in range(nb):
        gc = gc_ref[0, bb]
        gr = gr_ref[0, bb, 0]
        fc = jax.nn.log_sigmoid(gc[:, heads:])
        fr = jax.nn.log_sigmoid(gr[heads:, :])
        b_col = jnp.dot(tri, fc, preferred_element_type=F32, precision=HI)
        b_row = lax.dot_general(fr, tri, NT, preferred_element_type=F32, precision=HI)
        b_last = jnp.where(d == 0, b_col[t - 1:t, :], b_col[0:1, :])
        gate.append((gc[:, :heads], gr[:heads, :], b_col, b_row, b_last))
    qs, ks, vs, s_raw, q_c = {}, {}, {}, {}, {}
    for bb, h in items:
        hs = slice(h * dh, (h + 1) * dh)
        qs[bb, h] = q_ref[bb, :, hs]
        ks[bb, h] = k_ref[bb, :, hs]
        vs[bb, h] = v_ref[bb, :, hs].astype(BF16)
        s_raw[bb, h] = lax.dot_general(qs[bb, h], ks[bb, h], NT, preferred_element_type=F32)
        q_c[bb, h] = jnp.dot(qs[bb, h], c_ref[bb * heads + h].astype(BF16), preferred_element_type=F32)
    logw, log_inter, m_t, w_inter, scores, den, qn = {}, {}, {}, {}, {}, {}, {}
    for bb, h in items:
        _, ir, b_col, b_row, _ = gate[bb]
        bc = b_col[:, h:h + 1]
        logw[bb, h] = jnp.where(mask, bc - b_row[h:h + 1, :] + ir[h:h + 1, :], -jnp.inf)
        log_inter[bb, h] = bc + m_ref[bb * heads + h]
    for bb, h in items:
        m_t[bb, h] = jnp.maximum(log_inter[bb, h], jnp.max(logw[bb, h], axis=-1, keepdims=True))
        qn[bb, h] = jnp.sum(qs[bb, h].astype(F32) * n_ref[bb * heads + h], axis=-1, keepdims=True)
    for bb, h in items:
        w_inter[bb, h] = jnp.exp(log_inter[bb, h] - m_t[bb, h])
        scores[bb, h] = s_raw[bb, h] * jnp.exp(logw[bb, h] - m_t[bb, h])
    for bb, h in items:
        den[bb, h] = jnp.sum(scores[bb, h], axis=-1, keepdims=True) + w_inter[bb, h] * qn[bb, h]
    num = {}
    for bb, h in items:
        num[bb, h] = (jnp.dot(scores[bb, h].astype(BF16), vs[bb, h], preferred_element_type=F32)
                      + w_inter[bb, h] * q_c[bb, h])
    for bb, h in items:
        hs = slice(h * dh, (h + 1) * dh)
        o_ref[0, bb, :, hs] = num[bb, h] / jnp.maximum(jnp.abs(den[bb, h]), jnp.exp(-m_t[bb, h]))
    log_g, m_new, kw, upd, ksum = {}, {}, {}, {}, {}
    for bb, h in items:
        ic, _, b_col, _, b_last = gate[bb]
        log_g[bb, h] = b_last[:, h:h + 1] - b_col[:, h:h + 1] + ic[:, h:h + 1]
    for bb, h in items:
        b_last = gate[bb][4]
        m_new[bb, h] = jnp.maximum(b_last[:, h:h + 1] + m_ref[bb * heads + h],
                                   jnp.max(log_g[bb, h], axis=0, keepdims=True))
    for bb, h in items:
        kw[bb, h] = ks[bb, h].astype(F32) * jnp.exp(log_g[bb, h] - m_new[bb, h])
    for bb, h in items:
        upd[bb, h] = lax.dot_general(kw[bb, h].astype(BF16), vs[bb, h], TN, preferred_element_type=F32)
        ksum[bb, h] = jnp.sum(kw[bb, h], axis=0, keepdims=True)
    for bb, h in items:
        si = bb * heads + h
        b_last = gate[bb][4]
        keep = jnp.exp(b_last[:, h:h + 1] + m_ref[si] - m_new[bb, h])
        c_ref[si] = keep * c_ref[si] + upd[bb, h]
        n_ref[si] = keep * n_ref[si] + ksum[bb, h]
        m_ref[si] = m_new[bb, h]


def _mlstm_mixer(qk, v, gates, n_ctx):
    bsz, lt, w2 = qk.shape
    w = w2 // 2
    t, heads = MIX_CHUNK, MLSTM_HEADS
    dh = w // heads
    nch = lt // t
    ncc = n_ctx // t
    gc = gates.reshape(bsz, lt, 2, 2 * heads).transpose(2, 0, 1, 3)
    gr = gc.reshape(2, bsz, nch, t, 2 * heads).transpose(0, 1, 2, 4, 3)
    cmap = lambda d, b, c: _chunk_order(d, c, ncc, nch)
    bt = MIX_BATCH
    return pl.pallas_call(
        functools.partial(_mlstm_kernel, t=t, heads=heads),
        grid=(2, bsz // bt, nch),
        in_specs=[
            pl.BlockSpec((bt, t, w), lambda d, b, c: (b, cmap(d, b, c), 0)),
            pl.BlockSpec((bt, t, w), lambda d, b, c: (b, cmap(d, b, c), 1)),
            pl.BlockSpec((bt, t, w), lambda d, b, c: (b, cmap(d, b, c), 0)),
            pl.BlockSpec((1, bt, t, 2 * heads), lambda d, b, c: (d, b, cmap(d, b, c), 0)),
            pl.BlockSpec((1, bt, 1, 2 * heads, t), lambda d, b, c: (d, b, cmap(d, b, c), 0, 0)),
        ],
        out_specs=pl.BlockSpec((1, bt, t, w), lambda d, b, c: (d, b, cmap(d, b, c), 0)),
        out_shape=jax.ShapeDtypeStruct((2, bsz, lt, w), F32),
        scratch_shapes=[pltpu.VMEM((bt * heads, dh, dh), F32), pltpu.VMEM((bt * heads, 1, dh), F32),
                        pltpu.VMEM((bt * heads, 1, 1), F32)],
        compiler_params=_cparams(("parallel", "parallel", "arbitrary")),
        name="mlstm_mixer",
    )(qk, qk, v, gc, gr)


def _conv_kernel(x_ref, w_ref, b_ref, s_ref, o_ref, *, n_ctx):
    x = x_ref[0]
    lt = x.shape[0]
    row = lax.broadcasted_iota(jnp.int32, x.shape, 0)
    prev = jnp.where((row == 0) | (row == n_ctx), 0.0, pltpu.roll(x, 1, 0))
    nxt = jnp.where((row == n_ctx - 1) | (row == lt - 1), 0.0, pltpu.roll(x, lt - 1, 0))
    y = b_ref[...] + w_ref[0:1, :] * prev + w_ref[1:2, :] * x + w_ref[2:3, :] * nxt
    o_ref[0] = (y * jax.nn.sigmoid(y) * s_ref[...]).astype(o_ref.dtype)


def _conv_silu(x, w, b, colscale, n_ctx):
    bsz, lt, ch = x.shape
    tc = 256
    return pl.pallas_call(
        functools.partial(_conv_kernel, n_ctx=n_ctx),
        grid=(bsz, ch // tc),
        in_specs=[
            pl.BlockSpec((1, lt, tc), lambda b, j: (b, 0, j)),
            pl.BlockSpec((3, tc), lambda b, j: (0, j)),
            pl.BlockSpec((1, tc), lambda b, j: (0, j)),
            pl.BlockSpec((1, tc), lambda b, j: (0, j)),
        ],
        out_specs=pl.BlockSpec((1, lt, tc), lambda b, j: (b, 0, j)),
        out_shape=jax.ShapeDtypeStruct((bsz, lt, ch), BF16),
        compiler_params=_cparams(("parallel", "parallel")),
        name="conv_silu",
    )(x, w, b.reshape(1, ch), colscale.reshape(1, ch))


def _s5_matrices(a_re, a_im, log_dt, b_re, b_im, c_re, c_im, backward, lane_groups=8):
    g, p = a_re.shape
    cg = b_re.shape[-1]
    j = S5_J
    lg = lane_groups
    nq = g // lg
    dt = jnp.exp(log_dt)[:, None]
    lam_re = jnp.minimum(a_re, -1e-4)
    lam_im = a_im
    decay = jnp.exp(lam_re * dt)
    ab_re = decay * jnp.cos(lam_im * dt)
    ab_im = decay * jnp.sin(lam_im * dt)
    den = lam_re * lam_re + lam_im * lam_im
    zr = ((ab_re - 1) * lam_re + ab_im * lam_im) / den
    zi = (ab_im * lam_re - (ab_re - 1) * lam_im) / den
    bb_re = zr[..., None] * b_re - zi[..., None] * b_im
    bb_im = zr[..., None] * b_im + zi[..., None] * b_re
    pw_re, pw_im = [jnp.ones_like(ab_re)], [jnp.zeros_like(ab_im)]
    for _ in range(j):
        r0, i0 = pw_re[-1], pw_im[-1]
        pw_re.append(ab_re * r0 - ab_im * i0)
        pw_im.append(ab_re * i0 + ab_im * r0)
    pw_re, pw_im = jnp.stack(pw_re), jnp.stack(pw_im)
    ca_re = c_re[None] * pw_re[:, :, None, :] - c_im[None] * pw_im[:, :, None, :]
    ca_im = c_re[None] * pw_im[:, :, None, :] + c_im[None] * pw_re[:, :, None, :]
    kk = (jnp.einsum('tgcp,gpd->tgcd', ca_re[:j], bb_re, precision=HI)
          - jnp.einsum('tgcp,gpd->tgcd', ca_im[:j], bb_im, precision=HI))
    ab_pw_re = pw_re[:j, :, :, None] * bb_re[None] - pw_im[:j, :, :, None] * bb_im[None]
    ab_pw_im = pw_re[:j, :, :, None] * bb_im[None] + pw_im[:j, :, :, None] * bb_re[None]
    eye = jnp.eye(lg, dtype=BF16)
    lb = lg * cg
    sw = lg * p
    bd_k = jnp.einsum('tqgcd,gh->tqgdhc', kk.astype(BF16).reshape(j, nq, lg, cg, cg), eye
                      ).reshape(j, nq, lb, lb)
    bd_in = [jnp.einsum('tqgpc,gh->tqgchp', a.astype(BF16).reshape(j, nq, lg, p, cg), eye
                        ).reshape(j, nq, lb, sw) for a in (ab_pw_re, ab_pw_im)]
    bd_out = [jnp.einsum('tqgcp,gh->tqgphc', a.astype(BF16).reshape(j + 1, nq, lg, cg, p), eye
                         ).reshape(j + 1, nq, sw, lb) for a in (ca_re, -ca_im)]
    jj = jnp.arange(j)
    lag = (jj[:, None] - jj[None, :]) if backward else (jj[None, :] - jj[:, None])
    kt = jnp.where((lag >= 0)[:, :, None, None, None], bd_k[jnp.clip(lag, 0, j - 1)], 0)
    ktoep = kt.transpose(2, 0, 3, 1, 4).reshape(nq, j * lb, j * lb)
    tau_in = jj if backward else (j - 1 - jj)
    win_re, win_im = (a[tau_in].transpose(1, 0, 2, 3).reshape(nq, j * lb, sw) for a in bd_in)
    tau_out = (j - jj) if backward else (jj + 1)
    wout_re, wout_im = (a[tau_out].transpose(1, 2, 0, 3).reshape(nq, sw, j * lb) for a in bd_out)
    dec_re = pw_re[j].reshape(nq, 1, sw)
    dec_im = pw_im[j].reshape(nq, 1, sw)
    return ktoep, win_re, win_im, wout_re, wout_im, dec_re, dec_im


def _s5_kernel(u_ref, kt_ref, wir_ref, wii_ref, wor_ref, woi_ref, dr_ref, di_ref, y_ref,
               xf_ref, yf_ref, sre_ref, sim_ref, *, bt, nk, nk_ctx, rs):
    d = pl.program_id(0)
    j = S5_J
    lanes = u_ref.shape[-1]
    for b in range(bt):
        for jj in range(j):
            xf_ref[b * nk:(b + 1) * nk, jj * lanes:(jj + 1) * lanes] = (
                u_ref.at[b][pl.ds(jj, nk, stride=j), :].astype(BF16))
    xf = xf_ref[...]
    yf_ref[...] = jnp.dot(xf, kt_ref[0, 0], preferred_element_type=F32)
    inc_re = jnp.dot(xf, wir_ref[0, 0], preferred_element_type=F32)
    inc_im = jnp.dot(xf, wii_ref[0, 0], preferred_element_type=F32)
    nl = sre_ref.shape[0]
    for b in range(bt):
        for l in range(nl):
            sre_ref[l, b * rs:b * rs + nk, :] = inc_re[b * nk:(b + 1) * nk, l * lanes:(l + 1) * lanes]
            sim_ref[l, b * rs:b * rs + nk, :] = inc_im[b * nk:(b + 1) * nk, l * lanes:(l + 1) * lanes]
    a_re = [dr_ref[0, 0, :, l * lanes:(l + 1) * lanes] for l in range(nl)]
    a_im = [di_ref[0, 0, :, l * lanes:(l + 1) * lanes] for l in range(nl)]

    def step(kidx, carry):
        rows = pl.ds(kidx, bt, stride=rs)
        out = []
        for l in range(nl):
            s_re, s_im = carry[2 * l], carry[2 * l + 1]
            i_re = sre_ref.at[l][rows, :]
            i_im = sim_ref.at[l][rows, :]
            sre_ref.at[l][rows, :] = s_re
            sim_ref.at[l][rows, :] = s_im
            out.append(a_re[l] * s_re - a_im[l] * s_im + i_re)
            out.append(a_re[l] * s_im + a_im[l] * s_re + i_im)
        return tuple(out)

    zero = tuple(jnp.zeros((bt, lanes), F32) for _ in range(2 * nl))

    @pl.when(d == 0)
    def _():
        lax.fori_loop(0, nk, step, zero)

    @pl.when(d == 1)
    def _():
        carry = lax.fori_loop(0, nk_ctx, lambda i, cr: step(nk_ctx - 1 - i, cr), zero)
        lax.fori_loop(0, nk - nk_ctx, lambda i, cr: step(nk - 1 - i, cr), carry)

    for b in range(bt):
        sp_re = jnp.concatenate([sre_ref[l, b * rs:b * rs + nk, :] for l in range(nl)], axis=-1).astype(BF16)
        sp_im = jnp.concatenate([sim_ref[l, b * rs:b * rs + nk, :] for l in range(nl)], axis=-1).astype(BF16)
        yb = (yf_ref[b * nk:(b + 1) * nk, :]
              + jnp.dot(sp_re, wor_ref[0, 0], preferred_element_type=F32)
              + jnp.dot(sp_im, woi_ref[0, 0], preferred_element_type=F32))
        for jj in range(j):
            y_ref.at[0, b][pl.ds(jj, nk, stride=j), :] = yb[:, jj * lanes:(jj + 1) * lanes]


def _s5_mixer(u, mats, n_ctx):
    bsz, lt, w = u.shape
    ktoep, win_re, win_im, wout_re, wout_im, dec_re, dec_im = mats
    lanes = 128
    bt = 4 if bsz % 4 == 0 else 2
    nq = w // lanes
    j = S5_J
    nk = lt // j
    nk_ctx = n_ctx // j
    rs = nk + 8
    fl = j * lanes
    sw = win_re.shape[-1]
    wmap = lambda d, q, b: (d, q, 0, 0)
    return pl.pallas_call(
        functools.partial(_s5_kernel, bt=bt, nk=nk, nk_ctx=nk_ctx, rs=rs),
        grid=(2, nq, bsz // bt),
        in_specs=[
            pl.BlockSpec((bt, lt, lanes), lambda d, q, b: (b, 0, q)),
            pl.BlockSpec((1, 1, fl, fl), wmap),
            pl.BlockSpec((1, 1, fl, sw), wmap),
            pl.BlockSpec((1, 1, fl, sw), wmap),
            pl.BlockSpec((1, 1, sw, fl), wmap),
            pl.BlockSpec((1, 1, sw, fl), wmap),
            pl.BlockSpec((1, 1, 1, sw), wmap),
            pl.BlockSpec((1, 1, 1, sw), wmap),
        ],
        out_specs=pl.BlockSpec((1, bt, lt, lanes), lambda d, q, b: (d, b, 0, q)),
        out_shape=jax.ShapeDtypeStruct((2, bsz, lt, w), F32),
        scratch_shapes=[pltpu.VMEM((bt * nk, fl), BF16), pltpu.VMEM((bt * nk, fl), F32),
                        pltpu.VMEM((sw // lanes, bt * rs, lanes), F32),
                        pltpu.VMEM((sw // lanes, bt * rs, lanes), F32)],
        compiler_params=_cparams(("parallel", "parallel", "arbitrary")),
        name="s5_mixer",
    )(u, ktoep, win_re, win_im, wout_re, wout_im, dec_re, dec_im)


def _head_norm(x, heads):
    dh = x.shape[-1] // heads
    outs = []
    for h in range(heads):
        xh = x[:, h * dh:(h + 1) * dh]
        outs.append(xh * lax.rsqrt(jnp.mean(xh * xh, axis=-1, keepdims=True) + EPS))
    return jnp.concatenate(outs, axis=-1)


def _even_post_kernel(m_ref, o_ref, s_ref, u_ref, mg_ref, dsk_ref, gw_ref, gb_ref, w_ref, h_ref, gate_ref,
                      out_ref, *, heads):
    m = m_ref[0, 0] + m_ref[1, 0]
    m_out = _head_norm(m, heads) * mg_ref[...] * jax.nn.sigmoid(o_ref[0])
    y = jax.nn.gelu(s_ref[0, 0] + s_ref[1, 0] + dsk_ref[...] * u_ref[0])
    glu = jnp.dot(y.astype(BF16), gw_ref[...], preferred_element_type=F32) + gb_ref[...]
    s_out = y * jax.nn.sigmoid(glu)
    cat = jnp.concatenate([m_out, s_out], axis=-1).astype(BF16)
    z = jnp.dot(cat, w_ref[...], preferred_element_type=F32)
    out_ref[0] = h_ref[0] + gate_ref[0, 0] * z


def _even_post(m2, o, s2, u, mnorm_g, d_skip, glu_w, glu_b, w_out, h, gate):
    bsz, lt, d = h.shape
    mw = o.shape[-1]
    sw = u.shape[-1]
    tm = ROW_TILE
    row = lambda b, i: (b, i, 0)
    row2 = lambda b, i: (0, b, i, 0)
    const = lambda b, i: (0, 0)
    return pl.pallas_call(
        functools.partial(_even_post_kernel, heads=MLSTM_HEADS),
        grid=(bsz, lt // tm),
        in_specs=[
            pl.BlockSpec((2, 1, tm, mw), row2),
            pl.BlockSpec((1, tm, mw), row),
            pl.BlockSpec((2, 1, tm, sw), row2),
            pl.BlockSpec((1, tm, sw), row),
            pl.BlockSpec((1, mw), const),
            pl.BlockSpec((1, sw), const),
            pl.BlockSpec((sw, sw), const),
            pl.BlockSpec((1, sw), const),
            pl.BlockSpec((mw + sw, d), const),
            pl.BlockSpec((1, tm, d), row),
            pl.BlockSpec((1, 1, 1, d), _seg_map),
        ],
        out_specs=pl.BlockSpec((1, tm, d), row),
        out_shape=jax.ShapeDtypeStruct((bsz, lt, d), F32),
        compiler_params=_cparams(("parallel", "parallel")),
        name="even_post",
    )(m2, o, s2, u, mnorm_g.reshape(1, mw), d_skip.reshape(1, sw), glu_w.astype(BF16), glu_b.reshape(1, sw),
      w_out.astype(BF16), h, gate)


def _odd_post_kernel(o_ref, g_ref, ng_ref, w_ref, h_ref, gate_ref, out_ref, *, heads):
    g = g_ref[0]
    y = _head_norm(o_ref[0, 0] + o_ref[1, 0], heads) * ng_ref[...] * (g * jax.nn.sigmoid(g))
    z = jnp.dot(y.astype(BF16), w_ref[...], preferred_element_type=F32)
    out_ref[0] = h_ref[0] + gate_ref[0, 0] * z


def _odd_post(o, g, norm_g, w_out, h, gate):
    bsz, lt, d = h.shape
    dv = o.shape[-1]
    tm = ROW_TILE
    row = lambda b, i: (b, i, 0)
    const = lambda b, i: (0, 0)
    return pl.pallas_call(
        functools.partial(_odd_post_kernel, heads=GLA_HEADS),
        grid=(bsz, lt // tm),
        in_specs=[
            pl.BlockSpec((2, 1, tm, dv), lambda b, i: (0, b, i, 0)),
            pl.BlockSpec((1, tm, dv), row),
            pl.BlockSpec((1, dv), const),
            pl.BlockSpec((dv, d), const),
            pl.BlockSpec((1, tm, d), row),
            pl.BlockSpec((1, 1, 1, d), _seg_map),
        ],
        out_specs=pl.BlockSpec((1, tm, d), row),
        out_shape=jax.ShapeDtypeStruct((bsz, lt, d), F32),
        compiler_params=_cparams(("parallel", "parallel")),
        name="odd_post",
    )(o, g, norm_g.reshape(1, dv), w_out.astype(BF16), h, gate)


def kernel(x, c, ctx, c_ctx, mod_w, mod_b, norm_mix_g, norm_ffn_g, ev_w_in, ev_b_in, ev_conv_w, ev_conv_b, ev_mlstm_norm_g, ev_s5_a_re_f, ev_s5_a_im_f, ev_s5_log_dt_f, ev_s5_a_re_b, ev_s5_a_im_b, ev_s5_log_dt_b, ev_s5_b_re, ev_s5_b_im, ev_s5_c_re, ev_s5_c_im, ev_s5_d, ev_s5_glu_w, ev_s5_glu_b, ev_w_out, od_w_in, od_gate_w2_f, od_gate_b2_f, od_gate_w2_b, od_gate_b2_b, od_norm_g, od_w_out, router_w, router_b, moe_w_gu, moe_b_gu, moe_w_down, moe_b_down, final_norm_g):
    bsz, seq, d = x.shape
    n_ctx = ctx.shape[1]
    depth = mod_w.shape[0]
    lt = n_ctx + seq
    assert n_ctx == ROW_TILE and seq % ROW_TILE == 0 and seq % GRID_W == 0

    h = jnp.concatenate([ctx, x], axis=1)
    c_all = jnp.concatenate([c, c_ctx[None, :]], axis=0)
    c_all = jnp.pad(c_all, ((0, (-c_all.shape[0]) % 8), (0, 0)))
    mods = _modulation(c_all, mod_w, mod_b)
    mod_lat = mods[:, :bsz]
    mod_ctx = jnp.broadcast_to(mods[:, bsz:bsz + 1], mod_lat.shape)
    mod6 = jnp.stack([mod_ctx, mod_lat], axis=2).reshape(depth, bsz, 2, 6, 1, d)

    bg_all = moe_b_gu[..., 0::2]
    bu_all = moe_b_gu[..., 1::2]

    mw = ev_conv_w.shape[-1] // 2
    n_gates = 4 * MLSTM_HEADS
    s5w = ev_s5_d.shape[-1]
    dk_t = od_gate_w2_f.shape[-1]
    dv_t = od_norm_g.shape[-1]
    for layer in range(depth):
        last = layer == depth - 1
        j = layer // 2
        m6 = mod6[layer]
        sh1, sc1, g1, sh2, sc2, g2 = (m6[:, :, i] for i in range(6))
        if layer % 2 == 0:
            w_in, b_in = ev_w_in[j], ev_b_in[j]
            cols = jnp.concatenate([jnp.arange(0, 4 * mw), jnp.arange(4 * mw + n_gates, 4 * mw + n_gates + s5w),
                                    jnp.arange(4 * mw, 4 * mw + n_gates)])
            qk_pre, v, o, u, gates = _nm_matmul(h, norm_mix_g[layer], sh1, sc1, w_in[:, cols], b_in[cols],
                                                (2 * mw, mw, mw, s5w, n_gates))
            dh = mw // MLSTM_HEADS
            colscale = jnp.concatenate([jnp.full((mw,), dh ** -0.5, F32), jnp.ones((mw,), F32)])
            qk = _conv_silu(qk_pre, ev_conv_w[j], ev_conv_b[j], colscale, n_ctx)
            m2 = _mlstm_mixer(qk, v, gates, n_ctx)
            shared = (ev_s5_b_re[j], ev_s5_b_im[j], ev_s5_c_re[j], ev_s5_c_im[j])
            mats_f = _s5_matrices(ev_s5_a_re_f[j], ev_s5_a_im_f[j], ev_s5_log_dt_f[j], *shared, backward=False)
            mats_b = _s5_matrices(ev_s5_a_re_b[j], ev_s5_a_im_b[j], ev_s5_log_dt_b[j], *shared, backward=True)
            s2 = _s5_mixer(u, tuple(jnp.stack([a, b]) for a, b in zip(mats_f, mats_b)), n_ctx)
            h = _even_post(m2, o, s2, u, ev_mlstm_norm_g[j], ev_s5_d[j], ev_s5_glu_w[j], ev_s5_glu_b[j],
                           ev_w_out[j], h, g1)
        else:
            hc = _grid_reorder(h, n_ctx, True)
            qq, kk, vv, gg, rr = _nm_matmul(hc, norm_mix_g[layer], sh1, sc1, od_w_in[j],
                                            jnp.zeros((od_w_in.shape[-1],), F32),
                                            (dk_t, dk_t, dv_t, dv_t, 2 * GLA_RANK))
            zero = jnp.zeros_like(od_gate_w2_f[j])
            w2 = jnp.stack([jnp.concatenate([od_gate_w2_f[j], zero], axis=0),
                            jnp.concatenate([zero, od_gate_w2_b[j]], axis=0)])
            b2 = jnp.stack([od_gate_b2_f[j], od_gate_b2_b[j]])[:, None, :]
            o2 = _gla_mixer(qq, kk, vv, rr, w2, b2, n_ctx)
            h = _grid_reorder(_odd_post(o2, gg, od_norm_g[j], od_w_out[j], hc, g1), n_ctx, False)
        f, logits = _ffn_prep(h, norm_ffn_g[layer], sh2, sc2, router_w[layer], router_b[layer])
        weights = (layer, moe_w_gu, moe_w_down, bg_all[layer], bu_all[layer], moe_b_down[layer])
        h = _moe_layer(h, f, logits, g2, weights, last, n_ctx)
    return _final_norm(h, final_norm_g)
```

```python
import functools

import jax
import jax.numpy as jnp
from jax import lax
from jax.experimental import pallas as pl
from jax.experimental.pallas import tpu as pltpu

F32 = jnp.float32
BF16 = jnp.bfloat16
HI = lax.Precision.HIGHEST

EPS = 1e-6
GRID_W = 64
MLSTM_HEADS = 4
S5_GROUP = 16
GLA_HEADS = 4
GLA_RANK = 16
GLA_TAU = 16.0
N_EXPERTS = 32
TOP_K = 4
SWIGLU_LIMIT = 7.0
SWIGLU_ALPHA = 1.702

ROW_TILE = 256
MOE_TILE = 512
MIX_CHUNK = 64
MIX_BATCH = 4
S5_J = 8
VMEM_LIMIT = 56 * 1024 * 1024

NT = (((1,), (1,)), ((), ()))
TN = (((0,), (0,)), ((), ()))


def _cparams(sem):
    return pltpu.CompilerParams(dimension_semantics=sem, vmem_limit_bytes=VMEM_LIMIT)


def _mod_kernel(c_ref, w_ref, b_ref, o_ref):
    c = c_ref[...]
    a = c * jax.nn.sigmoid(c)
    o_ref[0] = jnp.dot(a.astype(BF16), w_ref[0].astype(BF16), preferred_element_type=F32) + b_ref[0]


def _modulation(c_all, mod_w, mod_b):
    depth, d, n6 = mod_w.shape
    rows = c_all.shape[0]
    tn = d
    return pl.pallas_call(
        _mod_kernel,
        grid=(depth, n6 // tn),
        in_specs=[
            pl.BlockSpec((rows, d), lambda l, j: (0, 0)),
            pl.BlockSpec((1, d, tn), lambda l, j: (l, 0, j)),
            pl.BlockSpec((1, 1, tn), lambda l, j: (l, 0, j)),
        ],
        out_specs=pl.BlockSpec((1, rows, tn), lambda l, j: (l, 0, j)),
        out_shape=jax.ShapeDtypeStruct((depth, rows, n6), F32),
        compiler_params=_cparams(("arbitrary", "arbitrary")),
        name="modulation",
    )(c_all, mod_w, mod_b.reshape(depth, 1, n6))


def _norm_mod(x, g, sh, sc):
    ms = jnp.mean(x * x, axis=-1, keepdims=True)
    return (x * lax.rsqrt(ms + EPS) * g) * (1.0 + sc) + sh


def _nm_matmul_kernel(x_ref, g_ref, sh_ref, sc_ref, w_ref, b_ref, *out_refs, splits):
    a = _norm_mod(x_ref[0], g_ref[...], sh_ref[0, 0], sc_ref[0, 0])
    z = jnp.dot(a.astype(BF16), w_ref[...], preferred_element_type=F32) + b_ref[...]
    for (lo, hi), o_ref in zip(splits, out_refs):
        o_ref[0] = z[:, lo:hi].astype(o_ref.dtype)


def _seg_map(b, i):
    return (b, jnp.minimum(i, 1), 0, 0)


def _nm_matmul(h, g, shift, scale, w, bias, widths):
    bsz, lt, d = h.shape
    p = w.shape[1]
    splits, lo = [], 0
    for wd in widths:
        splits.append((lo, lo + wd))
        lo += wd
    assert lo == p
    tm = ROW_TILE
    return pl.pallas_call(
        functools.partial(_nm_matmul_kernel, splits=tuple(splits)),
        grid=(bsz, lt // tm),
        in_specs=[
            pl.BlockSpec((1, tm, d), lambda b, i: (b, i, 0)),
            pl.BlockSpec((1, d), lambda b, i: (0, 0)),
            pl.BlockSpec((1, 1, 1, d), _seg_map),
            pl.BlockSpec((1, 1, 1, d), _seg_map),
            pl.BlockSpec((d, p), lambda b, i: (0, 0)),
            pl.BlockSpec((1, p), lambda b, i: (0, 0)),
        ],
        out_specs=[pl.BlockSpec((1, tm, wd), lambda b, i: (b, i, 0)) for wd in widths],
        out_shape=[jax.ShapeDtypeStruct((bsz, lt, wd), F32) for wd in widths],
        compiler_params=_cparams(("parallel", "parallel")),
        name="norm_mod_matmul",
    )(h, g.reshape(1, d), shift, scale, w.astype(BF16), bias.reshape(1, p))


def _ffn_prep_kernel(x_ref, g_ref, sh_ref, sc_ref, rw_ref, rb_ref, f_ref, logit_ref):
    a = _norm_mod(x_ref[0], g_ref[...], sh_ref[0, 0], sc_ref[0, 0])
    f_ref[0] = a.astype(f_ref.dtype)
    logit_ref[0] = jnp.dot(a.astype(BF16), rw_ref[...].astype(BF16), preferred_element_type=F32) + rb_ref[...]


def _ffn_prep(h, g, shift, scale, router_w, router_b):
    bsz, lt, d = h.shape
    ne = router_w.shape[1]
    tm = ROW_TILE
    return pl.pallas_call(
        _ffn_prep_kernel,
        grid=(bsz, lt // tm),
        in_specs=[
            pl.BlockSpec((1, tm, d), lambda b, i: (b, i, 0)),
            pl.BlockSpec((1, d), lambda b, i: (0, 0)),
            pl.BlockSpec((1, 1, 1, d), _seg_map),
            pl.BlockSpec((1, 1, 1, d), _seg_map),
            pl.BlockSpec((d, ne), lambda b, i: (0, 0)),
            pl.BlockSpec((1, ne), lambda b, i: (0, 0)),
        ],
        out_specs=[
            pl.BlockSpec((1, tm, d), lambda b, i: (b, i, 0)),
            pl.BlockSpec((1, tm, ne), lambda b, i: (b, i, 0)),
        ],
        out_shape=[
            jax.ShapeDtypeStruct((bsz, lt, d), BF16),
            jax.ShapeDtypeStruct((bsz, lt, ne), F32),
        ],
        compiler_params=_cparams(("parallel", "parallel")),
        name="ffn_prep",
    )(h, g.reshape(1, d), shift, scale, router_w, router_b.reshape(1, ne))


GU_BLOCK = 256


def _moe_kernel(be_ref, nb_ref, first_ref, slot_ref, nxt_ref, x_ref, wgu_hbm, wd_hbm, bg_ref, bu_ref, bd_ref,
                o_ref, wgu_buf, wd_buf, wgu_s, wd_s, sem, *, layer):
    i = pl.program_id(0)
    active = i < nb_ref[0]
    half = GU_BLOCK // 2
    nblk = wgu_s.shape[1] // GU_BLOCK

    def weight_copies(e, slot):
        return (pltpu.make_async_copy(wgu_hbm.at[layer, e], wgu_buf.at[slot], sem.at[0, slot]),
                pltpu.make_async_copy(wd_hbm.at[layer, e], wd_buf.at[slot], sem.at[1, slot]))

    @pl.when(active & (i == 0))
    def _():
        for cp in weight_copies(be_ref[0], 0):
            cp.start()

    @pl.when(active & (first_ref[i] == 1))
    def _():
        slot = slot_ref[i]
        for cp in weight_copies(be_ref[i], slot):
            cp.wait()

        @pl.when(nxt_ref[i] >= 0)
        def _():
            for cp in weight_copies(nxt_ref[i], 1 - slot):
                cp.start()

        r = lax.broadcasted_iota(jnp.int32, (GU_BLOCK, GU_BLOCK), 0)
        c = lax.broadcasted_iota(jnp.int32, (GU_BLOCK, GU_BLOCK), 1)
        perm = (r == jnp.where(c < half, 2 * c, 2 * (c - half) + 1)).astype(BF16)
        for k in range(nblk):
            cs = slice(k * GU_BLOCK, (k + 1) * GU_BLOCK)
            wgu_s[:, cs] = jnp.dot(wgu_buf[slot, :, cs].astype(BF16), perm,
                                   preferred_element_type=F32).astype(BF16)
        wd_s[...] = wd_buf[slot].astype(BF16)

    @pl.when(active)
    def _():
        gu = jnp.dot(x_ref[...], wgu_s[...], preferred_element_type=F32)
        hdn = []
        for k in range(nblk):
            hs = slice(k * half, (k + 1) * half)
            g = gu[:, k * GU_BLOCK:k * GU_BLOCK + half] + bg_ref[0, :, hs]
            u = gu[:, k * GU_BLOCK + half:(k + 1) * GU_BLOCK] + bu_ref[0, :, hs]
            g = jnp.minimum(g, SWIGLU_LIMIT)
            u = jnp.clip(u, -SWIGLU_LIMIT, SWIGLU_LIMIT)
            hdn.append(((u + 1.0) * (g * jax.nn.sigmoid(SWIGLU_ALPHA * g))).astype(BF16))
        hdn = jnp.concatenate(hdn, axis=-1)
        o_ref[...] = (jnp.dot(hdn, wd_s[...], preferred_element_type=F32) + bd_ref[0]).astype(o_ref.dtype)

    @pl.when(jnp.logical_not(active))
    def _():
        o_ref[...] = jnp.zeros_like(o_ref)


def _moe_experts(x_sorted, block_expert, n_used, layer, w_gu, w_down, bg, bu, bd):
    n_rows, d = x_sorted.shape
    _, ne, _, f2 = w_gu.shape
    f = f2 // 2
    tm = MOE_TILE
    n_blocks = n_rows // tm
    assert f2 % GU_BLOCK == 0
    blk = jnp.arange(n_blocks, dtype=jnp.int32)
    prev = jnp.concatenate([block_expert[:1], block_expert[:-1]])
    first = (blk < n_used[0]) & ((blk == 0) | (block_expert != prev))
    slot = (jnp.cumsum(first.astype(jnp.int32)) - 1) & 1
    first_idx = jnp.where(first, blk, n_blocks)
    next_first = lax.cummin(first_idx, axis=0, reverse=True)
    next_first = jnp.concatenate([next_first[1:], jnp.full((1,), n_blocks, jnp.int32)])
    nxt = jnp.where(next_first < n_blocks, block_expert[jnp.minimum(next_first, n_blocks - 1)], -1)
    bmap = lambda i, be, nb, fi, sl, nx: (be[i], 0, 0)
    rmap = lambda i, be, nb, fi, sl, nx: (i, 0)
    grid_spec = pltpu.PrefetchScalarGridSpec(
        num_scalar_prefetch=5,
        grid=(n_blocks,),
        in_specs=[
            pl.BlockSpec((tm, d), rmap),
            pl.BlockSpec(memory_space=pl.ANY),
            pl.BlockSpec(memory_space=pl.ANY),
            pl.BlockSpec((1, 1, f), bmap),
            pl.BlockSpec((1, 1, f), bmap),
            pl.BlockSpec((1, 1, d), bmap),
        ],
        out_specs=pl.BlockSpec((tm, d), rmap),
        scratch_shapes=[pltpu.VMEM((2, d, f2), F32), pltpu.VMEM((2, f, d), F32),
                        pltpu.VMEM((d, f2), BF16), pltpu.VMEM((f, d), BF16),
                        pltpu.SemaphoreType.DMA((2, 2))],
    )
    return pl.pallas_call(
        functools.partial(_moe_kernel, layer=layer),
        grid_spec=grid_spec,
        out_shape=jax.ShapeDtypeStruct((n_rows, d), BF16),
        compiler_params=_cparams(("arbitrary",)),
        name="moe_experts",
    )(block_expert, n_used, first.astype(jnp.int32), slot.astype(jnp.int32), nxt.astype(jnp.int32),
      x_sorted, w_gu, w_down, bg.reshape(ne, 1, f), bu.reshape(ne, 1, f), bd.reshape(ne, 1, d))


def _combine_kernel(y_ref, gt_ref, h_ref, g2_ref, o_ref):
    gt = gt_ref[0]
    acc = y_ref[0, 0].astype(F32) * gt[:, 0:1]
    for k in range(1, TOP_K):
        acc = acc + y_ref[k, 0].astype(F32) * gt[:, k:k + 1]
    o_ref[0] = h_ref[0] + g2_ref[0, 0] * acc


def _moe_combine(yg, gate, h, g2, seg_map):
    k, bsz, lt, d = yg.shape
    tm = ROW_TILE
    return pl.pallas_call(
        _combine_kernel,
        grid=(bsz, lt // tm),
        in_specs=[
            pl.BlockSpec((k, 1, tm, d), lambda b, i: (0, b, i, 0)),
            pl.BlockSpec((1, tm, k), lambda b, i: (b, i, 0)),
            pl.BlockSpec((1, tm, d), lambda b, i: (b, i, 0)),
            pl.BlockSpec((1, 1, 1, d), seg_map),
        ],
        out_specs=pl.BlockSpec((1, tm, d), lambda b, i: (b, i, 0)),
        out_shape=jax.ShapeDtypeStruct((bsz, lt, d), F32),
        compiler_params=_cparams(("parallel", "parallel")),
        name="moe_combine",
    )(yg, gate, h, g2)


def _lat_seg_map(b, i):
    return (b, 1, 0, 0)


def _moe_layer(h, f, logits, g2, weights, lat_only, n_ctx):
    layer, w_gu, w_down, bg, bu, bd = weights
    bsz, lt, d = h.shape
    if lat_only:
        h_in, f, logits = h[:, n_ctx:], f[:, n_ctx:], logits[:, n_ctx:]
    else:
        h_in = h
    ltok = h_in.shape[1]
    n = bsz * ltok
    top_logit, top_e = lax.top_k(logits.reshape(n, -1), TOP_K)
    gate = jax.nn.softmax(top_logit, axis=-1)
    n_assign = n * TOP_K
    tm = MOE_TILE
    flat_e = top_e.reshape(-1).astype(jnp.int32)
    iota = jnp.arange(n_assign, dtype=jnp.int32)
    e_sorted, order = lax.sort_key_val(flat_e, iota, is_stable=True)
    experts = jnp.arange(N_EXPERTS, dtype=jnp.int32)
    start = jnp.searchsorted(e_sorted, experts, side='left', method='compare_all').astype(jnp.int32)
    counts = jnp.searchsorted(e_sorted, experts, side='right', method='compare_all').astype(jnp.int32) - start
    padded = (counts + tm - 1) // tm * tm
    pad_end = jnp.cumsum(padded)
    pad_start = pad_end - padded
    dest = pad_start[e_sorted] + iota - start[e_sorted]
    n_blocks = -(-(n_assign + N_EXPERTS * (tm - 1)) // tm)
    n_rows = n_blocks * tm
    block_expert = jnp.minimum(
        jnp.searchsorted(pad_end, jnp.arange(n_blocks, dtype=jnp.int32) * tm, side='right', method='compare_all'),
        N_EXPERTS - 1).astype(jnp.int32)
    n_used = (pad_end[-1] // tm).astype(jnp.int32).reshape(1)
    row = jnp.arange(n_rows, dtype=jnp.int32).reshape(n_blocks, tm)
    blk_shift = (start - pad_start)[block_expert][:, None]
    blk_end = (pad_start + counts)[block_expert][:, None]
    slot = jnp.clip(row + blk_shift, 0, n_assign - 1).reshape(-1)
    row_token = jnp.where((row < blk_end).reshape(-1),
                          order.at[slot].get(mode='promise_in_bounds') // TOP_K, row.reshape(-1) % n)
    _, pos = lax.sort_key_val(order, dest, is_stable=True)
    x_sorted = f.reshape(n, d).at[row_token].get(mode='promise_in_bounds')
    y = _moe_experts(x_sorted, block_expert, n_used, layer, w_gu, w_down, bg, bu, bd)
    pos_k = pos.reshape(n, TOP_K).T
    yg = y.at[pos_k.reshape(-1)].get(mode='promise_in_bounds').reshape(TOP_K, bsz, ltok, d)
    out = _moe_combine(yg, gate.reshape(bsz, ltok, TOP_K), h_in, g2,
                       _lat_seg_map if lat_only else _seg_map)
    return out


def _rmsnorm_kernel(x_ref, g_ref, o_ref):
    x = x_ref[0]
    ms = jnp.mean(x * x, axis=-1, keepdims=True)
    o_ref[0] = x * lax.rsqrt(ms + EPS) * g_ref[...]


def _final_norm(h, g):
    bsz, lt, d = h.shape
    tm = ROW_TILE
    return pl.pallas_call(
        _rmsnorm_kernel,
        grid=(bsz, lt // tm),
        in_specs=[pl.BlockSpec((1, tm, d), lambda b, i: (b, i, 0)),
                  pl.BlockSpec((1, d), lambda b, i: (0, 0))],
        out_specs=pl.BlockSpec((1, tm, d), lambda b, i: (b, i, 0)),
        out_shape=jax.ShapeDtypeStruct((bsz, lt, d), F32),
        compiler_params=_cparams(("parallel", "parallel")),
        name="final_norm",
    )(h, g.reshape(1, d))


def _grid_reorder_kernel(x_ref, o_ref, *, n_ctx, rows, to_cols):
    o_ref[0, :n_ctx, :] = x_ref[0, :n_ctx, :]
    for c in range(GRID_W):
        raster = pl.ds(n_ctx + c, rows, stride=GRID_W)
        dense = pl.ds(n_ctx + c * rows, rows)
        if to_cols:
            o_ref.at[0][dense, :] = x_ref.at[0][raster, :]
        else:
            o_ref.at[0][raster, :] = x_ref.at[0][dense, :]


def _grid_reorder(h, n_ctx, to_cols):
    bsz, lt, d = h.shape
    lanes = 128
    spec = pl.BlockSpec((1, lt, lanes), lambda b, j: (b, 0, j))
    return pl.pallas_call(
        functools.partial(_grid_reorder_kernel, n_ctx=n_ctx, rows=(lt - n_ctx) // GRID_W, to_cols=to_cols),
        grid=(bsz, d // lanes),
        in_specs=[spec],
        out_specs=spec,
        out_shape=jax.ShapeDtypeStruct(h.shape, h.dtype),
        compiler_params=_cparams(("parallel", "parallel")),
        name="grid_reorder",
    )(h)


def _chunk_order(d, c, n_ctx_chunks, n_chunks):
    bwd = jnp.where(c < n_ctx_chunks, n_ctx_chunks - 1 - c, n_chunks + n_ctx_chunks - 1 - c)
    return jnp.where(d == 0, c, bwd)


def _dir_tri(d, t):
    row = lax.broadcasted_iota(jnp.int32, (t, t), 0)
    col = lax.broadcasted_iota(jnp.int32, (t, t), 1)
    return jnp.where(d == 0, col - row, row - col) <= 0


def _gla_kernel(q_ref, k_ref, v_ref, r_ref, w2_ref, b2_ref, o_ref, st_ref, *, t, heads, scale):
    d = pl.program_id(0)
    c = pl.program_id(2)

    @pl.when(c == 0)
    def _():
        st_ref[...] = jnp.zeros_like(st_ref)

    dk = q_ref.shape[-1] // heads
    dv = v_ref.shape[-1] // heads
    nb = q_ref.shape[0]
    mask = _dir_tri(d, t)
    tri = mask.astype(F32)
    mid = t // 2
    items = [(bb, h) for bb in range(nb) for h in range(heads)]
    xs = [jnp.dot(r_ref[bb], w2_ref[0], preferred_element_type=F32, precision=HI) + b2_ref[0] for bb in range(nb)]
    las = [jax.nn.log_sigmoid(x) * (1.0 / GLA_TAU) for x in xs]
    bs = [jnp.dot(tri, la, preferred_element_type=F32, precision=HI) for la in las]
    qt, kt, qe, kh_end, e_end = [], [], [], [], []
    for bb in range(nb):
        b = bs[bb]
        b_m = b[mid:mid + 1, :]
        b_end = jnp.where(d == 0, b[t - 1:t, :], b[0:1, :])
        q_s = q_ref[bb] * (jnp.exp(b - b_m) * scale)
        k_s = k_ref[bb] * jnp.exp(b_m - b)
        qe.append((q_s * jnp.exp(b_m)).astype(BF16))
        kh_end.append((k_s * jnp.exp(b_end - b_m)).astype(BF16))
        e_end.append(jnp.exp(b_end))
        qt.append(q_s.astype(BF16))
        kt.append(k_s.astype(BF16))
    att, q_st, vs = {}, {}, {}
    for bb, h in items:
        ks = slice(h * dk, (h + 1) * dk)
        vs[bb, h] = v_ref[bb, :, h * dv:(h + 1) * dv].astype(BF16)
        att[bb, h] = lax.dot_general(qt[bb][:, ks], kt[bb][:, ks], NT, preferred_element_type=F32)
        q_st[bb, h] = lax.dot_general(qe[bb][:, ks], st_ref[bb * heads + h].astype(BF16), NT,
                                      preferred_element_type=F32)
    for bb, h in items:
        a = jnp.where(mask, att[bb, h], 0.0).astype(BF16)
        o_ref[0, bb, :, h * dv:(h + 1) * dv] = jnp.dot(a, vs[bb, h], preferred_element_type=F32) + q_st[bb, h]
    for bb, h in items:
        ks = slice(h * dk, (h + 1) * dk)
        upd = lax.dot_general(vs[bb, h], kh_end[bb][:, ks], TN, preferred_element_type=F32)
        st_ref[bb * heads + h] = st_ref[bb * heads + h] * e_end[bb][:, ks] + upd


def _gla_mixer(q, k, v, r, w2, b2, n_ctx):
    bsz, lt, dkt = q.shape
    dvt = v.shape[-1]
    nr = r.shape[-1]
    t, heads = MIX_CHUNK, GLA_HEADS
    bt = MIX_BATCH
    nch = lt // t
    ncc = n_ctx // t
    dk = dkt // heads
    dv = dvt // heads
    imap = lambda d, b, c: (b, _chunk_order(d, c, ncc, nch), 0)
    return pl.pallas_call(
        functools.partial(_gla_kernel, t=t, heads=heads, scale=dk ** -0.5),
        grid=(2, bsz // bt, nch),
        in_specs=[
            pl.BlockSpec((bt, t, dkt), imap),
            pl.BlockSpec((bt, t, dkt), imap),
            pl.BlockSpec((bt, t, dvt), imap),
            pl.BlockSpec((bt, t, nr), imap),
            pl.BlockSpec((1, nr, dkt), lambda d, b, c: (d, 0, 0)),
            pl.BlockSpec((1, 1, dkt), lambda d, b, c: (d, 0, 0)),
        ],
        out_specs=pl.BlockSpec((1, bt, t, dvt), lambda d, b, c: (d, b, _chunk_order(d, c, ncc, nch), 0)),
        out_shape=jax.ShapeDtypeStruct((2, bsz, lt, dvt), F32),
        scratch_shapes=[pltpu.VMEM((bt * heads, dv, dk), F32)],
        compiler_params=_cparams(("parallel", "parallel", "arbitrary")),
        name="gla_mixer",
    )(q, k, v, r, w2, b2)


def _mlstm_kernel(q_ref, k_ref, v_ref, gc_ref, gr_ref, o_ref, c_ref, n_ref, m_ref, *, t, heads):
    d = pl.program_id(0)
    c = pl.program_id(2)

    @pl.when(c == 0)
    def _():
        c_ref[...] = jnp.zeros_like(c_ref)
        n_ref[...] = jnp.zeros_like(n_ref)
        m_ref[...] = jnp.zeros_like(m_ref)

    dh = q_ref.shape[-1] // heads
    nb = q_ref.shape[0]
    mask = _dir_tri(d, t)
    tri = mask.astype(F32)
    items = [(bb, h) for bb in range(nb) for h in range(heads)]
    gate = []
    for bb in range(nb):
        gc = gc_ref[0, bb]
        gr = gr_ref[0, bb, 0]
        fc = jax.nn.log_sigmoid(gc[:, heads:])
        fr = jax.nn.log_sigmoid(gr[heads:, :])
        b_col = jnp.dot(tri, fc, preferred_element_type=F32, precision=HI)
        b_row = lax.dot_general(fr, tri, NT, preferred_element_type=F32, precision=HI)
        b_last = jnp.where(d == 0, b_col[t - 1:t, :], b_col[0:1, :])
        gate.append((gc[:, :heads], gr[:heads, :], b_col, b_row, b_last))
    qs, ks, vs, s_raw, q_c = {}, {}, {}, {}, {}
    for bb, h in items:
        hs = slice(h * dh, (h + 1) * dh)
        qs[bb, h] = q_ref[bb, :, hs]
        ks[bb, h] = k_ref[bb, :, hs]
        vs[bb, h] = v_ref[bb, :, hs].astype(BF16)
        s_raw[bb, h] = lax.dot_general(qs[bb, h], ks[bb, h], NT, preferred_element_type=F32)
        q_c[bb, h] = jnp.dot(qs[bb, h], c_ref[bb * heads + h].astype(BF16), preferred_element_type=F32)
    logw, log_inter, m_t, w_inter, scores, den, qn = {}, {}, {}, {}, {}, {}, {}
    for bb, h in items:
        _, ir, b_col, b_row, _ = gate[bb]
        bc = b_col[:, h:h + 1]
        logw[bb, h] = jnp.where(mask, bc - b_row[h:h + 1, :] + ir[h:h + 1, :], -jnp.inf)
        log_inter[bb, h] = bc + m_ref[bb * heads + h]
    for bb, h in items:
        m_t[bb, h] = jnp.maximum(log_inter[bb, h], jnp.max(logw[bb, h], axis=-1, keepdims=True))
        qn[bb, h] = jnp.sum(qs[bb, h].astype(F32) * n_ref[bb * heads + h], axis=-1, keepdims=True)
    for bb, h in items:
        w_inter[bb, h] = jnp.exp(log_inter[bb, h] - m_t[bb, h])
        scores[bb, h] = s_raw[bb, h] * jnp.exp(logw[bb, h] - m_t[bb, h])
    for bb, h in items:
        den[bb, h] = jnp.sum(scores[bb, h], axis=-1, keepdims=True) + w_inter[bb, h] * qn[bb, h]
    num = {}
    for bb, h in items:
        num[bb, h] = (jnp.dot(scores[bb, h].astype(BF16), vs[bb, h], preferred_element_type=F32)
                      + w_inter[bb, h] * q_c[bb, h])
    for bb, h in items:
        hs = slice(h * dh, (h + 1) * dh)
        o_ref[0, bb, :, hs] = num[bb, h] / jnp.maximum(jnp.abs(den[bb, h]), jnp.exp(-m_t[bb, h]))
    log_g, m_new, kw, upd, ksum = {}, {}, {}, {}, {}
    for bb, h in items:
        ic, _, b_col, _, b_last = gate[bb]
        log_g[bb, h] = b_last[:, h:h + 1] - b_col[:, h:h + 1] + ic[:, h:h + 1]
    for bb, h in items:
        b_last = gate[bb][4]
        m_new[bb, h] = jnp.maximum(b_last[:, h:h + 1] + m_ref[bb * heads + h],
                                   jnp.max(log_g[bb, h], axis=0, keepdims=True))
    for bb, h in items:
        kw[bb, h] = ks[bb, h].astype(F32) * jnp.exp(log_g[bb, h] - m_new[bb, h])
    for bb, h in items:
        upd[bb, h] = lax.dot_general(kw[bb, h].astype(BF16), vs[bb, h], TN, preferred_element_type=F32)
        ksum[bb, h] = jnp.sum(kw[bb, h], axis=0, keepdims=True)
    for bb, h in items:
        si = bb * heads + h
        b_last = gate[bb][4]
        keep = jnp.exp(b_last[:, h:h + 1] + m_ref[si] - m_new[bb, h])
        c_ref[si] = keep * c_ref[si] + upd[bb, h]
        n_ref[si] = keep * n_ref[si] + ksum[bb, h]
        m_ref[si] = m_new[bb, h]


def _mlstm_mixer(qk, v, gates, n_ctx):
    bsz, lt, w2 = qk.shape
    w = w2 // 2
    t, heads = MIX_CHUNK, MLSTM_HEADS
    dh = w // heads
    nch = lt // t
    ncc = n_ctx // t
    gc = gates.reshape(bsz, lt, 2, 2 * heads).transpose(2, 0, 1, 3)
    gr = gc.reshape(2, bsz, nch, t, 2 * heads).transpose(0, 1, 2, 4, 3)
    cmap = lambda d, b, c: _chunk_order(d, c, ncc, nch)
    bt = MIX_BATCH
    return pl.pallas_call(
        functools.partial(_mlstm_kernel, t=t, heads=heads),
        grid=(2, bsz // bt, nch),
        in_specs=[
            pl.BlockSpec((bt, t, w), lambda d, b, c: (b, cmap(d, b, c), 0)),
            pl.BlockSpec((bt, t, w), lambda d, b, c: (b, cmap(d, b, c), 1)),
            pl.BlockSpec((bt, t, w), lambda d, b, c: (b, cmap(d, b, c), 0)),
            pl.BlockSpec((1, bt, t, 2 * heads), lambda d, b, c: (d, b, cmap(d, b, c), 0)),
            pl.BlockSpec((1, bt, 1, 2 * heads, t), lambda d, b, c: (d, b, cmap(d, b, c), 0, 0)),
        ],
        out_specs=pl.BlockSpec((1, bt, t, w), lambda d, b, c: (d, b, cmap(d, b, c), 0)),
        out_shape=jax.ShapeDtypeStruct((2, bsz, lt, w), F32),
        scratch_shapes=[pltpu.VMEM((bt * heads, dh, dh), F32), pltpu.VMEM((bt * heads, 1, dh), F32),
                        pltpu.VMEM((bt * heads, 1, 1), F32)],
        compiler_params=_cparams(("parallel", "parallel", "arbitrary")),
        name="mlstm_mixer",
    )(qk, qk, v, gc, gr)


def _conv_kernel(x_ref, w_ref, b_ref, s_ref, o_ref, *, n_ctx):
    x = x_ref[0]
    lt = x.shape[0]
    row = lax.broadcasted_iota(jnp.int32, x.shape, 0)
    prev = jnp.where((row == 0) | (row == n_ctx), 0.0, pltpu.roll(x, 1, 0))
    nxt = jnp.where((row == n_ctx - 1) | (row == lt - 1), 0.0, pltpu.roll(x, lt - 1, 0))
    y = b_ref[...] + w_ref[0:1, :] * prev + w_ref[1:2, :] * x + w_ref[2:3, :] * nxt
    o_ref[0] = (y * jax.nn.sigmoid(y) * s_ref[...]).astype(o_ref.dtype)


def _conv_silu(x, w, b, colscale, n_ctx):
    bsz, lt, ch = x.shape
    tc = 256
    return pl.pallas_call(
        functools.partial(_conv_kernel, n_ctx=n_ctx),
        grid=(bsz, ch // tc),
        in_specs=[
            pl.BlockSpec((1, lt, tc), lambda b, j: (b, 0, j)),
            pl.BlockSpec((3, tc), lambda b, j: (0, j)),
            pl.BlockSpec((1, tc), lambda b, j: (0, j)),
            pl.BlockSpec((1, tc), lambda b, j: (0, j)),
        ],
        out_specs=pl.BlockSpec((1, lt, tc), lambda b, j: (b, 0, j)),
        out_shape=jax.ShapeDtypeStruct((bsz, lt, ch), BF16),
        compiler_params=_cparams(("parallel", "parallel")),
        name="conv_silu",
    )(x, w, b.reshape(1, ch), colscale.reshape(1, ch))


def _s5_matrices(a_re, a_im, log_dt, b_re, b_im, c_re, c_im, backward, lane_groups=8):
    g, p = a_re.shape
    cg = b_re.shape[-1]
    j = S5_J
    lg = lane_groups
    nq = g // lg
    dt = jnp.exp(log_dt)[:, None]
    lam_re = jnp.minimum(a_re, -1e-4)
    lam_im = a_im
    decay = jnp.exp(lam_re * dt)
    ab_re = decay * jnp.cos(lam_im * dt)
    ab_im = decay * jnp.sin(lam_im * dt)
    den = lam_re * lam_re + lam_im * lam_im
    zr = ((ab_re - 1) * lam_re + ab_im * lam_im) / den
    zi = (ab_im * lam_re - (ab_re - 1) * lam_im) / den
    bb_re = zr[..., None] * b_re - zi[..., None] * b_im
    bb_im = zr[..., None] * b_im + zi[..., None] * b_re
    pw_re, pw_im = [jnp.ones_like(ab_re)], [jnp.zeros_like(ab_im)]
    for _ in range(j):
        r0, i0 = pw_re[-1], pw_im[-1]
        pw_re.append(ab_re * r0 - ab_im * i0)
        pw_im.append(ab_re * i0 + ab_im * r0)
    pw_re, pw_im = jnp.stack(pw_re), jnp.stack(pw_im)
    ca_re = c_re[None] * pw_re[:, :, None, :] - c_im[None] * pw_im[:, :, None, :]
    ca_im = c_re[None] * pw_im[:, :, None, :] + c_im[None] * pw_re[:, :, None, :]
    kk = (jnp.einsum('tgcp,gpd->tgcd', ca_re[:j], bb_re, precision=HI)
          - jnp.einsum('tgcp,gpd->tgcd', ca_im[:j], bb_im, precision=HI))
    ab_pw_re = pw_re[:j, :, :, None] * bb_re[None] - pw_im[:j, :, :, None] * bb_im[None]
    ab_pw_im = pw_re[:j, :, :, None] * bb_im[None] + pw_im[:j, :, :, None] * bb_re[None]
    eye = jnp.eye(lg, dtype=BF16)
    lb = lg * cg
    sw = lg * p
    bd_k = jnp.einsum('tqgcd,gh->tqgdhc', kk.astype(BF16).reshape(j, nq, lg, cg, cg), eye
                      ).reshape(j, nq, lb, lb)
    bd_in = [jnp.einsum('tqgpc,gh->tqgchp', a.astype(BF16).reshape(j, nq, lg, p, cg), eye
                        ).reshape(j, nq, lb, sw) for a in (ab_pw_re, ab_pw_im)]
    bd_out = [jnp.einsum('tqgcp,gh->tqgphc', a.astype(BF16).reshape(j + 1, nq, lg, cg, p), eye
                         ).reshape(j + 1, nq, sw, lb) for a in (ca_re, -ca_im)]
    jj = jnp.arange(j)
    lag = (jj[:, None] - jj[None, :]) if backward else (jj[None, :] - jj[:, None])
    kt = jnp.where((lag >= 0)[:, :, None, None, None], bd_k[jnp.clip(lag, 0, j - 1)], 0)
    ktoep = kt.transpose(2, 0, 3, 1, 4).reshape(nq, j * lb, j * lb)
    tau_in = jj if backward else (j - 1 - jj)
    win_re, win_im = (a[tau_in].transpose(1, 0, 2, 3).reshape(nq, j * lb, sw) for a in bd_in)
    tau_out = (j - jj) if backward else (jj + 1)
    wout_re, wout_im = (a[tau_out].transpose(1, 2, 0, 3).reshape(nq, sw, j * lb) for a in bd_out)
    dec_re = pw_re[j].reshape(nq, 1, sw)
    dec_im = pw_im[j].reshape(nq, 1, sw)
    return ktoep, win_re, win_im, wout_re, wout_im, dec_re, dec_im


def _s5_kernel(u_ref, kt_ref, wir_ref, wii_ref, wor_ref, woi_ref, dr_ref, di_ref, y_ref,
               xf_ref, yf_ref, sre_ref, sim_ref, *, bt, nk, nk_ctx, rs):
    d = pl.program_id(0)
    j = S5_J
    lanes = u_ref.shape[-1]
    for b in range(bt):
        for jj in range(j):
            xf_ref[b * nk:(b + 1) * nk, jj * lanes:(jj + 1) * lanes] = (
                u_ref.at[b][pl.ds(jj, nk, stride=j), :].astype(BF16))
    xf = xf_ref[...]
    yf_ref[...] = jnp.dot(xf, kt_ref[0, 0], preferred_element_type=F32)
    inc_re = jnp.dot(xf, wir_ref[0, 0], preferred_element_type=F32)
    inc_im = jnp.dot(xf, wii_ref[0, 0], preferred_element_type=F32)
    nl = sre_ref.shape[0]
    for b in range(bt):
        for l in range(nl):
            sre_ref[l, b * rs:b * rs + nk, :] = inc_re[b * nk:(b + 1) * nk, l * lanes:(l + 1) * lanes]
            sim_ref[l, b * rs:b * rs + nk, :] = inc_im[b * nk:(b + 1) * nk, l * lanes:(l + 1) * lanes]
    a_re = [dr_ref[0, 0, :, l * lanes:(l + 1) * lanes] for l in range(nl)]
    a_im = [di_ref[0, 0, :, l * lanes:(l + 1) * lanes] for l in range(nl)]

    def step(kidx, carry):
        rows = pl.ds(kidx, bt, stride=rs)
        out = []
        for l in range(nl):
            s_re, s_im = carry[2 * l], carry[2 * l + 1]
            i_re = sre_ref.at[l][rows, :]
            i_im = sim_ref.at[l][rows, :]
            sre_ref.at[l][rows, :] = s_re
            sim_ref.at[l][rows, :] = s_im
            out.append(a_re[l] * s_re - a_im[l] * s_im + i_re)
            out.append(a_re[l] * s_im + a_im[l] * s_re + i_im)
        return tuple(out)

    zero = tuple(jnp.zeros((bt, lanes), F32) for _ in range(2 * nl))

    @pl.when(d == 0)
    def _():
        lax.fori_loop(0, nk, step, zero)

    @pl.when(d == 1)
    def _():
        carry = lax.fori_loop(0, nk_ctx, lambda i, cr: step(nk_ctx - 1 - i, cr), zero)
        lax.fori_loop(0, nk - nk_ctx, lambda i, cr: step(nk - 1 - i, cr), carry)

    for b in range(bt):
        sp_re = jnp.concatenate([sre_ref[l, b * rs:b * rs + nk, :] for l in range(nl)], axis=-1).astype(BF16)
        sp_im = jnp.concatenate([sim_ref[l, b * rs:b * rs + nk, :] for l in range(nl)], axis=-1).astype(BF16)
        yb = (yf_ref[b * nk:(b + 1) * nk, :]
              + jnp.dot(sp_re, wor_ref[0, 0], preferred_element_type=F32)
              + jnp.dot(sp_im, woi_ref[0, 0], preferred_element_type=F32))
        for jj in range(j):
            y_ref.at[0, b][pl.ds(jj, nk, stride=j), :] = yb[:, jj * lanes:(jj + 1) * lanes]


def _s5_mixer(u, mats, n_ctx):
    bsz, lt, w = u.shape
    ktoep, win_re, win_im, wout_re, wout_im, dec_re, dec_im = mats
    lanes = 128
    bt = 4 if bsz % 4 == 0 else 2
    nq = w // lanes
    j = S5_J
    nk = lt // j
    nk_ctx = n_ctx // j
    rs = nk + 8
    fl = j * lanes
    sw = win_re.shape[-1]
    wmap = lambda d, q, b: (d, q, 0, 0)
    return pl.pallas_call(
        functools.partial(_s5_kernel, bt=bt, nk=nk, nk_ctx=nk_ctx, rs=rs),
        grid=(2, nq, bsz // bt),
        in_specs=[
            pl.BlockSpec((bt, lt, lanes), lambda d, q, b: (b, 0, q)),
            pl.BlockSpec((1, 1, fl, fl), wmap),
            pl.BlockSpec((1, 1, fl, sw), wmap),
            pl.BlockSpec((1, 1, fl, sw), wmap),
            pl.BlockSpec((1, 1, sw, fl), wmap),
            pl.BlockSpec((1, 1, sw, fl), wmap),
            pl.BlockSpec((1, 1, 1, sw), wmap),
            pl.BlockSpec((1, 1, 1, sw), wmap),
        ],
        out_specs=pl.BlockSpec((1, bt, lt, lanes), lambda d, q, b: (d, b, 0, q)),
        out_shape=jax.ShapeDtypeStruct((2, bsz, lt, w), F32),
        scratch_shapes=[pltpu.VMEM((bt * nk, fl), BF16), pltpu.VMEM((bt * nk, fl), F32),
                        pltpu.VMEM((sw // lanes, bt * rs, lanes), F32),
                        pltpu.VMEM((sw // lanes, bt * rs, lanes), F32)],
        compiler_params=_cparams(("parallel", "parallel", "arbitrary")),
        name="s5_mixer",
    )(u, ktoep, win_re, win_im, wout_re, wout_im, dec_re, dec_im)


def _head_norm(x, heads):
    dh = x.shape[-1] // heads
    outs = []
    for h in range(heads):
        xh = x[:, h * dh:(h + 1) * dh]
        outs.append(xh * lax.rsqrt(jnp.mean(xh * xh, axis=-1, keepdims=True) + EPS))
    return jnp.concatenate(outs, axis=-1)


def _even_post_kernel(m_ref, o_ref, s_ref, u_ref, mg_ref, dsk_ref, gw_ref, gb_ref, w_ref, h_ref, gate_ref,
                      out_ref, *, heads):
    m = m_ref[0, 0] + m_ref[1, 0]
    m_out = _head_norm(m, heads) * mg_ref[...] * jax.nn.sigmoid(o_ref[0])
    y = jax.nn.gelu(s_ref[0, 0] + s_ref[1, 0] + dsk_ref[...] * u_ref[0])
    glu = jnp.dot(y.astype(BF16), gw_ref[...], preferred_element_type=F32) + gb_ref[...]
    s_out = y * jax.nn.sigmoid(glu)
    cat = jnp.concatenate([m_out, s_out], axis=-1).astype(BF16)
    z = jnp.dot(cat, w_ref[...], preferred_element_type=F32)
    out_ref[0] = h_ref[0] + gate_ref[0, 0] * z


def _even_post(m2, o, s2, u, mnorm_g, d_skip, glu_w, glu_b, w_out, h, gate):
    bsz, lt, d = h.shape
    mw = o.shape[-1]
    sw = u.shape[-1]
    tm = ROW_TILE
    row = lambda b, i: (b, i, 0)
    row2 = lambda b, i: (0, b, i, 0)
    const = lambda b, i: (0, 0)
    return pl.pallas_call(
        functools.partial(_even_post_kernel, heads=MLSTM_HEADS),
        grid=(bsz, lt // tm),
        in_specs=[
            pl.BlockSpec((2, 1, tm, mw), row2),
            pl.BlockSpec((1, tm, mw), row),
            pl.BlockSpec((2, 1, tm, sw), row2),
            pl.BlockSpec((1, tm, sw), row),
            pl.BlockSpec((1, mw), const),
            pl.BlockSpec((1, sw), const),
            pl.BlockSpec((sw, sw), const),
            pl.BlockSpec((1, sw), const),
            pl.BlockSpec((mw + sw, d), const),
            pl.BlockSpec((1, tm, d), row),
            pl.BlockSpec((1, 1, 1, d), _seg_map),
        ],
        out_specs=pl.BlockSpec((1, tm, d), row),
        out_shape=jax.ShapeDtypeStruct((bsz, lt, d), F32),
        compiler_params=_cparams(("parallel", "parallel")),
        name="even_post",
    )(m2, o, s2, u, mnorm_g.reshape(1, mw), d_skip.reshape(1, sw), glu_w.astype(BF16), glu_b.reshape(1, sw),
      w_out.astype(BF16), h, gate)


def _odd_post_kernel(o_ref, g_ref, ng_ref, w_ref, h_ref, gate_ref, out_ref, *, heads):
    g = g_ref[0]
    y = _head_norm(o_ref[0, 0] + o_ref[1, 0], heads) * ng_ref[...] * (g * jax.nn.sigmoid(g))
    z = jnp.dot(y.astype(BF16), w_ref[...], preferred_element_type=F32)
    out_ref[0] = h_ref[0] + gate_ref[0, 0] * z


def _odd_post(o, g, norm_g, w_out, h, gate):
    bsz, lt, d = h.shape
    dv = o.shape[-1]
    tm = ROW_TILE
    row = lambda b, i: (b, i, 0)
    const = lambda b, i: (0, 0)
    return pl.pallas_call(
        functools.partial(_odd_post_kernel, heads=GLA_HEADS),
        grid=(bsz, lt // tm),
        in_specs=[
            pl.BlockSpec((2, 1, tm, dv), lambda b, i: (0, b, i, 0)),
            pl.BlockSpec((1, tm, dv), row),
            pl.BlockSpec((1, dv), const),
            pl.BlockSpec((dv, d), const),
            pl.BlockSpec((1, tm, d), row),
            pl.BlockSpec((1, 1, 1, d), _seg_map),
        ],
        out_specs=pl.BlockSpec((1, tm, d), row),
        out_shape=jax.ShapeDtypeStruct((bsz, lt, d), F32),
        compiler_params=_cparams(("parallel", "parallel")),
        name="odd_post",
    )(o, g, norm_g.reshape(1, dv), w_out.astype(BF16), h, gate)


def kernel(x, c, ctx, c_ctx, mod_w, mod_b, norm_mix_g, norm_ffn_g, ev_w_in, ev_b_in, ev_conv_w, ev_conv_b, ev_mlstm_norm_g, ev_s5_a_re_f, ev_s5_a_im_f, ev_s5_log_dt_f, ev_s5_a_re_b, ev_s5_a_im_b, ev_s5_log_dt_b, ev_s5_b_re, ev_s5_b_im, ev_s5_c_re, ev_s5_c_im, ev_s5_d, ev_s5_glu_w, ev_s5_glu_b, ev_w_out, od_w_in, od_gate_w2_f, od_gate_b2_f, od_gate_w2_b, od_gate_b2_b, od_norm_g, od_w_out, router_w, router_b, moe_w_gu, moe_b_gu, moe_w_down, moe_b_down, final_norm_g):
    bsz, seq, d = x.shape
    n_ctx = ctx.shape[1]
    depth = mod_w.shape[0]
    lt = n_ctx + seq
    assert n_ctx == ROW_TILE and seq % ROW_TILE == 0 and seq % GRID_W == 0

    h = jnp.concatenate([ctx, x], axis=1)
    c_all = jnp.concatenate([c, c_ctx[None, :]], axis=0)
    c_all = jnp.pad(c_all, ((0, (-c_all.shape[0]) % 8), (0, 0)))
    mods = _modulation(c_all, mod_w, mod_b)
    mod_lat = mods[:, :bsz]
    mod_ctx = jnp.broadcast_to(mods[:, bsz:bsz + 1], mod_lat.shape)
    mod6 = jnp.stack([mod_ctx, mod_lat], axis=2).reshape(depth, bsz, 2, 6, 1, d)

    bg_all = moe_b_gu[..., 0::2]
    bu_all = moe_b_gu[..., 1::2]

    mw = ev_conv_w.shape[-1] // 2
    n_gates = 4 * MLSTM_HEADS
    s5w = ev_s5_d.shape[-1]
    dk_t = od_gate_w2_f.shape[-1]
    dv_t = od_norm_g.shape[-1]
    for layer in range(depth):
        last = layer == depth - 1
        j = layer // 2
        m6 = mod6[layer]
        sh1, sc1, g1, sh2, sc2, g2 = (m6[:, :, i] for i in range(6))
        if layer % 2 == 0:
            w_in, b_in = ev_w_in[j], ev_b_in[j]
            cols = jnp.concatenate([jnp.arange(0, 4 * mw), jnp.arange(4 * mw + n_gates, 4 * mw + n_gates + s5w),
                                    jnp.arange(4 * mw, 4 * mw + n_gates)])
            qk_pre, v, o, u, gates = _nm_matmul(h, norm_mix_g[layer], sh1, sc1, w_in[:, cols], b_in[cols],
                                                (2 * mw, mw, mw, s5w, n_gates))
            dh = mw // MLSTM_HEADS
            colscale = jnp.concatenate([jnp.full((mw,), dh ** -0.5, F32), jnp.ones((mw,), F32)])
            qk = _conv_silu(qk_pre, ev_conv_w[j], ev_conv_b[j], colscale, n_ctx)
            m2 = _mlstm_mixer(qk, v, gates, n_ctx)
            shared = (ev_s5_b_re[j], ev_s5_b_im[j], ev_s5_c_re[j], ev_s5_c_im[j])
            mats_f = _s5_matrices(ev_s5_a_re_f[j], ev_s5_a_im_f[j], ev_s5_log_dt_f[j], *shared, backward=False)
            mats_b = _s5_matrices(ev_s5_a_re_b[j], ev_s5_a_im_b[j], ev_s5_log_dt_b[j], *shared, backward=True)
            s2 = _s5_mixer(u, tuple(jnp.stack([a, b]) for a, b in zip(mats_f, mats_b)), n_ctx)
            h = _even_post(m2, o, s2, u, ev_mlstm_norm_g[j], ev_s5_d[j], ev_s5_glu_w[j], ev_s5_glu_b[j],
                           ev_w_out[j], h, g1)
        else:
            hc = _grid_reorder(h, n_ctx, True)
            qq, kk, vv, gg, rr = _nm_matmul(hc, norm_mix_g[layer], sh1, sc1, od_w_in[j],
                                            jnp.zeros((od_w_in.shape[-1],), F32),
                                            (dk_t, dk_t, dv_t, dv_t, 2 * GLA_RANK))
            zero = jnp.zeros_like(od_gate_w2_f[j])
            w2 = jnp.stack([jnp.concatenate([od_gate_w2_f[j], zero], axis=0),
                            jnp.concatenate([zero, od_gate_w2_b[j]], axis=0)])
            b2 = jnp.stack([od_gate_b2_f[j], od_gate_b2_b[j]])[:, None, :]
            o2 = _gla_mixer(qq, kk, vv, rr, w2, b2, n_ctx)
            h = _grid_reorder(_odd_post(o2, gg, od_norm_g[j], od_w_out[j], hc, g1), n_ctx, False)
        f, logits = _ffn_prep(h, norm_ffn_g[layer], sh2, sc2, router_w[layer], router_b[layer])
        weights = (layer, moe_w_gu, moe_w_down, bg_all[layer], bu_all[layer], moe_b_down[layer])
        h = _moe_layer(h, f, logits, g2, weights, last, n_ctx)
    return _final_norm(h, final_norm_g)
```

```python
import functools

import jax
import jax.numpy as jnp
from jax import lax
from jax.experimental import pallas as pl
from jax.experimental.pallas import tpu as pltpu

F32 = jnp.float32
BF16 = jnp.bfloat16
HI = lax.Precision.HIGHEST

EPS = 1e-6
GRID_W = 64
MLSTM_HEADS = 4
S5_GROUP = 16
GLA_HEADS = 4
GLA_RANK = 16
GLA_TAU = 16.0
N_EXPERTS = 32
TOP_K = 4
SWIGLU_LIMIT = 7.0
SWIGLU_ALPHA = 1.702

ROW_TILE = 256
MOE_TILE = 512
MIX_CHUNK = 64
MLSTM_CHUNK = 128
MIX_BATCH = 4
S5_J = 8
VMEM_LIMIT = 56 * 1024 * 1024

NT = (((1,), (1,)), ((), ()))
TN = (((0,), (0,)), ((), ()))


def _cparams(sem):
    return pltpu.CompilerParams(dimension_semantics=sem, vmem_limit_bytes=VMEM_LIMIT)


def _mod_kernel(c_ref, w_ref, b_ref, o_ref):
    c = c_ref[...]
    a = c * jax.nn.sigmoid(c)
    o_ref[0] = jnp.dot(a.astype(BF16), w_ref[0].astype(BF16), preferred_element_type=F32) + b_ref[0]


def _modulation(c_all, mod_w, mod_b):
    depth, d, n6 = mod_w.shape
    rows = c_all.shape[0]
    tn = d
    return pl.pallas_call(
        _mod_kernel,
        grid=(depth, n6 // tn),
        in_specs=[
            pl.BlockSpec((rows, d), lambda l, j: (0, 0)),
            pl.BlockSpec((1, d, tn), lambda l, j: (l, 0, j)),
            pl.BlockSpec((1, 1, tn), lambda l, j: (l, 0, j)),
        ],
        out_specs=pl.BlockSpec((1, rows, tn), lambda l, j: (l, 0, j)),
        out_shape=jax.ShapeDtypeStruct((depth, rows, n6), F32),
        compiler_params=_cparams(("arbitrary", "arbitrary")),
        name="modulation",
    )(c_all, mod_w, mod_b.reshape(depth, 1, n6))


def _norm_mod(x, g, sh, sc):
    ms = jnp.mean(x * x, axis=-1, keepdims=True)
    return (x * lax.rsqrt(ms + EPS) * g) * (1.0 + sc) + sh


def _nm_matmul_kernel(x_ref, g_ref, sh_ref, sc_ref, w_ref, b_ref, *out_refs, splits):
    a = _norm_mod(x_ref[0], g_ref[...], sh_ref[0, 0], sc_ref[0, 0])
    z = jnp.dot(a.astype(BF16), w_ref[...], preferred_element_type=F32) + b_ref[...]
    for (lo, hi), o_ref in zip(splits, out_refs):
        o_ref[0] = z[:, lo:hi].astype(o_ref.dtype)


def _seg_map(b, i):
    return (b, jnp.minimum(i, 1), 0, 0)


def _nm_matmul(h, g, shift, scale, w, bias, widths):
    bsz, lt, d = h.shape
    p = w.shape[1]
    splits, lo = [], 0
    for wd in widths:
        splits.append((lo, lo + wd))
        lo += wd
    assert lo == p
    tm = ROW_TILE
    return pl.pallas_call(
        functools.partial(_nm_matmul_kernel, splits=tuple(splits)),
        grid=(bsz, lt // tm),
        in_specs=[
            pl.BlockSpec((1, tm, d), lambda b, i: (b, i, 0)),
            pl.BlockSpec((1, d), lambda b, i: (0, 0)),
            pl.BlockSpec((1, 1, 1, d), _seg_map),
            pl.BlockSpec((1, 1, 1, d), _seg_map),
            pl.BlockSpec((d, p), lambda b, i: (0, 0)),
            pl.BlockSpec((1, p), lambda b, i: (0, 0)),
        ],
        out_specs=[pl.BlockSpec((1, tm, wd), lambda b, i: (b, i, 0)) for wd in widths],
        out_shape=[jax.ShapeDtypeStruct((bsz, lt, wd), F32) for wd in widths],
        compiler_params=_cparams(("parallel", "parallel")),
        name="norm_mod_matmul",
    )(h, g.reshape(1, d), shift, scale, w.astype(BF16), bias.reshape(1, p))


def _ffn_prep_kernel(x_ref, g_ref, sh_ref, sc_ref, rw_ref, rb_ref, f_ref, te_ref, gt_ref):
    a = _norm_mod(x_ref[0], g_ref[...], sh_ref[0, 0], sc_ref[0, 0])
    f_ref[0] = a.astype(f_ref.dtype)
    logits = lax.dot_general(rw_ref[...], a.astype(BF16), NT, preferred_element_type=F32) + rb_ref[...]
    ne = logits.shape[0]
    eidx = lax.broadcasted_iota(jnp.int32, logits.shape, 0)
    work = logits
    vals, idxs = [], []
    for _ in range(TOP_K):
        m = jnp.max(work, axis=0, keepdims=True)
        idx = jnp.min(jnp.where(work == m, eidx, ne), axis=0, keepdims=True)
        vals.append(m)
        idxs.append(idx)
        work = jnp.where(eidx == idx, -jnp.inf, work)
    exps = [jnp.exp(v - vals[0]) for v in vals]
    denom = exps[0]
    for e in exps[1:]:
        denom = denom + e
    for k in range(TOP_K):
        te_ref[0, k:k + 1, :] = idxs[k]
        gt_ref[0, k:k + 1, :] = exps[k] / denom


def _ffn_prep(h, g, shift, scale, router_w, router_b):
    bsz, lt, d = h.shape
    ne = router_w.shape[1]
    tm = ROW_TILE
    f, top_e, gate = pl.pallas_call(
        _ffn_prep_kernel,
        grid=(bsz, lt // tm),
        in_specs=[
            pl.BlockSpec((1, tm, d), lambda b, i: (b, i, 0)),
            pl.BlockSpec((1, d), lambda b, i: (0, 0)),
            pl.BlockSpec((1, 1, 1, d), _seg_map),
            pl.BlockSpec((1, 1, 1, d), _seg_map),
            pl.BlockSpec((ne, d), lambda b, i: (0, 0)),
            pl.BlockSpec((ne, 1), lambda b, i: (0, 0)),
        ],
        out_specs=[
            pl.BlockSpec((1, tm, d), lambda b, i: (b, i, 0)),
            pl.BlockSpec((1, TOP_K, tm), lambda b, i: (b, 0, i)),
            pl.BlockSpec((1, TOP_K, tm), lambda b, i: (b, 0, i)),
        ],
        out_shape=[
            jax.ShapeDtypeStruct((bsz, lt, d), BF16),
            jax.ShapeDtypeStruct((bsz, TOP_K, lt), jnp.int32),
            jax.ShapeDtypeStruct((bsz, TOP_K, lt), F32),
        ],
        compiler_params=_cparams(("parallel", "parallel")),
        name="ffn_prep",
    )(h, g.reshape(1, d), shift, scale, router_w.T.astype(BF16), router_b.reshape(ne, 1))
    return f, top_e.transpose(0, 2, 1), gate.transpose(0, 2, 1)


GU_BLOCK = 256


def _moe_kernel(be_ref, nb_ref, first_ref, slot_ref, nxt_ref, x_ref, wgu_hbm, wd_hbm, bg_ref, bu_ref, bd_ref,
                o_ref, wgu_buf, wd_buf, wgu_s, wd_s, sem, *, layer):
    i = pl.program_id(0)
    active = i < nb_ref[0]
    half = GU_BLOCK // 2
    nblk = wgu_s.shape[1] // GU_BLOCK

    def weight_copies(e, slot):
        return (pltpu.make_async_copy(wgu_hbm.at[layer, e], wgu_buf.at[slot], sem.at[0, slot]),
                pltpu.make_async_copy(wd_hbm.at[layer, e], wd_buf.at[slot], sem.at[1, slot]))

    @pl.when(active & (i == 0))
    def _():
        for cp in weight_copies(be_ref[0], 0):
            cp.start()

    @pl.when(active & (first_ref[i] == 1))
    def _():
        slot = slot_ref[i]
        for cp in weight_copies(be_ref[i], slot):
            cp.wait()

        @pl.when(nxt_ref[i] >= 0)
        def _():
            for cp in weight_copies(nxt_ref[i], 1 - slot):
                cp.start()

        r = lax.broadcasted_iota(jnp.int32, (GU_BLOCK, GU_BLOCK), 0)
        c = lax.broadcasted_iota(jnp.int32, (GU_BLOCK, GU_BLOCK), 1)
        perm = (r == jnp.where(c < half, 2 * c, 2 * (c - half) + 1)).astype(BF16)
        for k in range(nblk):
            cs = slice(k * GU_BLOCK, (k + 1) * GU_BLOCK)
            wgu_s[:, cs] = jnp.dot(wgu_buf[slot, :, cs].astype(BF16), perm,
                                   preferred_element_type=F32).astype(BF16)
        wd_s[...] = wd_buf[slot].astype(BF16)

    @pl.when(active)
    def _():
        gu = jnp.dot(x_ref[...], wgu_s[...], preferred_element_type=F32)
        hdn = []
        for k in range(nblk):
            hs = slice(k * half, (k + 1) * half)
            g = gu[:, k * GU_BLOCK:k * GU_BLOCK + half] + bg_ref[0, :, hs]
            u = gu[:, k * GU_BLOCK + half:(k + 1) * GU_BLOCK] + bu_ref[0, :, hs]
            g = jnp.minimum(g, SWIGLU_LIMIT)
            u = jnp.clip(u, -SWIGLU_LIMIT, SWIGLU_LIMIT)
            hdn.append(((u + 1.0) * (g * jax.nn.sigmoid(SWIGLU_ALPHA * g))).astype(BF16))
        hdn = jnp.concatenate(hdn, axis=-1)
        o_ref[...] = (jnp.dot(hdn, wd_s[...], preferred_element_type=F32) + bd_ref[0]).astype(o_ref.dtype)

    @pl.when(jnp.logical_not(active))
    def _():
        o_ref[...] = jnp.zeros_like(o_ref)


def _moe_experts(x_sorted, block_expert, n_used, layer, w_gu, w_down, bg, bu, bd):
    n_rows, d = x_sorted.shape
    _, ne, _, f2 = w_gu.shape
    f = f2 // 2
    tm = MOE_TILE
    n_blocks = n_rows // tm
    assert f2 % GU_BLOCK == 0
    blk = jnp.arange(n_blocks, dtype=jnp.int32)
    prev = jnp.concatenate([block_expert[:1], block_expert[:-1]])
    first = (blk < n_used[0]) & ((blk == 0) | (block_expert != prev))
    slot = (jnp.cumsum(first.astype(jnp.int32)) - 1) & 1
    first_idx = jnp.where(first, blk, n_blocks)
    next_first = lax.cummin(first_idx, axis=0, reverse=True)
    next_first = jnp.concatenate([next_first[1:], jnp.full((1,), n_blocks, jnp.int32)])
    nxt = jnp.where(next_first < n_blocks, block_expert[jnp.minimum(next_first, n_blocks - 1)], -1)
    bmap = lambda i, be, nb, fi, sl, nx: (be[i], 0, 0)
    rmap = lambda i, be, nb, fi, sl, nx: (i, 0)
    grid_spec = pltpu.PrefetchScalarGridSpec(
        num_scalar_prefetch=5,
        grid=(n_blocks,),
        in_specs=[
            pl.BlockSpec((tm, d), rmap),
            pl.BlockSpec(memory_space=pl.ANY),
            pl.BlockSpec(memory_space=pl.ANY),
            pl.BlockSpec((1, 1, f), bmap),
            pl.BlockSpec((1, 1, f), bmap),
            pl.BlockSpec((1, 1, d), bmap),
        ],
        out_specs=pl.BlockSpec((tm, d), rmap),
        scratch_shapes=[pltpu.VMEM((2, d, f2), F32), pltpu.VMEM((2, f, d), F32),
                        pltpu.VMEM((d, f2), BF16), pltpu.VMEM((f, d), BF16),
                        pltpu.SemaphoreType.DMA((2, 2))],
    )
    return pl.pallas_call(
        functools.partial(_moe_kernel, layer=layer),
        grid_spec=grid_spec,
        out_shape=jax.ShapeDtypeStruct((n_rows, d), BF16),
        compiler_params=_cparams(("arbitrary",)),
        name="moe_experts",
    )(block_expert, n_used, first.astype(jnp.int32), slot.astype(jnp.int32), nxt.astype(jnp.int32),
      x_sorted, w_gu, w_down, bg.reshape(ne, 1, f), bu.reshape(ne, 1, f), bd.reshape(ne, 1, d))


def _combine_kernel(y_ref, gt_ref, h_ref, g2_ref, o_ref):
    gt = gt_ref[0]
    acc = y_ref[0, 0].astype(F32) * gt[:, 0:1]
    for k in range(1, TOP_K):
        acc = acc + y_ref[k, 0].astype(F32) * gt[:, k:k + 1]
    o_ref[0] = h_ref[0] + g2_ref[0, 0] * acc


def _moe_combine(yg, gate, h, g2, seg_map, tile_off):
    k, bsz, lt, d = yg.shape
    tm = ROW_TILE
    return pl.pallas_call(
        _combine_kernel,
        grid=(bsz, lt // tm),
        in_specs=[
            pl.BlockSpec((k, 1, tm, d), lambda b, i: (0, b, i, 0)),
            pl.BlockSpec((1, tm, k), lambda b, i: (b, i, 0)),
            pl.BlockSpec((1, tm, d), lambda b, i: (b, i + tile_off, 0)),
            pl.BlockSpec((1, 1, 1, d), seg_map),
        ],
        out_specs=pl.BlockSpec((1, tm, d), lambda b, i: (b, i, 0)),
        out_shape=jax.ShapeDtypeStruct((bsz, lt, d), F32),
        compiler_params=_cparams(("parallel", "parallel")),
        name="moe_combine",
    )(yg, gate, h, g2)


def _lat_seg_map(b, i):
    return (b, 1, 0, 0)


def _moe_layer(h, f, top_e, gate, g2, weights, lat_only, n_ctx):
    layer, w_gu, w_down, bg, bu, bd = weights
    bsz, lt, d = h.shape
    skip = n_ctx if lat_only else 0
    ltok = lt - skip
    top_e, gate = top_e[:, skip:], gate[:, skip:]
    n = bsz * ltok
    n_assign = n * TOP_K
    tm = MOE_TILE
    flat_e = top_e.reshape(-1).astype(jnp.int32)
    iota = jnp.arange(n_assign, dtype=jnp.int32)
    e_sorted, order = lax.sort_key_val(flat_e, iota, is_stable=True)
    experts = jnp.arange(N_EXPERTS, dtype=jnp.int32)
    start = jnp.searchsorted(e_sorted, experts, side='left', method='compare_all').astype(jnp.int32)
    counts = jnp.searchsorted(e_sorted, experts, side='right', method='compare_all').astype(jnp.int32) - start
    padded = (counts + tm - 1) // tm * tm
    pad_end = jnp.cumsum(padded)
    pad_start = pad_end - padded
    dest = pad_start[e_sorted] + iota - start[e_sorted]
    n_blocks = -(-(n_assign + N_EXPERTS * (tm - 1)) // tm)
    n_rows = n_blocks * tm
    block_expert = jnp.minimum(
        jnp.searchsorted(pad_end, jnp.arange(n_blocks, dtype=jnp.int32) * tm, side='right', method='compare_all'),
        N_EXPERTS - 1).astype(jnp.int32)
    n_used = (pad_end[-1] // tm).astype(jnp.int32).reshape(1)
    row = jnp.arange(n_rows, dtype=jnp.int32).reshape(n_blocks, tm)
    blk_shift = (start - pad_start)[block_expert][:, None]
    blk_end = (pad_start + counts)[block_expert][:, None]
    slot = jnp.clip(row + blk_shift, 0, n_assign - 1).reshape(-1)
    row_token = jnp.where((row < blk_end).reshape(-1),
                          order.at[slot].get(mode='promise_in_bounds') // TOP_K, row.reshape(-1) % n)
    row_src = row_token + skip * (row_token // ltok + 1)
    _, pos = lax.sort_key_val(order, dest, is_stable=True)
    x_sorted = f.reshape(bsz * lt, d).at[row_src].get(mode='promise_in_bounds')
    y = _moe_experts(x_sorted, block_expert, n_used, layer, w_gu, w_down, bg, bu, bd)
    pos_k = pos.reshape(n, TOP_K).T
    yg = y.at[pos_k.reshape(-1)].get(mode='promise_in_bounds').reshape(TOP_K, bsz, ltok, d)
    return _moe_combine(yg, gate, h, g2, _lat_seg_map if lat_only else _seg_map, skip // ROW_TILE)


def _rmsnorm_kernel(x_ref, g_ref, o_ref):
    x = x_ref[0]
    ms = jnp.mean(x * x, axis=-1, keepdims=True)
    o_ref[0] = x * lax.rsqrt(ms + EPS) * g_ref[...]


def _final_norm(h, g):
    bsz, lt, d = h.shape
    tm = ROW_TILE
    return pl.pallas_call(
        _rmsnorm_kernel,
        grid=(bsz, lt // tm),
        in_specs=[pl.BlockSpec((1, tm, d), lambda b, i: (b, i, 0)),
                  pl.BlockSpec((1, d), lambda b, i: (0, 0))],
        out_specs=pl.BlockSpec((1, tm, d), lambda b, i: (b, i, 0)),
        out_shape=jax.ShapeDtypeStruct((bsz, lt, d), F32),
        compiler_params=_cparams(("parallel", "parallel")),
        name="final_norm",
    )(h, g.reshape(1, d))


def _grid_reorder_kernel(x_ref, o_ref, *, n_ctx, rows, to_cols):
    o_ref[0, :n_ctx, :] = x_ref[0, :n_ctx, :]
    for c in range(GRID_W):
        raster = pl.ds(n_ctx + c, rows, stride=GRID_W)
        dense = pl.ds(n_ctx + c * rows, rows)
        if to_cols:
            o_ref.at[0][dense, :] = x_ref.at[0][raster, :]
        else:
            o_ref.at[0][raster, :] = x_ref.at[0][dense, :]


def _grid_reorder(h, n_ctx, to_cols):
    bsz, lt, d = h.shape
    lanes = 128
    spec = pl.BlockSpec((1, lt, lanes), lambda b, j: (b, 0, j))
    return pl.pallas_call(
        functools.partial(_grid_reorder_kernel, n_ctx=n_ctx, rows=(lt - n_ctx) // GRID_W, to_cols=to_cols),
        grid=(bsz, d // lanes),
        in_specs=[spec],
        out_specs=spec,
        out_shape=jax.ShapeDtypeStruct(h.shape, h.dtype),
        compiler_params=_cparams(("parallel", "parallel")),
        name="grid_reorder",
    )(h)


def _chunk_order(d, c, n_ctx_chunks, n_chunks):
    bwd = jnp.where(c < n_ctx_chunks, n_ctx_chunks - 1 - c, n_chunks + n_ctx_chunks - 1 - c)
    return jnp.where(d == 0, c, bwd)


def _dir_tri(d, t):
    row = lax.broadcasted_iota(jnp.int32, (t, t), 0)
    col = lax.broadcasted_iota(jnp.int32, (t, t), 1)
    return jnp.where(d == 0, col - row, row - col) <= 0


def _gla_kernel(q_ref, k_ref, v_ref, r_ref, w2_ref, b2_ref, o_ref, st_ref, *, t, heads, scale):
    d = pl.program_id(0)
    c = pl.program_id(2)

    @pl.when(c == 0)
    def _():
        st_ref[...] = jnp.zeros_like(st_ref)

    dk = q_ref.shape[-1] // heads
    dv = v_ref.shape[-1] // heads
    nb = q_ref.shape[0]
    mask = _dir_tri(d, t)
    tri = mask.astype(F32)
    mid = t // 2
    items = [(bb, h) for bb in range(nb) for h in range(heads)]
    xs = [jnp.dot(r_ref[bb], w2_ref[0], preferred_element_type=F32, precision=HI) + b2_ref[0] for bb in range(nb)]
    las = [jax.nn.log_sigmoid(x) * (1.0 / GLA_TAU) for x in xs]
    bs = [jnp.dot(tri, la, preferred_element_type=F32, precision=HI) for la in las]
    qt, kt, qe, kh_end, e_end = [], [], [], [], []
    for bb in range(nb):
        b = bs[bb]
        b_m = b[mid:mid + 1, :]
        b_end = jnp.where(d == 0, b[t - 1:t, :], b[0:1, :])
        q_s = q_ref[bb] * (jnp.exp(b - b_m) * scale)
        k_s = k_ref[bb] * jnp.exp(b_m - b)
        qe.append((q_s * jnp.exp(b_m)).astype(BF16))
        kh_end.append((k_s * jnp.exp(b_end - b_m)).astype(BF16))
        e_end.append(jnp.exp(b_end))
        qt.append(q_s.astype(BF16))
        kt.append(k_s.astype(BF16))
    att, q_st, vs = {}, {}, {}
    for bb, h in items:
        ks = slice(h * dk, (h + 1) * dk)
        vs[bb, h] = v_ref[bb, :, h * dv:(h + 1) * dv].astype(BF16)
        att[bb, h] = lax.dot_general(qt[bb][:, ks], kt[bb][:, ks], NT, preferred_element_type=F32)
        q_st[bb, h] = lax.dot_general(qe[bb][:, ks], st_ref[bb * heads + h].astype(BF16), NT,
                                      preferred_element_type=F32)
    for bb, h in items:
        a = jnp.where(mask, att[bb, h], 0.0).astype(BF16)
        o_ref[0, bb, :, h * dv:(h + 1) * dv] = jnp.dot(a, vs[bb, h], preferred_element_type=F32) + q_st[bb, h]
    for bb, h in items:
        ks = slice(h * dk, (h + 1) * dk)
        upd = lax.dot_general(vs[bb, h], kh_end[bb][:, ks], TN, preferred_element_type=F32)
        st_ref[bb * heads + h] = st_ref[bb * heads + h] * e_end[bb][:, ks] + upd


def _gla_mixer(q, k, v, r, w2, b2, n_ctx):
    bsz, lt, dkt = q.shape
    dvt = v.shape[-1]
    nr = r.shape[-1]
    t, heads = MIX_CHUNK, GLA_HEADS
    bt = MIX_BATCH
    nch = lt // t
    ncc = n_ctx // t
    dk = dkt // heads
    dv = dvt // heads
    imap = lambda d, b, c: (b, _chunk_order(d, c, ncc, nch), 0)
    return pl.pallas_call(
        functools.partial(_gla_kernel, t=t, heads=heads, scale=dk ** -0.5),
        grid=(2, bsz // bt, nch),
        in_specs=[
            pl.BlockSpec((bt, t, dkt), imap),
            pl.BlockSpec((bt, t, dkt), imap),
            pl.BlockSpec((bt, t, dvt), imap),
            pl.BlockSpec((bt, t, nr), imap),
            pl.BlockSpec((1, nr, dkt), lambda d, b, c: (d, 0, 0)),
            pl.BlockSpec((1, 1, dkt), lambda d, b, c: (d, 0, 0)),
        ],
        out_specs=pl.BlockSpec((1, bt, t, dvt), lambda d, b, c: (d, b, _chunk_order(d, c, ncc, nch), 0)),
        out_shape=jax.ShapeDtypeStruct((2, bsz, lt, dvt), F32),
        scratch_shapes=[pltpu.VMEM((bt * heads, dv, dk), F32)],
        compiler_params=_cparams(("parallel", "parallel", "arbitrary")),
        name="gla_mixer",
    )(q, k, v, r, w2, b2)


def _mlstm_kernel(q_ref, k_ref, v_ref, gc_ref, gr_ref, o_ref, c_ref, n_ref, m_ref, *, t, heads):
    d = pl.program_id(0)
    c = pl.program_id(2)

    @pl.when(c == 0)
    def _():
        c_ref[...] = jnp.zeros_like(c_ref)
        n_ref[...] = jnp.zeros_like(n_ref)
        m_ref[...] = jnp.zeros_like(m_ref)

    dh = q_ref.shape[-1] // heads
    nb = q_ref.shape[0]
    mask = _dir_tri(d, t)
    tri = mask.astype(F32)
    items = [(bb, h) for bb in range(nb) for h in range(heads)]
    gate = []
    for bb in range(nb):
        gc = gc_ref[0, bb]
        gr = gr_ref[0, bb, 0]
        fc = jax.nn.log_sigmoid(gc[:, heads:])
        fr = jax.nn.log_sigmoid(gr[heads:, :])
        b_col = jnp.dot(tri, fc, preferred_element_type=F32, precision=HI)
        b_row = lax.dot_general(fr, tri, NT, preferred_element_type=F32, precision=HI)
        b_last = jnp.where(d == 0, b_col[t - 1:t, :], b_col[0:1, :])
        gate.append((gc[:, :heads], gr[:heads, :], b_col, b_row, b_last))
    qs, ks, vs, s_raw, q_c = {}, {}, {}, {}, {}
    for bb, h in items:
        hs = slice(h * dh, (h + 1) * dh)
        qs[bb, h] = q_ref[bb, :, hs]
        ks[bb, h] = k_ref[bb, :, hs]
        vs[bb, h] = v_ref[bb, :, hs].astype(BF16)
        s_raw[bb, h] = lax.dot_general(qs[bb, h], ks[bb, h], NT, preferred_element_type=F32)
        q_c[bb, h] = jnp.dot(qs[bb, h], c_ref[bb * heads + h].astype(BF16), preferred_element_type=F32)
    logw, log_inter, m_t, w_inter, scores, den, qn = {}, {}, {}, {}, {}, {}, {}
    for bb, h in items:
        _, ir, b_col, b_row, _ = gate[bb]
        bc = b_col[:, h:h + 1]
        logw[bb, h] = jnp.where(mask, bc - b_row[h:h + 1, :] + ir[h:h + 1, :], -jnp.inf)
        log_inter[bb, h] = bc + m_ref[bb * heads + h]
    for bb, h in items:
        m_t[bb, h] = jnp.maximum(log_inter[bb, h], jnp.max(logw[bb, h], axis=-1, keepdims=True))
        qn[bb, h] = jnp.sum(qs[bb, h].astype(F32) * n_ref[bb * heads + h], axis=-1, keepdims=True)
    for bb, h in items:
        w_inter[bb, h] = jnp.exp(log_inter[bb, h] - m_t[bb, h])
        scores[bb, h] = s_raw[bb, h] * jnp.exp(logw[bb, h] - m_t[bb, h])
    for bb, h in items:
        den[bb, h] = jnp.sum(scores[bb, h], axis=-1, keepdims=True) + w_inter[bb, h] * qn[bb, h]
    num = {}
    for bb, h in items:
        num[bb, h] = (jnp.dot(scores[bb, h].astype(BF16), vs[bb, h], preferred_element_type=F32)
                      + w_inter[bb, h] * q_c[bb, h])
    for bb, h in items:
        hs = slice(h * dh, (h + 1) * dh)
        o_ref[0, bb, :, hs] = num[bb, h] / jnp.maximum(jnp.abs(den[bb, h]), jnp.exp(-m_t[bb, h]))
    log_g, m_new, kw, upd, ksum = {}, {}, {}, {}, {}
    for bb, h in items:
        ic, _, b_col, _, b_last = gate[bb]
        log_g[bb, h] = b_last[:, h:h + 1] - b_col[:, h:h + 1] + ic[:, h:h + 1]
    for bb, h in items:
        b_last = gate[bb][4]
        m_new[bb, h] = jnp.maximum(b_last[:, h:h + 1] + m_ref[bb * heads + h],
                                   jnp.max(log_g[bb, h], axis=0, keepdims=True))
    for bb, h in items:
        kw[bb, h] = ks[bb, h].astype(F32) * jnp.exp(log_g[bb, h] - m_new[bb, h])
    for bb, h in items:
        upd[bb, h] = lax.dot_general(kw[bb, h].astype(BF16), vs[bb, h], TN, preferred_element_type=F32)
        ksum[bb, h] = jnp.sum(kw[bb, h], axis=0, keepdims=True)
    for bb, h in items:
        si = bb * heads + h
        b_last = gate[bb][4]
        keep = jnp.exp(b_last[:, h:h + 1] + m_ref[si] - m_new[bb, h])
        c_ref[si] = keep * c_ref[si] + upd[bb, h]
        n_ref[si] = keep * n_ref[si] + ksum[bb, h]
        m_ref[si] = m_new[bb, h]


def _mlstm_mixer(qk, v, gates, n_ctx):
    bsz, lt, w2 = qk.shape
    w = w2 // 2
    t, heads = MLSTM_CHUNK, MLSTM_HEADS
    dh = w // heads
    nch = lt // t
    ncc = n_ctx // t
    gc = gates.reshape(bsz, lt, 2, 2 * heads).transpose(2, 0, 1, 3)
    gr = gc.reshape(2, bsz, nch, t, 2 * heads).transpose(0, 1, 2, 4, 3)
    cmap = lambda d, b, c: _chunk_order(d, c, ncc, nch)
    bt = MIX_BATCH
    return pl.pallas_call(
        functools.partial(_mlstm_kernel, t=t, heads=heads),
        grid=(2, bsz // bt, nch),
        in_specs=[
            pl.BlockSpec((bt, t, w), lambda d, b, c: (b, cmap(d, b, c), 0)),
            pl.BlockSpec((bt, t, w), lambda d, b, c: (b, cmap(d, b, c), 1)),
            pl.BlockSpec((bt, t, w), lambda d, b, c: (b, cmap(d, b, c), 0)),
            pl.BlockSpec((1, bt, t, 2 * heads), lambda d, b, c: (d, b, cmap(d, b, c), 0)),
            pl.BlockSpec((1, bt, 1, 2 * heads, t), lambda d, b, c: (d, b, cmap(d, b, c), 0, 0)),
        ],
        out_specs=pl.BlockSpec((1, bt, t, w), lambda d, b, c: (d, b, cmap(d, b, c), 0)),
        out_shape=jax.ShapeDtypeStruct((2, bsz, lt, w), F32),
        scratch_shapes=[pltpu.VMEM((bt * heads, dh, dh), F32), pltpu.VMEM((bt * heads, 1, dh), F32),
                        pltpu.VMEM((bt * heads, 1, 1), F32)],
        compiler_params=_cparams(("parallel", "parallel", "arbitrary")),
        name="mlstm_mixer",
    )(qk, qk, v, gc, gr)


def _conv_kernel(x_ref, w_ref, b_ref, s_ref, o_ref, *, n_ctx):
    x = x_ref[0]
    lt = x.shape[0]
    row = lax.broadcasted_iota(jnp.int32, x.shape, 0)
    prev = jnp.where((row == 0) | (row == n_ctx), 0.0, pltpu.roll(x, 1, 0))
    nxt = jnp.where((row == n_ctx - 1) | (row == lt - 1), 0.0, pltpu.roll(x, lt - 1, 0))
    y = b_ref[...] + w_ref[0:1, :] * prev + w_ref[1:2, :] * x + w_ref[2:3, :] * nxt
    o_ref[0] = (y * jax.nn.sigmoid(y) * s_ref[...]).astype(o_ref.dtype)


def _conv_silu(x, w, b, colscale, n_ctx):
    bsz, lt, ch = x.shape
    tc = 256
    return pl.pallas_call(
        functools.partial(_conv_kernel, n_ctx=n_ctx),
        grid=(bsz, ch // tc),
        in_specs=[
            pl.BlockSpec((1, lt, tc), lambda b, j: (b, 0, j)),
            pl.BlockSpec((3, tc), lambda b, j: (0, j)),
            pl.BlockSpec((1, tc), lambda b, j: (0, j)),
            pl.BlockSpec((1, tc), lambda b, j: (0, j)),
        ],
        out_specs=pl.BlockSpec((1, lt, tc), lambda b, j: (b, 0, j)),
        out_shape=jax.ShapeDtypeStruct((bsz, lt, ch), BF16),
        compiler_params=_cparams(("parallel", "parallel")),
        name="conv_silu",
    )(x, w, b.reshape(1, ch), colscale.reshape(1, ch))


def _s5_matrices(a_re, a_im, log_dt, b_re, b_im, c_re, c_im, backward, lane_groups=8):
    g, p = a_re.shape
    cg = b_re.shape[-1]
    j = S5_J
    lg = lane_groups
    nq = g // lg
    dt = jnp.exp(log_dt)[:, None]
    lam_re = jnp.minimum(a_re, -1e-4)
    lam_im = a_im
    decay = jnp.exp(lam_re * dt)
    ab_re = decay * jnp.cos(lam_im * dt)
    ab_im = decay * jnp.sin(lam_im * dt)
    den = lam_re * lam_re + lam_im * lam_im
    zr = ((ab_re - 1) * lam_re + ab_im * lam_im) / den
    zi = (ab_im * lam_re - (ab_re - 1) * lam_im) / den
    bb_re = zr[..., None] * b_re - zi[..., None] * b_im
    bb_im = zr[..., None] * b_im + zi[..., None] * b_re
    pw_re, pw_im = [jnp.ones_like(ab_re)], [jnp.zeros_like(ab_im)]
    for _ in range(j):
        r0, i0 = pw_re[-1], pw_im[-1]
        pw_re.append(ab_re * r0 - ab_im * i0)
        pw_im.append(ab_re * i0 + ab_im * r0)
    pw_re, pw_im = jnp.stack(pw_re), jnp.stack(pw_im)
    ca_re = c_re[None] * pw_re[:, :, None, :] - c_im[None] * pw_im[:, :, None, :]
    ca_im = c_re[None] * pw_im[:, :, None, :] + c_im[None] * pw_re[:, :, None, :]
    kk = (jnp.einsum('tgcp,gpd->tgcd', ca_re[:j], bb_re, precision=HI)
          - jnp.einsum('tgcp,gpd->tgcd', ca_im[:j], bb_im, precision=HI))
    ab_pw_re = pw_re[:j, :, :, None] * bb_re[None] - pw_im[:j, :, :, None] * bb_im[None]
    ab_pw_im = pw_re[:j, :, :, None] * bb_im[None] + pw_im[:j, :, :, None] * bb_re[None]
    eye = jnp.eye(lg, dtype=BF16)
    lb = lg * cg
    sw = lg * p
    bd_k = jnp.einsum('tqgcd,gh->tqgdhc', kk.astype(BF16).reshape(j, nq, lg, cg, cg), eye
                      ).reshape(j, nq, lb, lb)
    bd_in = [jnp.einsum('tqgpc,gh->tqgchp', a.astype(BF16).reshape(j, nq, lg, p, cg), eye
                        ).reshape(j, nq, lb, sw) for a in (ab_pw_re, ab_pw_im)]
    bd_out = [jnp.einsum('tqgcp,gh->tqgphc', a.astype(BF16).reshape(j + 1, nq, lg, cg, p), eye
                         ).reshape(j + 1, nq, sw, lb) for a in (ca_re, -ca_im)]
    jj = jnp.arange(j)
    lag = (jj[:, None] - jj[None, :]) if backward else (jj[None, :] - jj[:, None])
    kt = jnp.where((lag >= 0)[:, :, None, None, None], bd_k[jnp.clip(lag, 0, j - 1)], 0)
    ktoep = kt.transpose(2, 0, 3, 1, 4).reshape(nq, j * lb, j * lb)
    tau_in = jj if backward else (j - 1 - jj)
    win_re, win_im = (a[tau_in].transpose(1, 0, 2, 3).reshape(nq, j * lb, sw) for a in bd_in)
    tau_out = (j - jj) if backward else (jj + 1)
    wout_re, wout_im = (a[tau_out].transpose(1, 2, 0, 3).reshape(nq, sw, j * lb) for a in bd_out)
    dec_re = pw_re[j].reshape(nq, 1, sw)
    dec_im = pw_im[j].reshape(nq, 1, sw)
    return ktoep, win_re, win_im, wout_re, wout_im, dec_re, dec_im


def _s5_kernel(u_ref, kt_ref, wir_ref, wii_ref, wor_ref, woi_ref, dr_ref, di_ref, y_ref,
               xf_ref, yf_ref, sre_ref, sim_ref, *, bt, nk, nk_ctx, rs):
    d = pl.program_id(0)
    j = S5_J
    lanes = u_ref.shape[-1]
    for b in range(bt):
        for jj in range(j):
            xf_ref[b * nk:(b + 1) * nk, jj * lanes:(jj + 1) * lanes] = (
                u_ref.at[b][pl.ds(jj, nk, stride=j), :].astype(BF16))
    xf = xf_ref[...]
    yf_ref[...] = jnp.dot(xf, kt_ref[0, 0], preferred_element_type=F32)
    inc_re = jnp.dot(xf, wir_ref[0, 0], preferred_element_type=F32)
    inc_im = jnp.dot(xf, wii_ref[0, 0], preferred_element_type=F32)
    nl = sre_ref.shape[0]
    for b in range(bt):
        for l in range(nl):
            sre_ref[l, b * rs:b * rs + nk, :] = inc_re[b * nk:(b + 1) * nk, l * lanes:(l + 1) * lanes]
            sim_ref[l, b * rs:b * rs + nk, :] = inc_im[b * nk:(b + 1) * nk, l * lanes:(l + 1) * lanes]
    a_re = [dr_ref[0, 0, :, l * lanes:(l + 1) * lanes] for l in range(nl)]
    a_im = [di_ref[0, 0, :, l * lanes:(l + 1) * lanes] for l in range(nl)]

    def step(kidx, carry):
        rows = pl.ds(kidx, bt, stride=rs)
        out = []
        for l in range(nl):
            s_re, s_im = carry[2 * l], carry[2 * l + 1]
            i_re = sre_ref.at[l][rows, :]
            i_im = sim_ref.at[l][rows, :]
            sre_ref.at[l][rows, :] = s_re
            sim_ref.at[l][rows, :] = s_im
            out.append(a_re[l] * s_re - a_im[l] * s_im + i_re)
            out.append(a_re[l] * s_im + a_im[l] * s_re + i_im)
        return tuple(out)

    zero = tuple(jnp.zeros((bt, lanes), F32) for _ in range(2 * nl))

    @pl.when(d == 0)
    def _():
        lax.fori_loop(0, nk, step, zero)

    @pl.when(d == 1)
    def _():
        carry = lax.fori_loop(0, nk_ctx, lambda i, cr: step(nk_ctx - 1 - i, cr), zero)
        lax.fori_loop(0, nk - nk_ctx, lambda i, cr: step(nk - 1 - i, cr), carry)

    for b in range(bt):
        sp_re = jnp.concatenate([sre_ref[l, b * rs:b * rs + nk, :] for l in range(nl)], axis=-1).astype(BF16)
        sp_im = jnp.concatenate([sim_ref[l, b * rs:b * rs + nk, :] for l in range(nl)], axis=-1).astype(BF16)
        yb = (yf_ref[b * nk:(b + 1) * nk, :]
              + jnp.dot(sp_re, wor_ref[0, 0], preferred_element_type=F32)
              + jnp.dot(sp_im, woi_ref[0, 0], preferred_element_type=F32))
        for jj in range(j):
            y_ref.at[0, b][pl.ds(jj, nk, stride=j), :] = yb[:, jj * lanes:(jj + 1) * lanes]


def _s5_mixer(u, mats, n_ctx):
    bsz, lt, w = u.shape
    ktoep, win_re, win_im, wout_re, wout_im, dec_re, dec_im = mats
    lanes = 128
    bt = 4 if bsz % 4 == 0 else 2
    nq = w // lanes
    j = S5_J
    nk = lt // j
    nk_ctx = n_ctx // j
    rs = nk + 8
    fl = j * lanes
    sw = win_re.shape[-1]
    wmap = lambda d, q, b: (d, q, 0, 0)
    return pl.pallas_call(
        functools.partial(_s5_kernel, bt=bt, nk=nk, nk_ctx=nk_ctx, rs=rs),
        grid=(2, nq, bsz // bt),
        in_specs=[
            pl.BlockSpec((bt, lt, lanes), lambda d, q, b: (b, 0, q)),
            pl.BlockSpec((1, 1, fl, fl), wmap),
            pl.BlockSpec((1, 1, fl, sw), wmap),
            pl.BlockSpec((1, 1, fl, sw), wmap),
            pl.BlockSpec((1, 1, sw, fl), wmap),
            pl.BlockSpec((1, 1, sw, fl), wmap),
            pl.BlockSpec((1, 1, 1, sw), wmap),
            pl.BlockSpec((1, 1, 1, sw), wmap),
        ],
        out_specs=pl.BlockSpec((1, bt, lt, lanes), lambda d, q, b: (d, b, 0, q)),
        out_shape=jax.ShapeDtypeStruct((2, bsz, lt, w), F32),
        scratch_shapes=[pltpu.VMEM((bt * nk, fl), BF16), pltpu.VMEM((bt * nk, fl), F32),
                        pltpu.VMEM((sw // lanes, bt * rs, lanes), F32),
                        pltpu.VMEM((sw // lanes, bt * rs, lanes), F32)],
        compiler_params=_cparams(("parallel", "parallel", "arbitrary")),
        name="s5_mixer",
    )(u, ktoep, win_re, win_im, wout_re, wout_im, dec_re, dec_im)


def _head_norm(x, heads):
    dh = x.shape[-1] // heads
    outs = []
    for h in range(heads):
        xh = x[:, h * dh:(h + 1) * dh]
        outs.append(xh * lax.rsqrt(jnp.mean(xh * xh, axis=-1, keepdims=True) + EPS))
    return jnp.concatenate(outs, axis=-1)


def _even_post_kernel(m_ref, o_ref, s_ref, u_ref, mg_ref, dsk_ref, gw_ref, gb_ref, w_ref, h_ref, gate_ref,
                      out_ref, *, heads):
    m = m_ref[0, 0] + m_ref[1, 0]
    m_out = _head_norm(m, heads) * mg_ref[...] * jax.nn.sigmoid(o_ref[0])
    y = jax.nn.gelu(s_ref[0, 0] + s_ref[1, 0] + dsk_ref[...] * u_ref[0])
    glu = jnp.dot(y.astype(BF16), gw_ref[...], preferred_element_type=F32) + gb_ref[...]
    s_out = y * jax.nn.sigmoid(glu)
    cat = jnp.concatenate([m_out, s_out], axis=-1).astype(BF16)
    z = jnp.dot(cat, w_ref[...], preferred_element_type=F32)
    out_ref[0] = h_ref[0] + gate_ref[0, 0] * z


def _even_post(m2, o, s2, u, mnorm_g, d_skip, glu_w, glu_b, w_out, h, gate):
    bsz, lt, d = h.shape
    mw = o.shape[-1]
    sw = u.shape[-1]
    tm = ROW_TILE
    row = lambda b, i: (b, i, 0)
    row2 = lambda b, i: (0, b, i, 0)
    const = lambda b, i: (0, 0)
    return pl.pallas_call(
        functools.partial(_even_post_kernel, heads=MLSTM_HEADS),
        grid=(bsz, lt // tm),
        in_specs=[
            pl.BlockSpec((2, 1, tm, mw), row2),
            pl.BlockSpec((1, tm, mw), row),
            pl.BlockSpec((2, 1, tm, sw), row2),
            pl.BlockSpec((1, tm, sw), row),
            pl.BlockSpec((1, mw), const),
            pl.BlockSpec((1, sw), const),
            pl.BlockSpec((sw, sw), const),
            pl.BlockSpec((1, sw), const),
            pl.BlockSpec((mw + sw, d), const),
            pl.BlockSpec((1, tm, d), row),
            pl.BlockSpec((1, 1, 1, d), _seg_map),
        ],
        out_specs=pl.BlockSpec((1, tm, d), row),
        out_shape=jax.ShapeDtypeStruct((bsz, lt, d), F32),
        compiler_params=_cparams(("parallel", "parallel")),
        name="even_post",
    )(m2, o, s2, u, mnorm_g.reshape(1, mw), d_skip.reshape(1, sw), glu_w.astype(BF16), glu_b.reshape(1, sw),
      w_out.astype(BF16), h, gate)


def _odd_post_kernel(o_ref, g_ref, ng_ref, w_ref, h_ref, gate_ref, out_ref, *, heads):
    g = g_ref[0]
    y = _head_norm(o_ref[0, 0] + o_ref[1, 0], heads) * ng_ref[...] * (g * jax.nn.sigmoid(g))
    z = jnp.dot(y.astype(BF16), w_ref[...], preferred_element_type=F32)
    out_ref[0] = h_ref[0] + gate_ref[0, 0] * z


def _odd_post(o, g, norm_g, w_out, h, gate):
    bsz, lt, d = h.shape
    dv = o.shape[-1]
    tm = ROW_TILE
    row = lambda b, i: (b, i, 0)
    const = lambda b, i: (0, 0)
    return pl.pallas_call(
        functools.partial(_odd_post_kernel, heads=GLA_HEADS),
        grid=(bsz, lt // tm),
        in_specs=[
            pl.BlockSpec((2, 1, tm, dv), lambda b, i: (0, b, i, 0)),
            pl.BlockSpec((1, tm, dv), row),
            pl.BlockSpec((1, dv), const),
            pl.BlockSpec((dv, d), const),
            pl.BlockSpec((1, tm, d), row),
            pl.BlockSpec((1, 1, 1, d), _seg_map),
        ],
        out_specs=pl.BlockSpec((1, tm, d), row),
        out_shape=jax.ShapeDtypeStruct((bsz, lt, d), F32),
        compiler_params=_cparams(("parallel", "parallel")),
        name="odd_post",
    )(o, g, norm_g.reshape(1, dv), w_out.astype(BF16), h, gate)


def kernel(x, c, ctx, c_ctx, mod_w, mod_b, norm_mix_g, norm_ffn_g, ev_w_in, ev_b_in, ev_conv_w, ev_conv_b, ev_mlstm_norm_g, ev_s5_a_re_f, ev_s5_a_im_f, ev_s5_log_dt_f, ev_s5_a_re_b, ev_s5_a_im_b, ev_s5_log_dt_b, ev_s5_b_re, ev_s5_b_im, ev_s5_c_re, ev_s5_c_im, ev_s5_d, ev_s5_glu_w, ev_s5_glu_b, ev_w_out, od_w_in, od_gate_w2_f, od_gate_b2_f, od_gate_w2_b, od_gate_b2_b, od_norm_g, od_w_out, router_w, router_b, moe_w_gu, moe_b_gu, moe_w_down, moe_b_down, final_norm_g):
    bsz, seq, d = x.shape
    n_ctx = ctx.shape[1]
    depth = mod_w.shape[0]
    lt = n_ctx + seq
    assert n_ctx == ROW_TILE and seq % ROW_TILE == 0 and seq % GRID_W == 0

    h = jnp.concatenate([ctx, x], axis=1)
    c_all = jnp.concatenate([c, c_ctx[None, :]], axis=0)
    c_all = jnp.pad(c_all, ((0, (-c_all.shape[0]) % 8), (0, 0)))
    mods = _modulation(c_all, mod_w, mod_b)
    mod_lat = mods[:, :bsz]
    mod_ctx = jnp.broadcast_to(mods[:, bsz:bsz + 1], mod_lat.shape)
    mod6 = jnp.stack([mod_ctx, mod_lat], axis=2).reshape(depth, bsz, 2, 6, 1, d)

    bg_all = moe_b_gu[..., 0::2]
    bu_all = moe_b_gu[..., 1::2]

    mw = ev_conv_w.shape[-1] // 2
    n_gates = 4 * MLSTM_HEADS
    s5w = ev_s5_d.shape[-1]
    dk_t = od_gate_w2_f.shape[-1]
    dv_t = od_norm_g.shape[-1]
    for layer in range(depth):
        last = layer == depth - 1
        j = layer // 2
        m6 = mod6[layer]
        sh1, sc1, g1, sh2, sc2, g2 = (m6[:, :, i] for i in range(6))
        if layer % 2 == 0:
            w_in, b_in = ev_w_in[j], ev_b_in[j]
            cols = jnp.concatenate([jnp.arange(0, 4 * mw), jnp.arange(4 * mw + n_gates, 4 * mw + n_gates + s5w),
                                    jnp.arange(4 * mw, 4 * mw + n_gates)])
            qk_pre, v, o, u, gates = _nm_matmul(h, norm_mix_g[layer], sh1, sc1, w_in[:, cols], b_in[cols],
                                                (2 * mw, mw, mw, s5w, n_gates))
            dh = mw // MLSTM_HEADS
            colscale = jnp.concatenate([jnp.full((mw,), dh ** -0.5, F32), jnp.ones((mw,), F32)])
            qk = _conv_silu(qk_pre, ev_conv_w[j], ev_conv_b[j], colscale, n_ctx)
            m2 = _mlstm_mixer(qk, v, gates, n_ctx)
            shared = (ev_s5_b_re[j], ev_s5_b_im[j], ev_s5_c_re[j], ev_s5_c_im[j])
            mats_f = _s5_matrices(ev_s5_a_re_f[j], ev_s5_a_im_f[j], ev_s5_log_dt_f[j], *shared, backward=False)
            mats_b = _s5_matrices(ev_s5_a_re_b[j], ev_s5_a_im_b[j], ev_s5_log_dt_b[j], *shared, backward=True)
            s2 = _s5_mixer(u, tuple(jnp.stack([a, b]) for a, b in zip(mats_f, mats_b)), n_ctx)
            h = _even_post(m2, o, s2, u, ev_mlstm_norm_g[j], ev_s5_d[j], ev_s5_glu_w[j], ev_s5_glu_b[j],
                           ev_w_out[j], h, g1)
        else:
            hc = _grid_reorder(h, n_ctx, True)
            qq, kk, vv, gg, rr = _nm_matmul(hc, norm_mix_g[layer], sh1, sc1, od_w_in[j],
                                            jnp.zeros((od_w_in.shape[-1],), F32),
                                            (dk_t, dk_t, dv_t, dv_t, 2 * GLA_RANK))
            zero = jnp.zeros_like(od_gate_w2_f[j])
            w2 = jnp.stack([jnp.concatenate([od_gate_w2_f[j], zero], axis=0),
                            jnp.concatenate([zero, od_gate_w2_b[j]], axis=0)])
            b2 = jnp.stack([od_gate_b2_f[j], od_gate_b2_b[j]])[:, None, :]
            o2 = _gla_mixer(qq, kk, vv, rr, w2, b2, n_ctx)
            h = _grid_reorder(_odd_post(o2, gg, od_norm_g[j], od_w_out[j], hc, g1), n_ctx, False)
        f, top_e, gate = _ffn_prep(h, norm_ffn_g[layer], sh2, sc2, router_w[layer], router_b[layer])
        weights = (layer, moe_w_gu, moe_w_down, bg_all[layer], bu_all[layer], moe_b_down[layer])
        h = _moe_layer(h, f, top_e, gate, g2, weights, last, n_ctx)
    return _final_norm(h, final_norm_g)
```

```python
import functools

import jax
import jax.numpy as jnp
from jax import lax
from jax.experimental import pallas as pl
from jax.experimental.pallas import tpu as pltpu

F32 = jnp.float32
BF16 = jnp.bfloat16
HI = lax.Precision.HIGHEST

EPS = 1e-6
GRID_W = 64
MLSTM_HEADS = 4
S5_GROUP = 16
GLA_HEADS = 4
GLA_RANK = 16
GLA_TAU = 16.0
N_EXPERTS = 32
TOP_K = 4
SWIGLU_LIMIT = 7.0
SWIGLU_ALPHA = 1.702

ROW_TILE = 256
MOE_TILE = 512
MIX_CHUNK = 64
MLSTM_CHUNK = 128
MIX_BATCH = 4
S5_J = 8
VMEM_LIMIT = 56 * 1024 * 1024

NT = (((1,), (1,)), ((), ()))
TN = (((0,), (0,)), ((), ()))


def _cparams(sem):
    return pltpu.CompilerParams(dimension_semantics=sem, vmem_limit_bytes=VMEM_LIMIT)


def _mod_kernel(c_ref, w_ref, b_ref, o_ref):
    c = c_ref[...]
    a = c * jax.nn.sigmoid(c)
    o_ref[0] = jnp.dot(a.astype(BF16), w_ref[0].astype(BF16), preferred_element_type=F32) + b_ref[0]


def _modulation(c_all, mod_w, mod_b):
    depth, d, n6 = mod_w.shape
    rows = c_all.shape[0]
    tn = d
    return pl.pallas_call(
        _mod_kernel,
        grid=(depth, n6 // tn),
        in_specs=[
            pl.BlockSpec((rows, d), lambda l, j: (0, 0)),
            pl.BlockSpec((1, d, tn), lambda l, j: (l, 0, j)),
            pl.BlockSpec((1, 1, tn), lambda l, j: (l, 0, j)),
        ],
        out_specs=pl.BlockSpec((1, rows, tn), lambda l, j: (l, 0, j)),
        out_shape=jax.ShapeDtypeStruct((depth, rows, n6), F32),
        compiler_params=_cparams(("arbitrary", "arbitrary")),
        name="modulation",
    )(c_all, mod_w, mod_b.reshape(depth, 1, n6))


def _norm_mod(x, g, sh, sc):
    ms = jnp.mean(x * x, axis=-1, keepdims=True)
    return (x * lax.rsqrt(ms + EPS) * g) * (1.0 + sc) + sh


def _nm_matmul_kernel(x_ref, g_ref, sh_ref, sc_ref, w_ref, b_ref, *out_refs, splits):
    a = _norm_mod(x_ref[0], g_ref[...], sh_ref[0, 0], sc_ref[0, 0])
    z = jnp.dot(a.astype(BF16), w_ref[...], preferred_element_type=F32) + b_ref[...]
    for (lo, hi), o_ref in zip(splits, out_refs):
        o_ref[0] = z[:, lo:hi].astype(o_ref.dtype)


def _seg_map(b, i):
    return (b, jnp.minimum(i, 1), 0, 0)


def _nm_matmul(h, g, shift, scale, w, bias, widths):
    bsz, lt, d = h.shape
    p = w.shape[1]
    splits, lo = [], 0
    for wd in widths:
        splits.append((lo, lo + wd))
        lo += wd
    assert lo == p
    tm = ROW_TILE
    return pl.pallas_call(
        functools.partial(_nm_matmul_kernel, splits=tuple(splits)),
        grid=(bsz, lt // tm),
        in_specs=[
            pl.BlockSpec((1, tm, d), lambda b, i: (b, i, 0)),
            pl.BlockSpec((1, d), lambda b, i: (0, 0)),
            pl.BlockSpec((1, 1, 1, d), _seg_map),
            pl.BlockSpec((1, 1, 1, d), _seg_map),
            pl.BlockSpec((d, p), lambda b, i: (0, 0)),
            pl.BlockSpec((1, p), lambda b, i: (0, 0)),
        ],
        out_specs=[pl.BlockSpec((1, tm, wd), lambda b, i: (b, i, 0)) for wd in widths],
        out_shape=[jax.ShapeDtypeStruct((bsz, lt, wd), F32) for wd in widths],
        compiler_params=_cparams(("parallel", "parallel")),
        name="norm_mod_matmul",
    )(h, g.reshape(1, d), shift, scale, w.astype(BF16), bias.reshape(1, p))


def _ffn_prep_kernel(x_ref, g_ref, sh_ref, sc_ref, rw_ref, rb_ref, f_ref, te_ref, gt_ref):
    a = _norm_mod(x_ref[0], g_ref[...], sh_ref[0, 0], sc_ref[0, 0])
    f_ref[0] = a.astype(f_ref.dtype)
    logits = lax.dot_general(rw_ref[...], a.astype(BF16), NT, preferred_element_type=F32) + rb_ref[...]
    ne = logits.shape[0]
    eidx = lax.broadcasted_iota(jnp.int32, logits.shape, 0)
    work = logits
    vals, idxs = [], []
    for _ in range(TOP_K):
        m = jnp.max(work, axis=0, keepdims=True)
        idx = jnp.min(jnp.where(work == m, eidx, ne), axis=0, keepdims=True)
        vals.append(m)
        idxs.append(idx)
        work = jnp.where(eidx == idx, -jnp.inf, work)
    exps = [jnp.exp(v - vals[0]) for v in vals]
    denom = exps[0]
    for e in exps[1:]:
        denom = denom + e
    for k in range(TOP_K):
        te_ref[0, k:k + 1, :] = idxs[k]
        gt_ref[0, k:k + 1, :] = exps[k] / denom


def _ffn_prep(h, g, shift, scale, router_w, router_b):
    bsz, lt, d = h.shape
    ne = router_w.shape[1]
    tm = ROW_TILE
    return pl.pallas_call(
        _ffn_prep_kernel,
        grid=(bsz, lt // tm),
        in_specs=[
            pl.BlockSpec((1, tm, d), lambda b, i: (b, i, 0)),
            pl.BlockSpec((1, d), lambda b, i: (0, 0)),
            pl.BlockSpec((1, 1, 1, d), _seg_map),
            pl.BlockSpec((1, 1, 1, d), _seg_map),
            pl.BlockSpec((ne, d), lambda b, i: (0, 0)),
            pl.BlockSpec((ne, 1), lambda b, i: (0, 0)),
        ],
        out_specs=[
            pl.BlockSpec((1, tm, d), lambda b, i: (b, i, 0)),
            pl.BlockSpec((1, TOP_K, tm), lambda b, i: (b, 0, i)),
            pl.BlockSpec((1, TOP_K, tm), lambda b, i: (b, 0, i)),
        ],
        out_shape=[
            jax.ShapeDtypeStruct((bsz, lt, d), BF16),
            jax.ShapeDtypeStruct((bsz, TOP_K, lt), jnp.int32),
            jax.ShapeDtypeStruct((bsz, TOP_K, lt), F32),
        ],
        compiler_params=_cparams(("parallel", "parallel")),
        name="ffn_prep",
    )(h, g.reshape(1, d), shift, scale, router_w.T.astype(BF16), router_b.reshape(ne, 1))


GU_BLOCK = 256


def _moe_kernel(be_ref, nb_ref, first_ref, slot_ref, nxt_ref, x_ref, wgu_hbm, wd_hbm, bg_ref, bu_ref, bd_ref,
                o_ref, wgu_buf, wd_buf, wgu_s, wd_s, sem, *, layer):
    i = pl.program_id(0)
    active = i < nb_ref[0]
    half = GU_BLOCK // 2
    nblk = wgu_s.shape[1] // GU_BLOCK

    def weight_copies(e, slot):
        return (pltpu.make_async_copy(wgu_hbm.at[layer, e], wgu_buf.at[slot], sem.at[0, slot]),
                pltpu.make_async_copy(wd_hbm.at[layer, e], wd_buf.at[slot], sem.at[1, slot]))

    @pl.when(active & (i == 0))
    def _():
        for cp in weight_copies(be_ref[0], 0):
            cp.start()

    @pl.when(active & (first_ref[i] == 1))
    def _():
        slot = slot_ref[i]
        for cp in weight_copies(be_ref[i], slot):
            cp.wait()

        @pl.when(nxt_ref[i] >= 0)
        def _():
            for cp in weight_copies(nxt_ref[i], 1 - slot):
                cp.start()

        r = lax.broadcasted_iota(jnp.int32, (GU_BLOCK, GU_BLOCK), 0)
        c = lax.broadcasted_iota(jnp.int32, (GU_BLOCK, GU_BLOCK), 1)
        perm = (r == jnp.where(c < half, 2 * c, 2 * (c - half) + 1)).astype(BF16)
        for k in range(nblk):
            cs = slice(k * GU_BLOCK, (k + 1) * GU_BLOCK)
            wgu_s[:, cs] = jnp.dot(wgu_buf[slot, :, cs].astype(BF16), perm,
                                   preferred_element_type=F32).astype(BF16)
        wd_s[...] = wd_buf[slot].astype(BF16)

    @pl.when(active)
    def _():
        gu = jnp.dot(x_ref[...], wgu_s[...], preferred_element_type=F32)
        hdn = []
        for k in range(nblk):
            hs = slice(k * half, (k + 1) * half)
            g = gu[:, k * GU_BLOCK:k * GU_BLOCK + half] + bg_ref[0, :, hs]
            u = gu[:, k * GU_BLOCK + half:(k + 1) * GU_BLOCK] + bu_ref[0, :, hs]
            g = jnp.minimum(g, SWIGLU_LIMIT)
            u = jnp.clip(u, -SWIGLU_LIMIT, SWIGLU_LIMIT)
            hdn.append(((u + 1.0) * (g * jax.nn.sigmoid(SWIGLU_ALPHA * g))).astype(BF16))
        hdn = jnp.concatenate(hdn, axis=-1)
        o_ref[...] = (jnp.dot(hdn, wd_s[...], preferred_element_type=F32) + bd_ref[0]).astype(o_ref.dtype)

    @pl.when(jnp.logical_not(active))
    def _():
        o_ref[...] = jnp.zeros_like(o_ref)


def _moe_experts(x_sorted, block_expert, n_used, layer, w_gu, w_down, bg, bu, bd):
    n_rows, d = x_sorted.shape
    _, ne, _, f2 = w_gu.shape
    f = f2 // 2
    tm = MOE_TILE
    n_blocks = n_rows // tm
    assert f2 % GU_BLOCK == 0
    blk = jnp.arange(n_blocks, dtype=jnp.int32)
    prev = jnp.concatenate([block_expert[:1], block_expert[:-1]])
    first = (blk < n_used[0]) & ((blk == 0) | (block_expert != prev))
    slot = (jnp.cumsum(first.astype(jnp.int32)) - 1) & 1
    first_idx = jnp.where(first, blk, n_blocks)
    next_first = lax.cummin(first_idx, axis=0, reverse=True)
    next_first = jnp.concatenate([next_first[1:], jnp.full((1,), n_blocks, jnp.int32)])
    nxt = jnp.where(next_first < n_blocks, block_expert[jnp.minimum(next_first, n_blocks - 1)], -1)
    bmap = lambda i, be, nb, fi, sl, nx: (be[i], 0, 0)
    rmap = lambda i, be, nb, fi, sl, nx: (i, 0)
    grid_spec = pltpu.PrefetchScalarGridSpec(
        num_scalar_prefetch=5,
        grid=(n_blocks,),
        in_specs=[
            pl.BlockSpec((tm, d), rmap),
            pl.BlockSpec(memory_space=pl.ANY),
            pl.BlockSpec(memory_space=pl.ANY),
            pl.BlockSpec((1, 1, f), bmap),
            pl.BlockSpec((1, 1, f), bmap),
            pl.BlockSpec((1, 1, d), bmap),
        ],
        out_specs=pl.BlockSpec((tm, d), rmap),
        scratch_shapes=[pltpu.VMEM((2, d, f2), F32), pltpu.VMEM((2, f, d), F32),
                        pltpu.VMEM((d, f2), BF16), pltpu.VMEM((f, d), BF16),
                        pltpu.SemaphoreType.DMA((2, 2))],
    )
    return pl.pallas_call(
        functools.partial(_moe_kernel, layer=layer),
        grid_spec=grid_spec,
        out_shape=jax.ShapeDtypeStruct((n_rows, d), BF16),
        compiler_params=_cparams(("arbitrary",)),
        name="moe_experts",
    )(block_expert, n_used, first.astype(jnp.int32), slot.astype(jnp.int32), nxt.astype(jnp.int32),
      x_sorted, w_gu, w_down, bg.reshape(ne, 1, f), bu.reshape(ne, 1, f), bd.reshape(ne, 1, d))


def _combine_kernel(y_ref, gt_ref, h_ref, g2_ref, o_ref):
    gt = gt_ref[0]
    acc = y_ref[0, 0].astype(F32) * gt[:, 0:1]
    for k in range(1, TOP_K):
        acc = acc + y_ref[k, 0].astype(F32) * gt[:, k:k + 1]
    o_ref[0] = h_ref[0] + g2_ref[0, 0] * acc


def _moe_combine(yg, gate, h, g2, seg_map, tile_off):
    k, bsz, lt, d = yg.shape
    tm = ROW_TILE
    return pl.pallas_call(
        _combine_kernel,
        grid=(bsz, lt // tm),
        in_specs=[
            pl.BlockSpec((k, 1, tm, d), lambda b, i: (0, b, i, 0)),
            pl.BlockSpec((1, tm, k), lambda b, i: (b, i, 0)),
            pl.BlockSpec((1, tm, d), lambda b, i: (b, i + tile_off, 0)),
            pl.BlockSpec((1, 1, 1, d), seg_map),
        ],
        out_specs=pl.BlockSpec((1, tm, d), lambda b, i: (b, i, 0)),
        out_shape=jax.ShapeDtypeStruct((bsz, lt, d), F32),
        compiler_params=_cparams(("parallel", "parallel")),
        name="moe_combine",
    )(yg, gate, h, g2)


def _lat_seg_map(b, i):
    return (b, 1, 0, 0)


def _route_kernel(e_ref, pos_ref, cnt_ref, *, tm):
    x = e_ref[...]
    rows, lanes = x.shape
    li = lax.broadcasted_iota(jnp.int32, (lanes, lanes), 0)
    lj = lax.broadcasted_iota(jnp.int32, (lanes, lanes), 1)
    before_lane = (li < lj).astype(BF16)
    ones = jnp.ones((lanes, lanes), BF16)
    ri = lax.broadcasted_iota(jnp.int32, (rows, rows), 0)
    rj = lax.broadcasted_iota(jnp.int32, (rows, rows), 1)
    before_row = (rj < ri).astype(BF16)
    lane = lax.broadcasted_iota(jnp.int32, (1, lanes), 1)
    pos = jnp.zeros(x.shape, F32)
    counts = jnp.zeros((1, lanes), F32)
    pad_off = jnp.zeros((1, lanes), F32)
    for e in range(N_EXPERTS):
        m = x == e
        mb = m.astype(BF16)
        in_row = jnp.dot(mb, before_lane, preferred_element_type=F32)
        row_sum = jnp.dot(mb, ones, preferred_element_type=F32)
        row_off = jnp.dot(before_row, row_sum.astype(BF16), preferred_element_type=F32)
        count = row_off[rows - 1:rows, :] + row_sum[rows - 1:rows, :]
        pos = pos + jnp.where(m, in_row + row_off + pad_off, 0.0)
        counts = jnp.where(lane == e, count, counts)
        pad_off = pad_off + jnp.floor((count + (tm - 1)) * (1.0 / tm)) * tm
    pos_ref[...] = pos.astype(jnp.int32)
    cnt_ref[...] = counts.astype(jnp.int32)


def _route_positions(flat_e, tm):
    lanes = 128
    rows = flat_e.shape[0] // lanes
    pos, counts = pl.pallas_call(
        functools.partial(_route_kernel, tm=tm),
        out_shape=[jax.ShapeDtypeStruct((rows, lanes), jnp.int32), jax.ShapeDtypeStruct((1, lanes), jnp.int32)],
        compiler_params=pltpu.CompilerParams(vmem_limit_bytes=VMEM_LIMIT),
        name="route_positions",
    )(flat_e.reshape(rows, lanes))
    return pos.reshape(-1), counts[0, :N_EXPERTS]


def _moe_layer(h, f, top_e, gate, g2, weights, lat_only, n_ctx):
    layer, w_gu, w_down, bg, bu, bd = weights
    bsz, lt, d = h.shape
    skip = n_ctx if lat_only else 0
    ltok = lt - skip
    top_e, gate = top_e[:, :, skip:], gate[:, :, skip:]
    n = bsz * ltok
    n_assign = n * TOP_K
    tm = MOE_TILE
    flat_e = top_e.transpose(1, 0, 2).reshape(-1).astype(jnp.int32)
    iota = jnp.arange(n_assign, dtype=jnp.int32)
    bits = max(n_assign - 1, 1).bit_length()
    assert N_EXPERTS << bits < 2 ** 31
    order = lax.sort((flat_e << bits) | iota) & ((1 << bits) - 1)
    pos, counts = _route_positions(flat_e, tm)
    start = jnp.cumsum(counts) - counts
    padded = (counts + tm - 1) // tm * tm
    pad_end = jnp.cumsum(padded)
    pad_start = pad_end - padded
    n_blocks = -(-(n_assign + N_EXPERTS * (tm - 1)) // tm)
    n_rows = n_blocks * tm
    block_expert = jnp.minimum(
        jnp.searchsorted(pad_end, jnp.arange(n_blocks, dtype=jnp.int32) * tm, side='right', method='compare_all'),
        N_EXPERTS - 1).astype(jnp.int32)
    n_used = (pad_end[-1] // tm).astype(jnp.int32).reshape(1)
    row = jnp.arange(n_rows, dtype=jnp.int32).reshape(n_blocks, tm)
    blk_shift = (start - pad_start)[block_expert][:, None]
    blk_end = (pad_start + counts)[block_expert][:, None]
    slot = jnp.clip(row + blk_shift, 0, n_assign - 1).reshape(-1)
    row_token = jnp.where((row < blk_end).reshape(-1),
                          order.at[slot].get(mode='promise_in_bounds'), row.reshape(-1)) % n
    row_src = row_token + skip * (row_token // ltok + 1)
    x_sorted = f.reshape(bsz * lt, d).at[row_src].get(mode='promise_in_bounds')
    y = _moe_experts(x_sorted, block_expert, n_used, layer, w_gu, w_down, bg, bu, bd)
    yg = y.at[pos].get(mode='promise_in_bounds').reshape(TOP_K, bsz, ltok, d)
    return _moe_combine(yg, gate.transpose(0, 2, 1), h, g2, _lat_seg_map if lat_only else _seg_map,
                        skip // ROW_TILE)


def _rmsnorm_kernel(x_ref, g_ref, o_ref):
    x = x_ref[0]
    ms = jnp.mean(x * x, axis=-1, keepdims=True)
    o_ref[0] = x * lax.rsqrt(ms + EPS) * g_ref[...]


def _final_norm(h, g):
    bsz, lt, d = h.shape
    tm = ROW_TILE
    return pl.pallas_call(
        _rmsnorm_kernel,
        grid=(bsz, lt // tm),
        in_specs=[pl.BlockSpec((1, tm, d), lambda b, i: (b, i, 0)),
                  pl.BlockSpec((1, d), lambda b, i: (0, 0))],
        out_specs=pl.BlockSpec((1, tm, d), lambda b, i: (b, i, 0)),
        out_shape=jax.ShapeDtypeStruct((bsz, lt, d), F32),
        compiler_params=_cparams(("parallel", "parallel")),
        name="final_norm",
    )(h, g.reshape(1, d))


def _grid_reorder_kernel(x_ref, o_ref, *, n_ctx, rows, to_cols):
    o_ref[0, :n_ctx, :] = x_ref[0, :n_ctx, :]
    for c in range(GRID_W):
        raster = pl.ds(n_ctx + c, rows, stride=GRID_W)
        dense = pl.ds(n_ctx + c * rows, rows)
        if to_cols:
            o_ref.at[0][dense, :] = x_ref.at[0][raster, :]
        else:
            o_ref.at[0][raster, :] = x_ref.at[0][dense, :]


def _grid_reorder(h, n_ctx, to_cols):
    bsz, lt, d = h.shape
    lanes = 128
    spec = pl.BlockSpec((1, lt, lanes), lambda b, j: (b, 0, j))
    return pl.pallas_call(
        functools.partial(_grid_reorder_kernel, n_ctx=n_ctx, rows=(lt - n_ctx) // GRID_W, to_cols=to_cols),
        grid=(bsz, d // lanes),
        in_specs=[spec],
        out_specs=spec,
        out_shape=jax.ShapeDtypeStruct(h.shape, h.dtype),
        compiler_params=_cparams(("parallel", "parallel")),
        name="grid_reorder",
    )(h)


def _chunk_order(d, c, n_ctx_chunks, n_chunks):
    bwd = jnp.where(c < n_ctx_chunks, n_ctx_chunks - 1 - c, n_chunks + n_ctx_chunks - 1 - c)
    return jnp.where(d == 0, c, bwd)


def _split_bf16(x, n):
    out = []
    for _ in range(n):
        p = x.astype(BF16)
        out.append(p)
        x = x - p.astype(F32)
    return out


def _dir_tri(d, t):
    row = lax.broadcasted_iota(jnp.int32, (t, t), 0)
    col = lax.broadcasted_iota(jnp.int32, (t, t), 1)
    return jnp.where(d == 0, col - row, row - col) <= 0


def _gla_kernel(q_ref, k_ref, v_ref, r_ref, w2_ref, b2_ref, o_ref, st_ref, *, t, heads, scale):
    d = pl.program_id(0)
    c = pl.program_id(2)

    @pl.when(c == 0)
    def _():
        st_ref[...] = jnp.zeros_like(st_ref)

    dk = q_ref.shape[-1] // heads
    dv = v_ref.shape[-1] // heads
    nb = q_ref.shape[0]
    mask = _dir_tri(d, t)
    tri = mask.astype(F32)
    mid = t // 2
    items = [(bb, h) for bb in range(nb) for h in range(heads)]
    w_hi, w_lo = _split_bf16(w2_ref[0], 2)
    xs = []
    for bb in range(nb):
        r_hi, r_lo = _split_bf16(r_ref[bb], 2)
        xs.append(jnp.dot(r_hi, w_hi, preferred_element_type=F32) + jnp.dot(r_hi, w_lo, preferred_element_type=F32)
                  + jnp.dot(r_lo, w_hi, preferred_element_type=F32) + b2_ref[0])
    las = [jax.nn.log_sigmoid(x) * (1.0 / GLA_TAU) for x in xs]
    tri_b = tri.astype(BF16)
    bs = [sum(jnp.dot(tri_b, p, preferred_element_type=F32) for p in _split_bf16(la, 3)) for la in las]
    qt, kt, qe, kh_end, e_end = [], [], [], [], []
    for bb in range(nb):
        b = bs[bb]
        b_m = b[mid:mid + 1, :]
        b_end = jnp.where(d == 0, b[t - 1:t, :], b[0:1, :])
        q_s = q_ref[bb] * (jnp.exp(b - b_m) * scale)
        k_s = k_ref[bb] * jnp.exp(b_m - b)
        qe.append((q_s * jnp.exp(b_m)).astype(BF16))
        kh_end.append((k_s * jnp.exp(b_end - b_m)).astype(BF16))
        e_end.append(jnp.exp(b_end))
        qt.append(q_s.astype(BF16))
        kt.append(k_s.astype(BF16))
    att, q_st, vs = {}, {}, {}
    for bb, h in items:
        ks = slice(h * dk, (h + 1) * dk)
        vs[bb, h] = v_ref[bb, :, h * dv:(h + 1) * dv].astype(BF16)
        att[bb, h] = lax.dot_general(qt[bb][:, ks], kt[bb][:, ks], NT, preferred_element_type=F32)
        q_st[bb, h] = lax.dot_general(qe[bb][:, ks], st_ref[bb * heads + h].astype(BF16), NT,
                                      preferred_element_type=F32)
    for bb, h in items:
        a = jnp.where(mask, att[bb, h], 0.0).astype(BF16)
        o_ref[0, bb, :, h * dv:(h + 1) * dv] = jnp.dot(a, vs[bb, h], preferred_element_type=F32) + q_st[bb, h]
    for bb, h in items:
        ks = slice(h * dk, (h + 1) * dk)
        upd = lax.dot_general(vs[bb, h], kh_end[bb][:, ks], TN, preferred_element_type=F32)
        st_ref[bb * heads + h] = st_ref[bb * heads + h] * e_end[bb][:, ks] + upd


def _gla_mixer(q, k, v, r, w2, b2, n_ctx):
    bsz, lt, dkt = q.shape
    dvt = v.shape[-1]
    nr = r.shape[-1]
    t, heads = MIX_CHUNK, GLA_HEADS
    bt = MIX_BATCH
    nch = lt // t
    ncc = n_ctx // t
    dk = dkt // heads
    dv = dvt // heads
    imap = lambda d, b, c: (b, _chunk_order(d, c, ncc, nch), 0)
    return pl.pallas_call(
        functools.partial(_gla_kernel, t=t, heads=heads, scale=dk ** -0.5),
        grid=(2, bsz // bt, nch),
        in_specs=[
            pl.BlockSpec((bt, t, dkt), imap),
            pl.BlockSpec((bt, t, dkt), imap),
            pl.BlockSpec((bt, t, dvt), imap),
            pl.BlockSpec((bt, t, nr), imap),
            pl.BlockSpec((1, nr, dkt), lambda d, b, c: (d, 0, 0)),
            pl.BlockSpec((1, 1, dkt), lambda d, b, c: (d, 0, 0)),
        ],
        out_specs=pl.BlockSpec((1, bt, t, dvt), lambda d, b, c: (d, b, _chunk_order(d, c, ncc, nch), 0)),
        out_shape=jax.ShapeDtypeStruct((2, bsz, lt, dvt), F32),
        scratch_shapes=[pltpu.VMEM((bt * heads, dv, dk), F32)],
        compiler_params=_cparams(("parallel", "parallel", "arbitrary")),
        name="gla_mixer",
    )(q, k, v, r, w2, b2)


def _mlstm_kernel(q_ref, k_ref, v_ref, gc_ref, gr_ref, o_ref, c_ref, n_ref, m_ref, *, t, heads):
    d = pl.program_id(0)
    c = pl.program_id(2)

    @pl.when(c == 0)
    def _():
        c_ref[...] = jnp.zeros_like(c_ref)
        n_ref[...] = jnp.zeros_like(n_ref)
        m_ref[...] = jnp.zeros_like(m_ref)

    dh = q_ref.shape[-1] // heads
    nb = q_ref.shape[0]
    mask = _dir_tri(d, t)
    tri = mask.astype(F32)
    items = [(bb, h) for bb in range(nb) for h in range(heads)]
    gate = []
    for bb in range(nb):
        gc = gc_ref[0, bb]
        gr = gr_ref[0, bb, 0]
        fc = jax.nn.log_sigmoid(gc[:, heads:])
        fr = jax.nn.log_sigmoid(gr[heads:, :])
        b_col = jnp.dot(tri, fc, preferred_element_type=F32, precision=HI)
        b_row = lax.dot_general(fr, tri, NT, preferred_element_type=F32, precision=HI)
        b_last = jnp.where(d == 0, b_col[t - 1:t, :], b_col[0:1, :])
        gate.append((gc[:, :heads], gr[:heads, :], b_col, b_row, b_last))
    qs, ks, vs, s_raw, q_c = {}, {}, {}, {}, {}
    for bb, h in items:
        hs = slice(h * dh, (h + 1) * dh)
        qs[bb, h] = q_ref[bb, :, hs]
        ks[bb, h] = k_ref[bb, :, hs]
        vs[bb, h] = v_ref[bb, :, hs].astype(BF16)
        s_raw[bb, h] = lax.dot_general(qs[bb, h], ks[bb, h], NT, preferred_element_type=F32)
        q_c[bb, h] = jnp.dot(qs[bb, h], c_ref[bb * heads + h].astype(BF16), preferred_element_type=F32)
    logw, log_inter, m_t, w_inter, scores, den, qn = {}, {}, {}, {}, {}, {}, {}
    for bb, h in items:
        _, ir, b_col, b_row, _ = gate[bb]
        bc = b_col[:, h:h + 1]
        logw[bb, h] = jnp.where(mask, bc - b_row[h:h + 1, :] + ir[h:h + 1, :], -jnp.inf)
        log_inter[bb, h] = bc + m_ref[bb * heads + h]
    for bb, h in items:
        m_t[bb, h] = jnp.maximum(log_inter[bb, h], jnp.max(logw[bb, h], axis=-1, keepdims=True))
        qn[bb, h] = jnp.sum(qs[bb, h].astype(F32) * n_ref[bb * heads + h], axis=-1, keepdims=True)
    for bb, h in items:
        w_inter[bb, h] = jnp.exp(log_inter[bb, h] - m_t[bb, h])
        scores[bb, h] = s_raw[bb, h] * jnp.exp(logw[bb, h] - m_t[bb, h])
    for bb, h in items:
        den[bb, h] = jnp.sum(scores[bb, h], axis=-1, keepdims=True) + w_inter[bb, h] * qn[bb, h]
    num = {}
    for bb, h in items:
        num[bb, h] = (jnp.dot(scores[bb, h].astype(BF16), vs[bb, h], preferred_element_type=F32)
                      + w_inter[bb, h] * q_c[bb, h])
    for bb, h in items:
        hs = slice(h * dh, (h + 1) * dh)
        o_ref[0, bb, :, hs] = num[bb, h] / jnp.maximum(jnp.abs(den[bb, h]), jnp.exp(-m_t[bb, h]))
    log_g, m_new, kw, upd, ksum = {}, {}, {}, {}, {}
    for bb, h in items:
        ic, _, b_col, _, b_last = gate[bb]
        log_g[bb, h] = b_last[:, h:h + 1] - b_col[:, h:h + 1] + ic[:, h:h + 1]
    for bb, h in items:
        b_last = gate[bb][4]
        m_new[bb, h] = jnp.maximum(b_last[:, h:h + 1] + m_ref[bb * heads + h],
                                   jnp.max(log_g[bb, h], axis=0, keepdims=True))
    for bb, h in items:
        kw[bb, h] = ks[bb, h].astype(F32) * jnp.exp(log_g[bb, h] - m_new[bb, h])
    for bb, h in items:
        upd[bb, h] = lax.dot_general(kw[bb, h].astype(BF16), vs[bb, h], TN, preferred_element_type=F32)
        ksum[bb, h] = jnp.sum(kw[bb, h], axis=0, keepdims=True)
    for bb, h in items:
        si = bb * heads + h
        b_last = gate[bb][4]
        keep = jnp.exp(b_last[:, h:h + 1] + m_ref[si] - m_new[bb, h])
        c_ref[si] = keep * c_ref[si] + upd[bb, h]
        n_ref[si] = keep * n_ref[si] + ksum[bb, h]
        m_ref[si] = m_new[bb, h]


def _mlstm_mixer(qk, v, gates, n_ctx):
    bsz, lt, w2 = qk.shape
    w = w2 // 2
    t, heads = MLSTM_CHUNK, MLSTM_HEADS
    dh = w // heads
    nch = lt // t
    ncc = n_ctx // t
    gc = gates.reshape(bsz, lt, 2, 2 * heads).transpose(2, 0, 1, 3)
    gr = gc.reshape(2, bsz, nch, t, 2 * heads).transpose(0, 1, 2, 4, 3)
    cmap = lambda d, b, c: _chunk_order(d, c, ncc, nch)
    bt = MIX_BATCH
    return pl.pallas_call(
        functools.partial(_mlstm_kernel, t=t, heads=heads),
        grid=(2, bsz // bt, nch),
        in_specs=[
            pl.BlockSpec((bt, t, w), lambda d, b, c: (b, cmap(d, b, c), 0)),
            pl.BlockSpec((bt, t, w), lambda d, b, c: (b, cmap(d, b, c), 1)),
            pl.BlockSpec((bt, t, w), lambda d, b, c: (b, cmap(d, b, c), 0)),
            pl.BlockSpec((1, bt, t, 2 * heads), lambda d, b, c: (d, b, cmap(d, b, c), 0)),
            pl.BlockSpec((1, bt, 1, 2 * heads, t), lambda d, b, c: (d, b, cmap(d, b, c), 0, 0)),
        ],
        out_specs=pl.BlockSpec((1, bt, t, w), lambda d, b, c: (d, b, cmap(d, b, c), 0)),
        out_shape=jax.ShapeDtypeStruct((2, bsz, lt, w), F32),
        scratch_shapes=[pltpu.VMEM((bt * heads, dh, dh), F32), pltpu.VMEM((bt * heads, 1, dh), F32),
                        pltpu.VMEM((bt * heads, 1, 1), F32)],
        compiler_params=_cparams(("parallel", "parallel", "arbitrary")),
        name="mlstm_mixer",
    )(qk, qk, v, gc, gr)


def _conv_kernel(x_ref, w_ref, b_ref, s_ref, o_ref, *, n_ctx):
    x = x_ref[0]
    lt = x.shape[0]
    row = lax.broadcasted_iota(jnp.int32, x.shape, 0)
    prev = jnp.where((row == 0) | (row == n_ctx), 0.0, pltpu.roll(x, 1, 0))
    nxt = jnp.where((row == n_ctx - 1) | (row == lt - 1), 0.0, pltpu.roll(x, lt - 1, 0))
    y = b_ref[...] + w_ref[0:1, :] * prev + w_ref[1:2, :] * x + w_ref[2:3, :] * nxt
    o_ref[0] = (y * jax.nn.sigmoid(y) * s_ref[...]).astype(o_ref.dtype)


def _conv_silu(x, w, b, colscale, n_ctx):
    bsz, lt, ch = x.shape
    tc = 256
    return pl.pallas_call(
        functools.partial(_conv_kernel, n_ctx=n_ctx),
        grid=(bsz, ch // tc),
        in_specs=[
            pl.BlockSpec((1, lt, tc), lambda b, j: (b, 0, j)),
            pl.BlockSpec((3, tc), lambda b, j: (0, j)),
            pl.BlockSpec((1, tc), lambda b, j: (0, j)),
            pl.BlockSpec((1, tc), lambda b, j: (0, j)),
        ],
        out_specs=pl.BlockSpec((1, lt, tc), lambda b, j: (b, 0, j)),
        out_shape=jax.ShapeDtypeStruct((bsz, lt, ch), BF16),
        compiler_params=_cparams(("parallel", "parallel")),
        name="conv_silu",
    )(x, w, b.reshape(1, ch), colscale.reshape(1, ch))


def _s5_matrices(a_re, a_im, log_dt, b_re, b_im, c_re, c_im, backward, lane_groups=8):
    g, p = a_re.shape
    cg = b_re.shape[-1]
    j = S5_J
    lg = lane_groups
    nq = g // lg
    dt = jnp.exp(log_dt)[:, None]
    lam_re = jnp.minimum(a_re, -1e-4)
    lam_im = a_im
    decay = jnp.exp(lam_re * dt)
    ab_re = decay * jnp.cos(lam_im * dt)
    ab_im = decay * jnp.sin(lam_im * dt)
    den = lam_re * lam_re + lam_im * lam_im
    zr = ((ab_re - 1) * lam_re + ab_im * lam_im) / den
    zi = (ab_im * lam_re - (ab_re - 1) * lam_im) / den
    bb_re = zr[..., None] * b_re - zi[..., None] * b_im
    bb_im = zr[..., None] * b_im + zi[..., None] * b_re
    pw_re, pw_im = [jnp.ones_like(ab_re)], [jnp.zeros_like(ab_im)]
    for _ in range(j):
        r0, i0 = pw_re[-1], pw_im[-1]
        pw_re.append(ab_re * r0 - ab_im * i0)
        pw_im.append(ab_re * i0 + ab_im * r0)
    pw_re, pw_im = jnp.stack(pw_re), jnp.stack(pw_im)
    ca_re = c_re[None] * pw_re[:, :, None, :] - c_im[None] * pw_im[:, :, None, :]
    ca_im = c_re[None] * pw_im[:, :, None, :] + c_im[None] * pw_re[:, :, None, :]
    kk = (jnp.einsum('tgcp,gpd->tgcd', ca_re[:j], bb_re, precision=HI)
          - jnp.einsum('tgcp,gpd->tgcd', ca_im[:j], bb_im, precision=HI))
    ab_pw_re = pw_re[:j, :, :, None] * bb_re[None] - pw_im[:j, :, :, None] * bb_im[None]
    ab_pw_im = pw_re[:j, :, :, None] * bb_im[None] + pw_im[:j, :, :, None] * bb_re[None]
    eye = jnp.eye(lg, dtype=BF16)
    lb = lg * cg
    sw = lg * p
    bd_k = jnp.einsum('tqgcd,gh->tqgdhc', kk.astype(BF16).reshape(j, nq, lg, cg, cg), eye
                      ).reshape(j, nq, lb, lb)
    bd_in = [jnp.einsum('tqgpc,gh->tqgchp', a.astype(BF16).reshape(j, nq, lg, p, cg), eye
                        ).reshape(j, nq, lb, sw) for a in (ab_pw_re, ab_pw_im)]
    bd_out = [jnp.einsum('tqgcp,gh->tqgphc', a.astype(BF16).reshape(j + 1, nq, lg, cg, p), eye
                         ).reshape(j + 1, nq, sw, lb) for a in (ca_re, -ca_im)]
    jj = jnp.arange(j)
    lag = (jj[:, None] - jj[None, :]) if backward else (jj[None, :] - jj[:, None])
    kt = jnp.where((lag >= 0)[:, :, None, None, None], bd_k[jnp.clip(lag, 0, j - 1)], 0)
    ktoep = kt.transpose(2, 0, 3, 1, 4).reshape(nq, j * lb, j * lb)
    tau_in = jj if backward else (j - 1 - jj)
    win_re, win_im = (a[tau_in].transpose(1, 0, 2, 3).reshape(nq, j * lb, sw) for a in bd_in)
    tau_out = (j - jj) if backward else (jj + 1)
    wout_re, wout_im = (a[tau_out].transpose(1, 2, 0, 3).reshape(nq, sw, j * lb) for a in bd_out)
    dec_re = pw_re[j].reshape(nq, 1, sw)
    dec_im = pw_im[j].reshape(nq, 1, sw)
    return ktoep, win_re, win_im, wout_re, wout_im, dec_re, dec_im


def _s5_kernel(u_ref, kt_ref, wir_ref, wii_ref, wor_ref, woi_ref, dr_ref, di_ref, y_ref,
               xf_ref, yf_ref, sre_ref, sim_ref, *, bt, nk, nk_ctx, rs):
    d = pl.program_id(0)
    j = S5_J
    lanes = u_ref.shape[-1]
    for b in range(bt):
        for jj in range(j):
            xf_ref[b * nk:(b + 1) * nk, jj * lanes:(jj + 1) * lanes] = (
                u_ref.at[b][pl.ds(jj, nk, stride=j), :].astype(BF16))
    xf = xf_ref[...]
    yf_ref[...] = jnp.dot(xf, kt_ref[0, 0], preferred_element_type=F32)
    inc_re = jnp.dot(xf, wir_ref[0, 0], preferred_element_type=F32)
    inc_im = jnp.dot(xf, wii_ref[0, 0], preferred_element_type=F32)
    nl = sre_ref.shape[0]
    for b in range(bt):
        for l in range(nl):
            sre_ref[l, b * rs:b * rs + nk, :] = inc_re[b * nk:(b + 1) * nk, l * lanes:(l + 1) * lanes]
            sim_ref[l, b * rs:b * rs + nk, :] = inc_im[b * nk:(b + 1) * nk, l * lanes:(l + 1) * lanes]
    a_re = [dr_ref[0, 0, :, l * lanes:(l + 1) * lanes] for l in range(nl)]
    a_im = [di_ref[0, 0, :, l * lanes:(l + 1) * lanes] for l in range(nl)]

    def step(kidx, carry):
        rows = pl.ds(kidx, bt, stride=rs)
        out = []
        for l in range(nl):
            s_re, s_im = carry[2 * l], carry[2 * l + 1]
            i_re = sre_ref.at[l][rows, :]
            i_im = sim_ref.at[l][rows, :]
            sre_ref.at[l][rows, :] = s_re
            sim_ref.at[l][rows, :] = s_im
            out.append(a_re[l] * s_re - a_im[l] * s_im + i_re)
            out.append(a_re[l] * s_im + a_im[l] * s_re + i_im)
        return tuple(out)

    zero = tuple(jnp.zeros((bt, lanes), F32) for _ in range(2 * nl))

    @pl.when(d == 0)
    def _():
        lax.fori_loop(0, nk, step, zero)

    @pl.when(d == 1)
    def _():
        carry = lax.fori_loop(0, nk_ctx, lambda i, cr: step(nk_ctx - 1 - i, cr), zero)
        lax.fori_loop(0, nk - nk_ctx, lambda i, cr: step(nk - 1 - i, cr), carry)

    for b in range(bt):
        sp_re = jnp.concatenate([sre_ref[l, b * rs:b * rs + nk, :] for l in range(nl)], axis=-1).astype(BF16)
        sp_im = jnp.concatenate([sim_ref[l, b * rs:b * rs + nk, :] for l in range(nl)], axis=-1).astype(BF16)
        yb = (yf_ref[b * nk:(b + 1) * nk, :]
              + jnp.dot(sp_re, wor_ref[0, 0], preferred_element_type=F32)
              + jnp.dot(sp_im, woi_ref[0, 0], preferred_element_type=F32))
        for jj in range(j):
            y_ref.at[0, b][pl.ds(jj, nk, stride=j), :] = yb[:, jj * lanes:(jj + 1) * lanes]


def _s5_mixer(u, mats, n_ctx):
    bsz, lt, w = u.shape
    ktoep, win_re, win_im, wout_re, wout_im, dec_re, dec_im = mats
    lanes = 128
    bt = 4 if bsz % 4 == 0 else 2
    nq = w // lanes
    j = S5_J
    nk = lt // j
    nk_ctx = n_ctx // j
    rs = nk + 8
    fl = j * lanes
    sw = win_re.shape[-1]
    wmap = lambda d, q, b: (d, q, 0, 0)
    return pl.pallas_call(
        functools.partial(_s5_kernel, bt=bt, nk=nk, nk_ctx=nk_ctx, rs=rs),
        grid=(2, nq, bsz // bt),
        in_specs=[
            pl.BlockSpec((bt, lt, lanes), lambda d, q, b: (b, 0, q)),
            pl.BlockSpec((1, 1, fl, fl), wmap),
            pl.BlockSpec((1, 1, fl, sw), wmap),
            pl.BlockSpec((1, 1, fl, sw), wmap),
            pl.BlockSpec((1, 1, sw, fl), wmap),
            pl.BlockSpec((1, 1, sw, fl), wmap),
            pl.BlockSpec((1, 1, 1, sw), wmap),
            pl.BlockSpec((1, 1, 1, sw), wmap),
        ],
        out_specs=pl.BlockSpec((1, bt, lt, lanes), lambda d, q, b: (d, b, 0, q)),
        out_shape=jax.ShapeDtypeStruct((2, bsz, lt, w), F32),
        scratch_shapes=[pltpu.VMEM((bt * nk, fl), BF16), pltpu.VMEM((bt * nk, fl), F32),
                        pltpu.VMEM((sw // lanes, bt * rs, lanes), F32),
                        pltpu.VMEM((sw // lanes, bt * rs, lanes), F32)],
        compiler_params=_cparams(("parallel", "parallel", "arbitrary")),
        name="s5_mixer",
    )(u, ktoep, win_re, win_im, wout_re, wout_im, dec_re, dec_im)


def _head_norm(x, heads):
    dh = x.shape[-1] // heads
    outs = []
    for h in range(heads):
        xh = x[:, h * dh:(h + 1) * dh]
        outs.append(xh * lax.rsqrt(jnp.mean(xh * xh, axis=-1, keepdims=True) + EPS))
    return jnp.concatenate(outs, axis=-1)


def _even_post_kernel(m_ref, o_ref, s_ref, u_ref, mg_ref, dsk_ref, gw_ref, gb_ref, w_ref, h_ref, gate_ref,
                      out_ref, *, heads):
    m = m_ref[0, 0] + m_ref[1, 0]
    m_out = _head_norm(m, heads) * mg_ref[...] * jax.nn.sigmoid(o_ref[0])
    y = jax.nn.gelu(s_ref[0, 0] + s_ref[1, 0] + dsk_ref[...] * u_ref[0])
    glu = jnp.dot(y.astype(BF16), gw_ref[...], preferred_element_type=F32) + gb_ref[...]
    s_out = y * jax.nn.sigmoid(glu)
    cat = jnp.concatenate([m_out, s_out], axis=-1).astype(BF16)
    z = jnp.dot(cat, w_ref[...], preferred_element_type=F32)
    out_ref[0] = h_ref[0] + gate_ref[0, 0] * z


def _even_post(m2, o, s2, u, mnorm_g, d_skip, glu_w, glu_b, w_out, h, gate):
    bsz, lt, d = h.shape
    mw = o.shape[-1]
    sw = u.shape[-1]
    tm = ROW_TILE
    row = lambda b, i: (b, i, 0)
    row2 = lambda b, i: (0, b, i, 0)
    const = lambda b, i: (0, 0)
    return pl.pallas_call(
        functools.partial(_even_post_kernel, heads=MLSTM_HEADS),
        grid=(bsz, lt // tm),
        in_specs=[
            pl.BlockSpec((2, 1, tm, mw), row2),
            pl.BlockSpec((1, tm, mw), row),
            pl.BlockSpec((2, 1, tm, sw), row2),
            pl.BlockSpec((1, tm, sw), row),
            pl.BlockSpec((1, mw), const),
            pl.BlockSpec((1, sw), const),
            pl.BlockSpec((sw, sw), const),
            pl.BlockSpec((1, sw), const),
            pl.BlockSpec((mw + sw, d), const),
            pl.BlockSpec((1, tm, d), row),
            pl.BlockSpec((1, 1, 1, d), _seg_map),
        ],
        out_specs=pl.BlockSpec((1, tm, d), row),
        out_shape=jax.ShapeDtypeStruct((bsz, lt, d), F32),
        compiler_params=_cparams(("parallel", "parallel")),
        name="even_post",
    )(m2, o, s2, u, mnorm_g.reshape(1, mw), d_skip.reshape(1, sw), glu_w.astype(BF16), glu_b.reshape(1, sw),
      w_out.astype(BF16), h, gate)


def _odd_post_kernel(o_ref, g_ref, ng_ref, w_ref, h_ref, gate_ref, out_ref, *, heads):
    g = g_ref[0]
    y = _head_norm(o_ref[0, 0] + o_ref[1, 0], heads) * ng_ref[...] * (g * jax.nn.sigmoid(g))
    z = jnp.dot(y.astype(BF16), w_ref[...], preferred_element_type=F32)
    out_ref[0] = h_ref[0] + gate_ref[0, 0] * z


def _odd_post(o, g, norm_g, w_out, h, gate):
    bsz, lt, d = h.shape
    dv = o.shape[-1]
    tm = ROW_TILE
    row = lambda b, i: (b, i, 0)
    const = lambda b, i: (0, 0)
    return pl.pallas_call(
        functools.partial(_odd_post_kernel, heads=GLA_HEADS),
        grid=(bsz, lt // tm),
        in_specs=[
            pl.BlockSpec((2, 1, tm, dv), lambda b, i: (0, b, i, 0)),
            pl.BlockSpec((1, tm, dv), row),
            pl.BlockSpec((1, dv), const),
            pl.BlockSpec((dv, d), const),
            pl.BlockSpec((1, tm, d), row),
            pl.BlockSpec((1, 1, 1, d), _seg_map),
        ],
        out_specs=pl.BlockSpec((1, tm, d), row),
        out_shape=jax.ShapeDtypeStruct((bsz, lt, d), F32),
        compiler_params=_cparams(("parallel", "parallel")),
        name="odd_post",
    )(o, g, norm_g.reshape(1, dv), w_out.astype(BF16), h, gate)


def kernel(x, c, ctx, c_ctx, mod_w, mod_b, norm_mix_g, norm_ffn_g, ev_w_in, ev_b_in, ev_conv_w, ev_conv_b, ev_mlstm_norm_g, ev_s5_a_re_f, ev_s5_a_im_f, ev_s5_log_dt_f, ev_s5_a_re_b, ev_s5_a_im_b, ev_s5_log_dt_b, ev_s5_b_re, ev_s5_b_im, ev_s5_c_re, ev_s5_c_im, ev_s5_d, ev_s5_glu_w, ev_s5_glu_b, ev_w_out, od_w_in, od_gate_w2_f, od_gate_b2_f, od_gate_w2_b, od_gate_b2_b, od_norm_g, od_w_out, router_w, router_b, moe_w_gu, moe_b_gu, moe_w_down, moe_b_down, final_norm_g):
    bsz, seq, d = x.shape
    n_ctx = ctx.shape[1]
    depth = mod_w.shape[0]
    lt = n_ctx + seq
    assert n_ctx == ROW_TILE and seq % ROW_TILE == 0 and seq % GRID_W == 0

    h = jnp.concatenate([ctx, x], axis=1)
    c_all = jnp.concatenate([c, c_ctx[None, :]], axis=0)
    c_all = jnp.pad(c_all, ((0, (-c_all.shape[0]) % 8), (0, 0)))
    mods = _modulation(c_all, mod_w, mod_b)
    mod_lat = mods[:, :bsz]
    mod_ctx = jnp.broadcast_to(mods[:, bsz:bsz + 1], mod_lat.shape)
    mod6 = jnp.stack([mod_ctx, mod_lat], axis=2).reshape(depth, bsz, 2, 6, 1, d)

    bg_all = moe_b_gu[..., 0::2]
    bu_all = moe_b_gu[..., 1::2]

    mw = ev_conv_w.shape[-1] // 2
    n_gates = 4 * MLSTM_HEADS
    s5w = ev_s5_d.shape[-1]
    dk_t = od_gate_w2_f.shape[-1]
    dv_t = od_norm_g.shape[-1]
    for layer in range(depth):
        last = layer == depth - 1
        j = layer // 2
        m6 = mod6[layer]
        sh1, sc1, g1, sh2, sc2, g2 = (m6[:, :, i] for i in range(6))
        if layer % 2 == 0:
            w_in, b_in = ev_w_in[j], ev_b_in[j]
            cols = jnp.concatenate([jnp.arange(0, 4 * mw), jnp.arange(4 * mw + n_gates, 4 * mw + n_gates + s5w),
                                    jnp.arange(4 * mw, 4 * mw + n_gates)])
            qk_pre, v, o, u, gates = _nm_matmul(h, norm_mix_g[layer], sh1, sc1, w_in[:, cols], b_in[cols],
                                                (2 * mw, mw, mw, s5w, n_gates))
            dh = mw // MLSTM_HEADS
            colscale = jnp.concatenate([jnp.full((mw,), dh ** -0.5, F32), jnp.ones((mw,), F32)])
            qk = _conv_silu(qk_pre, ev_conv_w[j], ev_conv_b[j], colscale, n_ctx)
            m2 = _mlstm_mixer(qk, v, gates, n_ctx)
            shared = (ev_s5_b_re[j], ev_s5_b_im[j], ev_s5_c_re[j], ev_s5_c_im[j])
            mats_f = _s5_matrices(ev_s5_a_re_f[j], ev_s5_a_im_f[j], ev_s5_log_dt_f[j], *shared, backward=False)
            mats_b = _s5_matrices(ev_s5_a_re_b[j], ev_s5_a_im_b[j], ev_s5_log_dt_b[j], *shared, backward=True)
            s2 = _s5_mixer(u, tuple(jnp.stack([a, b]) for a, b in zip(mats_f, mats_b)), n_ctx)
            h = _even_post(m2, o, s2, u, ev_mlstm_norm_g[j], ev_s5_d[j], ev_s5_glu_w[j], ev_s5_glu_b[j],
                           ev_w_out[j], h, g1)
        else:
            hc = _grid_reorder(h, n_ctx, True)
            qq, kk, vv, gg, rr = _nm_matmul(hc, norm_mix_g[layer], sh1, sc1, od_w_in[j],
                                            jnp.zeros((od_w_in.shape[-1],), F32),
                                            (dk_t, dk_t, dv_t, dv_t, 2 * GLA_RANK))
            zero = jnp.zeros_like(od_gate_w2_f[j])
            w2 = jnp.stack([jnp.concatenate([od_gate_w2_f[j], zero], axis=0),
                            jnp.concatenate([zero, od_gate_w2_b[j]], axis=0)])
            b2 = jnp.stack([od_gate_b2_f[j], od_gate_b2_b[j]])[:, None, :]
            o2 = _gla_mixer(qq, kk, vv, rr, w2, b2, n_ctx)
            h = _grid_reorder(_odd_post(o2, gg, od_norm_g[j], od_w_out[j], hc, g1), n_ctx, False)
        f, top_e, gate = _ffn_prep(h, norm_ffn_g[layer], sh2, sc2, router_w[layer], router_b[layer])
        weights = (layer, moe_w_gu, moe_w_down, bg_all[layer], bu_all[layer], moe_b_down[layer])
        h = _moe_layer(h, f, top_e, gate, g2, weights, last, n_ctx)
    return _final_norm(h, final_norm_g)
```

```python
import functools

import jax
import jax.numpy as jnp
from jax import lax
from jax.experimental import pallas as pl
from jax.experimental.pallas import tpu as pltpu

F32 = jnp.float32
BF16 = jnp.bfloat16
HI = lax.Precision.HIGHEST

EPS = 1e-6
GRID_W = 64
MLSTM_HEADS = 4
S5_GROUP = 16
GLA_HEADS = 4
GLA_RANK = 16
GLA_TAU = 16.0
N_EXPERTS = 32
TOP_K = 4
SWIGLU_LIMIT = 7.0
SWIGLU_ALPHA = 1.702

ROW_TILE = 256
MOE_TILE = 512
MIX_CHUNK = 64
MLSTM_CHUNK = 128
MIX_BATCH = 4
S5_J = 8
VMEM_LIMIT = 56 * 1024 * 1024

NT = (((1,), (1,)), ((), ()))
TN = (((0,), (0,)), ((), ()))


def _cparams(sem):
    return pltpu.CompilerParams(dimension_semantics=sem, vmem_limit_bytes=VMEM_LIMIT)


def _mod_kernel(c_ref, w_ref, b_ref, o_ref):
    c = c_ref[...]
    a = c * jax.nn.sigmoid(c)
    o_ref[0] = jnp.dot(a.astype(BF16), w_ref[0].astype(BF16), preferred_element_type=F32) + b_ref[0]


def _modulation(c_all, mod_w, mod_b):
    depth, d, n6 = mod_w.shape
    rows = c_all.shape[0]
    tn = d
    return pl.pallas_call(
        _mod_kernel,
        grid=(depth, n6 // tn),
        in_specs=[
            pl.BlockSpec((rows, d), lambda l, j: (0, 0)),
            pl.BlockSpec((1, d, tn), lambda l, j: (l, 0, j)),
            pl.BlockSpec((1, 1, tn), lambda l, j: (l, 0, j)),
        ],
        out_specs=pl.BlockSpec((1, rows, tn), lambda l, j: (l, 0, j)),
        out_shape=jax.ShapeDtypeStruct((depth, rows, n6), F32),
        compiler_params=_cparams(("arbitrary", "arbitrary")),
        name="modulation",
    )(c_all, mod_w, mod_b.reshape(depth, 1, n6))


def _norm_mod(x, g, sh, sc):
    ms = jnp.mean(x * x, axis=-1, keepdims=True)
    return (x * lax.rsqrt(ms + EPS) * g) * (1.0 + sc) + sh


def _nm_matmul_kernel(x_ref, g_ref, sh_ref, sc_ref, w_ref, b_ref, *out_refs, splits):
    a = _norm_mod(x_ref[0], g_ref[...], sh_ref[0, 0], sc_ref[0, 0])
    z = jnp.dot(a.astype(BF16), w_ref[...], preferred_element_type=F32) + b_ref[...]
    for (lo, hi), o_ref in zip(splits, out_refs):
        o_ref[0] = z[:, lo:hi].astype(o_ref.dtype)


def _wide_tile(rows):
    return next(t for t in (4 * ROW_TILE, 3 * ROW_TILE, 2 * ROW_TILE, ROW_TILE) if rows % t == 0)


def _seg_map(b, i):
    return (b, jnp.minimum(i, 1), 0, 0)


def _nm_matmul(h, g, shift, scale, w, bias, widths):
    bsz, lt, d = h.shape
    p = w.shape[1]
    splits, lo = [], 0
    for wd in widths:
        splits.append((lo, lo + wd))
        lo += wd
    assert lo == p
    tm = ROW_TILE
    return pl.pallas_call(
        functools.partial(_nm_matmul_kernel, splits=tuple(splits)),
        grid=(bsz, lt // tm),
        in_specs=[
            pl.BlockSpec((1, tm, d), lambda b, i: (b, i, 0)),
            pl.BlockSpec((1, d), lambda b, i: (0, 0)),
            pl.BlockSpec((1, 1, 1, d), _seg_map),
            pl.BlockSpec((1, 1, 1, d), _seg_map),
            pl.BlockSpec((d, p), lambda b, i: (0, 0)),
            pl.BlockSpec((1, p), lambda b, i: (0, 0)),
        ],
        out_specs=[pl.BlockSpec((1, tm, wd), lambda b, i: (b, i, 0)) for wd in widths],
        out_shape=[jax.ShapeDtypeStruct((bsz, lt, wd), F32) for wd in widths],
        compiler_params=_cparams(("parallel", "parallel")),
        name="norm_mod_matmul",
    )(h, g.reshape(1, d), shift, scale, w.astype(BF16), bias.reshape(1, p))


def _ffn_prep_kernel(x_ref, g_ref, sh_ref, sc_ref, rw_ref, rb_ref, f_ref, te_ref, gt_ref, *, n_ctx):
    tm = x_ref.shape[1]
    is_ctx = pl.program_id(1) * tm + lax.broadcasted_iota(jnp.int32, (tm, 1), 0) < n_ctx
    a = _norm_mod(x_ref[0], g_ref[...], jnp.where(is_ctx, sh_ref[0, 0], sh_ref[0, 1]),
                  jnp.where(is_ctx, sc_ref[0, 0], sc_ref[0, 1]))
    f_ref[0] = a.astype(f_ref.dtype)
    logits = lax.dot_general(rw_ref[...], a.astype(BF16), NT, preferred_element_type=F32) + rb_ref[...]
    ne = logits.shape[0]
    eidx = lax.broadcasted_iota(jnp.int32, logits.shape, 0)
    work = logits
    vals, idxs = [], []
    for _ in range(TOP_K):
        m = jnp.max(work, axis=0, keepdims=True)
        idx = jnp.min(jnp.where(work == m, eidx, ne), axis=0, keepdims=True)
        vals.append(m)
        idxs.append(idx)
        work = jnp.where(eidx == idx, -jnp.inf, work)
    exps = [jnp.exp(v - vals[0]) for v in vals]
    denom = exps[0]
    for e in exps[1:]:
        denom = denom + e
    for k in range(TOP_K):
        te_ref[0, k:k + 1, :] = idxs[k]
        gt_ref[0, k:k + 1, :] = exps[k] / denom


def _ffn_prep(h, g, shift, scale, router_w, router_b):
    bsz, lt, d = h.shape
    ne = router_w.shape[1]
    tm = _wide_tile(lt)
    both = pl.BlockSpec((1, 2, 1, d), lambda b, i: (b, 0, 0, 0))
    return pl.pallas_call(
        functools.partial(_ffn_prep_kernel, n_ctx=ROW_TILE),
        grid=(bsz, lt // tm),
        in_specs=[
            pl.BlockSpec((1, tm, d), lambda b, i: (b, i, 0)),
            pl.BlockSpec((1, d), lambda b, i: (0, 0)),
            both,
            both,
            pl.BlockSpec((ne, d), lambda b, i: (0, 0)),
            pl.BlockSpec((ne, 1), lambda b, i: (0, 0)),
        ],
        out_specs=[
            pl.BlockSpec((1, tm, d), lambda b, i: (b, i, 0)),
            pl.BlockSpec((1, TOP_K, tm), lambda b, i: (b, 0, i)),
            pl.BlockSpec((1, TOP_K, tm), lambda b, i: (b, 0, i)),
        ],
        out_shape=[
            jax.ShapeDtypeStruct((bsz, lt, d), BF16),
            jax.ShapeDtypeStruct((bsz, TOP_K, lt), jnp.int32),
            jax.ShapeDtypeStruct((bsz, TOP_K, lt), F32),
        ],
        compiler_params=_cparams(("parallel", "parallel")),
        name="ffn_prep",
    )(h, g.reshape(1, d), shift, scale, router_w.T.astype(BF16), router_b.reshape(ne, 1))


GU_BLOCK = 256


def _moe_kernel(be_ref, nb_ref, first_ref, slot_ref, nxt_ref, x_ref, wgu_hbm, wd_hbm, bg_ref, bu_ref, bd_ref,
                o_ref, wgu_buf, wd_buf, wgu_s, wd_s, sem, *, layer):
    i = pl.program_id(0)
    active = i < nb_ref[0]
    half = GU_BLOCK // 2
    nblk = wgu_s.shape[1] // GU_BLOCK

    def weight_copies(e, slot):
        return (pltpu.make_async_copy(wgu_hbm.at[layer, e], wgu_buf.at[slot], sem.at[0, slot]),
                pltpu.make_async_copy(wd_hbm.at[layer, e], wd_buf.at[slot], sem.at[1, slot]))

    @pl.when(active & (i == 0))
    def _():
        for cp in weight_copies(be_ref[0], 0):
            cp.start()

    @pl.when(active & (first_ref[i] == 1))
    def _():
        slot = slot_ref[i]
        for cp in weight_copies(be_ref[i], slot):
            cp.wait()

        @pl.when(nxt_ref[i] >= 0)
        def _():
            for cp in weight_copies(nxt_ref[i], 1 - slot):
                cp.start()

        r = lax.broadcasted_iota(jnp.int32, (GU_BLOCK, GU_BLOCK), 0)
        c = lax.broadcasted_iota(jnp.int32, (GU_BLOCK, GU_BLOCK), 1)
        perm = (r == jnp.where(c < half, 2 * c, 2 * (c - half) + 1)).astype(BF16)
        for k in range(nblk):
            cs = slice(k * GU_BLOCK, (k + 1) * GU_BLOCK)
            wgu_s[:, cs] = jnp.dot(wgu_buf[slot, :, cs].astype(BF16), perm,
                                   preferred_element_type=F32).astype(BF16)
        wd_s[...] = wd_buf[slot].astype(BF16)

    @pl.when(active)
    def _():
        gu = jnp.dot(x_ref[...], wgu_s[...], preferred_element_type=F32)
        hdn = []
        for k in range(nblk):
            hs = slice(k * half, (k + 1) * half)
            g = gu[:, k * GU_BLOCK:k * GU_BLOCK + half] + bg_ref[0, :, hs]
            u = gu[:, k * GU_BLOCK + half:(k + 1) * GU_BLOCK] + bu_ref[0, :, hs]
            g = jnp.minimum(g, SWIGLU_LIMIT)
            u = jnp.clip(u, -SWIGLU_LIMIT, SWIGLU_LIMIT)
            hdn.append(((u + 1.0) * (g * jax.nn.sigmoid(SWIGLU_ALPHA * g))).astype(BF16))
        hdn = jnp.concatenate(hdn, axis=-1)
        o_ref[...] = (jnp.dot(hdn, wd_s[...], preferred_element_type=F32) + bd_ref[0]).astype(o_ref.dtype)

    @pl.when(jnp.logical_not(active))
    def _():
        o_ref[...] = jnp.zeros_like(o_ref)


def _moe_experts(x_sorted, block_expert, n_used, layer, w_gu, w_down, bg, bu, bd):
    n_rows, d = x_sorted.shape
    _, ne, _, f2 = w_gu.shape
    f = f2 // 2
    tm = MOE_TILE
    n_blocks = n_rows // tm
    assert f2 % GU_BLOCK == 0
    blk = jnp.arange(n_blocks, dtype=jnp.int32)
    prev = jnp.concatenate([block_expert[:1], block_expert[:-1]])
    first = (blk < n_used[0]) & ((blk == 0) | (block_expert != prev))
    slot = (jnp.cumsum(first.astype(jnp.int32)) - 1) & 1
    first_idx = jnp.where(first, blk, n_blocks)
    next_first = lax.cummin(first_idx, axis=0, reverse=True)
    next_first = jnp.concatenate([next_first[1:], jnp.full((1,), n_blocks, jnp.int32)])
    nxt = jnp.where(next_first < n_blocks, block_expert[jnp.minimum(next_first, n_blocks - 1)], -1)
    bmap = lambda i, be, nb, fi, sl, nx: (be[i], 0, 0)
    rmap = lambda i, be, nb, fi, sl, nx: (i, 0)
    grid_spec = pltpu.PrefetchScalarGridSpec(
        num_scalar_prefetch=5,
        grid=(n_blocks,),
        in_specs=[
            pl.BlockSpec((tm, d), rmap),
            pl.BlockSpec(memory_space=pl.ANY),
            pl.BlockSpec(memory_space=pl.ANY),
            pl.BlockSpec((1, 1, f), bmap),
            pl.BlockSpec((1, 1, f), bmap),
            pl.BlockSpec((1, 1, d), bmap),
        ],
        out_specs=pl.BlockSpec((tm, d), rmap),
        scratch_shapes=[pltpu.VMEM((2, d, f2), F32), pltpu.VMEM((2, f, d), F32),
                        pltpu.VMEM((d, f2), BF16), pltpu.VMEM((f, d), BF16),
                        pltpu.SemaphoreType.DMA((2, 2))],
    )
    return pl.pallas_call(
        functools.partial(_moe_kernel, layer=layer),
        grid_spec=grid_spec,
        out_shape=jax.ShapeDtypeStruct((n_rows, d), BF16),
        compiler_params=_cparams(("arbitrary",)),
        name="moe_experts",
    )(block_expert, n_used, first.astype(jnp.int32), slot.astype(jnp.int32), nxt.astype(jnp.int32),
      x_sorted, w_gu, w_down, bg.reshape(ne, 1, f), bu.reshape(ne, 1, f), bd.reshape(ne, 1, d))


def _combine_kernel(y_ref, gt_ref, h_ref, g2_ref, o_ref, *, n_ctx):
    gt = gt_ref[0]
    acc = y_ref[0, 0].astype(F32) * gt[:, 0:1]
    for k in range(1, TOP_K):
        acc = acc + y_ref[k, 0].astype(F32) * gt[:, k:k + 1]
    tm = acc.shape[0]
    is_ctx = pl.program_id(1) * tm + lax.broadcasted_iota(jnp.int32, (tm, 1), 0) < n_ctx
    o_ref[0] = h_ref[0] + jnp.where(is_ctx, g2_ref[0, 0], g2_ref[0, 1]) * acc


def _moe_combine(yg, gate, h, g2, lat_only):
    k, bsz, lt, d = yg.shape
    tm = ROW_TILE if lat_only else _wide_tile(lt)
    tile_off = 1 if lat_only else 0
    return pl.pallas_call(
        functools.partial(_combine_kernel, n_ctx=0 if lat_only else ROW_TILE),
        grid=(bsz, lt // tm),
        in_specs=[
            pl.BlockSpec((k, 1, tm, d), lambda b, i: (0, b, i, 0)),
            pl.BlockSpec((1, tm, k), lambda b, i: (b, i, 0)),
            pl.BlockSpec((1, tm, d), lambda b, i: (b, i + tile_off, 0)),
            pl.BlockSpec((1, 2, 1, d), lambda b, i: (b, 0, 0, 0)),
        ],
        out_specs=pl.BlockSpec((1, tm, d), lambda b, i: (b, i, 0)),
        out_shape=jax.ShapeDtypeStruct((bsz, lt, d), F32),
        compiler_params=_cparams(("parallel", "parallel")),
        name="moe_combine",
    )(yg, gate, h, g2)


def _route_kernel(e_ref, pos_ref, cnt_ref, *, tm):
    nk, rows, lanes = e_ref.shape
    li = lax.broadcasted_iota(jnp.int32, (lanes, lanes), 0)
    lj = lax.broadcasted_iota(jnp.int32, (lanes, lanes), 1)
    before_lane = (li < lj).astype(BF16)
    ones = jnp.ones((lanes, lanes), BF16)
    ri = lax.broadcasted_iota(jnp.int32, (rows, rows), 0)
    rj = lax.broadcasted_iota(jnp.int32, (rows, rows), 1)
    before_row = (rj < ri).astype(BF16)
    lane = lax.broadcasted_iota(jnp.int32, (1, lanes), 1)
    xs = [e_ref[k] for k in range(nk)]
    pos = [jnp.zeros((rows, lanes), F32) for _ in range(nk)]
    counts = jnp.zeros((1, lanes), F32)
    pad_off = jnp.zeros((1, lanes), F32)
    for e in range(N_EXPERTS):
        ms = [x == e for x in xs]
        hit = ms[0]
        for m in ms[1:]:
            hit = hit | m
        mb = hit.astype(BF16)
        in_row = jnp.dot(mb, before_lane, preferred_element_type=F32)
        row_sum = jnp.dot(mb, ones, preferred_element_type=F32)
        row_off = jnp.dot(before_row, row_sum.astype(BF16), preferred_element_type=F32)
        count = row_off[rows - 1:rows, :] + row_sum[rows - 1:rows, :]
        dest = in_row + row_off + pad_off
        pos = [p + jnp.where(m, dest, 0.0) for p, m in zip(pos, ms)]
        counts = jnp.where(lane == e, count, counts)
        pad_off = pad_off + jnp.floor((count + (tm - 1)) * (1.0 / tm)) * tm
    for k in range(nk):
        pos_ref[k] = pos[k].astype(jnp.int32)
    cnt_ref[...] = counts.astype(jnp.int32)


def _route_positions(top_e, tm):
    lanes = 128
    nk, n = top_e.shape
    rows = n // lanes
    pos, counts = pl.pallas_call(
        functools.partial(_route_kernel, tm=tm),
        out_shape=[jax.ShapeDtypeStruct((nk, rows, lanes), jnp.int32), jax.ShapeDtypeStruct((1, lanes), jnp.int32)],
        compiler_params=pltpu.CompilerParams(vmem_limit_bytes=VMEM_LIMIT),
        name="route_positions",
    )(top_e.reshape(nk, rows, lanes))
    return pos.reshape(nk, n), counts[0, :N_EXPERTS]


def _moe_layer(h, f, top_e, gate, g2, weights, lat_only, n_ctx):
    layer, w_gu, w_down, bg, bu, bd = weights
    bsz, lt, d = h.shape
    skip = n_ctx if lat_only else 0
    ltok = lt - skip
    top_e, gate = top_e[:, :, skip:], gate[:, :, skip:]
    n = bsz * ltok
    n_assign = n * TOP_K
    tm = MOE_TILE
    te = top_e.transpose(1, 0, 2).reshape(TOP_K, n).astype(jnp.int32)
    bits = max(n - 1, 1).bit_length()
    assert N_EXPERTS << bits < 2 ** 31
    keys = (te << bits) | jnp.arange(n, dtype=jnp.int32)[None, :]
    slot_token = lax.sort(keys.reshape(-1)) & ((1 << bits) - 1)
    pos, counts = _route_positions(te, tm)
    start = jnp.cumsum(counts) - counts
    padded = (counts + tm - 1) // tm * tm
    pad_end = jnp.cumsum(padded)
    pad_start = pad_end - padded
    n_blocks = -(-(n_assign + N_EXPERTS * (tm - 1)) // tm)
    n_rows = n_blocks * tm
    block_expert = jnp.minimum(
        jnp.searchsorted(pad_end, jnp.arange(n_blocks, dtype=jnp.int32) * tm, side='right', method='compare_all'),
        N_EXPERTS - 1).astype(jnp.int32)
    n_used = (pad_end[-1] // tm).astype(jnp.int32).reshape(1)
    row = jnp.arange(n_rows, dtype=jnp.int32).reshape(n_blocks, tm)
    blk_shift = (start - pad_start)[block_expert][:, None]
    blk_end = (pad_start + counts)[block_expert][:, None]
    slot = jnp.clip(row + blk_shift, 0, n_assign - 1).reshape(-1)
    row_token = jnp.where((row < blk_end).reshape(-1),
                          slot_token.at[slot].get(mode='promise_in_bounds'), row.reshape(-1) % n)
    row_src = row_token + skip * (row_token // ltok + 1)
    x_sorted = f.reshape(bsz * lt, d).at[row_src].get(mode='promise_in_bounds')
    y = _moe_experts(x_sorted, block_expert, n_used, layer, w_gu, w_down, bg, bu, bd)
    yg = y.at[pos.reshape(-1)].get(mode='promise_in_bounds').reshape(TOP_K, bsz, ltok, d)
    return _moe_combine(yg, gate.transpose(0, 2, 1), h, g2, lat_only)


def _rmsnorm_kernel(x_ref, g_ref, o_ref):
    x = x_ref[0]
    ms = jnp.mean(x * x, axis=-1, keepdims=True)
    o_ref[0] = x * lax.rsqrt(ms + EPS) * g_ref[...]


def _final_norm(h, g):
    bsz, lt, d = h.shape
    tm = _wide_tile(lt)
    return pl.pallas_call(
        _rmsnorm_kernel,
        grid=(bsz, lt // tm),
        in_specs=[pl.BlockSpec((1, tm, d), lambda b, i: (b, i, 0)),
                  pl.BlockSpec((1, d), lambda b, i: (0, 0))],
        out_specs=pl.BlockSpec((1, tm, d), lambda b, i: (b, i, 0)),
        out_shape=jax.ShapeDtypeStruct((bsz, lt, d), F32),
        compiler_params=_cparams(("parallel", "parallel")),
        name="final_norm",
    )(h, g.reshape(1, d))


def _grid_reorder_kernel(x_ref, o_ref, *, n_ctx, rows, to_cols):
    o_ref[0, :n_ctx, :] = x_ref[0, :n_ctx, :]
    for c in range(GRID_W):
        raster = pl.ds(n_ctx + c, rows, stride=GRID_W)
        dense = pl.ds(n_ctx + c * rows, rows)
        if to_cols:
            o_ref.at[0][dense, :] = x_ref.at[0][raster, :]
        else:
            o_ref.at[0][raster, :] = x_ref.at[0][dense, :]


def _grid_reorder(h, n_ctx, to_cols):
    bsz, lt, d = h.shape
    lanes = 128
    spec = pl.BlockSpec((1, lt, lanes), lambda b, j: (b, 0, j))
    return pl.pallas_call(
        functools.partial(_grid_reorder_kernel, n_ctx=n_ctx, rows=(lt - n_ctx) // GRID_W, to_cols=to_cols),
        grid=(bsz, d // lanes),
        in_specs=[spec],
        out_specs=spec,
        out_shape=jax.ShapeDtypeStruct(h.shape, h.dtype),
        compiler_params=_cparams(("parallel", "parallel")),
        name="grid_reorder",
    )(h)


def _chunk_order(d, c, n_ctx_chunks, n_chunks):
    bwd = jnp.where(c < n_ctx_chunks, n_ctx_chunks - 1 - c, n_chunks + n_ctx_chunks - 1 - c)
    return jnp.where(d == 0, c, bwd)


def _split_bf16(x, n):
    out = []
    for _ in range(n):
        p = x.astype(BF16)
        out.append(p)
        x = x - p.astype(F32)
    return out


def _dir_tri(d, t):
    row = lax.broadcasted_iota(jnp.int32, (t, t), 0)
    col = lax.broadcasted_iota(jnp.int32, (t, t), 1)
    return jnp.where(d == 0, col - row, row - col) <= 0


def _gla_kernel(q_ref, k_ref, v_ref, r_ref, w2_ref, b2_ref, o_ref, st_ref, *, t, heads, scale):
    d = pl.program_id(0)
    c = pl.program_id(2)

    @pl.when(c == 0)
    def _():
        st_ref[...] = jnp.zeros_like(st_ref)

    dk = q_ref.shape[-1] // heads
    dv = v_ref.shape[-1] // heads
    nb = q_ref.shape[0]
    mask = _dir_tri(d, t)
    tri = mask.astype(F32)
    mid = t // 2
    items = [(bb, h) for bb in range(nb) for h in range(heads)]
    w_hi, w_lo = _split_bf16(w2_ref[0], 2)
    xs = []
    for bb in range(nb):
        r_hi, r_lo = _split_bf16(r_ref[bb], 2)
        xs.append(jnp.dot(r_hi, w_hi, preferred_element_type=F32) + jnp.dot(r_hi, w_lo, preferred_element_type=F32)
                  + jnp.dot(r_lo, w_hi, preferred_element_type=F32) + b2_ref[0])
    las = [jax.nn.log_sigmoid(x) * (1.0 / GLA_TAU) for x in xs]
    tri_b = tri.astype(BF16)
    bs = [sum(jnp.dot(tri_b, p, preferred_element_type=F32) for p in _split_bf16(la, 3)) for la in las]
    qt, kt, qe, kh_end, e_end = [], [], [], [], []
    for bb in range(nb):
        b = bs[bb]
        b_m = b[mid:mid + 1, :]
        b_end = jnp.where(d == 0, b[t - 1:t, :], b[0:1, :])
        q_s = q_ref[bb] * (jnp.exp(b - b_m) * scale)
        k_s = k_ref[bb] * jnp.exp(b_m - b)
        qe.append((q_s * jnp.exp(b_m)).astype(BF16))
        kh_end.append((k_s * jnp.exp(b_end - b_m)).astype(BF16))
        e_end.append(jnp.exp(b_end))
        qt.append(q_s.astype(BF16))
        kt.append(k_s.astype(BF16))
    att, q_st, vs = {}, {}, {}
    for bb, h in items:
        ks = slice(h * dk, (h + 1) * dk)
        vs[bb, h] = v_ref[bb, :, h * dv:(h + 1) * dv].astype(BF16)
        att[bb, h] = lax.dot_general(qt[bb][:, ks], kt[bb][:, ks], NT, preferred_element_type=F32)
        q_st[bb, h] = lax.dot_general(qe[bb][:, ks], st_ref[bb * heads + h].astype(BF16), NT,
                                      preferred_element_type=F32)
    for bb, h in items:
        a = jnp.where(mask, att[bb, h], 0.0).astype(BF16)
        o_ref[0, bb, :, h * dv:(h + 1) * dv] = jnp.dot(a, vs[bb, h], preferred_element_type=F32) + q_st[bb, h]
    for bb, h in items:
        ks = slice(h * dk, (h + 1) * dk)
        upd = lax.dot_general(vs[bb, h], kh_end[bb][:, ks], TN, preferred_element_type=F32)
        st_ref[bb * heads + h] = st_ref[bb * heads + h] * e_end[bb][:, ks] + upd


def _gla_mixer(q, k, v, r, w2, b2, n_ctx):
    bsz, lt, dkt = q.shape
    dvt = v.shape[-1]
    nr = r.shape[-1]
    t, heads = MIX_CHUNK, GLA_HEADS
    bt = MIX_BATCH
    nch = lt // t
    ncc = n_ctx // t
    dk = dkt // heads
    dv = dvt // heads
    imap = lambda d, b, c: (b, _chunk_order(d, c, ncc, nch), 0)
    return pl.pallas_call(
        functools.partial(_gla_kernel, t=t, heads=heads, scale=dk ** -0.5),
        grid=(2, bsz // bt, nch),
        in_specs=[
            pl.BlockSpec((bt, t, dkt), imap),
            pl.BlockSpec((bt, t, dkt), imap),
            pl.BlockSpec((bt, t, dvt), imap),
            pl.BlockSpec((bt, t, nr), imap),
            pl.BlockSpec((1, nr, dkt), lambda d, b, c: (d, 0, 0)),
            pl.BlockSpec((1, 1, dkt), lambda d, b, c: (d, 0, 0)),
        ],
        out_specs=pl.BlockSpec((1, bt, t, dvt), lambda d, b, c: (d, b, _chunk_order(d, c, ncc, nch), 0)),
        out_shape=jax.ShapeDtypeStruct((2, bsz, lt, dvt), F32),
        scratch_shapes=[pltpu.VMEM((bt * heads, dv, dk), F32)],
        compiler_params=_cparams(("parallel", "parallel", "arbitrary")),
        name="gla_mixer",
    )(q, k, v, r, w2, b2)


def _mlstm_kernel(q_ref, k_ref, v_ref, gc_ref, gr_ref, o_ref, c_ref, n_ref, m_ref, *, t, heads):
    d = pl.program_id(0)
    c = pl.program_id(2)

    @pl.when(c == 0)
    def _():
        c_ref[...] = jnp.zeros_like(c_ref)
        n_ref[...] = jnp.zeros_like(n_ref)
        m_ref[...] = jnp.zeros_like(m_ref)

    dh = q_ref.shape[-1] // heads
    nb = q_ref.shape[0]
    mask = _dir_tri(d, t)
    tri = mask.astype(F32)
    items = [(bb, h) for bb in range(nb) for h in range(heads)]
    gate = []
    for bb in range(nb):
        gc = gc_ref[0, bb]
        gr = gr_ref[0, bb, 0]
        fc = jax.nn.log_sigmoid(gc[:, heads:])
        fr = jax.nn.log_sigmoid(gr[heads:, :])
        b_col = jnp.dot(tri, fc, preferred_element_type=F32, precision=HI)
        b_row = lax.dot_general(fr, tri, NT, preferred_element_type=F32, precision=HI)
        b_last = jnp.where(d == 0, b_col[t - 1:t, :], b_col[0:1, :])
        gate.append((gc[:, :heads], gr[:heads, :], b_col, b_row, b_last))
    qs, ks, vs, s_raw, q_c = {}, {}, {}, {}, {}
    for bb, h in items:
        hs = slice(h * dh, (h + 1) * dh)
        qs[bb, h] = q_ref[bb, :, hs]
        ks[bb, h] = k_ref[bb, :, hs]
        vs[bb, h] = v_ref[bb, :, hs].astype(BF16)
        s_raw[bb, h] = lax.dot_general(qs[bb, h], ks[bb, h], NT, preferred_element_type=F32)
        q_c[bb, h] = jnp.dot(qs[bb, h], c_ref[bb * heads + h].astype(BF16), preferred_element_type=F32)
    logw, log_inter, m_t, w_inter, scores, den, qn = {}, {}, {}, {}, {}, {}, {}
    for bb, h in items:
        _, ir, b_col, b_row, _ = gate[bb]
        bc = b_col[:, h:h + 1]
        logw[bb, h] = jnp.where(mask, bc - b_row[h:h + 1, :] + ir[h:h + 1, :], -jnp.inf)
        log_inter[bb, h] = bc + m_ref[bb * heads + h]
    for bb, h in items:
        m_t[bb, h] = jnp.maximum(log_inter[bb, h], jnp.max(logw[bb, h], axis=-1, keepdims=True))
        qn[bb, h] = jnp.sum(qs[bb, h].astype(F32) * n_ref[bb * heads + h], axis=-1, keepdims=True)
    for bb, h in items:
        w_inter[bb, h] = jnp.exp(log_inter[bb, h] - m_t[bb, h])
        scores[bb, h] = s_raw[bb, h] * jnp.exp(logw[bb, h] - m_t[bb, h])
    for bb, h in items:
        den[bb, h] = jnp.sum(scores[bb, h], axis=-1, keepdims=True) + w_inter[bb, h] * qn[bb, h]
    num = {}
    for bb, h in items:
        num[bb, h] = (jnp.dot(scores[bb, h].astype(BF16), vs[bb, h], preferred_element_type=F32)
                      + w_inter[bb, h] * q_c[bb, h])
    for bb, h in items:
        hs = slice(h * dh, (h + 1) * dh)
        o_ref[0, bb, :, hs] = num[bb, h] / jnp.maximum(jnp.abs(den[bb, h]), jnp.exp(-m_t[bb, h]))
    log_g, m_new, kw, upd, ksum = {}, {}, {}, {}, {}
    for bb, h in items:
        ic, _, b_col, _, b_last = gate[bb]
        log_g[bb, h] = b_last[:, h:h + 1] - b_col[:, h:h + 1] + ic[:, h:h + 1]
    for bb, h in items:
        b_last = gate[bb][4]
        m_new[bb, h] = jnp.maximum(b_last[:, h:h + 1] + m_ref[bb * heads + h],
                                   jnp.max(log_g[bb, h], axis=0, keepdims=True))
    for bb, h in items:
        kw[bb, h] = ks[bb, h].astype(F32) * jnp.exp(log_g[bb, h] - m_new[bb, h])
    for bb, h in items:
        upd[bb, h] = lax.dot_general(kw[bb, h].astype(BF16), vs[bb, h], TN, preferred_element_type=F32)
        ksum[bb, h] = jnp.sum(kw[bb, h], axis=0, keepdims=True)
    for bb, h in items:
        si = bb * heads + h
        b_last = gate[bb][4]
        keep = jnp.exp(b_last[:, h:h + 1] + m_ref[si] - m_new[bb, h])
        c_ref[si] = keep * c_ref[si] + upd[bb, h]
        n_ref[si] = keep * n_ref[si] + ksum[bb, h]
        m_ref[si] = m_new[bb, h]


def _mlstm_mixer(qk, v, gates, n_ctx):
    bsz, lt, w2 = qk.shape
    w = w2 // 2
    t, heads = MLSTM_CHUNK, MLSTM_HEADS
    dh = w // heads
    nch = lt // t
    ncc = n_ctx // t
    gc = gates.reshape(bsz, lt, 2, 2 * heads).transpose(2, 0, 1, 3)
    gr = gc.reshape(2, bsz, nch, t, 2 * heads).transpose(0, 1, 2, 4, 3)
    cmap = lambda d, b, c: _chunk_order(d, c, ncc, nch)
    bt = MIX_BATCH
    return pl.pallas_call(
        functools.partial(_mlstm_kernel, t=t, heads=heads),
        grid=(2, bsz // bt, nch),
        in_specs=[
            pl.BlockSpec((bt, t, w), lambda d, b, c: (b, cmap(d, b, c), 0)),
            pl.BlockSpec((bt, t, w), lambda d, b, c: (b, cmap(d, b, c), 1)),
            pl.BlockSpec((bt, t, w), lambda d, b, c: (b, cmap(d, b, c), 0)),
            pl.BlockSpec((1, bt, t, 2 * heads), lambda d, b, c: (d, b, cmap(d, b, c), 0)),
            pl.BlockSpec((1, bt, 1, 2 * heads, t), lambda d, b, c: (d, b, cmap(d, b, c), 0, 0)),
        ],
        out_specs=pl.BlockSpec((1, bt, t, w), lambda d, b, c: (d, b, cmap(d, b, c), 0)),
        out_shape=jax.ShapeDtypeStruct((2, bsz, lt, w), F32),
        scratch_shapes=[pltpu.VMEM((bt * heads, dh, dh), F32), pltpu.VMEM((bt * heads, 1, dh), F32),
                        pltpu.VMEM((bt * heads, 1, 1), F32)],
        compiler_params=_cparams(("parallel", "parallel", "arbitrary")),
        name="mlstm_mixer",
    )(qk, qk, v, gc, gr)


def _conv_kernel(x_ref, w_ref, b_ref, s_ref, o_ref, *, n_ctx):
    x = x_ref[0]
    lt = x.shape[0]
    row = lax.broadcasted_iota(jnp.int32, x.shape, 0)
    prev = jnp.where((row == 0) | (row == n_ctx), 0.0, pltpu.roll(x, 1, 0))
    nxt = jnp.where((row == n_ctx - 1) | (row == lt - 1), 0.0, pltpu.roll(x, lt - 1, 0))
    y = b_ref[...] + w_ref[0:1, :] * prev + w_ref[1:2, :] * x + w_ref[2:3, :] * nxt
    o_ref[0] = (y * jax.nn.sigmoid(y) * s_ref[...]).astype(o_ref.dtype)


def _conv_silu(x, w, b, colscale, n_ctx):
    bsz, lt, ch = x.shape
    tc = 256
    return pl.pallas_call(
        functools.partial(_conv_kernel, n_ctx=n_ctx),
        grid=(bsz, ch // tc),
        in_specs=[
            pl.BlockSpec((1, lt, tc), lambda b, j: (b, 0, j)),
            pl.BlockSpec((3, tc), lambda b, j: (0, j)),
            pl.BlockSpec((1, tc), lambda b, j: (0, j)),
            pl.BlockSpec((1, tc), lambda b, j: (0, j)),
        ],
        out_specs=pl.BlockSpec((1, lt, tc), lambda b, j: (b, 0, j)),
        out_shape=jax.ShapeDtypeStruct((bsz, lt, ch), BF16),
        compiler_params=_cparams(("parallel", "parallel")),
        name="conv_silu",
    )(x, w, b.reshape(1, ch), colscale.reshape(1, ch))


def _s5_matrices(a_re, a_im, log_dt, b_re, b_im, c_re, c_im, backward, lane_groups=8):
    g, p = a_re.shape
    cg = b_re.shape[-1]
    j = S5_J
    lg = lane_groups
    nq = g // lg
    dt = jnp.exp(log_dt)[:, None]
    lam_re = jnp.minimum(a_re, -1e-4)
    lam_im = a_im
    decay = jnp.exp(lam_re * dt)
    ab_re = decay * jnp.cos(lam_im * dt)
    ab_im = decay * jnp.sin(lam_im * dt)
    den = lam_re * lam_re + lam_im * lam_im
    zr = ((ab_re - 1) * lam_re + ab_im * lam_im) / den
    zi = (ab_im * lam_re - (ab_re - 1) * lam_im) / den
    bb_re = zr[..., None] * b_re - zi[..., None] * b_im
    bb_im = zr[..., None] * b_im + zi[..., None] * b_re
    pw_re, pw_im = [jnp.ones_like(ab_re)], [jnp.zeros_like(ab_im)]
    for _ in range(j):
        r0, i0 = pw_re[-1], pw_im[-1]
        pw_re.append(ab_re * r0 - ab_im * i0)
        pw_im.append(ab_re * i0 + ab_im * r0)
    pw_re, pw_im = jnp.stack(pw_re), jnp.stack(pw_im)
    ca_re = c_re[None] * pw_re[:, :, None, :] - c_im[None] * pw_im[:, :, None, :]
    ca_im = c_re[None] * pw_im[:, :, None, :] + c_im[None] * pw_re[:, :, None, :]
    kk = (jnp.einsum('tgcp,gpd->tgcd', ca_re[:j], bb_re, precision=HI)
          - jnp.einsum('tgcp,gpd->tgcd', ca_im[:j], bb_im, precision=HI))
    ab_pw_re = pw_re[:j, :, :, None] * bb_re[None] - pw_im[:j, :, :, None] * bb_im[None]
    ab_pw_im = pw_re[:j, :, :, None] * bb_im[None] + pw_im[:j, :, :, None] * bb_re[None]
    eye = jnp.eye(lg, dtype=BF16)
    lb = lg * cg
    sw = lg * p
    bd_k = jnp.einsum('tqgcd,gh->tqgdhc', kk.astype(BF16).reshape(j, nq, lg, cg, cg), eye
                      ).reshape(j, nq, lb, lb)
    bd_in = [jnp.einsum('tqgpc,gh->tqgchp', a.astype(BF16).reshape(j, nq, lg, p, cg), eye
                        ).reshape(j, nq, lb, sw) for a in (ab_pw_re, ab_pw_im)]
    bd_out = [jnp.einsum('tqgcp,gh->tqgphc', a.astype(BF16).reshape(j + 1, nq, lg, cg, p), eye
                         ).reshape(j + 1, nq, sw, lb) for a in (ca_re, -ca_im)]
    jj = jnp.arange(j)
    lag = (jj[:, None] - jj[None, :]) if backward else (jj[None, :] - jj[:, None])
    kt = jnp.where((lag >= 0)[:, :, None, None, None], bd_k[jnp.clip(lag, 0, j - 1)], 0)
    ktoep = kt.transpose(2, 0, 3, 1, 4).reshape(nq, j * lb, j * lb)
    tau_in = jj if backward else (j - 1 - jj)
    win_re, win_im = (a[tau_in].transpose(1, 0, 2, 3).reshape(nq, j * lb, sw) for a in bd_in)
    tau_out = (j - jj) if backward else (jj + 1)
    wout_re, wout_im = (a[tau_out].transpose(1, 2, 0, 3).reshape(nq, sw, j * lb) for a in bd_out)
    dec_re = pw_re[j].reshape(nq, 1, sw)
    dec_im = pw_im[j].reshape(nq, 1, sw)
    return ktoep, win_re, win_im, wout_re, wout_im, dec_re, dec_im


def _s5_kernel(u_ref, kt_ref, wir_ref, wii_ref, wor_ref, woi_ref, dr_ref, di_ref, y_ref,
               xf_ref, yf_ref, sre_ref, sim_ref, *, bt, nk, nk_ctx, rs):
    d = pl.program_id(0)
    j = S5_J
    lanes = u_ref.shape[-1]
    for b in range(bt):
        for jj in range(j):
            xf_ref[b * nk:(b + 1) * nk, jj * lanes:(jj + 1) * lanes] = (
                u_ref.at[b][pl.ds(jj, nk, stride=j), :].astype(BF16))
    xf = xf_ref[...]
    yf_ref[...] = jnp.dot(xf, kt_ref[0, 0], preferred_element_type=F32)
    inc_re = jnp.dot(xf, wir_ref[0, 0], preferred_element_type=F32)
    inc_im = jnp.dot(xf, wii_ref[0, 0], preferred_element_type=F32)
    nl = sre_ref.shape[0]
    for b in range(bt):
        for l in range(nl):
            sre_ref[l, b * rs:b * rs + nk, :] = inc_re[b * nk:(b + 1) * nk, l * lanes:(l + 1) * lanes]
            sim_ref[l, b * rs:b * rs + nk, :] = inc_im[b * nk:(b + 1) * nk, l * lanes:(l + 1) * lanes]
    a_re = [dr_ref[0, 0, :, l * lanes:(l + 1) * lanes] for l in range(nl)]
    a_im = [di_ref[0, 0, :, l * lanes:(l + 1) * lanes] for l in range(nl)]

    def step(kidx, carry):
        rows = pl.ds(kidx, bt, stride=rs)
        out = []
        for l in range(nl):
            s_re, s_im = carry[2 * l], carry[2 * l + 1]
            i_re = sre_ref.at[l][rows, :]
            i_im = sim_ref.at[l][rows, :]
            sre_ref.at[l][rows, :] = s_re
            sim_ref.at[l][rows, :] = s_im
            out.append(a_re[l] * s_re - a_im[l] * s_im + i_re)
            out.append(a_re[l] * s_im + a_im[l] * s_re + i_im)
        return tuple(out)

    zero = tuple(jnp.zeros((bt, lanes), F32) for _ in range(2 * nl))

    @pl.when(d == 0)
    def _():
        lax.fori_loop(0, nk, step, zero)

    @pl.when(d == 1)
    def _():
        carry = lax.fori_loop(0, nk_ctx, lambda i, cr: step(nk_ctx - 1 - i, cr), zero)
        lax.fori_loop(0, nk - nk_ctx, lambda i, cr: step(nk - 1 - i, cr), carry)

    for b in range(bt):
        sp_re = jnp.concatenate([sre_ref[l, b * rs:b * rs + nk, :] for l in range(nl)], axis=-1).astype(BF16)
        sp_im = jnp.concatenate([sim_ref[l, b * rs:b * rs + nk, :] for l in range(nl)], axis=-1).astype(BF16)
        yb = (yf_ref[b * nk:(b + 1) * nk, :]
              + jnp.dot(sp_re, wor_ref[0, 0], preferred_element_type=F32)
              + jnp.dot(sp_im, woi_ref[0, 0], preferred_element_type=F32))
        for jj in range(j):
            y_ref.at[0, b][pl.ds(jj, nk, stride=j), :] = yb[:, jj * lanes:(jj + 1) * lanes]


def _s5_mixer(u, mats, n_ctx):
    bsz, lt, w = u.shape
    ktoep, win_re, win_im, wout_re, wout_im, dec_re, dec_im = mats
    lanes = 128
    bt = 4 if bsz % 4 == 0 else 2
    nq = w // lanes
    j = S5_J
    nk = lt // j
    nk_ctx = n_ctx // j
    rs = nk + 8
    fl = j * lanes
    sw = win_re.shape[-1]
    wmap = lambda d, q, b: (d, q, 0, 0)
    return pl.pallas_call(
        functools.partial(_s5_kernel, bt=bt, nk=nk, nk_ctx=nk_ctx, rs=rs),
        grid=(2, nq, bsz // bt),
        in_specs=[
            pl.BlockSpec((bt, lt, lanes), lambda d, q, b: (b, 0, q)),
            pl.BlockSpec((1, 1, fl, fl), wmap),
            pl.BlockSpec((1, 1, fl, sw), wmap),
            pl.BlockSpec((1, 1, fl, sw), wmap),
            pl.BlockSpec((1, 1, sw, fl), wmap),
            pl.BlockSpec((1, 1, sw, fl), wmap),
            pl.BlockSpec((1, 1, 1, sw), wmap),
            pl.BlockSpec((1, 1, 1, sw), wmap),
        ],
        out_specs=pl.BlockSpec((1, bt, lt, lanes), lambda d, q, b: (d, b, 0, q)),
        out_shape=jax.ShapeDtypeStruct((2, bsz, lt, w), F32),
        scratch_shapes=[pltpu.VMEM((bt * nk, fl), BF16), pltpu.VMEM((bt * nk, fl), F32),
                        pltpu.VMEM((sw // lanes, bt * rs, lanes), F32),
                        pltpu.VMEM((sw // lanes, bt * rs, lanes), F32)],
        compiler_params=_cparams(("parallel", "parallel", "arbitrary")),
        name="s5_mixer",
    )(u, ktoep, win_re, win_im, wout_re, wout_im, dec_re, dec_im)


def _head_norm(x, heads):
    dh = x.shape[-1] // heads
    outs = []
    for h in range(heads):
        xh = x[:, h * dh:(h + 1) * dh]
        outs.append(xh * lax.rsqrt(jnp.mean(xh * xh, axis=-1, keepdims=True) + EPS))
    return jnp.concatenate(outs, axis=-1)


def _even_post_kernel(m_ref, o_ref, s_ref, u_ref, mg_ref, dsk_ref, gw_ref, gb_ref, w_ref, h_ref, gate_ref,
                      out_ref, *, heads):
    m = m_ref[0, 0] + m_ref[1, 0]
    m_out = _head_norm(m, heads) * mg_ref[...] * jax.nn.sigmoid(o_ref[0])
    y = jax.nn.gelu(s_ref[0, 0] + s_ref[1, 0] + dsk_ref[...] * u_ref[0])
    glu = jnp.dot(y.astype(BF16), gw_ref[...], preferred_element_type=F32) + gb_ref[...]
    s_out = y * jax.nn.sigmoid(glu)
    cat = jnp.concatenate([m_out, s_out], axis=-1).astype(BF16)
    z = jnp.dot(cat, w_ref[...], preferred_element_type=F32)
    out_ref[0] = h_ref[0] + gate_ref[0, 0] * z


def _even_post(m2, o, s2, u, mnorm_g, d_skip, glu_w, glu_b, w_out, h, gate):
    bsz, lt, d = h.shape
    mw = o.shape[-1]
    sw = u.shape[-1]
    tm = ROW_TILE
    row = lambda b, i: (b, i, 0)
    row2 = lambda b, i: (0, b, i, 0)
    const = lambda b, i: (0, 0)
    return pl.pallas_call(
        functools.partial(_even_post_kernel, heads=MLSTM_HEADS),
        grid=(bsz, lt // tm),
        in_specs=[
            pl.BlockSpec((2, 1, tm, mw), row2),
            pl.BlockSpec((1, tm, mw), row),
            pl.BlockSpec((2, 1, tm, sw), row2),
            pl.BlockSpec((1, tm, sw), row),
            pl.BlockSpec((1, mw), const),
            pl.BlockSpec((1, sw), const),
            pl.BlockSpec((sw, sw), const),
            pl.BlockSpec((1, sw), const),
            pl.BlockSpec((mw + sw, d), const),
            pl.BlockSpec((1, tm, d), row),
            pl.BlockSpec((1, 1, 1, d), _seg_map),
        ],
        out_specs=pl.BlockSpec((1, tm, d), row),
        out_shape=jax.ShapeDtypeStruct((bsz, lt, d), F32),
        compiler_params=_cparams(("parallel", "parallel")),
        name="even_post",
    )(m2, o, s2, u, mnorm_g.reshape(1, mw), d_skip.reshape(1, sw), glu_w.astype(BF16), glu_b.reshape(1, sw),
      w_out.astype(BF16), h, gate)


def _odd_post_kernel(o_ref, g_ref, ng_ref, w_ref, h_ref, gate_ref, out_ref, *, heads):
    g = g_ref[0]
    y = _head_norm(o_ref[0, 0] + o_ref[1, 0], heads) * ng_ref[...] * (g * jax.nn.sigmoid(g))
    z = jnp.dot(y.astype(BF16), w_ref[...], preferred_element_type=F32)
    out_ref[0] = h_ref[0] + gate_ref[0, 0] * z


def _odd_post(o, g, norm_g, w_out, h, gate):
    bsz, lt, d = h.shape
    dv = o.shape[-1]
    tm = ROW_TILE
    row = lambda b, i: (b, i, 0)
    const = lambda b, i: (0, 0)
    return pl.pallas_call(
        functools.partial(_odd_post_kernel, heads=GLA_HEADS),
        grid=(bsz, lt // tm),
        in_specs=[
            pl.BlockSpec((2, 1, tm, dv), lambda b, i: (0, b, i, 0)),
            pl.BlockSpec((1, tm, dv), row),
            pl.BlockSpec((1, dv), const),
            pl.BlockSpec((dv, d), const),
            pl.BlockSpec((1, tm, d), row),
            pl.BlockSpec((1, 1, 1, d), _seg_map),
        ],
        out_specs=pl.BlockSpec((1, tm, d), row),
        out_shape=jax.ShapeDtypeStruct((bsz, lt, d), F32),
        compiler_params=_cparams(("parallel", "parallel")),
        name="odd_post",
    )(o, g, norm_g.reshape(1, dv), w_out.astype(BF16), h, gate)


def kernel(x, c, ctx, c_ctx, mod_w, mod_b, norm_mix_g, norm_ffn_g, ev_w_in, ev_b_in, ev_conv_w, ev_conv_b, ev_mlstm_norm_g, ev_s5_a_re_f, ev_s5_a_im_f, ev_s5_log_dt_f, ev_s5_a_re_b, ev_s5_a_im_b, ev_s5_log_dt_b, ev_s5_b_re, ev_s5_b_im, ev_s5_c_re, ev_s5_c_im, ev_s5_d, ev_s5_glu_w, ev_s5_glu_b, ev_w_out, od_w_in, od_gate_w2_f, od_gate_b2_f, od_gate_w2_b, od_gate_b2_b, od_norm_g, od_w_out, router_w, router_b, moe_w_gu, moe_b_gu, moe_w_down, moe_b_down, final_norm_g):
    bsz, seq, d = x.shape
    n_ctx = ctx.shape[1]
    depth = mod_w.shape[0]
    lt = n_ctx + seq
    assert n_ctx == ROW_TILE and seq % ROW_TILE == 0 and seq % GRID_W == 0

    h = jnp.concatenate([ctx, x], axis=1)
    c_all = jnp.concatenate([c, c_ctx[None, :]], axis=0)
    c_all = jnp.pad(c_all, ((0, (-c_all.shape[0]) % 8), (0, 0)))
    mods = _modulation(c_all, mod_w, mod_b)
    mod_lat = mods[:, :bsz]
    mod_ctx = jnp.broadcast_to(mods[:, bsz:bsz + 1], mod_lat.shape)
    mod6 = jnp.stack([mod_ctx, mod_lat], axis=2).reshape(depth, bsz, 2, 6, 1, d)

    bg_all = moe_b_gu[..., 0::2]
    bu_all = moe_b_gu[..., 1::2]

    mw = ev_conv_w.shape[-1] // 2
    n_gates = 4 * MLSTM_HEADS
    s5w = ev_s5_d.shape[-1]
    dk_t = od_gate_w2_f.shape[-1]
    dv_t = od_norm_g.shape[-1]
    for layer in range(depth):
        last = layer == depth - 1
        j = layer // 2
        m6 = mod6[layer]
        sh1, sc1, g1, sh2, sc2, g2 = (m6[:, :, i] for i in range(6))
        if layer % 2 == 0:
            w_in, b_in = ev_w_in[j], ev_b_in[j]
            cols = jnp.concatenate([jnp.arange(0, 4 * mw), jnp.arange(4 * mw + n_gates, 4 * mw + n_gates + s5w),
                                    jnp.arange(4 * mw, 4 * mw + n_gates)])
            qk_pre, v, o, u, gates = _nm_matmul(h, norm_mix_g[layer], sh1, sc1, w_in[:, cols], b_in[cols],
                                                (2 * mw, mw, mw, s5w, n_gates))
            dh = mw // MLSTM_HEADS
            colscale = jnp.concatenate([jnp.full((mw,), dh ** -0.5, F32), jnp.ones((mw,), F32)])
            qk = _conv_silu(qk_pre, ev_conv_w[j], ev_conv_b[j], colscale, n_ctx)
            m2 = _mlstm_mixer(qk, v, gates, n_ctx)
            shared = (ev_s5_b_re[j], ev_s5_b_im[j], ev_s5_c_re[j], ev_s5_c_im[j])
            mats_f = _s5_matrices(ev_s5_a_re_f[j], ev_s5_a_im_f[j], ev_s5_log_dt_f[j], *shared, backward=False)
            mats_b = _s5_matrices(ev_s5_a_re_b[j], ev_s5_a_im_b[j], ev_s5_log_dt_b[j], *shared, backward=True)
            s2 = _s5_mixer(u, tuple(jnp.stack([a, b]) for a, b in zip(mats_f, mats_b)), n_ctx)
            h = _even_post(m2, o, s2, u, ev_mlstm_norm_g[j], ev_s5_d[j], ev_s5_glu_w[j], ev_s5_glu_b[j],
                           ev_w_out[j], h, g1)
        else:
            hc = _grid_reorder(h, n_ctx, True)
            qq, kk, vv, gg, rr = _nm_matmul(hc, norm_mix_g[layer], sh1, sc1, od_w_in[j],
                                            jnp.zeros((od_w_in.shape[-1],), F32),
                                            (dk_t, dk_t, dv_t, dv_t, 2 * GLA_RANK))
            zero = jnp.zeros_like(od_gate_w2_f[j])
            w2 = jnp.stack([jnp.concatenate([od_gate_w2_f[j], zero], axis=0),
                            jnp.concatenate([zero, od_gate_w2_b[j]], axis=0)])
            b2 = jnp.stack([od_gate_b2_f[j], od_gate_b2_b[j]])[:, None, :]
            o2 = _gla_mixer(qq, kk, vv, rr, w2, b2, n_ctx)
            h = _grid_reorder(_odd_post(o2, gg, od_norm_g[j], od_w_out[j], hc, g1), n_ctx, False)
        f, top_e, gate = _ffn_prep(h, norm_ffn_g[layer], sh2, sc2, router_w[layer], router_b[layer])
        weights = (layer, moe_w_gu, moe_w_down, bg_all[layer], bu_all[layer], moe_b_down[layer])
        h = _moe_layer(h, f, top_e, gate, g2, weights, last, n_ctx)
    return _final_norm(h, final_norm_g)
```

```python
import functools

import jax
import jax.numpy as jnp
from jax import lax
from jax.experimental import pallas as pl
from jax.experimental.pallas import tpu as pltpu

F32 = jnp.float32
BF16 = jnp.bfloat16
HI = lax.Precision.HIGHEST

EPS = 1e-6
GRID_W = 64
MLSTM_HEADS = 4
S5_GROUP = 16
GLA_HEADS = 4
GLA_RANK = 16
GLA_TAU = 16.0
N_EXPERTS = 32
TOP_K = 4
SWIGLU_LIMIT = 7.0
SWIGLU_ALPHA = 1.702

ROW_TILE = 256
MOE_TILE = 512
MIX_CHUNK = 64
MLSTM_CHUNK = 128
MIX_BATCH = 4
S5_J = 8
VMEM_LIMIT = 56 * 1024 * 1024

NT = (((1,), (1,)), ((), ()))
TN = (((0,), (0,)), ((), ()))


def _cparams(sem):
    return pltpu.CompilerParams(dimension_semantics=sem, vmem_limit_bytes=VMEM_LIMIT)


def _mod_kernel(c_ref, w_ref, b_ref, o_ref):
    c = c_ref[...]
    a = c * jax.nn.sigmoid(c)
    o_ref[0] = jnp.dot(a.astype(BF16), w_ref[0].astype(BF16), preferred_element_type=F32) + b_ref[0]


def _modulation(c_all, mod_w, mod_b):
    depth, d, n6 = mod_w.shape
    rows = c_all.shape[0]
    tn = d
    return pl.pallas_call(
        _mod_kernel,
        grid=(depth, n6 // tn),
        in_specs=[
            pl.BlockSpec((rows, d), lambda l, j: (0, 0)),
            pl.BlockSpec((1, d, tn), lambda l, j: (l, 0, j)),
            pl.BlockSpec((1, 1, tn), lambda l, j: (l, 0, j)),
        ],
        out_specs=pl.BlockSpec((1, rows, tn), lambda l, j: (l, 0, j)),
        out_shape=jax.ShapeDtypeStruct((depth, rows, n6), F32),
        compiler_params=_cparams(("arbitrary", "arbitrary")),
        name="modulation",
    )(c_all, mod_w, mod_b.reshape(depth, 1, n6))


def _norm_mod(x, g, sh, sc):
    ms = jnp.mean(x * x, axis=-1, keepdims=True)
    return (x * lax.rsqrt(ms + EPS) * g) * (1.0 + sc) + sh


def _nm_matmul_kernel(x_ref, g_ref, sh_ref, sc_ref, w_ref, b_ref, *out_refs, splits):
    a = _norm_mod(x_ref[0], g_ref[...], sh_ref[0, 0], sc_ref[0, 0])
    z = jnp.dot(a.astype(BF16), w_ref[...], preferred_element_type=F32) + b_ref[...]
    for (lo, hi), o_ref in zip(splits, out_refs):
        o_ref[0] = z[:, lo:hi].astype(o_ref.dtype)


def _wide_tile(rows):
    return next(t for t in (4 * ROW_TILE, 3 * ROW_TILE, 2 * ROW_TILE, ROW_TILE) if rows % t == 0)


def _seg_map(b, i):
    return (b, jnp.minimum(i, 1), 0, 0)


def _nm_matmul(h, g, shift, scale, w, bias, widths):
    bsz, lt, d = h.shape
    p = w.shape[1]
    splits, lo = [], 0
    for wd in widths:
        splits.append((lo, lo + wd))
        lo += wd
    assert lo == p
    tm = ROW_TILE
    return pl.pallas_call(
        functools.partial(_nm_matmul_kernel, splits=tuple(splits)),
        grid=(bsz, lt // tm),
        in_specs=[
            pl.BlockSpec((1, tm, d), lambda b, i: (b, i, 0)),
            pl.BlockSpec((1, d), lambda b, i: (0, 0)),
            pl.BlockSpec((1, 1, 1, d), _seg_map),
            pl.BlockSpec((1, 1, 1, d), _seg_map),
            pl.BlockSpec((d, p), lambda b, i: (0, 0)),
            pl.BlockSpec((1, p), lambda b, i: (0, 0)),
        ],
        out_specs=[pl.BlockSpec((1, tm, wd), lambda b, i: (b, i, 0)) for wd in widths],
        out_shape=[jax.ShapeDtypeStruct((bsz, lt, wd), F32) for wd in widths],
        compiler_params=_cparams(("parallel", "parallel")),
        name="norm_mod_matmul",
    )(h, g.reshape(1, d), shift, scale, w.astype(BF16), bias.reshape(1, p))


def _ffn_prep_kernel(x_ref, g_ref, sh_ref, sc_ref, rw_ref, rb_ref, f_ref, te_ref, gt_ref, *, n_ctx):
    tm = x_ref.shape[1]
    is_ctx = pl.program_id(1) * tm + lax.broadcasted_iota(jnp.int32, (tm, 1), 0) < n_ctx
    a = _norm_mod(x_ref[0], g_ref[...], jnp.where(is_ctx, sh_ref[0, 0], sh_ref[0, 1]),
                  jnp.where(is_ctx, sc_ref[0, 0], sc_ref[0, 1]))
    f_ref[0] = a.astype(f_ref.dtype)
    logits = lax.dot_general(rw_ref[...], a.astype(BF16), NT, preferred_element_type=F32) + rb_ref[...]
    ne = logits.shape[0]
    eidx = lax.broadcasted_iota(jnp.int32, logits.shape, 0)
    work = logits
    vals, idxs = [], []
    for _ in range(TOP_K):
        m = jnp.max(work, axis=0, keepdims=True)
        idx = jnp.min(jnp.where(work == m, eidx, ne), axis=0, keepdims=True)
        vals.append(m)
        idxs.append(idx)
        work = jnp.where(eidx == idx, -jnp.inf, work)
    exps = [jnp.exp(v - vals[0]) for v in vals]
    denom = exps[0]
    for e in exps[1:]:
        denom = denom + e
    for k in range(TOP_K):
        te_ref[0, k:k + 1, :] = idxs[k]
        gt_ref[0, k:k + 1, :] = exps[k] / denom


def _ffn_prep(h, g, shift, scale, router_w, router_b):
    bsz, lt, d = h.shape
    ne = router_w.shape[1]
    tm = _wide_tile(lt)
    both = pl.BlockSpec((1, 2, 1, d), lambda b, i: (b, 0, 0, 0))
    return pl.pallas_call(
        functools.partial(_ffn_prep_kernel, n_ctx=ROW_TILE),
        grid=(bsz, lt // tm),
        in_specs=[
            pl.BlockSpec((1, tm, d), lambda b, i: (b, i, 0)),
            pl.BlockSpec((1, d), lambda b, i: (0, 0)),
            both,
            both,
            pl.BlockSpec((ne, d), lambda b, i: (0, 0)),
            pl.BlockSpec((ne, 1), lambda b, i: (0, 0)),
        ],
        out_specs=[
            pl.BlockSpec((1, tm, d), lambda b, i: (b, i, 0)),
            pl.BlockSpec((1, TOP_K, tm), lambda b, i: (b, 0, i)),
            pl.BlockSpec((1, TOP_K, tm), lambda b, i: (b, 0, i)),
        ],
        out_shape=[
            jax.ShapeDtypeStruct((bsz, lt, d), BF16),
            jax.ShapeDtypeStruct((bsz, TOP_K, lt), jnp.int32),
            jax.ShapeDtypeStruct((bsz, TOP_K, lt), F32),
        ],
        compiler_params=_cparams(("parallel", "parallel")),
        name="ffn_prep",
    )(h, g.reshape(1, d), shift, scale, router_w.T.astype(BF16), router_b.reshape(ne, 1))


GU_BLOCK = 256


def _moe_kernel(be_ref, nb_ref, first_ref, slot_ref, nxt_ref, x_ref, wgu_hbm, wd_hbm, bg_ref, bu_ref, bd_ref,
                *rest, layer, has_prev):
    o_ref, wgu_buf, wd_buf, wgu_s, wd_s, sem = rest[1:] if has_prev else rest
    i = pl.program_id(0)
    active = i < nb_ref[0]
    half = GU_BLOCK // 2
    nblk = wgu_s.shape[1] // GU_BLOCK

    def weight_copies(e, slot):
        return (pltpu.make_async_copy(wgu_hbm.at[layer, e], wgu_buf.at[slot], sem.at[0, slot]),
                pltpu.make_async_copy(wd_hbm.at[layer, e], wd_buf.at[slot], sem.at[1, slot]))

    @pl.when(active & (i == 0))
    def _():
        for cp in weight_copies(be_ref[0], 0):
            cp.start()

    @pl.when(active & (first_ref[i] == 1))
    def _():
        slot = slot_ref[i]
        for cp in weight_copies(be_ref[i], slot):
            cp.wait()

        @pl.when(nxt_ref[i] >= 0)
        def _():
            for cp in weight_copies(nxt_ref[i], 1 - slot):
                cp.start()

        r = lax.broadcasted_iota(jnp.int32, (GU_BLOCK, GU_BLOCK), 0)
        c = lax.broadcasted_iota(jnp.int32, (GU_BLOCK, GU_BLOCK), 1)
        perm = (r == jnp.where(c < half, 2 * c, 2 * (c - half) + 1)).astype(BF16)
        for k in range(nblk):
            cs = slice(k * GU_BLOCK, (k + 1) * GU_BLOCK)
            wgu_s[:, cs] = jnp.dot(wgu_buf[slot, :, cs].astype(BF16), perm,
                                   preferred_element_type=F32).astype(BF16)
        wd_s[...] = wd_buf[slot].astype(BF16)

    @pl.when(active)
    def _():
        gu = jnp.dot(x_ref[...], wgu_s[...], preferred_element_type=F32)
        hdn = []
        for k in range(nblk):
            hs = slice(k * half, (k + 1) * half)
            g = gu[:, k * GU_BLOCK:k * GU_BLOCK + half] + bg_ref[0, :, hs]
            u = gu[:, k * GU_BLOCK + half:(k + 1) * GU_BLOCK] + bu_ref[0, :, hs]
            g = jnp.minimum(g, SWIGLU_LIMIT)
            u = jnp.clip(u, -SWIGLU_LIMIT, SWIGLU_LIMIT)
            hdn.append(((u + 1.0) * (g * jax.nn.sigmoid(SWIGLU_ALPHA * g))).astype(BF16))
        hdn = jnp.concatenate(hdn, axis=-1)
        o_ref[...] = (jnp.dot(hdn, wd_s[...], preferred_element_type=F32) + bd_ref[0]).astype(o_ref.dtype)

    @pl.when(jnp.logical_not(active))
    def _():
        o_ref[...] = jnp.zeros_like(o_ref)


def _moe_experts(x_sorted, block_expert, n_used, layer, w_gu, w_down, bg, bu, bd, out_rows, block_off, y_prev):
    n_rows, d = x_sorted.shape
    _, ne, _, f2 = w_gu.shape
    f = f2 // 2
    tm = MOE_TILE
    n_blocks = n_rows // tm
    assert f2 % GU_BLOCK == 0
    blk = jnp.arange(n_blocks, dtype=jnp.int32)
    prev = jnp.concatenate([block_expert[:1], block_expert[:-1]])
    first = (blk < n_used[0]) & ((blk == 0) | (block_expert != prev))
    slot = (jnp.cumsum(first.astype(jnp.int32)) - 1) & 1
    first_idx = jnp.where(first, blk, n_blocks)
    next_first = lax.cummin(first_idx, axis=0, reverse=True)
    next_first = jnp.concatenate([next_first[1:], jnp.full((1,), n_blocks, jnp.int32)])
    nxt = jnp.where(next_first < n_blocks, block_expert[jnp.minimum(next_first, n_blocks - 1)], -1)
    bmap = lambda i, be, nb, fi, sl, nx: (be[i], 0, 0)
    rmap = lambda i, be, nb, fi, sl, nx: (i, 0)
    has_prev = y_prev is not None
    n_prefetch = 5
    grid_spec = pltpu.PrefetchScalarGridSpec(
        num_scalar_prefetch=n_prefetch,
        grid=(n_blocks,),
        in_specs=[
            pl.BlockSpec((tm, d), rmap),
            pl.BlockSpec(memory_space=pl.ANY),
            pl.BlockSpec(memory_space=pl.ANY),
            pl.BlockSpec((1, 1, f), bmap),
            pl.BlockSpec((1, 1, f), bmap),
            pl.BlockSpec((1, 1, d), bmap),
        ] + ([pl.BlockSpec(memory_space=pl.ANY)] if has_prev else []),
        out_specs=pl.BlockSpec((tm, d), lambda i, be, nb, fi, sl, nx: (i + block_off, 0)),
        scratch_shapes=[pltpu.VMEM((2, d, f2), F32), pltpu.VMEM((2, f, d), F32),
                        pltpu.VMEM((d, f2), BF16), pltpu.VMEM((f, d), BF16),
                        pltpu.SemaphoreType.DMA((2, 2))],
    )
    operands = (block_expert, n_used, first.astype(jnp.int32), slot.astype(jnp.int32), nxt.astype(jnp.int32),
                x_sorted, w_gu, w_down, bg.reshape(ne, 1, f), bu.reshape(ne, 1, f), bd.reshape(ne, 1, d))
    return pl.pallas_call(
        functools.partial(_moe_kernel, layer=layer, has_prev=has_prev),
        grid_spec=grid_spec,
        out_shape=jax.ShapeDtypeStruct((out_rows, d), BF16),
        input_output_aliases={len(operands): 0} if has_prev else {},
        compiler_params=_cparams(("arbitrary",)),
        name="moe_experts",
    )(*operands, *((y_prev,) if has_prev else ()))


def _combine_kernel(y_ref, gt_ref, h_ref, g2_ref, o_ref, *, n_ctx):
    gt = gt_ref[0]
    acc = y_ref[0, 0].astype(F32) * gt[:, 0:1]
    for k in range(1, TOP_K):
        acc = acc + y_ref[k, 0].astype(F32) * gt[:, k:k + 1]
    tm = acc.shape[0]
    is_ctx = pl.program_id(1) * tm + lax.broadcasted_iota(jnp.int32, (tm, 1), 0) < n_ctx
    o_ref[0] = h_ref[0] + jnp.where(is_ctx, g2_ref[0, 0], g2_ref[0, 1]) * acc


def _moe_combine(yg, gate, h, g2, lat_only):
    k, bsz, lt, d = yg.shape
    tm = ROW_TILE if lat_only else _wide_tile(lt)
    tile_off = 1 if lat_only else 0
    return pl.pallas_call(
        functools.partial(_combine_kernel, n_ctx=0 if lat_only else ROW_TILE),
        grid=(bsz, lt // tm),
        in_specs=[
            pl.BlockSpec((k, 1, tm, d), lambda b, i: (0, b, i, 0)),
            pl.BlockSpec((1, tm, k), lambda b, i: (b, i, 0)),
            pl.BlockSpec((1, tm, d), lambda b, i: (b, i + tile_off, 0)),
            pl.BlockSpec((1, 2, 1, d), lambda b, i: (b, 0, 0, 0)),
        ],
        out_specs=pl.BlockSpec((1, tm, d), lambda b, i: (b, i, 0)),
        out_shape=jax.ShapeDtypeStruct((bsz, lt, d), F32),
        compiler_params=_cparams(("parallel", "parallel")),
        name="moe_combine",
    )(yg, gate, h, g2)


def _route_kernel(e_ref, pos_ref, cnt_ref, *, tm):
    nk, rows, lanes = e_ref.shape
    li = lax.broadcasted_iota(jnp.int32, (lanes, lanes), 0)
    lj = lax.broadcasted_iota(jnp.int32, (lanes, lanes), 1)
    before_lane = (li < lj).astype(BF16)
    ones = jnp.ones((lanes, lanes), BF16)
    ri = lax.broadcasted_iota(jnp.int32, (rows, rows), 0)
    rj = lax.broadcasted_iota(jnp.int32, (rows, rows), 1)
    before_row = (rj < ri).astype(BF16)
    lane = lax.broadcasted_iota(jnp.int32, (1, lanes), 1)
    xs = [e_ref[k] for k in range(nk)]
    pos = [jnp.zeros((rows, lanes), F32) for _ in range(nk)]
    counts = jnp.zeros((1, lanes), F32)
    pad_off = jnp.zeros((1, lanes), F32)
    for e in range(N_EXPERTS):
        ms = [x == e for x in xs]
        hit = ms[0]
        for m in ms[1:]:
            hit = hit | m
        mb = hit.astype(BF16)
        in_row = jnp.dot(mb, before_lane, preferred_element_type=F32)
        row_sum = jnp.dot(mb, ones, preferred_element_type=F32)
        row_off = jnp.dot(before_row, row_sum.astype(BF16), preferred_element_type=F32)
        count = row_off[rows - 1:rows, :] + row_sum[rows - 1:rows, :]
        dest = in_row + row_off + pad_off
        pos = [p + jnp.where(m, dest, 0.0) for p, m in zip(pos, ms)]
        counts = jnp.where(lane == e, count, counts)
        pad_off = pad_off + jnp.floor((count + (tm - 1)) * (1.0 / tm)) * tm
    for k in range(nk):
        pos_ref[k] = pos[k].astype(jnp.int32)
    cnt_ref[...] = counts.astype(jnp.int32)


def _route_positions(top_e, tm):
    lanes = 128
    nk, n = top_e.shape
    rows = n // lanes
    pos, counts = pl.pallas_call(
        functools.partial(_route_kernel, tm=tm),
        out_shape=[jax.ShapeDtypeStruct((nk, rows, lanes), jnp.int32), jax.ShapeDtypeStruct((1, lanes), jnp.int32)],
        compiler_params=pltpu.CompilerParams(vmem_limit_bytes=VMEM_LIMIT),
        name="route_positions",
    )(top_e.reshape(nk, rows, lanes))
    return pos.reshape(nk, n), counts[0, :N_EXPERTS]


def _moe_layer(h, f, top_e, gate, g2, weights, lat_only, n_ctx):
    layer, w_gu, w_down, bg, bu, bd = weights
    bsz, lt, d = h.shape
    skip = n_ctx if lat_only else 0
    ltok = lt - skip
    top_e, gate = top_e[:, :, skip:], gate[:, :, skip:]
    n = bsz * ltok
    n_assign = n * TOP_K
    tm = MOE_TILE
    te = top_e.transpose(1, 0, 2).reshape(TOP_K, n).astype(jnp.int32)
    bits = max(n - 1, 1).bit_length()
    assert N_EXPERTS << bits < 2 ** 31
    keys = (te << bits) | jnp.arange(n, dtype=jnp.int32)[None, :]
    slot_token = lax.sort(keys.reshape(-1)) & ((1 << bits) - 1)
    pos, counts = _route_positions(te, tm)
    start = jnp.cumsum(counts) - counts
    padded = (counts + tm - 1) // tm * tm
    pad_end = jnp.cumsum(padded)
    pad_start = pad_end - padded
    n_blocks = -(-(n_assign + N_EXPERTS * (tm - 1)) // tm)
    n_rows = n_blocks * tm
    block_expert = jnp.minimum(
        jnp.searchsorted(pad_end, jnp.arange(n_blocks, dtype=jnp.int32) * tm, side='right', method='compare_all'),
        N_EXPERTS - 1).astype(jnp.int32)
    n_used = (pad_end[-1] // tm).astype(jnp.int32).reshape(1)
    row = jnp.arange(n_rows, dtype=jnp.int32).reshape(n_blocks, tm)
    blk_shift = (start - pad_start)[block_expert][:, None]
    blk_end = (pad_start + counts)[block_expert][:, None]
    slot = jnp.clip(row + blk_shift, 0, n_assign - 1).reshape(-1)
    row_token = jnp.where((row < blk_end).reshape(-1),
                          slot_token.at[slot].get(mode='promise_in_bounds'), row.reshape(-1) % n)
    row_src = row_token + skip * (row_token // ltok + 1)
    f2d = f.reshape(bsz * lt, d)
    hb = n_blocks // 2
    y = None
    for lo, hi in ((0, hb), (hb, n_blocks)):
        x_part = f2d.at[row_src[lo * tm:hi * tm]].get(mode='promise_in_bounds')
        y = _moe_experts(x_part, block_expert[lo:hi], jnp.clip(n_used - lo, 0, hi - lo), layer,
                         w_gu, w_down, bg, bu, bd, n_rows, lo, y)
    yg = y.at[pos.reshape(-1)].get(mode='promise_in_bounds').reshape(TOP_K, bsz, ltok, d)
    return _moe_combine(yg, gate.transpose(0, 2, 1), h, g2, lat_only)


def _rmsnorm_kernel(x_ref, g_ref, o_ref):
    x = x_ref[0]
    ms = jnp.mean(x * x, axis=-1, keepdims=True)
    o_ref[0] = x * lax.rsqrt(ms + EPS) * g_ref[...]


def _final_norm(h, g):
    bsz, lt, d = h.shape
    tm = _wide_tile(lt)
    return pl.pallas_call(
        _rmsnorm_kernel,
        grid=(bsz, lt // tm),
        in_specs=[pl.BlockSpec((1, tm, d), lambda b, i: (b, i, 0)),
                  pl.BlockSpec((1, d), lambda b, i: (0, 0))],
        out_specs=pl.BlockSpec((1, tm, d), lambda b, i: (b, i, 0)),
        out_shape=jax.ShapeDtypeStruct((bsz, lt, d), F32),
        compiler_params=_cparams(("parallel", "parallel")),
        name="final_norm",
    )(h, g.reshape(1, d))


def _grid_reorder_kernel(x_ref, o_ref, *, n_ctx, rows, to_cols):
    o_ref[0, :n_ctx, :] = x_ref[0, :n_ctx, :]
    for c in range(GRID_W):
        raster = pl.ds(n_ctx + c, rows, stride=GRID_W)
        dense = pl.ds(n_ctx + c * rows, rows)
        if to_cols:
            o_ref.at[0][dense, :] = x_ref.at[0][raster, :]
        else:
            o_ref.at[0][raster, :] = x_ref.at[0][dense, :]


def _grid_reorder(h, n_ctx, to_cols):
    bsz, lt, d = h.shape
    lanes = 128
    spec = pl.BlockSpec((1, lt, lanes), lambda b, j: (b, 0, j))
    return pl.pallas_call(
        functools.partial(_grid_reorder_kernel, n_ctx=n_ctx, rows=(lt - n_ctx) // GRID_W, to_cols=to_cols),
        grid=(bsz, d // lanes),
        in_specs=[spec],
        out_specs=spec,
        out_shape=jax.ShapeDtypeStruct(h.shape, h.dtype),
        compiler_params=_cparams(("parallel", "parallel")),
        name="grid_reorder",
    )(h)


def _chunk_order(d, c, n_ctx_chunks, n_chunks):
    bwd = jnp.where(c < n_ctx_chunks, n_ctx_chunks - 1 - c, n_chunks + n_ctx_chunks - 1 - c)
    return jnp.where(d == 0, c, bwd)


def _split_bf16(x, n):
    out = []
    for _ in range(n):
        p = x.astype(BF16)
        out.append(p)
        x = x - p.astype(F32)
    return out


def _dir_tri(d, t):
    row = lax.broadcasted_iota(jnp.int32, (t, t), 0)
    col = lax.broadcasted_iota(jnp.int32, (t, t), 1)
    return jnp.where(d == 0, col - row, row - col) <= 0


def _gla_kernel(q_ref, k_ref, v_ref, r_ref, w2_ref, b2_ref, o_ref, st_ref, *, t, heads, scale):
    d = pl.program_id(0)
    c = pl.program_id(2)

    @pl.when(c == 0)
    def _():
        st_ref[...] = jnp.zeros_like(st_ref)

    dk = q_ref.shape[-1] // heads
    dv = v_ref.shape[-1] // heads
    nb = q_ref.shape[0]
    mask = _dir_tri(d, t)
    tri = mask.astype(F32)
    mid = t // 2
    items = [(bb, h) for bb in range(nb) for h in range(heads)]
    w_hi, w_lo = _split_bf16(w2_ref[0], 2)
    xs = []
    for bb in range(nb):
        r_hi, r_lo = _split_bf16(r_ref[bb], 2)
        xs.append(jnp.dot(r_hi, w_hi, preferred_element_type=F32) + jnp.dot(r_hi, w_lo, preferred_element_type=F32)
                  + jnp.dot(r_lo, w_hi, preferred_element_type=F32) + b2_ref[0])
    las = [jax.nn.log_sigmoid(x) * (1.0 / GLA_TAU) for x in xs]
    tri_b = tri.astype(BF16)
    bs = [sum(jnp.dot(tri_b, p, preferred_element_type=F32) for p in _split_bf16(la, 3)) for la in las]
    qt, kt, qe, kh_end, e_end = [], [], [], [], []
    for bb in range(nb):
        b = bs[bb]
        b_m = b[mid:mid + 1, :]
        b_end = jnp.where(d == 0, b[t - 1:t, :], b[0:1, :])
        q_s = q_ref[bb] * (jnp.exp(b - b_m) * scale)
        k_s = k_ref[bb] * jnp.exp(b_m - b)
        qe.append((q_s * jnp.exp(b_m)).astype(BF16))
        kh_end.append((k_s * jnp.exp(b_end - b_m)).astype(BF16))
        e_end.append(jnp.exp(b_end))
        qt.append(q_s.astype(BF16))
        kt.append(k_s.astype(BF16))
    att, q_st, vs = {}, {}, {}
    for bb, h in items:
        ks = slice(h * dk, (h + 1) * dk)
        vs[bb, h] = v_ref[bb, :, h * dv:(h + 1) * dv].astype(BF16)
        att[bb, h] = lax.dot_general(qt[bb][:, ks], kt[bb][:, ks], NT, preferred_element_type=F32)
        q_st[bb, h] = lax.dot_general(qe[bb][:, ks], st_ref[bb * heads + h].astype(BF16), NT,
                                      preferred_element_type=F32)
    for bb, h in items:
        a = jnp.where(mask, att[bb, h], 0.0).astype(BF16)
        o_ref[0, bb, :, h * dv:(h + 1) * dv] = jnp.dot(a, vs[bb, h], preferred_element_type=F32) + q_st[bb, h]
    for bb, h in items:
        ks = slice(h * dk, (h + 1) * dk)
        upd = lax.dot_general(vs[bb, h], kh_end[bb][:, ks], TN, preferred_element_type=F32)
        st_ref[bb * heads + h] = st_ref[bb * heads + h] * e_end[bb][:, ks] + upd


def _gla_mixer(q, k, v, r, w2, b2, n_ctx):
    bsz, lt, dkt = q.shape
    dvt = v.shape[-1]
    nr = r.shape[-1]
    t, heads = MIX_CHUNK, GLA_HEADS
    bt = MIX_BATCH
    nch = lt // t
    ncc = n_ctx // t
    dk = dkt // heads
    dv = dvt // heads
    imap = lambda d, b, c: (b, _chunk_order(d, c, ncc, nch), 0)
    return pl.pallas_call(
        functools.partial(_gla_kernel, t=t, heads=heads, scale=dk ** -0.5),
        grid=(2, bsz // bt, nch),
        in_specs=[
            pl.BlockSpec((bt, t, dkt), imap),
            pl.BlockSpec((bt, t, dkt), imap),
            pl.BlockSpec((bt, t, dvt), imap),
            pl.BlockSpec((bt, t, nr), imap),
            pl.BlockSpec((1, nr, dkt), lambda d, b, c: (d, 0, 0)),
            pl.BlockSpec((1, 1, dkt), lambda d, b, c: (d, 0, 0)),
        ],
        out_specs=pl.BlockSpec((1, bt, t, dvt), lambda d, b, c: (d, b, _chunk_order(d, c, ncc, nch), 0)),
        out_shape=jax.ShapeDtypeStruct((2, bsz, lt, dvt), F32),
        scratch_shapes=[pltpu.VMEM((bt * heads, dv, dk), F32)],
        compiler_params=_cparams(("parallel", "parallel", "arbitrary")),
        name="gla_mixer",
    )(q, k, v, r, w2, b2)


def _mlstm_kernel(q_ref, k_ref, v_ref, gc_ref, gr_ref, o_ref, c_ref, n_ref, m_ref, *, t, heads):
    d = pl.program_id(0)
    c = pl.program_id(2)

    @pl.when(c == 0)
    def _():
        c_ref[...] = jnp.zeros_like(c_ref)
        n_ref[...] = jnp.zeros_like(n_ref)
        m_ref[...] = jnp.zeros_like(m_ref)

    dh = q_ref.shape[-1] // heads
    nb = q_ref.shape[0]
    mask = _dir_tri(d, t)
    tri = mask.astype(F32)
    items = [(bb, h) for bb in range(nb) for h in range(heads)]
    gate = []
    for bb in range(nb):
        gc = gc_ref[0, bb]
        gr = gr_ref[0, bb, 0]
        fc = jax.nn.log_sigmoid(gc[:, heads:])
        fr = jax.nn.log_sigmoid(gr[heads:, :])
        b_col = jnp.dot(tri, fc, preferred_element_type=F32, precision=HI)
        b_row = lax.dot_general(fr, tri, NT, preferred_element_type=F32, precision=HI)
        b_last = jnp.where(d == 0, b_col[t - 1:t, :], b_col[0:1, :])
        gate.append((gc[:, :heads], gr[:heads, :], b_col, b_row, b_last))
    qs, ks, vs, s_raw, q_c = {}, {}, {}, {}, {}
    for bb, h in items:
        hs = slice(h * dh, (h + 1) * dh)
        qs[bb, h] = q_ref[bb, :, hs]
        ks[bb, h] = k_ref[bb, :, hs]
        vs[bb, h] = v_ref[bb, :, hs].astype(BF16)
        s_raw[bb, h] = lax.dot_general(qs[bb, h], ks[bb, h], NT, preferred_element_type=F32)
        q_c[bb, h] = jnp.dot(qs[bb, h], c_ref[bb * heads + h].astype(BF16), preferred_element_type=F32)
    logw, log_inter, m_t, w_inter, scores, den, qn = {}, {}, {}, {}, {}, {}, {}
    for bb, h in items:
        _, ir, b_col, b_row, _ = gate[bb]
        bc = b_col[:, h:h + 1]
        logw[bb, h] = jnp.where(mask, bc - b_row[h:h + 1, :] + ir[h:h + 1, :], -jnp.inf)
        log_inter[bb, h] = bc + m_ref[bb * heads + h]
    for bb, h in items:
        m_t[bb, h] = jnp.maximum(log_inter[bb, h], jnp.max(logw[bb, h], axis=-1, keepdims=True))
        qn[bb, h] = jnp.sum(qs[bb, h].astype(F32) * n_ref[bb * heads + h], axis=-1, keepdims=True)
    for bb, h in items:
        w_inter[bb, h] = jnp.exp(log_inter[bb, h] - m_t[bb, h])
        scores[bb, h] = s_raw[bb, h] * jnp.exp(logw[bb, h] - m_t[bb, h])
    for bb, h in items:
        den[bb, h] = jnp.sum(scores[bb, h], axis=-1, keepdims=True) + w_inter[bb, h] * qn[bb, h]
    num = {}
    for bb, h in items:
        num[bb, h] = (jnp.dot(scores[bb, h].astype(BF16), vs[bb, h], preferred_element_type=F32)
                      + w_inter[bb, h] * q_c[bb, h])
    for bb, h in items:
        hs = slice(h * dh, (h + 1) * dh)
        o_ref[0, bb, :, hs] = num[bb, h] / jnp.maximum(jnp.abs(den[bb, h]), jnp.exp(-m_t[bb, h]))
    log_g, m_new, kw, upd, ksum = {}, {}, {}, {}, {}
    for bb, h in items:
        ic, _, b_col, _, b_last = gate[bb]
        log_g[bb, h] = b_last[:, h:h + 1] - b_col[:, h:h + 1] + ic[:, h:h + 1]
    for bb, h in items:
        b_last = gate[bb][4]
        m_new[bb, h] = jnp.maximum(b_last[:, h:h + 1] + m_ref[bb * heads + h],
                                   jnp.max(log_g[bb, h], axis=0, keepdims=True))
    for bb, h in items:
        kw[bb, h] = ks[bb, h].astype(F32) * jnp.exp(log_g[bb, h] - m_new[bb, h])
    for bb, h in items:
        upd[bb, h] = lax.dot_general(kw[bb, h].astype(BF16), vs[bb, h], TN, preferred_element_type=F32)
        ksum[bb, h] = jnp.sum(kw[bb, h], axis=0, keepdims=True)
    for bb, h in items:
        si = bb * heads + h
        b_last = gate[bb][4]
        keep = jnp.exp(b_last[:, h:h + 1] + m_ref[si] - m_new[bb, h])
        c_ref[si] = keep * c_ref[si] + upd[bb, h]
        n_ref[si] = keep * n_ref[si] + ksum[bb, h]
        m_ref[si] = m_new[bb, h]


def _mlstm_mixer(qk, v, gates, n_ctx):
    bsz, lt, w2 = qk.shape
    w = w2 // 2
    t, heads = MLSTM_CHUNK, MLSTM_HEADS
    dh = w // heads
    nch = lt // t
    ncc = n_ctx // t
    gc = gates.reshape(bsz, lt, 2, 2 * heads).transpose(2, 0, 1, 3)
    gr = gc.reshape(2, bsz, nch, t, 2 * heads).transpose(0, 1, 2, 4, 3)
    cmap = lambda d, b, c: _chunk_order(d, c, ncc, nch)
    bt = MIX_BATCH
    return pl.pallas_call(
        functools.partial(_mlstm_kernel, t=t, heads=heads),
        grid=(2, bsz // bt, nch),
        in_specs=[
            pl.BlockSpec((bt, t, w), lambda d, b, c: (b, cmap(d, b, c), 0)),
            pl.BlockSpec((bt, t, w), lambda d, b, c: (b, cmap(d, b, c), 1)),
            pl.BlockSpec((bt, t, w), lambda d, b, c: (b, cmap(d, b, c), 0)),
            pl.BlockSpec((1, bt, t, 2 * heads), lambda d, b, c: (d, b, cmap(d, b, c), 0)),
            pl.BlockSpec((1, bt, 1, 2 * heads, t), lambda d, b, c: (d, b, cmap(d, b, c), 0, 0)),
        ],
        out_specs=pl.BlockSpec((1, bt, t, w), lambda d, b, c: (d, b, cmap(d, b, c), 0)),
        out_shape=jax.ShapeDtypeStruct((2, bsz, lt, w), F32),
        scratch_shapes=[pltpu.VMEM((bt * heads, dh, dh), F32), pltpu.VMEM((bt * heads, 1, dh), F32),
                        pltpu.VMEM((bt * heads, 1, 1), F32)],
        compiler_params=_cparams(("parallel", "parallel", "arbitrary")),
        name="mlstm_mixer",
    )(qk, qk, v, gc, gr)


def _conv_kernel(x_ref, w_ref, b_ref, s_ref, o_ref, *, n_ctx):
    x = x_ref[0]
    lt = x.shape[0]
    row = lax.broadcasted_iota(jnp.int32, x.shape, 0)
    prev = jnp.where((row == 0) | (row == n_ctx), 0.0, pltpu.roll(x, 1, 0))
    nxt = jnp.where((row == n_ctx - 1) | (row == lt - 1), 0.0, pltpu.roll(x, lt - 1, 0))
    y = b_ref[...] + w_ref[0:1, :] * prev + w_ref[1:2, :] * x + w_ref[2:3, :] * nxt
    o_ref[0] = (y * jax.nn.sigmoid(y) * s_ref[...]).astype(o_ref.dtype)


def _conv_silu(x, w, b, colscale, n_ctx):
    bsz, lt, ch = x.shape
    tc = 256
    return pl.pallas_call(
        functools.partial(_conv_kernel, n_ctx=n_ctx),
        grid=(bsz, ch // tc),
        in_specs=[
            pl.BlockSpec((1, lt, tc), lambda b, j: (b, 0, j)),
            pl.BlockSpec((3, tc), lambda b, j: (0, j)),
            pl.BlockSpec((1, tc), lambda b, j: (0, j)),
            pl.BlockSpec((1, tc), lambda b, j: (0, j)),
        ],
        out_specs=pl.BlockSpec((1, lt, tc), lambda b, j: (b, 0, j)),
        out_shape=jax.ShapeDtypeStruct((bsz, lt, ch), BF16),
        compiler_params=_cparams(("parallel", "parallel")),
        name="conv_silu",
    )(x, w, b.reshape(1, ch), colscale.reshape(1, ch))


def _s5_matrices(a_re, a_im, log_dt, b_re, b_im, c_re, c_im, backward, lane_groups=8):
    g, p = a_re.shape
    cg = b_re.shape[-1]
    j = S5_J
    lg = lane_groups
    nq = g // lg
    dt = jnp.exp(log_dt)[:, None]
    lam_re = jnp.minimum(a_re, -1e-4)
    lam_im = a_im
    decay = jnp.exp(lam_re * dt)
    ab_re = decay * jnp.cos(lam_im * dt)
    ab_im = decay * jnp.sin(lam_im * dt)
    den = lam_re * lam_re + lam_im * lam_im
    zr = ((ab_re - 1) * lam_re + ab_im * lam_im) / den
    zi = (ab_im * lam_re - (ab_re - 1) * lam_im) / den
    bb_re = zr[..., None] * b_re - zi[..., None] * b_im
    bb_im = zr[..., None] * b_im + zi[..., None] * b_re
    pw_re, pw_im = [jnp.ones_like(ab_re)], [jnp.zeros_like(ab_im)]
    for _ in range(j):
        r0, i0 = pw_re[-1], pw_im[-1]
        pw_re.append(ab_re * r0 - ab_im * i0)
        pw_im.append(ab_re * i0 + ab_im * r0)
    pw_re, pw_im = jnp.stack(pw_re), jnp.stack(pw_im)
    ca_re = c_re[None] * pw_re[:, :, None, :] - c_im[None] * pw_im[:, :, None, :]
    ca_im = c_re[None] * pw_im[:, :, None, :] + c_im[None] * pw_re[:, :, None, :]
    kk = (jnp.einsum('tgcp,gpd->tgcd', ca_re[:j], bb_re, precision=HI)
          - jnp.einsum('tgcp,gpd->tgcd', ca_im[:j], bb_im, precision=HI))
    ab_pw_re = pw_re[:j, :, :, None] * bb_re[None] - pw_im[:j, :, :, None] * bb_im[None]
    ab_pw_im = pw_re[:j, :, :, None] * bb_im[None] + pw_im[:j, :, :, None] * bb_re[None]
    eye = jnp.eye(lg, dtype=BF16)
    lb = lg * cg
    sw = lg * p
    bd_k = jnp.einsum('tqgcd,gh->tqgdhc', kk.astype(BF16).reshape(j, nq, lg, cg, cg), eye
                      ).reshape(j, nq, lb, lb)
    bd_in = [jnp.einsum('tqgpc,gh->tqgchp', a.astype(BF16).reshape(j, nq, lg, p, cg), eye
                        ).reshape(j, nq, lb, sw) for a in (ab_pw_re, ab_pw_im)]
    bd_out = [jnp.einsum('tqgcp,gh->tqgphc', a.astype(BF16).reshape(j + 1, nq, lg, cg, p), eye
                         ).reshape(j + 1, nq, sw, lb) for a in (ca_re, -ca_im)]
    jj = jnp.arange(j)
    lag = (jj[:, None] - jj[None, :]) if backward else (jj[None, :] - jj[:, None])
    kt = jnp.where((lag >= 0)[:, :, None, None, None], bd_k[jnp.clip(lag, 0, j - 1)], 0)
    ktoep = kt.transpose(2, 0, 3, 1, 4).reshape(nq, j * lb, j * lb)
    tau_in = jj if backward else (j - 1 - jj)
    win_re, win_im = (a[tau_in].transpose(1, 0, 2, 3).reshape(nq, j * lb, sw) for a in bd_in)
    tau_out = (j - jj) if backward else (jj + 1)
    wout_re, wout_im = (a[tau_out].transpose(1, 2, 0, 3).reshape(nq, sw, j * lb) for a in bd_out)
    dec_re = pw_re[j].reshape(nq, 1, sw)
    dec_im = pw_im[j].reshape(nq, 1, sw)
    return ktoep, win_re, win_im, wout_re, wout_im, dec_re, dec_im


def _s5_kernel(u_ref, kt_ref, wir_ref, wii_ref, wor_ref, woi_ref, dr_ref, di_ref, y_ref,
               xf_ref, yf_ref, sre_ref, sim_ref, *, bt, nk, nk_ctx, rs):
    d = pl.program_id(0)
    j = S5_J
    lanes = u_ref.shape[-1]
    for b in range(bt):
        for jj in range(j):
            xf_ref[b * nk:(b + 1) * nk, jj * lanes:(jj + 1) * lanes] = (
                u_ref.at[b][pl.ds(jj, nk, stride=j), :].astype(BF16))
    xf = xf_ref[...]
    yf_ref[...] = jnp.dot(xf, kt_ref[0, 0], preferred_element_type=F32)
    inc_re = jnp.dot(xf, wir_ref[0, 0], preferred_element_type=F32)
    inc_im = jnp.dot(xf, wii_ref[0, 0], preferred_element_type=F32)
    nl = sre_ref.shape[0]
    for b in range(bt):
        for l in range(nl):
            sre_ref[l, b * rs:b * rs + nk, :] = inc_re[b * nk:(b + 1) * nk, l * lanes:(l + 1) * lanes]
            sim_ref[l, b * rs:b * rs + nk, :] = inc_im[b * nk:(b + 1) * nk, l * lanes:(l + 1) * lanes]
    a_re = [dr_ref[0, 0, :, l * lanes:(l + 1) * lanes] for l in range(nl)]
    a_im = [di_ref[0, 0, :, l * lanes:(l + 1) * lanes] for l in range(nl)]

    def step(kidx, carry):
        rows = pl.ds(kidx, bt, stride=rs)
        out = []
        for l in range(nl):
            s_re, s_im = carry[2 * l], carry[2 * l + 1]
            i_re = sre_ref.at[l][rows, :]
            i_im = sim_ref.at[l][rows, :]
            sre_ref.at[l][rows, :] = s_re
            sim_ref.at[l][rows, :] = s_im
            out.append(a_re[l] * s_re - a_im[l] * s_im + i_re)
            out.append(a_re[l] * s_im + a_im[l] * s_re + i_im)
        return tuple(out)

    zero = tuple(jnp.zeros((bt, lanes), F32) for _ in range(2 * nl))

    @pl.when(d == 0)
    def _():
        lax.fori_loop(0, nk, step, zero)

    @pl.when(d == 1)
    def _():
        carry = lax.fori_loop(0, nk_ctx, lambda i, cr: step(nk_ctx - 1 - i, cr), zero)
        lax.fori_loop(0, nk - nk_ctx, lambda i, cr: step(nk - 1 - i, cr), carry)

    for b in range(bt):
        sp_re = jnp.concatenate([sre_ref[l, b * rs:b * rs + nk, :] for l in range(nl)], axis=-1).astype(BF16)
        sp_im = jnp.concatenate([sim_ref[l, b * rs:b * rs + nk, :] for l in range(nl)], axis=-1).astype(BF16)
        yb = (yf_ref[b * nk:(b + 1) * nk, :]
              + jnp.dot(sp_re, wor_ref[0, 0], preferred_element_type=F32)
              + jnp.dot(sp_im, woi_ref[0, 0], preferred_element_type=F32))
        for jj in range(j):
            y_ref.at[0, b][pl.ds(jj, nk, stride=j), :] = yb[:, jj * lanes:(jj + 1) * lanes]


def _s5_mixer(u, mats, n_ctx):
    bsz, lt, w = u.shape
    ktoep, win_re, win_im, wout_re, wout_im, dec_re, dec_im = mats
    lanes = 128
    bt = 4 if bsz % 4 == 0 else 2
    nq = w // lanes
    j = S5_J
    nk = lt // j
    nk_ctx = n_ctx // j
    rs = nk + 8
    fl = j * lanes
    sw = win_re.shape[-1]
    wmap = lambda d, q, b: (d, q, 0, 0)
    return pl.pallas_call(
        functools.partial(_s5_kernel, bt=bt, nk=nk, nk_ctx=nk_ctx, rs=rs),
        grid=(2, nq, bsz // bt),
        in_specs=[
            pl.BlockSpec((bt, lt, lanes), lambda d, q, b: (b, 0, q)),
            pl.BlockSpec((1, 1, fl, fl), wmap),
            pl.BlockSpec((1, 1, fl, sw), wmap),
            pl.BlockSpec((1, 1, fl, sw), wmap),
            pl.BlockSpec((1, 1, sw, fl), wmap),
            pl.BlockSpec((1, 1, sw, fl), wmap),
            pl.BlockSpec((1, 1, 1, sw), wmap),
            pl.BlockSpec((1, 1, 1, sw), wmap),
        ],
        out_specs=pl.BlockSpec((1, bt, lt, lanes), lambda d, q, b: (d, b, 0, q)),
        out_shape=jax.ShapeDtypeStruct((2, bsz, lt, w), F32),
        scratch_shapes=[pltpu.VMEM((bt * nk, fl), BF16), pltpu.VMEM((bt * nk, fl), F32),
                        pltpu.VMEM((sw // lanes, bt * rs, lanes), F32),
                        pltpu.VMEM((sw // lanes, bt * rs, lanes), F32)],
        compiler_params=_cparams(("parallel", "parallel", "arbitrary")),
        name="s5_mixer",
    )(u, ktoep, win_re, win_im, wout_re, wout_im, dec_re, dec_im)


def _head_norm(x, heads):
    dh = x.shape[-1] // heads
    outs = []
    for h in range(heads):
        xh = x[:, h * dh:(h + 1) * dh]
        outs.append(xh * lax.rsqrt(jnp.mean(xh * xh, axis=-1, keepdims=True) + EPS))
    return jnp.concatenate(outs, axis=-1)


def _even_post_kernel(m_ref, o_ref, s_ref, u_ref, mg_ref, dsk_ref, gw_ref, gb_ref, w_ref, h_ref, gate_ref,
                      out_ref, *, heads):
    m = m_ref[0, 0] + m_ref[1, 0]
    m_out = _head_norm(m, heads) * mg_ref[...] * jax.nn.sigmoid(o_ref[0])
    y = jax.nn.gelu(s_ref[0, 0] + s_ref[1, 0] + dsk_ref[...] * u_ref[0])
    glu = jnp.dot(y.astype(BF16), gw_ref[...], preferred_element_type=F32) + gb_ref[...]
    s_out = y * jax.nn.sigmoid(glu)
    cat = jnp.concatenate([m_out, s_out], axis=-1).astype(BF16)
    z = jnp.dot(cat, w_ref[...], preferred_element_type=F32)
    out_ref[0] = h_ref[0] + gate_ref[0, 0] * z


def _even_post(m2, o, s2, u, mnorm_g, d_skip, glu_w, glu_b, w_out, h, gate):
    bsz, lt, d = h.shape
    mw = o.shape[-1]
    sw = u.shape[-1]
    tm = ROW_TILE
    row = lambda b, i: (b, i, 0)
    row2 = lambda b, i: (0, b, i, 0)
    const = lambda b, i: (0, 0)
    return pl.pallas_call(
        functools.partial(_even_post_kernel, heads=MLSTM_HEADS),
        grid=(bsz, lt // tm),
        in_specs=[
            pl.BlockSpec((2, 1, tm, mw), row2),
            pl.BlockSpec((1, tm, mw), row),
            pl.BlockSpec((2, 1, tm, sw), row2),
            pl.BlockSpec((1, tm, sw), row),
            pl.BlockSpec((1, mw), const),
            pl.BlockSpec((1, sw), const),
            pl.BlockSpec((sw, sw), const),
            pl.BlockSpec((1, sw), const),
            pl.BlockSpec((mw + sw, d), const),
            pl.BlockSpec((1, tm, d), row),
            pl.BlockSpec((1, 1, 1, d), _seg_map),
        ],
        out_specs=pl.BlockSpec((1, tm, d), row),
        out_shape=jax.ShapeDtypeStruct((bsz, lt, d), F32),
        compiler_params=_cparams(("parallel", "parallel")),
        name="even_post",
    )(m2, o, s2, u, mnorm_g.reshape(1, mw), d_skip.reshape(1, sw), glu_w.astype(BF16), glu_b.reshape(1, sw),
      w_out.astype(BF16), h, gate)


def _odd_post_kernel(o_ref, g_ref, ng_ref, w_ref, h_ref, gate_ref, out_ref, *, heads):
    g = g_ref[0]
    y = _head_norm(o_ref[0, 0] + o_ref[1, 0], heads) * ng_ref[...] * (g * jax.nn.sigmoid(g))
    z = jnp.dot(y.astype(BF16), w_ref[...], preferred_element_type=F32)
    out_ref[0] = h_ref[0] + gate_ref[0, 0] * z


def _odd_post(o, g, norm_g, w_out, h, gate):
    bsz, lt, d = h.shape
    dv = o.shape[-1]
    tm = ROW_TILE
    row = lambda b, i: (b, i, 0)
    const = lambda b, i: (0, 0)
    return pl.pallas_call(
        functools.partial(_odd_post_kernel, heads=GLA_HEADS),
        grid=(bsz, lt // tm),
        in_specs=[
            pl.BlockSpec((2, 1, tm, dv), lambda b, i: (0, b, i, 0)),
            pl.BlockSpec((1, tm, dv), row),
            pl.BlockSpec((1, dv), const),
            pl.BlockSpec((dv, d), const),
            pl.BlockSpec((1, tm, d), row),
            pl.BlockSpec((1, 1, 1, d), _seg_map),
        ],
        out_specs=pl.BlockSpec((1, tm, d), row),
        out_shape=jax.ShapeDtypeStruct((bsz, lt, d), F32),
        compiler_params=_cparams(("parallel", "parallel")),
        name="odd_post",
    )(o, g, norm_g.reshape(1, dv), w_out.astype(BF16), h, gate)


def kernel(x, c, ctx, c_ctx, mod_w, mod_b, norm_mix_g, norm_ffn_g, ev_w_in, ev_b_in, ev_conv_w, ev_conv_b, ev_mlstm_norm_g, ev_s5_a_re_f, ev_s5_a_im_f, ev_s5_log_dt_f, ev_s5_a_re_b, ev_s5_a_im_b, ev_s5_log_dt_b, ev_s5_b_re, ev_s5_b_im, ev_s5_c_re, ev_s5_c_im, ev_s5_d, ev_s5_glu_w, ev_s5_glu_b, ev_w_out, od_w_in, od_gate_w2_f, od_gate_b2_f, od_gate_w2_b, od_gate_b2_b, od_norm_g, od_w_out, router_w, router_b, moe_w_gu, moe_b_gu, moe_w_down, moe_b_down, final_norm_g):
    bsz, seq, d = x.shape
    n_ctx = ctx.shape[1]
    depth = mod_w.shape[0]
    lt = n_ctx + seq
    assert n_ctx == ROW_TILE and seq % ROW_TILE == 0 and seq % GRID_W == 0

    h = jnp.concatenate([ctx, x], axis=1)
    c_all = jnp.concatenate([c, c_ctx[None, :]], axis=0)
    c_all = jnp.pad(c_all, ((0, (-c_all.shape[0]) % 8), (0, 0)))
    mods = _modulation(c_all, mod_w, mod_b)
    mod_lat = mods[:, :bsz]
    mod_ctx = jnp.broadcast_to(mods[:, bsz:bsz + 1], mod_lat.shape)
    mod6 = jnp.stack([mod_ctx, mod_lat], axis=2).reshape(depth, bsz, 2, 6, 1, d)

    bg_all = moe_b_gu[..., 0::2]
    bu_all = moe_b_gu[..., 1::2]

    mw = ev_conv_w.shape[-1] // 2
    n_gates = 4 * MLSTM_HEADS
    s5w = ev_s5_d.shape[-1]
    dk_t = od_gate_w2_f.shape[-1]
    dv_t = od_norm_g.shape[-1]
    for layer in range(depth):
        last = layer == depth - 1
        j = layer // 2
        m6 = mod6[layer]
        sh1, sc1, g1, sh2, sc2, g2 = (m6[:, :, i] for i in range(6))
        if layer % 2 == 0:
            w_in, b_in = ev_w_in[j], ev_b_in[j]
            cols = jnp.concatenate([jnp.arange(0, 4 * mw), jnp.arange(4 * mw + n_gates, 4 * mw + n_gates + s5w),
                                    jnp.arange(4 * mw, 4 * mw + n_gates)])
            qk_pre, v, o, u, gates = _nm_matmul(h, norm_mix_g[layer], sh1, sc1, w_in[:, cols], b_in[cols],
                                                (2 * mw, mw, mw, s5w, n_gates))
            dh = mw // MLSTM_HEADS
            colscale = jnp.concatenate([jnp.full((mw,), dh ** -0.5, F32), jnp.ones((mw,), F32)])
            qk = _conv_silu(qk_pre, ev_conv_w[j], ev_conv_b[j], colscale, n_ctx)
            m2 = _mlstm_mixer(qk, v, gates, n_ctx)
            shared = (ev_s5_b_re[j], ev_s5_b_im[j], ev_s5_c_re[j], ev_s5_c_im[j])
            mats_f = _s5_matrices(ev_s5_a_re_f[j], ev_s5_a_im_f[j], ev_s5_log_dt_f[j], *shared, backward=False)
            mats_b = _s5_matrices(ev_s5_a_re_b[j], ev_s5_a_im_b[j], ev_s5_log_dt_b[j], *shared, backward=True)
            s2 = _s5_mixer(u, tuple(jnp.stack([a, b]) for a, b in zip(mats_f, mats_b)), n_ctx)
            h = _even_post(m2, o, s2, u, ev_mlstm_norm_g[j], ev_s5_d[j], ev_s5_glu_w[j], ev_s5_glu_b[j],
                           ev_w_out[j], h, g1)
        else:
            hc = _grid_reorder(h, n_ctx, True)
            qq, kk, vv, gg, rr = _nm_matmul(hc, norm_mix_g[layer], sh1, sc1, od_w_in[j],
                                            jnp.zeros((od_w_in.shape[-1],), F32),
                                            (dk_t, dk_t, dv_t, dv_t, 2 * GLA_RANK))
            zero = jnp.zeros_like(od_gate_w2_f[j])
            w2 = jnp.stack([jnp.concatenate([od_gate_w2_f[j], zero], axis=0),
                            jnp.concatenate([zero, od_gate_w2_b[j]], axis=0)])
            b2 = jnp.stack([od_gate_b2_f[j], od_gate_b2_b[j]])[:, None, :]
            o2 = _gla_mixer(qq, kk, vv, rr, w2, b2, n_ctx)
            h = _grid_reorder(_odd_post(o2, gg, od_norm_g[j], od_w_out[j], hc, g1), n_ctx, False)
        f, top_e, gate = _ffn_prep(h, norm_ffn_g[layer], sh2, sc2, router_w[layer], router_b[layer])
        weights = (layer, moe_w_gu, moe_w_down, bg_all[layer], bu_all[layer], moe_b_down[layer])
        h = _moe_layer(h, f, top_e, gate, g2, weights, last, n_ctx)
    return _final_norm(h, final_norm_g)
```

```python
import functools

import jax
import jax.numpy as jnp
from jax import lax
from jax.experimental import pallas as pl
from jax.experimental.pallas import tpu as pltpu

F32 = jnp.float32
BF16 = jnp.bfloat16
HI = lax.Precision.HIGHEST

EPS = 1e-6
GRID_W = 64
MLSTM_HEADS = 4
S5_GROUP = 16
GLA_HEADS = 4
GLA_RANK = 16
GLA_TAU = 16.0
N_EXPERTS = 32
TOP_K = 4
SWIGLU_LIMIT = 7.0
SWIGLU_ALPHA = 1.702

ROW_TILE = 256
MOE_TILE = 512
MIX_CHUNK = 64
MLSTM_CHUNK = 128
MIX_BATCH = 4
S5_J = 8
VMEM_LIMIT = 56 * 1024 * 1024

NT = (((1,), (1,)), ((), ()))
TN = (((0,), (0,)), ((), ()))


def _cparams(sem):
    return pltpu.CompilerParams(dimension_semantics=sem, vmem_limit_bytes=VMEM_LIMIT)


def _mod_kernel(c_ref, w_ref, b_ref, o_ref):
    c = c_ref[...]
    a = c * jax.nn.sigmoid(c)
    o_ref[0] = jnp.dot(a.astype(BF16), w_ref[0].astype(BF16), preferred_element_type=F32) + b_ref[0]


def _modulation(c_all, mod_w, mod_b):
    depth, d, n6 = mod_w.shape
    rows = c_all.shape[0]
    tn = d
    return pl.pallas_call(
        _mod_kernel,
        grid=(depth, n6 // tn),
        in_specs=[
            pl.BlockSpec((rows, d), lambda l, j: (0, 0)),
            pl.BlockSpec((1, d, tn), lambda l, j: (l, 0, j)),
            pl.BlockSpec((1, 1, tn), lambda l, j: (l, 0, j)),
        ],
        out_specs=pl.BlockSpec((1, rows, tn), lambda l, j: (l, 0, j)),
        out_shape=jax.ShapeDtypeStruct((depth, rows, n6), F32),
        compiler_params=_cparams(("arbitrary", "arbitrary")),
        name="modulation",
    )(c_all, mod_w, mod_b.reshape(depth, 1, n6))


def _norm_mod(x, g, sh, sc):
    ms = jnp.mean(x * x, axis=-1, keepdims=True)
    return (x * lax.rsqrt(ms + EPS) * g) * (1.0 + sc) + sh


def _nm_matmul_kernel(x_ref, g_ref, sh_ref, sc_ref, w_ref, b_ref, *out_refs, splits):
    a = _norm_mod(x_ref[0], g_ref[...], sh_ref[0, 0], sc_ref[0, 0])
    z = jnp.dot(a.astype(BF16), w_ref[...], preferred_element_type=F32) + b_ref[...]
    for (lo, hi), o_ref in zip(splits, out_refs):
        o_ref[0] = z[:, lo:hi].astype(o_ref.dtype)


def _wide_tile(rows):
    return next(t for t in (4 * ROW_TILE, 3 * ROW_TILE, 2 * ROW_TILE, ROW_TILE) if rows % t == 0)


def _seg_map(b, i):
    return (b, jnp.minimum(i, 1), 0, 0)


def _nm_matmul(h, g, shift, scale, w, bias, widths):
    bsz, lt, d = h.shape
    p = w.shape[1]
    splits, lo = [], 0
    for wd in widths:
        splits.append((lo, lo + wd))
        lo += wd
    assert lo == p
    tm = ROW_TILE
    return pl.pallas_call(
        functools.partial(_nm_matmul_kernel, splits=tuple(splits)),
        grid=(bsz, lt // tm),
        in_specs=[
            pl.BlockSpec((1, tm, d), lambda b, i: (b, i, 0)),
            pl.BlockSpec((1, d), lambda b, i: (0, 0)),
            pl.BlockSpec((1, 1, 1, d), _seg_map),
            pl.BlockSpec((1, 1, 1, d), _seg_map),
            pl.BlockSpec((d, p), lambda b, i: (0, 0)),
            pl.BlockSpec((1, p), lambda b, i: (0, 0)),
        ],
        out_specs=[pl.BlockSpec((1, tm, wd), lambda b, i: (b, i, 0)) for wd in widths],
        out_shape=[jax.ShapeDtypeStruct((bsz, lt, wd), F32) for wd in widths],
        compiler_params=_cparams(("parallel", "parallel")),
        name="norm_mod_matmul",
    )(h, g.reshape(1, d), shift, scale, w.astype(BF16), bias.reshape(1, p))


def _ffn_prep_kernel(x_ref, g_ref, sh_ref, sc_ref, rw_ref, rb_ref, f_ref, te_ref, gt_ref, *, n_ctx):
    tm = x_ref.shape[1]
    is_ctx = pl.program_id(1) * tm + lax.broadcasted_iota(jnp.int32, (tm, 1), 0) < n_ctx
    a = _norm_mod(x_ref[0], g_ref[...], jnp.where(is_ctx, sh_ref[0, 0], sh_ref[0, 1]),
                  jnp.where(is_ctx, sc_ref[0, 0], sc_ref[0, 1]))
    f_ref[0] = a.astype(f_ref.dtype)
    logits = lax.dot_general(rw_ref[...], a.astype(BF16), NT, preferred_element_type=F32) + rb_ref[...]
    ne = logits.shape[0]
    eidx = lax.broadcasted_iota(jnp.int32, logits.shape, 0)
    work = logits
    vals, idxs = [], []
    for _ in range(TOP_K):
        m = jnp.max(work, axis=0, keepdims=True)
        idx = jnp.min(jnp.where(work == m, eidx, ne), axis=0, keepdims=True)
        vals.append(m)
        idxs.append(idx)
        work = jnp.where(eidx == idx, -jnp.inf, work)
    exps = [jnp.exp(v - vals[0]) for v in vals]
    denom = exps[0]
    for e in exps[1:]:
        denom = denom + e
    for k in range(TOP_K):
        te_ref[0, k:k + 1, :] = idxs[k]
        gt_ref[0, k:k + 1, :] = exps[k] / denom


def _ffn_prep(h, g, shift, scale, router_w, router_b):
    bsz, lt, d = h.shape
    ne = router_w.shape[1]
    tm = _wide_tile(lt)
    both = pl.BlockSpec((1, 2, 1, d), lambda b, i: (b, 0, 0, 0))
    return pl.pallas_call(
        functools.partial(_ffn_prep_kernel, n_ctx=ROW_TILE),
        grid=(bsz, lt // tm),
        in_specs=[
            pl.BlockSpec((1, tm, d), lambda b, i: (b, i, 0)),
            pl.BlockSpec((1, d), lambda b, i: (0, 0)),
            both,
            both,
            pl.BlockSpec((ne, d), lambda b, i: (0, 0)),
            pl.BlockSpec((ne, 1), lambda b, i: (0, 0)),
        ],
        out_specs=[
            pl.BlockSpec((1, tm, d), lambda b, i: (b, i, 0)),
            pl.BlockSpec((1, TOP_K, tm), lambda b, i: (b, 0, i)),
            pl.BlockSpec((1, TOP_K, tm), lambda b, i: (b, 0, i)),
        ],
        out_shape=[
            jax.ShapeDtypeStruct((bsz, lt, d), BF16),
            jax.ShapeDtypeStruct((bsz, TOP_K, lt), jnp.int32),
            jax.ShapeDtypeStruct((bsz, TOP_K, lt), F32),
        ],
        compiler_params=_cparams(("parallel", "parallel")),
        name="ffn_prep",
    )(h, g.reshape(1, d), shift, scale, router_w.T.astype(BF16), router_b.reshape(ne, 1))


GU_BLOCK = 256


def _moe_kernel(be_ref, nb_ref, first_ref, slot_ref, nxt_ref, x_ref, wgu_hbm, wd_hbm, bg_ref, bu_ref, bd_ref,
                y_prev_ref, o_ref, wgu_buf, wd_buf, wgu_s, wd_s, sem, *, layer):
    del y_prev_ref
    i = pl.program_id(0)
    active = i < nb_ref[0]
    half = GU_BLOCK // 2
    nblk = wgu_s.shape[1] // GU_BLOCK

    def weight_copies(e, slot):
        return (pltpu.make_async_copy(wgu_hbm.at[layer, e], wgu_buf.at[slot], sem.at[0, slot]),
                pltpu.make_async_copy(wd_hbm.at[layer, e], wd_buf.at[slot], sem.at[1, slot]))

    @pl.when(active & (i == 0))
    def _():
        for cp in weight_copies(be_ref[0], 0):
            cp.start()

    @pl.when(active & (first_ref[i] == 1))
    def _():
        slot = slot_ref[i]
        for cp in weight_copies(be_ref[i], slot):
            cp.wait()

        @pl.when(nxt_ref[i] >= 0)
        def _():
            for cp in weight_copies(nxt_ref[i], 1 - slot):
                cp.start()

        r = lax.broadcasted_iota(jnp.int32, (GU_BLOCK, GU_BLOCK), 0)
        c = lax.broadcasted_iota(jnp.int32, (GU_BLOCK, GU_BLOCK), 1)
        perm = (r == jnp.where(c < half, 2 * c, 2 * (c - half) + 1)).astype(BF16)
        for k in range(nblk):
            cs = slice(k * GU_BLOCK, (k + 1) * GU_BLOCK)
            wgu_s[:, cs] = jnp.dot(wgu_buf[slot, :, cs].astype(BF16), perm,
                                   preferred_element_type=F32).astype(BF16)
        wd_s[...] = wd_buf[slot].astype(BF16)

    @pl.when(active)
    def _():
        gu = jnp.dot(x_ref[...], wgu_s[...], preferred_element_type=F32)
        hdn = []
        for k in range(nblk):
            hs = slice(k * half, (k + 1) * half)
            g = gu[:, k * GU_BLOCK:k * GU_BLOCK + half] + bg_ref[0, :, hs]
            u = gu[:, k * GU_BLOCK + half:(k + 1) * GU_BLOCK] + bu_ref[0, :, hs]
            g = jnp.minimum(g, SWIGLU_LIMIT)
            u = jnp.clip(u, -SWIGLU_LIMIT, SWIGLU_LIMIT)
            hdn.append(((u + 1.0) * (g * jax.nn.sigmoid(SWIGLU_ALPHA * g))).astype(BF16))
        hdn = jnp.concatenate(hdn, axis=-1)
        o_ref[...] = (jnp.dot(hdn, wd_s[...], preferred_element_type=F32) + bd_ref[0]).astype(o_ref.dtype)

    @pl.when(jnp.logical_not(active))
    def _():
        o_ref[...] = jnp.zeros_like(o_ref)


def _moe_experts(x_sorted, block_expert, n_used, layer, w_gu, w_down, bg, bu, bd, block_off, y_prev):
    n_rows, d = x_sorted.shape
    _, ne, _, f2 = w_gu.shape
    f = f2 // 2
    tm = MOE_TILE
    n_blocks = n_rows // tm
    assert f2 % GU_BLOCK == 0
    blk = jnp.arange(n_blocks, dtype=jnp.int32)
    prev = jnp.concatenate([block_expert[:1], block_expert[:-1]])
    first = (blk < n_used[0]) & ((blk == 0) | (block_expert != prev))
    slot = (jnp.cumsum(first.astype(jnp.int32)) - 1) & 1
    first_idx = jnp.where(first, blk, n_blocks)
    next_first = lax.cummin(first_idx, axis=0, reverse=True)
    next_first = jnp.concatenate([next_first[1:], jnp.full((1,), n_blocks, jnp.int32)])
    nxt = jnp.where(next_first < n_blocks, block_expert[jnp.minimum(next_first, n_blocks - 1)], -1)
    bmap = lambda i, be, nb, fi, sl, nx: (be[i], 0, 0)
    rmap = lambda i, be, nb, fi, sl, nx: (i, 0)
    grid_spec = pltpu.PrefetchScalarGridSpec(
        num_scalar_prefetch=5,
        grid=(n_blocks,),
        in_specs=[
            pl.BlockSpec((tm, d), rmap),
            pl.BlockSpec(memory_space=pl.ANY),
            pl.BlockSpec(memory_space=pl.ANY),
            pl.BlockSpec((1, 1, f), bmap),
            pl.BlockSpec((1, 1, f), bmap),
            pl.BlockSpec((1, 1, d), bmap),
            pl.BlockSpec(memory_space=pl.ANY),
        ],
        out_specs=pl.BlockSpec((tm, d), lambda i, be, nb, fi, sl, nx: (i + block_off, 0)),
        scratch_shapes=[pltpu.VMEM((2, d, f2), F32), pltpu.VMEM((2, f, d), F32),
                        pltpu.VMEM((d, f2), BF16), pltpu.VMEM((f, d), BF16),
                        pltpu.SemaphoreType.DMA((2, 2))],
    )
    operands = (block_expert, n_used, first.astype(jnp.int32), slot.astype(jnp.int32), nxt.astype(jnp.int32),
                x_sorted, w_gu, w_down, bg.reshape(ne, 1, f), bu.reshape(ne, 1, f), bd.reshape(ne, 1, d), y_prev)
    return pl.pallas_call(
        functools.partial(_moe_kernel, layer=layer),
        grid_spec=grid_spec,
        out_shape=jax.ShapeDtypeStruct(y_prev.shape, y_prev.dtype),
        input_output_aliases={len(operands) - 1: 0},
        compiler_params=_cparams(("arbitrary",)),
        name="moe_experts",
    )(*operands)


def _combine_kernel(y_ref, gt_ref, h_ref, g2_ref, o_ref, *, n_ctx):
    gt = gt_ref[0]
    acc = y_ref[0, 0].astype(F32) * gt[:, 0:1]
    for k in range(1, TOP_K):
        acc = acc + y_ref[k, 0].astype(F32) * gt[:, k:k + 1]
    tm = acc.shape[0]
    is_ctx = pl.program_id(1) * tm + lax.broadcasted_iota(jnp.int32, (tm, 1), 0) < n_ctx
    o_ref[0] = h_ref[0] + jnp.where(is_ctx, g2_ref[0, 0], g2_ref[0, 1]) * acc


def _moe_combine(yg, gate, h, g2, lat_only):
    k, bsz, lt, d = yg.shape
    tm = ROW_TILE if lat_only else _wide_tile(lt)
    tile_off = 1 if lat_only else 0
    return pl.pallas_call(
        functools.partial(_combine_kernel, n_ctx=0 if lat_only else ROW_TILE),
        grid=(bsz, lt // tm),
        in_specs=[
            pl.BlockSpec((k, 1, tm, d), lambda b, i: (0, b, i, 0)),
            pl.BlockSpec((1, tm, k), lambda b, i: (b, i, 0)),
            pl.BlockSpec((1, tm, d), lambda b, i: (b, i + tile_off, 0)),
            pl.BlockSpec((1, 2, 1, d), lambda b, i: (b, 0, 0, 0)),
        ],
        out_specs=pl.BlockSpec((1, tm, d), lambda b, i: (b, i, 0)),
        out_shape=jax.ShapeDtypeStruct((bsz, lt, d), F32),
        compiler_params=_cparams(("parallel", "parallel")),
        name="moe_combine",
    )(yg, gate, h, g2)


def _route_kernel(e_ref, pos_ref, cnt_ref, *, tm):
    nk, rows, lanes = e_ref.shape
    li = lax.broadcasted_iota(jnp.int32, (lanes, lanes), 0)
    lj = lax.broadcasted_iota(jnp.int32, (lanes, lanes), 1)
    before_lane = (li < lj).astype(BF16)
    ones = jnp.ones((lanes, lanes), BF16)
    ri = lax.broadcasted_iota(jnp.int32, (rows, rows), 0)
    rj = lax.broadcasted_iota(jnp.int32, (rows, rows), 1)
    before_row = (rj < ri).astype(BF16)
    lane = lax.broadcasted_iota(jnp.int32, (1, lanes), 1)
    xs = [e_ref[k] for k in range(nk)]
    pos = [jnp.zeros((rows, lanes), F32) for _ in range(nk)]
    counts = jnp.zeros((1, lanes), F32)
    pad_off = jnp.zeros((1, lanes), F32)
    for e in range(N_EXPERTS):
        ms = [x == e for x in xs]
        hit = ms[0]
        for m in ms[1:]:
            hit = hit | m
        mb = hit.astype(BF16)
        in_row = jnp.dot(mb, before_lane, preferred_element_type=F32)
        row_sum = jnp.dot(mb, ones, preferred_element_type=F32)
        row_off = jnp.dot(before_row, row_sum.astype(BF16), preferred_element_type=F32)
        count = row_off[rows - 1:rows, :] + row_sum[rows - 1:rows, :]
        dest = in_row + row_off + pad_off
        pos = [p + jnp.where(m, dest, 0.0) for p, m in zip(pos, ms)]
        counts = jnp.where(lane == e, count, counts)
        pad_off = pad_off + jnp.floor((count + (tm - 1)) * (1.0 / tm)) * tm
    for k in range(nk):
        pos_ref[k] = pos[k].astype(jnp.int32)
    cnt_ref[...] = counts.astype(jnp.int32)


def _route_positions(top_e, tm):
    lanes = 128
    nk, n = top_e.shape
    rows = n // lanes
    pos, counts = pl.pallas_call(
        functools.partial(_route_kernel, tm=tm),
        out_shape=[jax.ShapeDtypeStruct((nk, rows, lanes), jnp.int32), jax.ShapeDtypeStruct((1, lanes), jnp.int32)],
        compiler_params=pltpu.CompilerParams(vmem_limit_bytes=VMEM_LIMIT),
        name="route_positions",
    )(top_e.reshape(nk, rows, lanes))
    return pos.reshape(nk, n), counts[0, :N_EXPERTS]


def _moe_layer(h, f, top_e, gate, g2, weights, lat_only, n_ctx):
    layer, w_gu, w_down, bg, bu, bd = weights
    bsz, lt, d = h.shape
    skip = n_ctx if lat_only else 0
    ltok = lt - skip
    top_e, gate = top_e[:, :, skip:], gate[:, :, skip:]
    n = bsz * ltok
    n_assign = n * TOP_K
    tm = MOE_TILE
    te = top_e.transpose(1, 0, 2).reshape(TOP_K, n).astype(jnp.int32)
    bits = max(n - 1, 1).bit_length()
    assert N_EXPERTS << bits < 2 ** 31
    keys = (te << bits) | jnp.arange(n, dtype=jnp.int32)[None, :]
    slot_token = lax.sort(keys.reshape(-1)) & ((1 << bits) - 1)
    pos, counts = _route_positions(te, tm)
    start = jnp.cumsum(counts) - counts
    padded = (counts + tm - 1) // tm * tm
    pad_end = jnp.cumsum(padded)
    pad_start = pad_end - padded
    n_blocks = -(-(n_assign + N_EXPERTS * (tm - 1)) // tm)
    n_rows = n_blocks * tm
    block_expert = jnp.minimum(
        jnp.searchsorted(pad_end, jnp.arange(n_blocks, dtype=jnp.int32) * tm, side='right', method='compare_all'),
        N_EXPERTS - 1).astype(jnp.int32)
    n_used = (pad_end[-1] // tm).astype(jnp.int32).reshape(1)
    row = jnp.arange(n_rows, dtype=jnp.int32).reshape(n_blocks, tm)
    blk_shift = (start - pad_start)[block_expert][:, None]
    blk_end = (pad_start + counts)[block_expert][:, None]
    slot = jnp.clip(row + blk_shift, 0, n_assign - 1).reshape(-1)
    row_token = jnp.where((row < blk_end).reshape(-1),
                          slot_token.at[slot].get(mode='promise_in_bounds'), row.reshape(-1) % n)
    row_src = row_token + skip * (row_token // ltok + 1)
    f2d = f.reshape(bsz * lt, d)
    hb = n_blocks // 2
    y = jnp.zeros((n_rows, d), BF16)
    for lo, hi in ((0, hb), (hb, n_blocks)):
        x_part = f2d.at[row_src[lo * tm:hi * tm]].get(mode='promise_in_bounds')
        y = _moe_experts(x_part, block_expert[lo:hi], jnp.clip(n_used - lo, 0, hi - lo), layer,
                         w_gu, w_down, bg, bu, bd, lo, y)
    yg = y.at[pos.reshape(-1)].get(mode='promise_in_bounds').reshape(TOP_K, bsz, ltok, d)
    return _moe_combine(yg, gate.transpose(0, 2, 1), h, g2, lat_only)


def _rmsnorm_kernel(x_ref, g_ref, o_ref):
    x = x_ref[0]
    ms = jnp.mean(x * x, axis=-1, keepdims=True)
    o_ref[0] = x * lax.rsqrt(ms + EPS) * g_ref[...]


def _final_norm(h, g):
    bsz, lt, d = h.shape
    tm = _wide_tile(lt)
    return pl.pallas_call(
        _rmsnorm_kernel,
        grid=(bsz, lt // tm),
        in_specs=[pl.BlockSpec((1, tm, d), lambda b, i: (b, i, 0)),
                  pl.BlockSpec((1, d), lambda b, i: (0, 0))],
        out_specs=pl.BlockSpec((1, tm, d), lambda b, i: (b, i, 0)),
        out_shape=jax.ShapeDtypeStruct((bsz, lt, d), F32),
        compiler_params=_cparams(("parallel", "parallel")),
        name="final_norm",
    )(h, g.reshape(1, d))


def _grid_reorder_kernel(x_ref, o_ref, *, n_ctx, rows, to_cols):
    o_ref[0, :n_ctx, :] = x_ref[0, :n_ctx, :]
    for c in range(GRID_W):
        raster = pl.ds(n_ctx + c, rows, stride=GRID_W)
        dense = pl.ds(n_ctx + c * rows, rows)
        if to_cols:
            o_ref.at[0][dense, :] = x_ref.at[0][raster, :]
        else:
            o_ref.at[0][raster, :] = x_ref.at[0][dense, :]


def _grid_reorder(h, n_ctx, to_cols):
    bsz, lt, d = h.shape
    lanes = 128
    spec = pl.BlockSpec((1, lt, lanes), lambda b, j: (b, 0, j))
    return pl.pallas_call(
        functools.partial(_grid_reorder_kernel, n_ctx=n_ctx, rows=(lt - n_ctx) // GRID_W, to_cols=to_cols),
        grid=(bsz, d // lanes),
        in_specs=[spec],
        out_specs=spec,
        out_shape=jax.ShapeDtypeStruct(h.shape, h.dtype),
        compiler_params=_cparams(("parallel", "parallel")),
        name="grid_reorder",
    )(h)


def _chunk_order(d, c, n_ctx_chunks, n_chunks):
    bwd = jnp.where(c < n_ctx_chunks, n_ctx_chunks - 1 - c, n_chunks + n_ctx_chunks - 1 - c)
    return jnp.where(d == 0, c, bwd)


def _split_bf16(x, n):
    out = []
    for _ in range(n):
        p = x.astype(BF16)
        out.append(p)
        x = x - p.astype(F32)
    return out


def _dir_tri(d, t):
    row = lax.broadcasted_iota(jnp.int32, (t, t), 0)
    col = lax.broadcasted_iota(jnp.int32, (t, t), 1)
    return jnp.where(d == 0, col - row, row - col) <= 0


def _gla_kernel(q_ref, k_ref, v_ref, r_ref, w2_ref, b2_ref, o_ref, st_ref, *, t, heads, scale):
    d = pl.program_id(0)
    c = pl.program_id(2)

    @pl.when(c == 0)
    def _():
        st_ref[...] = jnp.zeros_like(st_ref)

    dk = q_ref.shape[-1] // heads
    dv = v_ref.shape[-1] // heads
    nb = q_ref.shape[0]
    mask = _dir_tri(d, t)
    tri = mask.astype(F32)
    mid = t // 2
    items = [(bb, h) for bb in range(nb) for h in range(heads)]
    w_hi, w_lo = _split_bf16(w2_ref[0], 2)
    xs = []
    for bb in range(nb):
        r_hi, r_lo = _split_bf16(r_ref[bb], 2)
        xs.append(jnp.dot(r_hi, w_hi, preferred_element_type=F32) + jnp.dot(r_hi, w_lo, preferred_element_type=F32)
                  + jnp.dot(r_lo, w_hi, preferred_element_type=F32) + b2_ref[0])
    las = [jax.nn.log_sigmoid(x) * (1.0 / GLA_TAU) for x in xs]
    tri_b = tri.astype(BF16)
    bs = [sum(jnp.dot(tri_b, p, preferred_element_type=F32) for p in _split_bf16(la, 3)) for la in las]
    qt, kt, qe, kh_end, e_end = [], [], [], [], []
    for bb in range(nb):
        b = bs[bb]
        b_m = b[mid:mid + 1, :]
        b_end = jnp.where(d == 0, b[t - 1:t, :], b[0:1, :])
        q_s = q_ref[bb] * (jnp.exp(b - b_m) * scale)
        k_s = k_ref[bb] * jnp.exp(b_m - b)
        qe.append((q_s * jnp.exp(b_m)).astype(BF16))
        kh_end.append((k_s * jnp.exp(b_end - b_m)).astype(BF16))
        e_end.append(jnp.exp(b_end))
        qt.append(q_s.astype(BF16))
        kt.append(k_s.astype(BF16))
    att, q_st, vs = {}, {}, {}
    for bb, h in items:
        ks = slice(h * dk, (h + 1) * dk)
        vs[bb, h] = v_ref[bb, :, h * dv:(h + 1) * dv].astype(BF16)
        att[bb, h] = lax.dot_general(qt[bb][:, ks], kt[bb][:, ks], NT, preferred_element_type=F32)
        q_st[bb, h] = lax.dot_general(qe[bb][:, ks], st_ref[bb * heads + h].astype(BF16), NT,
                                      preferred_element_type=F32)
    for bb, h in items:
        a = jnp.where(mask, att[bb, h], 0.0).astype(BF16)
        o_ref[0, bb, :, h * dv:(h + 1) * dv] = jnp.dot(a, vs[bb, h], preferred_element_type=F32) + q_st[bb, h]
    for bb, h in items:
        ks = slice(h * dk, (h + 1) * dk)
        upd = lax.dot_general(vs[bb, h], kh_end[bb][:, ks], TN, preferred_element_type=F32)
        st_ref[bb * heads + h] = st_ref[bb * heads + h] * e_end[bb][:, ks] + upd


def _gla_mixer(q, k, v, r, w2, b2, n_ctx):
    bsz, lt, dkt = q.shape
    dvt = v.shape[-1]
    nr = r.shape[-1]
    t, heads = MIX_CHUNK, GLA_HEADS
    bt = MIX_BATCH
    nch = lt // t
    ncc = n_ctx // t
    dk = dkt // heads
    dv = dvt // heads
    imap = lambda d, b, c: (b, _chunk_order(d, c, ncc, nch), 0)
    return pl.pallas_call(
        functools.partial(_gla_kernel, t=t, heads=heads, scale=dk ** -0.5),
        grid=(2, bsz // bt, nch),
        in_specs=[
            pl.BlockSpec((bt, t, dkt), imap),
            pl.BlockSpec((bt, t, dkt), imap),
            pl.BlockSpec((bt, t, dvt), imap),
            pl.BlockSpec((bt, t, nr), imap),
            pl.BlockSpec((1, nr, dkt), lambda d, b, c: (d, 0, 0)),
            pl.BlockSpec((1, 1, dkt), lambda d, b, c: (d, 0, 0)),
        ],
        out_specs=pl.BlockSpec((1, bt, t, dvt), lambda d, b, c: (d, b, _chunk_order(d, c, ncc, nch), 0)),
        out_shape=jax.ShapeDtypeStruct((2, bsz, lt, dvt), F32),
        scratch_shapes=[pltpu.VMEM((bt * heads, dv, dk), F32)],
        compiler_params=_cparams(("parallel", "parallel", "arbitrary")),
        name="gla_mixer",
    )(q, k, v, r, w2, b2)


def _mlstm_kernel(q_ref, k_ref, v_ref, gc_ref, gr_ref, o_ref, c_ref, n_ref, m_ref, *, t, heads):
    d = pl.program_id(0)
    c = pl.program_id(2)

    @pl.when(c == 0)
    def _():
        c_ref[...] = jnp.zeros_like(c_ref)
        n_ref[...] = jnp.zeros_like(n_ref)
        m_ref[...] = jnp.zeros_like(m_ref)

    dh = q_ref.shape[-1] // heads
    nb = q_ref.shape[0]
    mask = _dir_tri(d, t)
    tri = mask.astype(F32)
    items = [(bb, h) for bb in range(nb) for h in range(heads)]
    gate = []
    for bb in range(nb):
        gc = gc_ref[0, bb]
        gr = gr_ref[0, bb, 0]
        fc = jax.nn.log_sigmoid(gc[:, heads:])
        fr = jax.nn.log_sigmoid(gr[heads:, :])
        b_col = jnp.dot(tri, fc, preferred_element_type=F32, precision=HI)
        b_row = lax.dot_general(fr, tri, NT, preferred_element_type=F32, precision=HI)
        b_last = jnp.where(d == 0, b_col[t - 1:t, :], b_col[0:1, :])
        gate.append((gc[:, :heads], gr[:heads, :], b_col, b_row, b_last))
    qs, ks, vs, s_raw, q_c = {}, {}, {}, {}, {}
    for bb, h in items:
        hs = slice(h * dh, (h + 1) * dh)
        qs[bb, h] = q_ref[bb, :, hs]
        ks[bb, h] = k_ref[bb, :, hs]
        vs[bb, h] = v_ref[bb, :, hs].astype(BF16)
        s_raw[bb, h] = lax.dot_general(qs[bb, h], ks[bb, h], NT, preferred_element_type=F32)
        q_c[bb, h] = jnp.dot(qs[bb, h], c_ref[bb * heads + h].astype(BF16), preferred_element_type=F32)
    logw, log_inter, m_t, w_inter, scores, den, qn = {}, {}, {}, {}, {}, {}, {}
    for bb, h in items:
        _, ir, b_col, b_row, _ = gate[bb]
        bc = b_col[:, h:h + 1]
        logw[bb, h] = jnp.where(mask, bc - b_row[h:h + 1, :] + ir[h:h + 1, :], -jnp.inf)
        log_inter[bb, h] = bc + m_ref[bb * heads + h]
    for bb, h in items:
        m_t[bb, h] = jnp.maximum(log_inter[bb, h], jnp.max(logw[bb, h], axis=-1, keepdims=True))
        qn[bb, h] = jnp.sum(qs[bb, h].astype(F32) * n_ref[bb * heads + h], axis=-1, keepdims=True)
    for bb, h in items:
        w_inter[bb, h] = jnp.exp(log_inter[bb, h] - m_t[bb, h])
        scores[bb, h] = s_raw[bb, h] * jnp.exp(logw[bb, h] - m_t[bb, h])
    for bb, h in items:
        den[bb, h] = jnp.sum(scores[bb, h], axis=-1, keepdims=True) + w_inter[bb, h] * qn[bb, h]
    num = {}
    for bb, h in items:
        num[bb, h] = (jnp.dot(scores[bb, h].astype(BF16), vs[bb, h], preferred_element_type=F32)
                      + w_inter[bb, h] * q_c[bb, h])
    for bb, h in items:
        hs = slice(h * dh, (h + 1) * dh)
        o_ref[0, bb, :, hs] = num[bb, h] / jnp.maximum(jnp.abs(den[bb, h]), jnp.exp(-m_t[bb, h]))
    log_g, m_new, kw, upd, ksum = {}, {}, {}, {}, {}
    for bb, h in items:
        ic, _, b_col, _, b_last = gate[bb]
        log_g[bb, h] = b_last[:, h:h + 1] - b_col[:, h:h + 1] + ic[:, h:h + 1]
    for bb, h in items:
        b_last = gate[bb][4]
        m_new[bb, h] = jnp.maximum(b_last[:, h:h + 1] + m_ref[bb * heads + h],
                                   jnp.max(log_g[bb, h], axis=0, keepdims=True))
    for bb, h in items:
        kw[bb, h] = ks[bb, h].astype(F32) * jnp.exp(log_g[bb, h] - m_new[bb, h])
    for bb, h in items:
        upd[bb, h] = lax.dot_general(kw[bb, h].astype(BF16), vs[bb, h], TN, preferred_element_type=F32)
        ksum[bb, h] = jnp.sum(kw[bb, h], axis=0, keepdims=True)
    for bb, h in items:
        si = bb * heads + h
        b_last = gate[bb][4]
        keep = jnp.exp(b_last[:, h:h + 1] + m_ref[si] - m_new[bb, h])
        c_ref[si] = keep * c_ref[si] + upd[bb, h]
        n_ref[si] = keep * n_ref[si] + ksum[bb, h]
        m_ref[si] = m_new[bb, h]


def _mlstm_mixer(qk, v, gates, n_ctx):
    bsz, lt, w2 = qk.shape
    w = w2 // 2
    t, heads = MLSTM_CHUNK, MLSTM_HEADS
    dh = w // heads
    nch = lt // t
    ncc = n_ctx // t
    gc = gates.reshape(bsz, lt, 2, 2 * heads).transpose(2, 0, 1, 3)
    gr = gc.reshape(2, bsz, nch, t, 2 * heads).transpose(0, 1, 2, 4, 3)
    cmap = lambda d, b, c: _chunk_order(d, c, ncc, nch)
    bt = MIX_BATCH
    return pl.pallas_call(
        functools.partial(_mlstm_kernel, t=t, heads=heads),
        grid=(2, bsz // bt, nch),
        in_specs=[
            pl.BlockSpec((bt, t, w), lambda d, b, c: (b, cmap(d, b, c), 0)),
            pl.BlockSpec((bt, t, w), lambda d, b, c: (b, cmap(d, b, c), 1)),
            pl.BlockSpec((bt, t, w), lambda d, b, c: (b, cmap(d, b, c), 0)),
            pl.BlockSpec((1, bt, t, 2 * heads), lambda d, b, c: (d, b, cmap(d, b, c), 0)),
            pl.BlockSpec((1, bt, 1, 2 * heads, t), lambda d, b, c: (d, b, cmap(d, b, c), 0, 0)),
        ],
        out_specs=pl.BlockSpec((1, bt, t, w), lambda d, b, c: (d, b, cmap(d, b, c), 0)),
        out_shape=jax.ShapeDtypeStruct((2, bsz, lt, w), F32),
        scratch_shapes=[pltpu.VMEM((bt * heads, dh, dh), F32), pltpu.VMEM((bt * heads, 1, dh), F32),
                        pltpu.VMEM((bt * heads, 1, 1), F32)],
        compiler_params=_cparams(("parallel", "parallel", "arbitrary")),
        name="mlstm_mixer",
    )(qk, qk, v, gc, gr)


def _conv_kernel(x_ref, w_ref, b_ref, s_ref, o_ref, *, n_ctx):
    x = x_ref[0]
    lt = x.shape[0]
    row = lax.broadcasted_iota(jnp.int32, x.shape, 0)
    prev = jnp.where((row == 0) | (row == n_ctx), 0.0, pltpu.roll(x, 1, 0))
    nxt = jnp.where((row == n_ctx - 1) | (row == lt - 1), 0.0, pltpu.roll(x, lt - 1, 0))
    y = b_ref[...] + w_ref[0:1, :] * prev + w_ref[1:2, :] * x + w_ref[2:3, :] * nxt
    o_ref[0] = (y * jax.nn.sigmoid(y) * s_ref[...]).astype(o_ref.dtype)


def _conv_silu(x, w, b, colscale, n_ctx):
    bsz, lt, ch = x.shape
    tc = 256
    return pl.pallas_call(
        functools.partial(_conv_kernel, n_ctx=n_ctx),
        grid=(bsz, ch // tc),
        in_specs=[
            pl.BlockSpec((1, lt, tc), lambda b, j: (b, 0, j)),
            pl.BlockSpec((3, tc), lambda b, j: (0, j)),
            pl.BlockSpec((1, tc), lambda b, j: (0, j)),
            pl.BlockSpec((1, tc), lambda b, j: (0, j)),
        ],
        out_specs=pl.BlockSpec((1, lt, tc), lambda b, j: (b, 0, j)),
        out_shape=jax.ShapeDtypeStruct((bsz, lt, ch), BF16),
        compiler_params=_cparams(("parallel", "parallel")),
        name="conv_silu",
    )(x, w, b.reshape(1, ch), colscale.reshape(1, ch))


def _s5_matrices(a_re, a_im, log_dt, b_re, b_im, c_re, c_im, backward, lane_groups=8):
    g, p = a_re.shape
    cg = b_re.shape[-1]
    j = S5_J
    lg = lane_groups
    nq = g // lg
    dt = jnp.exp(log_dt)[:, None]
    lam_re = jnp.minimum(a_re, -1e-4)
    lam_im = a_im
    decay = jnp.exp(lam_re * dt)
    ab_re = decay * jnp.cos(lam_im * dt)
    ab_im = decay * jnp.sin(lam_im * dt)
    den = lam_re * lam_re + lam_im * lam_im
    zr = ((ab_re - 1) * lam_re + ab_im * lam_im) / den
    zi = (ab_im * lam_re - (ab_re - 1) * lam_im) / den
    bb_re = zr[..., None] * b_re - zi[..., None] * b_im
    bb_im = zr[..., None] * b_im + zi[..., None] * b_re
    pw_re, pw_im = [jnp.ones_like(ab_re)], [jnp.zeros_like(ab_im)]
    for _ in range(j):
        r0, i0 = pw_re[-1], pw_im[-1]
        pw_re.append(ab_re * r0 - ab_im * i0)
        pw_im.append(ab_re * i0 + ab_im * r0)
    pw_re, pw_im = jnp.stack(pw_re), jnp.stack(pw_im)
    ca_re = c_re[None] * pw_re[:, :, None, :] - c_im[None] * pw_im[:, :, None, :]
    ca_im = c_re[None] * pw_im[:, :, None, :] + c_im[None] * pw_re[:, :, None, :]
    kk = (jnp.einsum('tgcp,gpd->tgcd', ca_re[:j], bb_re, precision=HI)
          - jnp.einsum('tgcp,gpd->tgcd', ca_im[:j], bb_im, precision=HI))
    ab_pw_re = pw_re[:j, :, :, None] * bb_re[None] - pw_im[:j, :, :, None] * bb_im[None]
    ab_pw_im = pw_re[:j, :, :, None] * bb_im[None] + pw_im[:j, :, :, None] * bb_re[None]
    eye = jnp.eye(lg, dtype=BF16)
    lb = lg * cg
    sw = lg * p
    bd_k = jnp.einsum('tqgcd,gh->tqgdhc', kk.astype(BF16).reshape(j, nq, lg, cg, cg), eye
                      ).reshape(j, nq, lb, lb)
    bd_in = [jnp.einsum('tqgpc,gh->tqgchp', a.astype(BF16).reshape(j, nq, lg, p, cg), eye
                        ).reshape(j, nq, lb, sw) for a in (ab_pw_re, ab_pw_im)]
    bd_out = [jnp.einsum('tqgcp,gh->tqgphc', a.astype(BF16).reshape(j + 1, nq, lg, cg, p), eye
                         ).reshape(j + 1, nq, sw, lb) for a in (ca_re, -ca_im)]
    jj = jnp.arange(j)
    lag = (jj[:, None] - jj[None, :]) if backward else (jj[None, :] - jj[:, None])
    kt = jnp.where((lag >= 0)[:, :, None, None, None], bd_k[jnp.clip(lag, 0, j - 1)], 0)
    ktoep = kt.transpose(2, 0, 3, 1, 4).reshape(nq, j * lb, j * lb)
    tau_in = jj if backward else (j - 1 - jj)
    win_re, win_im = (a[tau_in].transpose(1, 0, 2, 3).reshape(nq, j * lb, sw) for a in bd_in)
    tau_out = (j - jj) if backward else (jj + 1)
    wout_re, wout_im = (a[tau_out].transpose(1, 2, 0, 3).reshape(nq, sw, j * lb) for a in bd_out)
    dec_re = pw_re[j].reshape(nq, 1, sw)
    dec_im = pw_im[j].reshape(nq, 1, sw)
    return ktoep, win_re, win_im, wout_re, wout_im, dec_re, dec_im


def _s5_kernel(u_ref, kt_ref, wir_ref, wii_ref, wor_ref, woi_ref, dr_ref, di_ref, y_ref,
               xf_ref, yf_ref, sre_ref, sim_ref, *, bt, nk, nk_ctx, rs):
    d = pl.program_id(0)
    j = S5_J
    lanes = u_ref.shape[-1]
    for b in range(bt):
        for jj in range(j):
            xf_ref[b * nk:(b + 1) * nk, jj * lanes:(jj + 1) * lanes] = (
                u_ref.at[b][pl.ds(jj, nk, stride=j), :].astype(BF16))
    xf = xf_ref[...]
    yf_ref[...] = jnp.dot(xf, kt_ref[0, 0], preferred_element_type=F32)
    inc_re = jnp.dot(xf, wir_ref[0, 0], preferred_element_type=F32)
    inc_im = jnp.dot(xf, wii_ref[0, 0], preferred_element_type=F32)
    nl = sre_ref.shape[0]
    for b in range(bt):
        for l in range(nl):
            sre_ref[l, b * rs:b * rs + nk, :] = inc_re[b * nk:(b + 1) * nk, l * lanes:(l + 1) * lanes]
            sim_ref[l, b * rs:b * rs + nk, :] = inc_im[b * nk:(b + 1) * nk, l * lanes:(l + 1) * lanes]
    a_re = [dr_ref[0, 0, :, l * lanes:(l + 1) * lanes] for l in range(nl)]
    a_im = [di_ref[0, 0, :, l * lanes:(l + 1) * lanes] for l in range(nl)]

    def step(kidx, carry):
        rows = pl.ds(kidx, bt, stride=rs)
        out = []
        for l in range(nl):
            s_re, s_im = carry[2 * l], carry[2 * l + 1]
            i_re = sre_ref.at[l][rows, :]
            i_im = sim_ref.at[l][rows, :]
            sre_ref.at[l][rows, :] = s_re
            sim_ref.at[l][rows, :] = s_im
            out.append(a_re[l] * s_re - a_im[l] * s_im + i_re)
            out.append(a_re[l] * s_im + a_im[l] * s_re + i_im)
        return tuple(out)

    zero = tuple(jnp.zeros((bt, lanes), F32) for _ in range(2 * nl))

    @pl.when(d == 0)
    def _():
        lax.fori_loop(0, nk, step, zero)

    @pl.when(d == 1)
    def _():
        carry = lax.fori_loop(0, nk_ctx, lambda i, cr: step(nk_ctx - 1 - i, cr), zero)
        lax.fori_loop(0, nk - nk_ctx, lambda i, cr: step(nk - 1 - i, cr), carry)

    for b in range(bt):
        sp_re = jnp.concatenate([sre_ref[l, b * rs:b * rs + nk, :] for l in range(nl)], axis=-1).astype(BF16)
        sp_im = jnp.concatenate([sim_ref[l, b * rs:b * rs + nk, :] for l in range(nl)], axis=-1).astype(BF16)
        yb = (yf_ref[b * nk:(b + 1) * nk, :]
              + jnp.dot(sp_re, wor_ref[0, 0], preferred_element_type=F32)
              + jnp.dot(sp_im, woi_ref[0, 0], preferred_element_type=F32))
        for jj in range(j):
            y_ref.at[0, b][pl.ds(jj, nk, stride=j), :] = yb[:, jj * lanes:(jj + 1) * lanes]


def _s5_mixer(u, mats, n_ctx):
    bsz, lt, w = u.shape
    ktoep, win_re, win_im, wout_re, wout_im, dec_re, dec_im = mats
    lanes = 128
    bt = 4 if bsz % 4 == 0 else 2
    nq = w // lanes
    j = S5_J
    nk = lt // j
    nk_ctx = n_ctx // j
    rs = nk + 8
    fl = j * lanes
    sw = win_re.shape[-1]
    wmap = lambda d, q, b: (d, q, 0, 0)
    return pl.pallas_call(
        functools.partial(_s5_kernel, bt=bt, nk=nk, nk_ctx=nk_ctx, rs=rs),
        grid=(2, nq, bsz // bt),
        in_specs=[
            pl.BlockSpec((bt, lt, lanes), lambda d, q, b: (b, 0, q)),
            pl.BlockSpec((1, 1, fl, fl), wmap),
            pl.BlockSpec((1, 1, fl, sw), wmap),
            pl.BlockSpec((1, 1, fl, sw), wmap),
            pl.BlockSpec((1, 1, sw, fl), wmap),
            pl.BlockSpec((1, 1, sw, fl), wmap),
            pl.BlockSpec((1, 1, 1, sw), wmap),
            pl.BlockSpec((1, 1, 1, sw), wmap),
        ],
        out_specs=pl.BlockSpec((1, bt, lt, lanes), lambda d, q, b: (d, b, 0, q)),
        out_shape=jax.ShapeDtypeStruct((2, bsz, lt, w), F32),
        scratch_shapes=[pltpu.VMEM((bt * nk, fl), BF16), pltpu.VMEM((bt * nk, fl), F32),
                        pltpu.VMEM((sw // lanes, bt * rs, lanes), F32),
                        pltpu.VMEM((sw // lanes, bt * rs, lanes), F32)],
        compiler_params=_cparams(("parallel", "parallel", "arbitrary")),
        name="s5_mixer",
    )(u, ktoep, win_re, win_im, wout_re, wout_im, dec_re, dec_im)


def _head_norm(x, heads):
    dh = x.shape[-1] // heads
    outs = []
    for h in range(heads):
        xh = x[:, h * dh:(h + 1) * dh]
        outs.append(xh * lax.rsqrt(jnp.mean(xh * xh, axis=-1, keepdims=True) + EPS))
    return jnp.concatenate(outs, axis=-1)


def _even_post_kernel(m_ref, o_ref, s_ref, u_ref, mg_ref, dsk_ref, gw_ref, gb_ref, w_ref, h_ref, gate_ref,
                      out_ref, *, heads):
    m = m_ref[0, 0] + m_ref[1, 0]
    m_out = _head_norm(m, heads) * mg_ref[...] * jax.nn.sigmoid(o_ref[0])
    y = jax.nn.gelu(s_ref[0, 0] + s_ref[1, 0] + dsk_ref[...] * u_ref[0])
    glu = jnp.dot(y.astype(BF16), gw_ref[...], preferred_element_type=F32) + gb_ref[...]
    s_out = y * jax.nn.sigmoid(glu)
    cat = jnp.concatenate([m_out, s_out], axis=-1).astype(BF16)
    z = jnp.dot(cat, w_ref[...], preferred_element_type=F32)
    out_ref[0] = h_ref[0] + gate_ref[0, 0] * z


def _even_post(m2, o, s2, u, mnorm_g, d_skip, glu_w, glu_b, w_out, h, gate):
    bsz, lt, d = h.shape
    mw = o.shape[-1]
    sw = u.shape[-1]
    tm = ROW_TILE
    row = lambda b, i: (b, i, 0)
    row2 = lambda b, i: (0, b, i, 0)
    const = lambda b, i: (0, 0)
    return pl.pallas_call(
        functools.partial(_even_post_kernel, heads=MLSTM_HEADS),
        grid=(bsz, lt // tm),
        in_specs=[
            pl.BlockSpec((2, 1, tm, mw), row2),
            pl.BlockSpec((1, tm, mw), row),
            pl.BlockSpec((2, 1, tm, sw), row2),
            pl.BlockSpec((1, tm, sw), row),
            pl.BlockSpec((1, mw), const),
            pl.BlockSpec((1, sw), const),
            pl.BlockSpec((sw, sw), const),
            pl.BlockSpec((1, sw), const),
            pl.BlockSpec((mw + sw, d), const),
            pl.BlockSpec((1, tm, d), row),
            pl.BlockSpec((1, 1, 1, d), _seg_map),
        ],
        out_specs=pl.BlockSpec((1, tm, d), row),
        out_shape=jax.ShapeDtypeStruct((bsz, lt, d), F32),
        compiler_params=_cparams(("parallel", "parallel")),
        name="even_post",
    )(m2, o, s2, u, mnorm_g.reshape(1, mw), d_skip.reshape(1, sw), glu_w.astype(BF16), glu_b.reshape(1, sw),
      w_out.astype(BF16), h, gate)


def _odd_post_kernel(o_ref, g_ref, ng_ref, w_ref, h_ref, gate_ref, out_ref, *, heads):
    g = g_ref[0]
    y = _head_norm(o_ref[0, 0] + o_ref[1, 0], heads) * ng_ref[...] * (g * jax.nn.sigmoid(g))
    z = jnp.dot(y.astype(BF16), w_ref[...], preferred_element_type=F32)
    out_ref[0] = h_ref[0] + gate_ref[0, 0] * z


def _odd_post(o, g, norm_g, w_out, h, gate):
    bsz, lt, d = h.shape
    dv = o.shape[-1]
    tm = ROW_TILE
    row = lambda b, i: (b, i, 0)
    const = lambda b, i: (0, 0)
    return pl.pallas_call(
        functools.partial(_odd_post_kernel, heads=GLA_HEADS),
        grid=(bsz, lt // tm),
        in_specs=[
            pl.BlockSpec((2, 1, tm, dv), lambda b, i: (0, b, i, 0)),
            pl.BlockSpec((1, tm, dv), row),
            pl.BlockSpec((1, dv), const),
            pl.BlockSpec((dv, d), const),
            pl.BlockSpec((1, tm, d), row),
            pl.BlockSpec((1, 1, 1, d), _seg_map),
        ],
        out_specs=pl.BlockSpec((1, tm, d), row),
        out_shape=jax.ShapeDtypeStruct((bsz, lt, d), F32),
        compiler_params=_cparams(("parallel", "parallel")),
        name="odd_post",
    )(o, g, norm_g.reshape(1, dv), w_out.astype(BF16), h, gate)


def kernel(x, c, ctx, c_ctx, mod_w, mod_b, norm_mix_g, norm_ffn_g, ev_w_in, ev_b_in, ev_conv_w, ev_conv_b, ev_mlstm_norm_g, ev_s5_a_re_f, ev_s5_a_im_f, ev_s5_log_dt_f, ev_s5_a_re_b, ev_s5_a_im_b, ev_s5_log_dt_b, ev_s5_b_re, ev_s5_b_im, ev_s5_c_re, ev_s5_c_im, ev_s5_d, ev_s5_glu_w, ev_s5_glu_b, ev_w_out, od_w_in, od_gate_w2_f, od_gate_b2_f, od_gate_w2_b, od_gate_b2_b, od_norm_g, od_w_out, router_w, router_b, moe_w_gu, moe_b_gu, moe_w_down, moe_b_down, final_norm_g):
    bsz, seq, d = x.shape
    n_ctx = ctx.shape[1]
    depth = mod_w.shape[0]
    lt = n_ctx + seq
    assert n_ctx == ROW_TILE and seq % ROW_TILE == 0 and seq % GRID_W == 0

    h = jnp.concatenate([ctx, x], axis=1)
    c_all = jnp.concatenate([c, c_ctx[None, :]], axis=0)
    c_all = jnp.pad(c_all, ((0, (-c_all.shape[0]) % 8), (0, 0)))
    mods = _modulation(c_all, mod_w, mod_b)
    mod_lat = mods[:, :bsz]
    mod_ctx = jnp.broadcast_to(mods[:, bsz:bsz + 1], mod_lat.shape)
    mod6 = jnp.stack([mod_ctx, mod_lat], axis=2).reshape(depth, bsz, 2, 6, 1, d)

    bg_all = moe_b_gu[..., 0::2]
    bu_all = moe_b_gu[..., 1::2]

    mw = ev_conv_w.shape[-1] // 2
    n_gates = 4 * MLSTM_HEADS
    s5w = ev_s5_d.shape[-1]
    dk_t = od_gate_w2_f.shape[-1]
    dv_t = od_norm_g.shape[-1]
    for layer in range(depth):
        last = layer == depth - 1
        j = layer // 2
        m6 = mod6[layer]
        sh1, sc1, g1, sh2, sc2, g2 = (m6[:, :, i] for i in range(6))
        if layer % 2 == 0:
            w_in, b_in = ev_w_in[j], ev_b_in[j]
            cols = jnp.concatenate([jnp.arange(0, 4 * mw), jnp.arange(4 * mw + n_gates, 4 * mw + n_gates + s5w),
                                    jnp.arange(4 * mw, 4 * mw + n_gates)])
            qk_pre, v, o, u, gates = _nm_matmul(h, norm_mix_g[layer], sh1, sc1, w_in[:, cols], b_in[cols],
                                                (2 * mw, mw, mw, s5w, n_gates))
            dh = mw // MLSTM_HEADS
            colscale = jnp.concatenate([jnp.full((mw,), dh ** -0.5, F32), jnp.ones((mw,), F32)])
            qk = _conv_silu(qk_pre, ev_conv_w[j], ev_conv_b[j], colscale, n_ctx)
            m2 = _mlstm_mixer(qk, v, gates, n_ctx)
            shared = (ev_s5_b_re[j], ev_s5_b_im[j], ev_s5_c_re[j], ev_s5_c_im[j])
            mats_f = _s5_matrices(ev_s5_a_re_f[j], ev_s5_a_im_f[j], ev_s5_log_dt_f[j], *shared, backward=False)
            mats_b = _s5_matrices(ev_s5_a_re_b[j], ev_s5_a_im_b[j], ev_s5_log_dt_b[j], *shared, backward=True)
            s2 = _s5_mixer(u, tuple(jnp.stack([a, b]) for a, b in zip(mats_f, mats_b)), n_ctx)
            h = _even_post(m2, o, s2, u, ev_mlstm_norm_g[j], ev_s5_d[j], ev_s5_glu_w[j], ev_s5_glu_b[j],
                           ev_w_out[j], h, g1)
        else:
            hc = _grid_reorder(h, n_ctx, True)
            qq, kk, vv, gg, rr = _nm_matmul(hc, norm_mix_g[layer], sh1, sc1, od_w_in[j],
                                            jnp.zeros((od_w_in.shape[-1],), F32),
                                            (dk_t, dk_t, dv_t, dv_t, 2 * GLA_RANK))
            zero = jnp.zeros_like(od_gate_w2_f[j])
            w2 = jnp.stack([jnp.concatenate([od_gate_w2_f[j], zero], axis=0),
                            jnp.concatenate([zero, od_gate_w2_b[j]], axis=0)])
            b2 = jnp.stack([od_gate_b2_f[j], od_gate_b2_b[j]])[:, None, :]
            o2 = _gla_mixer(qq, kk, vv, rr, w2, b2, n_ctx)
            h = _grid_reorder(_odd_post(o2, gg, od_norm_g[j], od_w_out[j], hc, g1), n_ctx, False)
        f, top_e, gate = _ffn_prep(h, norm_ffn_g[layer], sh2, sc2, router_w[layer], router_b[layer])
        weights = (layer, moe_w_gu, moe_w_down, bg_all[layer], bu_all[layer], moe_b_down[layer])
        h = _moe_layer(h, f, top_e, gate, g2, weights, last, n_ctx)
    return _final_norm(h, final_norm_g)
```

```python
import functools

import jax
import jax.numpy as jnp
from jax import lax
from jax.experimental import pallas as pl
from jax.experimental.pallas import tpu as pltpu

F32 = jnp.float32
BF16 = jnp.bfloat16
HI = lax.Precision.HIGHEST

EPS = 1e-6
GRID_W = 64
MLSTM_HEADS = 4
S5_GROUP = 16
GLA_HEADS = 4
GLA_RANK = 16
GLA_TAU = 16.0
N_EXPERTS = 32
TOP_K = 4
SWIGLU_LIMIT = 7.0
SWIGLU_ALPHA = 1.702

ROW_TILE = 256
MOE_TILE = 512
MIX_CHUNK = 64
MLSTM_CHUNK = 128
MIX_BATCH = 4
S5_J = 8
VMEM_LIMIT = 56 * 1024 * 1024

NT = (((1,), (1,)), ((), ()))
TN = (((0,), (0,)), ((), ()))


def _cparams(sem):
    return pltpu.CompilerParams(dimension_semantics=sem, vmem_limit_bytes=VMEM_LIMIT)


def _mod_kernel(c_ref, w_ref, b_ref, o_ref):
    c = c_ref[...]
    a = c * jax.nn.sigmoid(c)
    o_ref[0] = jnp.dot(a.astype(BF16), w_ref[0].astype(BF16), preferred_element_type=F32) + b_ref[0]


def _modulation(c_all, mod_w, mod_b):
    depth, d, n6 = mod_w.shape
    rows = c_all.shape[0]
    tn = d
    return pl.pallas_call(
        _mod_kernel,
        grid=(depth, n6 // tn),
        in_specs=[
            pl.BlockSpec((rows, d), lambda l, j: (0, 0)),
            pl.BlockSpec((1, d, tn), lambda l, j: (l, 0, j)),
            pl.BlockSpec((1, 1, tn), lambda l, j: (l, 0, j)),
        ],
        out_specs=pl.BlockSpec((1, rows, tn), lambda l, j: (l, 0, j)),
        out_shape=jax.ShapeDtypeStruct((depth, rows, n6), F32),
        compiler_params=_cparams(("arbitrary", "arbitrary")),
        name="modulation",
    )(c_all, mod_w, mod_b.reshape(depth, 1, n6))


def _norm_mod(x, g, sh, sc):
    ms = jnp.mean(x * x, axis=-1, keepdims=True)
    return (x * lax.rsqrt(ms + EPS) * g) * (1.0 + sc) + sh


def _nm_matmul_kernel(x_ref, g_ref, sh_ref, sc_ref, w_ref, b_ref, *out_refs, splits):
    a = _norm_mod(x_ref[0], g_ref[...], sh_ref[0, 0], sc_ref[0, 0])
    z = jnp.dot(a.astype(BF16), w_ref[...], preferred_element_type=F32) + b_ref[...]
    for (lo, hi), o_ref in zip(splits, out_refs):
        o_ref[0] = z[:, lo:hi].astype(o_ref.dtype)


def _wide_tile(rows):
    return next(t for t in (4 * ROW_TILE, 3 * ROW_TILE, 2 * ROW_TILE, ROW_TILE) if rows % t == 0)


def _seg_map(b, i):
    return (b, jnp.minimum(i, 1), 0, 0)


def _nm_matmul(h, g, shift, scale, w, bias, widths):
    bsz, lt, d = h.shape
    p = w.shape[1]
    splits, lo = [], 0
    for wd in widths:
        splits.append((lo, lo + wd))
        lo += wd
    assert lo == p
    tm = ROW_TILE
    return pl.pallas_call(
        functools.partial(_nm_matmul_kernel, splits=tuple(splits)),
        grid=(bsz, lt // tm),
        in_specs=[
            pl.BlockSpec((1, tm, d), lambda b, i: (b, i, 0)),
            pl.BlockSpec((1, d), lambda b, i: (0, 0)),
            pl.BlockSpec((1, 1, 1, d), _seg_map),
            pl.BlockSpec((1, 1, 1, d), _seg_map),
            pl.BlockSpec((d, p), lambda b, i: (0, 0)),
            pl.BlockSpec((1, p), lambda b, i: (0, 0)),
        ],
        out_specs=[pl.BlockSpec((1, tm, wd), lambda b, i: (b, i, 0)) for wd in widths],
        out_shape=[jax.ShapeDtypeStruct((bsz, lt, wd), F32) for wd in widths],
        compiler_params=_cparams(("parallel", "parallel")),
        name="norm_mod_matmul",
    )(h, g.reshape(1, d), shift, scale, w.astype(BF16), bias.reshape(1, p))


def _ffn_prep_kernel(x_ref, g_ref, sh_ref, sc_ref, rw_ref, rb_ref, f_ref, te_ref, gt_ref, *, n_ctx):
    tm = x_ref.shape[1]
    is_ctx = pl.program_id(1) * tm + lax.broadcasted_iota(jnp.int32, (tm, 1), 0) < n_ctx
    a = _norm_mod(x_ref[0], g_ref[...], jnp.where(is_ctx, sh_ref[0, 0], sh_ref[0, 1]),
                  jnp.where(is_ctx, sc_ref[0, 0], sc_ref[0, 1]))
    f_ref[0] = a.astype(f_ref.dtype)
    logits = lax.dot_general(rw_ref[...], a.astype(BF16), NT, preferred_element_type=F32) + rb_ref[...]
    ne = logits.shape[0]
    eidx = lax.broadcasted_iota(jnp.int32, logits.shape, 0)
    work = logits
    vals, idxs = [], []
    for _ in range(TOP_K):
        m = jnp.max(work, axis=0, keepdims=True)
        idx = jnp.min(jnp.where(work == m, eidx, ne), axis=0, keepdims=True)
        vals.append(m)
        idxs.append(idx)
        work = jnp.where(eidx == idx, -jnp.inf, work)
    exps = [jnp.exp(v - vals[0]) for v in vals]
    denom = exps[0]
    for e in exps[1:]:
        denom = denom + e
    for k in range(TOP_K):
        te_ref[0, k:k + 1, :] = idxs[k]
        gt_ref[0, k:k + 1, :] = exps[k] / denom


def _ffn_prep(h, g, shift, scale, router_w, router_b):
    bsz, lt, d = h.shape
    ne = router_w.shape[1]
    tm = _wide_tile(lt)
    both = pl.BlockSpec((1, 2, 1, d), lambda b, i: (b, 0, 0, 0))
    return pl.pallas_call(
        functools.partial(_ffn_prep_kernel, n_ctx=ROW_TILE),
        grid=(bsz, lt // tm),
        in_specs=[
            pl.BlockSpec((1, tm, d), lambda b, i: (b, i, 0)),
            pl.BlockSpec((1, d), lambda b, i: (0, 0)),
            both,
            both,
            pl.BlockSpec((ne, d), lambda b, i: (0, 0)),
            pl.BlockSpec((ne, 1), lambda b, i: (0, 0)),
        ],
        out_specs=[
            pl.BlockSpec((1, tm, d), lambda b, i: (b, i, 0)),
            pl.BlockSpec((1, TOP_K, tm), lambda b, i: (b, 0, i)),
            pl.BlockSpec((1, TOP_K, tm), lambda b, i: (b, 0, i)),
        ],
        out_shape=[
            jax.ShapeDtypeStruct((bsz, lt, d), BF16),
            jax.ShapeDtypeStruct((bsz, TOP_K, lt), jnp.int32),
            jax.ShapeDtypeStruct((bsz, TOP_K, lt), F32),
        ],
        compiler_params=_cparams(("parallel", "parallel")),
        name="ffn_prep",
    )(h, g.reshape(1, d), shift, scale, router_w.T.astype(BF16), router_b.reshape(ne, 1))


GU_BLOCK = 256


def _moe_kernel(be_ref, nb_ref, first_ref, slot_ref, nxt_ref, x_ref, wgu_hbm, wd_hbm, bg_ref, bu_ref, bd_ref,
                y_prev_ref, o_ref, wgu_buf, wd_buf, wgu_s, wd_s, sem, *, layer):
    del y_prev_ref
    i = pl.program_id(0)
    active = i < nb_ref[0]
    half = GU_BLOCK // 2
    nblk = wgu_s.shape[1] // GU_BLOCK

    def weight_copies(e, slot):
        return (pltpu.make_async_copy(wgu_hbm.at[layer, e], wgu_buf.at[slot], sem.at[0, slot]),
                pltpu.make_async_copy(wd_hbm.at[layer, e], wd_buf.at[slot], sem.at[1, slot]))

    @pl.when(active & (i == 0))
    def _():
        for cp in weight_copies(be_ref[0], 0):
            cp.start()

    @pl.when(active & (first_ref[i] == 1))
    def _():
        slot = slot_ref[i]
        for cp in weight_copies(be_ref[i], slot):
            cp.wait()

        @pl.when(nxt_ref[i] >= 0)
        def _():
            for cp in weight_copies(nxt_ref[i], 1 - slot):
                cp.start()

        r = lax.broadcasted_iota(jnp.int32, (GU_BLOCK, GU_BLOCK), 0)
        c = lax.broadcasted_iota(jnp.int32, (GU_BLOCK, GU_BLOCK), 1)
        perm = (r == jnp.where(c < half, 2 * c, 2 * (c - half) + 1)).astype(BF16)
        for k in range(nblk):
            cs = slice(k * GU_BLOCK, (k + 1) * GU_BLOCK)
            wgu_s[:, cs] = jnp.dot(wgu_buf[slot, :, cs].astype(BF16), perm,
                                   preferred_element_type=F32).astype(BF16)
        wd_s[...] = wd_buf[slot].astype(BF16)

    @pl.when(active)
    def _():
        gu = jnp.dot(x_ref[...], wgu_s[...], preferred_element_type=F32)
        hdn = []
        for k in range(nblk):
            hs = slice(k * half, (k + 1) * half)
            g = gu[:, k * GU_BLOCK:k * GU_BLOCK + half] + bg_ref[0, :, hs]
            u = gu[:, k * GU_BLOCK + half:(k + 1) * GU_BLOCK] + bu_ref[0, :, hs]
            g = jnp.minimum(g, SWIGLU_LIMIT)
            u = jnp.clip(u, -SWIGLU_LIMIT, SWIGLU_LIMIT)
            hdn.append(((u + 1.0) * (g * jax.nn.sigmoid(SWIGLU_ALPHA * g))).astype(BF16))
        hdn = jnp.concatenate(hdn, axis=-1)
        o_ref[...] = (jnp.dot(hdn, wd_s[...], preferred_element_type=F32) + bd_ref[0]).astype(o_ref.dtype)

    @pl.when(jnp.logical_not(active))
    def _():
        o_ref[...] = jnp.zeros_like(o_ref)


def _moe_experts(x_sorted, block_expert, n_used, layer, w_gu, w_down, bg, bu, bd, block_off, y_prev):
    n_rows, d = x_sorted.shape
    _, ne, _, f2 = w_gu.shape
    f = f2 // 2
    tm = MOE_TILE
    n_blocks = n_rows // tm
    assert f2 % GU_BLOCK == 0
    blk = jnp.arange(n_blocks, dtype=jnp.int32)
    prev = jnp.concatenate([block_expert[:1], block_expert[:-1]])
    first = (blk < n_used[0]) & ((blk == 0) | (block_expert != prev))
    slot = (jnp.cumsum(first.astype(jnp.int32)) - 1) & 1
    first_idx = jnp.where(first, blk, n_blocks)
    next_first = lax.cummin(first_idx, axis=0, reverse=True)
    next_first = jnp.concatenate([next_first[1:], jnp.full((1,), n_blocks, jnp.int32)])
    nxt = jnp.where(next_first < n_blocks, block_expert[jnp.minimum(next_first, n_blocks - 1)], -1)
    bmap = lambda i, be, nb, fi, sl, nx: (be[i], 0, 0)
    rmap = lambda i, be, nb, fi, sl, nx: (i, 0)
    grid_spec = pltpu.PrefetchScalarGridSpec(
        num_scalar_prefetch=5,
        grid=(n_blocks,),
        in_specs=[
            pl.BlockSpec((tm, d), rmap),
            pl.BlockSpec(memory_space=pl.ANY),
            pl.BlockSpec(memory_space=pl.ANY),
            pl.BlockSpec((1, 1, f), bmap),
            pl.BlockSpec((1, 1, f), bmap),
            pl.BlockSpec((1, 1, d), bmap),
            pl.BlockSpec(memory_space=pl.ANY),
        ],
        out_specs=pl.BlockSpec((tm, d), lambda i, be, nb, fi, sl, nx: (i + block_off, 0)),
        scratch_shapes=[pltpu.VMEM((2, d, f2), F32), pltpu.VMEM((2, f, d), F32),
                        pltpu.VMEM((d, f2), BF16), pltpu.VMEM((f, d), BF16),
                        pltpu.SemaphoreType.DMA((2, 2))],
    )
    operands = (block_expert, n_used, first.astype(jnp.int32), slot.astype(jnp.int32), nxt.astype(jnp.int32),
                x_sorted, w_gu, w_down, bg.reshape(ne, 1, f), bu.reshape(ne, 1, f), bd.reshape(ne, 1, d), y_prev)
    return pl.pallas_call(
        functools.partial(_moe_kernel, layer=layer),
        grid_spec=grid_spec,
        out_shape=jax.ShapeDtypeStruct(y_prev.shape, y_prev.dtype),
        input_output_aliases={len(operands) - 1: 0},
        compiler_params=_cparams(("arbitrary",)),
        name="moe_experts",
    )(*operands)


def _combine_kernel(y_ref, gt_ref, h_ref, g2_ref, o_ref, *, n_ctx):
    gt = gt_ref[0]
    acc = y_ref[0, 0].astype(F32) * gt[:, 0:1]
    for k in range(1, TOP_K):
        acc = acc + y_ref[k, 0].astype(F32) * gt[:, k:k + 1]
    tm = acc.shape[0]
    is_ctx = pl.program_id(1) * tm + lax.broadcasted_iota(jnp.int32, (tm, 1), 0) < n_ctx
    o_ref[0] = h_ref[0] + jnp.where(is_ctx, g2_ref[0, 0], g2_ref[0, 1]) * acc


def _moe_combine(yg, gate, h, g2, lat_only):
    k, bsz, lt, d = yg.shape
    tm = ROW_TILE if lat_only else _wide_tile(lt)
    tile_off = 1 if lat_only else 0
    return pl.pallas_call(
        functools.partial(_combine_kernel, n_ctx=0 if lat_only else ROW_TILE),
        grid=(bsz, lt // tm),
        in_specs=[
            pl.BlockSpec((k, 1, tm, d), lambda b, i: (0, b, i, 0)),
            pl.BlockSpec((1, tm, k), lambda b, i: (b, i, 0)),
            pl.BlockSpec((1, tm, d), lambda b, i: (b, i + tile_off, 0)),
            pl.BlockSpec((1, 2, 1, d), lambda b, i: (b, 0, 0, 0)),
        ],
        out_specs=pl.BlockSpec((1, tm, d), lambda b, i: (b, i, 0)),
        out_shape=jax.ShapeDtypeStruct((bsz, lt, d), F32),
        compiler_params=_cparams(("parallel", "parallel")),
        name="moe_combine",
    )(yg, gate, h, g2)


def _route_kernel(e_ref, pos_ref, cnt_ref, *, tm):
    nk, rows, lanes = e_ref.shape
    li = lax.broadcasted_iota(jnp.int32, (lanes, lanes), 0)
    lj = lax.broadcasted_iota(jnp.int32, (lanes, lanes), 1)
    before_lane = (li < lj).astype(BF16)
    ones = jnp.ones((lanes, lanes), BF16)
    ri = lax.broadcasted_iota(jnp.int32, (rows, rows), 0)
    rj = lax.broadcasted_iota(jnp.int32, (rows, rows), 1)
    before_row = (rj < ri).astype(BF16)
    lane = lax.broadcasted_iota(jnp.int32, (1, lanes), 1)
    xs = [e_ref[k] for k in range(nk)]
    pos = [jnp.zeros((rows, lanes), F32) for _ in range(nk)]
    counts = jnp.zeros((1, lanes), F32)
    pad_off = jnp.zeros((1, lanes), F32)
    for e in range(N_EXPERTS):
        ms = [x == e for x in xs]
        hit = ms[0]
        for m in ms[1:]:
            hit = hit | m
        mb = hit.astype(BF16)
        in_row = jnp.dot(mb, before_lane, preferred_element_type=F32)
        row_sum = jnp.dot(mb, ones, preferred_element_type=F32)
        row_off = jnp.dot(before_row, row_sum.astype(BF16), preferred_element_type=F32)
        count = row_off[rows - 1:rows, :] + row_sum[rows - 1:rows, :]
        dest = in_row + row_off + pad_off
        pos = [p + jnp.where(m, dest, 0.0) for p, m in zip(pos, ms)]
        counts = jnp.where(lane == e, count, counts)
        pad_off = pad_off + jnp.floor((count + (tm - 1)) * (1.0 / tm)) * tm
    for k in range(nk):
        pos_ref[k] = pos[k].astype(jnp.int32)
    cnt_ref[...] = counts.astype(jnp.int32)


def _route_positions(top_e, tm):
    lanes = 128
    nk, n = top_e.shape
    rows = n // lanes
    pos, counts = pl.pallas_call(
        functools.partial(_route_kernel, tm=tm),
        out_shape=[jax.ShapeDtypeStruct((nk, rows, lanes), jnp.int32), jax.ShapeDtypeStruct((1, lanes), jnp.int32)],
        compiler_params=pltpu.CompilerParams(vmem_limit_bytes=VMEM_LIMIT),
        name="route_positions",
    )(top_e.reshape(nk, rows, lanes))
    return pos.reshape(nk, n), counts[0, :N_EXPERTS]


def _moe_rows(n_tokens):
    n_assign = n_tokens * TOP_K
    return -(-(n_assign + N_EXPERTS * (MOE_TILE - 1)) // MOE_TILE) * MOE_TILE


def _moe_layer(h, f, top_e, gate, g2, weights, lat_only, n_ctx, y_buf):
    layer, w_gu, w_down, bg, bu, bd = weights
    bsz, lt, d = h.shape
    skip = n_ctx if lat_only else 0
    ltok = lt - skip
    top_e, gate = top_e[:, :, skip:], gate[:, :, skip:]
    n = bsz * ltok
    n_assign = n * TOP_K
    tm = MOE_TILE
    te = top_e.transpose(1, 0, 2).reshape(TOP_K, n).astype(jnp.int32)
    bits = max(n - 1, 1).bit_length()
    assert N_EXPERTS << bits < 2 ** 31
    keys = (te << bits) | jnp.arange(n, dtype=jnp.int32)[None, :]
    slot_token = lax.sort(keys.reshape(-1)) & ((1 << bits) - 1)
    pos, counts = _route_positions(te, tm)
    start = jnp.cumsum(counts) - counts
    padded = (counts + tm - 1) // tm * tm
    pad_end = jnp.cumsum(padded)
    pad_start = pad_end - padded
    n_rows = y_buf.shape[0]
    n_blocks = n_rows // tm
    assert n_rows >= _moe_rows(n)
    block_expert = jnp.minimum(
        jnp.searchsorted(pad_end, jnp.arange(n_blocks, dtype=jnp.int32) * tm, side='right', method='compare_all'),
        N_EXPERTS - 1).astype(jnp.int32)
    n_used = (pad_end[-1] // tm).astype(jnp.int32).reshape(1)
    row = jnp.arange(n_rows, dtype=jnp.int32).reshape(n_blocks, tm)
    blk_shift = (start - pad_start)[block_expert][:, None]
    blk_end = (pad_start + counts)[block_expert][:, None]
    slot = jnp.clip(row + blk_shift, 0, n_assign - 1).reshape(-1)
    row_token = jnp.where((row < blk_end).reshape(-1),
                          slot_token.at[slot].get(mode='promise_in_bounds'), row.reshape(-1) % n)
    row_src = row_token + skip * (row_token // ltok + 1)
    f2d = f.reshape(bsz * lt, d)
    hb = n_blocks // 2
    y = y_buf
    for lo, hi in ((0, hb), (hb, n_blocks)):
        x_part = f2d.at[row_src[lo * tm:hi * tm]].get(mode='promise_in_bounds')
        y = _moe_experts(x_part, block_expert[lo:hi], jnp.clip(n_used - lo, 0, hi - lo), layer,
                         w_gu, w_down, bg, bu, bd, lo, y)
    yg = y.at[pos.reshape(-1)].get(mode='promise_in_bounds').reshape(TOP_K, bsz, ltok, d)
    return _moe_combine(yg, gate.transpose(0, 2, 1), h, g2, lat_only), y


def _rmsnorm_kernel(x_ref, g_ref, o_ref):
    x = x_ref[0]
    ms = jnp.mean(x * x, axis=-1, keepdims=True)
    o_ref[0] = x * lax.rsqrt(ms + EPS) * g_ref[...]


def _final_norm(h, g):
    bsz, lt, d = h.shape
    tm = _wide_tile(lt)
    return pl.pallas_call(
        _rmsnorm_kernel,
        grid=(bsz, lt // tm),
        in_specs=[pl.BlockSpec((1, tm, d), lambda b, i: (b, i, 0)),
                  pl.BlockSpec((1, d), lambda b, i: (0, 0))],
        out_specs=pl.BlockSpec((1, tm, d), lambda b, i: (b, i, 0)),
        out_shape=jax.ShapeDtypeStruct((bsz, lt, d), F32),
        compiler_params=_cparams(("parallel", "parallel")),
        name="final_norm",
    )(h, g.reshape(1, d))


def _grid_reorder_kernel(x_ref, o_ref, *, n_ctx, rows, to_cols):
    o_ref[0, :n_ctx, :] = x_ref[0, :n_ctx, :]
    for c in range(GRID_W):
        raster = pl.ds(n_ctx + c, rows, stride=GRID_W)
        dense = pl.ds(n_ctx + c * rows, rows)
        if to_cols:
            o_ref.at[0][dense, :] = x_ref.at[0][raster, :]
        else:
            o_ref.at[0][raster, :] = x_ref.at[0][dense, :]


def _grid_reorder(h, n_ctx, to_cols):
    bsz, lt, d = h.shape
    lanes = 128
    spec = pl.BlockSpec((1, lt, lanes), lambda b, j: (b, 0, j))
    return pl.pallas_call(
        functools.partial(_grid_reorder_kernel, n_ctx=n_ctx, rows=(lt - n_ctx) // GRID_W, to_cols=to_cols),
        grid=(bsz, d // lanes),
        in_specs=[spec],
        out_specs=spec,
        out_shape=jax.ShapeDtypeStruct(h.shape, h.dtype),
        compiler_params=_cparams(("parallel", "parallel")),
        name="grid_reorder",
    )(h)


def _chunk_order(d, c, n_ctx_chunks, n_chunks):
    bwd = jnp.where(c < n_ctx_chunks, n_ctx_chunks - 1 - c, n_chunks + n_ctx_chunks - 1 - c)
    return jnp.where(d == 0, c, bwd)


def _split_bf16(x, n):
    out = []
    for _ in range(n):
        p = x.astype(BF16)
        out.append(p)
        x = x - p.astype(F32)
    return out


def _dir_tri(d, t):
    row = lax.broadcasted_iota(jnp.int32, (t, t), 0)
    col = lax.broadcasted_iota(jnp.int32, (t, t), 1)
    return jnp.where(d == 0, col - row, row - col) <= 0


def _gla_kernel(q_ref, k_ref, v_ref, r_ref, w2_ref, b2_ref, o_ref, st_ref, *, t, heads, scale):
    d = pl.program_id(0)
    c = pl.program_id(2)

    @pl.when(c == 0)
    def _():
        st_ref[...] = jnp.zeros_like(st_ref)

    dk = q_ref.shape[-1] // heads
    dv = v_ref.shape[-1] // heads
    nb = q_ref.shape[0]
    mask = _dir_tri(d, t)
    tri = mask.astype(F32)
    mid = t // 2
    items = [(bb, h) for bb in range(nb) for h in range(heads)]
    w_hi, w_lo = _split_bf16(w2_ref[0], 2)
    xs = []
    for bb in range(nb):
        r_hi, r_lo = _split_bf16(r_ref[bb], 2)
        xs.append(jnp.dot(r_hi, w_hi, preferred_element_type=F32) + jnp.dot(r_hi, w_lo, preferred_element_type=F32)
                  + jnp.dot(r_lo, w_hi, preferred_element_type=F32) + b2_ref[0])
    las = [jax.nn.log_sigmoid(x) * (1.0 / GLA_TAU) for x in xs]
    tri_b = tri.astype(BF16)
    bs = [sum(jnp.dot(tri_b, p, preferred_element_type=F32) for p in _split_bf16(la, 3)) for la in las]
    qt, kt, qe, kh_end, e_end = [], [], [], [], []
    for bb in range(nb):
        b = bs[bb]
        b_m = b[mid:mid + 1, :]
        b_end = jnp.where(d == 0, b[t - 1:t, :], b[0:1, :])
        q_s = q_ref[bb] * (jnp.exp(b - b_m) * scale)
        k_s = k_ref[bb] * jnp.exp(b_m - b)
        qe.append((q_s * jnp.exp(b_m)).astype(BF16))
        kh_end.append((k_s * jnp.exp(b_end - b_m)).astype(BF16))
        e_end.append(jnp.exp(b_end))
        qt.append(q_s.astype(BF16))
        kt.append(k_s.astype(BF16))
    att, q_st, vs = {}, {}, {}
    for bb, h in items:
        ks = slice(h * dk, (h + 1) * dk)
        vs[bb, h] = v_ref[bb, :, h * dv:(h + 1) * dv].astype(BF16)
        att[bb, h] = lax.dot_general(qt[bb][:, ks], kt[bb][:, ks], NT, preferred_element_type=F32)
        q_st[bb, h] = lax.dot_general(qe[bb][:, ks], st_ref[bb * heads + h].astype(BF16), NT,
                                      preferred_element_type=F32)
    for bb, h in items:
        a = jnp.where(mask, att[bb, h], 0.0).astype(BF16)
        o_ref[0, bb, :, h * dv:(h + 1) * dv] = jnp.dot(a, vs[bb, h], preferred_element_type=F32) + q_st[bb, h]
    for bb, h in items:
        ks = slice(h * dk, (h + 1) * dk)
        upd = lax.dot_general(vs[bb, h], kh_end[bb][:, ks], TN, preferred_element_type=F32)
        st_ref[bb * heads + h] = st_ref[bb * heads + h] * e_end[bb][:, ks] + upd


def _gla_mixer(q, k, v, r, w2, b2, n_ctx):
    bsz, lt, dkt = q.shape
    dvt = v.shape[-1]
    nr = r.shape[-1]
    t, heads = MIX_CHUNK, GLA_HEADS
    bt = MIX_BATCH
    nch = lt // t
    ncc = n_ctx // t
    dk = dkt // heads
    dv = dvt // heads
    imap = lambda d, b, c: (b, _chunk_order(d, c, ncc, nch), 0)
    return pl.pallas_call(
        functools.partial(_gla_kernel, t=t, heads=heads, scale=dk ** -0.5),
        grid=(2, bsz // bt, nch),
        in_specs=[
            pl.BlockSpec((bt, t, dkt), imap),
            pl.BlockSpec((bt, t, dkt), imap),
            pl.BlockSpec((bt, t, dvt), imap),
            pl.BlockSpec((bt, t, nr), imap),
            pl.BlockSpec((1, nr, dkt), lambda d, b, c: (d, 0, 0)),
            pl.BlockSpec((1, 1, dkt), lambda d, b, c: (d, 0, 0)),
        ],
        out_specs=pl.BlockSpec((1, bt, t, dvt), lambda d, b, c: (d, b, _chunk_order(d, c, ncc, nch), 0)),
        out_shape=jax.ShapeDtypeStruct((2, bsz, lt, dvt), F32),
        scratch_shapes=[pltpu.VMEM((bt * heads, dv, dk), F32)],
        compiler_params=_cparams(("parallel", "parallel", "arbitrary")),
        name="gla_mixer",
    )(q, k, v, r, w2, b2)


def _mlstm_kernel(q_ref, k_ref, v_ref, gc_ref, gr_ref, o_ref, c_ref, n_ref, m_ref, *, t, heads):
    d = pl.program_id(0)
    c = pl.program_id(2)

    @pl.when(c == 0)
    def _():
        c_ref[...] = jnp.zeros_like(c_ref)
        n_ref[...] = jnp.zeros_like(n_ref)
        m_ref[...] = jnp.zeros_like(m_ref)

    dh = q_ref.shape[-1] // heads
    nb = q_ref.shape[0]
    mask = _dir_tri(d, t)
    tri = mask.astype(F32)
    items = [(bb, h) for bb in range(nb) for h in range(heads)]
    gate = []
    for bb in range(nb):
        gc = gc_ref[0, bb]
        gr = gr_ref[0, bb, 0]
        fc = jax.nn.log_sigmoid(gc[:, heads:])
        fr = jax.nn.log_sigmoid(gr[heads:, :])
        b_col = jnp.dot(tri, fc, preferred_element_type=F32, precision=HI)
        b_row = lax.dot_general(fr, tri, NT, preferred_element_type=F32, precision=HI)
        b_last = jnp.where(d == 0, b_col[t - 1:t, :], b_col[0:1, :])
        gate.append((gc[:, :heads], gr[:heads, :], b_col, b_row, b_last))
    qs, ks, vs, s_raw, q_c = {}, {}, {}, {}, {}
    for bb, h in items:
        hs = slice(h * dh, (h + 1) * dh)
        qs[bb, h] = q_ref[bb, :, hs]
        ks[bb, h] = k_ref[bb, :, hs]
        vs[bb, h] = v_ref[bb, :, hs].astype(BF16)
        s_raw[bb, h] = lax.dot_general(qs[bb, h], ks[bb, h], NT, preferred_element_type=F32)
        q_c[bb, h] = jnp.dot(qs[bb, h], c_ref[bb * heads + h].astype(BF16), preferred_element_type=F32)
    logw, log_inter, m_t, w_inter, scores, den, qn = {}, {}, {}, {}, {}, {}, {}
    for bb, h in items:
        _, ir, b_col, b_row, _ = gate[bb]
        bc = b_col[:, h:h + 1]
        logw[bb, h] = jnp.where(mask, bc - b_row[h:h + 1, :] + ir[h:h + 1, :], -jnp.inf)
        log_inter[bb, h] = bc + m_ref[bb * heads + h]
    for bb, h in items:
        m_t[bb, h] = jnp.maximum(log_inter[bb, h], jnp.max(logw[bb, h], axis=-1, keepdims=True))
        qn[bb, h] = jnp.sum(qs[bb, h].astype(F32) * n_ref[bb * heads + h], axis=-1, keepdims=True)
    for bb, h in items:
        w_inter[bb, h] = jnp.exp(log_inter[bb, h] - m_t[bb, h])
        scores[bb, h] = s_raw[bb, h] * jnp.exp(logw[bb, h] - m_t[bb, h])
    for bb, h in items:
        den[bb, h] = jnp.sum(scores[bb, h], axis=-1, keepdims=True) + w_inter[bb, h] * qn[bb, h]
    num = {}
    for bb, h in items:
        num[bb, h] = (jnp.dot(scores[bb, h].astype(BF16), vs[bb, h], preferred_element_type=F32)
                      + w_inter[bb, h] * q_c[bb, h])
    for bb, h in items:
        hs = slice(h * dh, (h + 1) * dh)
        o_ref[0, bb, :, hs] = num[bb, h] / jnp.maximum(jnp.abs(den[bb, h]), jnp.exp(-m_t[bb, h]))
    log_g, m_new, kw, upd, ksum = {}, {}, {}, {}, {}
    for bb, h in items:
        ic, _, b_col, _, b_last = gate[bb]
        log_g[bb, h] = b_last[:, h:h + 1] - b_col[:, h:h + 1] + ic[:, h:h + 1]
    for bb, h in items:
        b_last = gate[bb][4]
        m_new[bb, h] = jnp.maximum(b_last[:, h:h + 1] + m_ref[bb * heads + h],
                                   jnp.max(log_g[bb, h], axis=0, keepdims=True))
    for bb, h in items:
        kw[bb, h] = ks[bb, h].astype(F32) * jnp.exp(log_g[bb, h] - m_new[bb, h])
    for bb, h in items:
        upd[bb, h] = lax.dot_general(kw[bb, h].astype(BF16), vs[bb, h], TN, preferred_element_type=F32)
        ksum[bb, h] = jnp.sum(kw[bb, h], axis=0, keepdims=True)
    for bb, h in items:
        si = bb * heads + h
        b_last = gate[bb][4]
        keep = jnp.exp(b_last[:, h:h + 1] + m_ref[si] - m_new[bb, h])
        c_ref[si] = keep * c_ref[si] + upd[bb, h]
        n_ref[si] = keep * n_ref[si] + ksum[bb, h]
        m_ref[si] = m_new[bb, h]


def _mlstm_mixer(qk, v, gates, n_ctx):
    bsz, lt, w2 = qk.shape
    w = w2 // 2
    t, heads = MLSTM_CHUNK, MLSTM_HEADS
    dh = w // heads
    nch = lt // t
    ncc = n_ctx // t
    gc = gates.reshape(bsz, lt, 2, 2 * heads).transpose(2, 0, 1, 3)
    gr = gc.reshape(2, bsz, nch, t, 2 * heads).transpose(0, 1, 2, 4, 3)
    cmap = lambda d, b, c: _chunk_order(d, c, ncc, nch)
    bt = MIX_BATCH
    return pl.pallas_call(
        functools.partial(_mlstm_kernel, t=t, heads=heads),
        grid=(2, bsz // bt, nch),
        in_specs=[
            pl.BlockSpec((bt, t, w), lambda d, b, c: (b, cmap(d, b, c), 0)),
            pl.BlockSpec((bt, t, w), lambda d, b, c: (b, cmap(d, b, c), 1)),
            pl.BlockSpec((bt, t, w), lambda d, b, c: (b, cmap(d, b, c), 0)),
            pl.BlockSpec((1, bt, t, 2 * heads), lambda d, b, c: (d, b, cmap(d, b, c), 0)),
            pl.BlockSpec((1, bt, 1, 2 * heads, t), lambda d, b, c: (d, b, cmap(d, b, c), 0, 0)),
        ],
        out_specs=pl.BlockSpec((1, bt, t, w), lambda d, b, c: (d, b, cmap(d, b, c), 0)),
        out_shape=jax.ShapeDtypeStruct((2, bsz, lt, w), F32),
        scratch_shapes=[pltpu.VMEM((bt * heads, dh, dh), F32), pltpu.VMEM((bt * heads, 1, dh), F32),
                        pltpu.VMEM((bt * heads, 1, 1), F32)],
        compiler_params=_cparams(("parallel", "parallel", "arbitrary")),
        name="mlstm_mixer",
    )(qk, qk, v, gc, gr)


def _conv_kernel(x_ref, w_ref, b_ref, s_ref, o_ref, *, n_ctx):
    x = x_ref[0]
    lt = x.shape[0]
    row = lax.broadcasted_iota(jnp.int32, x.shape, 0)
    prev = jnp.where((row == 0) | (row == n_ctx), 0.0, pltpu.roll(x, 1, 0))
    nxt = jnp.where((row == n_ctx - 1) | (row == lt - 1), 0.0, pltpu.roll(x, lt - 1, 0))
    y = b_ref[...] + w_ref[0:1, :] * prev + w_ref[1:2, :] * x + w_ref[2:3, :] * nxt
    o_ref[0] = (y * jax.nn.sigmoid(y) * s_ref[...]).astype(o_ref.dtype)


def _conv_silu(x, w, b, colscale, n_ctx):
    bsz, lt, ch = x.shape
    tc = 256
    return pl.pallas_call(
        functools.partial(_conv_kernel, n_ctx=n_ctx),
        grid=(bsz, ch // tc),
        in_specs=[
            pl.BlockSpec((1, lt, tc), lambda b, j: (b, 0, j)),
            pl.BlockSpec((3, tc), lambda b, j: (0, j)),
            pl.BlockSpec((1, tc), lambda b, j: (0, j)),
            pl.BlockSpec((1, tc), lambda b, j: (0, j)),
        ],
        out_specs=pl.BlockSpec((1, lt, tc), lambda b, j: (b, 0, j)),
        out_shape=jax.ShapeDtypeStruct((bsz, lt, ch), BF16),
        compiler_params=_cparams(("parallel", "parallel")),
        name="conv_silu",
    )(x, w, b.reshape(1, ch), colscale.reshape(1, ch))


def _s5_matrices(a_re, a_im, log_dt, b_re, b_im, c_re, c_im, backward, lane_groups=8):
    g, p = a_re.shape
    cg = b_re.shape[-1]
    j = S5_J
    lg = lane_groups
    nq = g // lg
    dt = jnp.exp(log_dt)[:, None]
    lam_re = jnp.minimum(a_re, -1e-4)
    lam_im = a_im
    decay = jnp.exp(lam_re * dt)
    ab_re = decay * jnp.cos(lam_im * dt)
    ab_im = decay * jnp.sin(lam_im * dt)
    den = lam_re * lam_re + lam_im * lam_im
    zr = ((ab_re - 1) * lam_re + ab_im * lam_im) / den
    zi = (ab_im * lam_re - (ab_re - 1) * lam_im) / den
    bb_re = zr[..., None] * b_re - zi[..., None] * b_im
    bb_im = zr[..., None] * b_im + zi[..., None] * b_re
    pw_re, pw_im = [jnp.ones_like(ab_re)], [jnp.zeros_like(ab_im)]
    for _ in range(j):
        r0, i0 = pw_re[-1], pw_im[-1]
        pw_re.append(ab_re * r0 - ab_im * i0)
        pw_im.append(ab_re * i0 + ab_im * r0)
    pw_re, pw_im = jnp.stack(pw_re), jnp.stack(pw_im)
    ca_re = c_re[None] * pw_re[:, :, None, :] - c_im[None] * pw_im[:, :, None, :]
    ca_im = c_re[None] * pw_im[:, :, None, :] + c_im[None] * pw_re[:, :, None, :]
    kk = (jnp.einsum('tgcp,gpd->tgcd', ca_re[:j], bb_re, precision=HI)
          - jnp.einsum('tgcp,gpd->tgcd', ca_im[:j], bb_im, precision=HI))
    ab_pw_re = pw_re[:j, :, :, None] * bb_re[None] - pw_im[:j, :, :, None] * bb_im[None]
    ab_pw_im = pw_re[:j, :, :, None] * bb_im[None] + pw_im[:j, :, :, None] * bb_re[None]
    eye = jnp.eye(lg, dtype=BF16)
    lb = lg * cg
    sw = lg * p
    bd_k = jnp.einsum('tqgcd,gh->tqgdhc', kk.astype(BF16).reshape(j, nq, lg, cg, cg), eye
                      ).reshape(j, nq, lb, lb)
    bd_in = [jnp.einsum('tqgpc,gh->tqgchp', a.astype(BF16).reshape(j, nq, lg, p, cg), eye
                        ).reshape(j, nq, lb, sw) for a in (ab_pw_re, ab_pw_im)]
    bd_out = [jnp.einsum('tqgcp,gh->tqgphc', a.astype(BF16).reshape(j + 1, nq, lg, cg, p), eye
                         ).reshape(j + 1, nq, sw, lb) for a in (ca_re, -ca_im)]
    jj = jnp.arange(j)
    lag = (jj[:, None] - jj[None, :]) if backward else (jj[None, :] - jj[:, None])
    kt = jnp.where((lag >= 0)[:, :, None, None, None], bd_k[jnp.clip(lag, 0, j - 1)], 0)
    ktoep = kt.transpose(2, 0, 3, 1, 4).reshape(nq, j * lb, j * lb)
    tau_in = jj if backward else (j - 1 - jj)
    win_re, win_im = (a[tau_in].transpose(1, 0, 2, 3).reshape(nq, j * lb, sw) for a in bd_in)
    tau_out = (j - jj) if backward else (jj + 1)
    wout_re, wout_im = (a[tau_out].transpose(1, 2, 0, 3).reshape(nq, sw, j * lb) for a in bd_out)
    dec_re = pw_re[j].reshape(nq, 1, sw)
    dec_im = pw_im[j].reshape(nq, 1, sw)
    return ktoep, win_re, win_im, wout_re, wout_im, dec_re, dec_im


def _s5_kernel(u_ref, kt_ref, wir_ref, wii_ref, wor_ref, woi_ref, dr_ref, di_ref, y_ref,
               xf_ref, yf_ref, sre_ref, sim_ref, *, bt, nk, nk_ctx, rs):
    d = pl.program_id(0)
    j = S5_J
    lanes = u_ref.shape[-1]
    for b in range(bt):
        for jj in range(j):
            xf_ref[b * nk:(b + 1) * nk, jj * lanes:(jj + 1) * lanes] = (
                u_ref.at[b][pl.ds(jj, nk, stride=j), :].astype(BF16))
    xf = xf_ref[...]
    yf_ref[...] = jnp.dot(xf, kt_ref[0, 0], preferred_element_type=F32)
    inc_re = jnp.dot(xf, wir_ref[0, 0], preferred_element_type=F32)
    inc_im = jnp.dot(xf, wii_ref[0, 0], preferred_element_type=F32)
    nl = sre_ref.shape[0]
    for b in range(bt):
        for l in range(nl):
            sre_ref[l, b * rs:b * rs + nk, :] = inc_re[b * nk:(b + 1) * nk, l * lanes:(l + 1) * lanes]
            sim_ref[l, b * rs:b * rs + nk, :] = inc_im[b * nk:(b + 1) * nk, l * lanes:(l + 1) * lanes]
    a_re = [dr_ref[0, 0, :, l * lanes:(l + 1) * lanes] for l in range(nl)]
    a_im = [di_ref[0, 0, :, l * lanes:(l + 1) * lanes] for l in range(nl)]

    def step(kidx, carry):
        rows = pl.ds(kidx, bt, stride=rs)
        out = []
        for l in range(nl):
            s_re, s_im = carry[2 * l], carry[2 * l + 1]
            i_re = sre_ref.at[l][rows, :]
            i_im = sim_ref.at[l][rows, :]
            sre_ref.at[l][rows, :] = s_re
            sim_ref.at[l][rows, :] = s_im
            out.append(a_re[l] * s_re - a_im[l] * s_im + i_re)
            out.append(a_re[l] * s_im + a_im[l] * s_re + i_im)
        return tuple(out)

    zero = tuple(jnp.zeros((bt, lanes), F32) for _ in range(2 * nl))

    @pl.when(d == 0)
    def _():
        lax.fori_loop(0, nk, step, zero)

    @pl.when(d == 1)
    def _():
        carry = lax.fori_loop(0, nk_ctx, lambda i, cr: step(nk_ctx - 1 - i, cr), zero)
        lax.fori_loop(0, nk - nk_ctx, lambda i, cr: step(nk - 1 - i, cr), carry)

    for b in range(bt):
        sp_re = jnp.concatenate([sre_ref[l, b * rs:b * rs + nk, :] for l in range(nl)], axis=-1).astype(BF16)
        sp_im = jnp.concatenate([sim_ref[l, b * rs:b * rs + nk, :] for l in range(nl)], axis=-1).astype(BF16)
        yb = (yf_ref[b * nk:(b + 1) * nk, :]
              + jnp.dot(sp_re, wor_ref[0, 0], preferred_element_type=F32)
              + jnp.dot(sp_im, woi_ref[0, 0], preferred_element_type=F32))
        for jj in range(j):
            y_ref.at[0, b][pl.ds(jj, nk, stride=j), :] = yb[:, jj * lanes:(jj + 1) * lanes]


def _s5_mixer(u, mats, n_ctx):
    bsz, lt, w = u.shape
    ktoep, win_re, win_im, wout_re, wout_im, dec_re, dec_im = mats
    lanes = 128
    bt = 4 if bsz % 4 == 0 else 2
    nq = w // lanes
    j = S5_J
    nk = lt // j
    nk_ctx = n_ctx // j
    rs = nk + 8
    fl = j * lanes
    sw = win_re.shape[-1]
    wmap = lambda d, q, b: (d, q, 0, 0)
    return pl.pallas_call(
        functools.partial(_s5_kernel, bt=bt, nk=nk, nk_ctx=nk_ctx, rs=rs),
        grid=(2, nq, bsz // bt),
        in_specs=[
            pl.BlockSpec((bt, lt, lanes), lambda d, q, b: (b, 0, q)),
            pl.BlockSpec((1, 1, fl, fl), wmap),
            pl.BlockSpec((1, 1, fl, sw), wmap),
            pl.BlockSpec((1, 1, fl, sw), wmap),
            pl.BlockSpec((1, 1, sw, fl), wmap),
            pl.BlockSpec((1, 1, sw, fl), wmap),
            pl.BlockSpec((1, 1, 1, sw), wmap),
            pl.BlockSpec((1, 1, 1, sw), wmap),
        ],
        out_specs=pl.BlockSpec((1, bt, lt, lanes), lambda d, q, b: (d, b, 0, q)),
        out_shape=jax.ShapeDtypeStruct((2, bsz, lt, w), F32),
        scratch_shapes=[pltpu.VMEM((bt * nk, fl), BF16), pltpu.VMEM((bt * nk, fl), F32),
                        pltpu.VMEM((sw // lanes, bt * rs, lanes), F32),
                        pltpu.VMEM((sw // lanes, bt * rs, lanes), F32)],
        compiler_params=_cparams(("parallel", "parallel", "arbitrary")),
        name="s5_mixer",
    )(u, ktoep, win_re, win_im, wout_re, wout_im, dec_re, dec_im)


def _head_norm(x, heads):
    dh = x.shape[-1] // heads
    outs = []
    for h in range(heads):
        xh = x[:, h * dh:(h + 1) * dh]
        outs.append(xh * lax.rsqrt(jnp.mean(xh * xh, axis=-1, keepdims=True) + EPS))
    return jnp.concatenate(outs, axis=-1)


def _even_post_kernel(m_ref, o_ref, s_ref, u_ref, mg_ref, dsk_ref, gw_ref, gb_ref, w_ref, h_ref, gate_ref,
                      out_ref, *, heads):
    m = m_ref[0, 0] + m_ref[1, 0]
    m_out = _head_norm(m, heads) * mg_ref[...] * jax.nn.sigmoid(o_ref[0])
    y = jax.nn.gelu(s_ref[0, 0] + s_ref[1, 0] + dsk_ref[...] * u_ref[0])
    glu = jnp.dot(y.astype(BF16), gw_ref[...], preferred_element_type=F32) + gb_ref[...]
    s_out = y * jax.nn.sigmoid(glu)
    cat = jnp.concatenate([m_out, s_out], axis=-1).astype(BF16)
    z = jnp.dot(cat, w_ref[...], preferred_element_type=F32)
    out_ref[0] = h_ref[0] + gate_ref[0, 0] * z


def _even_post(m2, o, s2, u, mnorm_g, d_skip, glu_w, glu_b, w_out, h, gate):
    bsz, lt, d = h.shape
    mw = o.shape[-1]
    sw = u.shape[-1]
    tm = ROW_TILE
    row = lambda b, i: (b, i, 0)
    row2 = lambda b, i: (0, b, i, 0)
    const = lambda b, i: (0, 0)
    return pl.pallas_call(
        functools.partial(_even_post_kernel, heads=MLSTM_HEADS),
        grid=(bsz, lt // tm),
        in_specs=[
            pl.BlockSpec((2, 1, tm, mw), row2),
            pl.BlockSpec((1, tm, mw), row),
            pl.BlockSpec((2, 1, tm, sw), row2),
            pl.BlockSpec((1, tm, sw), row),
            pl.BlockSpec((1, mw), const),
            pl.BlockSpec((1, sw), const),
            pl.BlockSpec((sw, sw), const),
            pl.BlockSpec((1, sw), const),
            pl.BlockSpec((mw + sw, d), const),
            pl.BlockSpec((1, tm, d), row),
            pl.BlockSpec((1, 1, 1, d), _seg_map),
        ],
        out_specs=pl.BlockSpec((1, tm, d), row),
        out_shape=jax.ShapeDtypeStruct((bsz, lt, d), F32),
        compiler_params=_cparams(("parallel", "parallel")),
        name="even_post",
    )(m2, o, s2, u, mnorm_g.reshape(1, mw), d_skip.reshape(1, sw), glu_w.astype(BF16), glu_b.reshape(1, sw),
      w_out.astype(BF16), h, gate)


def _odd_post_kernel(o_ref, g_ref, ng_ref, w_ref, h_ref, gate_ref, out_ref, *, heads):
    g = g_ref[0]
    y = _head_norm(o_ref[0, 0] + o_ref[1, 0], heads) * ng_ref[...] * (g * jax.nn.sigmoid(g))
    z = jnp.dot(y.astype(BF16), w_ref[...], preferred_element_type=F32)
    out_ref[0] = h_ref[0] + gate_ref[0, 0] * z


def _odd_post(o, g, norm_g, w_out, h, gate):
    bsz, lt, d = h.shape
    dv = o.shape[-1]
    tm = ROW_TILE
    row = lambda b, i: (b, i, 0)
    const = lambda b, i: (0, 0)
    return pl.pallas_call(
        functools.partial(_odd_post_kernel, heads=GLA_HEADS),
        grid=(bsz, lt // tm),
        in_specs=[
            pl.BlockSpec((2, 1, tm, dv), lambda b, i: (0, b, i, 0)),
            pl.BlockSpec((1, tm, dv), row),
            pl.BlockSpec((1, dv), const),
            pl.BlockSpec((dv, d), const),
            pl.BlockSpec((1, tm, d), row),
            pl.BlockSpec((1, 1, 1, d), _seg_map),
        ],
        out_specs=pl.BlockSpec((1, tm, d), row),
        out_shape=jax.ShapeDtypeStruct((bsz, lt, d), F32),
        compiler_params=_cparams(("parallel", "parallel")),
        name="odd_post",
    )(o, g, norm_g.reshape(1, dv), w_out.astype(BF16), h, gate)


def kernel(x, c, ctx, c_ctx, mod_w, mod_b, norm_mix_g, norm_ffn_g, ev_w_in, ev_b_in, ev_conv_w, ev_conv_b, ev_mlstm_norm_g, ev_s5_a_re_f, ev_s5_a_im_f, ev_s5_log_dt_f, ev_s5_a_re_b, ev_s5_a_im_b, ev_s5_log_dt_b, ev_s5_b_re, ev_s5_b_im, ev_s5_c_re, ev_s5_c_im, ev_s5_d, ev_s5_glu_w, ev_s5_glu_b, ev_w_out, od_w_in, od_gate_w2_f, od_gate_b2_f, od_gate_w2_b, od_gate_b2_b, od_norm_g, od_w_out, router_w, router_b, moe_w_gu, moe_b_gu, moe_w_down, moe_b_down, final_norm_g):
    bsz, seq, d = x.shape
    n_ctx = ctx.shape[1]
    depth = mod_w.shape[0]
    lt = n_ctx + seq
    assert n_ctx == ROW_TILE and seq % ROW_TILE == 0 and seq % GRID_W == 0

    h = jnp.concatenate([ctx, x], axis=1)
    c_all = jnp.concatenate([c, c_ctx[None, :]], axis=0)
    c_all = jnp.pad(c_all, ((0, (-c_all.shape[0]) % 8), (0, 0)))
    mods = _modulation(c_all, mod_w, mod_b)
    mod_lat = mods[:, :bsz]
    mod_ctx = jnp.broadcast_to(mods[:, bsz:bsz + 1], mod_lat.shape)
    mod6 = jnp.stack([mod_ctx, mod_lat], axis=2).reshape(depth, bsz, 2, 6, 1, d)

    bg_all = moe_b_gu[..., 0::2]
    bu_all = moe_b_gu[..., 1::2]

    mw = ev_conv_w.shape[-1] // 2
    n_gates = 4 * MLSTM_HEADS
    s5w = ev_s5_d.shape[-1]
    dk_t = od_gate_w2_f.shape[-1]
    dv_t = od_norm_g.shape[-1]
    y_buf = jnp.zeros((_moe_rows(bsz * lt), d), BF16)
    for layer in range(depth):
        last = layer == depth - 1
        j = layer // 2
        m6 = mod6[layer]
        sh1, sc1, g1, sh2, sc2, g2 = (m6[:, :, i] for i in range(6))
        if layer % 2 == 0:
            w_in, b_in = ev_w_in[j], ev_b_in[j]
            cols = jnp.concatenate([jnp.arange(0, 4 * mw), jnp.arange(4 * mw + n_gates, 4 * mw + n_gates + s5w),
                                    jnp.arange(4 * mw, 4 * mw + n_gates)])
            qk_pre, v, o, u, gates = _nm_matmul(h, norm_mix_g[layer], sh1, sc1, w_in[:, cols], b_in[cols],
                                                (2 * mw, mw, mw, s5w, n_gates))
            dh = mw // MLSTM_HEADS
            colscale = jnp.concatenate([jnp.full((mw,), dh ** -0.5, F32), jnp.ones((mw,), F32)])
            qk = _conv_silu(qk_pre, ev_conv_w[j], ev_conv_b[j], colscale, n_ctx)
            m2 = _mlstm_mixer(qk, v, gates, n_ctx)
            shared = (ev_s5_b_re[j], ev_s5_b_im[j], ev_s5_c_re[j], ev_s5_c_im[j])
            mats_f = _s5_matrices(ev_s5_a_re_f[j], ev_s5_a_im_f[j], ev_s5_log_dt_f[j], *shared, backward=False)
            mats_b = _s5_matrices(ev_s5_a_re_b[j], ev_s5_a_im_b[j], ev_s5_log_dt_b[j], *shared, backward=True)
            s2 = _s5_mixer(u, tuple(jnp.stack([a, b]) for a, b in zip(mats_f, mats_b)), n_ctx)
            h = _even_post(m2, o, s2, u, ev_mlstm_norm_g[j], ev_s5_d[j], ev_s5_glu_w[j], ev_s5_glu_b[j],
                           ev_w_out[j], h, g1)
        else:
            hc = _grid_reorder(h, n_ctx, True)
            qq, kk, vv, gg, rr = _nm_matmul(hc, norm_mix_g[layer], sh1, sc1, od_w_in[j],
                                            jnp.zeros((od_w_in.shape[-1],), F32),
                                            (dk_t, dk_t, dv_t, dv_t, 2 * GLA_RANK))
            zero = jnp.zeros_like(od_gate_w2_f[j])
            w2 = jnp.stack([jnp.concatenate([od_gate_w2_f[j], zero], axis=0),
                            jnp.concatenate([zero, od_gate_w2_b[j]], axis=0)])
            b2 = jnp.stack([od_gate_b2_f[j], od_gate_b2_b[j]])[:, None, :]
            o2 = _gla_mixer(qq, kk, vv, rr, w2, b2, n_ctx)
            h = _grid_reorder(_odd_post(o2, gg, od_norm_g[j], od_w_out[j], hc, g1), n_ctx, False)
        f, top_e, gate = _ffn_prep(h, norm_ffn_g[layer], sh2, sc2, router_w[layer], router_b[layer])
        weights = (layer, moe_w_gu, moe_w_down, bg_all[layer], bu_all[layer], moe_b_down[layer])
        h, y_buf = _moe_layer(h, f, top_e, gate, g2, weights, last, n_ctx, y_buf)
    return _final_norm(h, final_norm_g)
```

```python
import functools

import jax
import jax.numpy as jnp
from jax import lax
from jax.experimental import pallas as pl
from jax.experimental.pallas import tpu as pltpu

F32 = jnp.float32
BF16 = jnp.bfloat16
HI = lax.Precision.HIGHEST

EPS = 1e-6
GRID_W = 64
MLSTM_HEADS = 4
S5_GROUP = 16
GLA_HEADS = 4
GLA_RANK = 16
GLA_TAU = 16.0
N_EXPERTS = 32
TOP_K = 4
SWIGLU_LIMIT = 7.0
SWIGLU_ALPHA = 1.702

ROW_TILE = 256
MOE_TILE = 512
MIX_CHUNK = 64
MLSTM_CHUNK = 128
MIX_BATCH = 4
S5_J = 8
VMEM_LIMIT = 56 * 1024 * 1024

NT = (((1,), (1,)), ((), ()))
TN = (((0,), (0,)), ((), ()))


def _cparams(sem):
    return pltpu.CompilerParams(dimension_semantics=sem, vmem_limit_bytes=VMEM_LIMIT)


def _mod_kernel(c_ref, w_ref, b_ref, o_ref):
    c = c_ref[...]
    a = c * jax.nn.sigmoid(c)
    o_ref[0] = jnp.dot(a.astype(BF16), w_ref[0].astype(BF16), preferred_element_type=F32) + b_ref[0]


def _modulation(c_all, mod_w, mod_b):
    depth, d, n6 = mod_w.shape
    rows = c_all.shape[0]
    tn = d
    return pl.pallas_call(
        _mod_kernel,
        grid=(depth, n6 // tn),
        in_specs=[
            pl.BlockSpec((rows, d), lambda l, j: (0, 0)),
            pl.BlockSpec((1, d, tn), lambda l, j: (l, 0, j)),
            pl.BlockSpec((1, 1, tn), lambda l, j: (l, 0, j)),
        ],
        out_specs=pl.BlockSpec((1, rows, tn), lambda l, j: (l, 0, j)),
        out_shape=jax.ShapeDtypeStruct((depth, rows, n6), F32),
        compiler_params=_cparams(("arbitrary", "arbitrary")),
        name="modulation",
    )(c_all, mod_w, mod_b.reshape(depth, 1, n6))


def _norm_mod(x, g, sh, sc):
    ms = jnp.mean(x * x, axis=-1, keepdims=True)
    return (x * lax.rsqrt(ms + EPS) * g) * (1.0 + sc) + sh


def _nm_matmul_kernel(x_ref, g_ref, sh_ref, sc_ref, w_ref, b_ref, *out_refs, splits):
    a = _norm_mod(x_ref[0], g_ref[...], sh_ref[0, 0], sc_ref[0, 0])
    z = jnp.dot(a.astype(BF16), w_ref[...], preferred_element_type=F32) + b_ref[...]
    for (lo, hi), o_ref in zip(splits, out_refs):
        o_ref[0] = z[:, lo:hi].astype(o_ref.dtype)


def _wide_tile(rows):
    return next(t for t in (4 * ROW_TILE, 3 * ROW_TILE, 2 * ROW_TILE, ROW_TILE) if rows % t == 0)


def _seg_map(b, i):
    return (b, jnp.minimum(i, 1), 0, 0)


def _nm_matmul(h, g, shift, scale, w, bias, widths):
    bsz, lt, d = h.shape
    p = w.shape[1]
    splits, lo = [], 0
    for wd in widths:
        splits.append((lo, lo + wd))
        lo += wd
    assert lo == p
    tm = ROW_TILE
    return pl.pallas_call(
        functools.partial(_nm_matmul_kernel, splits=tuple(splits)),
        grid=(bsz, lt // tm),
        in_specs=[
            pl.BlockSpec((1, tm, d), lambda b, i: (b, i, 0)),
            pl.BlockSpec((1, d), lambda b, i: (0, 0)),
            pl.BlockSpec((1, 1, 1, d), _seg_map),
            pl.BlockSpec((1, 1, 1, d), _seg_map),
            pl.BlockSpec((d, p), lambda b, i: (0, 0)),
            pl.BlockSpec((1, p), lambda b, i: (0, 0)),
        ],
        out_specs=[pl.BlockSpec((1, tm, wd), lambda b, i: (b, i, 0)) for wd in widths],
        out_shape=[jax.ShapeDtypeStruct((bsz, lt, wd), F32) for wd in widths],
        compiler_params=_cparams(("parallel", "parallel")),
        name="norm_mod_matmul",
    )(h, g.reshape(1, d), shift, scale, w.astype(BF16), bias.reshape(1, p))


def _ffn_prep_kernel(x_ref, g_ref, sh_ref, sc_ref, rw_ref, rb_ref, f_ref, te_ref, gt_ref, *, n_ctx):
    tm = x_ref.shape[1]
    is_ctx = pl.program_id(1) * tm + lax.broadcasted_iota(jnp.int32, (tm, 1), 0) < n_ctx
    a = _norm_mod(x_ref[0], g_ref[...], jnp.where(is_ctx, sh_ref[0, 0], sh_ref[0, 1]),
                  jnp.where(is_ctx, sc_ref[0, 0], sc_ref[0, 1]))
    f_ref[0] = a.astype(f_ref.dtype)
    logits = lax.dot_general(rw_ref[...], a.astype(BF16), NT, preferred_element_type=F32) + rb_ref[...]
    ne = logits.shape[0]
    eidx = lax.broadcasted_iota(jnp.int32, logits.shape, 0)
    work = logits
    vals, idxs = [], []
    for _ in range(TOP_K):
        m = jnp.max(work, axis=0, keepdims=True)
        idx = jnp.min(jnp.where(work == m, eidx, ne), axis=0, keepdims=True)
        vals.append(m)
        idxs.append(idx)
        work = jnp.where(eidx == idx, -jnp.inf, work)
    exps = [jnp.exp(v - vals[0]) for v in vals]
    denom = exps[0]
    for e in exps[1:]:
        denom = denom + e
    for k in range(TOP_K):
        te_ref[0, k:k + 1, :] = idxs[k]
        gt_ref[0, k:k + 1, :] = exps[k] / denom


def _ffn_prep(h, g, shift, scale, router_w, router_b):
    bsz, lt, d = h.shape
    ne = router_w.shape[1]
    tm = _wide_tile(lt)
    both = pl.BlockSpec((1, 2, 1, d), lambda b, i: (b, 0, 0, 0))
    return pl.pallas_call(
        functools.partial(_ffn_prep_kernel, n_ctx=ROW_TILE),
        grid=(bsz, lt // tm),
        in_specs=[
            pl.BlockSpec((1, tm, d), lambda b, i: (b, i, 0)),
            pl.BlockSpec((1, d), lambda b, i: (0, 0)),
            both,
            both,
            pl.BlockSpec((ne, d), lambda b, i: (0, 0)),
            pl.BlockSpec((ne, 1), lambda b, i: (0, 0)),
        ],
        out_specs=[
            pl.BlockSpec((1, tm, d), lambda b, i: (b, i, 0)),
            pl.BlockSpec((1, TOP_K, tm), lambda b, i: (b, 0, i)),
            pl.BlockSpec((1, TOP_K, tm), lambda b, i: (b, 0, i)),
        ],
        out_shape=[
            jax.ShapeDtypeStruct((bsz, lt, d), BF16),
            jax.ShapeDtypeStruct((bsz, TOP_K, lt), jnp.int32),
            jax.ShapeDtypeStruct((bsz, TOP_K, lt), F32),
        ],
        compiler_params=_cparams(("parallel", "parallel")),
        name="ffn_prep",
    )(h, g.reshape(1, d), shift, scale, router_w.T.astype(BF16), router_b.reshape(ne, 1))


GU_BLOCK = 256


def _moe_kernel(be_ref, nb_ref, first_ref, slot_ref, nxt_ref, x_ref, wgu_hbm, wd_hbm, bg_ref, bu_ref, bd_ref,
                *rest, layer, in_place):
    o_ref, wgu_buf, wd_buf, wgu_s, wd_s, sem = rest[1:] if in_place else rest
    i = pl.program_id(0)
    active = i < nb_ref[0]
    half = GU_BLOCK // 2
    nblk = wgu_s.shape[1] // GU_BLOCK

    def weight_copies(e, slot):
        return (pltpu.make_async_copy(wgu_hbm.at[layer, e], wgu_buf.at[slot], sem.at[0, slot]),
                pltpu.make_async_copy(wd_hbm.at[layer, e], wd_buf.at[slot], sem.at[1, slot]))

    @pl.when(active & (i == 0))
    def _():
        for cp in weight_copies(be_ref[0], 0):
            cp.start()

    @pl.when(active & (first_ref[i] == 1))
    def _():
        slot = slot_ref[i]
        for cp in weight_copies(be_ref[i], slot):
            cp.wait()

        @pl.when(nxt_ref[i] >= 0)
        def _():
            for cp in weight_copies(nxt_ref[i], 1 - slot):
                cp.start()

        r = lax.broadcasted_iota(jnp.int32, (GU_BLOCK, GU_BLOCK), 0)
        c = lax.broadcasted_iota(jnp.int32, (GU_BLOCK, GU_BLOCK), 1)
        perm = (r == jnp.where(c < half, 2 * c, 2 * (c - half) + 1)).astype(BF16)
        for k in range(nblk):
            cs = slice(k * GU_BLOCK, (k + 1) * GU_BLOCK)
            wgu_s[:, cs] = jnp.dot(wgu_buf[slot, :, cs].astype(BF16), perm,
                                   preferred_element_type=F32).astype(BF16)
        wd_s[...] = wd_buf[slot].astype(BF16)

    @pl.when(active)
    def _():
        gu = jnp.dot(x_ref[...], wgu_s[...], preferred_element_type=F32)
        hdn = []
        for k in range(nblk):
            hs = slice(k * half, (k + 1) * half)
            g = gu[:, k * GU_BLOCK:k * GU_BLOCK + half] + bg_ref[0, :, hs]
            u = gu[:, k * GU_BLOCK + half:(k + 1) * GU_BLOCK] + bu_ref[0, :, hs]
            g = jnp.minimum(g, SWIGLU_LIMIT)
            u = jnp.clip(u, -SWIGLU_LIMIT, SWIGLU_LIMIT)
            hdn.append(((u + 1.0) * (g * jax.nn.sigmoid(SWIGLU_ALPHA * g))).astype(BF16))
        hdn = jnp.concatenate(hdn, axis=-1)
        o_ref[...] = (jnp.dot(hdn, wd_s[...], preferred_element_type=F32) + bd_ref[0]).astype(o_ref.dtype)

    @pl.when(jnp.logical_not(active))
    def _():
        o_ref[...] = jnp.zeros_like(o_ref)


def _moe_experts(x_sorted, block_expert, n_used, layer, w_gu, w_down, bg, bu, bd, block_off, y_prev, out_rows):
    n_rows, d = x_sorted.shape
    _, ne, _, f2 = w_gu.shape
    f = f2 // 2
    tm = MOE_TILE
    x_blocks = n_rows // tm
    in_place = y_prev is not None
    n_blocks = x_blocks if in_place else out_rows // tm - block_off
    assert f2 % GU_BLOCK == 0 and block_expert.shape[0] == n_blocks
    blk = jnp.arange(n_blocks, dtype=jnp.int32)
    prev = jnp.concatenate([block_expert[:1], block_expert[:-1]])
    first = (blk < n_used[0]) & ((blk == 0) | (block_expert != prev))
    slot = (jnp.cumsum(first.astype(jnp.int32)) - 1) & 1
    first_idx = jnp.where(first, blk, n_blocks)
    next_first = lax.cummin(first_idx, axis=0, reverse=True)
    next_first = jnp.concatenate([next_first[1:], jnp.full((1,), n_blocks, jnp.int32)])
    nxt = jnp.where(next_first < n_blocks, block_expert[jnp.minimum(next_first, n_blocks - 1)], -1)
    bmap = lambda i, be, nb, fi, sl, nx: (be[i], 0, 0)
    rmap = lambda i, be, nb, fi, sl, nx: (jnp.minimum(i, x_blocks - 1), 0)
    grid_spec = pltpu.PrefetchScalarGridSpec(
        num_scalar_prefetch=5,
        grid=(n_blocks,),
        in_specs=[
            pl.BlockSpec((tm, d), rmap),
            pl.BlockSpec(memory_space=pl.ANY),
            pl.BlockSpec(memory_space=pl.ANY),
            pl.BlockSpec((1, 1, f), bmap),
            pl.BlockSpec((1, 1, f), bmap),
            pl.BlockSpec((1, 1, d), bmap),
        ] + ([pl.BlockSpec(memory_space=pl.ANY)] if in_place else []),
        out_specs=pl.BlockSpec((tm, d), lambda i, be, nb, fi, sl, nx: (i + block_off, 0)),
        scratch_shapes=[pltpu.VMEM((2, d, f2), F32), pltpu.VMEM((2, f, d), F32),
                        pltpu.VMEM((d, f2), BF16), pltpu.VMEM((f, d), BF16),
                        pltpu.SemaphoreType.DMA((2, 2))],
    )
    operands = (block_expert, n_used, first.astype(jnp.int32), slot.astype(jnp.int32), nxt.astype(jnp.int32),
                x_sorted, w_gu, w_down, bg.reshape(ne, 1, f), bu.reshape(ne, 1, f), bd.reshape(ne, 1, d))
    if in_place:
        assert y_prev.shape == (out_rows, d)
        operands = operands + (y_prev,)
    return pl.pallas_call(
        functools.partial(_moe_kernel, layer=layer, in_place=in_place),
        grid_spec=grid_spec,
        out_shape=jax.ShapeDtypeStruct((out_rows, d), BF16),
        input_output_aliases={len(operands) - 1: 0} if in_place else {},
        compiler_params=_cparams(("arbitrary",)),
        name="moe_experts",
    )(*operands)


def _combine_kernel(y_ref, gt_ref, h_ref, g2_ref, o_ref, *, n_ctx):
    gt = gt_ref[0]
    acc = y_ref[0, 0].astype(F32) * gt[:, 0:1]
    for k in range(1, TOP_K):
        acc = acc + y_ref[k, 0].astype(F32) * gt[:, k:k + 1]
    tm = acc.shape[0]
    is_ctx = pl.program_id(1) * tm + lax.broadcasted_iota(jnp.int32, (tm, 1), 0) < n_ctx
    o_ref[0] = h_ref[0] + jnp.where(is_ctx, g2_ref[0, 0], g2_ref[0, 1]) * acc


def _moe_combine(yg, gate, h, g2, lat_only):
    k, bsz, lt, d = yg.shape
    tm = ROW_TILE if lat_only else _wide_tile(lt)
    tile_off = 1 if lat_only else 0
    return pl.pallas_call(
        functools.partial(_combine_kernel, n_ctx=0 if lat_only else ROW_TILE),
        grid=(bsz, lt // tm),
        in_specs=[
            pl.BlockSpec((k, 1, tm, d), lambda b, i: (0, b, i, 0)),
            pl.BlockSpec((1, tm, k), lambda b, i: (b, i, 0)),
            pl.BlockSpec((1, tm, d), lambda b, i: (b, i + tile_off, 0)),
            pl.BlockSpec((1, 2, 1, d), lambda b, i: (b, 0, 0, 0)),
        ],
        out_specs=pl.BlockSpec((1, tm, d), lambda b, i: (b, i, 0)),
        out_shape=jax.ShapeDtypeStruct((bsz, lt, d), F32),
        compiler_params=_cparams(("parallel", "parallel")),
        name="moe_combine",
    )(yg, gate, h, g2)


def _route_kernel(e_ref, pos_ref, cnt_ref, *, tm):
    nk, rows, lanes = e_ref.shape
    li = lax.broadcasted_iota(jnp.int32, (lanes, lanes), 0)
    lj = lax.broadcasted_iota(jnp.int32, (lanes, lanes), 1)
    before_lane = (li < lj).astype(BF16)
    ones = jnp.ones((lanes, lanes), BF16)
    ri = lax.broadcasted_iota(jnp.int32, (rows, rows), 0)
    rj = lax.broadcasted_iota(jnp.int32, (rows, rows), 1)
    before_row = (rj < ri).astype(BF16)
    lane = lax.broadcasted_iota(jnp.int32, (1, lanes), 1)
    xs = [e_ref[k] for k in range(nk)]
    pos = [jnp.zeros((rows, lanes), F32) for _ in range(nk)]
    counts = jnp.zeros((1, lanes), F32)
    pad_off = jnp.zeros((1, lanes), F32)
    for e in range(N_EXPERTS):
        ms = [x == e for x in xs]
        hit = ms[0]
        for m in ms[1:]:
            hit = hit | m
        mb = hit.astype(BF16)
        in_row = jnp.dot(mb, before_lane, preferred_element_type=F32)
        row_sum = jnp.dot(mb, ones, preferred_element_type=F32)
        row_off = jnp.dot(before_row, row_sum.astype(BF16), preferred_element_type=F32)
        count = row_off[rows - 1:rows, :] + row_sum[rows - 1:rows, :]
        dest = in_row + row_off + pad_off
        pos = [p + jnp.where(m, dest, 0.0) for p, m in zip(pos, ms)]
        counts = jnp.where(lane == e, count, counts)
        pad_off = pad_off + jnp.floor((count + (tm - 1)) * (1.0 / tm)) * tm
    for k in range(nk):
        pos_ref[k] = pos[k].astype(jnp.int32)
    cnt_ref[...] = counts.astype(jnp.int32)


def _route_positions(top_e, tm):
    lanes = 128
    nk, n = top_e.shape
    rows = n // lanes
    pos, counts = pl.pallas_call(
        functools.partial(_route_kernel, tm=tm),
        out_shape=[jax.ShapeDtypeStruct((nk, rows, lanes), jnp.int32), jax.ShapeDtypeStruct((1, lanes), jnp.int32)],
        compiler_params=pltpu.CompilerParams(vmem_limit_bytes=VMEM_LIMIT),
        name="route_positions",
    )(top_e.reshape(nk, rows, lanes))
    return pos.reshape(nk, n), counts[0, :N_EXPERTS]


def _moe_rows(n_tokens):
    n_assign = n_tokens * TOP_K
    return -(-(n_assign + N_EXPERTS * (MOE_TILE - 1)) // MOE_TILE) * MOE_TILE


def _moe_layer(h, f, top_e, gate, g2, weights, lat_only, n_ctx, y_buf):
    layer, w_gu, w_down, bg, bu, bd = weights
    bsz, lt, d = h.shape
    skip = n_ctx if lat_only else 0
    ltok = lt - skip
    top_e, gate = top_e[:, :, skip:], gate[:, :, skip:]
    n = bsz * ltok
    n_assign = n * TOP_K
    tm = MOE_TILE
    te = top_e.transpose(1, 0, 2).reshape(TOP_K, n).astype(jnp.int32)
    bits = max(n - 1, 1).bit_length()
    assert N_EXPERTS << bits < 2 ** 31
    keys = (te << bits) | jnp.arange(n, dtype=jnp.int32)[None, :]
    slot_token = lax.sort(keys.reshape(-1)) & ((1 << bits) - 1)
    pos, counts = _route_positions(te, tm)
    start = jnp.cumsum(counts) - counts
    padded = (counts + tm - 1) // tm * tm
    pad_end = jnp.cumsum(padded)
    pad_start = pad_end - padded
    n_rows = _moe_rows(n)
    n_blocks = n_rows // tm
    buf_rows = n_rows if y_buf is None else y_buf.shape[0]
    assert buf_rows >= n_rows
    block_expert = jnp.minimum(
        jnp.searchsorted(pad_end, jnp.arange(n_blocks, dtype=jnp.int32) * tm, side='right', method='compare_all'),
        N_EXPERTS - 1).astype(jnp.int32)
    n_used = (pad_end[-1] // tm).astype(jnp.int32).reshape(1)
    row = jnp.arange(n_rows, dtype=jnp.int32).reshape(n_blocks, tm)
    blk_shift = (start - pad_start)[block_expert][:, None]
    blk_end = (pad_start + counts)[block_expert][:, None]
    slot = jnp.clip(row + blk_shift, 0, n_assign - 1).reshape(-1)
    row_token = jnp.where((row < blk_end).reshape(-1),
                          slot_token.at[slot].get(mode='promise_in_bounds'), row.reshape(-1) % n)
    row_src = row_token + skip * (row_token // ltok + 1)
    f2d = f.reshape(bsz * lt, d)
    hb = n_blocks // 2
    y = y_buf
    for lo, hi in ((0, hb), (hb, n_blocks)):
        x_part = f2d.at[row_src[lo * tm:hi * tm]].get(mode='promise_in_bounds')
        blocks = block_expert[lo:hi] if y is not None else block_expert[lo:]
        y = _moe_experts(x_part, blocks, jnp.clip(n_used - lo, 0, hi - lo), layer,
                         w_gu, w_down, bg, bu, bd, lo, y, buf_rows)
    yg = y.at[pos.reshape(-1)].get(mode='promise_in_bounds').reshape(TOP_K, bsz, ltok, d)
    return _moe_combine(yg, gate.transpose(0, 2, 1), h, g2, lat_only), y


def _rmsnorm_kernel(x_ref, g_ref, o_ref):
    x = x_ref[0]
    ms = jnp.mean(x * x, axis=-1, keepdims=True)
    o_ref[0] = x * lax.rsqrt(ms + EPS) * g_ref[...]


def _final_norm(h, g):
    bsz, lt, d = h.shape
    tm = _wide_tile(lt)
    return pl.pallas_call(
        _rmsnorm_kernel,
        grid=(bsz, lt // tm),
        in_specs=[pl.BlockSpec((1, tm, d), lambda b, i: (b, i, 0)),
                  pl.BlockSpec((1, d), lambda b, i: (0, 0))],
        out_specs=pl.BlockSpec((1, tm, d), lambda b, i: (b, i, 0)),
        out_shape=jax.ShapeDtypeStruct((bsz, lt, d), F32),
        compiler_params=_cparams(("parallel", "parallel")),
        name="final_norm",
    )(h, g.reshape(1, d))


def _grid_reorder_kernel(x_ref, o_ref, *, n_ctx, rows, to_cols):
    o_ref[0, :n_ctx, :] = x_ref[0, :n_ctx, :]
    for c in range(GRID_W):
        raster = pl.ds(n_ctx + c, rows, stride=GRID_W)
        dense = pl.ds(n_ctx + c * rows, rows)
        if to_cols:
            o_ref.at[0][dense, :] = x_ref.at[0][raster, :]
        else:
            o_ref.at[0][raster, :] = x_ref.at[0][dense, :]


def _grid_reorder(h, n_ctx, to_cols):
    bsz, lt, d = h.shape
    lanes = 128
    spec = pl.BlockSpec((1, lt, lanes), lambda b, j: (b, 0, j))
    return pl.pallas_call(
        functools.partial(_grid_reorder_kernel, n_ctx=n_ctx, rows=(lt - n_ctx) // GRID_W, to_cols=to_cols),
        grid=(bsz, d // lanes),
        in_specs=[spec],
        out_specs=spec,
        out_shape=jax.ShapeDtypeStruct(h.shape, h.dtype),
        compiler_params=_cparams(("parallel", "parallel")),
        name="grid_reorder",
    )(h)


def _chunk_order(d, c, n_ctx_chunks, n_chunks):
    bwd = jnp.where(c < n_ctx_chunks, n_ctx_chunks - 1 - c, n_chunks + n_ctx_chunks - 1 - c)
    return jnp.where(d == 0, c, bwd)


def _split_bf16(x, n):
    out = []
    for _ in range(n):
        p = x.astype(BF16)
        out.append(p)
        x = x - p.astype(F32)
    return out


def _dir_tri(d, t):
    row = lax.broadcasted_iota(jnp.int32, (t, t), 0)
    col = lax.broadcasted_iota(jnp.int32, (t, t), 1)
    return jnp.where(d == 0, col - row, row - col) <= 0


def _gla_kernel(q_ref, k_ref, v_ref, r_ref, w2_ref, b2_ref, o_ref, st_ref, *, t, heads, scale):
    d = pl.program_id(0)
    c = pl.program_id(2)

    @pl.when(c == 0)
    def _():
        st_ref[...] = jnp.zeros_like(st_ref)

    dk = q_ref.shape[-1] // heads
    dv = v_ref.shape[-1] // heads
    nb = q_ref.shape[0]
    mask = _dir_tri(d, t)
    tri = mask.astype(F32)
    mid = t // 2
    items = [(bb, h) for bb in range(nb) for h in range(heads)]
    w_hi, w_lo = _split_bf16(w2_ref[0], 2)
    xs = []
    for bb in range(nb):
        r_hi, r_lo = _split_bf16(r_ref[bb], 2)
        xs.append(jnp.dot(r_hi, w_hi, preferred_element_type=F32) + jnp.dot(r_hi, w_lo, preferred_element_type=F32)
                  + jnp.dot(r_lo, w_hi, preferred_element_type=F32) + b2_ref[0])
    las = [jax.nn.log_sigmoid(x) * (1.0 / GLA_TAU) for x in xs]
    tri_b = tri.astype(BF16)
    bs = [sum(jnp.dot(tri_b, p, preferred_element_type=F32) for p in _split_bf16(la, 3)) for la in las]
    qt, kt, qe, kh_end, e_end = [], [], [], [], []
    for bb in range(nb):
        b = bs[bb]
        b_m = b[mid:mid + 1, :]
        b_end = jnp.where(d == 0, b[t - 1:t, :], b[0:1, :])
        q_s = q_ref[bb] * (jnp.exp(b - b_m) * scale)
        k_s = k_ref[bb] * jnp.exp(b_m - b)
        qe.append((q_s * jnp.exp(b_m)).astype(BF16))
        kh_end.append((k_s * jnp.exp(b_end - b_m)).astype(BF16))
        e_end.append(jnp.exp(b_end))
        qt.append(q_s.astype(BF16))
        kt.append(k_s.astype(BF16))
    att, q_st, vs = {}, {}, {}
    for bb, h in items:
        ks = slice(h * dk, (h + 1) * dk)
        vs[bb, h] = v_ref[bb, :, h * dv:(h + 1) * dv].astype(BF16)
        att[bb, h] = lax.dot_general(qt[bb][:, ks], kt[bb][:, ks], NT, preferred_element_type=F32)
        q_st[bb, h] = lax.dot_general(qe[bb][:, ks], st_ref[bb * heads + h].astype(BF16), NT,
                                      preferred_element_type=F32)
    for bb, h in items:
        a = jnp.where(mask, att[bb, h], 0.0).astype(BF16)
        o_ref[0, bb, :, h * dv:(h + 1) * dv] = jnp.dot(a, vs[bb, h], preferred_element_type=F32) + q_st[bb, h]
    for bb, h in items:
        ks = slice(h * dk, (h + 1) * dk)
        upd = lax.dot_general(vs[bb, h], kh_end[bb][:, ks], TN, preferred_element_type=F32)
        st_ref[bb * heads + h] = st_ref[bb * heads + h] * e_end[bb][:, ks] + upd


def _gla_mixer(q, k, v, r, w2, b2, n_ctx):
    bsz, lt, dkt = q.shape
    dvt = v.shape[-1]
    nr = r.shape[-1]
    t, heads = MIX_CHUNK, GLA_HEADS
    bt = MIX_BATCH
    nch = lt // t
    ncc = n_ctx // t
    dk = dkt // heads
    dv = dvt // heads
    imap = lambda d, b, c: (b, _chunk_order(d, c, ncc, nch), 0)
    return pl.pallas_call(
        functools.partial(_gla_kernel, t=t, heads=heads, scale=dk ** -0.5),
        grid=(2, bsz // bt, nch),
        in_specs=[
            pl.BlockSpec((bt, t, dkt), imap),
            pl.BlockSpec((bt, t, dkt), imap),
            pl.BlockSpec((bt, t, dvt), imap),
            pl.BlockSpec((bt, t, nr), imap),
            pl.BlockSpec((1, nr, dkt), lambda d, b, c: (d, 0, 0)),
            pl.BlockSpec((1, 1, dkt), lambda d, b, c: (d, 0, 0)),
        ],
        out_specs=pl.BlockSpec((1, bt, t, dvt), lambda d, b, c: (d, b, _chunk_order(d, c, ncc, nch), 0)),
        out_shape=jax.ShapeDtypeStruct((2, bsz, lt, dvt), F32),
        scratch_shapes=[pltpu.VMEM((bt * heads, dv, dk), F32)],
        compiler_params=_cparams(("parallel", "parallel", "arbitrary")),
        name="gla_mixer",
    )(q, k, v, r, w2, b2)


def _mlstm_kernel(q_ref, k_ref, v_ref, gc_ref, gr_ref, o_ref, c_ref, n_ref, m_ref, *, t, heads):
    d = pl.program_id(0)
    c = pl.program_id(2)

    @pl.when(c == 0)
    def _():
        c_ref[...] = jnp.zeros_like(c_ref)
        n_ref[...] = jnp.zeros_like(n_ref)
        m_ref[...] = jnp.zeros_like(m_ref)

    dh = q_ref.shape[-1] // heads
    nb = q_ref.shape[0]
    mask = _dir_tri(d, t)
    tri = mask.astype(F32)
    items = [(bb, h) for bb in range(nb) for h in range(heads)]
    gate = []
    for bb in range(nb):
        gc = gc_ref[0, bb]
        gr = gr_ref[0, bb, 0]
        fc = jax.nn.log_sigmoid(gc[:, heads:])
        fr = jax.nn.log_sigmoid(gr[heads:, :])
        b_col = jnp.dot(tri, fc, preferred_element_type=F32, precision=HI)
        b_row = lax.dot_general(fr, tri, NT, preferred_element_type=F32, precision=HI)
        b_last = jnp.where(d == 0, b_col[t - 1:t, :], b_col[0:1, :])
        gate.append((gc[:, :heads], gr[:heads, :], b_col, b_row, b_last))
    qs, ks, vs, s_raw, q_c = {}, {}, {}, {}, {}
    for bb, h in items:
        hs = slice(h * dh, (h + 1) * dh)
        qs[bb, h] = q_ref[bb, :, hs]
        ks[bb, h] = k_ref[bb, :, hs]
        vs[bb, h] = v_ref[bb, :, hs].astype(BF16)
        s_raw[bb, h] = lax.dot_general(qs[bb, h], ks[bb, h], NT, preferred_element_type=F32)
        q_c[bb, h] = jnp.dot(qs[bb, h], c_ref[bb * heads + h].astype(BF16), preferred_element_type=F32)
    logw, log_inter, m_t, w_inter, scores, den, qn = {}, {}, {}, {}, {}, {}, {}
    for bb, h in items:
        _, ir, b_col, b_row, _ = gate[bb]
        bc = b_col[:, h:h + 1]
        logw[bb, h] = jnp.where(mask, bc - b_row[h:h + 1, :] + ir[h:h + 1, :], -jnp.inf)
        log_inter[bb, h] = bc + m_ref[bb * heads + h]
    for bb, h in items:
        m_t[bb, h] = jnp.maximum(log_inter[bb, h], jnp.max(logw[bb, h], axis=-1, keepdims=True))
        qn[bb, h] = jnp.sum(qs[bb, h].astype(F32) * n_ref[bb * heads + h], axis=-1, keepdims=True)
    for bb, h in items:
        w_inter[bb, h] = jnp.exp(log_inter[bb, h] - m_t[bb, h])
        scores[bb, h] = s_raw[bb, h] * jnp.exp(logw[bb, h] - m_t[bb, h])
    for bb, h in items:
        den[bb, h] = jnp.sum(scores[bb, h], axis=-1, keepdims=True) + w_inter[bb, h] * qn[bb, h]
    num = {}
    for bb, h in items:
        num[bb, h] = (jnp.dot(scores[bb, h].astype(BF16), vs[bb, h], preferred_element_type=F32)
                      + w_inter[bb, h] * q_c[bb, h])
    for bb, h in items:
        hs = slice(h * dh, (h + 1) * dh)
        o_ref[0, bb, :, hs] = num[bb, h] / jnp.maximum(jnp.abs(den[bb, h]), jnp.exp(-m_t[bb, h]))
    log_g, m_new, kw, upd, ksum = {}, {}, {}, {}, {}
    for bb, h in items:
        ic, _, b_col, _, b_last = gate[bb]
        log_g[bb, h] = b_last[:, h:h + 1] - b_col[:, h:h + 1] + ic[:, h:h + 1]
    for bb, h in items:
        b_last = gate[bb][4]
        m_new[bb, h] = jnp.maximum(b_last[:, h:h + 1] + m_ref[bb * heads + h],
                                   jnp.max(log_g[bb, h], axis=0, keepdims=True))
    for bb, h in items:
        kw[bb, h] = ks[bb, h].astype(F32) * jnp.exp(log_g[bb, h] - m_new[bb, h])
    for bb, h in items:
        upd[bb, h] = lax.dot_general(kw[bb, h].astype(BF16), vs[bb, h], TN, preferred_element_type=F32)
        ksum[bb, h] = jnp.sum(kw[bb, h], axis=0, keepdims=True)
    for bb, h in items:
        si = bb * heads + h
        b_last = gate[bb][4]
        keep = jnp.exp(b_last[:, h:h + 1] + m_ref[si] - m_new[bb, h])
        c_ref[si] = keep * c_ref[si] + upd[bb, h]
        n_ref[si] = keep * n_ref[si] + ksum[bb, h]
        m_ref[si] = m_new[bb, h]


def _mlstm_mixer(qk, v, gates, n_ctx):
    bsz, lt, w2 = qk.shape
    w = w2 // 2
    t, heads = MLSTM_CHUNK, MLSTM_HEADS
    dh = w // heads
    nch = lt // t
    ncc = n_ctx // t
    gc = gates.reshape(bsz, lt, 2, 2 * heads).transpose(2, 0, 1, 3)
    gr = gc.reshape(2, bsz, nch, t, 2 * heads).transpose(0, 1, 2, 4, 3)
    cmap = lambda d, b, c: _chunk_order(d, c, ncc, nch)
    bt = MIX_BATCH
    return pl.pallas_call(
        functools.partial(_mlstm_kernel, t=t, heads=heads),
        grid=(2, bsz // bt, nch),
        in_specs=[
            pl.BlockSpec((bt, t, w), lambda d, b, c: (b, cmap(d, b, c), 0)),
            pl.BlockSpec((bt, t, w), lambda d, b, c: (b, cmap(d, b, c), 1)),
            pl.BlockSpec((bt, t, w), lambda d, b, c: (b, cmap(d, b, c), 0)),
            pl.BlockSpec((1, bt, t, 2 * heads), lambda d, b, c: (d, b, cmap(d, b, c), 0)),
            pl.BlockSpec((1, bt, 1, 2 * heads, t), lambda d, b, c: (d, b, cmap(d, b, c), 0, 0)),
        ],
        out_specs=pl.BlockSpec((1, bt, t, w), lambda d, b, c: (d, b, cmap(d, b, c), 0)),
        out_shape=jax.ShapeDtypeStruct((2, bsz, lt, w), F32),
        scratch_shapes=[pltpu.VMEM((bt * heads, dh, dh), F32), pltpu.VMEM((bt * heads, 1, dh), F32),
                        pltpu.VMEM((bt * heads, 1, 1), F32)],
        compiler_params=_cparams(("parallel", "parallel", "arbitrary")),
        name="mlstm_mixer",
    )(qk, qk, v, gc, gr)


def _conv_kernel(x_ref, w_ref, b_ref, s_ref, o_ref, *, n_ctx):
    x = x_ref[0]
    lt = x.shape[0]
    row = lax.broadcasted_iota(jnp.int32, x.shape, 0)
    prev = jnp.where((row == 0) | (row == n_ctx), 0.0, pltpu.roll(x, 1, 0))
    nxt = jnp.where((row == n_ctx - 1) | (row == lt - 1), 0.0, pltpu.roll(x, lt - 1, 0))
    y = b_ref[...] + w_ref[0:1, :] * prev + w_ref[1:2, :] * x + w_ref[2:3, :] * nxt
    o_ref[0] = (y * jax.nn.sigmoid(y) * s_ref[...]).astype(o_ref.dtype)


def _conv_silu(x, w, b, colscale, n_ctx):
    bsz, lt, ch = x.shape
    tc = 256
    return pl.pallas_call(
        functools.partial(_conv_kernel, n_ctx=n_ctx),
        grid=(bsz, ch // tc),
        in_specs=[
            pl.BlockSpec((1, lt, tc), lambda b, j: (b, 0, j)),
            pl.BlockSpec((3, tc), lambda b, j: (0, j)),
            pl.BlockSpec((1, tc), lambda b, j: (0, j)),
            pl.BlockSpec((1, tc), lambda b, j: (0, j)),
        ],
        out_specs=pl.BlockSpec((1, lt, tc), lambda b, j: (b, 0, j)),
        out_shape=jax.ShapeDtypeStruct((bsz, lt, ch), BF16),
        compiler_params=_cparams(("parallel", "parallel")),
        name="conv_silu",
    )(x, w, b.reshape(1, ch), colscale.reshape(1, ch))


def _s5_matrices(a_re, a_im, log_dt, b_re, b_im, c_re, c_im, backward, lane_groups=8):
    g, p = a_re.shape
    cg = b_re.shape[-1]
    j = S5_J
    lg = lane_groups
    nq = g // lg
    dt = jnp.exp(log_dt)[:, None]
    lam_re = jnp.minimum(a_re, -1e-4)
    lam_im = a_im
    decay = jnp.exp(lam_re * dt)
    ab_re = decay * jnp.cos(lam_im * dt)
    ab_im = decay * jnp.sin(lam_im * dt)
    den = lam_re * lam_re + lam_im * lam_im
    zr = ((ab_re - 1) * lam_re + ab_im * lam_im) / den
    zi = (ab_im * lam_re - (ab_re - 1) * lam_im) / den
    bb_re = zr[..., None] * b_re - zi[..., None] * b_im
    bb_im = zr[..., None] * b_im + zi[..., None] * b_re
    pw_re, pw_im = [jnp.ones_like(ab_re)], [jnp.zeros_like(ab_im)]
    for _ in range(j):
        r0, i0 = pw_re[-1], pw_im[-1]
        pw_re.append(ab_re * r0 - ab_im * i0)
        pw_im.append(ab_re * i0 + ab_im * r0)
    pw_re, pw_im = jnp.stack(pw_re), jnp.stack(pw_im)
    ca_re = c_re[None] * pw_re[:, :, None, :] - c_im[None] * pw_im[:, :, None, :]
    ca_im = c_re[None] * pw_im[:, :, None, :] + c_im[None] * pw_re[:, :, None, :]
    kk = (jnp.einsum('tgcp,gpd->tgcd', ca_re[:j], bb_re, precision=HI)
          - jnp.einsum('tgcp,gpd->tgcd', ca_im[:j], bb_im, precision=HI))
    ab_pw_re = pw_re[:j, :, :, None] * bb_re[None] - pw_im[:j, :, :, None] * bb_im[None]
    ab_pw_im = pw_re[:j, :, :, None] * bb_im[None] + pw_im[:j, :, :, None] * bb_re[None]
    eye = jnp.eye(lg, dtype=BF16)
    lb = lg * cg
    sw = lg * p
    bd_k = jnp.einsum('tqgcd,gh->tqgdhc', kk.astype(BF16).reshape(j, nq, lg, cg, cg), eye
                      ).reshape(j, nq, lb, lb)
    bd_in = [jnp.einsum('tqgpc,gh->tqgchp', a.astype(BF16).reshape(j, nq, lg, p, cg), eye
                        ).reshape(j, nq, lb, sw) for a in (ab_pw_re, ab_pw_im)]
    bd_out = [jnp.einsum('tqgcp,gh->tqgphc', a.astype(BF16).reshape(j + 1, nq, lg, cg, p), eye
                         ).reshape(j + 1, nq, sw, lb) for a in (ca_re, -ca_im)]
    jj = jnp.arange(j)
    lag = (jj[:, None] - jj[None, :]) if backward else (jj[None, :] - jj[:, None])
    kt = jnp.where((lag >= 0)[:, :, None, None, None], bd_k[jnp.clip(lag, 0, j - 1)], 0)
    ktoep = kt.transpose(2, 0, 3, 1, 4).reshape(nq, j * lb, j * lb)
    tau_in = jj if backward else (j - 1 - jj)
    win_re, win_im = (a[tau_in].transpose(1, 0, 2, 3).reshape(nq, j * lb, sw) for a in bd_in)
    tau_out = (j - jj) if backward else (jj + 1)
    wout_re, wout_im = (a[tau_out].transpose(1, 2, 0, 3).reshape(nq, sw, j * lb) for a in bd_out)
    dec_re = pw_re[j].reshape(nq, 1, sw)
    dec_im = pw_im[j].reshape(nq, 1, sw)
    return ktoep, win_re, win_im, wout_re, wout_im, dec_re, dec_im


def _s5_kernel(u_ref, kt_ref, wir_ref, wii_ref, wor_ref, woi_ref, dr_ref, di_ref, y_ref,
               xf_ref, yf_ref, sre_ref, sim_ref, *, bt, nk, nk_ctx, rs):
    d = pl.program_id(0)
    j = S5_J
    lanes = u_ref.shape[-1]
    for b in range(bt):
        for jj in range(j):
            xf_ref[b * nk:(b + 1) * nk, jj * lanes:(jj + 1) * lanes] = (
                u_ref.at[b][pl.ds(jj, nk, stride=j), :].astype(BF16))
    xf = xf_ref[...]
    yf_ref[...] = jnp.dot(xf, kt_ref[0, 0], preferred_element_type=F32)
    inc_re = jnp.dot(xf, wir_ref[0, 0], preferred_element_type=F32)
    inc_im = jnp.dot(xf, wii_ref[0, 0], preferred_element_type=F32)
    nl = sre_ref.shape[0]
    for b in range(bt):
        for l in range(nl):
            sre_ref[l, b * rs:b * rs + nk, :] = inc_re[b * nk:(b + 1) * nk, l * lanes:(l + 1) * lanes]
            sim_ref[l, b * rs:b * rs + nk, :] = inc_im[b * nk:(b + 1) * nk, l * lanes:(l + 1) * lanes]
    a_re = [dr_ref[0, 0, :, l * lanes:(l + 1) * lanes] for l in range(nl)]
    a_im = [di_ref[0, 0, :, l * lanes:(l + 1) * lanes] for l in range(nl)]

    def step(kidx, carry):
        rows = pl.ds(kidx, bt, stride=rs)
        out = []
        for l in range(nl):
            s_re, s_im = carry[2 * l], carry[2 * l + 1]
            i_re = sre_ref.at[l][rows, :]
            i_im = sim_ref.at[l][rows, :]
            sre_ref.at[l][rows, :] = s_re
            sim_ref.at[l][rows, :] = s_im
            out.append(a_re[l] * s_re - a_im[l] * s_im + i_re)
            out.append(a_re[l] * s_im + a_im[l] * s_re + i_im)
        return tuple(out)

    zero = tuple(jnp.zeros((bt, lanes), F32) for _ in range(2 * nl))

    @pl.when(d == 0)
    def _():
        lax.fori_loop(0, nk, step, zero)

    @pl.when(d == 1)
    def _():
        carry = lax.fori_loop(0, nk_ctx, lambda i, cr: step(nk_ctx - 1 - i, cr), zero)
        lax.fori_loop(0, nk - nk_ctx, lambda i, cr: step(nk - 1 - i, cr), carry)

    for b in range(bt):
        sp_re = jnp.concatenate([sre_ref[l, b * rs:b * rs + nk, :] for l in range(nl)], axis=-1).astype(BF16)
        sp_im = jnp.concatenate([sim_ref[l, b * rs:b * rs + nk, :] for l in range(nl)], axis=-1).astype(BF16)
        yb = (yf_ref[b * nk:(b + 1) * nk, :]
              + jnp.dot(sp_re, wor_ref[0, 0], preferred_element_type=F32)
              + jnp.dot(sp_im, woi_ref[0, 0], preferred_element_type=F32))
        for jj in range(j):
            y_ref.at[0, b][pl.ds(jj, nk, stride=j), :] = yb[:, jj * lanes:(jj + 1) * lanes]


def _s5_mixer(u, mats, n_ctx):
    bsz, lt, w = u.shape
    ktoep, win_re, win_im, wout_re, wout_im, dec_re, dec_im = mats
    lanes = 128
    bt = 4 if bsz % 4 == 0 else 2
    nq = w // lanes
    j = S5_J
    nk = lt // j
    nk_ctx = n_ctx // j
    rs = nk + 8
    fl = j * lanes
    sw = win_re.shape[-1]
    wmap = lambda d, q, b: (d, q, 0, 0)
    return pl.pallas_call(
        functools.partial(_s5_kernel, bt=bt, nk=nk, nk_ctx=nk_ctx, rs=rs),
        grid=(2, nq, bsz // bt),
        in_specs=[
            pl.BlockSpec((bt, lt, lanes), lambda d, q, b: (b, 0, q)),
            pl.BlockSpec((1, 1, fl, fl), wmap),
            pl.BlockSpec((1, 1, fl, sw), wmap),
            pl.BlockSpec((1, 1, fl, sw), wmap),
            pl.BlockSpec((1, 1, sw, fl), wmap),
            pl.BlockSpec((1, 1, sw, fl), wmap),
            pl.BlockSpec((1, 1, 1, sw), wmap),
            pl.BlockSpec((1, 1, 1, sw), wmap),
        ],
        out_specs=pl.BlockSpec((1, bt, lt, lanes), lambda d, q, b: (d, b, 0, q)),
        out_shape=jax.ShapeDtypeStruct((2, bsz, lt, w), F32),
        scratch_shapes=[pltpu.VMEM((bt * nk, fl), BF16), pltpu.VMEM((bt * nk, fl), F32),
                        pltpu.VMEM((sw // lanes, bt * rs, lanes), F32),
                        pltpu.VMEM((sw // lanes, bt * rs, lanes), F32)],
        compiler_params=_cparams(("parallel", "parallel", "arbitrary")),
        name="s5_mixer",
    )(u, ktoep, win_re, win_im, wout_re, wout_im, dec_re, dec_im)


def _head_norm(x, heads):
    dh = x.shape[-1] // heads
    outs = []
    for h in range(heads):
        xh = x[:, h * dh:(h + 1) * dh]
        outs.append(xh * lax.rsqrt(jnp.mean(xh * xh, axis=-1, keepdims=True) + EPS))
    return jnp.concatenate(outs, axis=-1)


def _even_post_kernel(m_ref, o_ref, s_ref, u_ref, mg_ref, dsk_ref, gw_ref, gb_ref, w_ref, h_ref, gate_ref,
                      out_ref, *, heads):
    m = m_ref[0, 0] + m_ref[1, 0]
    m_out = _head_norm(m, heads) * mg_ref[...] * jax.nn.sigmoid(o_ref[0])
    y = jax.nn.gelu(s_ref[0, 0] + s_ref[1, 0] + dsk_ref[...] * u_ref[0])
    glu = jnp.dot(y.astype(BF16), gw_ref[...], preferred_element_type=F32) + gb_ref[...]
    s_out = y * jax.nn.sigmoid(glu)
    cat = jnp.concatenate([m_out, s_out], axis=-1).astype(BF16)
    z = jnp.dot(cat, w_ref[...], preferred_element_type=F32)
    out_ref[0] = h_ref[0] + gate_ref[0, 0] * z


def _even_post(m2, o, s2, u, mnorm_g, d_skip, glu_w, glu_b, w_out, h, gate):
    bsz, lt, d = h.shape
    mw = o.shape[-1]
    sw = u.shape[-1]
    tm = ROW_TILE
    row = lambda b, i: (b, i, 0)
    row2 = lambda b, i: (0, b, i, 0)
    const = lambda b, i: (0, 0)
    return pl.pallas_call(
        functools.partial(_even_post_kernel, heads=MLSTM_HEADS),
        grid=(bsz, lt // tm),
        in_specs=[
            pl.BlockSpec((2, 1, tm, mw), row2),
            pl.BlockSpec((1, tm, mw), row),
            pl.BlockSpec((2, 1, tm, sw), row2),
            pl.BlockSpec((1, tm, sw), row),
            pl.BlockSpec((1, mw), const),
            pl.BlockSpec((1, sw), const),
            pl.BlockSpec((sw, sw), const),
            pl.BlockSpec((1, sw), const),
            pl.BlockSpec((mw + sw, d), const),
            pl.BlockSpec((1, tm, d), row),
            pl.BlockSpec((1, 1, 1, d), _seg_map),
        ],
        out_specs=pl.BlockSpec((1, tm, d), row),
        out_shape=jax.ShapeDtypeStruct((bsz, lt, d), F32),
        compiler_params=_cparams(("parallel", "parallel")),
        name="even_post",
    )(m2, o, s2, u, mnorm_g.reshape(1, mw), d_skip.reshape(1, sw), glu_w.astype(BF16), glu_b.reshape(1, sw),
      w_out.astype(BF16), h, gate)


def _odd_post_kernel(o_ref, g_ref, ng_ref, w_ref, h_ref, gate_ref, out_ref, *, heads):
    g = g_ref[0]
    y = _head_norm(o_ref[0, 0] + o_ref[1, 0], heads) * ng_ref[...] * (g * jax.nn.sigmoid(g))
    z = jnp.dot(y.astype(BF16), w_ref[...], preferred_element_type=F32)
    out_ref[0] = h_ref[0] + gate_ref[0, 0] * z


def _odd_post(o, g, norm_g, w_out, h, gate):
    bsz, lt, d = h.shape
    dv = o.shape[-1]
    tm = ROW_TILE
    row = lambda b, i: (b, i, 0)
    const = lambda b, i: (0, 0)
    return pl.pallas_call(
        functools.partial(_odd_post_kernel, heads=GLA_HEADS),
        grid=(bsz, lt // tm),
        in_specs=[
            pl.BlockSpec((2, 1, tm, dv), lambda b, i: (0, b, i, 0)),
            pl.BlockSpec((1, tm, dv), row),
            pl.BlockSpec((1, dv), const),
            pl.BlockSpec((dv, d), const),
            pl.BlockSpec((1, tm, d), row),
            pl.BlockSpec((1, 1, 1, d), _seg_map),
        ],
        out_specs=pl.BlockSpec((1, tm, d), row),
        out_shape=jax.ShapeDtypeStruct((bsz, lt, d), F32),
        compiler_params=_cparams(("parallel", "parallel")),
        name="odd_post",
    )(o, g, norm_g.reshape(1, dv), w_out.astype(BF16), h, gate)


def kernel(x, c, ctx, c_ctx, mod_w, mod_b, norm_mix_g, norm_ffn_g, ev_w_in, ev_b_in, ev_conv_w, ev_conv_b, ev_mlstm_norm_g, ev_s5_a_re_f, ev_s5_a_im_f, ev_s5_log_dt_f, ev_s5_a_re_b, ev_s5_a_im_b, ev_s5_log_dt_b, ev_s5_b_re, ev_s5_b_im, ev_s5_c_re, ev_s5_c_im, ev_s5_d, ev_s5_glu_w, ev_s5_glu_b, ev_w_out, od_w_in, od_gate_w2_f, od_gate_b2_f, od_gate_w2_b, od_gate_b2_b, od_norm_g, od_w_out, router_w, router_b, moe_w_gu, moe_b_gu, moe_w_down, moe_b_down, final_norm_g):
    bsz, seq, d = x.shape
    n_ctx = ctx.shape[1]
    depth = mod_w.shape[0]
    lt = n_ctx + seq
    assert n_ctx == ROW_TILE and seq % ROW_TILE == 0 and seq % GRID_W == 0

    h = jnp.concatenate([ctx, x], axis=1)
    c_all = jnp.concatenate([c, c_ctx[None, :]], axis=0)
    c_all = jnp.pad(c_all, ((0, (-c_all.shape[0]) % 8), (0, 0)))
    mods = _modulation(c_all, mod_w, mod_b)
    mod_lat = mods[:, :bsz]
    mod_ctx = jnp.broadcast_to(mods[:, bsz:bsz + 1], mod_lat.shape)
    mod6 = jnp.stack([mod_ctx, mod_lat], axis=2).reshape(depth, bsz, 2, 6, 1, d)

    bg_all = moe_b_gu[..., 0::2]
    bu_all = moe_b_gu[..., 1::2]

    mw = ev_conv_w.shape[-1] // 2
    n_gates = 4 * MLSTM_HEADS
    s5w = ev_s5_d.shape[-1]
    dk_t = od_gate_w2_f.shape[-1]
    dv_t = od_norm_g.shape[-1]
    y_buf = None
    for layer in range(depth):
        last = layer == depth - 1
        j = layer // 2
        m6 = mod6[layer]
        sh1, sc1, g1, sh2, sc2, g2 = (m6[:, :, i] for i in range(6))
        if layer % 2 == 0:
            w_in, b_in = ev_w_in[j], ev_b_in[j]
            cols = jnp.concatenate([jnp.arange(0, 4 * mw), jnp.arange(4 * mw + n_gates, 4 * mw + n_gates + s5w),
                                    jnp.arange(4 * mw, 4 * mw + n_gates)])
            qk_pre, v, o, u, gates = _nm_matmul(h, norm_mix_g[layer], sh1, sc1, w_in[:, cols], b_in[cols],
                                                (2 * mw, mw, mw, s5w, n_gates))
            dh = mw // MLSTM_HEADS
            colscale = jnp.concatenate([jnp.full((mw,), dh ** -0.5, F32), jnp.ones((mw,), F32)])
            qk = _conv_silu(qk_pre, ev_conv_w[j], ev_conv_b[j], colscale, n_ctx)
            m2 = _mlstm_mixer(qk, v, gates, n_ctx)
            shared = (ev_s5_b_re[j], ev_s5_b_im[j], ev_s5_c_re[j], ev_s5_c_im[j])
            mats_f = _s5_matrices(ev_s5_a_re_f[j], ev_s5_a_im_f[j], ev_s5_log_dt_f[j], *shared, backward=False)
            mats_b = _s5_matrices(ev_s5_a_re_b[j], ev_s5_a_im_b[j], ev_s5_log_dt_b[j], *shared, backward=True)
            s2 = _s5_mixer(u, tuple(jnp.stack([a, b]) for a, b in zip(mats_f, mats_b)), n_ctx)
            h = _even_post(m2, o, s2, u, ev_mlstm_norm_g[j], ev_s5_d[j], ev_s5_glu_w[j], ev_s5_glu_b[j],
                           ev_w_out[j], h, g1)
        else:
            hc = _grid_reorder(h, n_ctx, True)
            qq, kk, vv, gg, rr = _nm_matmul(hc, norm_mix_g[layer], sh1, sc1, od_w_in[j],
                                            jnp.zeros((od_w_in.shape[-1],), F32),
                                            (dk_t, dk_t, dv_t, dv_t, 2 * GLA_RANK))
            zero = jnp.zeros_like(od_gate_w2_f[j])
            w2 = jnp.stack([jnp.concatenate([od_gate_w2_f[j], zero], axis=0),
                            jnp.concatenate([zero, od_gate_w2_b[j]], axis=0)])
            b2 = jnp.stack([od_gate_b2_f[j], od_gate_b2_b[j]])[:, None, :]
            o2 = _gla_mixer(qq, kk, vv, rr, w2, b2, n_ctx)
            h = _grid_reorder(_odd_post(o2, gg, od_norm_g[j], od_w_out[j], hc, g1), n_ctx, False)
        f, top_e, gate = _ffn_prep(h, norm_ffn_g[layer], sh2, sc2, router_w[layer], router_b[layer])
        weights = (layer, moe_w_gu, moe_w_down, bg_all[layer], bu_all[layer], moe_b_down[layer])
        h, y_buf = _moe_layer(h, f, top_e, gate, g2, weights, last, n_ctx, y_buf)
    return _final_norm(h, final_norm_g)
```

```python
import functools

import jax
import jax.numpy as jnp
from jax import lax
from jax.experimental import pallas as pl
from jax.experimental.pallas import tpu as pltpu

F32 = jnp.float32
BF16 = jnp.bfloat16
HI = lax.Precision.HIGHEST

EPS = 1e-6
GRID_W = 64
MLSTM_HEADS = 4
S5_GROUP = 16
GLA_HEADS = 4
GLA_RANK = 16
GLA_TAU = 16.0
N_EXPERTS = 32
TOP_K = 4
SWIGLU_LIMIT = 7.0
SWIGLU_ALPHA = 1.702

ROW_TILE = 256
MOE_TILE = 512
MOE_PARTS = 4
MIX_CHUNK = 64
MLSTM_CHUNK = 128
MIX_BATCH = 4
S5_J = 8
VMEM_LIMIT = 56 * 1024 * 1024

NT = (((1,), (1,)), ((), ()))
TN = (((0,), (0,)), ((), ()))


def _cparams(sem):
    return pltpu.CompilerParams(dimension_semantics=sem, vmem_limit_bytes=VMEM_LIMIT)


def _mod_kernel(c_ref, w_ref, b_ref, o_ref):
    c = c_ref[...]
    a = c * jax.nn.sigmoid(c)
    o_ref[0] = jnp.dot(a.astype(BF16), w_ref[0].astype(BF16), preferred_element_type=F32) + b_ref[0]


def _modulation(c_all, mod_w, mod_b):
    depth, d, n6 = mod_w.shape
    rows = c_all.shape[0]
    tn = d
    return pl.pallas_call(
        _mod_kernel,
        grid=(depth, n6 // tn),
        in_specs=[
            pl.BlockSpec((rows, d), lambda l, j: (0, 0)),
            pl.BlockSpec((1, d, tn), lambda l, j: (l, 0, j)),
            pl.BlockSpec((1, 1, tn), lambda l, j: (l, 0, j)),
        ],
        out_specs=pl.BlockSpec((1, rows, tn), lambda l, j: (l, 0, j)),
        out_shape=jax.ShapeDtypeStruct((depth, rows, n6), F32),
        compiler_params=_cparams(("arbitrary", "arbitrary")),
        name="modulation",
    )(c_all, mod_w, mod_b.reshape(depth, 1, n6))


def _norm_mod(x, g, sh, sc):
    ms = jnp.mean(x * x, axis=-1, keepdims=True)
    return (x * lax.rsqrt(ms + EPS) * g) * (1.0 + sc) + sh


def _nm_matmul_kernel(x_ref, g_ref, sh_ref, sc_ref, w_ref, b_ref, *out_refs, splits):
    a = _norm_mod(x_ref[0], g_ref[...], sh_ref[0, 0], sc_ref[0, 0])
    z = jnp.dot(a.astype(BF16), w_ref[...], preferred_element_type=F32) + b_ref[...]
    for (lo, hi), o_ref in zip(splits, out_refs):
        o_ref[0] = z[:, lo:hi].astype(o_ref.dtype)


def _wide_tile(rows):
    return next(t for t in (4 * ROW_TILE, 3 * ROW_TILE, 2 * ROW_TILE, ROW_TILE) if rows % t == 0)


def _seg_map(b, i):
    return (b, jnp.minimum(i, 1), 0, 0)


def _nm_matmul(h, g, shift, scale, w, bias, widths):
    bsz, lt, d = h.shape
    p = w.shape[1]
    splits, lo = [], 0
    for wd in widths:
        splits.append((lo, lo + wd))
        lo += wd
    assert lo == p
    tm = ROW_TILE
    return pl.pallas_call(
        functools.partial(_nm_matmul_kernel, splits=tuple(splits)),
        grid=(bsz, lt // tm),
        in_specs=[
            pl.BlockSpec((1, tm, d), lambda b, i: (b, i, 0)),
            pl.BlockSpec((1, d), lambda b, i: (0, 0)),
            pl.BlockSpec((1, 1, 1, d), _seg_map),
            pl.BlockSpec((1, 1, 1, d), _seg_map),
            pl.BlockSpec((d, p), lambda b, i: (0, 0)),
            pl.BlockSpec((1, p), lambda b, i: (0, 0)),
        ],
        out_specs=[pl.BlockSpec((1, tm, wd), lambda b, i: (b, i, 0)) for wd in widths],
        out_shape=[jax.ShapeDtypeStruct((bsz, lt, wd), F32) for wd in widths],
        compiler_params=_cparams(("parallel", "parallel")),
        name="norm_mod_matmul",
    )(h, g.reshape(1, d), shift, scale, w.astype(BF16), bias.reshape(1, p))


def _ffn_prep_kernel(x_ref, g_ref, sh_ref, sc_ref, rw_ref, rb_ref, f_ref, te_ref, gt_ref, *, n_ctx):
    tm = x_ref.shape[1]
    is_ctx = pl.program_id(1) * tm + lax.broadcasted_iota(jnp.int32, (tm, 1), 0) < n_ctx
    a = _norm_mod(x_ref[0], g_ref[...], jnp.where(is_ctx, sh_ref[0, 0], sh_ref[0, 1]),
                  jnp.where(is_ctx, sc_ref[0, 0], sc_ref[0, 1]))
    f_ref[0] = a.astype(f_ref.dtype)
    logits = lax.dot_general(rw_ref[...], a.astype(BF16), NT, preferred_element_type=F32) + rb_ref[...]
    ne = logits.shape[0]
    eidx = lax.broadcasted_iota(jnp.int32, logits.shape, 0)
    work = logits
    vals, idxs = [], []
    for _ in range(TOP_K):
        m = jnp.max(work, axis=0, keepdims=True)
        idx = jnp.min(jnp.where(work == m, eidx, ne), axis=0, keepdims=True)
        vals.append(m)
        idxs.append(idx)
        work = jnp.where(eidx == idx, -jnp.inf, work)
    exps = [jnp.exp(v - vals[0]) for v in vals]
    denom = exps[0]
    for e in exps[1:]:
        denom = denom + e
    for k in range(TOP_K):
        te_ref[0, k:k + 1, :] = idxs[k]
        gt_ref[0, k:k + 1, :] = exps[k] / denom


def _ffn_prep(h, g, shift, scale, router_w, router_b):
    bsz, lt, d = h.shape
    ne = router_w.shape[1]
    tm = _wide_tile(lt)
    both = pl.BlockSpec((1, 2, 1, d), lambda b, i: (b, 0, 0, 0))
    return pl.pallas_call(
        functools.partial(_ffn_prep_kernel, n_ctx=ROW_TILE),
        grid=(bsz, lt // tm),
        in_specs=[
            pl.BlockSpec((1, tm, d), lambda b, i: (b, i, 0)),
            pl.BlockSpec((1, d), lambda b, i: (0, 0)),
            both,
            both,
            pl.BlockSpec((ne, d), lambda b, i: (0, 0)),
            pl.BlockSpec((ne, 1), lambda b, i: (0, 0)),
        ],
        out_specs=[
            pl.BlockSpec((1, tm, d), lambda b, i: (b, i, 0)),
            pl.BlockSpec((1, TOP_K, tm), lambda b, i: (b, 0, i)),
            pl.BlockSpec((1, TOP_K, tm), lambda b, i: (b, 0, i)),
        ],
        out_shape=[
            jax.ShapeDtypeStruct((bsz, lt, d), BF16),
            jax.ShapeDtypeStruct((bsz, TOP_K, lt), jnp.int32),
            jax.ShapeDtypeStruct((bsz, TOP_K, lt), F32),
        ],
        compiler_params=_cparams(("parallel", "parallel")),
        name="ffn_prep",
    )(h, g.reshape(1, d), shift, scale, router_w.T.astype(BF16), router_b.reshape(ne, 1))


GU_BLOCK = 256


def _moe_kernel(be_ref, nb_ref, first_ref, slot_ref, nxt_ref, x_ref, wgu_hbm, wd_hbm, bg_ref, bu_ref, bd_ref,
                *rest, layer, in_place):
    o_ref, wgu_buf, wd_buf, wgu_s, wd_s, sem = rest[1:] if in_place else rest
    i = pl.program_id(0)
    active = i < nb_ref[0]
    half = GU_BLOCK // 2
    nblk = wgu_s.shape[1] // GU_BLOCK

    def weight_copies(e, slot):
        return (pltpu.make_async_copy(wgu_hbm.at[layer, e], wgu_buf.at[slot], sem.at[0, slot]),
                pltpu.make_async_copy(wd_hbm.at[layer, e], wd_buf.at[slot], sem.at[1, slot]))

    @pl.when(active & (i == 0))
    def _():
        for cp in weight_copies(be_ref[0], 0):
            cp.start()

    @pl.when(active & (first_ref[i] == 1))
    def _():
        slot = slot_ref[i]
        for cp in weight_copies(be_ref[i], slot):
            cp.wait()

        @pl.when(nxt_ref[i] >= 0)
        def _():
            for cp in weight_copies(nxt_ref[i], 1 - slot):
                cp.start()

        r = lax.broadcasted_iota(jnp.int32, (GU_BLOCK, GU_BLOCK), 0)
        c = lax.broadcasted_iota(jnp.int32, (GU_BLOCK, GU_BLOCK), 1)
        perm = (r == jnp.where(c < half, 2 * c, 2 * (c - half) + 1)).astype(BF16)
        for k in range(nblk):
            cs = slice(k * GU_BLOCK, (k + 1) * GU_BLOCK)
            wgu_s[:, cs] = jnp.dot(wgu_buf[slot, :, cs].astype(BF16), perm,
                                   preferred_element_type=F32).astype(BF16)
        wd_s[...] = wd_buf[slot].astype(BF16)

    @pl.when(active)
    def _():
        gu = jnp.dot(x_ref[...], wgu_s[...], preferred_element_type=F32)
        hdn = []
        for k in range(nblk):
            hs = slice(k * half, (k + 1) * half)
            g = gu[:, k * GU_BLOCK:k * GU_BLOCK + half] + bg_ref[0, :, hs]
            u = gu[:, k * GU_BLOCK + half:(k + 1) * GU_BLOCK] + bu_ref[0, :, hs]
            g = jnp.minimum(g, SWIGLU_LIMIT)
            u = jnp.clip(u, -SWIGLU_LIMIT, SWIGLU_LIMIT)
            hdn.append(((u + 1.0) * (g * jax.nn.sigmoid(SWIGLU_ALPHA * g))).astype(BF16))
        hdn = jnp.concatenate(hdn, axis=-1)
        o_ref[...] = (jnp.dot(hdn, wd_s[...], preferred_element_type=F32) + bd_ref[0]).astype(o_ref.dtype)

    @pl.when(jnp.logical_not(active))
    def _():
        o_ref[...] = jnp.zeros_like(o_ref)


def _moe_experts(x_sorted, block_expert, n_used, layer, w_gu, w_down, bg, bu, bd, block_off, y_prev, out_rows):
    n_rows, d = x_sorted.shape
    _, ne, _, f2 = w_gu.shape
    f = f2 // 2
    tm = MOE_TILE
    x_blocks = n_rows // tm
    in_place = y_prev is not None
    n_blocks = x_blocks if in_place else out_rows // tm - block_off
    assert f2 % GU_BLOCK == 0 and block_expert.shape[0] == n_blocks
    blk = jnp.arange(n_blocks, dtype=jnp.int32)
    prev = jnp.concatenate([block_expert[:1], block_expert[:-1]])
    first = (blk < n_used[0]) & ((blk == 0) | (block_expert != prev))
    slot = (jnp.cumsum(first.astype(jnp.int32)) - 1) & 1
    first_idx = jnp.where(first, blk, n_blocks)
    next_first = lax.cummin(first_idx, axis=0, reverse=True)
    next_first = jnp.concatenate([next_first[1:], jnp.full((1,), n_blocks, jnp.int32)])
    nxt = jnp.where(next_first < n_blocks, block_expert[jnp.minimum(next_first, n_blocks - 1)], -1)
    bmap = lambda i, be, nb, fi, sl, nx: (be[i], 0, 0)
    rmap = lambda i, be, nb, fi, sl, nx: (jnp.minimum(i, x_blocks - 1), 0)
    grid_spec = pltpu.PrefetchScalarGridSpec(
        num_scalar_prefetch=5,
        grid=(n_blocks,),
        in_specs=[
            pl.BlockSpec((tm, d), rmap),
            pl.BlockSpec(memory_space=pl.ANY),
            pl.BlockSpec(memory_space=pl.ANY),
            pl.BlockSpec((1, 1, f), bmap),
            pl.BlockSpec((1, 1, f), bmap),
            pl.BlockSpec((1, 1, d), bmap),
        ] + ([pl.BlockSpec(memory_space=pl.ANY)] if in_place else []),
        out_specs=pl.BlockSpec((tm, d), lambda i, be, nb, fi, sl, nx: (i + block_off, 0)),
        scratch_shapes=[pltpu.VMEM((2, d, f2), F32), pltpu.VMEM((2, f, d), F32),
                        pltpu.VMEM((d, f2), BF16), pltpu.VMEM((f, d), BF16),
                        pltpu.SemaphoreType.DMA((2, 2))],
    )
    operands = (block_expert, n_used, first.astype(jnp.int32), slot.astype(jnp.int32), nxt.astype(jnp.int32),
                x_sorted, w_gu, w_down, bg.reshape(ne, 1, f), bu.reshape(ne, 1, f), bd.reshape(ne, 1, d))
    if in_place:
        assert y_prev.shape == (out_rows, d)
        operands = operands + (y_prev,)
    return pl.pallas_call(
        functools.partial(_moe_kernel, layer=layer, in_place=in_place),
        grid_spec=grid_spec,
        out_shape=jax.ShapeDtypeStruct((out_rows, d), BF16),
        input_output_aliases={len(operands) - 1: 0} if in_place else {},
        compiler_params=_cparams(("arbitrary",)),
        name="moe_experts",
    )(*operands)


def _combine_kernel(y_ref, gt_ref, h_ref, g2_ref, o_ref, *, n_ctx):
    gt = gt_ref[0]
    acc = y_ref[0, 0].astype(F32) * gt[:, 0:1]
    for k in range(1, TOP_K):
        acc = acc + y_ref[k, 0].astype(F32) * gt[:, k:k + 1]
    tm = acc.shape[0]
    is_ctx = pl.program_id(1) * tm + lax.broadcasted_iota(jnp.int32, (tm, 1), 0) < n_ctx
    o_ref[0] = h_ref[0] + jnp.where(is_ctx, g2_ref[0, 0], g2_ref[0, 1]) * acc


def _moe_combine(yg, gate, h, g2, lat_only):
    k, bsz, lt, d = yg.shape
    tm = ROW_TILE if lat_only else _wide_tile(lt)
    tile_off = 1 if lat_only else 0
    return pl.pallas_call(
        functools.partial(_combine_kernel, n_ctx=0 if lat_only else ROW_TILE),
        grid=(bsz, lt // tm),
        in_specs=[
            pl.BlockSpec((k, 1, tm, d), lambda b, i: (0, b, i, 0)),
            pl.BlockSpec((1, tm, k), lambda b, i: (b, i, 0)),
            pl.BlockSpec((1, tm, d), lambda b, i: (b, i + tile_off, 0)),
            pl.BlockSpec((1, 2, 1, d), lambda b, i: (b, 0, 0, 0)),
        ],
        out_specs=pl.BlockSpec((1, tm, d), lambda b, i: (b, i, 0)),
        out_shape=jax.ShapeDtypeStruct((bsz, lt, d), F32),
        compiler_params=_cparams(("parallel", "parallel")),
        name="moe_combine",
    )(yg, gate, h, g2)


def _route_kernel(e_ref, pos_ref, cnt_ref, *, tm):
    nk, rows, lanes = e_ref.shape
    li = lax.broadcasted_iota(jnp.int32, (lanes, lanes), 0)
    lj = lax.broadcasted_iota(jnp.int32, (lanes, lanes), 1)
    before_lane = (li < lj).astype(BF16)
    ones = jnp.ones((lanes, lanes), BF16)
    ri = lax.broadcasted_iota(jnp.int32, (rows, rows), 0)
    rj = lax.broadcasted_iota(jnp.int32, (rows, rows), 1)
    before_row = (rj < ri).astype(BF16)
    lane = lax.broadcasted_iota(jnp.int32, (1, lanes), 1)
    xs = [e_ref[k] for k in range(nk)]
    pos = [jnp.zeros((rows, lanes), F32) for _ in range(nk)]
    counts = jnp.zeros((1, lanes), F32)
    pad_off = jnp.zeros((1, lanes), F32)
    for e in range(N_EXPERTS):
        ms = [x == e for x in xs]
        hit = ms[0]
        for m in ms[1:]:
            hit = hit | m
        mb = hit.astype(BF16)
        in_row = jnp.dot(mb, before_lane, preferred_element_type=F32)
        row_sum = jnp.dot(mb, ones, preferred_element_type=F32)
        row_off = jnp.dot(before_row, row_sum.astype(BF16), preferred_element_type=F32)
        count = row_off[rows - 1:rows, :] + row_sum[rows - 1:rows, :]
        dest = in_row + row_off + pad_off
        pos = [p + jnp.where(m, dest, 0.0) for p, m in zip(pos, ms)]
        counts = jnp.where(lane == e, count, counts)
        pad_off = pad_off + jnp.floor((count + (tm - 1)) * (1.0 / tm)) * tm
    for k in range(nk):
        pos_ref[k] = pos[k].astype(jnp.int32)
    cnt_ref[...] = counts.astype(jnp.int32)


def _route_positions(top_e, tm):
    lanes = 128
    nk, n = top_e.shape
    rows = n // lanes
    pos, counts = pl.pallas_call(
        functools.partial(_route_kernel, tm=tm),
        out_shape=[jax.ShapeDtypeStruct((nk, rows, lanes), jnp.int32), jax.ShapeDtypeStruct((1, lanes), jnp.int32)],
        compiler_params=pltpu.CompilerParams(vmem_limit_bytes=VMEM_LIMIT),
        name="route_positions",
    )(top_e.reshape(nk, rows, lanes))
    return pos.reshape(nk, n), counts[0, :N_EXPERTS]


def _moe_rows(n_tokens):
    n_assign = n_tokens * TOP_K
    return -(-(n_assign + N_EXPERTS * (MOE_TILE - 1)) // MOE_TILE) * MOE_TILE


def _moe_layer(h, f, top_e, gate, g2, weights, lat_only, n_ctx, y_buf):
    layer, w_gu, w_down, bg, bu, bd = weights
    bsz, lt, d = h.shape
    skip = n_ctx if lat_only else 0
    ltok = lt - skip
    top_e, gate = top_e[:, :, skip:], gate[:, :, skip:]
    n = bsz * ltok
    n_assign = n * TOP_K
    tm = MOE_TILE
    te = top_e.transpose(1, 0, 2).reshape(TOP_K, n).astype(jnp.int32)
    bits = max(n - 1, 1).bit_length()
    assert N_EXPERTS << bits < 2 ** 31
    keys = (te << bits) | jnp.arange(n, dtype=jnp.int32)[None, :]
    slot_token = lax.sort(keys.reshape(-1)) & ((1 << bits) - 1)
    pos, counts = _route_positions(te, tm)
    start = jnp.cumsum(counts) - counts
    padded = (counts + tm - 1) // tm * tm
    pad_end = jnp.cumsum(padded)
    pad_start = pad_end - padded
    n_rows = _moe_rows(n)
    n_blocks = n_rows // tm
    buf_rows = n_rows if y_buf is None else y_buf.shape[0]
    assert buf_rows >= n_rows
    block_expert = jnp.minimum(
        jnp.searchsorted(pad_end, jnp.arange(n_blocks, dtype=jnp.int32) * tm, side='right', method='compare_all'),
        N_EXPERTS - 1).astype(jnp.int32)
    n_used = (pad_end[-1] // tm).astype(jnp.int32).reshape(1)
    row = jnp.arange(n_rows, dtype=jnp.int32).reshape(n_blocks, tm)
    blk_shift = (start - pad_start)[block_expert][:, None]
    blk_end = (pad_start + counts)[block_expert][:, None]
    slot = jnp.clip(row + blk_shift, 0, n_assign - 1).reshape(-1)
    row_token = jnp.where((row < blk_end).reshape(-1),
                          slot_token.at[slot].get(mode='promise_in_bounds'), row.reshape(-1) % n)
    row_src = row_token + skip * (row_token // ltok + 1)
    f2d = f.reshape(bsz * lt, d)
    cuts = [n_blocks * p // MOE_PARTS for p in range(MOE_PARTS + 1)]
    y = y_buf
    for lo, hi in zip(cuts[:-1], cuts[1:]):
        x_part = f2d.at[row_src[lo * tm:hi * tm]].get(mode='promise_in_bounds')
        blocks = block_expert[lo:hi] if y is not None else block_expert[lo:]
        y = _moe_experts(x_part, blocks, jnp.clip(n_used - lo, 0, hi - lo), layer,
                         w_gu, w_down, bg, bu, bd, lo, y, buf_rows)
    yg = y.at[pos.reshape(-1)].get(mode='promise_in_bounds').reshape(TOP_K, bsz, ltok, d)
    return _moe_combine(yg, gate.transpose(0, 2, 1), h, g2, lat_only), y


def _rmsnorm_kernel(x_ref, g_ref, o_ref):
    x = x_ref[0]
    ms = jnp.mean(x * x, axis=-1, keepdims=True)
    o_ref[0] = x * lax.rsqrt(ms + EPS) * g_ref[...]


def _final_norm(h, g):
    bsz, lt, d = h.shape
    tm = _wide_tile(lt)
    return pl.pallas_call(
        _rmsnorm_kernel,
        grid=(bsz, lt // tm),
        in_specs=[pl.BlockSpec((1, tm, d), lambda b, i: (b, i, 0)),
                  pl.BlockSpec((1, d), lambda b, i: (0, 0))],
        out_specs=pl.BlockSpec((1, tm, d), lambda b, i: (b, i, 0)),
        out_shape=jax.ShapeDtypeStruct((bsz, lt, d), F32),
        compiler_params=_cparams(("parallel", "parallel")),
        name="final_norm",
    )(h, g.reshape(1, d))


def _grid_reorder_kernel(x_ref, o_ref, *, n_ctx, rows, to_cols):
    o_ref[0, :n_ctx, :] = x_ref[0, :n_ctx, :]
    for c in range(GRID_W):
        raster = pl.ds(n_ctx + c, rows, stride=GRID_W)
        dense = pl.ds(n_ctx + c * rows, rows)
        if to_cols:
            o_ref.at[0][dense, :] = x_ref.at[0][raster, :]
        else:
            o_ref.at[0][raster, :] = x_ref.at[0][dense, :]


def _grid_reorder(h, n_ctx, to_cols):
    bsz, lt, d = h.shape
    lanes = 128
    spec = pl.BlockSpec((1, lt, lanes), lambda b, j: (b, 0, j))
    return pl.pallas_call(
        functools.partial(_grid_reorder_kernel, n_ctx=n_ctx, rows=(lt - n_ctx) // GRID_W, to_cols=to_cols),
        grid=(bsz, d // lanes),
        in_specs=[spec],
        out_specs=spec,
        out_shape=jax.ShapeDtypeStruct(h.shape, h.dtype),
        compiler_params=_cparams(("parallel", "parallel")),
        name="grid_reorder",
    )(h)


def _chunk_order(d, c, n_ctx_chunks, n_chunks):
    bwd = jnp.where(c < n_ctx_chunks, n_ctx_chunks - 1 - c, n_chunks + n_ctx_chunks - 1 - c)
    return jnp.where(d == 0, c, bwd)


def _split_bf16(x, n):
    out = []
    for _ in range(n):
        p = x.astype(BF16)
        out.append(p)
        x = x - p.astype(F32)
    return out


def _dir_tri(d, t):
    row = lax.broadcasted_iota(jnp.int32, (t, t), 0)
    col = lax.broadcasted_iota(jnp.int32, (t, t), 1)
    return jnp.where(d == 0, col - row, row - col) <= 0


def _gla_kernel(q_ref, k_ref, v_ref, r_ref, w2_ref, b2_ref, o_ref, st_ref, *, t, heads, scale):
    d = pl.program_id(0)
    c = pl.program_id(2)

    @pl.when(c == 0)
    def _():
        st_ref[...] = jnp.zeros_like(st_ref)

    dk = q_ref.shape[-1] // heads
    dv = v_ref.shape[-1] // heads
    nb = q_ref.shape[0]
    mask = _dir_tri(d, t)
    tri = mask.astype(F32)
    mid = t // 2
    items = [(bb, h) for bb in range(nb) for h in range(heads)]
    w_hi, w_lo = _split_bf16(w2_ref[0], 2)
    xs = []
    for bb in range(nb):
        r_hi, r_lo = _split_bf16(r_ref[bb], 2)
        xs.append(jnp.dot(r_hi, w_hi, preferred_element_type=F32) + jnp.dot(r_hi, w_lo, preferred_element_type=F32)
                  + jnp.dot(r_lo, w_hi, preferred_element_type=F32) + b2_ref[0])
    las = [jax.nn.log_sigmoid(x) * (1.0 / GLA_TAU) for x in xs]
    tri_b = tri.astype(BF16)
    bs = [sum(jnp.dot(tri_b, p, preferred_element_type=F32) for p in _split_bf16(la, 3)) for la in las]
    qt, kt, qe, kh_end, e_end = [], [], [], [], []
    for bb in range(nb):
        b = bs[bb]
        b_m = b[mid:mid + 1, :]
        b_end = jnp.where(d == 0, b[t - 1:t, :], b[0:1, :])
        q_s = q_ref[bb] * (jnp.exp(b - b_m) * scale)
        k_s = k_ref[bb] * jnp.exp(b_m - b)
        qe.append((q_s * jnp.exp(b_m)).astype(BF16))
        kh_end.append((k_s * jnp.exp(b_end - b_m)).astype(BF16))
        e_end.append(jnp.exp(b_end))
        qt.append(q_s.astype(BF16))
        kt.append(k_s.astype(BF16))
    att, q_st, vs = {}, {}, {}
    for bb, h in items:
        ks = slice(h * dk, (h + 1) * dk)
        vs[bb, h] = v_ref[bb, :, h * dv:(h + 1) * dv].astype(BF16)
        att[bb, h] = lax.dot_general(qt[bb][:, ks], kt[bb][:, ks], NT, preferred_element_type=F32)
        q_st[bb, h] = lax.dot_general(qe[bb][:, ks], st_ref[bb * heads + h].astype(BF16), NT,
                                      preferred_element_type=F32)
    for bb, h in items:
        a = jnp.where(mask, att[bb, h], 0.0).astype(BF16)
        o_ref[0, bb, :, h * dv:(h + 1) * dv] = jnp.dot(a, vs[bb, h], preferred_element_type=F32) + q_st[bb, h]
    for bb, h in items:
        ks = slice(h * dk, (h + 1) * dk)
        upd = lax.dot_general(vs[bb, h], kh_end[bb][:, ks], TN, preferred_element_type=F32)
        st_ref[bb * heads + h] = st_ref[bb * heads + h] * e_end[bb][:, ks] + upd


def _gla_mixer(q, k, v, r, w2, b2, n_ctx):
    bsz, lt, dkt = q.shape
    dvt = v.shape[-1]
    nr = r.shape[-1]
    t, heads = MIX_CHUNK, GLA_HEADS
    bt = MIX_BATCH
    nch = lt // t
    ncc = n_ctx // t
    dk = dkt // heads
    dv = dvt // heads
    imap = lambda d, b, c: (b, _chunk_order(d, c, ncc, nch), 0)
    return pl.pallas_call(
        functools.partial(_gla_kernel, t=t, heads=heads, scale=dk ** -0.5),
        grid=(2, bsz // bt, nch),
        in_specs=[
            pl.BlockSpec((bt, t, dkt), imap),
            pl.BlockSpec((bt, t, dkt), imap),
            pl.BlockSpec((bt, t, dvt), imap),
            pl.BlockSpec((bt, t, nr), imap),
            pl.BlockSpec((1, nr, dkt), lambda d, b, c: (d, 0, 0)),
            pl.BlockSpec((1, 1, dkt), lambda d, b, c: (d, 0, 0)),
        ],
        out_specs=pl.BlockSpec((1, bt, t, dvt), lambda d, b, c: (d, b, _chunk_order(d, c, ncc, nch), 0)),
        out_shape=jax.ShapeDtypeStruct((2, bsz, lt, dvt), F32),
        scratch_shapes=[pltpu.VMEM((bt * heads, dv, dk), F32)],
        compiler_params=_cparams(("parallel", "parallel", "arbitrary")),
        name="gla_mixer",
    )(q, k, v, r, w2, b2)


def _mlstm_kernel(q_ref, k_ref, v_ref, gc_ref, gr_ref, o_ref, c_ref, n_ref, m_ref, *, t, heads):
    d = pl.program_id(0)
    c = pl.program_id(2)

    @pl.when(c == 0)
    def _():
        c_ref[...] = jnp.zeros_like(c_ref)
        n_ref[...] = jnp.zeros_like(n_ref)
        m_ref[...] = jnp.zeros_like(m_ref)

    dh = q_ref.shape[-1] // heads
    nb = q_ref.shape[0]
    mask = _dir_tri(d, t)
    tri = mask.astype(F32)
    items = [(bb, h) for bb in range(nb) for h in range(heads)]
    gate = []
    for bb in range(nb):
        gc = gc_ref[0, bb]
        gr = gr_ref[0, bb, 0]
        fc = jax.nn.log_sigmoid(gc[:, heads:])
        fr = jax.nn.log_sigmoid(gr[heads:, :])
        b_col = jnp.dot(tri, fc, preferred_element_type=F32, precision=HI)
        b_row = lax.dot_general(fr, tri, NT, preferred_element_type=F32, precision=HI)
        b_last = jnp.where(d == 0, b_col[t - 1:t, :], b_col[0:1, :])
        gate.append((gc[:, :heads], gr[:heads, :], b_col, b_row, b_last))
    qs, ks, vs, s_raw, q_c = {}, {}, {}, {}, {}
    for bb, h in items:
        hs = slice(h * dh, (h + 1) * dh)
        qs[bb, h] = q_ref[bb, :, hs]
        ks[bb, h] = k_ref[bb, :, hs]
        vs[bb, h] = v_ref[bb, :, hs].astype(BF16)
        s_raw[bb, h] = lax.dot_general(qs[bb, h], ks[bb, h], NT, preferred_element_type=F32)
        q_c[bb, h] = jnp.dot(qs[bb, h], c_ref[bb * heads + h].astype(BF16), preferred_element_type=F32)
    logw, log_inter, m_t, w_inter, scores, den, qn = {}, {}, {}, {}, {}, {}, {}
    for bb, h in items:
        _, ir, b_col, b_row, _ = gate[bb]
        bc = b_col[:, h:h + 1]
        logw[bb, h] = jnp.where(mask, bc - b_row[h:h + 1, :] + ir[h:h + 1, :], -jnp.inf)
        log_inter[bb, h] = bc + m_ref[bb * heads + h]
    for bb, h in items:
        m_t[bb, h] = jnp.maximum(log_inter[bb, h], jnp.max(logw[bb, h], axis=-1, keepdims=True))
        qn[bb, h] = jnp.sum(qs[bb, h].astype(F32) * n_ref[bb * heads + h], axis=-1, keepdims=True)
    for bb, h in items:
        w_inter[bb, h] = jnp.exp(log_inter[bb, h] - m_t[bb, h])
        scores[bb, h] = s_raw[bb, h] * jnp.exp(logw[bb, h] - m_t[bb, h])
    for bb, h in items:
        den[bb, h] = jnp.sum(scores[bb, h], axis=-1, keepdims=True) + w_inter[bb, h] * qn[bb, h]
    num = {}
    for bb, h in items:
        num[bb, h] = (jnp.dot(scores[bb, h].astype(BF16), vs[bb, h], preferred_element_type=F32)
                      + w_inter[bb, h] * q_c[bb, h])
    for bb, h in items:
        hs = slice(h * dh, (h + 1) * dh)
        o_ref[0, bb, :, hs] = num[bb, h] / jnp.maximum(jnp.abs(den[bb, h]), jnp.exp(-m_t[bb, h]))
    log_g, m_new, kw, upd, ksum = {}, {}, {}, {}, {}
    for bb, h in items:
        ic, _, b_col, _, b_last = gate[bb]
        log_g[bb, h] = b_last[:, h:h + 1] - b_col[:, h:h + 1] + ic[:, h:h + 1]
    for bb, h in items:
        b_last = gate[bb][4]
        m_new[bb, h] = jnp.maximum(b_last[:, h:h + 1] + m_ref[bb * heads + h],
                                   jnp.max(log_g[bb, h], axis=0, keepdims=True))
    for bb, h in items:
        kw[bb, h] = ks[bb, h].astype(F32) * jnp.exp(log_g[bb, h] - m_new[bb, h])
    for bb, h in items:
        upd[bb, h] = lax.dot_general(kw[bb, h].astype(BF16), vs[bb, h], TN, preferred_element_type=F32)
        ksum[bb, h] = jnp.sum(kw[bb, h], axis=0, keepdims=True)
    for bb, h in items:
        si = bb * heads + h
        b_last = gate[bb][4]
        keep = jnp.exp(b_last[:, h:h + 1] + m_ref[si] - m_new[bb, h])
        c_ref[si] = keep * c_ref[si] + upd[bb, h]
        n_ref[si] = keep * n_ref[si] + ksum[bb, h]
        m_ref[si] = m_new[bb, h]


def _mlstm_mixer(qk, v, gates, n_ctx):
    bsz, lt, w2 = qk.shape
    w = w2 // 2
    t, heads = MLSTM_CHUNK, MLSTM_HEADS
    dh = w // heads
    nch = lt // t
    ncc = n_ctx // t
    gc = gates.reshape(bsz, lt, 2, 2 * heads).transpose(2, 0, 1, 3)
    gr = gc.reshape(2, bsz, nch, t, 2 * heads).transpose(0, 1, 2, 4, 3)
    cmap = lambda d, b, c: _chunk_order(d, c, ncc, nch)
    bt = MIX_BATCH
    return pl.pallas_call(
        functools.partial(_mlstm_kernel, t=t, heads=heads),
        grid=(2, bsz // bt, nch),
        in_specs=[
            pl.BlockSpec((bt, t, w), lambda d, b, c: (b, cmap(d, b, c), 0)),
            pl.BlockSpec((bt, t, w), lambda d, b, c: (b, cmap(d, b, c), 1)),
            pl.BlockSpec((bt, t, w), lambda d, b, c: (b, cmap(d, b, c), 0)),
            pl.BlockSpec((1, bt, t, 2 * heads), lambda d, b, c: (d, b, cmap(d, b, c), 0)),
            pl.BlockSpec((1, bt, 1, 2 * heads, t), lambda d, b, c: (d, b, cmap(d, b, c), 0, 0)),
        ],
        out_specs=pl.BlockSpec((1, bt, t, w), lambda d, b, c: (d, b, cmap(d, b, c), 0)),
        out_shape=jax.ShapeDtypeStruct((2, bsz, lt, w), F32),
        scratch_shapes=[pltpu.VMEM((bt * heads, dh, dh), F32), pltpu.VMEM((bt * heads, 1, dh), F32),
                        pltpu.VMEM((bt * heads, 1, 1), F32)],
        compiler_params=_cparams(("parallel", "parallel", "arbitrary")),
        name="mlstm_mixer",
    )(qk, qk, v, gc, gr)


def _conv_kernel(x_ref, w_ref, b_ref, s_ref, o_ref, *, n_ctx):
    x = x_ref[0]
    lt = x.shape[0]
    row = lax.broadcasted_iota(jnp.int32, x.shape, 0)
    prev = jnp.where((row == 0) | (row == n_ctx), 0.0, pltpu.roll(x, 1, 0))
    nxt = jnp.where((row == n_ctx - 1) | (row == lt - 1), 0.0, pltpu.roll(x, lt - 1, 0))
    y = b_ref[...] + w_ref[0:1, :] * prev + w_ref[1:2, :] * x + w_ref[2:3, :] * nxt
    o_ref[0] = (y * jax.nn.sigmoid(y) * s_ref[...]).astype(o_ref.dtype)


def _conv_silu(x, w, b, colscale, n_ctx):
    bsz, lt, ch = x.shape
    tc = 256
    return pl.pallas_call(
        functools.partial(_conv_kernel, n_ctx=n_ctx),
        grid=(bsz, ch // tc),
        in_specs=[
            pl.BlockSpec((1, lt, tc), lambda b, j: (b, 0, j)),
            pl.BlockSpec((3, tc), lambda b, j: (0, j)),
            pl.BlockSpec((1, tc), lambda b, j: (0, j)),
            pl.BlockSpec((1, tc), lambda b, j: (0, j)),
        ],
        out_specs=pl.BlockSpec((1, lt, tc), lambda b, j: (b, 0, j)),
        out_shape=jax.ShapeDtypeStruct((bsz, lt, ch), BF16),
        compiler_params=_cparams(("parallel", "parallel")),
        name="conv_silu",
    )(x, w, b.reshape(1, ch), colscale.reshape(1, ch))


def _s5_matrices(a_re, a_im, log_dt, b_re, b_im, c_re, c_im, backward, lane_groups=8):
    g, p = a_re.shape
    cg = b_re.shape[-1]
    j = S5_J
    lg = lane_groups
    nq = g // lg
    dt = jnp.exp(log_dt)[:, None]
    lam_re = jnp.minimum(a_re, -1e-4)
    lam_im = a_im
    decay = jnp.exp(lam_re * dt)
    ab_re = decay * jnp.cos(lam_im * dt)
    ab_im = decay * jnp.sin(lam_im * dt)
    den = lam_re * lam_re + lam_im * lam_im
    zr = ((ab_re - 1) * lam_re + ab_im * lam_im) / den
    zi = (ab_im * lam_re - (ab_re - 1) * lam_im) / den
    bb_re = zr[..., None] * b_re - zi[..., None] * b_im
    bb_im = zr[..., None] * b_im + zi[..., None] * b_re
    pw_re, pw_im = [jnp.ones_like(ab_re)], [jnp.zeros_like(ab_im)]
    for _ in range(j):
        r0, i0 = pw_re[-1], pw_im[-1]
        pw_re.append(ab_re * r0 - ab_im * i0)
        pw_im.append(ab_re * i0 + ab_im * r0)
    pw_re, pw_im = jnp.stack(pw_re), jnp.stack(pw_im)
    ca_re = c_re[None] * pw_re[:, :, None, :] - c_im[None] * pw_im[:, :, None, :]
    ca_im = c_re[None] * pw_im[:, :, None, :] + c_im[None] * pw_re[:, :, None, :]
    kk = (jnp.einsum('tgcp,gpd->tgcd', ca_re[:j], bb_re, precision=HI)
          - jnp.einsum('tgcp,gpd->tgcd', ca_im[:j], bb_im, precision=HI))
    ab_pw_re = pw_re[:j, :, :, None] * bb_re[None] - pw_im[:j, :, :, None] * bb_im[None]
    ab_pw_im = pw_re[:j, :, :, None] * bb_im[None] + pw_im[:j, :, :, None] * bb_re[None]
    eye = jnp.eye(lg, dtype=BF16)
    lb = lg * cg
    sw = lg * p
    bd_k = jnp.einsum('tqgcd,gh->tqgdhc', kk.astype(BF16).reshape(j, nq, lg, cg, cg), eye
                      ).reshape(j, nq, lb, lb)
    bd_in = [jnp.einsum('tqgpc,gh->tqgchp', a.astype(BF16).reshape(j, nq, lg, p, cg), eye
                        ).reshape(j, nq, lb, sw) for a in (ab_pw_re, ab_pw_im)]
    bd_out = [jnp.einsum('tqgcp,gh->tqgphc', a.astype(BF16).reshape(j + 1, nq, lg, cg, p), eye
                         ).reshape(j + 1, nq, sw, lb) for a in (ca_re, -ca_im)]
    jj = jnp.arange(j)
    lag = (jj[:, None] - jj[None, :]) if backward else (jj[None, :] - jj[:, None])
    kt = jnp.where((lag >= 0)[:, :, None, None, None], bd_k[jnp.clip(lag, 0, j - 1)], 0)
    ktoep = kt.transpose(2, 0, 3, 1, 4).reshape(nq, j * lb, j * lb)
    tau_in = jj if backward else (j - 1 - jj)
    win_re, win_im = (a[tau_in].transpose(1, 0, 2, 3).reshape(nq, j * lb, sw) for a in bd_in)
    tau_out = (j - jj) if backward else (jj + 1)
    wout_re, wout_im = (a[tau_out].transpose(1, 2, 0, 3).reshape(nq, sw, j * lb) for a in bd_out)
    dec_re = pw_re[j].reshape(nq, 1, sw)
    dec_im = pw_im[j].reshape(nq, 1, sw)
    return ktoep, win_re, win_im, wout_re, wout_im, dec_re, dec_im


def _s5_kernel(u_ref, kt_ref, wir_ref, wii_ref, wor_ref, woi_ref, dr_ref, di_ref, y_ref,
               xf_ref, yf_ref, sre_ref, sim_ref, *, bt, nk, nk_ctx, rs):
    d = pl.program_id(0)
    j = S5_J
    lanes = u_ref.shape[-1]
    for b in range(bt):
        for jj in range(j):
            xf_ref[b * nk:(b + 1) * nk, jj * lanes:(jj + 1) * lanes] = (
                u_ref.at[b][pl.ds(jj, nk, stride=j), :].astype(BF16))
    xf = xf_ref[...]
    yf_ref[...] = jnp.dot(xf, kt_ref[0, 0], preferred_element_type=F32)
    inc_re = jnp.dot(xf, wir_ref[0, 0], preferred_element_type=F32)
    inc_im = jnp.dot(xf, wii_ref[0, 0], preferred_element_type=F32)
    nl = sre_ref.shape[0]
    for b in range(bt):
        for l in range(nl):
            sre_ref[l, b * rs:b * rs + nk, :] = inc_re[b * nk:(b + 1) * nk, l * lanes:(l + 1) * lanes]
            sim_ref[l, b * rs:b * rs + nk, :] = inc_im[b * nk:(b + 1) * nk, l * lanes:(l + 1) * lanes]
    a_re = [dr_ref[0, 0, :, l * lanes:(l + 1) * lanes] for l in range(nl)]
    a_im = [di_ref[0, 0, :, l * lanes:(l + 1) * lanes] for l in range(nl)]

    def step(kidx, carry):
        rows = pl.ds(kidx, bt, stride=rs)
        out = []
        for l in range(nl):
            s_re, s_im = carry[2 * l], carry[2 * l + 1]
            i_re = sre_ref.at[l][rows, :]
            i_im = sim_ref.at[l][rows, :]
            sre_ref.at[l][rows, :] = s_re
            sim_ref.at[l][rows, :] = s_im
            out.append(a_re[l] * s_re - a_im[l] * s_im + i_re)
            out.append(a_re[l] * s_im + a_im[l] * s_re + i_im)
        return tuple(out)

    zero = tuple(jnp.zeros((bt, lanes), F32) for _ in range(2 * nl))

    @pl.when(d == 0)
    def _():
        lax.fori_loop(0, nk, step, zero)

    @pl.when(d == 1)
    def _():
        carry = lax.fori_loop(0, nk_ctx, lambda i, cr: step(nk_ctx - 1 - i, cr), zero)
        lax.fori_loop(0, nk - nk_ctx, lambda i, cr: step(nk - 1 - i, cr), carry)

    for b in range(bt):
        sp_re = jnp.concatenate([sre_ref[l, b * rs:b * rs + nk, :] for l in range(nl)], axis=-1).astype(BF16)
        sp_im = jnp.concatenate([sim_ref[l, b * rs:b * rs + nk, :] for l in range(nl)], axis=-1).astype(BF16)
        yb = (yf_ref[b * nk:(b + 1) * nk, :]
              + jnp.dot(sp_re, wor_ref[0, 0], preferred_element_type=F32)
              + jnp.dot(sp_im, woi_ref[0, 0], preferred_element_type=F32))
        for jj in range(j):
            y_ref.at[0, b][pl.ds(jj, nk, stride=j), :] = yb[:, jj * lanes:(jj + 1) * lanes]


def _s5_mixer(u, mats, n_ctx):
    bsz, lt, w = u.shape
    ktoep, win_re, win_im, wout_re, wout_im, dec_re, dec_im = mats
    lanes = 128
    bt = 4 if bsz % 4 == 0 else 2
    nq = w // lanes
    j = S5_J
    nk = lt // j
    nk_ctx = n_ctx // j
    rs = nk + 8
    fl = j * lanes
    sw = win_re.shape[-1]
    wmap = lambda d, q, b: (d, q, 0, 0)
    return pl.pallas_call(
        functools.partial(_s5_kernel, bt=bt, nk=nk, nk_ctx=nk_ctx, rs=rs),
        grid=(2, nq, bsz // bt),
        in_specs=[
            pl.BlockSpec((bt, lt, lanes), lambda d, q, b: (b, 0, q)),
            pl.BlockSpec((1, 1, fl, fl), wmap),
            pl.BlockSpec((1, 1, fl, sw), wmap),
            pl.BlockSpec((1, 1, fl, sw), wmap),
            pl.BlockSpec((1, 1, sw, fl), wmap),
            pl.BlockSpec((1, 1, sw, fl), wmap),
            pl.BlockSpec((1, 1, 1, sw), wmap),
            pl.BlockSpec((1, 1, 1, sw), wmap),
        ],
        out_specs=pl.BlockSpec((1, bt, lt, lanes), lambda d, q, b: (d, b, 0, q)),
        out_shape=jax.ShapeDtypeStruct((2, bsz, lt, w), F32),
        scratch_shapes=[pltpu.VMEM((bt * nk, fl), BF16), pltpu.VMEM((bt * nk, fl), F32),
                        pltpu.VMEM((sw // lanes, bt * rs, lanes), F32),
                        pltpu.VMEM((sw // lanes, bt * rs, lanes), F32)],
        compiler_params=_cparams(("parallel", "parallel", "arbitrary")),
        name="s5_mixer",
    )(u, ktoep, win_re, win_im, wout_re, wout_im, dec_re, dec_im)


def _head_norm(x, heads):
    dh = x.shape[-1] // heads
    outs = []
    for h in range(heads):
        xh = x[:, h * dh:(h + 1) * dh]
        outs.append(xh * lax.rsqrt(jnp.mean(xh * xh, axis=-1, keepdims=True) + EPS))
    return jnp.concatenate(outs, axis=-1)


def _even_post_kernel(m_ref, o_ref, s_ref, u_ref, mg_ref, dsk_ref, gw_ref, gb_ref, w_ref, h_ref, gate_ref,
                      out_ref, *, heads):
    m = m_ref[0, 0] + m_ref[1, 0]
    m_out = _head_norm(m, heads) * mg_ref[...] * jax.nn.sigmoid(o_ref[0])
    y = jax.nn.gelu(s_ref[0, 0] + s_ref[1, 0] + dsk_ref[...] * u_ref[0])
    glu = jnp.dot(y.astype(BF16), gw_ref[...], preferred_element_type=F32) + gb_ref[...]
    s_out = y * jax.nn.sigmoid(glu)
    cat = jnp.concatenate([m_out, s_out], axis=-1).astype(BF16)
    z = jnp.dot(cat, w_ref[...], preferred_element_type=F32)
    out_ref[0] = h_ref[0] + gate_ref[0, 0] * z


def _even_post(m2, o, s2, u, mnorm_g, d_skip, glu_w, glu_b, w_out, h, gate):
    bsz, lt, d = h.shape
    mw = o.shape[-1]
    sw = u.shape[-1]
    tm = ROW_TILE
    row = lambda b, i: (b, i, 0)
    row2 = lambda b, i: (0, b, i, 0)
    const = lambda b, i: (0, 0)
    return pl.pallas_call(
        functools.partial(_even_post_kernel, heads=MLSTM_HEADS),
        grid=(bsz, lt // tm),
        in_specs=[
            pl.BlockSpec((2, 1, tm, mw), row2),
            pl.BlockSpec((1, tm, mw), row),
            pl.BlockSpec((2, 1, tm, sw), row2),
            pl.BlockSpec((1, tm, sw), row),
            pl.BlockSpec((1, mw), const),
            pl.BlockSpec((1, sw), const),
            pl.BlockSpec((sw, sw), const),
            pl.BlockSpec((1, sw), const),
            pl.BlockSpec((mw + sw, d), const),
            pl.BlockSpec((1, tm, d), row),
            pl.BlockSpec((1, 1, 1, d), _seg_map),
        ],
        out_specs=pl.BlockSpec((1, tm, d), row),
        out_shape=jax.ShapeDtypeStruct((bsz, lt, d), F32),
        compiler_params=_cparams(("parallel", "parallel")),
        name="even_post",
    )(m2, o, s2, u, mnorm_g.reshape(1, mw), d_skip.reshape(1, sw), glu_w.astype(BF16), glu_b.reshape(1, sw),
      w_out.astype(BF16), h, gate)


def _odd_post_kernel(o_ref, g_ref, ng_ref, w_ref, h_ref, gate_ref, out_ref, *, heads):
    g = g_ref[0]
    y = _head_norm(o_ref[0, 0] + o_ref[1, 0], heads) * ng_ref[...] * (g * jax.nn.sigmoid(g))
    z = jnp.dot(y.astype(BF16), w_ref[...], preferred_element_type=F32)
    out_ref[0] = h_ref[0] + gate_ref[0, 0] * z


def _odd_post(o, g, norm_g, w_out, h, gate):
    bsz, lt, d = h.shape
    dv = o.shape[-1]
    tm = ROW_TILE
    row = lambda b, i: (b, i, 0)
    const = lambda b, i: (0, 0)
    return pl.pallas_call(
        functools.partial(_odd_post_kernel, heads=GLA_HEADS),
        grid=(bsz, lt // tm),
        in_specs=[
            pl.BlockSpec((2, 1, tm, dv), lambda b, i: (0, b, i, 0)),
            pl.BlockSpec((1, tm, dv), row),
            pl.BlockSpec((1, dv), const),
            pl.BlockSpec((dv, d), const),
            pl.BlockSpec((1, tm, d), row),
            pl.BlockSpec((1, 1, 1, d), _seg_map),
        ],
        out_specs=pl.BlockSpec((1, tm, d), row),
        out_shape=jax.ShapeDtypeStruct((bsz, lt, d), F32),
        compiler_params=_cparams(("parallel", "parallel")),
        name="odd_post",
    )(o, g, norm_g.reshape(1, dv), w_out.astype(BF16), h, gate)


def kernel(x, c, ctx, c_ctx, mod_w, mod_b, norm_mix_g, norm_ffn_g, ev_w_in, ev_b_in, ev_conv_w, ev_conv_b, ev_mlstm_norm_g, ev_s5_a_re_f, ev_s5_a_im_f, ev_s5_log_dt_f, ev_s5_a_re_b, ev_s5_a_im_b, ev_s5_log_dt_b, ev_s5_b_re, ev_s5_b_im, ev_s5_c_re, ev_s5_c_im, ev_s5_d, ev_s5_glu_w, ev_s5_glu_b, ev_w_out, od_w_in, od_gate_w2_f, od_gate_b2_f, od_gate_w2_b, od_gate_b2_b, od_norm_g, od_w_out, router_w, router_b, moe_w_gu, moe_b_gu, moe_w_down, moe_b_down, final_norm_g):
    bsz, seq, d = x.shape
    n_ctx = ctx.shape[1]
    depth = mod_w.shape[0]
    lt = n_ctx + seq
    assert n_ctx == ROW_TILE and seq % ROW_TILE == 0 and seq % GRID_W == 0

    h = jnp.concatenate([ctx, x], axis=1)
    c_all = jnp.concatenate([c, c_ctx[None, :]], axis=0)
    c_all = jnp.pad(c_all, ((0, (-c_all.shape[0]) % 8), (0, 0)))
    mods = _modulation(c_all, mod_w, mod_b)
    mod_lat = mods[:, :bsz]
    mod_ctx = jnp.broadcast_to(mods[:, bsz:bsz + 1], mod_lat.shape)
    mod6 = jnp.stack([mod_ctx, mod_lat], axis=2).reshape(depth, bsz, 2, 6, 1, d)

    bg_all = moe_b_gu[..., 0::2]
    bu_all = moe_b_gu[..., 1::2]

    mw = ev_conv_w.shape[-1] // 2
    n_gates = 4 * MLSTM_HEADS
    s5w = ev_s5_d.shape[-1]
    dk_t = od_gate_w2_f.shape[-1]
    dv_t = od_norm_g.shape[-1]
    y_buf = None
    for layer in range(depth):
        last = layer == depth - 1
        j = layer // 2
        m6 = mod6[layer]
        sh1, sc1, g1, sh2, sc2, g2 = (m6[:, :, i] for i in range(6))
        if layer % 2 == 0:
            w_in, b_in = ev_w_in[j], ev_b_in[j]
            cols = jnp.concatenate([jnp.arange(0, 4 * mw), jnp.arange(4 * mw + n_gates, 4 * mw + n_gates + s5w),
                                    jnp.arange(4 * mw, 4 * mw + n_gates)])
            qk_pre, v, o, u, gates = _nm_matmul(h, norm_mix_g[layer], sh1, sc1, w_in[:, cols], b_in[cols],
                                                (2 * mw, mw, mw, s5w, n_gates))
            dh = mw // MLSTM_HEADS
            colscale = jnp.concatenate([jnp.full((mw,), dh ** -0.5, F32), jnp.ones((mw,), F32)])
            qk = _conv_silu(qk_pre, ev_conv_w[j], ev_conv_b[j], colscale, n_ctx)
            m2 = _mlstm_mixer(qk, v, gates, n_ctx)
            shared = (ev_s5_b_re[j], ev_s5_b_im[j], ev_s5_c_re[j], ev_s5_c_im[j])
            mats_f = _s5_matrices(ev_s5_a_re_f[j], ev_s5_a_im_f[j], ev_s5_log_dt_f[j], *shared, backward=False)
            mats_b = _s5_matrices(ev_s5_a_re_b[j], ev_s5_a_im_b[j], ev_s5_log_dt_b[j], *shared, backward=True)
            s2 = _s5_mixer(u, tuple(jnp.stack([a, b]) for a, b in zip(mats_f, mats_b)), n_ctx)
            h = _even_post(m2, o, s2, u, ev_mlstm_norm_g[j], ev_s5_d[j], ev_s5_glu_w[j], ev_s5_glu_b[j],
                           ev_w_out[j], h, g1)
        else:
            hc = _grid_reorder(h, n_ctx, True)
            qq, kk, vv, gg, rr = _nm_matmul(hc, norm_mix_g[layer], sh1, sc1, od_w_in[j],
                                            jnp.zeros((od_w_in.shape[-1],), F32),
                                            (dk_t, dk_t, dv_t, dv_t, 2 * GLA_RANK))
            zero = jnp.zeros_like(od_gate_w2_f[j])
            w2 = jnp.stack([jnp.concatenate([od_gate_w2_f[j], zero], axis=0),
                            jnp.concatenate([zero, od_gate_w2_b[j]], axis=0)])
            b2 = jnp.stack([od_gate_b2_f[j], od_gate_b2_b[j]])[:, None, :]
            o2 = _gla_mixer(qq, kk, vv, rr, w2, b2, n_ctx)
            h = _grid_reorder(_odd_post(o2, gg, od_norm_g[j], od_w_out[j], hc, g1), n_ctx, False)
        f, top_e, gate = _ffn_prep(h, norm_ffn_g[layer], sh2, sc2, router_w[layer], router_b[layer])
        weights = (layer, moe_w_gu, moe_w_down, bg_all[layer], bu_all[layer], moe_b_down[layer])
        h, y_buf = _moe_layer(h, f, top_e, gate, g2, weights, last, n_ctx, y_buf)
    return _final_norm(h, final_norm_g)
```

```python
import functools

import jax
import jax.numpy as jnp
from jax import lax
from jax.experimental import pallas as pl
from jax.experimental.pallas import tpu as pltpu

F32 = jnp.float32
BF16 = jnp.bfloat16
HI = lax.Precision.HIGHEST

EPS = 1e-6
GRID_W = 64
MLSTM_HEADS = 4
S5_GROUP = 16
GLA_HEADS = 4
GLA_RANK = 16
GLA_TAU = 16.0
N_EXPERTS = 32
TOP_K = 4
SWIGLU_LIMIT = 7.0
SWIGLU_ALPHA = 1.702

ROW_TILE = 256
MOE_TILE = 512
MOE_PARTS = 4
MIX_CHUNK = 64
MLSTM_CHUNK = 128
MIX_BATCH = 4
S5_J = 8
VMEM_LIMIT = 56 * 1024 * 1024

NT = (((1,), (1,)), ((), ()))
TN = (((0,), (0,)), ((), ()))


def _cparams(sem):
    return pltpu.CompilerParams(dimension_semantics=sem, vmem_limit_bytes=VMEM_LIMIT)


def _mod_kernel(c_ref, w_ref, b_ref, o_ref):
    c = c_ref[...]
    a = c * jax.nn.sigmoid(c)
    o_ref[0] = jnp.dot(a.astype(BF16), w_ref[0].astype(BF16), preferred_element_type=F32) + b_ref[0]


def _modulation(c_all, mod_w, mod_b):
    depth, d, n6 = mod_w.shape
    rows = c_all.shape[0]
    tn = d
    return pl.pallas_call(
        _mod_kernel,
        grid=(depth, n6 // tn),
        in_specs=[
            pl.BlockSpec((rows, d), lambda l, j: (0, 0)),
            pl.BlockSpec((1, d, tn), lambda l, j: (l, 0, j)),
            pl.BlockSpec((1, 1, tn), lambda l, j: (l, 0, j)),
        ],
        out_specs=pl.BlockSpec((1, rows, tn), lambda l, j: (l, 0, j)),
        out_shape=jax.ShapeDtypeStruct((depth, rows, n6), F32),
        compiler_params=_cparams(("arbitrary", "arbitrary")),
        name="modulation",
    )(c_all, mod_w, mod_b.reshape(depth, 1, n6))


def _norm_mod(x, g, sh, sc):
    ms = jnp.mean(x * x, axis=-1, keepdims=True)
    return (x * lax.rsqrt(ms + EPS) * g) * (1.0 + sc) + sh


def _nm_matmul_kernel(x_ref, g_ref, sh_ref, sc_ref, w_ref, b_ref, *out_refs, splits):
    a = _norm_mod(x_ref[0], g_ref[...], sh_ref[0, 0], sc_ref[0, 0])
    z = jnp.dot(a.astype(BF16), w_ref[...], preferred_element_type=F32) + b_ref[...]
    for (lo, hi), o_ref in zip(splits, out_refs):
        o_ref[0] = z[:, lo:hi].astype(o_ref.dtype)


def _wide_tile(rows):
    return next(t for t in (4 * ROW_TILE, 3 * ROW_TILE, 2 * ROW_TILE, ROW_TILE) if rows % t == 0)


def _seg_map(b, i):
    return (b, jnp.minimum(i, 1), 0, 0)


def _nm_matmul(h, g, shift, scale, w, bias, widths):
    bsz, lt, d = h.shape
    p = w.shape[1]
    splits, lo = [], 0
    for wd in widths:
        splits.append((lo, lo + wd))
        lo += wd
    assert lo == p
    tm = ROW_TILE
    return pl.pallas_call(
        functools.partial(_nm_matmul_kernel, splits=tuple(splits)),
        grid=(bsz, lt // tm),
        in_specs=[
            pl.BlockSpec((1, tm, d), lambda b, i: (b, i, 0)),
            pl.BlockSpec((1, d), lambda b, i: (0, 0)),
            pl.BlockSpec((1, 1, 1, d), _seg_map),
            pl.BlockSpec((1, 1, 1, d), _seg_map),
            pl.BlockSpec((d, p), lambda b, i: (0, 0)),
            pl.BlockSpec((1, p), lambda b, i: (0, 0)),
        ],
        out_specs=[pl.BlockSpec((1, tm, wd), lambda b, i: (b, i, 0)) for wd in widths],
        out_shape=[jax.ShapeDtypeStruct((bsz, lt, wd), F32) for wd in widths],
        compiler_params=_cparams(("parallel", "parallel")),
        name="norm_mod_matmul",
    )(h, g.reshape(1, d), shift, scale, w.astype(BF16), bias.reshape(1, p))


def _ffn_prep_kernel(x_ref, g_ref, sh_ref, sc_ref, rw_ref, rb_ref, f_ref, te_ref, gt_ref, *, n_ctx):
    tm = x_ref.shape[1]
    is_ctx = pl.program_id(1) * tm + lax.broadcasted_iota(jnp.int32, (tm, 1), 0) < n_ctx
    a = _norm_mod(x_ref[0], g_ref[...], jnp.where(is_ctx, sh_ref[0, 0], sh_ref[0, 1]),
                  jnp.where(is_ctx, sc_ref[0, 0], sc_ref[0, 1]))
    f_ref[0] = a.astype(f_ref.dtype)
    logits = lax.dot_general(rw_ref[...], a.astype(BF16), NT, preferred_element_type=F32) + rb_ref[...]
    ne = logits.shape[0]
    eidx = lax.broadcasted_iota(jnp.int32, logits.shape, 0)
    work = logits
    vals, idxs = [], []
    for _ in range(TOP_K):
        m = jnp.max(work, axis=0, keepdims=True)
        idx = jnp.min(jnp.where(work == m, eidx, ne), axis=0, keepdims=True)
        vals.append(m)
        idxs.append(idx)
        work = jnp.where(eidx == idx, -jnp.inf, work)
    exps = [jnp.exp(v - vals[0]) for v in vals]
    denom = exps[0]
    for e in exps[1:]:
        denom = denom + e
    for k in range(TOP_K):
        te_ref[0, k:k + 1, :] = idxs[k]
        gt_ref[0, k:k + 1, :] = exps[k] / denom


def _ffn_prep(h, g, shift, scale, router_w, router_b):
    bsz, lt, d = h.shape
    ne = router_w.shape[1]
    tm = _wide_tile(lt)
    both = pl.BlockSpec((1, 2, 1, d), lambda b, i: (b, 0, 0, 0))
    return pl.pallas_call(
        functools.partial(_ffn_prep_kernel, n_ctx=ROW_TILE),
        grid=(bsz, lt // tm),
        in_specs=[
            pl.BlockSpec((1, tm, d), lambda b, i: (b, i, 0)),
            pl.BlockSpec((1, d), lambda b, i: (0, 0)),
            both,
            both,
            pl.BlockSpec((ne, d), lambda b, i: (0, 0)),
            pl.BlockSpec((ne, 1), lambda b, i: (0, 0)),
        ],
        out_specs=[
            pl.BlockSpec((1, tm, d), lambda b, i: (b, i, 0)),
            pl.BlockSpec((1, TOP_K, tm), lambda b, i: (b, 0, i)),
            pl.BlockSpec((1, TOP_K, tm), lambda b, i: (b, 0, i)),
        ],
        out_shape=[
            jax.ShapeDtypeStruct((bsz, lt, d), BF16),
            jax.ShapeDtypeStruct((bsz, TOP_K, lt), jnp.int32),
            jax.ShapeDtypeStruct((bsz, TOP_K, lt), F32),
        ],
        compiler_params=_cparams(("parallel", "parallel")),
        name="ffn_prep",
    )(h, g.reshape(1, d), shift, scale, router_w.T.astype(BF16), router_b.reshape(ne, 1))


GU_BLOCK = 256


def _moe_kernel(be_ref, nb_ref, first_ref, slot_ref, nxt_ref, x_ref, wgu_hbm, wd_hbm, bg_ref, bu_ref, bd_ref,
                *rest, layer, in_place):
    o_ref, wgu_buf, wd_buf, wgu_s, wd_s, sem = rest[1:] if in_place else rest
    i = pl.program_id(0)
    active = i < nb_ref[0]
    half = GU_BLOCK // 2
    nblk = wgu_s.shape[1] // GU_BLOCK

    def weight_copies(e, slot):
        return (pltpu.make_async_copy(wgu_hbm.at[layer, e], wgu_buf.at[slot], sem.at[0, slot]),
                pltpu.make_async_copy(wd_hbm.at[layer, e], wd_buf.at[slot], sem.at[1, slot]))

    @pl.when(active & (i == 0))
    def _():
        for cp in weight_copies(be_ref[0], 0):
            cp.start()

    @pl.when(active & (first_ref[i] == 1))
    def _():
        slot = slot_ref[i]
        for cp in weight_copies(be_ref[i], slot):
            cp.wait()

        @pl.when(nxt_ref[i] >= 0)
        def _():
            for cp in weight_copies(nxt_ref[i], 1 - slot):
                cp.start()

        r = lax.broadcasted_iota(jnp.int32, (GU_BLOCK, GU_BLOCK), 0)
        c = lax.broadcasted_iota(jnp.int32, (GU_BLOCK, GU_BLOCK), 1)
        perm = (r == jnp.where(c < half, 2 * c, 2 * (c - half) + 1)).astype(BF16)
        for k in range(nblk):
            cs = slice(k * GU_BLOCK, (k + 1) * GU_BLOCK)
            wgu_s[:, cs] = jnp.dot(wgu_buf[slot, :, cs].astype(BF16), perm,
                                   preferred_element_type=F32).astype(BF16)
        wd_s[...] = wd_buf[slot].astype(BF16)

    @pl.when(active)
    def _():
        gu = jnp.dot(x_ref[...], wgu_s[...], preferred_element_type=F32)
        hdn = []
        for k in range(nblk):
            hs = slice(k * half, (k + 1) * half)
            g = gu[:, k * GU_BLOCK:k * GU_BLOCK + half] + bg_ref[0, :, hs]
            u = gu[:, k * GU_BLOCK + half:(k + 1) * GU_BLOCK] + bu_ref[0, :, hs]
            g = jnp.minimum(g, SWIGLU_LIMIT)
            u = jnp.clip(u, -SWIGLU_LIMIT, SWIGLU_LIMIT)
            hdn.append(((u + 1.0) * (g * jax.nn.sigmoid(SWIGLU_ALPHA * g))).astype(BF16))
        hdn = jnp.concatenate(hdn, axis=-1)
        o_ref[...] = (jnp.dot(hdn, wd_s[...], preferred_element_type=F32) + bd_ref[0]).astype(o_ref.dtype)

    @pl.when(jnp.logical_not(active))
    def _():
        o_ref[...] = jnp.zeros_like(o_ref)


def _moe_experts(x_sorted, block_expert, n_used, layer, w_gu, w_down, bg, bu, bd, block_off, y_prev, out_rows):
    n_rows, d = x_sorted.shape
    _, ne, _, f2 = w_gu.shape
    f = f2 // 2
    tm = MOE_TILE
    x_blocks = n_rows // tm
    in_place = y_prev is not None
    n_blocks = x_blocks if in_place else out_rows // tm - block_off
    assert f2 % GU_BLOCK == 0 and block_expert.shape[0] == n_blocks
    blk = jnp.arange(n_blocks, dtype=jnp.int32)
    prev = jnp.concatenate([block_expert[:1], block_expert[:-1]])
    first = (blk < n_used[0]) & ((blk == 0) | (block_expert != prev))
    slot = (jnp.cumsum(first.astype(jnp.int32)) - 1) & 1
    first_idx = jnp.where(first, blk, n_blocks)
    next_first = lax.cummin(first_idx, axis=0, reverse=True)
    next_first = jnp.concatenate([next_first[1:], jnp.full((1,), n_blocks, jnp.int32)])
    nxt = jnp.where(next_first < n_blocks, block_expert[jnp.minimum(next_first, n_blocks - 1)], -1)
    bmap = lambda i, be, nb, fi, sl, nx: (be[i], 0, 0)
    rmap = lambda i, be, nb, fi, sl, nx: (jnp.minimum(i, x_blocks - 1), 0)
    grid_spec = pltpu.PrefetchScalarGridSpec(
        num_scalar_prefetch=5,
        grid=(n_blocks,),
        in_specs=[
            pl.BlockSpec((tm, d), rmap),
            pl.BlockSpec(memory_space=pl.ANY),
            pl.BlockSpec(memory_space=pl.ANY),
            pl.BlockSpec((1, 1, f), bmap),
            pl.BlockSpec((1, 1, f), bmap),
            pl.BlockSpec((1, 1, d), bmap),
        ] + ([pl.BlockSpec(memory_space=pl.ANY)] if in_place else []),
        out_specs=pl.BlockSpec((tm, d), lambda i, be, nb, fi, sl, nx: (i + block_off, 0)),
        scratch_shapes=[pltpu.VMEM((2, d, f2), F32), pltpu.VMEM((2, f, d), F32),
                        pltpu.VMEM((d, f2), BF16), pltpu.VMEM((f, d), BF16),
                        pltpu.SemaphoreType.DMA((2, 2))],
    )
    operands = (block_expert, n_used, first.astype(jnp.int32), slot.astype(jnp.int32), nxt.astype(jnp.int32),
                x_sorted, w_gu, w_down, bg.reshape(ne, 1, f), bu.reshape(ne, 1, f), bd.reshape(ne, 1, d))
    if in_place:
        assert y_prev.shape == (out_rows, d)
        operands = operands + (y_prev,)
    return pl.pallas_call(
        functools.partial(_moe_kernel, layer=layer, in_place=in_place),
        grid_spec=grid_spec,
        out_shape=jax.ShapeDtypeStruct((out_rows, d), BF16),
        input_output_aliases={len(operands) - 1: 0} if in_place else {},
        compiler_params=_cparams(("arbitrary",)),
        name="moe_experts",
    )(*operands)


def _combine_kernel(y_ref, gt_ref, h_ref, g2_ref, fg_ref, o_ref, *, n_ctx, final):
    gt = gt_ref[0]
    acc = y_ref[0, 0].astype(F32) * gt[:, 0:1]
    for k in range(1, TOP_K):
        acc = acc + y_ref[k, 0].astype(F32) * gt[:, k:k + 1]
    tm = acc.shape[0]
    is_ctx = pl.program_id(1) * tm + lax.broadcasted_iota(jnp.int32, (tm, 1), 0) < n_ctx
    out = h_ref[0] + jnp.where(is_ctx, g2_ref[0, 0], g2_ref[0, 1]) * acc
    if final:
        out = out * lax.rsqrt(jnp.mean(out * out, axis=-1, keepdims=True) + EPS) * fg_ref[...]
    o_ref[0] = out


def _moe_combine(yg, gate, h, g2, lat_only, final_g):
    k, bsz, lt, d = yg.shape
    tm = ROW_TILE if lat_only else _wide_tile(lt)
    tile_off = 1 if lat_only else 0
    final = final_g is not None
    return pl.pallas_call(
        functools.partial(_combine_kernel, n_ctx=0 if lat_only else ROW_TILE, final=final),
        grid=(bsz, lt // tm),
        in_specs=[
            pl.BlockSpec((k, 1, tm, d), lambda b, i: (0, b, i, 0)),
            pl.BlockSpec((1, tm, k), lambda b, i: (b, i, 0)),
            pl.BlockSpec((1, tm, d), lambda b, i: (b, i + tile_off, 0)),
            pl.BlockSpec((1, 2, 1, d), lambda b, i: (b, 0, 0, 0)),
            pl.BlockSpec((1, d), lambda b, i: (0, 0)),
        ],
        out_specs=pl.BlockSpec((1, tm, d), lambda b, i: (b, i, 0)),
        out_shape=jax.ShapeDtypeStruct((bsz, lt, d), F32),
        compiler_params=_cparams(("parallel", "parallel")),
        name="moe_combine",
    )(yg, gate, h, g2, (final_g if final else jnp.ones((d,), F32)).reshape(1, d))


def _route_kernel(e_ref, pos_ref, cnt_ref, *, tm):
    nk, rows, lanes = e_ref.shape
    li = lax.broadcasted_iota(jnp.int32, (lanes, lanes), 0)
    lj = lax.broadcasted_iota(jnp.int32, (lanes, lanes), 1)
    before_lane = (li < lj).astype(BF16)
    ones = jnp.ones((lanes, lanes), BF16)
    ri = lax.broadcasted_iota(jnp.int32, (rows, rows), 0)
    rj = lax.broadcasted_iota(jnp.int32, (rows, rows), 1)
    before_row = (rj < ri).astype(BF16)
    lane = lax.broadcasted_iota(jnp.int32, (1, lanes), 1)
    xs = [e_ref[k] for k in range(nk)]
    pos = [jnp.zeros((rows, lanes), F32) for _ in range(nk)]
    counts = jnp.zeros((1, lanes), F32)
    pad_off = jnp.zeros((1, lanes), F32)
    for e in range(N_EXPERTS):
        ms = [x == e for x in xs]
        hit = ms[0]
        for m in ms[1:]:
            hit = hit | m
        mb = hit.astype(BF16)
        in_row = jnp.dot(mb, before_lane, preferred_element_type=F32)
        row_sum = jnp.dot(mb, ones, preferred_element_type=F32)
        row_off = jnp.dot(before_row, row_sum.astype(BF16), preferred_element_type=F32)
        count = row_off[rows - 1:rows, :] + row_sum[rows - 1:rows, :]
        dest = in_row + row_off + pad_off
        pos = [p + jnp.where(m, dest, 0.0) for p, m in zip(pos, ms)]
        counts = jnp.where(lane == e, count, counts)
        pad_off = pad_off + jnp.floor((count + (tm - 1)) * (1.0 / tm)) * tm
    for k in range(nk):
        pos_ref[k] = pos[k].astype(jnp.int32)
    cnt_ref[...] = counts.astype(jnp.int32)


def _route_positions(top_e, tm):
    lanes = 128
    nk, n = top_e.shape
    rows = n // lanes
    pos, counts = pl.pallas_call(
        functools.partial(_route_kernel, tm=tm),
        out_shape=[jax.ShapeDtypeStruct((nk, rows, lanes), jnp.int32), jax.ShapeDtypeStruct((1, lanes), jnp.int32)],
        compiler_params=pltpu.CompilerParams(vmem_limit_bytes=VMEM_LIMIT),
        name="route_positions",
    )(top_e.reshape(nk, rows, lanes))
    return pos.reshape(nk, n), counts[0, :N_EXPERTS]


def _moe_rows(n_tokens):
    n_assign = n_tokens * TOP_K
    return -(-(n_assign + N_EXPERTS * (MOE_TILE - 1)) // MOE_TILE) * MOE_TILE


def _moe_layer(h, f, top_e, gate, g2, weights, lat_only, n_ctx, y_buf, final_g):
    layer, w_gu, w_down, bg, bu, bd = weights
    bsz, lt, d = h.shape
    skip = n_ctx if lat_only else 0
    ltok = lt - skip
    top_e, gate = top_e[:, :, skip:], gate[:, :, skip:]
    n = bsz * ltok
    n_assign = n * TOP_K
    tm = MOE_TILE
    te = top_e.transpose(1, 0, 2).reshape(TOP_K, n).astype(jnp.int32)
    bits = max(n - 1, 1).bit_length()
    assert N_EXPERTS << bits < 2 ** 31
    keys = (te << bits) | jnp.arange(n, dtype=jnp.int32)[None, :]
    slot_token = lax.sort(keys.reshape(-1)) & ((1 << bits) - 1)
    pos, counts = _route_positions(te, tm)
    start = jnp.cumsum(counts) - counts
    padded = (counts + tm - 1) // tm * tm
    pad_end = jnp.cumsum(padded)
    pad_start = pad_end - padded
    n_rows = _moe_rows(n)
    n_blocks = n_rows // tm
    buf_rows = n_rows if y_buf is None else y_buf.shape[0]
    assert buf_rows >= n_rows
    block_expert = jnp.minimum(
        jnp.searchsorted(pad_end, jnp.arange(n_blocks, dtype=jnp.int32) * tm, side='right', method='compare_all'),
        N_EXPERTS - 1).astype(jnp.int32)
    n_used = (pad_end[-1] // tm).astype(jnp.int32).reshape(1)
    row = jnp.arange(n_rows, dtype=jnp.int32).reshape(n_blocks, tm)
    blk_shift = (start - pad_start)[block_expert][:, None]
    blk_end = (pad_start + counts)[block_expert][:, None]
    slot = jnp.clip(row + blk_shift, 0, n_assign - 1).reshape(-1)
    row_token = jnp.where((row < blk_end).reshape(-1),
                          slot_token.at[slot].get(mode='promise_in_bounds'), row.reshape(-1) % n)
    row_src = row_token + skip * (row_token // ltok + 1)
    f2d = f.reshape(bsz * lt, d)
    cuts = [n_blocks * p // MOE_PARTS for p in range(MOE_PARTS + 1)]
    y = y_buf
    for lo, hi in zip(cuts[:-1], cuts[1:]):
        x_part = f2d.at[row_src[lo * tm:hi * tm]].get(mode='promise_in_bounds')
        blocks = block_expert[lo:hi] if y is not None else block_expert[lo:]
        y = _moe_experts(x_part, blocks, jnp.clip(n_used - lo, 0, hi - lo), layer,
                         w_gu, w_down, bg, bu, bd, lo, y, buf_rows)
    yg = y.at[pos.reshape(-1)].get(mode='promise_in_bounds').reshape(TOP_K, bsz, ltok, d)
    return _moe_combine(yg, gate.transpose(0, 2, 1), h, g2, lat_only, final_g), y


def _grid_reorder_kernel(x_ref, o_ref, *, n_ctx, rows, to_cols):
    o_ref[0, :n_ctx, :] = x_ref[0, :n_ctx, :]
    for c in range(GRID_W):
        raster = pl.ds(n_ctx + c, rows, stride=GRID_W)
        dense = pl.ds(n_ctx + c * rows, rows)
        if to_cols:
            o_ref.at[0][dense, :] = x_ref.at[0][raster, :]
        else:
            o_ref.at[0][raster, :] = x_ref.at[0][dense, :]


def _grid_reorder(h, n_ctx, to_cols):
    bsz, lt, d = h.shape
    lanes = 128
    spec = pl.BlockSpec((1, lt, lanes), lambda b, j: (b, 0, j))
    return pl.pallas_call(
        functools.partial(_grid_reorder_kernel, n_ctx=n_ctx, rows=(lt - n_ctx) // GRID_W, to_cols=to_cols),
        grid=(bsz, d // lanes),
        in_specs=[spec],
        out_specs=spec,
        out_shape=jax.ShapeDtypeStruct(h.shape, h.dtype),
        compiler_params=_cparams(("parallel", "parallel")),
        name="grid_reorder",
    )(h)


def _chunk_order(d, c, n_ctx_chunks, n_chunks):
    bwd = jnp.where(c < n_ctx_chunks, n_ctx_chunks - 1 - c, n_chunks + n_ctx_chunks - 1 - c)
    return jnp.where(d == 0, c, bwd)


def _split_bf16(x, n):
    out = []
    for _ in range(n):
        p = x.astype(BF16)
        out.append(p)
        x = x - p.astype(F32)
    return out


def _dir_tri(d, t):
    row = lax.broadcasted_iota(jnp.int32, (t, t), 0)
    col = lax.broadcasted_iota(jnp.int32, (t, t), 1)
    return jnp.where(d == 0, col - row, row - col) <= 0


def _gla_kernel(q_ref, k_ref, v_ref, r_ref, w2_ref, b2_ref, o_ref, st_ref, *, t, heads, scale):
    d = pl.program_id(0)
    c = pl.program_id(2)

    @pl.when(c == 0)
    def _():
        st_ref[...] = jnp.zeros_like(st_ref)

    dk = q_ref.shape[-1] // heads
    dv = v_ref.shape[-1] // heads
    nb = q_ref.shape[0]
    mask = _dir_tri(d, t)
    tri = mask.astype(F32)
    mid = t // 2
    items = [(bb, h) for bb in range(nb) for h in range(heads)]
    w_hi, w_lo = _split_bf16(w2_ref[0], 2)
    xs = []
    for bb in range(nb):
        r_hi, r_lo = _split_bf16(r_ref[bb], 2)
        xs.append(jnp.dot(r_hi, w_hi, preferred_element_type=F32) + jnp.dot(r_hi, w_lo, preferred_element_type=F32)
                  + jnp.dot(r_lo, w_hi, preferred_element_type=F32) + b2_ref[0])
    las = [jax.nn.log_sigmoid(x) * (1.0 / GLA_TAU) for x in xs]
    tri_b = tri.astype(BF16)
    bs = [sum(jnp.dot(tri_b, p, preferred_element_type=F32) for p in _split_bf16(la, 3)) for la in las]
    qt, kt, qe, kh_end, e_end = [], [], [], [], []
    for bb in range(nb):
        b = bs[bb]
        b_m = b[mid:mid + 1, :]
        b_end = jnp.where(d == 0, b[t - 1:t, :], b[0:1, :])
        q_s = q_ref[bb] * (jnp.exp(b - b_m) * scale)
        k_s = k_ref[bb] * jnp.exp(b_m - b)
        qe.append((q_s * jnp.exp(b_m)).astype(BF16))
        kh_end.append((k_s * jnp.exp(b_end - b_m)).astype(BF16))
        e_end.append(jnp.exp(b_end))
        qt.append(q_s.astype(BF16))
        kt.append(k_s.astype(BF16))
    att, q_st, vs = {}, {}, {}
    for bb, h in items:
        ks = slice(h * dk, (h + 1) * dk)
        vs[bb, h] = v_ref[bb, :, h * dv:(h + 1) * dv].astype(BF16)
        att[bb, h] = lax.dot_general(qt[bb][:, ks], kt[bb][:, ks], NT, preferred_element_type=F32)
        q_st[bb, h] = lax.dot_general(qe[bb][:, ks], st_ref[bb * heads + h].astype(BF16), NT,
                                      preferred_element_type=F32)
    for bb, h in items:
        a = jnp.where(mask, att[bb, h], 0.0).astype(BF16)
        o_ref[0, bb, :, h * dv:(h + 1) * dv] = jnp.dot(a, vs[bb, h], preferred_element_type=F32) + q_st[bb, h]
    for bb, h in items:
        ks = slice(h * dk, (h + 1) * dk)
        upd = lax.dot_general(vs[bb, h], kh_end[bb][:, ks], TN, preferred_element_type=F32)
        st_ref[bb * heads + h] = st_ref[bb * heads + h] * e_end[bb][:, ks] + upd


def _gla_mixer(q, k, v, r, w2, b2, n_ctx):
    bsz, lt, dkt = q.shape
    dvt = v.shape[-1]
    nr = r.shape[-1]
    t, heads = MIX_CHUNK, GLA_HEADS
    bt = MIX_BATCH
    nch = lt // t
    ncc = n_ctx // t
    dk = dkt // heads
    dv = dvt // heads
    imap = lambda d, b, c: (b, _chunk_order(d, c, ncc, nch), 0)
    return pl.pallas_call(
        functools.partial(_gla_kernel, t=t, heads=heads, scale=dk ** -0.5),
        grid=(2, bsz // bt, nch),
        in_specs=[
            pl.BlockSpec((bt, t, dkt), imap),
            pl.BlockSpec((bt, t, dkt), imap),
            pl.BlockSpec((bt, t, dvt), imap),
            pl.BlockSpec((bt, t, nr), imap),
            pl.BlockSpec((1, nr, dkt), lambda d, b, c: (d, 0, 0)),
            pl.BlockSpec((1, 1, dkt), lambda d, b, c: (d, 0, 0)),
        ],
        out_specs=pl.BlockSpec((1, bt, t, dvt), lambda d, b, c: (d, b, _chunk_order(d, c, ncc, nch), 0)),
        out_shape=jax.ShapeDtypeStruct((2, bsz, lt, dvt), F32),
        scratch_shapes=[pltpu.VMEM((bt * heads, dv, dk), F32)],
        compiler_params=_cparams(("parallel", "parallel", "arbitrary")),
        name="gla_mixer",
    )(q, k, v, r, w2, b2)


def _mlstm_kernel(q_ref, k_ref, v_ref, gc_ref, gr_ref, o_ref, c_ref, n_ref, m_ref, *, t, heads):
    d = pl.program_id(0)
    c = pl.program_id(2)

    @pl.when(c == 0)
    def _():
        c_ref[...] = jnp.zeros_like(c_ref)
        n_ref[...] = jnp.zeros_like(n_ref)
        m_ref[...] = jnp.zeros_like(m_ref)

    dh = q_ref.shape[-1] // heads
    nb = q_ref.shape[0]
    mask = _dir_tri(d, t)
    tri = mask.astype(F32)
    items = [(bb, h) for bb in range(nb) for h in range(heads)]
    gate = []
    for bb in range(nb):
        gc = gc_ref[0, bb]
        gr = gr_ref[0, bb, 0]
        fc = jax.nn.log_sigmoid(gc[:, heads:])
        fr = jax.nn.log_sigmoid(gr[heads:, :])
        b_col = jnp.dot(tri, fc, preferred_element_type=F32, precision=HI)
        b_row = lax.dot_general(fr, tri, NT, preferred_element_type=F32, precision=HI)
        b_last = jnp.where(d == 0, b_col[t - 1:t, :], b_col[0:1, :])
        gate.append((gc[:, :heads], gr[:heads, :], b_col, b_row, b_last))
    qs, ks, vs, s_raw, q_c = {}, {}, {}, {}, {}
    for bb, h in items:
        hs = slice(h * dh, (h + 1) * dh)
        qs[bb, h] = q_ref[bb, :, hs]
        ks[bb, h] = k_ref[bb, :, hs]
        vs[bb, h] = v_ref[bb, :, hs].astype(BF16)
        s_raw[bb, h] = lax.dot_general(qs[bb, h], ks[bb, h], NT, preferred_element_type=F32)
        q_c[bb, h] = jnp.dot(qs[bb, h], c_ref[bb * heads + h].astype(BF16), preferred_element_type=F32)
    logw, log_inter, m_t, w_inter, scores, den, qn = {}, {}, {}, {}, {}, {}, {}
    for bb, h in items:
        _, ir, b_col, b_row, _ = gate[bb]
        bc = b_col[:, h:h + 1]
        logw[bb, h] = jnp.where(mask, bc - b_row[h:h + 1, :] + ir[h:h + 1, :], -jnp.inf)
        log_inter[bb, h] = bc + m_ref[bb * heads + h]
    for bb, h in items:
        m_t[bb, h] = jnp.maximum(log_inter[bb, h], jnp.max(logw[bb, h], axis=-1, keepdims=True))
        qn[bb, h] = jnp.sum(qs[bb, h].astype(F32) * n_ref[bb * heads + h], axis=-1, keepdims=True)
    for bb, h in items:
        w_inter[bb, h] = jnp.exp(log_inter[bb, h] - m_t[bb, h])
        scores[bb, h] = s_raw[bb, h] * jnp.exp(logw[bb, h] - m_t[bb, h])
    for bb, h in items:
        den[bb, h] = jnp.sum(scores[bb, h], axis=-1, keepdims=True) + w_inter[bb, h] * qn[bb, h]
    num = {}
    for bb, h in items:
        num[bb, h] = (jnp.dot(scores[bb, h].astype(BF16), vs[bb, h], preferred_element_type=F32)
                      + w_inter[bb, h] * q_c[bb, h])
    for bb, h in items:
        hs = slice(h * dh, (h + 1) * dh)
        o_ref[0, bb, :, hs] = num[bb, h] / jnp.maximum(jnp.abs(den[bb, h]), jnp.exp(-m_t[bb, h]))
    log_g, m_new, kw, upd, ksum = {}, {}, {}, {}, {}
    for bb, h in items:
        ic, _, b_col, _, b_last = gate[bb]
        log_g[bb, h] = b_last[:, h:h + 1] - b_col[:, h:h + 1] + ic[:, h:h + 1]
    for bb, h in items:
        b_last = gate[bb][4]
        m_new[bb, h] = jnp.maximum(b_last[:, h:h + 1] + m_ref[bb * heads + h],
                                   jnp.max(log_g[bb, h], axis=0, keepdims=True))
    for bb, h in items:
        kw[bb, h] = ks[bb, h].astype(F32) * jnp.exp(log_g[bb, h] - m_new[bb, h])
    for bb, h in items:
        upd[bb, h] = lax.dot_general(kw[bb, h].astype(BF16), vs[bb, h], TN, preferred_element_type=F32)
        ksum[bb, h] = jnp.sum(kw[bb, h], axis=0, keepdims=True)
    for bb, h in items:
        si = bb * heads + h
        b_last = gate[bb][4]
        keep = jnp.exp(b_last[:, h:h + 1] + m_ref[si] - m_new[bb, h])
        c_ref[si] = keep * c_ref[si] + upd[bb, h]
        n_ref[si] = keep * n_ref[si] + ksum[bb, h]
        m_ref[si] = m_new[bb, h]


def _mlstm_mixer(qk, v, gates, n_ctx):
    bsz, lt, w2 = qk.shape
    w = w2 // 2
    t, heads = MLSTM_CHUNK, MLSTM_HEADS
    dh = w // heads
    nch = lt // t
    ncc = n_ctx // t
    gc = gates.reshape(bsz, lt, 2, 2 * heads).transpose(2, 0, 1, 3)
    gr = gc.reshape(2, bsz, nch, t, 2 * heads).transpose(0, 1, 2, 4, 3)
    cmap = lambda d, b, c: _chunk_order(d, c, ncc, nch)
    bt = MIX_BATCH
    return pl.pallas_call(
        functools.partial(_mlstm_kernel, t=t, heads=heads),
        grid=(2, bsz // bt, nch),
        in_specs=[
            pl.BlockSpec((bt, t, w), lambda d, b, c: (b, cmap(d, b, c), 0)),
            pl.BlockSpec((bt, t, w), lambda d, b, c: (b, cmap(d, b, c), 1)),
            pl.BlockSpec((bt, t, w), lambda d, b, c: (b, cmap(d, b, c), 0)),
            pl.BlockSpec((1, bt, t, 2 * heads), lambda d, b, c: (d, b, cmap(d, b, c), 0)),
            pl.BlockSpec((1, bt, 1, 2 * heads, t), lambda d, b, c: (d, b, cmap(d, b, c), 0, 0)),
        ],
        out_specs=pl.BlockSpec((1, bt, t, w), lambda d, b, c: (d, b, cmap(d, b, c), 0)),
        out_shape=jax.ShapeDtypeStruct((2, bsz, lt, w), F32),
        scratch_shapes=[pltpu.VMEM((bt * heads, dh, dh), F32), pltpu.VMEM((bt * heads, 1, dh), F32),
                        pltpu.VMEM((bt * heads, 1, 1), F32)],
        compiler_params=_cparams(("parallel", "parallel", "arbitrary")),
        name="mlstm_mixer",
    )(qk, qk, v, gc, gr)


def _conv_kernel(x_ref, w_ref, b_ref, s_ref, o_ref, *, n_ctx):
    x = x_ref[0]
    lt = x.shape[0]
    row = lax.broadcasted_iota(jnp.int32, x.shape, 0)
    prev = jnp.where((row == 0) | (row == n_ctx), 0.0, pltpu.roll(x, 1, 0))
    nxt = jnp.where((row == n_ctx - 1) | (row == lt - 1), 0.0, pltpu.roll(x, lt - 1, 0))
    y = b_ref[...] + w_ref[0:1, :] * prev + w_ref[1:2, :] * x + w_ref[2:3, :] * nxt
    o_ref[0] = (y * jax.nn.sigmoid(y) * s_ref[...]).astype(o_ref.dtype)


def _conv_silu(x, w, b, colscale, n_ctx):
    bsz, lt, ch = x.shape
    tc = 256
    return pl.pallas_call(
        functools.partial(_conv_kernel, n_ctx=n_ctx),
        grid=(bsz, ch // tc),
        in_specs=[
            pl.BlockSpec((1, lt, tc), lambda b, j: (b, 0, j)),
            pl.BlockSpec((3, tc), lambda b, j: (0, j)),
            pl.BlockSpec((1, tc), lambda b, j: (0, j)),
            pl.BlockSpec((1, tc), lambda b, j: (0, j)),
        ],
        out_specs=pl.BlockSpec((1, lt, tc), lambda b, j: (b, 0, j)),
        out_shape=jax.ShapeDtypeStruct((bsz, lt, ch), BF16),
        compiler_params=_cparams(("parallel", "parallel")),
        name="conv_silu",
    )(x, w, b.reshape(1, ch), colscale.reshape(1, ch))


def _s5_matrices(a_re, a_im, log_dt, b_re, b_im, c_re, c_im, backward, lane_groups=8):
    g, p = a_re.shape
    cg = b_re.shape[-1]
    j = S5_J
    lg = lane_groups
    nq = g // lg
    dt = jnp.exp(log_dt)[:, None]
    lam_re = jnp.minimum(a_re, -1e-4)
    lam_im = a_im
    decay = jnp.exp(lam_re * dt)
    ab_re = decay * jnp.cos(lam_im * dt)
    ab_im = decay * jnp.sin(lam_im * dt)
    den = lam_re * lam_re + lam_im * lam_im
    zr = ((ab_re - 1) * lam_re + ab_im * lam_im) / den
    zi = (ab_im * lam_re - (ab_re - 1) * lam_im) / den
    bb_re = zr[..., None] * b_re - zi[..., None] * b_im
    bb_im = zr[..., None] * b_im + zi[..., None] * b_re
    pw_re, pw_im = [jnp.ones_like(ab_re)], [jnp.zeros_like(ab_im)]
    for _ in range(j):
        r0, i0 = pw_re[-1], pw_im[-1]
        pw_re.append(ab_re * r0 - ab_im * i0)
        pw_im.append(ab_re * i0 + ab_im * r0)
    pw_re, pw_im = jnp.stack(pw_re), jnp.stack(pw_im)
    ca_re = c_re[None] * pw_re[:, :, None, :] - c_im[None] * pw_im[:, :, None, :]
    ca_im = c_re[None] * pw_im[:, :, None, :] + c_im[None] * pw_re[:, :, None, :]
    kk = (jnp.einsum('tgcp,gpd->tgcd', ca_re[:j], bb_re, precision=HI)
          - jnp.einsum('tgcp,gpd->tgcd', ca_im[:j], bb_im, precision=HI))
    ab_pw_re = pw_re[:j, :, :, None] * bb_re[None] - pw_im[:j, :, :, None] * bb_im[None]
    ab_pw_im = pw_re[:j, :, :, None] * bb_im[None] + pw_im[:j, :, :, None] * bb_re[None]
    eye = jnp.eye(lg, dtype=BF16)
    lb = lg * cg
    sw = lg * p
    bd_k = jnp.einsum('tqgcd,gh->tqgdhc', kk.astype(BF16).reshape(j, nq, lg, cg, cg), eye
                      ).reshape(j, nq, lb, lb)
    bd_in = [jnp.einsum('tqgpc,gh->tqgchp', a.astype(BF16).reshape(j, nq, lg, p, cg), eye
                        ).reshape(j, nq, lb, sw) for a in (ab_pw_re, ab_pw_im)]
    bd_out = [jnp.einsum('tqgcp,gh->tqgphc', a.astype(BF16).reshape(j + 1, nq, lg, cg, p), eye
                         ).reshape(j + 1, nq, sw, lb) for a in (ca_re, -ca_im)]
    jj = jnp.arange(j)
    lag = (jj[:, None] - jj[None, :]) if backward else (jj[None, :] - jj[:, None])
    kt = jnp.where((lag >= 0)[:, :, None, None, None], bd_k[jnp.clip(lag, 0, j - 1)], 0)
    ktoep = kt.transpose(2, 0, 3, 1, 4).reshape(nq, j * lb, j * lb)
    tau_in = jj if backward else (j - 1 - jj)
    win_re, win_im = (a[tau_in].transpose(1, 0, 2, 3).reshape(nq, j * lb, sw) for a in bd_in)
    tau_out = (j - jj) if backward else (jj + 1)
    wout_re, wout_im = (a[tau_out].transpose(1, 2, 0, 3).reshape(nq, sw, j * lb) for a in bd_out)
    dec_re = pw_re[j].reshape(nq, 1, sw)
    dec_im = pw_im[j].reshape(nq, 1, sw)
    return ktoep, win_re, win_im, wout_re, wout_im, dec_re, dec_im


def _s5_kernel(u_ref, kt_ref, wir_ref, wii_ref, wor_ref, woi_ref, dr_ref, di_ref, y_ref,
               xf_ref, yf_ref, sre_ref, sim_ref, *, bt, nk, nk_ctx, rs):
    d = pl.program_id(0)
    j = S5_J
    lanes = u_ref.shape[-1]
    for b in range(bt):
        for jj in range(j):
            xf_ref[b * nk:(b + 1) * nk, jj * lanes:(jj + 1) * lanes] = (
                u_ref.at[b][pl.ds(jj, nk, stride=j), :].astype(BF16))
    xf = xf_ref[...]
    yf_ref[...] = jnp.dot(xf, kt_ref[0, 0], preferred_element_type=F32)
    inc_re = jnp.dot(xf, wir_ref[0, 0], preferred_element_type=F32)
    inc_im = jnp.dot(xf, wii_ref[0, 0], preferred_element_type=F32)
    nl = sre_ref.shape[0]
    for b in range(bt):
        for l in range(nl):
            sre_ref[l, b * rs:b * rs + nk, :] = inc_re[b * nk:(b + 1) * nk, l * lanes:(l + 1) * lanes]
            sim_ref[l, b * rs:b * rs + nk, :] = inc_im[b * nk:(b + 1) * nk, l * lanes:(l + 1) * lanes]
    a_re = [dr_ref[0, 0, :, l * lanes:(l + 1) * lanes] for l in range(nl)]
    a_im = [di_ref[0, 0, :, l * lanes:(l + 1) * lanes] for l in range(nl)]

    def step(kidx, carry):
        rows = pl.ds(kidx, bt, stride=rs)
        out = []
        for l in range(nl):
            s_re, s_im = carry[2 * l], carry[2 * l + 1]
            i_re = sre_ref.at[l][rows, :]
            i_im = sim_ref.at[l][rows, :]
            sre_ref.at[l][rows, :] = s_re
            sim_ref.at[l][rows, :] = s_im
            out.append(a_re[l] * s_re - a_im[l] * s_im + i_re)
            out.append(a_re[l] * s_im + a_im[l] * s_re + i_im)
        return tuple(out)

    zero = tuple(jnp.zeros((bt, lanes), F32) for _ in range(2 * nl))

    @pl.when(d == 0)
    def _():
        lax.fori_loop(0, nk, step, zero)

    @pl.when(d == 1)
    def _():
        carry = lax.fori_loop(0, nk_ctx, lambda i, cr: step(nk_ctx - 1 - i, cr), zero)
        lax.fori_loop(0, nk - nk_ctx, lambda i, cr: step(nk - 1 - i, cr), carry)

    for b in range(bt):
        sp_re = jnp.concatenate([sre_ref[l, b * rs:b * rs + nk, :] for l in range(nl)], axis=-1).astype(BF16)
        sp_im = jnp.concatenate([sim_ref[l, b * rs:b * rs + nk, :] for l in range(nl)], axis=-1).astype(BF16)
        yb = (yf_ref[b * nk:(b + 1) * nk, :]
              + jnp.dot(sp_re, wor_ref[0, 0], preferred_element_type=F32)
              + jnp.dot(sp_im, woi_ref[0, 0], preferred_element_type=F32))
        for jj in range(j):
            y_ref.at[0, b][pl.ds(jj, nk, stride=j), :] = yb[:, jj * lanes:(jj + 1) * lanes]


def _s5_mixer(u, mats, n_ctx):
    bsz, lt, w = u.shape
    ktoep, win_re, win_im, wout_re, wout_im, dec_re, dec_im = mats
    lanes = 128
    bt = 4 if bsz % 4 == 0 else 2
    nq = w // lanes
    j = S5_J
    nk = lt // j
    nk_ctx = n_ctx // j
    rs = nk + 8
    fl = j * lanes
    sw = win_re.shape[-1]
    wmap = lambda d, q, b: (d, q, 0, 0)
    return pl.pallas_call(
        functools.partial(_s5_kernel, bt=bt, nk=nk, nk_ctx=nk_ctx, rs=rs),
        grid=(2, nq, bsz // bt),
        in_specs=[
            pl.BlockSpec((bt, lt, lanes), lambda d, q, b: (b, 0, q)),
            pl.BlockSpec((1, 1, fl, fl), wmap),
            pl.BlockSpec((1, 1, fl, sw), wmap),
            pl.BlockSpec((1, 1, fl, sw), wmap),
            pl.BlockSpec((1, 1, sw, fl), wmap),
            pl.BlockSpec((1, 1, sw, fl), wmap),
            pl.BlockSpec((1, 1, 1, sw), wmap),
            pl.BlockSpec((1, 1, 1, sw), wmap),
        ],
        out_specs=pl.BlockSpec((1, bt, lt, lanes), lambda d, q, b: (d, b, 0, q)),
        out_shape=jax.ShapeDtypeStruct((2, bsz, lt, w), F32),
        scratch_shapes=[pltpu.VMEM((bt * nk, fl), BF16), pltpu.VMEM((bt * nk, fl), F32),
                        pltpu.VMEM((sw // lanes, bt * rs, lanes), F32),
                        pltpu.VMEM((sw // lanes, bt * rs, lanes), F32)],
        compiler_params=_cparams(("parallel", "parallel", "arbitrary")),
        name="s5_mixer",
    )(u, ktoep, win_re, win_im, wout_re, wout_im, dec_re, dec_im)


def _head_norm(x, heads):
    dh = x.shape[-1] // heads
    outs = []
    for h in range(heads):
        xh = x[:, h * dh:(h + 1) * dh]
        outs.append(xh * lax.rsqrt(jnp.mean(xh * xh, axis=-1, keepdims=True) + EPS))
    return jnp.concatenate(outs, axis=-1)


def _even_post_kernel(m_ref, o_ref, s_ref, u_ref, mg_ref, dsk_ref, gw_ref, gb_ref, w_ref, h_ref, gate_ref,
                      out_ref, *, heads):
    m = m_ref[0, 0] + m_ref[1, 0]
    m_out = _head_norm(m, heads) * mg_ref[...] * jax.nn.sigmoid(o_ref[0])
    y = jax.nn.gelu(s_ref[0, 0] + s_ref[1, 0] + dsk_ref[...] * u_ref[0])
    glu = jnp.dot(y.astype(BF16), gw_ref[...], preferred_element_type=F32) + gb_ref[...]
    s_out = y * jax.nn.sigmoid(glu)
    cat = jnp.concatenate([m_out, s_out], axis=-1).astype(BF16)
    z = jnp.dot(cat, w_ref[...], preferred_element_type=F32)
    out_ref[0] = h_ref[0] + gate_ref[0, 0] * z


def _even_post(m2, o, s2, u, mnorm_g, d_skip, glu_w, glu_b, w_out, h, gate):
    bsz, lt, d = h.shape
    mw = o.shape[-1]
    sw = u.shape[-1]
    tm = ROW_TILE
    row = lambda b, i: (b, i, 0)
    row2 = lambda b, i: (0, b, i, 0)
    const = lambda b, i: (0, 0)
    return pl.pallas_call(
        functools.partial(_even_post_kernel, heads=MLSTM_HEADS),
        grid=(bsz, lt // tm),
        in_specs=[
            pl.BlockSpec((2, 1, tm, mw), row2),
            pl.BlockSpec((1, tm, mw), row),
            pl.BlockSpec((2, 1, tm, sw), row2),
            pl.BlockSpec((1, tm, sw), row),
            pl.BlockSpec((1, mw), const),
            pl.BlockSpec((1, sw), const),
            pl.BlockSpec((sw, sw), const),
            pl.BlockSpec((1, sw), const),
            pl.BlockSpec((mw + sw, d), const),
            pl.BlockSpec((1, tm, d), row),
            pl.BlockSpec((1, 1, 1, d), _seg_map),
        ],
        out_specs=pl.BlockSpec((1, tm, d), row),
        out_shape=jax.ShapeDtypeStruct((bsz, lt, d), F32),
        compiler_params=_cparams(("parallel", "parallel")),
        name="even_post",
    )(m2, o, s2, u, mnorm_g.reshape(1, mw), d_skip.reshape(1, sw), glu_w.astype(BF16), glu_b.reshape(1, sw),
      w_out.astype(BF16), h, gate)


def _odd_post_kernel(o_ref, g_ref, ng_ref, w_ref, h_ref, gate_ref, out_ref, *, heads):
    g = g_ref[0]
    y = _head_norm(o_ref[0, 0] + o_ref[1, 0], heads) * ng_ref[...] * (g * jax.nn.sigmoid(g))
    z = jnp.dot(y.astype(BF16), w_ref[...], preferred_element_type=F32)
    out_ref[0] = h_ref[0] + gate_ref[0, 0] * z


def _odd_post(o, g, norm_g, w_out, h, gate):
    bsz, lt, d = h.shape
    dv = o.shape[-1]
    tm = ROW_TILE
    row = lambda b, i: (b, i, 0)
    const = lambda b, i: (0, 0)
    return pl.pallas_call(
        functools.partial(_odd_post_kernel, heads=GLA_HEADS),
        grid=(bsz, lt // tm),
        in_specs=[
            pl.BlockSpec((2, 1, tm, dv), lambda b, i: (0, b, i, 0)),
            pl.BlockSpec((1, tm, dv), row),
            pl.BlockSpec((1, dv), const),
            pl.BlockSpec((dv, d), const),
            pl.BlockSpec((1, tm, d), row),
            pl.BlockSpec((1, 1, 1, d), _seg_map),
        ],
        out_specs=pl.BlockSpec((1, tm, d), row),
        out_shape=jax.ShapeDtypeStruct((bsz, lt, d), F32),
        compiler_params=_cparams(("parallel", "parallel")),
        name="odd_post",
    )(o, g, norm_g.reshape(1, dv), w_out.astype(BF16), h, gate)


def kernel(x, c, ctx, c_ctx, mod_w, mod_b, norm_mix_g, norm_ffn_g, ev_w_in, ev_b_in, ev_conv_w, ev_conv_b, ev_mlstm_norm_g, ev_s5_a_re_f, ev_s5_a_im_f, ev_s5_log_dt_f, ev_s5_a_re_b, ev_s5_a_im_b, ev_s5_log_dt_b, ev_s5_b_re, ev_s5_b_im, ev_s5_c_re, ev_s5_c_im, ev_s5_d, ev_s5_glu_w, ev_s5_glu_b, ev_w_out, od_w_in, od_gate_w2_f, od_gate_b2_f, od_gate_w2_b, od_gate_b2_b, od_norm_g, od_w_out, router_w, router_b, moe_w_gu, moe_b_gu, moe_w_down, moe_b_down, final_norm_g):
    bsz, seq, d = x.shape
    n_ctx = ctx.shape[1]
    depth = mod_w.shape[0]
    lt = n_ctx + seq
    assert n_ctx == ROW_TILE and seq % ROW_TILE == 0 and seq % GRID_W == 0

    h = jnp.concatenate([ctx, x], axis=1)
    c_all = jnp.concatenate([c, c_ctx[None, :]], axis=0)
    c_all = jnp.pad(c_all, ((0, (-c_all.shape[0]) % 8), (0, 0)))
    mods = _modulation(c_all, mod_w, mod_b)
    mod_lat = mods[:, :bsz]
    mod_ctx = jnp.broadcast_to(mods[:, bsz:bsz + 1], mod_lat.shape)
    mod6 = jnp.stack([mod_ctx, mod_lat], axis=2).reshape(depth, bsz, 2, 6, 1, d)

    bg_all = moe_b_gu[..., 0::2]
    bu_all = moe_b_gu[..., 1::2]

    mw = ev_conv_w.shape[-1] // 2
    n_gates = 4 * MLSTM_HEADS
    s5w = ev_s5_d.shape[-1]
    dk_t = od_gate_w2_f.shape[-1]
    dv_t = od_norm_g.shape[-1]
    y_buf = None
    for layer in range(depth):
        last = layer == depth - 1
        j = layer // 2
        m6 = mod6[layer]
        sh1, sc1, g1, sh2, sc2, g2 = (m6[:, :, i] for i in range(6))
        if layer % 2 == 0:
            w_in, b_in = ev_w_in[j], ev_b_in[j]
            cols = jnp.concatenate([jnp.arange(0, 4 * mw), jnp.arange(4 * mw + n_gates, 4 * mw + n_gates + s5w),
                                    jnp.arange(4 * mw, 4 * mw + n_gates)])
            qk_pre, v, o, u, gates = _nm_matmul(h, norm_mix_g[layer], sh1, sc1, w_in[:, cols], b_in[cols],
                                                (2 * mw, mw, mw, s5w, n_gates))
            dh = mw // MLSTM_HEADS
            colscale = jnp.concatenate([jnp.full((mw,), dh ** -0.5, F32), jnp.ones((mw,), F32)])
            qk = _conv_silu(qk_pre, ev_conv_w[j], ev_conv_b[j], colscale, n_ctx)
            m2 = _mlstm_mixer(qk, v, gates, n_ctx)
            shared = (ev_s5_b_re[j], ev_s5_b_im[j], ev_s5_c_re[j], ev_s5_c_im[j])
            mats_f = _s5_matrices(ev_s5_a_re_f[j], ev_s5_a_im_f[j], ev_s5_log_dt_f[j], *shared, backward=False)
            mats_b = _s5_matrices(ev_s5_a_re_b[j], ev_s5_a_im_b[j], ev_s5_log_dt_b[j], *shared, backward=True)
            s2 = _s5_mixer(u, tuple(jnp.stack([a, b]) for a, b in zip(mats_f, mats_b)), n_ctx)
            h = _even_post(m2, o, s2, u, ev_mlstm_norm_g[j], ev_s5_d[j], ev_s5_glu_w[j], ev_s5_glu_b[j],
                           ev_w_out[j], h, g1)
        else:
            hc = _grid_reorder(h, n_ctx, True)
            qq, kk, vv, gg, rr = _nm_matmul(hc, norm_mix_g[layer], sh1, sc1, od_w_in[j],
                                            jnp.zeros((od_w_in.shape[-1],), F32),
                                            (dk_t, dk_t, dv_t, dv_t, 2 * GLA_RANK))
            zero = jnp.zeros_like(od_gate_w2_f[j])
            w2 = jnp.stack([jnp.concatenate([od_gate_w2_f[j], zero], axis=0),
                            jnp.concatenate([zero, od_gate_w2_b[j]], axis=0)])
            b2 = jnp.stack([od_gate_b2_f[j], od_gate_b2_b[j]])[:, None, :]
            o2 = _gla_mixer(qq, kk, vv, rr, w2, b2, n_ctx)
            h = _grid_reorder(_odd_post(o2, gg, od_norm_g[j], od_w_out[j], hc, g1), n_ctx, False)
        f, top_e, gate = _ffn_prep(h, norm_ffn_g[layer], sh2, sc2, router_w[layer], router_b[layer])
        weights = (layer, moe_w_gu, moe_w_down, bg_all[layer], bu_all[layer], moe_b_down[layer])
        h, y_buf = _moe_layer(h, f, top_e, gate, g2, weights, last, n_ctx, y_buf, final_norm_g if last else None)
    return h
```

```python
import functools

import jax
import jax.numpy as jnp
from jax import lax
from jax.experimental import pallas as pl
from jax.experimental.pallas import tpu as pltpu

F32 = jnp.float32
BF16 = jnp.bfloat16
HI = lax.Precision.HIGHEST

EPS = 1e-6
GRID_W = 64
MLSTM_HEADS = 4
S5_GROUP = 16
GLA_HEADS = 4
GLA_RANK = 16
GLA_TAU = 16.0
N_EXPERTS = 32
TOP_K = 4
SWIGLU_LIMIT = 7.0
SWIGLU_ALPHA = 1.702

ROW_TILE = 256
MOE_TILE = 512
MOE_PARTS = 4
MIX_CHUNK = 64
MLSTM_CHUNK = 128
MIX_BATCH = 8
S5_J = 8
VMEM_LIMIT = 56 * 1024 * 1024

NT = (((1,), (1,)), ((), ()))
TN = (((0,), (0,)), ((), ()))


def _cparams(sem):
    return pltpu.CompilerParams(dimension_semantics=sem, vmem_limit_bytes=VMEM_LIMIT)


def _mod_kernel(c_ref, w_ref, b_ref, o_ref):
    c = c_ref[...]
    a = c * jax.nn.sigmoid(c)
    o_ref[0] = jnp.dot(a.astype(BF16), w_ref[0].astype(BF16), preferred_element_type=F32) + b_ref[0]


def _modulation(c_all, mod_w, mod_b):
    depth, d, n6 = mod_w.shape
    rows = c_all.shape[0]
    tn = d
    return pl.pallas_call(
        _mod_kernel,
        grid=(depth, n6 // tn),
        in_specs=[
            pl.BlockSpec((rows, d), lambda l, j: (0, 0)),
            pl.BlockSpec((1, d, tn), lambda l, j: (l, 0, j)),
            pl.BlockSpec((1, 1, tn), lambda l, j: (l, 0, j)),
        ],
        out_specs=pl.BlockSpec((1, rows, tn), lambda l, j: (l, 0, j)),
        out_shape=jax.ShapeDtypeStruct((depth, rows, n6), F32),
        compiler_params=_cparams(("arbitrary", "arbitrary")),
        name="modulation",
    )(c_all, mod_w, mod_b.reshape(depth, 1, n6))


def _norm_mod(x, g, sh, sc):
    ms = jnp.mean(x * x, axis=-1, keepdims=True)
    return (x * lax.rsqrt(ms + EPS) * g) * (1.0 + sc) + sh


def _nm_matmul_kernel(x_ref, g_ref, sh_ref, sc_ref, w_ref, b_ref, *out_refs, splits):
    a = _norm_mod(x_ref[0], g_ref[...], sh_ref[0, 0], sc_ref[0, 0])
    z = jnp.dot(a.astype(BF16), w_ref[...], preferred_element_type=F32) + b_ref[...]
    for (lo, hi), o_ref in zip(splits, out_refs):
        o_ref[0] = z[:, lo:hi].astype(o_ref.dtype)


def _wide_tile(rows):
    return next(t for t in (4 * ROW_TILE, 3 * ROW_TILE, 2 * ROW_TILE, ROW_TILE) if rows % t == 0)


def _seg_map(b, i):
    return (b, jnp.minimum(i, 1), 0, 0)


def _nm_matmul(h, g, shift, scale, w, bias, widths):
    bsz, lt, d = h.shape
    p = w.shape[1]
    splits, lo = [], 0
    for wd in widths:
        splits.append((lo, lo + wd))
        lo += wd
    assert lo == p
    tm = ROW_TILE
    return pl.pallas_call(
        functools.partial(_nm_matmul_kernel, splits=tuple(splits)),
        grid=(bsz, lt // tm),
        in_specs=[
            pl.BlockSpec((1, tm, d), lambda b, i: (b, i, 0)),
            pl.BlockSpec((1, d), lambda b, i: (0, 0)),
            pl.BlockSpec((1, 1, 1, d), _seg_map),
            pl.BlockSpec((1, 1, 1, d), _seg_map),
            pl.BlockSpec((d, p), lambda b, i: (0, 0)),
            pl.BlockSpec((1, p), lambda b, i: (0, 0)),
        ],
        out_specs=[pl.BlockSpec((1, tm, wd), lambda b, i: (b, i, 0)) for wd in widths],
        out_shape=[jax.ShapeDtypeStruct((bsz, lt, wd), F32) for wd in widths],
        compiler_params=_cparams(("parallel", "parallel")),
        name="norm_mod_matmul",
    )(h, g.reshape(1, d), shift, scale, w.astype(BF16), bias.reshape(1, p))


def _ffn_prep_kernel(x_ref, g_ref, sh_ref, sc_ref, rw_ref, rb_ref, f_ref, te_ref, gt_ref, *, n_ctx):
    tm = x_ref.shape[1]
    is_ctx = pl.program_id(1) * tm + lax.broadcasted_iota(jnp.int32, (tm, 1), 0) < n_ctx
    a = _norm_mod(x_ref[0], g_ref[...], jnp.where(is_ctx, sh_ref[0, 0], sh_ref[0, 1]),
                  jnp.where(is_ctx, sc_ref[0, 0], sc_ref[0, 1]))
    f_ref[0] = a.astype(f_ref.dtype)
    logits = lax.dot_general(rw_ref[...], a.astype(BF16), NT, preferred_element_type=F32) + rb_ref[...]
    ne = logits.shape[0]
    eidx = lax.broadcasted_iota(jnp.int32, logits.shape, 0)
    work = logits
    vals, idxs = [], []
    for _ in range(TOP_K):
        m = jnp.max(work, axis=0, keepdims=True)
        idx = jnp.min(jnp.where(work == m, eidx, ne), axis=0, keepdims=True)
        vals.append(m)
        idxs.append(idx)
        work = jnp.where(eidx == idx, -jnp.inf, work)
    exps = [jnp.exp(v - vals[0]) for v in vals]
    denom = exps[0]
    for e in exps[1:]:
        denom = denom + e
    for k in range(TOP_K):
        te_ref[0, k:k + 1, :] = idxs[k]
        gt_ref[0, k:k + 1, :] = exps[k] / denom


def _ffn_prep(h, g, shift, scale, router_w, router_b):
    bsz, lt, d = h.shape
    ne = router_w.shape[1]
    tm = _wide_tile(lt)
    both = pl.BlockSpec((1, 2, 1, d), lambda b, i: (b, 0, 0, 0))
    return pl.pallas_call(
        functools.partial(_ffn_prep_kernel, n_ctx=ROW_TILE),
        grid=(bsz, lt // tm),
        in_specs=[
            pl.BlockSpec((1, tm, d), lambda b, i: (b, i, 0)),
            pl.BlockSpec((1, d), lambda b, i: (0, 0)),
            both,
            both,
            pl.BlockSpec((ne, d), lambda b, i: (0, 0)),
            pl.BlockSpec((ne, 1), lambda b, i: (0, 0)),
        ],
        out_specs=[
            pl.BlockSpec((1, tm, d), lambda b, i: (b, i, 0)),
            pl.BlockSpec((1, TOP_K, tm), lambda b, i: (b, 0, i)),
            pl.BlockSpec((1, TOP_K, tm), lambda b, i: (b, 0, i)),
        ],
        out_shape=[
            jax.ShapeDtypeStruct((bsz, lt, d), BF16),
            jax.ShapeDtypeStruct((bsz, TOP_K, lt), jnp.int32),
            jax.ShapeDtypeStruct((bsz, TOP_K, lt), F32),
        ],
        compiler_params=_cparams(("parallel", "parallel")),
        name="ffn_prep",
    )(h, g.reshape(1, d), shift, scale, router_w.T.astype(BF16), router_b.reshape(ne, 1))


GU_BLOCK = 256


def _moe_kernel(be_ref, nb_ref, first_ref, slot_ref, nxt_ref, x_ref, wgu_hbm, wd_hbm, bg_ref, bu_ref, bd_ref,
                *rest, layer, in_place):
    o_ref, wgu_buf, wd_buf, wgu_s, wd_s, sem = rest[1:] if in_place else rest
    i = pl.program_id(0)
    active = i < nb_ref[0]
    half = GU_BLOCK // 2
    nblk = wgu_s.shape[1] // GU_BLOCK

    def weight_copies(e, slot):
        return (pltpu.make_async_copy(wgu_hbm.at[layer, e], wgu_buf.at[slot], sem.at[0, slot]),
                pltpu.make_async_copy(wd_hbm.at[layer, e], wd_buf.at[slot], sem.at[1, slot]))

    @pl.when(active & (i == 0))
    def _():
        for cp in weight_copies(be_ref[0], 0):
            cp.start()

    @pl.when(active & (first_ref[i] == 1))
    def _():
        slot = slot_ref[i]
        for cp in weight_copies(be_ref[i], slot):
            cp.wait()

        @pl.when(nxt_ref[i] >= 0)
        def _():
            for cp in weight_copies(nxt_ref[i], 1 - slot):
                cp.start()

        r = lax.broadcasted_iota(jnp.int32, (GU_BLOCK, GU_BLOCK), 0)
        c = lax.broadcasted_iota(jnp.int32, (GU_BLOCK, GU_BLOCK), 1)
        perm = (r == jnp.where(c < half, 2 * c, 2 * (c - half) + 1)).astype(BF16)
        for k in range(nblk):
            cs = slice(k * GU_BLOCK, (k + 1) * GU_BLOCK)
            wgu_s[:, cs] = jnp.dot(wgu_buf[slot, :, cs].astype(BF16), perm,
                                   preferred_element_type=F32).astype(BF16)
        wd_s[...] = wd_buf[slot].astype(BF16)

    @pl.when(active)
    def _():
        gu = jnp.dot(x_ref[...], wgu_s[...], preferred_element_type=F32)
        hdn = []
        for k in range(nblk):
            hs = slice(k * half, (k + 1) * half)
            g = gu[:, k * GU_BLOCK:k * GU_BLOCK + half] + bg_ref[0, :, hs]
            u = gu[:, k * GU_BLOCK + half:(k + 1) * GU_BLOCK] + bu_ref[0, :, hs]
            g = jnp.minimum(g, SWIGLU_LIMIT)
            u = jnp.clip(u, -SWIGLU_LIMIT, SWIGLU_LIMIT)
            hdn.append(((u + 1.0) * (g * jax.nn.sigmoid(SWIGLU_ALPHA * g))).astype(BF16))
        hdn = jnp.concatenate(hdn, axis=-1)
        o_ref[...] = (jnp.dot(hdn, wd_s[...], preferred_element_type=F32) + bd_ref[0]).astype(o_ref.dtype)

    @pl.when(jnp.logical_not(active))
    def _():
        o_ref[...] = jnp.zeros_like(o_ref)


def _moe_experts(x_sorted, block_expert, n_used, layer, w_gu, w_down, bg, bu, bd, block_off, y_prev, out_rows):
    n_rows, d = x_sorted.shape
    _, ne, _, f2 = w_gu.shape
    f = f2 // 2
    tm = MOE_TILE
    x_blocks = n_rows // tm
    in_place = y_prev is not None
    n_blocks = x_blocks if in_place else out_rows // tm - block_off
    assert f2 % GU_BLOCK == 0 and block_expert.shape[0] == n_blocks
    blk = jnp.arange(n_blocks, dtype=jnp.int32)
    prev = jnp.concatenate([block_expert[:1], block_expert[:-1]])
    first = (blk < n_used[0]) & ((blk == 0) | (block_expert != prev))
    slot = (jnp.cumsum(first.astype(jnp.int32)) - 1) & 1
    first_idx = jnp.where(first, blk, n_blocks)
    next_first = lax.cummin(first_idx, axis=0, reverse=True)
    next_first = jnp.concatenate([next_first[1:], jnp.full((1,), n_blocks, jnp.int32)])
    nxt = jnp.where(next_first < n_blocks, block_expert[jnp.minimum(next_first, n_blocks - 1)], -1)
    bmap = lambda i, be, nb, fi, sl, nx: (be[i], 0, 0)
    rmap = lambda i, be, nb, fi, sl, nx: (jnp.minimum(i, x_blocks - 1), 0)
    grid_spec = pltpu.PrefetchScalarGridSpec(
        num_scalar_prefetch=5,
        grid=(n_blocks,),
        in_specs=[
            pl.BlockSpec((tm, d), rmap),
            pl.BlockSpec(memory_space=pl.ANY),
            pl.BlockSpec(memory_space=pl.ANY),
            pl.BlockSpec((1, 1, f), bmap),
            pl.BlockSpec((1, 1, f), bmap),
            pl.BlockSpec((1, 1, d), bmap),
        ] + ([pl.BlockSpec(memory_space=pl.ANY)] if in_place else []),
        out_specs=pl.BlockSpec((tm, d), lambda i, be, nb, fi, sl, nx: (i + block_off, 0)),
        scratch_shapes=[pltpu.VMEM((2, d, f2), F32), pltpu.VMEM((2, f, d), F32),
                        pltpu.VMEM((d, f2), BF16), pltpu.VMEM((f, d), BF16),
                        pltpu.SemaphoreType.DMA((2, 2))],
    )
    operands = (block_expert, n_used, first.astype(jnp.int32), slot.astype(jnp.int32), nxt.astype(jnp.int32),
                x_sorted, w_gu, w_down, bg.reshape(ne, 1, f), bu.reshape(ne, 1, f), bd.reshape(ne, 1, d))
    if in_place:
        assert y_prev.shape == (out_rows, d)
        operands = operands + (y_prev,)
    return pl.pallas_call(
        functools.partial(_moe_kernel, layer=layer, in_place=in_place),
        grid_spec=grid_spec,
        out_shape=jax.ShapeDtypeStruct((out_rows, d), BF16),
        input_output_aliases={len(operands) - 1: 0} if in_place else {},
        compiler_params=_cparams(("arbitrary",)),
        name="moe_experts",
    )(*operands)


def _combine_kernel(y_ref, gt_ref, h_ref, g2_ref, fg_ref, o_ref, *, n_ctx, final):
    gt = gt_ref[0]
    acc = y_ref[0, 0].astype(F32) * gt[:, 0:1]
    for k in range(1, TOP_K):
        acc = acc + y_ref[k, 0].astype(F32) * gt[:, k:k + 1]
    tm = acc.shape[0]
    is_ctx = pl.program_id(1) * tm + lax.broadcasted_iota(jnp.int32, (tm, 1), 0) < n_ctx
    out = h_ref[0] + jnp.where(is_ctx, g2_ref[0, 0], g2_ref[0, 1]) * acc
    if final:
        out = out * lax.rsqrt(jnp.mean(out * out, axis=-1, keepdims=True) + EPS) * fg_ref[...]
    o_ref[0] = out


def _moe_combine(yg, gate, h, g2, lat_only, final_g):
    k, bsz, lt, d = yg.shape
    tm = ROW_TILE if lat_only else _wide_tile(lt)
    tile_off = 1 if lat_only else 0
    final = final_g is not None
    return pl.pallas_call(
        functools.partial(_combine_kernel, n_ctx=0 if lat_only else ROW_TILE, final=final),
        grid=(bsz, lt // tm),
        in_specs=[
            pl.BlockSpec((k, 1, tm, d), lambda b, i: (0, b, i, 0)),
            pl.BlockSpec((1, tm, k), lambda b, i: (b, i, 0)),
            pl.BlockSpec((1, tm, d), lambda b, i: (b, i + tile_off, 0)),
            pl.BlockSpec((1, 2, 1, d), lambda b, i: (b, 0, 0, 0)),
            pl.BlockSpec((1, d), lambda b, i: (0, 0)),
        ],
        out_specs=pl.BlockSpec((1, tm, d), lambda b, i: (b, i, 0)),
        out_shape=jax.ShapeDtypeStruct((bsz, lt, d), F32),
        compiler_params=_cparams(("parallel", "parallel")),
        name="moe_combine",
    )(yg, gate, h, g2, (final_g if final else jnp.ones((d,), F32)).reshape(1, d))


def _route_kernel(e_ref, pos_ref, cnt_ref, *, tm):
    nk, rows, lanes = e_ref.shape
    li = lax.broadcasted_iota(jnp.int32, (lanes, lanes), 0)
    lj = lax.broadcasted_iota(jnp.int32, (lanes, lanes), 1)
    before_lane = (li < lj).astype(BF16)
    ones = jnp.ones((lanes, lanes), BF16)
    ri = lax.broadcasted_iota(jnp.int32, (rows, rows), 0)
    rj = lax.broadcasted_iota(jnp.int32, (rows, rows), 1)
    before_row = (rj < ri).astype(BF16)
    lane = lax.broadcasted_iota(jnp.int32, (1, lanes), 1)
    xs = [e_ref[k] for k in range(nk)]
    pos = [jnp.zeros((rows, lanes), F32) for _ in range(nk)]
    counts = jnp.zeros((1, lanes), F32)
    pad_off = jnp.zeros((1, lanes), F32)
    for e in range(N_EXPERTS):
        ms = [x == e for x in xs]
        hit = ms[0]
        for m in ms[1:]:
            hit = hit | m
        mb = hit.astype(BF16)
        in_row = jnp.dot(mb, before_lane, preferred_element_type=F32)
        row_sum = jnp.dot(mb, ones, preferred_element_type=F32)
        row_off = jnp.dot(before_row, row_sum.astype(BF16), preferred_element_type=F32)
        count = row_off[rows - 1:rows, :] + row_sum[rows - 1:rows, :]
        dest = in_row + row_off + pad_off
        pos = [p + jnp.where(m, dest, 0.0) for p, m in zip(pos, ms)]
        counts = jnp.where(lane == e, count, counts)
        pad_off = pad_off + jnp.floor((count + (tm - 1)) * (1.0 / tm)) * tm
    for k in range(nk):
        pos_ref[k] = pos[k].astype(jnp.int32)
    cnt_ref[...] = counts.astype(jnp.int32)


def _route_positions(top_e, tm):
    lanes = 128
    nk, n = top_e.shape
    rows = n // lanes
    pos, counts = pl.pallas_call(
        functools.partial(_route_kernel, tm=tm),
        out_shape=[jax.ShapeDtypeStruct((nk, rows, lanes), jnp.int32), jax.ShapeDtypeStruct((1, lanes), jnp.int32)],
        compiler_params=pltpu.CompilerParams(vmem_limit_bytes=VMEM_LIMIT),
        name="route_positions",
    )(top_e.reshape(nk, rows, lanes))
    return pos.reshape(nk, n), counts[0, :N_EXPERTS]


def _moe_rows(n_tokens):
    n_assign = n_tokens * TOP_K
    return -(-(n_assign + N_EXPERTS * (MOE_TILE - 1)) // MOE_TILE) * MOE_TILE


def _moe_layer(h, f, top_e, gate, g2, weights, lat_only, n_ctx, y_buf, final_g):
    layer, w_gu, w_down, bg, bu, bd = weights
    bsz, lt, d = h.shape
    skip = n_ctx if lat_only else 0
    ltok = lt - skip
    top_e, gate = top_e[:, :, skip:], gate[:, :, skip:]
    n = bsz * ltok
    n_assign = n * TOP_K
    tm = MOE_TILE
    te = top_e.transpose(1, 0, 2).reshape(TOP_K, n).astype(jnp.int32)
    bits = max(n - 1, 1).bit_length()
    assert N_EXPERTS << bits < 2 ** 31
    keys = (te << bits) | jnp.arange(n, dtype=jnp.int32)[None, :]
    slot_token = lax.sort(keys.reshape(-1)) & ((1 << bits) - 1)
    pos, counts = _route_positions(te, tm)
    start = jnp.cumsum(counts) - counts
    padded = (counts + tm - 1) // tm * tm
    pad_end = jnp.cumsum(padded)
    pad_start = pad_end - padded
    n_rows = _moe_rows(n)
    n_blocks = n_rows // tm
    buf_rows = n_rows if y_buf is None else y_buf.shape[0]
    assert buf_rows >= n_rows
    block_expert = jnp.minimum(
        jnp.searchsorted(pad_end, jnp.arange(n_blocks, dtype=jnp.int32) * tm, side='right', method='compare_all'),
        N_EXPERTS - 1).astype(jnp.int32)
    n_used = (pad_end[-1] // tm).astype(jnp.int32).reshape(1)
    row = jnp.arange(n_rows, dtype=jnp.int32).reshape(n_blocks, tm)
    blk_shift = (start - pad_start)[block_expert][:, None]
    blk_end = (pad_start + counts)[block_expert][:, None]
    slot = jnp.clip(row + blk_shift, 0, n_assign - 1).reshape(-1)
    row_token = jnp.where((row < blk_end).reshape(-1),
                          slot_token.at[slot].get(mode='promise_in_bounds'), row.reshape(-1) % n)
    row_src = row_token + skip * (row_token // ltok + 1)
    f2d = f.reshape(bsz * lt, d)
    cuts = [n_blocks * p // MOE_PARTS for p in range(MOE_PARTS + 1)]
    y = y_buf
    for lo, hi in zip(cuts[:-1], cuts[1:]):
        x_part = f2d.at[row_src[lo * tm:hi * tm]].get(mode='promise_in_bounds')
        blocks = block_expert[lo:hi] if y is not None else block_expert[lo:]
        y = _moe_experts(x_part, blocks, jnp.clip(n_used - lo, 0, hi - lo), layer,
                         w_gu, w_down, bg, bu, bd, lo, y, buf_rows)
    yg = y.at[pos.reshape(-1)].get(mode='promise_in_bounds').reshape(TOP_K, bsz, ltok, d)
    return _moe_combine(yg, gate.transpose(0, 2, 1), h, g2, lat_only, final_g), y


def _grid_reorder_kernel(x_ref, o_ref, *, n_ctx, rows, to_cols):
    o_ref[0, :n_ctx, :] = x_ref[0, :n_ctx, :]
    for c in range(GRID_W):
        raster = pl.ds(n_ctx + c, rows, stride=GRID_W)
        dense = pl.ds(n_ctx + c * rows, rows)
        if to_cols:
            o_ref.at[0][dense, :] = x_ref.at[0][raster, :]
        else:
            o_ref.at[0][raster, :] = x_ref.at[0][dense, :]


def _grid_reorder(h, n_ctx, to_cols):
    bsz, lt, d = h.shape
    lanes = 128
    spec = pl.BlockSpec((1, lt, lanes), lambda b, j: (b, 0, j))
    return pl.pallas_call(
        functools.partial(_grid_reorder_kernel, n_ctx=n_ctx, rows=(lt - n_ctx) // GRID_W, to_cols=to_cols),
        grid=(bsz, d // lanes),
        in_specs=[spec],
        out_specs=spec,
        out_shape=jax.ShapeDtypeStruct(h.shape, h.dtype),
        compiler_params=_cparams(("parallel", "parallel")),
        name="grid_reorder",
    )(h)


def _chunk_order(d, c, n_ctx_chunks, n_chunks):
    bwd = jnp.where(c < n_ctx_chunks, n_ctx_chunks - 1 - c, n_chunks + n_ctx_chunks - 1 - c)
    return jnp.where(d == 0, c, bwd)


def _split_bf16(x, n):
    out = []
    for _ in range(n):
        p = x.astype(BF16)
        out.append(p)
        x = x - p.astype(F32)
    return out


def _dir_tri(d, t):
    row = lax.broadcasted_iota(jnp.int32, (t, t), 0)
    col = lax.broadcasted_iota(jnp.int32, (t, t), 1)
    return jnp.where(d == 0, col - row, row - col) <= 0


def _gla_kernel(q_ref, k_ref, v_ref, r_ref, w2_ref, b2_ref, o_ref, st_ref, *, t, heads, scale):
    d = pl.program_id(0)
    c = pl.program_id(2)

    @pl.when(c == 0)
    def _():
        st_ref[...] = jnp.zeros_like(st_ref)

    dk = q_ref.shape[-1] // heads
    dv = v_ref.shape[-1] // heads
    nb = q_ref.shape[0]
    mask = _dir_tri(d, t)
    tri = mask.astype(F32)
    mid = t // 2
    items = [(bb, h) for bb in range(nb) for h in range(heads)]
    w_hi, w_lo = _split_bf16(w2_ref[0], 2)
    xs = []
    for bb in range(nb):
        r_hi, r_lo = _split_bf16(r_ref[bb], 2)
        xs.append(jnp.dot(r_hi, w_hi, preferred_element_type=F32) + jnp.dot(r_hi, w_lo, preferred_element_type=F32)
                  + jnp.dot(r_lo, w_hi, preferred_element_type=F32) + b2_ref[0])
    las = [jax.nn.log_sigmoid(x) * (1.0 / GLA_TAU) for x in xs]
    tri_b = tri.astype(BF16)
    bs = [sum(jnp.dot(tri_b, p, preferred_element_type=F32) for p in _split_bf16(la, 3)) for la in las]
    qt, kt, qe, kh_end, e_end = [], [], [], [], []
    for bb in range(nb):
        b = bs[bb]
        b_m = b[mid:mid + 1, :]
        b_end = jnp.where(d == 0, b[t - 1:t, :], b[0:1, :])
        q_s = q_ref[bb] * (jnp.exp(b - b_m) * scale)
        k_s = k_ref[bb] * jnp.exp(b_m - b)
        qe.append((q_s * jnp.exp(b_m)).astype(BF16))
        kh_end.append((k_s * jnp.exp(b_end - b_m)).astype(BF16))
        e_end.append(jnp.exp(b_end))
        qt.append(q_s.astype(BF16))
        kt.append(k_s.astype(BF16))
    att, q_st, vs = {}, {}, {}
    for bb, h in items:
        ks = slice(h * dk, (h + 1) * dk)
        vs[bb, h] = v_ref[bb, :, h * dv:(h + 1) * dv].astype(BF16)
        att[bb, h] = lax.dot_general(qt[bb][:, ks], kt[bb][:, ks], NT, preferred_element_type=F32)
        q_st[bb, h] = lax.dot_general(qe[bb][:, ks], st_ref[bb * heads + h].astype(BF16), NT,
                                      preferred_element_type=F32)
    for bb, h in items:
        a = jnp.where(mask, att[bb, h], 0.0).astype(BF16)
        o_ref[0, bb, :, h * dv:(h + 1) * dv] = jnp.dot(a, vs[bb, h], preferred_element_type=F32) + q_st[bb, h]
    for bb, h in items:
        ks = slice(h * dk, (h + 1) * dk)
        upd = lax.dot_general(vs[bb, h], kh_end[bb][:, ks], TN, preferred_element_type=F32)
        st_ref[bb * heads + h] = st_ref[bb * heads + h] * e_end[bb][:, ks] + upd


def _gla_mixer(q, k, v, r, w2, b2, n_ctx):
    bsz, lt, dkt = q.shape
    dvt = v.shape[-1]
    nr = r.shape[-1]
    t, heads = MIX_CHUNK, GLA_HEADS
    bt = MIX_BATCH
    nch = lt // t
    ncc = n_ctx // t
    dk = dkt // heads
    dv = dvt // heads
    imap = lambda d, b, c: (b, _chunk_order(d, c, ncc, nch), 0)
    return pl.pallas_call(
        functools.partial(_gla_kernel, t=t, heads=heads, scale=dk ** -0.5),
        grid=(2, bsz // bt, nch),
        in_specs=[
            pl.BlockSpec((bt, t, dkt), imap),
            pl.BlockSpec((bt, t, dkt), imap),
            pl.BlockSpec((bt, t, dvt), imap),
            pl.BlockSpec((bt, t, nr), imap),
            pl.BlockSpec((1, nr, dkt), lambda d, b, c: (d, 0, 0)),
            pl.BlockSpec((1, 1, dkt), lambda d, b, c: (d, 0, 0)),
        ],
        out_specs=pl.BlockSpec((1, bt, t, dvt), lambda d, b, c: (d, b, _chunk_order(d, c, ncc, nch), 0)),
        out_shape=jax.ShapeDtypeStruct((2, bsz, lt, dvt), F32),
        scratch_shapes=[pltpu.VMEM((bt * heads, dv, dk), F32)],
        compiler_params=_cparams(("parallel", "parallel", "arbitrary")),
        name="gla_mixer",
    )(q, k, v, r, w2, b2)


def _mlstm_kernel(q_ref, k_ref, v_ref, gc_ref, gr_ref, o_ref, c_ref, n_ref, m_ref, *, t, heads):
    d = pl.program_id(0)
    c = pl.program_id(2)

    @pl.when(c == 0)
    def _():
        c_ref[...] = jnp.zeros_like(c_ref)
        n_ref[...] = jnp.zeros_like(n_ref)
        m_ref[...] = jnp.zeros_like(m_ref)

    dh = q_ref.shape[-1] // heads
    nb = q_ref.shape[0]
    mask = _dir_tri(d, t)
    tri = mask.astype(F32)
    items = [(bb, h) for bb in range(nb) for h in range(heads)]
    gate = []
    for bb in range(nb):
        gc = gc_ref[0, bb]
        gr = gr_ref[0, bb, 0]
        fc = jax.nn.log_sigmoid(gc[:, heads:])
        fr = jax.nn.log_sigmoid(gr[heads:, :])
        b_col = jnp.dot(tri, fc, preferred_element_type=F32, precision=HI)
        b_row = lax.dot_general(fr, tri, NT, preferred_element_type=F32, precision=HI)
        b_last = jnp.where(d == 0, b_col[t - 1:t, :], b_col[0:1, :])
        gate.append((gc[:, :heads], gr[:heads, :], b_col, b_row, b_last))
    qs, ks, vs, s_raw, q_c = {}, {}, {}, {}, {}
    for bb, h in items:
        hs = slice(h * dh, (h + 1) * dh)
        qs[bb, h] = q_ref[bb, :, hs]
        ks[bb, h] = k_ref[bb, :, hs]
        vs[bb, h] = v_ref[bb, :, hs].astype(BF16)
        s_raw[bb, h] = lax.dot_general(qs[bb, h], ks[bb, h], NT, preferred_element_type=F32)
        q_c[bb, h] = jnp.dot(qs[bb, h], c_ref[bb * heads + h].astype(BF16), preferred_element_type=F32)
    logw, log_inter, m_t, w_inter, scores, den, qn = {}, {}, {}, {}, {}, {}, {}
    for bb, h in items:
        _, ir, b_col, b_row, _ = gate[bb]
        bc = b_col[:, h:h + 1]
        logw[bb, h] = jnp.where(mask, bc - b_row[h:h + 1, :] + ir[h:h + 1, :], -jnp.inf)
        log_inter[bb, h] = bc + m_ref[bb * heads + h]
    for bb, h in items:
        m_t[bb, h] = jnp.maximum(log_inter[bb, h], jnp.max(logw[bb, h], axis=-1, keepdims=True))
        qn[bb, h] = jnp.sum(qs[bb, h].astype(F32) * n_ref[bb * heads + h], axis=-1, keepdims=True)
    for bb, h in items:
        w_inter[bb, h] = jnp.exp(log_inter[bb, h] - m_t[bb, h])
        scores[bb, h] = s_raw[bb, h] * jnp.exp(logw[bb, h] - m_t[bb, h])
    for bb, h in items:
        den[bb, h] = jnp.sum(scores[bb, h], axis=-1, keepdims=True) + w_inter[bb, h] * qn[bb, h]
    num = {}
    for bb, h in items:
        num[bb, h] = (jnp.dot(scores[bb, h].astype(BF16), vs[bb, h], preferred_element_type=F32)
                      + w_inter[bb, h] * q_c[bb, h])
    for bb, h in items:
        hs = slice(h * dh, (h + 1) * dh)
        o_ref[0, bb, :, hs] = num[bb, h] / jnp.maximum(jnp.abs(den[bb, h]), jnp.exp(-m_t[bb, h]))
    log_g, m_new, kw, upd, ksum = {}, {}, {}, {}, {}
    for bb, h in items:
        ic, _, b_col, _, b_last = gate[bb]
        log_g[bb, h] = b_last[:, h:h + 1] - b_col[:, h:h + 1] + ic[:, h:h + 1]
    for bb, h in items:
        b_last = gate[bb][4]
        m_new[bb, h] = jnp.maximum(b_last[:, h:h + 1] + m_ref[bb * heads + h],
                                   jnp.max(log_g[bb, h], axis=0, keepdims=True))
    for bb, h in items:
        kw[bb, h] = ks[bb, h].astype(F32) * jnp.exp(log_g[bb, h] - m_new[bb, h])
    for bb, h in items:
        upd[bb, h] = lax.dot_general(kw[bb, h].astype(BF16), vs[bb, h], TN, preferred_element_type=F32)
        ksum[bb, h] = jnp.sum(kw[bb, h], axis=0, keepdims=True)
    for bb, h in items:
        si = bb * heads + h
        b_last = gate[bb][4]
        keep = jnp.exp(b_last[:, h:h + 1] + m_ref[si] - m_new[bb, h])
        c_ref[si] = keep * c_ref[si] + upd[bb, h]
        n_ref[si] = keep * n_ref[si] + ksum[bb, h]
        m_ref[si] = m_new[bb, h]


def _mlstm_mixer(qk, v, gates, n_ctx):
    bsz, lt, w2 = qk.shape
    w = w2 // 2
    t, heads = MLSTM_CHUNK, MLSTM_HEADS
    dh = w // heads
    nch = lt // t
    ncc = n_ctx // t
    gc = gates.reshape(bsz, lt, 2, 2 * heads).transpose(2, 0, 1, 3)
    gr = gc.reshape(2, bsz, nch, t, 2 * heads).transpose(0, 1, 2, 4, 3)
    cmap = lambda d, b, c: _chunk_order(d, c, ncc, nch)
    bt = MIX_BATCH
    return pl.pallas_call(
        functools.partial(_mlstm_kernel, t=t, heads=heads),
        grid=(2, bsz // bt, nch),
        in_specs=[
            pl.BlockSpec((bt, t, w), lambda d, b, c: (b, cmap(d, b, c), 0)),
            pl.BlockSpec((bt, t, w), lambda d, b, c: (b, cmap(d, b, c), 1)),
            pl.BlockSpec((bt, t, w), lambda d, b, c: (b, cmap(d, b, c), 0)),
            pl.BlockSpec((1, bt, t, 2 * heads), lambda d, b, c: (d, b, cmap(d, b, c), 0)),
            pl.BlockSpec((1, bt, 1, 2 * heads, t), lambda d, b, c: (d, b, cmap(d, b, c), 0, 0)),
        ],
        out_specs=pl.BlockSpec((1, bt, t, w), lambda d, b, c: (d, b, cmap(d, b, c), 0)),
        out_shape=jax.ShapeDtypeStruct((2, bsz, lt, w), F32),
        scratch_shapes=[pltpu.VMEM((bt * heads, dh, dh), F32), pltpu.VMEM((bt * heads, 1, dh), F32),
                        pltpu.VMEM((bt * heads, 1, 1), F32)],
        compiler_params=_cparams(("parallel", "parallel", "arbitrary")),
        name="mlstm_mixer",
    )(qk, qk, v, gc, gr)


def _conv_kernel(x_ref, w_ref, b_ref, s_ref, o_ref, *, n_ctx):
    x = x_ref[0]
    lt = x.shape[0]
    row = lax.broadcasted_iota(jnp.int32, x.shape, 0)
    prev = jnp.where((row == 0) | (row == n_ctx), 0.0, pltpu.roll(x, 1, 0))
    nxt = jnp.where((row == n_ctx - 1) | (row == lt - 1), 0.0, pltpu.roll(x, lt - 1, 0))
    y = b_ref[...] + w_ref[0:1, :] * prev + w_ref[1:2, :] * x + w_ref[2:3, :] * nxt
    o_ref[0] = (y * jax.nn.sigmoid(y) * s_ref[...]).astype(o_ref.dtype)


def _conv_silu(x, w, b, colscale, n_ctx):
    bsz, lt, ch = x.shape
    tc = 256
    return pl.pallas_call(
        functools.partial(_conv_kernel, n_ctx=n_ctx),
        grid=(bsz, ch // tc),
        in_specs=[
            pl.BlockSpec((1, lt, tc), lambda b, j: (b, 0, j)),
            pl.BlockSpec((3, tc), lambda b, j: (0, j)),
            pl.BlockSpec((1, tc), lambda b, j: (0, j)),
            pl.BlockSpec((1, tc), lambda b, j: (0, j)),
        ],
        out_specs=pl.BlockSpec((1, lt, tc), lambda b, j: (b, 0, j)),
        out_shape=jax.ShapeDtypeStruct((bsz, lt, ch), BF16),
        compiler_params=_cparams(("parallel", "parallel")),
        name="conv_silu",
    )(x, w, b.reshape(1, ch), colscale.reshape(1, ch))


def _s5_matrices(a_re, a_im, log_dt, b_re, b_im, c_re, c_im, backward, lane_groups=8):
    g, p = a_re.shape
    cg = b_re.shape[-1]
    j = S5_J
    lg = lane_groups
    nq = g // lg
    dt = jnp.exp(log_dt)[:, None]
    lam_re = jnp.minimum(a_re, -1e-4)
    lam_im = a_im
    decay = jnp.exp(lam_re * dt)
    ab_re = decay * jnp.cos(lam_im * dt)
    ab_im = decay * jnp.sin(lam_im * dt)
    den = lam_re * lam_re + lam_im * lam_im
    zr = ((ab_re - 1) * lam_re + ab_im * lam_im) / den
    zi = (ab_im * lam_re - (ab_re - 1) * lam_im) / den
    bb_re = zr[..., None] * b_re - zi[..., None] * b_im
    bb_im = zr[..., None] * b_im + zi[..., None] * b_re
    pw_re, pw_im = [jnp.ones_like(ab_re)], [jnp.zeros_like(ab_im)]
    for _ in range(j):
        r0, i0 = pw_re[-1], pw_im[-1]
        pw_re.append(ab_re * r0 - ab_im * i0)
        pw_im.append(ab_re * i0 + ab_im * r0)
    pw_re, pw_im = jnp.stack(pw_re), jnp.stack(pw_im)
    ca_re = c_re[None] * pw_re[:, :, None, :] - c_im[None] * pw_im[:, :, None, :]
    ca_im = c_re[None] * pw_im[:, :, None, :] + c_im[None] * pw_re[:, :, None, :]
    kk = (jnp.einsum('tgcp,gpd->tgcd', ca_re[:j], bb_re, precision=HI)
          - jnp.einsum('tgcp,gpd->tgcd', ca_im[:j], bb_im, precision=HI))
    ab_pw_re = pw_re[:j, :, :, None] * bb_re[None] - pw_im[:j, :, :, None] * bb_im[None]
    ab_pw_im = pw_re[:j, :, :, None] * bb_im[None] + pw_im[:j, :, :, None] * bb_re[None]
    eye = jnp.eye(lg, dtype=BF16)
    lb = lg * cg
    sw = lg * p
    bd_k = jnp.einsum('tqgcd,gh->tqgdhc', kk.astype(BF16).reshape(j, nq, lg, cg, cg), eye
                      ).reshape(j, nq, lb, lb)
    bd_in = [jnp.einsum('tqgpc,gh->tqgchp', a.astype(BF16).reshape(j, nq, lg, p, cg), eye
                        ).reshape(j, nq, lb, sw) for a in (ab_pw_re, ab_pw_im)]
    bd_out = [jnp.einsum('tqgcp,gh->tqgphc', a.astype(BF16).reshape(j + 1, nq, lg, cg, p), eye
                         ).reshape(j + 1, nq, sw, lb) for a in (ca_re, -ca_im)]
    jj = jnp.arange(j)
    lag = (jj[:, None] - jj[None, :]) if backward else (jj[None, :] - jj[:, None])
    kt = jnp.where((lag >= 0)[:, :, None, None, None], bd_k[jnp.clip(lag, 0, j - 1)], 0)
    ktoep = kt.transpose(2, 0, 3, 1, 4).reshape(nq, j * lb, j * lb)
    tau_in = jj if backward else (j - 1 - jj)
    win_re, win_im = (a[tau_in].transpose(1, 0, 2, 3).reshape(nq, j * lb, sw) for a in bd_in)
    tau_out = (j - jj) if backward else (jj + 1)
    wout_re, wout_im = (a[tau_out].transpose(1, 2, 0, 3).reshape(nq, sw, j * lb) for a in bd_out)
    dec_re = pw_re[j].reshape(nq, 1, sw)
    dec_im = pw_im[j].reshape(nq, 1, sw)
    return ktoep, win_re, win_im, wout_re, wout_im, dec_re, dec_im


def _s5_kernel(u_ref, kt_ref, wir_ref, wii_ref, wor_ref, woi_ref, dr_ref, di_ref, y_ref,
               xf_ref, yf_ref, sre_ref, sim_ref, *, bt, nk, nk_ctx, rs):
    d = pl.program_id(0)
    j = S5_J
    lanes = u_ref.shape[-1]
    for b in range(bt):
        for jj in range(j):
            xf_ref[b * nk:(b + 1) * nk, jj * lanes:(jj + 1) * lanes] = (
                u_ref.at[b][pl.ds(jj, nk, stride=j), :].astype(BF16))
    xf = xf_ref[...]
    yf_ref[...] = jnp.dot(xf, kt_ref[0, 0], preferred_element_type=F32)
    inc_re = jnp.dot(xf, wir_ref[0, 0], preferred_element_type=F32)
    inc_im = jnp.dot(xf, wii_ref[0, 0], preferred_element_type=F32)
    nl = sre_ref.shape[0]
    for b in range(bt):
        for l in range(nl):
            sre_ref[l, b * rs:b * rs + nk, :] = inc_re[b * nk:(b + 1) * nk, l * lanes:(l + 1) * lanes]
            sim_ref[l, b * rs:b * rs + nk, :] = inc_im[b * nk:(b + 1) * nk, l * lanes:(l + 1) * lanes]
    a_re = [dr_ref[0, 0, :, l * lanes:(l + 1) * lanes] for l in range(nl)]
    a_im = [di_ref[0, 0, :, l * lanes:(l + 1) * lanes] for l in range(nl)]

    def step(kidx, carry):
        rows = pl.ds(kidx, bt, stride=rs)
        out = []
        for l in range(nl):
            s_re, s_im = carry[2 * l], carry[2 * l + 1]
            i_re = sre_ref.at[l][rows, :]
            i_im = sim_ref.at[l][rows, :]
            sre_ref.at[l][rows, :] = s_re
            sim_ref.at[l][rows, :] = s_im
            out.append(a_re[l] * s_re - a_im[l] * s_im + i_re)
            out.append(a_re[l] * s_im + a_im[l] * s_re + i_im)
        return tuple(out)

    zero = tuple(jnp.zeros((bt, lanes), F32) for _ in range(2 * nl))

    @pl.when(d == 0)
    def _():
        lax.fori_loop(0, nk, step, zero)

    @pl.when(d == 1)
    def _():
        carry = lax.fori_loop(0, nk_ctx, lambda i, cr: step(nk_ctx - 1 - i, cr), zero)
        lax.fori_loop(0, nk - nk_ctx, lambda i, cr: step(nk - 1 - i, cr), carry)

    for b in range(bt):
        sp_re = jnp.concatenate([sre_ref[l, b * rs:b * rs + nk, :] for l in range(nl)], axis=-1).astype(BF16)
        sp_im = jnp.concatenate([sim_ref[l, b * rs:b * rs + nk, :] for l in range(nl)], axis=-1).astype(BF16)
        yb = (yf_ref[b * nk:(b + 1) * nk, :]
              + jnp.dot(sp_re, wor_ref[0, 0], preferred_element_type=F32)
              + jnp.dot(sp_im, woi_ref[0, 0], preferred_element_type=F32))
        for jj in range(j):
            y_ref.at[0, b][pl.ds(jj, nk, stride=j), :] = yb[:, jj * lanes:(jj + 1) * lanes]


def _s5_mixer(u, mats, n_ctx):
    bsz, lt, w = u.shape
    ktoep, win_re, win_im, wout_re, wout_im, dec_re, dec_im = mats
    lanes = 128
    bt = 4 if bsz % 4 == 0 else 2
    nq = w // lanes
    j = S5_J
    nk = lt // j
    nk_ctx = n_ctx // j
    rs = nk + 8
    fl = j * lanes
    sw = win_re.shape[-1]
    wmap = lambda d, q, b: (d, q, 0, 0)
    return pl.pallas_call(
        functools.partial(_s5_kernel, bt=bt, nk=nk, nk_ctx=nk_ctx, rs=rs),
        grid=(2, nq, bsz // bt),
        in_specs=[
            pl.BlockSpec((bt, lt, lanes), lambda d, q, b: (b, 0, q)),
            pl.BlockSpec((1, 1, fl, fl), wmap),
            pl.BlockSpec((1, 1, fl, sw), wmap),
            pl.BlockSpec((1, 1, fl, sw), wmap),
            pl.BlockSpec((1, 1, sw, fl), wmap),
            pl.BlockSpec((1, 1, sw, fl), wmap),
            pl.BlockSpec((1, 1, 1, sw), wmap),
            pl.BlockSpec((1, 1, 1, sw), wmap),
        ],
        out_specs=pl.BlockSpec((1, bt, lt, lanes), lambda d, q, b: (d, b, 0, q)),
        out_shape=jax.ShapeDtypeStruct((2, bsz, lt, w), F32),
        scratch_shapes=[pltpu.VMEM((bt * nk, fl), BF16), pltpu.VMEM((bt * nk, fl), F32),
                        pltpu.VMEM((sw // lanes, bt * rs, lanes), F32),
                        pltpu.VMEM((sw // lanes, bt * rs, lanes), F32)],
        compiler_params=_cparams(("parallel", "parallel", "arbitrary")),
        name="s5_mixer",
    )(u, ktoep, win_re, win_im, wout_re, wout_im, dec_re, dec_im)


def _head_norm(x, heads):
    dh = x.shape[-1] // heads
    outs = []
    for h in range(heads):
        xh = x[:, h * dh:(h + 1) * dh]
        outs.append(xh * lax.rsqrt(jnp.mean(xh * xh, axis=-1, keepdims=True) + EPS))
    return jnp.concatenate(outs, axis=-1)


def _even_post_kernel(m_ref, o_ref, s_ref, u_ref, mg_ref, dsk_ref, gw_ref, gb_ref, w_ref, h_ref, gate_ref,
                      out_ref, *, heads):
    m = m_ref[0, 0] + m_ref[1, 0]
    m_out = _head_norm(m, heads) * mg_ref[...] * jax.nn.sigmoid(o_ref[0])
    y = jax.nn.gelu(s_ref[0, 0] + s_ref[1, 0] + dsk_ref[...] * u_ref[0])
    glu = jnp.dot(y.astype(BF16), gw_ref[...], preferred_element_type=F32) + gb_ref[...]
    s_out = y * jax.nn.sigmoid(glu)
    cat = jnp.concatenate([m_out, s_out], axis=-1).astype(BF16)
    z = jnp.dot(cat, w_ref[...], preferred_element_type=F32)
    out_ref[0] = h_ref[0] + gate_ref[0, 0] * z


def _even_post(m2, o, s2, u, mnorm_g, d_skip, glu_w, glu_b, w_out, h, gate):
    bsz, lt, d = h.shape
    mw = o.shape[-1]
    sw = u.shape[-1]
    tm = ROW_TILE
    row = lambda b, i: (b, i, 0)
    row2 = lambda b, i: (0, b, i, 0)
    const = lambda b, i: (0, 0)
    return pl.pallas_call(
        functools.partial(_even_post_kernel, heads=MLSTM_HEADS),
        grid=(bsz, lt // tm),
        in_specs=[
            pl.BlockSpec((2, 1, tm, mw), row2),
            pl.BlockSpec((1, tm, mw), row),
            pl.BlockSpec((2, 1, tm, sw), row2),
            pl.BlockSpec((1, tm, sw), row),
            pl.BlockSpec((1, mw), const),
            pl.BlockSpec((1, sw), const),
            pl.BlockSpec((sw, sw), const),
            pl.BlockSpec((1, sw), const),
            pl.BlockSpec((mw + sw, d), const),
            pl.BlockSpec((1, tm, d), row),
            pl.BlockSpec((1, 1, 1, d), _seg_map),
        ],
        out_specs=pl.BlockSpec((1, tm, d), row),
        out_shape=jax.ShapeDtypeStruct((bsz, lt, d), F32),
        compiler_params=_cparams(("parallel", "parallel")),
        name="even_post",
    )(m2, o, s2, u, mnorm_g.reshape(1, mw), d_skip.reshape(1, sw), glu_w.astype(BF16), glu_b.reshape(1, sw),
      w_out.astype(BF16), h, gate)


def _odd_post_kernel(o_ref, g_ref, ng_ref, w_ref, h_ref, gate_ref, out_ref, *, heads):
    g = g_ref[0]
    y = _head_norm(o_ref[0, 0] + o_ref[1, 0], heads) * ng_ref[...] * (g * jax.nn.sigmoid(g))
    z = jnp.dot(y.astype(BF16), w_ref[...], preferred_element_type=F32)
    out_ref[0] = h_ref[0] + gate_ref[0, 0] * z


def _odd_post(o, g, norm_g, w_out, h, gate):
    bsz, lt, d = h.shape
    dv = o.shape[-1]
    tm = ROW_TILE
    row = lambda b, i: (b, i, 0)
    const = lambda b, i: (0, 0)
    return pl.pallas_call(
        functools.partial(_odd_post_kernel, heads=GLA_HEADS),
        grid=(bsz, lt // tm),
        in_specs=[
            pl.BlockSpec((2, 1, tm, dv), lambda b, i: (0, b, i, 0)),
            pl.BlockSpec((1, tm, dv), row),
            pl.BlockSpec((1, dv), const),
            pl.BlockSpec((dv, d), const),
            pl.BlockSpec((1, tm, d), row),
            pl.BlockSpec((1, 1, 1, d), _seg_map),
        ],
        out_specs=pl.BlockSpec((1, tm, d), row),
        out_shape=jax.ShapeDtypeStruct((bsz, lt, d), F32),
        compiler_params=_cparams(("parallel", "parallel")),
        name="odd_post",
    )(o, g, norm_g.reshape(1, dv), w_out.astype(BF16), h, gate)


def kernel(x, c, ctx, c_ctx, mod_w, mod_b, norm_mix_g, norm_ffn_g, ev_w_in, ev_b_in, ev_conv_w, ev_conv_b, ev_mlstm_norm_g, ev_s5_a_re_f, ev_s5_a_im_f, ev_s5_log_dt_f, ev_s5_a_re_b, ev_s5_a_im_b, ev_s5_log_dt_b, ev_s5_b_re, ev_s5_b_im, ev_s5_c_re, ev_s5_c_im, ev_s5_d, ev_s5_glu_w, ev_s5_glu_b, ev_w_out, od_w_in, od_gate_w2_f, od_gate_b2_f, od_gate_w2_b, od_gate_b2_b, od_norm_g, od_w_out, router_w, router_b, moe_w_gu, moe_b_gu, moe_w_down, moe_b_down, final_norm_g):
    bsz, seq, d = x.shape
    n_ctx = ctx.shape[1]
    depth = mod_w.shape[0]
    lt = n_ctx + seq
    assert n_ctx == ROW_TILE and seq % ROW_TILE == 0 and seq % GRID_W == 0

    h = jnp.concatenate([ctx, x], axis=1)
    c_all = jnp.concatenate([c, c_ctx[None, :]], axis=0)
    c_all = jnp.pad(c_all, ((0, (-c_all.shape[0]) % 8), (0, 0)))
    mods = _modulation(c_all, mod_w, mod_b)
    mod_lat = mods[:, :bsz]
    mod_ctx = jnp.broadcast_to(mods[:, bsz:bsz + 1], mod_lat.shape)
    mod6 = jnp.stack([mod_ctx, mod_lat], axis=2).reshape(depth, bsz, 2, 6, 1, d)

    bg_all = moe_b_gu[..., 0::2]
    bu_all = moe_b_gu[..., 1::2]

    mw = ev_conv_w.shape[-1] // 2
    n_gates = 4 * MLSTM_HEADS
    s5w = ev_s5_d.shape[-1]
    dk_t = od_gate_w2_f.shape[-1]
    dv_t = od_norm_g.shape[-1]
    y_buf = None
    for layer in range(depth):
        last = layer == depth - 1
        j = layer // 2
        m6 = mod6[layer]
        sh1, sc1, g1, sh2, sc2, g2 = (m6[:, :, i] for i in range(6))
        if layer % 2 == 0:
            w_in, b_in = ev_w_in[j], ev_b_in[j]
            cols = jnp.concatenate([jnp.arange(0, 4 * mw), jnp.arange(4 * mw + n_gates, 4 * mw + n_gates + s5w),
                                    jnp.arange(4 * mw, 4 * mw + n_gates)])
            qk_pre, v, o, u, gates = _nm_matmul(h, norm_mix_g[layer], sh1, sc1, w_in[:, cols], b_in[cols],
                                                (2 * mw, mw, mw, s5w, n_gates))
            dh = mw // MLSTM_HEADS
            colscale = jnp.concatenate([jnp.full((mw,), dh ** -0.5, F32), jnp.ones((mw,), F32)])
            qk = _conv_silu(qk_pre, ev_conv_w[j], ev_conv_b[j], colscale, n_ctx)
            m2 = _mlstm_mixer(qk, v, gates, n_ctx)
            shared = (ev_s5_b_re[j], ev_s5_b_im[j], ev_s5_c_re[j], ev_s5_c_im[j])
            mats_f = _s5_matrices(ev_s5_a_re_f[j], ev_s5_a_im_f[j], ev_s5_log_dt_f[j], *shared, backward=False)
            mats_b = _s5_matrices(ev_s5_a_re_b[j], ev_s5_a_im_b[j], ev_s5_log_dt_b[j], *shared, backward=True)
            s2 = _s5_mixer(u, tuple(jnp.stack([a, b]) for a, b in zip(mats_f, mats_b)), n_ctx)
            h = _even_post(m2, o, s2, u, ev_mlstm_norm_g[j], ev_s5_d[j], ev_s5_glu_w[j], ev_s5_glu_b[j],
                           ev_w_out[j], h, g1)
        else:
            hc = _grid_reorder(h, n_ctx, True)
            qq, kk, vv, gg, rr = _nm_matmul(hc, norm_mix_g[layer], sh1, sc1, od_w_in[j],
                                            jnp.zeros((od_w_in.shape[-1],), F32),
                                            (dk_t, dk_t, dv_t, dv_t, 2 * GLA_RANK))
            zero = jnp.zeros_like(od_gate_w2_f[j])
            w2 = jnp.stack([jnp.concatenate([od_gate_w2_f[j], zero], axis=0),
                            jnp.concatenate([zero, od_gate_w2_b[j]], axis=0)])
            b2 = jnp.stack([od_gate_b2_f[j], od_gate_b2_b[j]])[:, None, :]
            o2 = _gla_mixer(qq, kk, vv, rr, w2, b2, n_ctx)
            h = _grid_reorder(_odd_post(o2, gg, od_norm_g[j], od_w_out[j], hc, g1), n_ctx, False)
        f, top_e, gate = _ffn_prep(h, norm_ffn_g[layer], sh2, sc2, router_w[layer], router_b[layer])
        weights = (layer, moe_w_gu, moe_w_down, bg_all[layer], bu_all[layer], moe_b_down[layer])
        h, y_buf = _moe_layer(h, f, top_e, gate, g2, weights, last, n_ctx, y_buf, final_norm_g if last else None)
    return h
```
